```python
import math
import jax, jax.numpy as jnp
from jax import lax
import numpy as np

D_MODEL = 1024
BATCH = 4
SEQ = 8192
DEPTH = 4

CTX_LEN = 256
GRID_W = 64
ROPE_BASE = 10000.0
EPS = 1e-6

RET_HEADS = 8
RET_DK = 64
RET_DV = 128
RET_CHUNK = 128
ATT_HEADS = 8
ATT_KV_HEADS = 2
ATT_HD = 64
Q_BLOCK = 128
DN_HEADS = 8
DN_DK = 128
DN_DV = 128
DN_CHUNK = 64
DN_CONV = 3
N_GROUPS = 4
EXPERTS_PER_GROUP = 8
N_EXPERTS = N_GROUPS * EXPERTS_PER_GROUP
TOP_K = 2
D_EXPERT = 512

EVEN_SPLITS = (RET_HEADS * RET_DK, RET_HEADS * RET_DK, RET_HEADS * RET_DV, RET_HEADS * RET_DV,
               ATT_HEADS * ATT_HD, ATT_KV_HEADS * ATT_HD, ATT_KV_HEADS * ATT_HD)
EVEN_IN = sum(EVEN_SPLITS)
EVEN_MIX = RET_HEADS * RET_DV + ATT_HEADS * ATT_HD
DN_CONV_CH = 2 * DN_HEADS * DN_DK + DN_HEADS * DN_DV
ODD_SPLITS = (DN_CONV_CH, DN_HEADS * DN_DV, DN_HEADS, DN_HEADS, DN_HEADS, DN_HEADS)
ODD_IN = sum(ODD_SPLITS)

kernel_name = "hybrid_retention_gqa_gdn_hmoe_dit"


def _split(p, sizes):
    idx = np.cumsum(sizes)[:-1].tolist()
    return jnp.split(p, idx, axis=-1)


def _flip(t):
    return jnp.flip(t, axis=2)


def rmsnorm(x, gain):
    xf = x.astype(jnp.float32)
    y = xf * lax.rsqrt(jnp.mean(xf * xf, axis=-1, keepdims=True) + EPS)
    return (y * gain.astype(jnp.float32)).astype(x.dtype)


def l2norm(x):
    xf = x.astype(jnp.float32)
    return xf * lax.rsqrt(jnp.sum(xf * xf, axis=-1, keepdims=True) + EPS)


def modulate(h, shift, scale):
    return h * (1.0 + scale) + shift


def axial_rope(n_tok, head_dim):
    n_rows = n_tok // GRID_W
    rows = jnp.repeat(jnp.arange(n_rows), GRID_W).astype(jnp.float32)
    cols = jnp.tile(jnp.arange(GRID_W), n_rows).astype(jnp.float32)
    nf = head_dim // 4
    inv = ROPE_BASE ** (-jnp.arange(nf, dtype=jnp.float32) / nf)
    ang = jnp.stack([rows[:, None] * inv, cols[:, None] * inv], axis=1)
    return jnp.cos(ang), jnp.sin(ang)


def apply_rope(x, cos, sin):
    B, L, H, hd = x.shape
    nf = hd // 4
    xr = x.astype(jnp.float32).reshape(B, L, H, 2, 2, nf)
    x1, x2 = xr[..., 0, :], xr[..., 1, :]
    c, s = cos[None, :, None], sin[None, :, None]
    out = jnp.stack([x1 * c - x2 * s, x2 * c + x1 * s], axis=-2)
    return out.reshape(B, L, H, hd).astype(x.dtype)


def retention_chunked(q, k, v, log_gamma, s0):
    B, H, L, dk = q.shape
    dv = v.shape[-1]
    C = RET_CHUNK
    n = L // C
    qc = q.reshape(B, H, n, C, dk)
    kc = k.reshape(B, H, n, C, dk)
    vc = v.reshape(B, H, n, C, dv)
    pos = jnp.arange(C, dtype=jnp.float32)
    lg = log_gamma[:, None]
    diff = pos[:, None] - pos[None, :]
    dmask = jnp.where(diff >= 0, jnp.exp(lg[:, :, None] * jnp.maximum(diff, 0.0)), 0.0)
    scores = jnp.einsum('bhnid,bhnjd->bhnij', qc, kc) * dmask[None, :, None]
    intra = jnp.einsum('bhnij,bhnje->bhnie', scores, vc)
    kv = jnp.einsum('bhnjd,bhnje->nbhde', kc * jnp.exp(lg * (C - 1 - pos))[None, :, None, :, None], vc)
    g_chunk = jnp.exp(log_gamma * C)[None, :, None, None]

    def step(S, kv_n):
        return g_chunk * S + kv_n, S

    s_final, s_prev = lax.scan(step, s0, kv)
    inter = jnp.einsum('bhnid,nbhde->bhnie', qc * jnp.exp(lg * (pos + 1.0))[None, :, None, :, None], s_prev)
    return (intra + inter).reshape(B, H, L, dv), s_final


def gated_delta_chunked(q, k, v, g, beta, s0):
    B, H, L, dk = q.shape
    dv = v.shape[-1]
    C = DN_CHUNK
    n = L // C
    qc = (q * dk ** -0.5).reshape(B, H, n, C, dk)
    kc = k.reshape(B, H, n, C, dk)
    vc = v.reshape(B, H, n, C, dv)
    gc = jnp.cumsum(g.reshape(B, H, n, C), axis=-1)
    bc = beta.reshape(B, H, n, C)[..., None]
    incl = jnp.tril(jnp.ones((C, C), dtype=bool))
    strict = jnp.tril(jnp.ones((C, C), dtype=bool), -1)
    gdiff = gc[..., :, None] - gc[..., None, :]
    decay = jnp.where(incl, jnp.exp(jnp.where(incl, gdiff, 0.0)), 0.0)
    kb = kc * bc
    lower = jnp.where(strict, jnp.einsum('bhnid,bhnjd->bhnij', kb, kc) * decay, 0.0)
    a_mat = lower + jnp.eye(C, dtype=jnp.float32)
    rhs = jnp.concatenate([vc * bc, kb * jnp.exp(gc)[..., None]], axis=-1)
    sol = lax.linalg.triangular_solve(a_mat, rhs, left_side=True, lower=True, unit_diagonal=True)
    u, w = sol[..., :dv], sol[..., dv:]
    attn = jnp.where(incl, jnp.einsum('bhnid,bhnjd->bhnij', qc, kc) * decay, 0.0)
    qg = qc * jnp.exp(gc)[..., None]
    kd = kc * jnp.exp(gc[..., -1:] - gc)[..., None]
    gl = jnp.exp(gc[..., -1])[..., None, None]
    xs = tuple(jnp.moveaxis(t, 2, 0) for t in (u, w, qg, attn, kd, gl))

    def step(S, inp):
        u_i, w_i, qg_i, attn_i, kd_i, gl_i = inp
        v_new = u_i - jnp.einsum('bhcd,bhde->bhce', w_i, S)
        o = jnp.einsum('bhcd,bhde->bhce', qg_i, S) + jnp.einsum('bhij,bhje->bhie', attn_i, v_new)
        S = S * gl_i + jnp.einsum('bhcd,bhce->bhde', kd_i, v_new)
        return S, o

    s_final, o = lax.scan(step, s0, xs)
    return jnp.moveaxis(o, 0, 2).reshape(B, H, L, dv), s_final


def gqa_block(qg, k, v):
    s = jnp.einsum('bqgrd,bkgd->bgrqk', qg, k, preferred_element_type=jnp.float32) * ATT_HD ** -0.5
    p = jax.nn.softmax(s, axis=-1).astype(v.dtype)
    return jnp.einsum('bgrqk,bkgd->bqgrd', p, v)


def attend_blocked(q, k, v):
    B, L, H, d = q.shape
    R = H // ATT_KV_HEADS
    nb = L // Q_BLOCK
    qb = jnp.moveaxis(q.reshape(B, nb, Q_BLOCK, ATT_KV_HEADS, R, d), 1, 0)
    o = lax.map(lambda qblk: gqa_block(qblk, k, v), qb)
    return jnp.moveaxis(o, 0, 1).reshape(B, L, H * d)


def retention_readout(o, gate):
    o = o * lax.rsqrt(jnp.mean(o * o, axis=-1, keepdims=True) + EPS)
    B, H, L, dv = o.shape
    o = jnp.swapaxes(o, 1, 2).reshape(B, L, H * dv).astype(gate.dtype)
    return jax.nn.silu(gate) * o


def mix_retention_attention(h_lat, h_ctx, w_in, q_gain, k_gain, decay_f, decay_b, w_out, cos, sin, with_ctx_out):
    B = h_lat.shape[0]

    def project(h, rotate):
        L = h.shape[1]
        rq, rk, rv, rg, aq, ak, av = _split(h @ w_in, EVEN_SPLITS)
        rq = rq.reshape(B, L, RET_HEADS, RET_DK)
        rk = rk.reshape(B, L, RET_HEADS, RET_DK)
        aq = rmsnorm(aq.reshape(B, L, ATT_HEADS, ATT_HD), q_gain)
        ak = rmsnorm(ak.reshape(B, L, ATT_KV_HEADS, ATT_HD), k_gain)
        if rotate:
            rq, rk = apply_rope(rq, cos, sin), apply_rope(rk, cos, sin)
            aq, ak = apply_rope(aq, cos, sin), apply_rope(ak, cos, sin)
        hf = lambda t: jnp.swapaxes(t, 1, 2).astype(jnp.float32)
        ret = (hf(rq), hf(rk) * RET_DK ** -0.5, hf(rv.reshape(B, L, RET_HEADS, RET_DV)))
        return ret, rg, aq, ak, av.reshape(B, L, ATT_KV_HEADS, ATT_HD)

    (lq, lk, lv), l_gate, laq, lak, lav = project(h_lat, True)
    (cq, ck, cv), c_gate, caq, cak, cav = project(h_ctx, False)

    ld_f = -jnp.exp(decay_f.astype(jnp.float32))
    ld_b = -jnp.exp(decay_b.astype(jnp.float32))
    s0 = jnp.zeros((B, RET_HEADS, RET_DK, RET_DV), jnp.float32)
    oc_f, sc_f = retention_chunked(cq, ck, cv, ld_f, s0)
    oc_b, sc_b = retention_chunked(_flip(cq), _flip(ck), _flip(cv), ld_b, s0)
    ol_f, _ = retention_chunked(lq, lk, lv, ld_f, sc_f)
    ol_b, _ = retention_chunked(_flip(lq), _flip(lk), _flip(lv), ld_b, sc_b)
    ret_lat = retention_readout(ol_f + _flip(ol_b), l_gate)

    k_all = jnp.concatenate([lak, cak], axis=1)
    v_all = jnp.concatenate([lav, cav], axis=1)
    att_lat = attend_blocked(laq, k_all, v_all)
    y_lat = jnp.concatenate([ret_lat, att_lat], axis=-1) @ w_out
    if not with_ctx_out:
        return y_lat, None

    Lc = h_ctx.shape[1]
    ret_ctx = retention_readout(oc_f + _flip(oc_b), c_gate)
    R = ATT_HEADS // ATT_KV_HEADS
    att_ctx = gqa_block(caq.reshape(B, Lc, ATT_KV_HEADS, R, ATT_HD), cak, cav).reshape(B, Lc, ATT_HEADS * ATT_HD)
    y_ctx = jnp.concatenate([ret_ctx, att_ctx], axis=-1) @ w_out
    return y_lat, y_ctx


def short_conv(x, w):
    y = lax.conv_general_dilated(x, w[:, None, :], window_strides=(1,),
                                 padding=[(DN_CONV // 2, DN_CONV // 2)],
                                 dimension_numbers=('NWC', 'WIO', 'NWC'),
                                 feature_group_count=x.shape[-1])
    return jax.nn.silu(y)


def decay_and_beta(a, b, a_log, dt_bias):
    g = -jnp.exp(a_log.astype(jnp.float32)) * jax.nn.softplus(a.astype(jnp.float32) + dt_bias.astype(jnp.float32))
    beta = jax.nn.sigmoid(b.astype(jnp.float32))
    return jnp.swapaxes(g, 1, 2), jnp.swapaxes(beta, 1, 2)


def deltanet_readout(o, z, gain):
    B, H, L, dv = o.shape
    o = rmsnorm(jnp.swapaxes(o, 1, 2), gain)
    o = o * jax.nn.silu(z.reshape(B, L, H, dv).astype(jnp.float32))
    return o.reshape(B, L, H * dv).astype(z.dtype)


def mix_gated_deltanet(h_lat, h_ctx, w_in, conv_w, a_log_f, a_log_b, dt_bias_f, dt_bias_b, out_gain, w_out, with_ctx_out):
    B = h_lat.shape[0]

    def project(h):
        L = h.shape[1]
        qkv, z, a_f, a_b, b_f, b_b = _split(h @ w_in, ODD_SPLITS)
        qkv = short_conv(qkv, conv_w)
        q, k, v = _split(qkv, (DN_HEADS * DN_DK, DN_HEADS * DN_DK, DN_HEADS * DN_DV))
        hf = lambda t: jnp.swapaxes(t, 1, 2)
        q = hf(l2norm(q.reshape(B, L, DN_HEADS, DN_DK)))
        k = hf(l2norm(k.reshape(B, L, DN_HEADS, DN_DK)))
        v = hf(v.reshape(B, L, DN_HEADS, DN_DV).astype(jnp.float32))
        return q, k, v, decay_and_beta(a_f, b_f, a_log_f, dt_bias_f), decay_and_beta(a_b, b_b, a_log_b, dt_bias_b), z

    lq, lk, lv, (lgf, lbf), (lgb, lbb), lz = project(h_lat)
    cq, ck, cv, (cgf, cbf), (cgb, cbb), cz = project(h_ctx)
    s0 = jnp.zeros((B, DN_HEADS, DN_DK, DN_DV), jnp.float32)
    oc_f, sc_f = gated_delta_chunked(cq, ck, cv, cgf, cbf, s0)
    oc_b, sc_b = gated_delta_chunked(_flip(cq), _flip(ck), _flip(cv), _flip(cgb), _flip(cbb), s0)
    ol_f, _ = gated_delta_chunked(lq, lk, lv, lgf, lbf, sc_f)
    ol_b, _ = gated_delta_chunked(_flip(lq), _flip(lk), _flip(lv), _flip(lgb), _flip(lbb), sc_b)
    y_lat = deltanet_readout(ol_f + _flip(ol_b), lz, out_gain) @ w_out
    if not with_ctx_out:
        return y_lat, None
    y_ctx = deltanet_readout(oc_f + _flip(oc_b), cz, out_gain) @ w_out
    return y_lat, y_ctx


def hier_moe(h, w_group, b_group, w_expert, b_expert, w_gate_up, w_down):
    T = h.shape[0]
    rows = jnp.arange(T)
    g_logits = jnp.dot(h, w_group, preferred_element_type=jnp.float32) + b_group.astype(jnp.float32)
    g_prob = jax.nn.softmax(g_logits, axis=-1)
    grp = jnp.argmax(g_logits, axis=-1)
    p_grp = g_prob[rows, grp][:, None]
    e_logits = (jnp.dot(h, w_expert, preferred_element_type=jnp.float32) + b_expert.astype(jnp.float32)).reshape(T, N_GROUPS, EXPERTS_PER_GROUP)
    e_sel = e_logits[rows, grp]
    top_val, top_idx = lax.top_k(e_sel, TOP_K)
    weights = jax.nn.softmax(top_val, axis=-1) * p_grp
    expert = grp[:, None] * EXPERTS_PER_GROUP + top_idx
    flat_e = expert.reshape(-1)
    flat_w = weights.reshape(-1)
    flat_tok = jnp.repeat(rows, TOP_K)
    order = jnp.argsort(flat_e)
    tok_s = flat_tok[order]
    w_s = flat_w[order]
    sizes = jnp.bincount(flat_e, length=N_EXPERTS).astype(jnp.int32)
    xs = h[tok_s]
    gate, up = jnp.split(lax.ragged_dot(xs, w_gate_up, sizes), 2, axis=-1)
    y = lax.ragged_dot(jax.nn.silu(gate) * up, w_down, sizes)
    y = y * w_s[:, None].astype(y.dtype)
    return jnp.zeros_like(h).at[tok_s].add(y)


def setup_inputs(seed: int = 0) -> dict:
    key = jax.random.key(seed)
    ks = list(jax.random.split(key, 40))
    nxt = ks.pop
    f32 = jnp.float32
    D = D_MODEL
    n_even = (DEPTH + 1) // 2
    n_odd = DEPTH // 2

    def w(shape, fan_in, scale=1.0):
        return jax.random.normal(nxt(), shape, f32) * (scale * fan_in ** -0.5)

    def gain(shape):
        return 1.0 + 0.05 * jax.random.normal(nxt(), shape, f32)

    def small(shape, s):
        return s * jax.random.normal(nxt(), shape, f32)

    ret_base = jnp.log(-jnp.log1p(-(2.0 ** (-5.0 - jnp.arange(RET_HEADS, dtype=f32)))))

    def dt_bias(shape):
        dt = jnp.exp(jax.random.uniform(nxt(), shape, f32, math.log(1e-3), math.log(1e-1)))
        return dt + jnp.log(-jnp.expm1(-dt))

    def a_log(shape):
        return jnp.log(jax.random.uniform(nxt(), shape, f32, 1.0, 16.0))

    inputs = {
        "x": jax.random.normal(nxt(), (BATCH, SEQ, D), f32),
        "c": jax.random.normal(nxt(), (BATCH, D), f32),
        "ctx": jax.random.normal(nxt(), (BATCH, CTX_LEN, D), f32),
        "c_ctx": jax.random.normal(nxt(), (D,), f32),
        "w_ada": w((DEPTH, D, 6 * D), D, 0.5),
        "b_ada": small((DEPTH, 6 * D), 0.02),
        "norm_mix": gain((DEPTH, D)),
        "norm_ffn": gain((DEPTH, D)),
        "ev_w_in": w((n_even, D, EVEN_IN), D),
        "ev_q_gain": gain((n_even, ATT_HD)),
        "ev_k_gain": gain((n_even, ATT_HD)),
        "ev_decay_f": ret_base + small((n_even, RET_HEADS), 0.05),
        "ev_decay_b": ret_base + small((n_even, RET_HEADS), 0.05),
        "ev_w_out": w((n_even, EVEN_MIX, D), EVEN_MIX),
        "od_w_in": jnp.concatenate([w((n_odd, D, DN_CONV_CH + DN_HEADS * DN_DV), D),
                                    w((n_odd, D, 4 * DN_HEADS), D, 0.1)], axis=-1),
        "od_conv": w((n_odd, DN_CONV, DN_CONV_CH), DN_CONV),
        "od_a_log_f": a_log((n_odd, DN_HEADS)),
        "od_a_log_b": a_log((n_odd, DN_HEADS)),
        "od_dt_bias_f": dt_bias((n_odd, DN_HEADS)),
        "od_dt_bias_b": dt_bias((n_odd, DN_HEADS)),
        "od_out_gain": gain((n_odd, DN_DV)),
        "od_w_out": w((n_odd, DN_HEADS * DN_DV, D), DN_HEADS * DN_DV),
        "moe_w_group": w((DEPTH, D, N_GROUPS), D),
        "moe_b_group": small((DEPTH, N_GROUPS), 0.01),
        "moe_w_expert": w((DEPTH, D, N_EXPERTS), D),
        "moe_b_expert": small((DEPTH, N_EXPERTS), 0.01),
        "moe_w_gate_up": w((DEPTH, N_EXPERTS, D, 2 * D_EXPERT), D),
        "moe_w_down": w((DEPTH, N_EXPERTS, D_EXPERT, D), D_EXPERT),
        "final_norm": gain((D,)),
    }
    return inputs


def reference(x, c, ctx, c_ctx, w_ada, b_ada, norm_mix, norm_ffn,
              ev_w_in, ev_q_gain, ev_k_gain, ev_decay_f, ev_decay_b, ev_w_out,
              od_w_in, od_conv, od_a_log_f, od_a_log_b, od_dt_bias_f, od_dt_bias_b, od_out_gain, od_w_out,
              moe_w_group, moe_b_group, moe_w_expert, moe_b_expert, moe_w_gate_up, moe_w_down,
              final_norm):
    B, L, D = x.shape
    Lc = ctx.shape[1]
    cos, sin = axial_rope(L, ATT_HD)
    silu_c = jax.nn.silu(c)
    silu_cc = jax.nn.silu(c_ctx)
    for layer in range(DEPTH):
        last = layer == DEPTH - 1
        mod_l = (silu_c @ w_ada[layer] + b_ada[layer])[:, None, :]
        mod_c = silu_cc @ w_ada[layer] + b_ada[layer]
        sh1, sc1, g1, sh2, sc2, g2 = jnp.split(mod_l, 6, axis=-1)
        csh1, csc1, cg1, csh2, csc2, cg2 = jnp.split(mod_c, 6, axis=-1)
        h_lat = modulate(rmsnorm(x, norm_mix[layer]), sh1, sc1)
        h_ctx = modulate(rmsnorm(ctx, norm_mix[layer]), csh1, csc1)
        i = layer // 2
        if layer % 2 == 0:
            y_lat, y_ctx = mix_retention_attention(h_lat, h_ctx, ev_w_in[i], ev_q_gain[i], ev_k_gain[i],
                                                   ev_decay_f[i], ev_decay_b[i], ev_w_out[i], cos, sin, not last)
        else:
            y_lat, y_ctx = mix_gated_deltanet(h_lat, h_ctx, od_w_in[i], od_conv[i], od_a_log_f[i], od_a_log_b[i],
                                              od_dt_bias_f[i], od_dt_bias_b[i], od_out_gain[i], od_w_out[i], not last)
        x = x + g1 * y_lat
        f_lat = modulate(rmsnorm(x, norm_ffn[layer]), sh2, sc2).reshape(B * L, D)
        moe_args = (moe_w_group[layer], moe_b_group[layer], moe_w_expert[layer], moe_b_expert[layer],
                    moe_w_gate_up[layer], moe_w_down[layer])
        if last:
            y = hier_moe(f_lat, *moe_args)
            x = x + g2 * y.reshape(B, L, D)
        else:
            ctx = ctx + cg1 * y_ctx
            f_ctx = modulate(rmsnorm(ctx, norm_ffn[layer]), csh2, csc2).reshape(B * Lc, D)
            y = hier_moe(jnp.concatenate([f_lat, f_ctx], axis=0), *moe_args)
            x = x + g2 * y[:B * L].reshape(B, L, D)
            ctx = ctx + cg2 * y[B * L:].reshape(B, Lc, D)
    return rmsnorm(x, final_norm)
```

```python
import functools
import math

import numpy as np
import jax
import jax.numpy as jnp
from jax import lax
from jax.experimental import pallas as pl
from jax.experimental.pallas import tpu as pltpu

F32 = jnp.float32
BF16 = jnp.bfloat16
HIGHEST = lax.Precision.HIGHEST

EPS = 1e-6
GRID_W = 64
ROPE_BASE = 10000.0
RET_HEADS, RET_DK, RET_DV, RET_CHUNK = 8, 64, 128, 128
ATT_HEADS, ATT_KV_HEADS, ATT_HD = 8, 2, 64
DN_HEADS, DN_DK, DN_DV, DN_CHUNK, DN_CONV = 8, 128, 128, 64, 3
N_GROUPS, EXPERTS_PER_GROUP, TOP_K = 4, 8, 2
N_EXPERTS = N_GROUPS * EXPERTS_PER_GROUP

EVEN_IN = 2 * RET_HEADS * RET_DK + 2 * RET_HEADS * RET_DV + (ATT_HEADS + 2 * ATT_KV_HEADS) * ATT_HD
EVEN_ATT_COL = 2 * RET_HEADS * RET_DK + 2 * RET_HEADS * RET_DV
EVEN_ATT_W = (ATT_HEADS + 2 * ATT_KV_HEADS) * ATT_HD
DN_QKV = 2 * DN_HEADS * DN_DK + DN_HEADS * DN_DV
ODD_IN = DN_QKV + DN_HEADS * DN_DV + 4 * DN_HEADS
ODD_IN_PAD = ((ODD_IN + 127) // 128) * 128

LANES = 128
VMEM_LIMIT = 56 * 1024 * 1024

NT_DIMS = (((1,), (1,)), ((), ()))
TN_DIMS = (((0,), (0,)), ((), ()))


def _cparams(sem):
    return pltpu.CompilerParams(dimension_semantics=sem, vmem_limit_bytes=VMEM_LIMIT)


def _silu(x):
    return x / (1.0 + jnp.exp(-x))


def _dot(a, b):
    return jnp.dot(a, b, preferred_element_type=F32)


def _adaln_kernel(c_ref, w_ref, b_ref, o_ref):
    s = _silu(c_ref[...])
    o_ref[...] = _dot(s.astype(BF16), w_ref[...].astype(BF16)) + b_ref[...]


def adaln(c8, w_ada, b_ada):
    depth, d, n6 = w_ada.shape
    tn = min(n6, 1536)
    return pl.pallas_call(
        _adaln_kernel,
        grid=(depth, n6 // tn),
        in_specs=[
            pl.BlockSpec((8, d), lambda l, j: (0, 0)),
            pl.BlockSpec((None, d, tn), lambda l, j: (l, 0, j)),
            pl.BlockSpec((None, 1, tn), lambda l, j: (l, 0, j)),
        ],
        out_specs=pl.BlockSpec((None, 8, tn), lambda l, j: (l, 0, j)),
        out_shape=jax.ShapeDtypeStruct((depth, 8, n6), F32),
        compiler_params=_cparams(("arbitrary", "arbitrary")),
    )(c8, w_ada, b_ada.reshape(depth, 1, n6))


def _norm_mod(x, gain, shift, scale):
    ms = jnp.mean(x * x, axis=-1, keepdims=True)
    h = x * lax.rsqrt(ms + EPS) * gain
    return h * (1.0 + scale) + shift


def _nmm_kernel(x_ref, g_ref, sh_ref, sc_ref, w_ref, o_ref, *, nchunk):
    hb = _norm_mod(x_ref[...], g_ref[...], sh_ref[...], sc_ref[...]).astype(BF16)
    n = o_ref.shape[-1]
    for n0 in range(0, n, nchunk):
        o_ref[:, n0:n0 + nchunk] = _dot(hb, w_ref[:, n0:n0 + nchunk]).astype(o_ref.dtype)


def _mod_row_map(tm, seq, n_lat_batches):
    return lambda i: (jnp.minimum((i * tm) // seq, n_lat_batches), 0, 0)


def norm_mod_matmul(xa, gain, shift, scale, w, tm, seq, nb):
    ta, d = xa.shape
    n = w.shape[1]
    nchunk = 512 if n % 512 == 0 else 384
    mrow = _mod_row_map(tm, seq, nb)
    return pl.pallas_call(
        functools.partial(_nmm_kernel, nchunk=nchunk),
        grid=(ta // tm,),
        in_specs=[
            pl.BlockSpec((tm, d), lambda i: (i, 0)),
            pl.BlockSpec((1, d), lambda i: (0, 0)),
            pl.BlockSpec((None, 1, d), mrow),
            pl.BlockSpec((None, 1, d), mrow),
            pl.BlockSpec((d, n), lambda i: (0, 0)),
        ],
        out_specs=pl.BlockSpec((tm, n), lambda i: (i, 0)),
        out_shape=jax.ShapeDtypeStruct((ta, n), BF16),
        compiler_params=_cparams(("arbitrary",)),
    )(xa, gain.reshape(1, d), shift, scale, w)


def _final_norm_kernel(x_ref, g_ref, o_ref):
    x = x_ref[...]
    ms = jnp.mean(x * x, axis=-1, keepdims=True)
    o_ref[...] = x * lax.rsqrt(ms + EPS) * g_ref[...]


def final_norm(xa, gain, t_rows, tm):
    d = xa.shape[1]
    return pl.pallas_call(
        _final_norm_kernel,
        grid=(t_rows // tm,),
        in_specs=[pl.BlockSpec((tm, d), lambda i: (i, 0)), pl.BlockSpec((1, d), lambda i: (0, 0))],
        out_specs=pl.BlockSpec((tm, d), lambda i: (i, 0)),
        out_shape=jax.ShapeDtypeStruct((t_rows, d), F32),
        compiler_params=_cparams(("arbitrary",)),
    )(xa, gain.reshape(1, d))


def _prep_even_kernel(qk_ref, att_ref, cos_ref, s1_ref, s2_ref, qg_ref, kg_ref, bd_ref,
                      rq_ref, rk_ref, aq_ref, ak_ref, av_ref):
    cos = cos_ref[...]
    s1 = s1_ref[...]
    s2 = s2_ref[...]
    bd = bd_ref[...]
    half = ATT_HD

    def rope(x):
        return x * cos + pltpu.roll(x, LANES - 16, 1) * s1 + pltpu.roll(x, 16, 1) * s2

    def head_norm(x, gain):
        ms = jnp.dot(x * x, bd, preferred_element_type=F32, precision=HIGHEST)
        return x * lax.rsqrt(ms + EPS) * gain

    nq = RET_HEADS * RET_DK // LANES
    for j in range(nq):
        cs = slice(j * LANES, (j + 1) * LANES)
        rq_ref[:, cs] = rope(qk_ref[:, cs].astype(F32)).astype(BF16)
        ks = slice(nq * LANES + j * LANES, nq * LANES + (j + 1) * LANES)
        rk_ref[:, cs] = (rope(qk_ref[:, ks].astype(F32)) * RET_DK ** -0.5).astype(BF16)

    qg = qg_ref[...]
    kg = kg_ref[...]
    for j in range(ATT_HEADS * ATT_HD // LANES):
        x = att_ref[:, j * LANES:(j + 1) * LANES].astype(F32)
        y = (rope(head_norm(x, qg)) * ATT_HD ** -0.5).astype(BF16)
        aq_ref[2 * j] = y[:, :half]
        aq_ref[2 * j + 1] = y[:, half:]
    c0 = ATT_HEADS * ATT_HD
    y = rope(head_norm(att_ref[:, c0:c0 + LANES].astype(F32), kg)).astype(BF16)
    ak_ref[0] = y[:, :half]
    ak_ref[1] = y[:, half:]
    v = att_ref[:, c0 + LANES:c0 + 2 * LANES]
    av_ref[0] = v[:, :half]
    av_ref[1] = v[:, half:]


def prep_even(p, tabs, q_gain, k_gain, tm, seq, t_lat):
    ta = p.shape[0]
    cos_t, s1_t, s2_t = tabs
    n_tab = seq // tm

    def tab_map(i):
        r = i * tm
        return (jnp.where(r < t_lat, (r % seq) // tm, n_tab), 0)

    ii = np.arange(LANES)
    bd = jnp.asarray((ii[:, None] // ATT_HD == ii[None, :] // ATT_HD).astype(np.float32) / ATT_HD)
    qg = jnp.tile(q_gain.astype(F32), LANES // ATT_HD).reshape(1, LANES)
    kg = jnp.tile(k_gain.astype(F32), LANES // ATT_HD).reshape(1, LANES)
    qkw = 2 * RET_HEADS * RET_DK
    tab_spec = pl.BlockSpec((tm, LANES), tab_map)
    one = lambda i: (0, 0)
    return pl.pallas_call(
        _prep_even_kernel,
        grid=(ta // tm,),
        in_specs=[
            pl.BlockSpec((tm, qkw), lambda i: (i, 0)),
            pl.BlockSpec((tm, EVEN_ATT_W), lambda i: (i, EVEN_ATT_COL // EVEN_ATT_W)),
            tab_spec, tab_spec, tab_spec,
            pl.BlockSpec((1, LANES), one), pl.BlockSpec((1, LANES), one),
            pl.BlockSpec((LANES, LANES), one),
        ],
        out_specs=[
            pl.BlockSpec((tm, RET_HEADS * RET_DK), lambda i: (i, 0)),
            pl.BlockSpec((tm, RET_HEADS * RET_DK), lambda i: (i, 0)),
            pl.BlockSpec((ATT_HEADS, tm, ATT_HD), lambda i: (0, i, 0)),
            pl.BlockSpec((ATT_KV_HEADS, tm, ATT_HD), lambda i: (0, i, 0)),
            pl.BlockSpec((ATT_KV_HEADS, tm, ATT_HD), lambda i: (0, i, 0)),
        ],
        out_shape=[
            jax.ShapeDtypeStruct((ta, RET_HEADS * RET_DK), BF16),
            jax.ShapeDtypeStruct((ta, RET_HEADS * RET_DK), BF16),
            jax.ShapeDtypeStruct((ATT_HEADS, ta, ATT_HD), BF16),
            jax.ShapeDtypeStruct((ATT_KV_HEADS, ta, ATT_HD), BF16),
            jax.ShapeDtypeStruct((ATT_KV_HEADS, ta, ATT_HD), BF16),
        ],
        compiler_params=_cparams(("arbitrary",)),
    )(p, p, cos_t, s1_t, s2_t, qg, kg, bd)


def rope_tables(seq, tm):
    nf = ATT_HD // 4
    t = jnp.arange(seq)
    rows = (t // GRID_W).astype(F32)
    cols = (t % GRID_W).astype(F32)
    inv = ROPE_BASE ** (-jnp.arange(nf, dtype=F32) / nf)
    lane = np.arange(LANES)
    axis = (lane % ATT_HD) // (ATT_HD // 2)
    f = lane % nf
    upper = ((lane % (ATT_HD // 2)) >= nf)
    pos = jnp.where(jnp.asarray(axis)[None, :] == 0, rows[:, None], cols[:, None])
    ang = pos * inv[jnp.asarray(f)][None, :]
    cos = jnp.cos(ang)
    sin = jnp.sin(ang)
    s1 = jnp.where(jnp.asarray(upper)[None, :], 0.0, -sin)
    s2 = jnp.where(jnp.asarray(upper)[None, :], sin, 0.0)
    pad1 = jnp.ones((tm, LANES), F32)
    pad0 = jnp.zeros((tm, LANES), F32)
    return (jnp.concatenate([cos, pad1]), jnp.concatenate([s1, pad0]), jnp.concatenate([s2, pad0]))


def _retention_kernel(dec_ref, q_ref, k_ref, v_ref, g_ref, s0f_ref, s0b_ref,
                      o_ref, sff_ref, sfb_ref, sprev_ref, *, n_chunks):
    hp = pl.program_id(1)
    C = RET_CHUNK
    dk, dv = RET_DK, RET_DV
    pos = lax.broadcasted_iota(jnp.int32, (C, dk), 0).astype(F32)
    ii = lax.broadcasted_iota(jnp.int32, (C, C), 0)
    jj = lax.broadcasted_iota(jnp.int32, (C, C), 1)
    dpos = (ii - jj).astype(F32)
    for hh in range(2):
        h = 2 * hp + hh
        df = dec_ref[0, h]
        db = dec_ref[1, h]
        lf = -jnp.exp(jnp.full((C, C), df, F32))
        lb = -jnp.exp(jnp.full((C, C), db, F32))
        lfk = -jnp.exp(jnp.full((C, dk), df, F32))
        lbk = -jnp.exp(jnp.full((C, dk), db, F32))
        f_in = jnp.exp(lfk * (pos + 1.0))
        f_out = jnp.exp(lfk * (C - 1.0 - pos))
        b_in = jnp.exp(lbk * (C - pos))
        b_out = jnp.exp(lbk * pos)
        gcf = jnp.exp(-jnp.exp(jnp.full((dk, dv), df, F32)) * C)
        gcb = jnp.exp(-jnp.exp(jnp.full((dk, dv), db, F32)) * C)
        mask = jnp.where(dpos > 0, jnp.exp(lf * jnp.maximum(dpos, 0.0)),
                         jnp.where(dpos < 0, jnp.exp(lb * jnp.maximum(-dpos, 0.0)), 2.0))
        qs = slice(hh * dk, (hh + 1) * dk)
        vs = slice(hh * dv, (hh + 1) * dv)

        def fwd_body(n, s, qs=qs, vs=vs, f_out=f_out, gcf=gcf, hh=hh):
            sprev_ref[hh, n] = s
            r0 = pl.multiple_of(n * C, C)
            k = k_ref[pl.ds(r0, C), qs].astype(F32)
            v = v_ref[pl.ds(r0, C), vs]
            kv = lax.dot_general((k * f_out).astype(BF16), v, TN_DIMS, preferred_element_type=F32)
            return gcf * s + kv

        sff_ref[hh] = lax.fori_loop(0, n_chunks, fwd_body, s0f_ref[hh])

        def bwd_body(i, s, qs=qs, vs=vs, f_in=f_in, b_in=b_in, b_out=b_out, gcb=gcb, mask=mask, hh=hh):
            n = n_chunks - 1 - i
            r0 = pl.multiple_of(n * C, C)
            qb = q_ref[pl.ds(r0, C), qs]
            kb = k_ref[pl.ds(r0, C), qs]
            v = v_ref[pl.ds(r0, C), vs]
            q = qb.astype(F32)
            sc = lax.dot_general(qb, kb, NT_DIMS, preferred_element_type=F32) * mask
            o = _dot(sc.astype(BF16), v)
            o = o + _dot((q * f_in).astype(BF16), sprev_ref[hh, n].astype(BF16))
            o = o + _dot((q * b_in).astype(BF16), s.astype(BF16))
            o = o * lax.rsqrt(jnp.mean(o * o, axis=-1, keepdims=True) + EPS)
            gate = g_ref[pl.ds(r0, C), vs].astype(F32)
            o_ref[pl.ds(r0, C), vs] = (_silu(gate) * o).astype(o_ref.dtype)
            kv = lax.dot_general((kb.astype(F32) * b_out).astype(BF16), v, TN_DIMS, preferred_element_type=F32)
            return gcb * s + kv

        sfb_ref[hh] = lax.fori_loop(0, n_chunks, bwd_body, s0b_ref[hh])


def retention(dec, rq, rk, p, s0f, s0b, nb, seq, row_off_blocks):
    n_chunks = seq // RET_CHUNK
    hp_n = RET_HEADS // 2
    vcol = RET_HEADS * RET_DK * 2 // (2 * RET_DV)
    gcol = vcol + RET_HEADS * RET_DV // (2 * RET_DV)
    ta = rq.shape[0]
    st_spec = pl.BlockSpec((None, 2, RET_DK, RET_DV), lambda b, hp, *_: (b, hp, 0, 0))
    grid_spec = pltpu.PrefetchScalarGridSpec(
        num_scalar_prefetch=1,
        grid=(nb, hp_n),
        in_specs=[
            pl.BlockSpec((seq, 2 * RET_DK), lambda b, hp, *_: (row_off_blocks + b, hp)),
            pl.BlockSpec((seq, 2 * RET_DK), lambda b, hp, *_: (row_off_blocks + b, hp)),
            pl.BlockSpec((seq, 2 * RET_DV), lambda b, hp, *_: (row_off_blocks + b, vcol + hp)),
            pl.BlockSpec((seq, 2 * RET_DV), lambda b, hp, *_: (row_off_blocks + b, gcol + hp)),
            st_spec, st_spec,
        ],
        out_specs=[
            pl.BlockSpec((seq, 2 * RET_DV), lambda b, hp, *_: (b, hp)),
            st_spec, st_spec,
        ],
        scratch_shapes=[pltpu.VMEM((2, n_chunks, RET_DK, RET_DV), F32)],
    )
    st_shape = jax.ShapeDtypeStruct((nb, RET_HEADS, RET_DK, RET_DV), F32)
    return pl.pallas_call(
        functools.partial(_retention_kernel, n_chunks=n_chunks),
        grid_spec=grid_spec,
        out_shape=[jax.ShapeDtypeStruct((nb * seq, RET_HEADS * RET_DV), BF16), st_shape, st_shape],
        compiler_params=_cparams(("arbitrary", "arbitrary")),
    )(dec, rq, rk, p, p, s0f, s0b)


def _attn_kernel(*refs, tk, n_lat, n_ctx):
    if n_lat:
        q_ref, kl_ref, vl_ref, kc_ref, vc_ref, o_ref = refs
    else:
        q_ref, kc_ref, vc_ref, o_ref = refs
        kl_ref = vl_ref = None
    tq = q_ref.shape[1]
    outs = []
    for hh in range(2):
        q = q_ref[hh]

        def make_body(k_ref, v_ref, q=q):
            def body(j, carry):
                m, l, acc = carry
                c0 = pl.multiple_of(j * tk, tk)
                k = k_ref[pl.ds(c0, tk), :]
                v = v_ref[pl.ds(c0, tk), :]
                s = lax.dot_general(q, k, NT_DIMS, preferred_element_type=F32)
                m_new = jnp.maximum(m, jnp.max(s, axis=-1, keepdims=True))
                a = jnp.exp(m - m_new)
                p = jnp.exp(s - m_new)
                l = a * l + jnp.sum(p, axis=-1, keepdims=True)
                acc = a * acc + _dot(p.astype(BF16), v)
                return m_new, l, acc
            return body

        carry = (jnp.full((tq, 1), -1e30, F32), jnp.zeros((tq, 1), F32), jnp.zeros((tq, ATT_HD), F32))
        if n_lat:
            carry = lax.fori_loop(0, n_lat, make_body(kl_ref, vl_ref), carry)
        carry = lax.fori_loop(0, n_ctx, make_body(kc_ref, vc_ref), carry)
        _, l, acc = carry
        outs.append(acc / l)
    o_ref[...] = jnp.concatenate(outs, axis=-1).astype(o_ref.dtype)


def attention(aq, ak, av, nb, seq_q, q_off_blocks, lat_seq, ctx_seq, t_lat, tq, tk):
    rep = ATT_HEADS // ATT_KV_HEADS
    nq = seq_q // tq
    n_lat = lat_seq // tk if lat_seq else 0
    n_ctx = ctx_seq // tk
    ctx_off = t_lat // ctx_seq
    q_spec = pl.BlockSpec((2, tq, ATT_HD), lambda b, hp, i: (hp, q_off_blocks + b * nq + i, 0))
    kc_spec = pl.BlockSpec((None, ctx_seq, ATT_HD), lambda b, hp, i: ((2 * hp) // rep, ctx_off + b, 0))
    in_specs = [q_spec]
    args = [aq]
    if n_lat:
        kl_spec = pl.BlockSpec((None, lat_seq, ATT_HD), lambda b, hp, i: ((2 * hp) // rep, b, 0))
        in_specs += [kl_spec, kl_spec]
        args += [ak, av]
    in_specs += [kc_spec, kc_spec]
    args += [ak, av]
    return pl.pallas_call(
        functools.partial(_attn_kernel, tk=tk, n_lat=n_lat, n_ctx=n_ctx),
        grid=(nb, ATT_HEADS // 2, nq),
        in_specs=in_specs,
        out_specs=pl.BlockSpec((tq, 2 * ATT_HD), lambda b, hp, i: (b * nq + i, hp)),
        out_shape=jax.ShapeDtypeStruct((nb * seq_q, ATT_HEADS * ATT_HD), BF16),
        compiler_params=_cparams(("arbitrary", "arbitrary", "arbitrary")),
    )(*args)


def _outproj_even_kernel(a1_ref, a2_ref, w1_ref, w2_ref, res_ref, gate_ref, o_ref):
    y = _dot(a1_ref[...], w1_ref[...]) + _dot(a2_ref[...], w2_ref[...])
    o_ref[...] = res_ref[...] + gate_ref[...] * y


def outproj_even(o_ret, o_att, w1, w2, xa, gate, tm, seq, nb):
    ta, d = xa.shape
    k1, k2 = w1.shape[0], w2.shape[0]
    return pl.pallas_call(
        _outproj_even_kernel,
        grid=(ta // tm,),
        in_specs=[
            pl.BlockSpec((tm, k1), lambda i: (i, 0)),
            pl.BlockSpec((tm, k2), lambda i: (i, 0)),
            pl.BlockSpec((k1, d), lambda i: (0, 0)),
            pl.BlockSpec((k2, d), lambda i: (0, 0)),
            pl.BlockSpec((tm, d), lambda i: (i, 0)),
            pl.BlockSpec((None, 1, d), _mod_row_map(tm, seq, nb)),
        ],
        out_specs=pl.BlockSpec((tm, d), lambda i: (i, 0)),
        out_shape=jax.ShapeDtypeStruct((ta, d), F32),
        compiler_params=_cparams(("arbitrary",)),
    )(o_ret, o_att, w1, w2, xa, gate)


def _outproj_odd_kernel(of_ref, ob_ref, z_ref, og_ref, w_ref, res_ref, gate_ref, o_ref):
    og = og_ref[...]
    parts = []
    for h in range(DN_HEADS):
        cs = slice(h * DN_DV, (h + 1) * DN_DV)
        o = of_ref[:, cs].astype(F32) + ob_ref[:, cs].astype(F32)
        o = o * lax.rsqrt(jnp.mean(o * o, axis=-1, keepdims=True) + EPS) * og
        parts.append((o * _silu(z_ref[:, cs].astype(F32))).astype(BF16))
    a = jnp.concatenate(parts, axis=-1)
    o_ref[...] = res_ref[...] + gate_ref[...] * _dot(a, w_ref[...])


def outproj_odd(o_f, o_b, p, out_gain, w, xa, gate, tm, seq, nb):
    ta, d = xa.shape
    kdim = DN_HEADS * DN_DV
    return pl.pallas_call(
        _outproj_odd_kernel,
        grid=(ta // tm,),
        in_specs=[
            pl.BlockSpec((tm, kdim), lambda i: (i, 0)),
            pl.BlockSpec((tm, kdim), lambda i: (i, 0)),
            pl.BlockSpec((tm, kdim), lambda i: (i, DN_QKV // kdim)),
            pl.BlockSpec((1, DN_DV), lambda i: (0, 0)),
            pl.BlockSpec((kdim, d), lambda i: (0, 0)),
            pl.BlockSpec((tm, d), lambda i: (i, 0)),
            pl.BlockSpec((None, 1, d), _mod_row_map(tm, seq, nb)),
        ],
        out_specs=pl.BlockSpec((tm, d), lambda i: (i, 0)),
        out_shape=jax.ShapeDtypeStruct((ta, d), F32),
        compiler_params=_cparams(("arbitrary",)),
    )(o_f, o_b, p, out_gain.reshape(1, DN_DV).astype(F32), w, xa, gate)


def _prep_odd_kernel(first_ref, last_ref, x_ref, prev_ref, next_ref, ab_ref, cw_ref, arow_ref, brow_ref,
                     q_ref, k_ref, v_ref, gb_ref):
    i = pl.program_id(0)
    tm = x_ref.shape[0]
    hrows = prev_ref.shape[0]
    keep_prev = 1.0 - first_ref[i].astype(F32)
    keep_next = 1.0 - last_ref[i].astype(F32)
    row = lax.broadcasted_iota(jnp.int32, (tm, LANES), 0)
    is_first = row == 0
    is_last = row == tm - 1
    n_qk = 2 * DN_HEADS * DN_DK // LANES
    n_q = DN_HEADS * DN_DK // LANES
    outs = (q_ref, k_ref, v_ref)
    for j in range(DN_QKV // LANES):
        cs = slice(j * LANES, (j + 1) * LANES)
        x = x_ref[:, cs].astype(F32)
        xp = prev_ref[:, cs].astype(F32)[hrows - 1:hrows, :] * keep_prev
        xn = next_ref[:, cs].astype(F32)[0:1, :] * keep_next
        x_dn = jnp.where(is_first, xp, pltpu.roll(x, 1, 0))
        x_up = jnp.where(is_last, xn, pltpu.roll(x, tm - 1, 0))
        w = cw_ref[:, cs]
        y = _silu(x_dn * w[0:1, :] + x * w[1:2, :] + x_up * w[2:3, :])
        if j < n_qk:
            y = y * lax.rsqrt(jnp.sum(y * y, axis=-1, keepdims=True) + EPS)
            if j < n_q:
                y = y * DN_DK ** -0.5
        lj = j % n_q
        outs[j // n_q][:, lj * LANES:(lj + 1) * LANES] = y.astype(BF16)

    a = ab_ref[...].astype(F32)
    lane = lax.broadcasted_iota(jnp.int32, (tm, LANES), 1)
    z = a + brow_ref[...]
    softplus = jnp.maximum(z, 0.0) + jnp.log(1.0 + jnp.exp(-jnp.abs(z)))
    g = -jnp.exp(arow_ref[...]) * softplus
    beta = 1.0 / (1.0 + jnp.exp(-a))
    gb_ref[...] = jnp.where(lane < 2 * DN_HEADS, g, jnp.where(lane < 4 * DN_HEADS, beta, 0.0))


def prep_odd(p, conv_w, arow, brow, first_flags, last_flags, tm):
    ta = p.shape[0]
    halo = 16
    hb = tm // halo
    n_h = ta // halo
    kdim = DN_HEADS * DN_DK
    grid_spec = pltpu.PrefetchScalarGridSpec(
        num_scalar_prefetch=2,
        grid=(ta // tm,),
        in_specs=[
            pl.BlockSpec((tm, DN_QKV), lambda i, *_: (i, 0)),
            pl.BlockSpec((halo, DN_QKV), lambda i, *_: (jnp.maximum(i * hb - 1, 0), 0)),
            pl.BlockSpec((halo, DN_QKV), lambda i, *_: (jnp.minimum((i + 1) * hb, n_h - 1), 0)),
            pl.BlockSpec((tm, LANES), lambda i, *_: (i, (DN_QKV + DN_HEADS * DN_DV) // LANES)),
            pl.BlockSpec((DN_CONV, DN_QKV), lambda i, *_: (0, 0)),
            pl.BlockSpec((1, LANES), lambda i, *_: (0, 0)),
            pl.BlockSpec((1, LANES), lambda i, *_: (0, 0)),
        ],
        out_specs=[
            pl.BlockSpec((tm, kdim), lambda i, *_: (i, 0)),
            pl.BlockSpec((tm, kdim), lambda i, *_: (i, 0)),
            pl.BlockSpec((tm, kdim), lambda i, *_: (i, 0)),
            pl.BlockSpec((tm, LANES), lambda i, *_: (i, 0)),
        ],
    )
    return pl.pallas_call(
        _prep_odd_kernel,
        grid_spec=grid_spec,
        out_shape=[
            jax.ShapeDtypeStruct((ta, kdim), BF16),
            jax.ShapeDtypeStruct((ta, kdim), BF16),
            jax.ShapeDtypeStruct((ta, kdim), BF16),
            jax.ShapeDtypeStruct((ta, LANES), F32),
        ],
        compiler_params=_cparams(("arbitrary",)),
    )(first_flags, last_flags, p, p, p, p, conv_w, arow, brow)


def _deltanet_kernel(q_ref, k_ref, v_ref, gb_ref, gbt_ref, s0_ref, o_ref, sf_ref, s_ref,
                     *, reverse, n_chunks, dir_off):
    t = pl.program_id(1)

    @pl.when(t == 0)
    def _():
        s_ref[...] = s0_ref[...]

    C = DN_CHUNK
    ii = lax.broadcasted_iota(jnp.int32, (C, C), 0)
    jj = lax.broadcasted_iota(jnp.int32, (C, C), 1)
    if reverse:
        incl = ii <= jj
        strict = ii < jj
    else:
        incl = ii >= jj
        strict = ii > jj
    tri = jnp.where(incl, 1.0, 0.0).astype(F32)
    if reverse:
        tri_t = jnp.where(ii >= jj, 1.0, 0.0).astype(F32)
    else:
        tri_t = jnp.where(ii <= jj, 1.0, 0.0).astype(F32)
    eye = jnp.where(ii == jj, 1.0, 0.0).astype(F32)
    blk = ii ^ jj

    order = range(n_chunks - 1, -1, -1) if reverse else range(n_chunks)
    for c in order:
        r0 = c * C
        gb_c = gb_ref[r0:r0 + C, :]
        gcol_all = jnp.dot(tri, gb_c, preferred_element_type=F32, precision=HIGHEST)
        grow_all = jnp.dot(gbt_ref[c], tri_t, preferred_element_type=F32, precision=HIGHEST)
        for h in range(DN_HEADS):
            gi = dir_off + h
            bi = 2 * DN_HEADS + dir_off + h
            cs = slice(h * DN_DK, (h + 1) * DN_DK)
            gc = gcol_all[:, gi:gi + 1]
            gr = grow_all[gi:gi + 1, :]
            beta = gb_c[:, bi:bi + 1]
            qb = q_ref[r0:r0 + C, cs]
            kb16 = k_ref[r0:r0 + C, cs]
            vb = v_ref[r0:r0 + C, cs]
            kf = kb16.astype(F32)
            decay = jnp.where(incl, jnp.exp(jnp.where(incl, gc - gr, 0.0)), 0.0)
            kbeta = kf * beta
            kk = lax.dot_general(kbeta.astype(BF16), kb16, NT_DIMS, preferred_element_type=F32)
            lm = jnp.where(strict, kk * decay, 0.0)
            dinv = eye - jnp.where(blk < 2, lm, 0.0)
            s = 2
            while s < C:
                off = jnp.where(jnp.logical_and(blk >= s, blk < 2 * s), lm, 0.0)
                tmp = _dot(dinv.astype(BF16), off.astype(BF16))
                dinv = dinv - _dot(tmp.astype(BF16), dinv.astype(BF16))
                s *= 2
            egc = jnp.exp(gc)
            db = dinv.astype(BF16)
            u = _dot(db, (vb.astype(F32) * beta).astype(BF16))
            w = _dot(db, (kbeta * egc).astype(BF16))
            qk = lax.dot_general(qb, kb16, NT_DIMS, preferred_element_type=F32)
            attn = jnp.where(incl, qk * decay, 0.0)
            qg = qb.astype(F32) * egc
            glast = gc[0:1, :] if reverse else gc[C - 1:C, :]
            kd = kf * jnp.exp(glast - gc)
            st = s_ref[h]
            wq = jnp.concatenate([w, qg], axis=0).astype(BF16)
            r = _dot(wq, st.astype(BF16))
            v_new = (u - r[:C]).astype(BF16)
            o = r[C:] + _dot(attn.astype(BF16), v_new)
            s_ref[h] = st * jnp.exp(glast) + lax.dot_general(kd.astype(BF16), v_new, TN_DIMS,
                                                             preferred_element_type=F32)
            o_ref[r0:r0 + C, cs] = o.astype(o_ref.dtype)

    @pl.when(t == pl.num_programs(1) - 1)
    def _():
        sf_ref[...] = s_ref[...]


def deltanet(q, k, v, gb, gbt, s0, nb, seq, row_off, reverse, tl):
    nblk = seq // tl
    n_chunks = tl // DN_CHUNK
    off_b = row_off // tl
    kdim = DN_HEADS * DN_DK

    def rb(b, t):
        tt = nblk - 1 - t if reverse else t
        return off_b + b * nblk + tt

    seq_spec = pl.BlockSpec((tl, kdim), lambda b, t: (rb(b, t), 0))
    st_spec = pl.BlockSpec((None, DN_HEADS, DN_DK, DN_DV), lambda b, t: (b, 0, 0, 0))
    return pl.pallas_call(
        functools.partial(_deltanet_kernel, reverse=reverse, n_chunks=n_chunks,
                          dir_off=DN_HEADS if reverse else 0),
        grid=(nb, nblk),
        in_specs=[
            seq_spec, seq_spec, seq_spec,
            pl.BlockSpec((tl, LANES), lambda b, t: (rb(b, t), 0)),
            pl.BlockSpec((n_chunks, 4 * DN_HEADS, DN_CHUNK), lambda b, t: (rb(b, t), 0, 0)),
            st_spec,
        ],
        out_specs=[
            pl.BlockSpec((tl, kdim), lambda b, t: (b * nblk + (nblk - 1 - t if reverse else t), 0)),
            st_spec,
        ],
        out_shape=[
            jax.ShapeDtypeStruct((nb * seq, kdim), BF16),
            jax.ShapeDtypeStruct((nb, DN_HEADS, DN_DK, DN_DV), F32),
        ],
        scratch_shapes=[pltpu.VMEM((DN_HEADS, DN_DK, DN_DV), F32)],
        compiler_params=_cparams(("arbitrary", "arbitrary")),
    )(q, k, v, gb, gbt, s0)


def _router_kernel(x_ref, g_ref, sh_ref, sc_ref, wr_ref, br_ref, f_ref, r_ref):
    h = _norm_mod(x_ref[...], g_ref[...], sh_ref[...], sc_ref[...])
    f_ref[...] = h
    logits = _dot(h.astype(BF16), wr_ref[...]) + br_ref[...]
    tm = logits.shape[0]
    lane = lax.broadcasted_iota(jnp.int32, (tm, LANES), 1)
    neg = -1e30
    big = 4 * LANES
    is_g = lane < N_GROUPS
    gl = jnp.where(is_g, logits, neg)
    gm = jnp.max(gl, axis=-1, keepdims=True)
    grp = jnp.min(jnp.where(gl == gm, lane, big), axis=-1, keepdims=True)
    psum = jnp.sum(jnp.where(is_g, jnp.exp(gl - gm), 0.0), axis=-1, keepdims=True)
    p_grp = 1.0 / psum
    e_lane = lane - N_GROUPS
    in_grp = jnp.logical_and(jnp.logical_and(e_lane >= 0, e_lane < N_EXPERTS),
                             (e_lane // EXPERTS_PER_GROUP) == grp)
    el = jnp.where(in_grp, logits, neg)
    m1 = jnp.max(el, axis=-1, keepdims=True)
    i1 = jnp.min(jnp.where(el == m1, lane, big), axis=-1, keepdims=True)
    el2 = jnp.where(lane == i1, neg, el)
    m2 = jnp.max(el2, axis=-1, keepdims=True)
    i2 = jnp.min(jnp.where(el2 == m2, lane, big), axis=-1, keepdims=True)
    e21 = jnp.exp(m2 - m1)
    w1 = p_grp / (1.0 + e21)
    w2 = p_grp * e21 / (1.0 + e21)
    e1 = (i1 - N_GROUPS).astype(F32)
    e2 = (i2 - N_GROUPS).astype(F32)
    r_ref[...] = jnp.where(lane == 0, e1, jnp.where(lane == 1, e2, jnp.where(lane == 2, w1,
                           jnp.where(lane == 3, w2, 0.0))))


def moe_router(xa, gain, shift, scale, w_router, b_router, tm, seq, nb):
    ta, d = xa.shape
    mrow = _mod_row_map(tm, seq, nb)
    return pl.pallas_call(
        _router_kernel,
        grid=(ta // tm,),
        in_specs=[
            pl.BlockSpec((tm, d), lambda i: (i, 0)),
            pl.BlockSpec((1, d), lambda i: (0, 0)),
            pl.BlockSpec((None, 1, d), mrow),
            pl.BlockSpec((None, 1, d), mrow),
            pl.BlockSpec((d, LANES), lambda i: (0, 0)),
            pl.BlockSpec((1, LANES), lambda i: (0, 0)),
        ],
        out_specs=[pl.BlockSpec((tm, d), lambda i: (i, 0)), pl.BlockSpec((tm, LANES), lambda i: (i, 0))],
        out_shape=[jax.ShapeDtypeStruct((ta, d), F32), jax.ShapeDtypeStruct((ta, LANES), F32)],
        compiler_params=_cparams(("arbitrary",)),
    )(xa, gain.reshape(1, d), shift, scale, w_router, b_router)


def _moe_ffn_kernel(te_ref, nu_ref, idx_ref, rw_ref, wgu_ref, wd_ref, f_hbm, o_ref,
                    xbuf, sem, wgu_bf, wd_bf):
    i = pl.program_id(0)
    tm = xbuf.shape[0]
    fdim = wd_bf.shape[0]

    @pl.when(i < nu_ref[0])
    def _():
        def issue(r, carry):
            tok = idx_ref[0, 0, r]
            pltpu.make_async_copy(f_hbm.at[pl.ds(tok, 1)], xbuf.at[pl.ds(r, 1)], sem).start()
            return carry

        lax.fori_loop(0, tm, issue, 0, unroll=8)
        prev = te_ref[jnp.maximum(i - 1, 0)]
        changed = jnp.logical_or(i == 0, te_ref[i] != prev)

        @pl.when(changed)
        def _():
            wgu_bf[...] = wgu_ref[...].astype(BF16)
            wd_bf[...] = wd_ref[...].astype(BF16)

        pltpu.make_async_copy(f_hbm.at[pl.ds(0, tm)], xbuf, sem).wait()
        xs = xbuf[...].astype(BF16)
        gu = _dot(xs, wgu_bf[...])
        hmid = _silu(gu[:, :fdim]) * gu[:, fdim:]
        o_ref[...] = _dot(hmid.astype(BF16), wd_bf[...]) * rw_ref[...]

    @pl.when(i >= nu_ref[0])
    def _():
        o_ref[...] = jnp.zeros_like(o_ref)


def moe_ffn(tile_expert, n_used, src_tok, row_w, w_gate_up, w_down, f, tm):
    n_tiles = src_tok.shape[0]
    _, d, f2 = w_gate_up.shape
    fdim = w_down.shape[1]
    grid_spec = pltpu.PrefetchScalarGridSpec(
        num_scalar_prefetch=2,
        grid=(n_tiles,),
        in_specs=[
            pl.BlockSpec((1, 1, tm), lambda i, te, nu: (i, 0, 0), memory_space=pltpu.SMEM),
            pl.BlockSpec((tm, 1), lambda i, te, nu: (i, 0)),
            pl.BlockSpec((None, d, f2), lambda i, te, nu: (te[i], 0, 0)),
            pl.BlockSpec((None, fdim, d), lambda i, te, nu: (te[i], 0, 0)),
            pl.BlockSpec(memory_space=pl.ANY),
        ],
        out_specs=pl.BlockSpec((tm, d), lambda i, te, nu: (i, 0)),
        scratch_shapes=[
            pltpu.VMEM((tm, d), F32),
            pltpu.SemaphoreType.DMA(()),
            pltpu.VMEM((d, f2), BF16),
            pltpu.VMEM((fdim, d), BF16),
        ],
    )
    return pl.pallas_call(
        _moe_ffn_kernel,
        grid_spec=grid_spec,
        out_shape=jax.ShapeDtypeStruct((n_tiles * tm, d), F32),
        compiler_params=_cparams(("arbitrary",)),
    )(tile_expert, n_used, src_tok, row_w, w_gate_up, w_down, f)


def _moe_combine_kernel(pos_ref, x_ref, gate_ref, y_hbm, o_ref, ybuf, sem):
    tm = x_ref.shape[0]

    def issue(r, carry):
        p = pos_ref[0, 0, r]
        pltpu.make_async_copy(y_hbm.at[pl.ds(p, 1)], ybuf.at[pl.ds(r, 1)], sem).start()
        return carry

    lax.fori_loop(0, 2 * tm, issue, 0, unroll=8)
    pltpu.make_async_copy(y_hbm.at[pl.ds(0, 2 * tm)], ybuf, sem).wait()
    o_ref[...] = x_ref[...] + gate_ref[...] * (ybuf[0:tm, :] + ybuf[tm:2 * tm, :])


def moe_combine(pos_tiles, xa, gate, y_sorted, tm, seq, nb):
    ta, d = xa.shape
    return pl.pallas_call(
        _moe_combine_kernel,
        grid=(ta // tm,),
        in_specs=[
            pl.BlockSpec((1, 1, 2 * tm), lambda i: (i, 0, 0), memory_space=pltpu.SMEM),
            pl.BlockSpec((tm, d), lambda i: (i, 0)),
            pl.BlockSpec((None, 1, d), _mod_row_map(tm, seq, nb)),
            pl.BlockSpec(memory_space=pl.ANY),
        ],
        out_specs=pl.BlockSpec((tm, d), lambda i: (i, 0)),
        out_shape=jax.ShapeDtypeStruct((ta, d), F32),
        scratch_shapes=[pltpu.VMEM((2 * tm, d), F32), pltpu.SemaphoreType.DMA(())],
        compiler_params=_cparams(("arbitrary",)),
    )(pos_tiles, xa, gate, y_sorted)


def moe_dispatch(route, tm_ffn, tm_comb):
    ta = route.shape[0]
    ids = route[:, 0:TOP_K].astype(jnp.int32)
    wts = route[:, TOP_K:2 * TOP_K]
    flat_e = ids.reshape(-1)
    n2 = flat_e.shape[0]
    onehot = (flat_e[:, None] == jnp.arange(N_EXPERTS, dtype=jnp.int32)[None, :]).astype(jnp.int32)
    csum = jnp.cumsum(onehot, axis=0)
    rank = jnp.sum((csum - onehot) * onehot, axis=1)
    counts = csum[-1]
    padded = ((counts + tm_ffn - 1) // tm_ffn) * tm_ffn
    ends = jnp.cumsum(padded)
    starts = ends - padded
    pos = starts[flat_e] + rank
    n_tiles = (n2 + N_EXPERTS * (tm_ffn - 1)) // tm_ffn
    n_pad = n_tiles * tm_ffn
    src_tok = jnp.zeros((n_pad,), jnp.int32).at[pos].set(jnp.arange(n2, dtype=jnp.int32) // TOP_K)
    row_w = jnp.zeros((n_pad,), F32).at[pos].set(wts.reshape(-1))
    tile_start = jnp.arange(n_tiles, dtype=jnp.int32) * tm_ffn
    tile_expert = jnp.minimum(jnp.searchsorted(ends, tile_start, side="right"), N_EXPERTS - 1).astype(jnp.int32)
    n_used = (ends[-1] // tm_ffn).astype(jnp.int32).reshape(1)
    pos_tiles = pos.reshape(ta // tm_comb, tm_comb, TOP_K).transpose(0, 2, 1).reshape(ta // tm_comb, 1,
                                                                                       TOP_K * tm_comb)
    return (tile_expert, n_used, src_tok.reshape(n_tiles, 1, tm_ffn), row_w.reshape(n_pad, 1),
            pos_tiles.astype(jnp.int32))


def _seq_flags(t_lat, seq, tc, cseq, tm):
    starts = np.arange(0, t_lat + tc, tm)
    first = np.where(starts < t_lat, starts % seq == 0, (starts - t_lat) % cseq == 0)
    ends = starts + tm
    last = np.where(starts < t_lat, ends % seq == 0, (ends - t_lat) % cseq == 0)
    return jnp.asarray(first.astype(np.int32)), jnp.asarray(last.astype(np.int32))


def kernel(x, c, ctx, c_ctx, w_ada, b_ada, norm_mix, norm_ffn, ev_w_in, ev_q_gain, ev_k_gain, ev_decay_f,
           ev_decay_b, ev_w_out, od_w_in, od_conv, od_a_log_f, od_a_log_b, od_dt_bias_f, od_dt_bias_b,
           od_out_gain, od_w_out, moe_w_group, moe_b_group, moe_w_expert, moe_b_expert, moe_w_gate_up,
           moe_w_down, final_norm_gain):
    nb, seq, d = x.shape
    cseq = ctx.shape[1]
    depth = w_ada.shape[0]
    t_lat = nb * seq
    tc = nb * cseq
    assert nb + 1 <= 8 and seq % cseq == 0 and cseq % RET_CHUNK == 0 and seq % GRID_W == 0

    tm = 512 if tc % 512 == 0 else cseq
    tm_prep = min(256, cseq)
    tq = min(256, cseq)
    tk = min(256, cseq)
    tl = 2 * DN_CHUNK
    tm_ffn = 256
    tm_comb = min(256, cseq)

    xa = jnp.concatenate([x.reshape(t_lat, d), ctx.reshape(tc, d)], axis=0)
    c8 = jnp.zeros((8, d), F32).at[:nb].set(c).at[nb].set(c_ctx)
    mod = adaln(c8, w_ada, b_ada)

    tabs = rope_tables(seq, tm_prep)
    first_flags, last_flags = _seq_flags(t_lat, seq, tc, cseq, tm_prep)
    ret_zero = jnp.zeros((nb, RET_HEADS, RET_DK, RET_DV), F32)
    dn_zero = jnp.zeros((nb, DN_HEADS, DN_DK, DN_DV), F32)

    for layer in range(depth):
        last = layer == depth - 1
        m = mod[layer].reshape(8, 6, 1, d)
        sh1, sc1, g1, sh2, sc2, g2 = (m[:, j] for j in range(6))
        i = layer // 2
        if layer % 2 == 0:
            w_in = ev_w_in[i].astype(BF16)
            p = norm_mod_matmul(xa, norm_mix[layer], sh1, sc1, w_in, tm, seq, nb)
            rq, rk, aq, ak, av = prep_even(p, tabs, ev_q_gain[i], ev_k_gain[i], tm_prep, seq, t_lat)
            dec = jnp.stack([ev_decay_f[i], ev_decay_b[i]]).astype(F32)
            oc, scf, scb = retention(dec, rq, rk, p, ret_zero, ret_zero, nb, cseq, t_lat // cseq)
            ol, _, _ = retention(dec, rq, rk, p, scf, scb, nb, seq, 0)
            att_l = attention(aq, ak, av, nb, seq, 0, seq, cseq, t_lat, tq, tk)
            att_c = attention(aq, ak, av, nb, cseq, t_lat // tq, 0, cseq, t_lat, tq, tk)
            o_ret = jnp.concatenate([ol, oc], axis=0)
            o_att = jnp.concatenate([att_l, att_c], axis=0)
            w_out = ev_w_out[i].astype(BF16)
            k1 = RET_HEADS * RET_DV
            xa = outproj_even(o_ret, o_att, w_out[:k1], w_out[k1:], xa, g1, tm, seq, nb)
        else:
            w_in = jnp.pad(od_w_in[i], ((0, 0), (0, ODD_IN_PAD - ODD_IN))).astype(BF16)
            p = norm_mod_matmul(xa, norm_mix[layer], sh1, sc1, w_in, tm, seq, nb)
            zpad = jnp.zeros((LANES - 2 * DN_HEADS,), F32)
            arow = jnp.concatenate([od_a_log_f[i], od_a_log_b[i], zpad]).reshape(1, LANES).astype(F32)
            brow = jnp.concatenate([od_dt_bias_f[i], od_dt_bias_b[i], zpad]).reshape(1, LANES).astype(F32)
            q, k, v, gb = prep_odd(p, od_conv[i].astype(F32), arow, brow, first_flags, last_flags, tm_prep)
            ta = t_lat + tc
            gbt = gb.reshape(ta // DN_CHUNK, DN_CHUNK, LANES)[:, :, :4 * DN_HEADS].transpose(0, 2, 1)
            oc_f, sc_f = deltanet(q, k, v, gb, gbt, dn_zero, nb, cseq, t_lat, False, tl)
            oc_b, sc_b = deltanet(q, k, v, gb, gbt, dn_zero, nb, cseq, t_lat, True, tl)
            ol_f, _ = deltanet(q, k, v, gb, gbt, sc_f, nb, seq, 0, False, tl)
            ol_b, _ = deltanet(q, k, v, gb, gbt, sc_b, nb, seq, 0, True, tl)
            o_f = jnp.concatenate([ol_f, oc_f], axis=0)
            o_b = jnp.concatenate([ol_b, oc_b], axis=0)
            xa = outproj_odd(o_f, o_b, p, od_out_gain[i], od_w_out[i].astype(BF16), xa, g1, tm, seq, nb)

        w_router = jnp.pad(jnp.concatenate([moe_w_group[layer], moe_w_expert[layer]], axis=1),
                           ((0, 0), (0, LANES - N_GROUPS - N_EXPERTS))).astype(BF16)
        b_router = jnp.pad(jnp.concatenate([moe_b_group[layer], moe_b_expert[layer]]),
                           (0, LANES - N_GROUPS - N_EXPERTS)).reshape(1, LANES).astype(F32)
        f, route = moe_router(xa, norm_ffn[layer], sh2, sc2, w_router, b_router, tm, seq, nb)
        tile_expert, n_used, src_tok, row_w, pos_tiles = moe_dispatch(route, tm_ffn, tm_comb)
        y_sorted = moe_ffn(tile_expert, n_used, src_tok, row_w, moe_w_gate_up[layer], moe_w_down[layer], f, tm_ffn)
        xa = moe_combine(pos_tiles, xa, g2, y_sorted, tm_comb, seq, nb)

    out = final_norm(xa, final_norm_gain, t_lat, tm)
    return out.reshape(nb, seq, d)
```

```python
import functools
import math

import numpy as np
import jax
import jax.numpy as jnp
from jax import lax
from jax.experimental import pallas as pl
from jax.experimental.pallas import tpu as pltpu

F32 = jnp.float32
BF16 = jnp.bfloat16
HIGHEST = lax.Precision.HIGHEST

EPS = 1e-6
GRID_W = 64
ROPE_BASE = 10000.0
RET_HEADS, RET_DK, RET_DV, RET_CHUNK = 8, 64, 128, 128
ATT_HEADS, ATT_KV_HEADS, ATT_HD = 8, 2, 64
DN_HEADS, DN_DK, DN_DV, DN_CHUNK, DN_CONV = 8, 128, 128, 64, 3
N_GROUPS, EXPERTS_PER_GROUP, TOP_K = 4, 8, 2
N_EXPERTS = N_GROUPS * EXPERTS_PER_GROUP

EVEN_IN = 2 * RET_HEADS * RET_DK + 2 * RET_HEADS * RET_DV + (ATT_HEADS + 2 * ATT_KV_HEADS) * ATT_HD
EVEN_ATT_COL = 2 * RET_HEADS * RET_DK + 2 * RET_HEADS * RET_DV
EVEN_ATT_W = (ATT_HEADS + 2 * ATT_KV_HEADS) * ATT_HD
DN_QKV = 2 * DN_HEADS * DN_DK + DN_HEADS * DN_DV
ODD_IN = DN_QKV + DN_HEADS * DN_DV + 4 * DN_HEADS
ODD_IN_PAD = ((ODD_IN + 127) // 128) * 128

LANES = 128
VMEM_LIMIT = 56 * 1024 * 1024

NT_DIMS = (((1,), (1,)), ((), ()))
TN_DIMS = (((0,), (0,)), ((), ()))


def _cparams(sem):
    return pltpu.CompilerParams(dimension_semantics=sem, vmem_limit_bytes=VMEM_LIMIT)


def _silu(x):
    return x / (1.0 + jnp.exp(-x))


def _dot(a, b):
    return jnp.dot(a, b, preferred_element_type=F32)


def _adaln_kernel(c_ref, w_ref, b_ref, o_ref):
    s = _silu(c_ref[...])
    o_ref[...] = _dot(s.astype(BF16), w_ref[...].astype(BF16)) + b_ref[...]


def adaln(c8, w_ada, b_ada):
    depth, d, n6 = w_ada.shape
    tn = min(n6, 1536)
    return pl.pallas_call(
        _adaln_kernel,
        grid=(depth, n6 // tn),
        in_specs=[
            pl.BlockSpec((8, d), lambda l, j: (0, 0)),
            pl.BlockSpec((None, d, tn), lambda l, j: (l, 0, j)),
            pl.BlockSpec((None, 1, tn), lambda l, j: (l, 0, j)),
        ],
        out_specs=pl.BlockSpec((None, 8, tn), lambda l, j: (l, 0, j)),
        out_shape=jax.ShapeDtypeStruct((depth, 8, n6), F32),
        compiler_params=_cparams(("arbitrary", "arbitrary")),
        name="adaln",
    )(c8, w_ada, b_ada.reshape(depth, 1, n6))


def _norm_mod(x, gain, shift, scale):
    ms = jnp.mean(x * x, axis=-1, keepdims=True)
    h = x * lax.rsqrt(ms + EPS) * gain
    return h * (1.0 + scale) + shift


def _nmm_kernel(x_ref, g_ref, sh_ref, sc_ref, w_ref, o_ref, *, nchunk):
    hb = _norm_mod(x_ref[...], g_ref[...], sh_ref[...], sc_ref[...]).astype(BF16)
    n = o_ref.shape[-1]
    for n0 in range(0, n, nchunk):
        o_ref[:, n0:n0 + nchunk] = _dot(hb, w_ref[:, n0:n0 + nchunk]).astype(o_ref.dtype)


def _mod_row_map(tm, seq, n_lat_batches):
    return lambda i: (jnp.minimum((i * tm) // seq, n_lat_batches), 0, 0)


def norm_mod_matmul(xa, gain, shift, scale, w, tm, seq, nb):
    ta, d = xa.shape
    n = w.shape[1]
    nchunk = 512 if n % 512 == 0 else 384
    mrow = _mod_row_map(tm, seq, nb)
    return pl.pallas_call(
        functools.partial(_nmm_kernel, nchunk=nchunk),
        grid=(ta // tm,),
        in_specs=[
            pl.BlockSpec((tm, d), lambda i: (i, 0)),
            pl.BlockSpec((1, d), lambda i: (0, 0)),
            pl.BlockSpec((None, 1, d), mrow),
            pl.BlockSpec((None, 1, d), mrow),
            pl.BlockSpec((d, n), lambda i: (0, 0)),
        ],
        out_specs=pl.BlockSpec((tm, n), lambda i: (i, 0)),
        out_shape=jax.ShapeDtypeStruct((ta, n), BF16),
        compiler_params=_cparams(("arbitrary",)),
        name="norm_mod_matmul",
    )(xa, gain.reshape(1, d), shift, scale, w)


def _final_norm_kernel(x_ref, g_ref, o_ref):
    x = x_ref[...]
    ms = jnp.mean(x * x, axis=-1, keepdims=True)
    o_ref[...] = x * lax.rsqrt(ms + EPS) * g_ref[...]


def final_norm(xa, gain, t_rows, tm):
    d = xa.shape[1]
    return pl.pallas_call(
        _final_norm_kernel,
        grid=(t_rows // tm,),
        in_specs=[pl.BlockSpec((tm, d), lambda i: (i, 0)), pl.BlockSpec((1, d), lambda i: (0, 0))],
        out_specs=pl.BlockSpec((tm, d), lambda i: (i, 0)),
        out_shape=jax.ShapeDtypeStruct((t_rows, d), F32),
        compiler_params=_cparams(("arbitrary",)),
        name="final_norm",
    )(xa, gain.reshape(1, d))


def _prep_even_kernel(qk_ref, att_ref, cos_ref, s1_ref, s2_ref, qg_ref, kg_ref, bd_ref,
                      rq_ref, rk_ref, aq_ref, ak_ref, av_ref):
    cos = cos_ref[...]
    s1 = s1_ref[...]
    s2 = s2_ref[...]
    bd = bd_ref[...]
    half = ATT_HD

    def rope(x):
        return x * cos + pltpu.roll(x, LANES - 16, 1) * s1 + pltpu.roll(x, 16, 1) * s2

    def head_norm(x, gain):
        ms = jnp.dot(x * x, bd, preferred_element_type=F32, precision=HIGHEST)
        return x * lax.rsqrt(ms + EPS) * gain

    nq = RET_HEADS * RET_DK // LANES
    for j in range(nq):
        cs = slice(j * LANES, (j + 1) * LANES)
        rq_ref[:, cs] = rope(qk_ref[:, cs].astype(F32)).astype(BF16)
        ks = slice(nq * LANES + j * LANES, nq * LANES + (j + 1) * LANES)
        rk_ref[:, cs] = (rope(qk_ref[:, ks].astype(F32)) * RET_DK ** -0.5).astype(BF16)

    qg = qg_ref[...]
    kg = kg_ref[...]
    for j in range(ATT_HEADS * ATT_HD // LANES):
        x = att_ref[:, j * LANES:(j + 1) * LANES].astype(F32)
        y = (rope(head_norm(x, qg)) * (ATT_HD ** -0.5 * math.log2(math.e))).astype(BF16)
        aq_ref[2 * j] = y[:, :half]
        aq_ref[2 * j + 1] = y[:, half:]
    c0 = ATT_HEADS * ATT_HD
    y = rope(head_norm(att_ref[:, c0:c0 + LANES].astype(F32), kg)).astype(BF16)
    ak_ref[0] = y[:, :half]
    ak_ref[1] = y[:, half:]
    v = att_ref[:, c0 + LANES:c0 + 2 * LANES]
    av_ref[0] = v[:, :half]
    av_ref[1] = v[:, half:]


def prep_even(p, tabs, q_gain, k_gain, tm, seq, t_lat):
    ta = p.shape[0]
    cos_t, s1_t, s2_t = tabs
    n_tab = seq // tm

    def tab_map(i):
        r = i * tm
        return (jnp.where(r < t_lat, (r % seq) // tm, n_tab), 0)

    ii = np.arange(LANES)
    bd = jnp.asarray((ii[:, None] // ATT_HD == ii[None, :] // ATT_HD).astype(np.float32) / ATT_HD)
    qg = jnp.tile(q_gain.astype(F32), LANES // ATT_HD).reshape(1, LANES)
    kg = jnp.tile(k_gain.astype(F32), LANES // ATT_HD).reshape(1, LANES)
    qkw = 2 * RET_HEADS * RET_DK
    tab_spec = pl.BlockSpec((tm, LANES), tab_map)
    one = lambda i: (0, 0)
    return pl.pallas_call(
        _prep_even_kernel,
        grid=(ta // tm,),
        in_specs=[
            pl.BlockSpec((tm, qkw), lambda i: (i, 0)),
            pl.BlockSpec((tm, EVEN_ATT_W), lambda i: (i, EVEN_ATT_COL // EVEN_ATT_W)),
            tab_spec, tab_spec, tab_spec,
            pl.BlockSpec((1, LANES), one), pl.BlockSpec((1, LANES), one),
            pl.BlockSpec((LANES, LANES), one),
        ],
        out_specs=[
            pl.BlockSpec((tm, RET_HEADS * RET_DK), lambda i: (i, 0)),
            pl.BlockSpec((tm, RET_HEADS * RET_DK), lambda i: (i, 0)),
            pl.BlockSpec((ATT_HEADS, tm, ATT_HD), lambda i: (0, i, 0)),
            pl.BlockSpec((ATT_KV_HEADS, tm, ATT_HD), lambda i: (0, i, 0)),
            pl.BlockSpec((ATT_KV_HEADS, tm, ATT_HD), lambda i: (0, i, 0)),
        ],
        out_shape=[
            jax.ShapeDtypeStruct((ta, RET_HEADS * RET_DK), BF16),
            jax.ShapeDtypeStruct((ta, RET_HEADS * RET_DK), BF16),
            jax.ShapeDtypeStruct((ATT_HEADS, ta, ATT_HD), BF16),
            jax.ShapeDtypeStruct((ATT_KV_HEADS, ta, ATT_HD), BF16),
            jax.ShapeDtypeStruct((ATT_KV_HEADS, ta, ATT_HD), BF16),
        ],
        compiler_params=_cparams(("arbitrary",)),
        name="prep_even",
    )(p, p, cos_t, s1_t, s2_t, qg, kg, bd)


def rope_tables(seq, tm):
    nf = ATT_HD // 4
    t = jnp.arange(seq)
    rows = (t // GRID_W).astype(F32)
    cols = (t % GRID_W).astype(F32)
    inv = ROPE_BASE ** (-jnp.arange(nf, dtype=F32) / nf)
    lane = np.arange(LANES)
    axis = (lane % ATT_HD) // (ATT_HD // 2)
    f = lane % nf
    upper = ((lane % (ATT_HD // 2)) >= nf)
    pos = jnp.where(jnp.asarray(axis)[None, :] == 0, rows[:, None], cols[:, None])
    ang = pos * inv[jnp.asarray(f)][None, :]
    cos = jnp.cos(ang)
    sin = jnp.sin(ang)
    s1 = jnp.where(jnp.asarray(upper)[None, :], 0.0, -sin)
    s2 = jnp.where(jnp.asarray(upper)[None, :], sin, 0.0)
    pad1 = jnp.ones((tm, LANES), F32)
    pad0 = jnp.zeros((tm, LANES), F32)
    return (jnp.concatenate([cos, pad1]), jnp.concatenate([s1, pad0]), jnp.concatenate([s2, pad0]))


def _retention_kernel(dec_ref, q_ref, k_ref, v_ref, g_ref, s0f_ref, s0b_ref,
                      o_ref, sff_ref, sfb_ref, sprev_ref, *, n_chunks):
    hp = pl.program_id(1)
    C = RET_CHUNK
    dk, dv = RET_DK, RET_DV
    pos = lax.broadcasted_iota(jnp.int32, (C, dk), 0).astype(F32)
    ii = lax.broadcasted_iota(jnp.int32, (C, C), 0)
    jj = lax.broadcasted_iota(jnp.int32, (C, C), 1)
    dpos = (ii - jj).astype(F32)
    for hh in range(2):
        h = 2 * hp + hh
        df = dec_ref[0, h]
        db = dec_ref[1, h]
        lf = -jnp.exp(jnp.full((C, C), df, F32))
        lb = -jnp.exp(jnp.full((C, C), db, F32))
        lfk = -jnp.exp(jnp.full((C, dk), df, F32))
        lbk = -jnp.exp(jnp.full((C, dk), db, F32))
        f_in = jnp.exp(lfk * (pos + 1.0))
        f_out = jnp.exp(lfk * (C - 1.0 - pos))
        b_in = jnp.exp(lbk * (C - pos))
        b_out = jnp.exp(lbk * pos)
        gcf = jnp.exp(-jnp.exp(jnp.full((dk, dv), df, F32)) * C)
        gcb = jnp.exp(-jnp.exp(jnp.full((dk, dv), db, F32)) * C)
        mask = jnp.where(dpos > 0, jnp.exp(lf * jnp.maximum(dpos, 0.0)),
                         jnp.where(dpos < 0, jnp.exp(lb * jnp.maximum(-dpos, 0.0)), 2.0))
        qs = slice(hh * dk, (hh + 1) * dk)
        vs = slice(hh * dv, (hh + 1) * dv)

        def fwd_body(n, s, qs=qs, vs=vs, f_out=f_out, gcf=gcf, hh=hh):
            sprev_ref[hh, n] = s
            r0 = pl.multiple_of(n * C, C)
            k = k_ref[pl.ds(r0, C), qs].astype(F32)
            v = v_ref[pl.ds(r0, C), vs]
            kv = lax.dot_general((k * f_out).astype(BF16), v, TN_DIMS, preferred_element_type=F32)
            return gcf * s + kv

        sff_ref[hh] = lax.fori_loop(0, n_chunks, fwd_body, s0f_ref[hh])

        def bwd_body(i, s, qs=qs, vs=vs, f_in=f_in, b_in=b_in, b_out=b_out, gcb=gcb, mask=mask, hh=hh):
            n = n_chunks - 1 - i
            r0 = pl.multiple_of(n * C, C)
            qb = q_ref[pl.ds(r0, C), qs]
            kb = k_ref[pl.ds(r0, C), qs]
            v = v_ref[pl.ds(r0, C), vs]
            q = qb.astype(F32)
            sc = lax.dot_general(qb, kb, NT_DIMS, preferred_element_type=F32) * mask
            o = _dot(sc.astype(BF16), v)
            o = o + _dot((q * f_in).astype(BF16), sprev_ref[hh, n].astype(BF16))
            o = o + _dot((q * b_in).astype(BF16), s.astype(BF16))
            o = o * lax.rsqrt(jnp.mean(o * o, axis=-1, keepdims=True) + EPS)
            gate = g_ref[pl.ds(r0, C), vs].astype(F32)
            o_ref[pl.ds(r0, C), vs] = (_silu(gate) * o).astype(o_ref.dtype)
            kv = lax.dot_general((kb.astype(F32) * b_out).astype(BF16), v, TN_DIMS, preferred_element_type=F32)
            return gcb * s + kv

        sfb_ref[hh] = lax.fori_loop(0, n_chunks, bwd_body, s0b_ref[hh])


def retention(dec, rq, rk, p, s0f, s0b, nb, seq, row_off_blocks):
    n_chunks = seq // RET_CHUNK
    hp_n = RET_HEADS // 2
    vcol = RET_HEADS * RET_DK * 2 // (2 * RET_DV)
    gcol = vcol + RET_HEADS * RET_DV // (2 * RET_DV)
    ta = rq.shape[0]
    st_spec = pl.BlockSpec((None, 2, RET_DK, RET_DV), lambda b, hp, *_: (b, hp, 0, 0))
    grid_spec = pltpu.PrefetchScalarGridSpec(
        num_scalar_prefetch=1,
        grid=(nb, hp_n),
        in_specs=[
            pl.BlockSpec((seq, 2 * RET_DK), lambda b, hp, *_: (row_off_blocks + b, hp)),
            pl.BlockSpec((seq, 2 * RET_DK), lambda b, hp, *_: (row_off_blocks + b, hp)),
            pl.BlockSpec((seq, 2 * RET_DV), lambda b, hp, *_: (row_off_blocks + b, vcol + hp)),
            pl.BlockSpec((seq, 2 * RET_DV), lambda b, hp, *_: (row_off_blocks + b, gcol + hp)),
            st_spec, st_spec,
        ],
        out_specs=[
            pl.BlockSpec((seq, 2 * RET_DV), lambda b, hp, *_: (b, hp)),
            st_spec, st_spec,
        ],
        scratch_shapes=[pltpu.VMEM((2, n_chunks, RET_DK, RET_DV), F32)],
    )
    st_shape = jax.ShapeDtypeStruct((nb, RET_HEADS, RET_DK, RET_DV), F32)
    return pl.pallas_call(
        functools.partial(_retention_kernel, n_chunks=n_chunks),
        grid_spec=grid_spec,
        out_shape=[jax.ShapeDtypeStruct((nb * seq, RET_HEADS * RET_DV), BF16), st_shape, st_shape],
        compiler_params=_cparams(("arbitrary", "arbitrary")),
        name="retention",
    )(dec, rq, rk, p, p, s0f, s0b)


ATT_VT_ROWS = ATT_HD + 16


def _attn_kernel(q_ref, k_ref, vt_ref, o_ref, *s_refs, tk, c_start, c_end, rep):
    tq = q_ref.shape[1]
    sets = (s_refs[:rep], s_refs[rep:])

    def scores(bufs, j):
        c0 = pl.multiple_of(j * tk, tk)
        k = k_ref[pl.ds(c0, tk), :]
        mxs = []
        for r in range(rep):
            s = lax.dot_general(k, q_ref[r], NT_DIMS, preferred_element_type=F32)
            bufs[r][...] = s
            mxs.append(jnp.max(s, axis=0, keepdims=True))
        return tuple(mxs)

    def softmax_pv(bufs, j, mxs, ms, accs):
        vt = vt_ref[j]
        new_m, new_acc = [], []
        for r in range(rep):
            m_new = jnp.maximum(ms[r], mxs[r])
            a = jnp.exp2(ms[r] - m_new)
            p = jnp.exp2(bufs[r][...] - m_new).astype(BF16)
            new_acc.append(a * accs[r] + _dot(vt, p))
            new_m.append(m_new)
        return tuple(new_m), tuple(new_acc)

    n = c_end - c_start
    last = c_end - 1

    def pair_body(t, carry):
        mx0, ms, accs = carry
        j = c_start + 2 * t
        mx1 = scores(sets[1], j + 1)
        ms, accs = softmax_pv(sets[0], j, mx0, ms, accs)
        mx0 = scores(sets[0], jnp.minimum(j + 2, last))
        ms, accs = softmax_pv(sets[1], j + 1, mx1, ms, accs)
        return mx0, ms, accs

    ms = tuple(jnp.full((1, tq), -1e30, F32) for _ in range(rep))
    accs = tuple(jnp.zeros((ATT_VT_ROWS, tq), F32) for _ in range(rep))
    mx0 = scores(sets[0], c_start)
    if n // 2:
        mx0, ms, accs = lax.fori_loop(0, n // 2, pair_body, (mx0, ms, accs))
    if n % 2:
        ms, accs = softmax_pv(sets[0], last, mx0, ms, accs)
    outs = [(acc[:ATT_HD, :] / acc[ATT_HD:ATT_HD + 1, :]).T for acc in accs]
    o_ref[...] = jnp.concatenate(outs, axis=-1).astype(o_ref.dtype)


def attention(aq, kcat, vtcat, seq_q, q_off_blocks, c_start, tq, tk):
    rep = ATT_HEADS // ATT_KV_HEADS
    _, nb, lk, _ = kcat.shape
    nq = seq_q // tq
    n_chunks = lk // tk
    return pl.pallas_call(
        functools.partial(_attn_kernel, tk=tk, c_start=c_start, c_end=n_chunks, rep=rep),
        grid=(nb, ATT_KV_HEADS, nq),
        in_specs=[
            pl.BlockSpec((rep, tq, ATT_HD), lambda b, g, i: (g, q_off_blocks + b * nq + i, 0)),
            pl.BlockSpec((None, None, lk, ATT_HD), lambda b, g, i: (g, b, 0, 0)),
            pl.BlockSpec((None, None, n_chunks, ATT_VT_ROWS, tk), lambda b, g, i: (g, b, 0, 0, 0)),
        ],
        out_specs=pl.BlockSpec((tq, rep * ATT_HD), lambda b, g, i: (b * nq + i, g)),
        out_shape=jax.ShapeDtypeStruct((nb * seq_q, ATT_HEADS * ATT_HD), BF16),
        scratch_shapes=[pltpu.VMEM((tk, tq), F32) for _ in range(2 * rep)],
        compiler_params=_cparams(("arbitrary", "arbitrary", "arbitrary")),
        name="attention",
    )(aq, kcat, vtcat)


def _outproj_even_kernel(a1_ref, a2_ref, w1_ref, w2_ref, res_ref, gate_ref, o_ref):
    y = _dot(a1_ref[...], w1_ref[...]) + _dot(a2_ref[...], w2_ref[...])
    o_ref[...] = res_ref[...] + gate_ref[...] * y


def outproj_even(o_ret, o_att, w1, w2, xa, gate, tm, seq, nb):
    ta, d = xa.shape
    k1, k2 = w1.shape[0], w2.shape[0]
    return pl.pallas_call(
        _outproj_even_kernel,
        grid=(ta // tm,),
        in_specs=[
            pl.BlockSpec((tm, k1), lambda i: (i, 0)),
            pl.BlockSpec((tm, k2), lambda i: (i, 0)),
            pl.BlockSpec((k1, d), lambda i: (0, 0)),
            pl.BlockSpec((k2, d), lambda i: (0, 0)),
            pl.BlockSpec((tm, d), lambda i: (i, 0)),
            pl.BlockSpec((None, 1, d), _mod_row_map(tm, seq, nb)),
        ],
        out_specs=pl.BlockSpec((tm, d), lambda i: (i, 0)),
        out_shape=jax.ShapeDtypeStruct((ta, d), F32),
        compiler_params=_cparams(("arbitrary",)),
        name="outproj_even",
    )(o_ret, o_att, w1, w2, xa, gate)


def _outproj_odd_kernel(of_ref, ob_ref, z_ref, og_ref, w_ref, res_ref, gate_ref, o_ref):
    og = og_ref[...]
    parts = []
    for h in range(DN_HEADS):
        cs = slice(h * DN_DV, (h + 1) * DN_DV)
        o = of_ref[:, cs].astype(F32) + ob_ref[:, cs].astype(F32)
        o = o * lax.rsqrt(jnp.mean(o * o, axis=-1, keepdims=True) + EPS) * og
        parts.append((o * _silu(z_ref[:, cs].astype(F32))).astype(BF16))
    a = jnp.concatenate(parts, axis=-1)
    o_ref[...] = res_ref[...] + gate_ref[...] * _dot(a, w_ref[...])


def outproj_odd(o_f, o_b, p, out_gain, w, xa, gate, tm, seq, nb):
    ta, d = xa.shape
    kdim = DN_HEADS * DN_DV
    return pl.pallas_call(
        _outproj_odd_kernel,
        grid=(ta // tm,),
        in_specs=[
            pl.BlockSpec((tm, kdim), lambda i: (i, 0)),
            pl.BlockSpec((tm, kdim), lambda i: (i, 0)),
            pl.BlockSpec((tm, kdim), lambda i: (i, DN_QKV // kdim)),
            pl.BlockSpec((1, DN_DV), lambda i: (0, 0)),
            pl.BlockSpec((kdim, d), lambda i: (0, 0)),
            pl.BlockSpec((tm, d), lambda i: (i, 0)),
            pl.BlockSpec((None, 1, d), _mod_row_map(tm, seq, nb)),
        ],
        out_specs=pl.BlockSpec((tm, d), lambda i: (i, 0)),
        out_shape=jax.ShapeDtypeStruct((ta, d), F32),
        compiler_params=_cparams(("arbitrary",)),
        name="outproj_odd",
    )(o_f, o_b, p, out_gain.reshape(1, DN_DV).astype(F32), w, xa, gate)


def _prep_odd_kernel(first_ref, last_ref, x_ref, prev_ref, next_ref, ab_ref, cw_ref, arow_ref, brow_ref,
                     q_ref, k_ref, v_ref, gb_ref):
    i = pl.program_id(0)
    tm = x_ref.shape[0]
    hrows = prev_ref.shape[0]
    keep_prev = 1.0 - first_ref[i].astype(F32)
    keep_next = 1.0 - last_ref[i].astype(F32)
    row = lax.broadcasted_iota(jnp.int32, (tm, LANES), 0)
    is_first = row == 0
    is_last = row == tm - 1
    n_qk = 2 * DN_HEADS * DN_DK // LANES
    n_q = DN_HEADS * DN_DK // LANES
    outs = (q_ref, k_ref, v_ref)
    for j in range(DN_QKV // LANES):
        cs = slice(j * LANES, (j + 1) * LANES)
        x = x_ref[:, cs].astype(F32)
        xp = prev_ref[:, cs].astype(F32)[hrows - 1:hrows, :] * keep_prev
        xn = next_ref[:, cs].astype(F32)[0:1, :] * keep_next
        x_dn = jnp.where(is_first, xp, pltpu.roll(x, 1, 0))
        x_up = jnp.where(is_last, xn, pltpu.roll(x, tm - 1, 0))
        w = cw_ref[:, cs]
        y = _silu(x_dn * w[0:1, :] + x * w[1:2, :] + x_up * w[2:3, :])
        if j < n_qk:
            y = y * lax.rsqrt(jnp.sum(y * y, axis=-1, keepdims=True) + EPS)
            if j < n_q:
                y = y * DN_DK ** -0.5
        lj = j % n_q
        outs[j // n_q][:, lj * LANES:(lj + 1) * LANES] = y.astype(BF16)

    a = ab_ref[...].astype(F32)
    lane = lax.broadcasted_iota(jnp.int32, (tm, LANES), 1)
    z = a + brow_ref[...]
    softplus = jnp.maximum(z, 0.0) + jnp.log(1.0 + jnp.exp(-jnp.abs(z)))
    g = -jnp.exp(arow_ref[...]) * softplus
    beta = 1.0 / (1.0 + jnp.exp(-a))
    gb_ref[...] = jnp.where(lane < 2 * DN_HEADS, g, jnp.where(lane < 4 * DN_HEADS, beta, 0.0))


def prep_odd(p, conv_w, arow, brow, first_flags, last_flags, tm):
    ta = p.shape[0]
    halo = 16
    hb = tm // halo
    n_h = ta // halo
    kdim = DN_HEADS * DN_DK
    grid_spec = pltpu.PrefetchScalarGridSpec(
        num_scalar_prefetch=2,
        grid=(ta // tm,),
        in_specs=[
            pl.BlockSpec((tm, DN_QKV), lambda i, *_: (i, 0)),
            pl.BlockSpec((halo, DN_QKV), lambda i, *_: (jnp.maximum(i * hb - 1, 0), 0)),
            pl.BlockSpec((halo, DN_QKV), lambda i, *_: (jnp.minimum((i + 1) * hb, n_h - 1), 0)),
            pl.BlockSpec((tm, LANES), lambda i, *_: (i, (DN_QKV + DN_HEADS * DN_DV) // LANES)),
            pl.BlockSpec((DN_CONV, DN_QKV), lambda i, *_: (0, 0)),
            pl.BlockSpec((1, LANES), lambda i, *_: (0, 0)),
            pl.BlockSpec((1, LANES), lambda i, *_: (0, 0)),
        ],
        out_specs=[
            pl.BlockSpec((tm, kdim), lambda i, *_: (i, 0)),
            pl.BlockSpec((tm, kdim), lambda i, *_: (i, 0)),
            pl.BlockSpec((tm, kdim), lambda i, *_: (i, 0)),
            pl.BlockSpec((tm, LANES), lambda i, *_: (i, 0)),
        ],
    )
    return pl.pallas_call(
        _prep_odd_kernel,
        grid_spec=grid_spec,
        out_shape=[
            jax.ShapeDtypeStruct((ta, kdim), BF16),
            jax.ShapeDtypeStruct((ta, kdim), BF16),
            jax.ShapeDtypeStruct((ta, kdim), BF16),
            jax.ShapeDtypeStruct((ta, LANES), F32),
        ],
        compiler_params=_cparams(("arbitrary",)),
        name="prep_odd",
    )(first_flags, last_flags, p, p, p, p, conv_w, arow, brow)


def _deltanet_kernel(q_ref, k_ref, v_ref, gb_ref, gbt_ref, s0_ref, o_ref, sf_ref, s_ref,
                     *, reverse, n_chunks, dir_off):
    t = pl.program_id(1)

    @pl.when(t == 0)
    def _():
        s_ref[...] = s0_ref[...]

    C = DN_CHUNK
    ii = lax.broadcasted_iota(jnp.int32, (C, C), 0)
    jj = lax.broadcasted_iota(jnp.int32, (C, C), 1)
    if reverse:
        incl = ii <= jj
        strict = ii < jj
    else:
        incl = ii >= jj
        strict = ii > jj
    tri = jnp.where(incl, 1.0, 0.0).astype(F32)
    if reverse:
        tri_t = jnp.where(ii >= jj, 1.0, 0.0).astype(F32)
    else:
        tri_t = jnp.where(ii <= jj, 1.0, 0.0).astype(F32)
    eye = jnp.where(ii == jj, 1.0, 0.0).astype(F32)
    blk = ii ^ jj

    order = list(range(n_chunks - 1, -1, -1) if reverse else range(n_chunks))
    items = [(c, h) for c in order for h in range(DN_HEADS)]
    gcols, grows, gbs = {}, {}, {}
    for c in order:
        gb_c = gb_ref[c * C:(c + 1) * C, :]
        gbs[c] = gb_c
        gcols[c] = jnp.dot(tri, gb_c, preferred_element_type=F32, precision=HIGHEST)
        grows[c] = jnp.dot(gbt_ref[c], tri_t, preferred_element_type=F32, precision=HIGHEST)

    qb, kb16, decay, kbeta, egc, kd, gl, rhs = {}, {}, {}, {}, {}, {}, {}, {}
    for it in items:
        c, h = it
        gi = dir_off + h
        bi = 2 * DN_HEADS + dir_off + h
        rows = slice(c * C, (c + 1) * C)
        cs = slice(h * DN_DK, (h + 1) * DN_DK)
        gc = gcols[c][:, gi:gi + 1]
        gr = grows[c][gi:gi + 1, :]
        beta = gbs[c][:, bi:bi + 1]
        qb[it] = q_ref[rows, cs]
        kb16[it] = k_ref[rows, cs]
        kf = kb16[it].astype(F32)
        decay[it] = jnp.where(incl, jnp.exp(jnp.where(incl, gc - gr, 0.0)), 0.0)
        kbeta[it] = kf * beta
        egc[it] = jnp.exp(gc)
        glast = gc[0:1, :] if reverse else gc[C - 1:C, :]
        kd[it] = (kf * jnp.exp(glast - gc)).astype(BF16)
        gl[it] = jnp.exp(glast)
        rhs[it] = jnp.concatenate([v_ref[rows, cs].astype(F32) * beta, kbeta[it] * egc[it]], axis=1).astype(BF16)

    kk = {it: lax.dot_general(kbeta[it].astype(BF16), kb16[it], NT_DIMS, preferred_element_type=F32)
          for it in items}
    qk = {it: lax.dot_general(qb[it], kb16[it], NT_DIMS, preferred_element_type=F32) for it in items}
    lm = {it: jnp.where(strict, kk[it] * decay[it], 0.0) for it in items}
    attn = {it: jnp.where(incl, qk[it] * decay[it], 0.0).astype(BF16) for it in items}
    dinv = {it: eye - jnp.where(blk < 2, lm[it], 0.0) for it in items}
    s = 2
    while s < C:
        in_band = jnp.logical_and(blk >= s, blk < 2 * s)
        tmp = {it: _dot(dinv[it].astype(BF16), jnp.where(in_band, lm[it], 0.0).astype(BF16)) for it in items}
        dinv = {it: dinv[it] - _dot(tmp[it].astype(BF16), dinv[it].astype(BF16)) for it in items}
        s *= 2
    uw = {it: _dot(dinv[it].astype(BF16), rhs[it]) for it in items}
    wq = {it: jnp.concatenate([uw[it][:, DN_DV:], qb[it].astype(F32) * egc[it]], axis=0).astype(BF16)
          for it in items}

    states = [s_ref[h] for h in range(DN_HEADS)]
    for c in order:
        its = [(c, h) for h in range(DN_HEADS)]
        r = {it: _dot(wq[it], states[it[1]].astype(BF16)) for it in its}
        v_new = {it: (uw[it][:, :DN_DV] - r[it][:C]).astype(BF16) for it in its}
        o = {it: r[it][C:] + _dot(attn[it], v_new[it]) for it in its}
        for it in its:
            h = it[1]
            states[h] = states[h] * gl[it] + lax.dot_general(kd[it], v_new[it], TN_DIMS,
                                                             preferred_element_type=F32)
        for it in its:
            h = it[1]
            o_ref[c * C:(c + 1) * C, h * DN_DK:(h + 1) * DN_DK] = o[it].astype(o_ref.dtype)
    for h in range(DN_HEADS):
        s_ref[h] = states[h]

    @pl.when(t == pl.num_programs(1) - 1)
    def _():
        sf_ref[...] = s_ref[...]


def deltanet(q, k, v, gb, gbt, s0, nb, seq, row_off, reverse, tl):
    nblk = seq // tl
    n_chunks = tl // DN_CHUNK
    off_b = row_off // tl
    kdim = DN_HEADS * DN_DK

    def rb(b, t):
        tt = nblk - 1 - t if reverse else t
        return off_b + b * nblk + tt

    seq_spec = pl.BlockSpec((tl, kdim), lambda b, t: (rb(b, t), 0))
    st_spec = pl.BlockSpec((None, DN_HEADS, DN_DK, DN_DV), lambda b, t: (b, 0, 0, 0))
    return pl.pallas_call(
        functools.partial(_deltanet_kernel, reverse=reverse, n_chunks=n_chunks,
                          dir_off=DN_HEADS if reverse else 0),
        grid=(nb, nblk),
        in_specs=[
            seq_spec, seq_spec, seq_spec,
            pl.BlockSpec((tl, LANES), lambda b, t: (rb(b, t), 0)),
            pl.BlockSpec((n_chunks, 4 * DN_HEADS, DN_CHUNK), lambda b, t: (rb(b, t), 0, 0)),
            st_spec,
        ],
        out_specs=[
            pl.BlockSpec((tl, kdim), lambda b, t: (b * nblk + (nblk - 1 - t if reverse else t), 0)),
            st_spec,
        ],
        out_shape=[
            jax.ShapeDtypeStruct((nb * seq, kdim), BF16),
            jax.ShapeDtypeStruct((nb, DN_HEADS, DN_DK, DN_DV), F32),
        ],
        scratch_shapes=[pltpu.VMEM((DN_HEADS, DN_DK, DN_DV), F32)],
        compiler_params=_cparams(("arbitrary", "arbitrary")),
        name="deltanet_bwd" if reverse else "deltanet_fwd",
    )(q, k, v, gb, gbt, s0)


def _router_kernel(x_ref, g_ref, sh_ref, sc_ref, wr_ref, br_ref, f_ref, r_ref):
    h = _norm_mod(x_ref[...], g_ref[...], sh_ref[...], sc_ref[...])
    f_ref[...] = h
    logits = _dot(h.astype(BF16), wr_ref[...]) + br_ref[...]
    tm = logits.shape[0]
    lane = lax.broadcasted_iota(jnp.int32, (tm, LANES), 1)
    neg = -1e30
    big = 4 * LANES
    is_g = lane < N_GROUPS
    gl = jnp.where(is_g, logits, neg)
    gm = jnp.max(gl, axis=-1, keepdims=True)
    grp = jnp.min(jnp.where(gl == gm, lane, big), axis=-1, keepdims=True)
    psum = jnp.sum(jnp.where(is_g, jnp.exp(gl - gm), 0.0), axis=-1, keepdims=True)
    p_grp = 1.0 / psum
    e_lane = lane - N_GROUPS
    in_grp = jnp.logical_and(jnp.logical_and(e_lane >= 0, e_lane < N_EXPERTS),
                             (e_lane // EXPERTS_PER_GROUP) == grp)
    el = jnp.where(in_grp, logits, neg)
    m1 = jnp.max(el, axis=-1, keepdims=True)
    i1 = jnp.min(jnp.where(el == m1, lane, big), axis=-1, keepdims=True)
    el2 = jnp.where(lane == i1, neg, el)
    m2 = jnp.max(el2, axis=-1, keepdims=True)
    i2 = jnp.min(jnp.where(el2 == m2, lane, big), axis=-1, keepdims=True)
    e21 = jnp.exp(m2 - m1)
    w1 = p_grp / (1.0 + e21)
    w2 = p_grp * e21 / (1.0 + e21)
    e1 = (i1 - N_GROUPS).astype(F32)
    e2 = (i2 - N_GROUPS).astype(F32)
    r_ref[...] = jnp.where(lane == 0, e1, jnp.where(lane == 1, e2, jnp.where(lane == 2, w1,
                           jnp.where(lane == 3, w2, 0.0))))


def moe_router(xa, gain, shift, scale, w_router, b_router, tm, seq, nb):
    ta, d = xa.shape
    mrow = _mod_row_map(tm, seq, nb)
    return pl.pallas_call(
        _router_kernel,
        grid=(ta // tm,),
        in_specs=[
            pl.BlockSpec((tm, d), lambda i: (i, 0)),
            pl.BlockSpec((1, d), lambda i: (0, 0)),
            pl.BlockSpec((None, 1, d), mrow),
            pl.BlockSpec((None, 1, d), mrow),
            pl.BlockSpec((d, LANES), lambda i: (0, 0)),
            pl.BlockSpec((1, LANES), lambda i: (0, 0)),
        ],
        out_specs=[pl.BlockSpec((tm, d), lambda i: (i, 0)), pl.BlockSpec((tm, LANES), lambda i: (i, 0))],
        out_shape=[jax.ShapeDtypeStruct((ta, d), F32), jax.ShapeDtypeStruct((ta, LANES), F32)],
        compiler_params=_cparams(("arbitrary",)),
        name="moe_router",
    )(xa, gain.reshape(1, d), shift, scale, w_router, b_router)


def _moe_ffn_kernel(te_ref, nu_ref, idx_ref, rw_ref, wgu_ref, wd_ref, f_hbm, o_ref,
                    xbuf, sem, wgu_bf, wd_bf):
    i = pl.program_id(0)
    tm = xbuf.shape[0]
    fdim = wd_bf.shape[0]

    @pl.when(i < nu_ref[0])
    def _():
        def issue(r, carry):
            tok = idx_ref[0, 0, r]
            pltpu.make_async_copy(f_hbm.at[pl.ds(tok, 1)], xbuf.at[pl.ds(r, 1)], sem).start()
            return carry

        lax.fori_loop(0, tm, issue, 0, unroll=8)
        prev = te_ref[jnp.maximum(i - 1, 0)]
        changed = jnp.logical_or(i == 0, te_ref[i] != prev)

        @pl.when(changed)
        def _():
            wgu_bf[...] = wgu_ref[...].astype(BF16)
            wd_bf[...] = wd_ref[...].astype(BF16)

        pltpu.make_async_copy(f_hbm.at[pl.ds(0, tm)], xbuf, sem).wait()
        xs = xbuf[...].astype(BF16)
        gu = _dot(xs, wgu_bf[...])
        hmid = _silu(gu[:, :fdim]) * gu[:, fdim:]
        o_ref[...] = _dot(hmid.astype(BF16), wd_bf[...]) * rw_ref[...]

    @pl.when(i >= nu_ref[0])
    def _():
        o_ref[...] = jnp.zeros_like(o_ref)


def moe_ffn(tile_expert, n_used, src_tok, row_w, w_gate_up, w_down, f, tm):
    n_tiles = src_tok.shape[0]
    _, d, f2 = w_gate_up.shape
    fdim = w_down.shape[1]
    grid_spec = pltpu.PrefetchScalarGridSpec(
        num_scalar_prefetch=2,
        grid=(n_tiles,),
        in_specs=[
            pl.BlockSpec((1, 1, tm), lambda i, te, nu: (i, 0, 0), memory_space=pltpu.SMEM),
            pl.BlockSpec((tm, 1), lambda i, te, nu: (i, 0)),
            pl.BlockSpec((None, d, f2), lambda i, te, nu: (te[i], 0, 0)),
            pl.BlockSpec((None, fdim, d), lambda i, te, nu: (te[i], 0, 0)),
            pl.BlockSpec(memory_space=pl.ANY),
        ],
        out_specs=pl.BlockSpec((tm, d), lambda i, te, nu: (i, 0)),
        scratch_shapes=[
            pltpu.VMEM((tm, d), F32),
            pltpu.SemaphoreType.DMA(()),
            pltpu.VMEM((d, f2), BF16),
            pltpu.VMEM((fdim, d), BF16),
        ],
    )
    return pl.pallas_call(
        _moe_ffn_kernel,
        grid_spec=grid_spec,
        out_shape=jax.ShapeDtypeStruct((n_tiles * tm, d), F32),
        compiler_params=_cparams(("arbitrary",)),
        name="moe_ffn",
    )(tile_expert, n_used, src_tok, row_w, w_gate_up, w_down, f)


def _moe_combine_kernel(pos_ref, x_ref, gate_ref, y_hbm, o_ref, ybuf, sem):
    tm = x_ref.shape[0]

    def issue(r, carry):
        p = pos_ref[0, 0, r]
        pltpu.make_async_copy(y_hbm.at[pl.ds(p, 1)], ybuf.at[pl.ds(r, 1)], sem).start()
        return carry

    lax.fori_loop(0, 2 * tm, issue, 0, unroll=8)
    pltpu.make_async_copy(y_hbm.at[pl.ds(0, 2 * tm)], ybuf, sem).wait()
    o_ref[...] = x_ref[...] + gate_ref[...] * (ybuf[0:tm, :] + ybuf[tm:2 * tm, :])


def moe_combine(pos_tiles, xa, gate, y_sorted, tm, seq, nb):
    ta, d = xa.shape
    return pl.pallas_call(
        _moe_combine_kernel,
        grid=(ta // tm,),
        in_specs=[
            pl.BlockSpec((1, 1, 2 * tm), lambda i: (i, 0, 0), memory_space=pltpu.SMEM),
            pl.BlockSpec((tm, d), lambda i: (i, 0)),
            pl.BlockSpec((None, 1, d), _mod_row_map(tm, seq, nb)),
            pl.BlockSpec(memory_space=pl.ANY),
        ],
        out_specs=pl.BlockSpec((tm, d), lambda i: (i, 0)),
        out_shape=jax.ShapeDtypeStruct((ta, d), F32),
        scratch_shapes=[pltpu.VMEM((2 * tm, d), F32), pltpu.SemaphoreType.DMA(())],
        compiler_params=_cparams(("arbitrary",)),
        name="moe_combine",
    )(pos_tiles, xa, gate, y_sorted)


def moe_dispatch(route, tm_ffn, tm_comb):
    ta = route.shape[0]
    ids = route[:, 0:TOP_K].astype(jnp.int32)
    wts = route[:, TOP_K:2 * TOP_K]
    flat_e = ids.reshape(-1)
    n2 = flat_e.shape[0]
    onehot = (flat_e[:, None] == jnp.arange(N_EXPERTS, dtype=jnp.int32)[None, :]).astype(jnp.int32)
    csum = jnp.cumsum(onehot, axis=0)
    rank = jnp.sum((csum - onehot) * onehot, axis=1)
    counts = csum[-1]
    padded = ((counts + tm_ffn - 1) // tm_ffn) * tm_ffn
    ends = jnp.cumsum(padded)
    starts = ends - padded
    pos = starts[flat_e] + rank
    n_tiles = (n2 + N_EXPERTS * (tm_ffn - 1)) // tm_ffn
    n_pad = n_tiles * tm_ffn
    src_tok = jnp.zeros((n_pad,), jnp.int32).at[pos].set(jnp.arange(n2, dtype=jnp.int32) // TOP_K)
    row_w = jnp.zeros((n_pad,), F32).at[pos].set(wts.reshape(-1))
    tile_start = jnp.arange(n_tiles, dtype=jnp.int32) * tm_ffn
    tile_expert = jnp.minimum(jnp.searchsorted(ends, tile_start, side="right"), N_EXPERTS - 1).astype(jnp.int32)
    n_used = (ends[-1] // tm_ffn).astype(jnp.int32).reshape(1)
    pos_tiles = pos.reshape(ta // tm_comb, tm_comb, TOP_K).transpose(0, 2, 1).reshape(ta // tm_comb, 1,
                                                                                       TOP_K * tm_comb)
    return (tile_expert, n_used, src_tok.reshape(n_tiles, 1, tm_ffn), row_w.reshape(n_pad, 1),
            pos_tiles.astype(jnp.int32))


def _seq_flags(t_lat, seq, tc, cseq, tm):
    starts = np.arange(0, t_lat + tc, tm)
    first = np.where(starts < t_lat, starts % seq == 0, (starts - t_lat) % cseq == 0)
    ends = starts + tm
    last = np.where(starts < t_lat, ends % seq == 0, (ends - t_lat) % cseq == 0)
    return jnp.asarray(first.astype(np.int32)), jnp.asarray(last.astype(np.int32))


def kernel(x, c, ctx, c_ctx, w_ada, b_ada, norm_mix, norm_ffn, ev_w_in, ev_q_gain, ev_k_gain, ev_decay_f,
           ev_decay_b, ev_w_out, od_w_in, od_conv, od_a_log_f, od_a_log_b, od_dt_bias_f, od_dt_bias_b,
           od_out_gain, od_w_out, moe_w_group, moe_b_group, moe_w_expert, moe_b_expert, moe_w_gate_up,
           moe_w_down, final_norm_gain):
    nb, seq, d = x.shape
    cseq = ctx.shape[1]
    depth = w_ada.shape[0]
    t_lat = nb * seq
    tc = nb * cseq
    assert nb + 1 <= 8 and seq % cseq == 0 and cseq % RET_CHUNK == 0 and seq % GRID_W == 0

    tm = 512 if tc % 512 == 0 else cseq
    tm_prep = min(256, cseq)
    tq = min(256, cseq)
    tk = min(256, cseq)
    tl = 2 * DN_CHUNK
    tm_ffn = 256
    tm_comb = min(256, cseq)

    xa = jnp.concatenate([x.reshape(t_lat, d), ctx.reshape(tc, d)], axis=0)
    c8 = jnp.zeros((8, d), F32).at[:nb].set(c).at[nb].set(c_ctx)
    mod = adaln(c8, w_ada, b_ada)

    tabs = rope_tables(seq, tm_prep)
    first_flags, last_flags = _seq_flags(t_lat, seq, tc, cseq, tm_prep)
    ret_zero = jnp.zeros((nb, RET_HEADS, RET_DK, RET_DV), F32)
    dn_zero = jnp.zeros((nb, DN_HEADS, DN_DK, DN_DV), F32)

    for layer in range(depth):
        last = layer == depth - 1
        m = mod[layer].reshape(8, 6, 1, d)
        sh1, sc1, g1, sh2, sc2, g2 = (m[:, j] for j in range(6))
        i = layer // 2
        if layer % 2 == 0:
            w_in = ev_w_in[i].astype(BF16)
            p = norm_mod_matmul(xa, norm_mix[layer], sh1, sc1, w_in, tm, seq, nb)
            rq, rk, aq, ak, av = prep_even(p, tabs, ev_q_gain[i], ev_k_gain[i], tm_prep, seq, t_lat)
            dec = jnp.stack([ev_decay_f[i], ev_decay_b[i]]).astype(F32)
            oc, scf, scb = retention(dec, rq, rk, p, ret_zero, ret_zero, nb, cseq, t_lat // cseq)
            ol, _, _ = retention(dec, rq, rk, p, scf, scb, nb, seq, 0)
            kcat = jnp.concatenate([ak[:, :t_lat].reshape(ATT_KV_HEADS, nb, seq, ATT_HD),
                                    ak[:, t_lat:].reshape(ATT_KV_HEADS, nb, cseq, ATT_HD)], axis=2)
            vcat = jnp.concatenate([av[:, :t_lat].reshape(ATT_KV_HEADS, nb, seq, ATT_HD),
                                    av[:, t_lat:].reshape(ATT_KV_HEADS, nb, cseq, ATT_HD)], axis=2)
            lk = seq + cseq
            vtcat = jnp.concatenate([vcat.transpose(0, 1, 3, 2),
                                     jnp.ones((ATT_KV_HEADS, nb, ATT_VT_ROWS - ATT_HD, lk), BF16)], axis=2)
            vtcat = vtcat.reshape(ATT_KV_HEADS, nb, ATT_VT_ROWS, lk // tk, tk).transpose(0, 1, 3, 2, 4)
            att_l = attention(aq, kcat, vtcat, seq, 0, 0, tq, tk)
            att_c = attention(aq, kcat, vtcat, cseq, t_lat // tq, seq // tk, tq, tk)
            o_ret = jnp.concatenate([ol, oc], axis=0)
            o_att = jnp.concatenate([att_l, att_c], axis=0)
            w_out = ev_w_out[i].astype(BF16)
            k1 = RET_HEADS * RET_DV
            xa = outproj_even(o_ret, o_att, w_out[:k1], w_out[k1:], xa, g1, tm, seq, nb)
        else:
            w_in = jnp.pad(od_w_in[i], ((0, 0), (0, ODD_IN_PAD - ODD_IN))).astype(BF16)
            p = norm_mod_matmul(xa, norm_mix[layer], sh1, sc1, w_in, tm, seq, nb)
            zpad = jnp.zeros((LANES - 2 * DN_HEADS,), F32)
            arow = jnp.concatenate([od_a_log_f[i], od_a_log_b[i], zpad]).reshape(1, LANES).astype(F32)
            brow = jnp.concatenate([od_dt_bias_f[i], od_dt_bias_b[i], zpad]).reshape(1, LANES).astype(F32)
            q, k, v, gb = prep_odd(p, od_conv[i].astype(F32), arow, brow, first_flags, last_flags, tm_prep)
            ta = t_lat + tc
            gbt = gb.reshape(ta // DN_CHUNK, DN_CHUNK, LANES)[:, :, :4 * DN_HEADS].transpose(0, 2, 1)
            oc_f, sc_f = deltanet(q, k, v, gb, gbt, dn_zero, nb, cseq, t_lat, False, tl)
            oc_b, sc_b = deltanet(q, k, v, gb, gbt, dn_zero, nb, cseq, t_lat, True, tl)
            ol_f, _ = deltanet(q, k, v, gb, gbt, sc_f, nb, seq, 0, False, tl)
            ol_b, _ = deltanet(q, k, v, gb, gbt, sc_b, nb, seq, 0, True, tl)
            o_f = jnp.concatenate([ol_f, oc_f], axis=0)
            o_b = jnp.concatenate([ol_b, oc_b], axis=0)
            xa = outproj_odd(o_f, o_b, p, od_out_gain[i], od_w_out[i].astype(BF16), xa, g1, tm, seq, nb)

        w_router = jnp.pad(jnp.concatenate([moe_w_group[layer], moe_w_expert[layer]], axis=1),
                           ((0, 0), (0, LANES - N_GROUPS - N_EXPERTS))).astype(BF16)
        b_router = jnp.pad(jnp.concatenate([moe_b_group[layer], moe_b_expert[layer]]),
                           (0, LANES - N_GROUPS - N_EXPERTS)).reshape(1, LANES).astype(F32)
        f, route = moe_router(xa, norm_ffn[layer], sh2, sc2, w_router, b_router, tm, seq, nb)
        tile_expert, n_used, src_tok, row_w, pos_tiles = moe_dispatch(route, tm_ffn, tm_comb)
        y_sorted = moe_ffn(tile_expert, n_used, src_tok, row_w, moe_w_gate_up[layer], moe_w_down[layer], f, tm_ffn)
        xa = moe_combine(pos_tiles, xa, g2, y_sorted, tm_comb, seq, nb)

    out = final_norm(xa, final_norm_gain, t_lat, tm)
    return out.reshape(nb, seq, d)
```

```python
import functools
import math

import numpy as np
import jax
import jax.numpy as jnp
from jax import lax
from jax.experimental import pallas as pl
from jax.experimental.pallas import tpu as pltpu

F32 = jnp.float32
BF16 = jnp.bfloat16
HIGHEST = lax.Precision.HIGHEST

EPS = 1e-6
GRID_W = 64
ROPE_BASE = 10000.0
RET_HEADS, RET_DK, RET_DV, RET_CHUNK = 8, 64, 128, 128
ATT_HEADS, ATT_KV_HEADS, ATT_HD = 8, 2, 64
DN_HEADS, DN_DK, DN_DV, DN_CHUNK, DN_CONV = 8, 128, 128, 64, 3
N_GROUPS, EXPERTS_PER_GROUP, TOP_K = 4, 8, 2
N_EXPERTS = N_GROUPS * EXPERTS_PER_GROUP

EVEN_IN = 2 * RET_HEADS * RET_DK + 2 * RET_HEADS * RET_DV + (ATT_HEADS + 2 * ATT_KV_HEADS) * ATT_HD
EVEN_ATT_COL = 2 * RET_HEADS * RET_DK + 2 * RET_HEADS * RET_DV
EVEN_ATT_W = (ATT_HEADS + 2 * ATT_KV_HEADS) * ATT_HD
DN_QKV = 2 * DN_HEADS * DN_DK + DN_HEADS * DN_DV
ODD_IN = DN_QKV + DN_HEADS * DN_DV + 4 * DN_HEADS
ODD_IN_PAD = ((ODD_IN + 127) // 128) * 128

LANES = 128
VMEM_LIMIT = 56 * 1024 * 1024

NT_DIMS = (((1,), (1,)), ((), ()))
TN_DIMS = (((0,), (0,)), ((), ()))


def _cparams(sem):
    return pltpu.CompilerParams(dimension_semantics=sem, vmem_limit_bytes=VMEM_LIMIT)


def _silu(x):
    return x / (1.0 + jnp.exp(-x))


def _dot(a, b):
    return jnp.dot(a, b, preferred_element_type=F32)


def _adaln_kernel(c_ref, w_ref, b_ref, o_ref):
    s = _silu(c_ref[...])
    o_ref[...] = _dot(s.astype(BF16), w_ref[...].astype(BF16)) + b_ref[...]


def adaln(c8, w_ada, b_ada):
    depth, d, n6 = w_ada.shape
    tn = min(n6, 1536)
    return pl.pallas_call(
        _adaln_kernel,
        grid=(depth, n6 // tn),
        in_specs=[
            pl.BlockSpec((8, d), lambda l, j: (0, 0)),
            pl.BlockSpec((None, d, tn), lambda l, j: (l, 0, j)),
            pl.BlockSpec((None, 1, tn), lambda l, j: (l, 0, j)),
        ],
        out_specs=pl.BlockSpec((None, 8, tn), lambda l, j: (l, 0, j)),
        out_shape=jax.ShapeDtypeStruct((depth, 8, n6), F32),
        compiler_params=_cparams(("arbitrary", "arbitrary")),
        name="adaln",
    )(c8, w_ada, b_ada.reshape(depth, 1, n6))


def _norm_mod(x, gain, shift, scale):
    ms = jnp.mean(x * x, axis=-1, keepdims=True)
    h = x * lax.rsqrt(ms + EPS) * gain
    return h * (1.0 + scale) + shift


def _nmm_kernel(x_ref, g_ref, sh_ref, sc_ref, w_ref, o_ref, *, nchunk):
    hb = _norm_mod(x_ref[...], g_ref[...], sh_ref[...], sc_ref[...]).astype(BF16)
    n = o_ref.shape[-1]
    for n0 in range(0, n, nchunk):
        o_ref[:, n0:n0 + nchunk] = _dot(hb, w_ref[:, n0:n0 + nchunk]).astype(o_ref.dtype)


def _mod_row_map(tm, seq, n_lat_batches):
    return lambda i: (jnp.minimum((i * tm) // seq, n_lat_batches), 0, 0)


def norm_mod_matmul(xa, gain, shift, scale, w, tm, seq, nb):
    ta, d = xa.shape
    n = w.shape[1]
    nchunk = 512 if n % 512 == 0 else 384
    mrow = _mod_row_map(tm, seq, nb)
    return pl.pallas_call(
        functools.partial(_nmm_kernel, nchunk=nchunk),
        grid=(ta // tm,),
        in_specs=[
            pl.BlockSpec((tm, d), lambda i: (i, 0)),
            pl.BlockSpec((1, d), lambda i: (0, 0)),
            pl.BlockSpec((None, 1, d), mrow),
            pl.BlockSpec((None, 1, d), mrow),
            pl.BlockSpec((d, n), lambda i: (0, 0)),
        ],
        out_specs=pl.BlockSpec((tm, n), lambda i: (i, 0)),
        out_shape=jax.ShapeDtypeStruct((ta, n), BF16),
        compiler_params=_cparams(("arbitrary",)),
        name="norm_mod_matmul",
    )(xa, gain.reshape(1, d), shift, scale, w)


def _final_norm_kernel(x_ref, g_ref, o_ref):
    x = x_ref[...]
    ms = jnp.mean(x * x, axis=-1, keepdims=True)
    o_ref[...] = x * lax.rsqrt(ms + EPS) * g_ref[...]


def final_norm(xa, gain, t_rows, tm):
    d = xa.shape[1]
    return pl.pallas_call(
        _final_norm_kernel,
        grid=(t_rows // tm,),
        in_specs=[pl.BlockSpec((tm, d), lambda i: (i, 0)), pl.BlockSpec((1, d), lambda i: (0, 0))],
        out_specs=pl.BlockSpec((tm, d), lambda i: (i, 0)),
        out_shape=jax.ShapeDtypeStruct((t_rows, d), F32),
        compiler_params=_cparams(("arbitrary",)),
        name="final_norm",
    )(xa, gain.reshape(1, d))


def _prep_even_kernel(qk_ref, att_ref, cos_ref, s1_ref, s2_ref, qg_ref, kg_ref, bd_ref,
                      rq_ref, rk_ref, aq_ref, ak_ref, av_ref):
    cos = cos_ref[...]
    s1 = s1_ref[...]
    s2 = s2_ref[...]
    bd = bd_ref[...]
    half = ATT_HD

    def rope(x):
        return x * cos + pltpu.roll(x, LANES - 16, 1) * s1 + pltpu.roll(x, 16, 1) * s2

    def head_norm(x, gain):
        ms = jnp.dot(x * x, bd, preferred_element_type=F32, precision=HIGHEST)
        return x * lax.rsqrt(ms + EPS) * gain

    nq = RET_HEADS * RET_DK // LANES
    for j in range(nq):
        cs = slice(j * LANES, (j + 1) * LANES)
        rq_ref[:, cs] = rope(qk_ref[:, cs].astype(F32)).astype(BF16)
        ks = slice(nq * LANES + j * LANES, nq * LANES + (j + 1) * LANES)
        rk_ref[:, cs] = (rope(qk_ref[:, ks].astype(F32)) * RET_DK ** -0.5).astype(BF16)

    qg = qg_ref[...]
    kg = kg_ref[...]
    for j in range(ATT_HEADS * ATT_HD // LANES):
        x = att_ref[:, j * LANES:(j + 1) * LANES].astype(F32)
        y = (rope(head_norm(x, qg)) * (ATT_HD ** -0.5 * math.log2(math.e))).astype(BF16)
        aq_ref[2 * j] = y[:, :half]
        aq_ref[2 * j + 1] = y[:, half:]
    c0 = ATT_HEADS * ATT_HD
    y = rope(head_norm(att_ref[:, c0:c0 + LANES].astype(F32), kg)).astype(BF16)
    ak_ref[0] = y[:, :half]
    ak_ref[1] = y[:, half:]
    v = att_ref[:, c0 + LANES:c0 + 2 * LANES]
    av_ref[0] = v[:, :half]
    av_ref[1] = v[:, half:]


def prep_even(p, tabs, q_gain, k_gain, tm, seq, t_lat):
    ta = p.shape[0]
    cos_t, s1_t, s2_t = tabs
    n_tab = seq // tm

    def tab_map(i):
        r = i * tm
        return (jnp.where(r < t_lat, (r % seq) // tm, n_tab), 0)

    ii = np.arange(LANES)
    bd = jnp.asarray((ii[:, None] // ATT_HD == ii[None, :] // ATT_HD).astype(np.float32) / ATT_HD)
    qg = jnp.tile(q_gain.astype(F32), LANES // ATT_HD).reshape(1, LANES)
    kg = jnp.tile(k_gain.astype(F32), LANES // ATT_HD).reshape(1, LANES)
    qkw = 2 * RET_HEADS * RET_DK
    tab_spec = pl.BlockSpec((tm, LANES), tab_map)
    one = lambda i: (0, 0)
    return pl.pallas_call(
        _prep_even_kernel,
        grid=(ta // tm,),
        in_specs=[
            pl.BlockSpec((tm, qkw), lambda i: (i, 0)),
            pl.BlockSpec((tm, EVEN_ATT_W), lambda i: (i, EVEN_ATT_COL // EVEN_ATT_W)),
            tab_spec, tab_spec, tab_spec,
            pl.BlockSpec((1, LANES), one), pl.BlockSpec((1, LANES), one),
            pl.BlockSpec((LANES, LANES), one),
        ],
        out_specs=[
            pl.BlockSpec((tm, RET_HEADS * RET_DK), lambda i: (i, 0)),
            pl.BlockSpec((tm, RET_HEADS * RET_DK), lambda i: (i, 0)),
            pl.BlockSpec((ATT_HEADS, tm, ATT_HD), lambda i: (0, i, 0)),
            pl.BlockSpec((ATT_KV_HEADS, tm, ATT_HD), lambda i: (0, i, 0)),
            pl.BlockSpec((ATT_KV_HEADS, tm, ATT_HD), lambda i: (0, i, 0)),
        ],
        out_shape=[
            jax.ShapeDtypeStruct((ta, RET_HEADS * RET_DK), BF16),
            jax.ShapeDtypeStruct((ta, RET_HEADS * RET_DK), BF16),
            jax.ShapeDtypeStruct((ATT_HEADS, ta, ATT_HD), BF16),
            jax.ShapeDtypeStruct((ATT_KV_HEADS, ta, ATT_HD), BF16),
            jax.ShapeDtypeStruct((ATT_KV_HEADS, ta, ATT_HD), BF16),
        ],
        compiler_params=_cparams(("arbitrary",)),
        name="prep_even",
    )(p, p, cos_t, s1_t, s2_t, qg, kg, bd)


def rope_tables(seq, tm):
    nf = ATT_HD // 4
    t = jnp.arange(seq)
    rows = (t // GRID_W).astype(F32)
    cols = (t % GRID_W).astype(F32)
    inv = ROPE_BASE ** (-jnp.arange(nf, dtype=F32) / nf)
    lane = np.arange(LANES)
    axis = (lane % ATT_HD) // (ATT_HD // 2)
    f = lane % nf
    upper = ((lane % (ATT_HD // 2)) >= nf)
    pos = jnp.where(jnp.asarray(axis)[None, :] == 0, rows[:, None], cols[:, None])
    ang = pos * inv[jnp.asarray(f)][None, :]
    cos = jnp.cos(ang)
    sin = jnp.sin(ang)
    s1 = jnp.where(jnp.asarray(upper)[None, :], 0.0, -sin)
    s2 = jnp.where(jnp.asarray(upper)[None, :], sin, 0.0)
    pad1 = jnp.ones((tm, LANES), F32)
    pad0 = jnp.zeros((tm, LANES), F32)
    return (jnp.concatenate([cos, pad1]), jnp.concatenate([s1, pad0]), jnp.concatenate([s2, pad0]))


def _retention_kernel(dec_ref, q_ref, k_ref, v_ref, g_ref, s0f_ref, s0b_ref,
                      o_ref, sff_ref, sfb_ref, st_ref, *, n_chunks, unroll):
    hp = pl.program_id(1)
    C = RET_CHUNK
    dk, dv = RET_DK, RET_DV
    pos = lax.broadcasted_iota(jnp.int32, (C, dk), 0).astype(F32)
    ii = lax.broadcasted_iota(jnp.int32, (C, C), 0)
    jj = lax.broadcasted_iota(jnp.int32, (C, C), 1)
    dpos = (ii - jj).astype(F32)
    heads = range(2)
    qs = [slice(hh * dk, (hh + 1) * dk) for hh in heads]
    vs = [slice(hh * dv, (hh + 1) * dv) for hh in heads]
    w_out, w_in, gcf, gcb, mask = [], [], [], [], []
    for hh in heads:
        h = 2 * hp + hh
        df = dec_ref[0, h]
        db = dec_ref[1, h]
        lf = -jnp.exp(jnp.full((C, C), df, F32))
        lb = -jnp.exp(jnp.full((C, C), db, F32))
        lfk = -jnp.exp(jnp.full((C, dk), df, F32))
        lbk = -jnp.exp(jnp.full((C, dk), db, F32))
        w_out.append(jnp.concatenate([jnp.exp(lfk * (C - 1.0 - pos)), jnp.exp(lbk * pos)], axis=1))
        w_in.append(jnp.concatenate([jnp.exp(lfk * (pos + 1.0)), jnp.exp(lbk * (C - pos))], axis=1))
        gcf.append(jnp.exp(-jnp.exp(jnp.full((dk, dv), df, F32)) * C))
        gcb.append(jnp.exp(-jnp.exp(jnp.full((dk, dv), db, F32)) * C))
        mask.append(jnp.where(dpos > 0, jnp.exp(lf * jnp.maximum(dpos, 0.0)),
                              jnp.where(dpos < 0, jnp.exp(lb * jnp.maximum(-dpos, 0.0)), 2.0)))

    def rows(n):
        return pl.ds(pl.multiple_of(n * C, C), C)

    items = [(u, hh) for u in range(unroll) for hh in heads]

    def sums_body(i, carry):
        kk = {}
        for u, hh in items:
            k = k_ref[rows(i * unroll + u), qs[hh]].astype(F32)
            kk[(u, hh)] = (jnp.concatenate([k, k], axis=1) * w_out[hh]).astype(BF16)
        kv = {(u, hh): lax.dot_general(kk[(u, hh)], v_ref[rows(i * unroll + u), vs[hh]], TN_DIMS,
                                       preferred_element_type=F32) for u, hh in items}
        for u, hh in items:
            st_ref[hh, i * unroll + u] = kv[(u, hh)]
        return carry

    lax.fori_loop(0, n_chunks // unroll, sums_body, 0)

    def scan_body(n, carry):
        n_rev = n_chunks - 1 - n
        out = []
        for hh in heads:
            sf, sb = carry[2 * hh], carry[2 * hh + 1]
            kvf = st_ref[hh, n, 0:dk, :]
            kvb = st_ref[hh, n_rev, dk:2 * dk, :]
            st_ref[hh, n, 0:dk, :] = sf
            st_ref[hh, n_rev, dk:2 * dk, :] = sb
            out += [gcf[hh] * sf + kvf, gcb[hh] * sb + kvb]
        return tuple(out)

    init = tuple(x for hh in heads for x in (s0f_ref[hh], s0b_ref[hh]))
    fin = lax.fori_loop(0, n_chunks, scan_body, init)
    for hh in heads:
        sff_ref[hh] = fin[2 * hh]
        sfb_ref[hh] = fin[2 * hh + 1]

    def out_body(i, carry):
        ns = [i * unroll + u for u in range(unroll)]
        qb = {(u, hh): q_ref[rows(ns[u]), qs[hh]] for u, hh in items}
        sc = {(u, hh): lax.dot_general(qb[(u, hh)], k_ref[rows(ns[u]), qs[hh]], NT_DIMS,
                                       preferred_element_type=F32) for u, hh in items}
        qw = {}
        for it in items:
            q = qb[it].astype(F32)
            qw[it] = (jnp.concatenate([q, q], axis=1) * w_in[it[1]]).astype(BF16)
        o1 = {(u, hh): _dot((sc[(u, hh)] * mask[hh]).astype(BF16), v_ref[rows(ns[u]), vs[hh]]) for u, hh in items}
        o2 = {(u, hh): _dot(qw[(u, hh)], st_ref[hh, ns[u]].astype(BF16)) for u, hh in items}
        for it in items:
            u, hh = it
            n = ns[u]
            o = o1[it] + o2[it]
            o = o * lax.rsqrt(jnp.mean(o * o, axis=-1, keepdims=True) + EPS)
            gate = g_ref[rows(n), vs[hh]].astype(F32)
            o_ref[rows(n), vs[hh]] = (_silu(gate) * o).astype(o_ref.dtype)
        return carry

    lax.fori_loop(0, n_chunks // unroll, out_body, 0)


def retention(dec, rq, rk, p, s0f, s0b, nb, seq, row_off_blocks):
    n_chunks = seq // RET_CHUNK
    hp_n = RET_HEADS // 2
    vcol = RET_HEADS * RET_DK * 2 // (2 * RET_DV)
    gcol = vcol + RET_HEADS * RET_DV // (2 * RET_DV)
    ta = rq.shape[0]
    st_spec = pl.BlockSpec((None, 2, RET_DK, RET_DV), lambda b, hp, *_: (b, hp, 0, 0))
    grid_spec = pltpu.PrefetchScalarGridSpec(
        num_scalar_prefetch=1,
        grid=(nb, hp_n),
        in_specs=[
            pl.BlockSpec((seq, 2 * RET_DK), lambda b, hp, *_: (row_off_blocks + b, hp)),
            pl.BlockSpec((seq, 2 * RET_DK), lambda b, hp, *_: (row_off_blocks + b, hp)),
            pl.BlockSpec((seq, 2 * RET_DV), lambda b, hp, *_: (row_off_blocks + b, vcol + hp)),
            pl.BlockSpec((seq, 2 * RET_DV), lambda b, hp, *_: (row_off_blocks + b, gcol + hp)),
            st_spec, st_spec,
        ],
        out_specs=[
            pl.BlockSpec((seq, 2 * RET_DV), lambda b, hp, *_: (b, hp)),
            st_spec, st_spec,
        ],
        scratch_shapes=[pltpu.VMEM((2, n_chunks, 2 * RET_DK, RET_DV), F32)],
    )
    st_shape = jax.ShapeDtypeStruct((nb, RET_HEADS, RET_DK, RET_DV), F32)
    return pl.pallas_call(
        functools.partial(_retention_kernel, n_chunks=n_chunks, unroll=math.gcd(n_chunks, 4)),
        grid_spec=grid_spec,
        out_shape=[jax.ShapeDtypeStruct((nb * seq, RET_HEADS * RET_DV), BF16), st_shape, st_shape],
        compiler_params=_cparams(("arbitrary", "arbitrary")),
        name="retention",
    )(dec, rq, rk, p, p, s0f, s0b)


ATT_VT_ROWS = ATT_HD + 16


def _attn_kernel(q_ref, k_ref, vt_ref, o_ref, *s_refs, tk, c_start, c_end, rep):
    tq = q_ref.shape[1]
    sets = (s_refs[:rep], s_refs[rep:])

    def scores(bufs, j):
        c0 = pl.multiple_of(j * tk, tk)
        k = k_ref[pl.ds(c0, tk), :]
        mxs = []
        for r in range(rep):
            s = lax.dot_general(k, q_ref[r], NT_DIMS, preferred_element_type=F32)
            bufs[r][...] = s
            mxs.append(jnp.max(s, axis=0, keepdims=True))
        return tuple(mxs)

    def softmax_pv(bufs, j, mxs, ms, accs):
        vt = vt_ref[j]
        new_m, new_acc = [], []
        for r in range(rep):
            m_new = jnp.maximum(ms[r], mxs[r])
            a = jnp.exp2(ms[r] - m_new)
            p = jnp.exp2(bufs[r][...] - m_new).astype(BF16)
            new_acc.append(a * accs[r] + _dot(vt, p))
            new_m.append(m_new)
        return tuple(new_m), tuple(new_acc)

    n = c_end - c_start
    last = c_end - 1

    def pair_body(t, carry):
        mx0, ms, accs = carry
        j = c_start + 2 * t
        mx1 = scores(sets[1], j + 1)
        ms, accs = softmax_pv(sets[0], j, mx0, ms, accs)
        mx0 = scores(sets[0], jnp.minimum(j + 2, last))
        ms, accs = softmax_pv(sets[1], j + 1, mx1, ms, accs)
        return mx0, ms, accs

    ms = tuple(jnp.full((1, tq), -1e30, F32) for _ in range(rep))
    accs = tuple(jnp.zeros((ATT_VT_ROWS, tq), F32) for _ in range(rep))
    mx0 = scores(sets[0], c_start)
    if n // 2:
        mx0, ms, accs = lax.fori_loop(0, n // 2, pair_body, (mx0, ms, accs))
    if n % 2:
        ms, accs = softmax_pv(sets[0], last, mx0, ms, accs)
    outs = [(acc[:ATT_HD, :] / acc[ATT_HD:ATT_HD + 1, :]).T for acc in accs]
    o_ref[...] = jnp.concatenate(outs, axis=-1).astype(o_ref.dtype)


def attention(aq, kcat, vtcat, seq_q, q_off_blocks, c_start, tq, tk):
    rep = ATT_HEADS // ATT_KV_HEADS
    _, nb, lk, _ = kcat.shape
    nq = seq_q // tq
    n_chunks = lk // tk
    return pl.pallas_call(
        functools.partial(_attn_kernel, tk=tk, c_start=c_start, c_end=n_chunks, rep=rep),
        grid=(nb, ATT_KV_HEADS, nq),
        in_specs=[
            pl.BlockSpec((rep, tq, ATT_HD), lambda b, g, i: (g, q_off_blocks + b * nq + i, 0)),
            pl.BlockSpec((None, None, lk, ATT_HD), lambda b, g, i: (g, b, 0, 0)),
            pl.BlockSpec((None, None, n_chunks, ATT_VT_ROWS, tk), lambda b, g, i: (g, b, 0, 0, 0)),
        ],
        out_specs=pl.BlockSpec((tq, rep * ATT_HD), lambda b, g, i: (b * nq + i, g)),
        out_shape=jax.ShapeDtypeStruct((nb * seq_q, ATT_HEADS * ATT_HD), BF16),
        scratch_shapes=[pltpu.VMEM((tk, tq), F32) for _ in range(2 * rep)],
        compiler_params=_cparams(("arbitrary", "arbitrary", "arbitrary")),
        name="attention",
    )(aq, kcat, vtcat)


def _lat_ctx_specs(tm, width, n_lat_tiles):
    return [pl.BlockSpec((tm, width), lambda i: (jnp.minimum(i, n_lat_tiles - 1), 0)),
            pl.BlockSpec((tm, width), lambda i: (jnp.maximum(i - n_lat_tiles, 0), 0))]


def _outproj_even_kernel(r_lat, r_ctx, a_lat, a_ctx, w1_ref, w2_ref, res_ref, gate_ref, o_ref, *, n_lat_tiles):
    is_lat = pl.program_id(0) < n_lat_tiles
    a1 = jnp.where(is_lat, r_lat[...], r_ctx[...])
    a2 = jnp.where(is_lat, a_lat[...], a_ctx[...])
    y = _dot(a1, w1_ref[...]) + _dot(a2, w2_ref[...])
    o_ref[...] = res_ref[...] + gate_ref[...] * y


def outproj_even(ret_lat, ret_ctx, att_lat, att_ctx, w1, w2, xa, gate, tm, seq, nb):
    ta, d = xa.shape
    k1, k2 = w1.shape[0], w2.shape[0]
    n_lat_tiles = ret_lat.shape[0] // tm
    return pl.pallas_call(
        functools.partial(_outproj_even_kernel, n_lat_tiles=n_lat_tiles),
        grid=(ta // tm,),
        in_specs=_lat_ctx_specs(tm, k1, n_lat_tiles) + _lat_ctx_specs(tm, k2, n_lat_tiles) + [
            pl.BlockSpec((k1, d), lambda i: (0, 0)),
            pl.BlockSpec((k2, d), lambda i: (0, 0)),
            pl.BlockSpec((tm, d), lambda i: (i, 0)),
            pl.BlockSpec((None, 1, d), _mod_row_map(tm, seq, nb)),
        ],
        out_specs=pl.BlockSpec((tm, d), lambda i: (i, 0)),
        out_shape=jax.ShapeDtypeStruct((ta, d), F32),
        compiler_params=_cparams(("arbitrary",)),
        name="outproj_even",
    )(ret_lat, ret_ctx, att_lat, att_ctx, w1, w2, xa, gate)


def _outproj_odd_kernel(f_lat, f_ctx, b_lat, b_ctx, z_ref, og_ref, w_ref, res_ref, gate_ref, o_ref,
                        *, n_lat_tiles):
    is_lat = pl.program_id(0) < n_lat_tiles
    og = og_ref[...]
    parts = []
    for h in range(DN_HEADS):
        cs = slice(h * DN_DV, (h + 1) * DN_DV)
        of = jnp.where(is_lat, f_lat[:, cs], f_ctx[:, cs]).astype(F32)
        ob = jnp.where(is_lat, b_lat[:, cs], b_ctx[:, cs]).astype(F32)
        o = of + ob
        o = o * lax.rsqrt(jnp.mean(o * o, axis=-1, keepdims=True) + EPS) * og
        parts.append((o * _silu(z_ref[:, cs].astype(F32))).astype(BF16))
    a = jnp.concatenate(parts, axis=-1)
    o_ref[...] = res_ref[...] + gate_ref[...] * _dot(a, w_ref[...])


def outproj_odd(of_lat, of_ctx, ob_lat, ob_ctx, p, out_gain, w, xa, gate, tm, seq, nb):
    ta, d = xa.shape
    kdim = DN_HEADS * DN_DV
    n_lat_tiles = of_lat.shape[0] // tm
    return pl.pallas_call(
        functools.partial(_outproj_odd_kernel, n_lat_tiles=n_lat_tiles),
        grid=(ta // tm,),
        in_specs=_lat_ctx_specs(tm, kdim, n_lat_tiles) + _lat_ctx_specs(tm, kdim, n_lat_tiles) + [
            pl.BlockSpec((tm, kdim), lambda i: (i, DN_QKV // kdim)),
            pl.BlockSpec((1, DN_DV), lambda i: (0, 0)),
            pl.BlockSpec((kdim, d), lambda i: (0, 0)),
            pl.BlockSpec((tm, d), lambda i: (i, 0)),
            pl.BlockSpec((None, 1, d), _mod_row_map(tm, seq, nb)),
        ],
        out_specs=pl.BlockSpec((tm, d), lambda i: (i, 0)),
        out_shape=jax.ShapeDtypeStruct((ta, d), F32),
        compiler_params=_cparams(("arbitrary",)),
        name="outproj_odd",
    )(of_lat, of_ctx, ob_lat, ob_ctx, p, out_gain.reshape(1, DN_DV).astype(F32), w, xa, gate)


def _prep_odd_kernel(first_ref, last_ref, x_ref, prev_ref, next_ref, ab_ref, cw_ref, arow_ref, brow_ref,
                     q_ref, k_ref, v_ref, gb_ref):
    i = pl.program_id(0)
    tm = x_ref.shape[0]
    hrows = prev_ref.shape[0]
    keep_prev = 1.0 - first_ref[i].astype(F32)
    keep_next = 1.0 - last_ref[i].astype(F32)
    row = lax.broadcasted_iota(jnp.int32, (tm, LANES), 0)
    is_first = row == 0
    is_last = row == tm - 1
    n_qk = 2 * DN_HEADS * DN_DK // LANES
    n_q = DN_HEADS * DN_DK // LANES
    outs = (q_ref, k_ref, v_ref)
    for j in range(DN_QKV // LANES):
        cs = slice(j * LANES, (j + 1) * LANES)
        x = x_ref[:, cs].astype(F32)
        xp = prev_ref[:, cs].astype(F32)[hrows - 1:hrows, :] * keep_prev
        xn = next_ref[:, cs].astype(F32)[0:1, :] * keep_next
        x_dn = jnp.where(is_first, xp, pltpu.roll(x, 1, 0))
        x_up = jnp.where(is_last, xn, pltpu.roll(x, tm - 1, 0))
        w = cw_ref[:, cs]
        y = _silu(x_dn * w[0:1, :] + x * w[1:2, :] + x_up * w[2:3, :])
        if j < n_qk:
            y = y * lax.rsqrt(jnp.sum(y * y, axis=-1, keepdims=True) + EPS)
            if j < n_q:
                y = y * DN_DK ** -0.5
        lj = j % n_q
        outs[j // n_q][:, lj * LANES:(lj + 1) * LANES] = y.astype(BF16)

    a = ab_ref[...].astype(F32)
    lane = lax.broadcasted_iota(jnp.int32, (tm, LANES), 1)
    z = a + brow_ref[...]
    softplus = jnp.maximum(z, 0.0) + jnp.log(1.0 + jnp.exp(-jnp.abs(z)))
    g = -jnp.exp(arow_ref[...]) * softplus
    beta = 1.0 / (1.0 + jnp.exp(-a))
    gb_ref[...] = jnp.where(lane < 2 * DN_HEADS, g, jnp.where(lane < 4 * DN_HEADS, beta, 0.0))


def prep_odd(p, conv_w, arow, brow, first_flags, last_flags, tm):
    ta = p.shape[0]
    halo = 16
    hb = tm // halo
    n_h = ta // halo
    kdim = DN_HEADS * DN_DK
    grid_spec = pltpu.PrefetchScalarGridSpec(
        num_scalar_prefetch=2,
        grid=(ta // tm,),
        in_specs=[
            pl.BlockSpec((tm, DN_QKV), lambda i, *_: (i, 0)),
            pl.BlockSpec((halo, DN_QKV), lambda i, *_: (jnp.maximum(i * hb - 1, 0), 0)),
            pl.BlockSpec((halo, DN_QKV), lambda i, *_: (jnp.minimum((i + 1) * hb, n_h - 1), 0)),
            pl.BlockSpec((tm, LANES), lambda i, *_: (i, (DN_QKV + DN_HEADS * DN_DV) // LANES)),
            pl.BlockSpec((DN_CONV, DN_QKV), lambda i, *_: (0, 0)),
            pl.BlockSpec((1, LANES), lambda i, *_: (0, 0)),
            pl.BlockSpec((1, LANES), lambda i, *_: (0, 0)),
        ],
        out_specs=[
            pl.BlockSpec((tm, kdim), lambda i, *_: (i, 0)),
            pl.BlockSpec((tm, kdim), lambda i, *_: (i, 0)),
            pl.BlockSpec((tm, kdim), lambda i, *_: (i, 0)),
            pl.BlockSpec((tm, LANES), lambda i, *_: (i, 0)),
        ],
    )
    return pl.pallas_call(
        _prep_odd_kernel,
        grid_spec=grid_spec,
        out_shape=[
            jax.ShapeDtypeStruct((ta, kdim), BF16),
            jax.ShapeDtypeStruct((ta, kdim), BF16),
            jax.ShapeDtypeStruct((ta, kdim), BF16),
            jax.ShapeDtypeStruct((ta, LANES), F32),
        ],
        compiler_params=_cparams(("arbitrary",)),
        name="prep_odd",
    )(first_flags, last_flags, p, p, p, p, conv_w, arow, brow)


def _deltanet_kernel(q_ref, k_ref, v_ref, gb_ref, gbt_ref, s0_ref, o_ref, sf_ref, s_ref,
                     *, reverse, n_chunks, dir_off):
    t = pl.program_id(1)

    @pl.when(t == 0)
    def _():
        s_ref[...] = s0_ref[...]

    C = DN_CHUNK
    ii = lax.broadcasted_iota(jnp.int32, (C, C), 0)
    jj = lax.broadcasted_iota(jnp.int32, (C, C), 1)
    if reverse:
        incl = ii <= jj
        strict = ii < jj
    else:
        incl = ii >= jj
        strict = ii > jj
    tri = jnp.where(incl, 1.0, 0.0).astype(F32)
    if reverse:
        tri_t = jnp.where(ii >= jj, 1.0, 0.0).astype(F32)
    else:
        tri_t = jnp.where(ii <= jj, 1.0, 0.0).astype(F32)
    eye = jnp.where(ii == jj, 1.0, 0.0).astype(F32)
    blk = ii ^ jj

    order = list(range(n_chunks - 1, -1, -1) if reverse else range(n_chunks))
    items = [(c, h) for c in order for h in range(DN_HEADS)]
    gcols, grows, gbs = {}, {}, {}
    for c in order:
        gb_c = gb_ref[c * C:(c + 1) * C, :]
        gbs[c] = gb_c
        gcols[c] = jnp.dot(tri, gb_c, preferred_element_type=F32, precision=HIGHEST)
        grows[c] = jnp.dot(gbt_ref[c], tri_t, preferred_element_type=F32, precision=HIGHEST)

    qb, kb16, decay, kbeta, egc, kd, gl, rhs = {}, {}, {}, {}, {}, {}, {}, {}
    for it in items:
        c, h = it
        gi = dir_off + h
        bi = 2 * DN_HEADS + dir_off + h
        rows = slice(c * C, (c + 1) * C)
        cs = slice(h * DN_DK, (h + 1) * DN_DK)
        gc = gcols[c][:, gi:gi + 1]
        gr = grows[c][gi:gi + 1, :]
        beta = gbs[c][:, bi:bi + 1]
        qb[it] = q_ref[rows, cs]
        kb16[it] = k_ref[rows, cs]
        kf = kb16[it].astype(F32)
        decay[it] = jnp.where(incl, jnp.exp(jnp.where(incl, gc - gr, 0.0)), 0.0)
        kbeta[it] = kf * beta
        egc[it] = jnp.exp(gc)
        glast = gc[0:1, :] if reverse else gc[C - 1:C, :]
        kd[it] = (kf * jnp.exp(glast - gc)).astype(BF16)
        gl[it] = jnp.exp(glast)
        rhs[it] = jnp.concatenate([v_ref[rows, cs].astype(F32) * beta, kbeta[it] * egc[it]], axis=1).astype(BF16)

    kk = {it: lax.dot_general(kbeta[it].astype(BF16), kb16[it], NT_DIMS, preferred_element_type=F32)
          for it in items}
    qk = {it: lax.dot_general(qb[it], kb16[it], NT_DIMS, preferred_element_type=F32) for it in items}
    lm = {it: jnp.where(strict, kk[it] * decay[it], 0.0) for it in items}
    attn = {it: jnp.where(incl, qk[it] * decay[it], 0.0).astype(BF16) for it in items}
    dinv = {it: eye - jnp.where(blk < 2, lm[it], 0.0) for it in items}
    s = 2
    while s < C:
        in_band = jnp.logical_and(blk >= s, blk < 2 * s)
        tmp = {it: _dot(dinv[it].astype(BF16), jnp.where(in_band, lm[it], 0.0).astype(BF16)) for it in items}
        dinv = {it: dinv[it] - _dot(tmp[it].astype(BF16), dinv[it].astype(BF16)) for it in items}
        s *= 2
    uw = {it: _dot(dinv[it].astype(BF16), rhs[it]) for it in items}
    wq = {it: jnp.concatenate([uw[it][:, DN_DV:], qb[it].astype(F32) * egc[it]], axis=0).astype(BF16)
          for it in items}

    states = [s_ref[h] for h in range(DN_HEADS)]
    for c in order:
        its = [(c, h) for h in range(DN_HEADS)]
        r = {it: _dot(wq[it], states[it[1]].astype(BF16)) for it in its}
        v_new = {it: (uw[it][:, :DN_DV] - r[it][:C]).astype(BF16) for it in its}
        o = {it: r[it][C:] + _dot(attn[it], v_new[it]) for it in its}
        for it in its:
            h = it[1]
            states[h] = states[h] * gl[it] + lax.dot_general(kd[it], v_new[it], TN_DIMS,
                                                             preferred_element_type=F32)
        for it in its:
            h = it[1]
            o_ref[c * C:(c + 1) * C, h * DN_DK:(h + 1) * DN_DK] = o[it].astype(o_ref.dtype)
    for h in range(DN_HEADS):
        s_ref[h] = states[h]

    @pl.when(t == pl.num_programs(1) - 1)
    def _():
        sf_ref[...] = s_ref[...]


def deltanet(q, k, v, gb, gbt, s0, nb, seq, row_off, reverse, tl):
    nblk = seq // tl
    n_chunks = tl // DN_CHUNK
    off_b = row_off // tl
    kdim = DN_HEADS * DN_DK

    def rb(b, t):
        tt = nblk - 1 - t if reverse else t
        return off_b + b * nblk + tt

    seq_spec = pl.BlockSpec((tl, kdim), lambda b, t: (rb(b, t), 0))
    st_spec = pl.BlockSpec((None, DN_HEADS, DN_DK, DN_DV), lambda b, t: (b, 0, 0, 0))
    return pl.pallas_call(
        functools.partial(_deltanet_kernel, reverse=reverse, n_chunks=n_chunks,
                          dir_off=DN_HEADS if reverse else 0),
        grid=(nb, nblk),
        in_specs=[
            seq_spec, seq_spec, seq_spec,
            pl.BlockSpec((tl, LANES), lambda b, t: (rb(b, t), 0)),
            pl.BlockSpec((n_chunks, 4 * DN_HEADS, DN_CHUNK), lambda b, t: (rb(b, t), 0, 0)),
            st_spec,
        ],
        out_specs=[
            pl.BlockSpec((tl, kdim), lambda b, t: (b * nblk + (nblk - 1 - t if reverse else t), 0)),
            st_spec,
        ],
        out_shape=[
            jax.ShapeDtypeStruct((nb * seq, kdim), BF16),
            jax.ShapeDtypeStruct((nb, DN_HEADS, DN_DK, DN_DV), F32),
        ],
        scratch_shapes=[pltpu.VMEM((DN_HEADS, DN_DK, DN_DV), F32)],
        compiler_params=_cparams(("arbitrary", "arbitrary")),
        name="deltanet_bwd" if reverse else "deltanet_fwd",
    )(q, k, v, gb, gbt, s0)


def _router_kernel(x_ref, g_ref, sh_ref, sc_ref, wr_ref, br_ref, ltri_ref, f_ref, r_ref, cnt_ref, base_ref):
    @pl.when(pl.program_id(0) == 0)
    def _():
        base_ref[...] = jnp.zeros_like(base_ref)

    h = _norm_mod(x_ref[...], g_ref[...], sh_ref[...], sc_ref[...])
    f_ref[...] = h
    logits = _dot(h.astype(BF16), wr_ref[...]) + br_ref[...]
    tm = logits.shape[0]
    lane = lax.broadcasted_iota(jnp.int32, (tm, LANES), 1)
    neg = -1e30
    big = 4 * LANES
    is_g = lane < N_GROUPS
    gl = jnp.where(is_g, logits, neg)
    gm = jnp.max(gl, axis=-1, keepdims=True)
    grp = jnp.min(jnp.where(gl == gm, lane, big), axis=-1, keepdims=True)
    psum = jnp.sum(jnp.where(is_g, jnp.exp(gl - gm), 0.0), axis=-1, keepdims=True)
    p_grp = 1.0 / psum
    e_lane = lane - N_GROUPS
    in_grp = jnp.logical_and(jnp.logical_and(e_lane >= 0, e_lane < N_EXPERTS),
                             (e_lane // EXPERTS_PER_GROUP) == grp)
    el = jnp.where(in_grp, logits, neg)
    m1 = jnp.max(el, axis=-1, keepdims=True)
    i1 = jnp.min(jnp.where(el == m1, lane, big), axis=-1, keepdims=True)
    el2 = jnp.where(lane == i1, neg, el)
    m2 = jnp.max(el2, axis=-1, keepdims=True)
    i2 = jnp.min(jnp.where(el2 == m2, lane, big), axis=-1, keepdims=True)
    e21 = jnp.exp(m2 - m1)
    w1 = p_grp / (1.0 + e21)
    w2 = p_grp * e21 / (1.0 + e21)
    e1 = (i1 - N_GROUPS).astype(F32)
    e2 = (i2 - N_GROUPS).astype(F32)
    oh1 = lane == i1
    oh2 = lane == i2
    oh1f = jnp.where(oh1, 1.0, 0.0)
    oh2f = jnp.where(oh2, 1.0, 0.0)
    ltri = ltri_ref[...]
    before1 = _dot(ltri, oh1f.astype(BF16))
    before2 = _dot(ltri, oh2f.astype(BF16))
    cnt1 = jnp.sum(oh1f, axis=0, keepdims=True)
    cnt2 = jnp.sum(oh2f, axis=0, keepdims=True)
    base = base_ref[0:1, :]
    rank1 = jnp.sum(jnp.where(oh1, base + before1, 0.0), axis=-1, keepdims=True)
    rank2 = jnp.sum(jnp.where(oh2, base + cnt1 + before2, 0.0), axis=-1, keepdims=True)
    total = base + cnt1 + cnt2
    base_ref[...] = jnp.broadcast_to(total, base_ref.shape)
    cnt_ref[...] = jnp.broadcast_to(total, cnt_ref.shape)
    vals = (e1, e2, w1, w2, rank1, rank2)
    out = jnp.zeros((tm, LANES), F32)
    for idx, val in enumerate(vals):
        out = jnp.where(lane == idx, val, out)
    r_ref[...] = out


def moe_router(xa, gain, shift, scale, w_router, b_router, tm, seq, nb):
    ta, d = xa.shape
    mrow = _mod_row_map(tm, seq, nb)
    ii = np.arange(tm)
    ltri = jnp.asarray((ii[:, None] > ii[None, :]).astype(np.float32)).astype(BF16)
    return pl.pallas_call(
        _router_kernel,
        grid=(ta // tm,),
        in_specs=[
            pl.BlockSpec((tm, d), lambda i: (i, 0)),
            pl.BlockSpec((1, d), lambda i: (0, 0)),
            pl.BlockSpec((None, 1, d), mrow),
            pl.BlockSpec((None, 1, d), mrow),
            pl.BlockSpec((d, LANES), lambda i: (0, 0)),
            pl.BlockSpec((1, LANES), lambda i: (0, 0)),
            pl.BlockSpec((tm, tm), lambda i: (0, 0)),
        ],
        out_specs=[pl.BlockSpec((tm, d), lambda i: (i, 0)), pl.BlockSpec((tm, LANES), lambda i: (i, 0)),
                   pl.BlockSpec((8, LANES), lambda i: (0, 0))],
        out_shape=[jax.ShapeDtypeStruct((ta, d), F32), jax.ShapeDtypeStruct((ta, LANES), F32),
                   jax.ShapeDtypeStruct((8, LANES), F32)],
        scratch_shapes=[pltpu.VMEM((8, LANES), F32)],
        compiler_params=_cparams(("arbitrary",)),
        name="moe_router",
    )(xa, gain.reshape(1, d), shift, scale, w_router, b_router, ltri)


ROW_DMA_UNROLL = 8


def _issue_row_copies(n_rows, make_copy):
    def trip(i, carry):
        for u in range(ROW_DMA_UNROLL):
            make_copy(i * ROW_DMA_UNROLL + u).start(priority=u % 2)
        return carry

    lax.fori_loop(0, n_rows // ROW_DMA_UNROLL, trip, 0)


def _moe_scatter_kernel(pos_ref, f_ref, xs_in, xs_out, sem):
    del xs_in
    tm = f_ref.shape[0]
    _issue_row_copies(2 * tm, lambda r: pltpu.make_async_copy(
        f_ref.at[pl.ds(r % tm, 1)], xs_out.at[pl.ds(pos_ref[0, 0, r], 1)], sem))
    for _ in range(2):
        pltpu.make_async_copy(f_ref, xs_out.at[pl.ds(0, tm)], sem).wait()


def moe_scatter(pos_tiles, f, xs_zero, tm):
    ta, d = f.shape
    return pl.pallas_call(
        _moe_scatter_kernel,
        grid=(ta // tm,),
        in_specs=[
            pl.BlockSpec((1, 1, 2 * tm), lambda i: (i, 0, 0), memory_space=pltpu.SMEM),
            pl.BlockSpec((tm, d), lambda i: (i, 0)),
            pl.BlockSpec(memory_space=pl.ANY),
        ],
        out_specs=pl.BlockSpec(memory_space=pl.ANY),
        out_shape=jax.ShapeDtypeStruct(xs_zero.shape, xs_zero.dtype),
        scratch_shapes=[pltpu.SemaphoreType.DMA(())],
        input_output_aliases={2: 0},
        compiler_params=_cparams(("arbitrary",)),
        name="moe_scatter",
    )(pos_tiles, f, xs_zero)


def _moe_ffn_kernel(te_ref, nu_ref, x_ref, wgu_ref, wd_ref, o_ref, wgu_bf, wd_bf):
    i = pl.program_id(0)
    fdim = wd_bf.shape[0]

    @pl.when(i < nu_ref[0])
    def _():
        prev = te_ref[jnp.maximum(i - 1, 0)]
        changed = jnp.logical_or(i == 0, te_ref[i] != prev)

        @pl.when(changed)
        def _():
            wgu_bf[...] = wgu_ref[...].astype(BF16)
            wd_bf[...] = wd_ref[...].astype(BF16)

        gu = _dot(x_ref[...].astype(BF16), wgu_bf[...])
        hmid = _silu(gu[:, :fdim]) * gu[:, fdim:]
        o_ref[...] = _dot(hmid.astype(BF16), wd_bf[...])

    @pl.when(i >= nu_ref[0])
    def _():
        o_ref[...] = jnp.zeros_like(o_ref)


def moe_ffn(tile_expert, n_used, xs, w_gate_up, w_down, tm):
    n_pad, d = xs.shape
    _, _, f2 = w_gate_up.shape
    fdim = w_down.shape[1]
    grid_spec = pltpu.PrefetchScalarGridSpec(
        num_scalar_prefetch=2,
        grid=(n_pad // tm,),
        in_specs=[
            pl.BlockSpec((tm, d), lambda i, te, nu: (i, 0)),
            pl.BlockSpec((None, d, f2), lambda i, te, nu: (te[i], 0, 0)),
            pl.BlockSpec((None, fdim, d), lambda i, te, nu: (te[i], 0, 0)),
        ],
        out_specs=pl.BlockSpec((tm, d), lambda i, te, nu: (i, 0)),
        scratch_shapes=[pltpu.VMEM((d, f2), BF16), pltpu.VMEM((fdim, d), BF16)],
    )
    return pl.pallas_call(
        _moe_ffn_kernel,
        grid_spec=grid_spec,
        out_shape=jax.ShapeDtypeStruct((n_pad, d), F32),
        compiler_params=_cparams(("arbitrary",)),
        name="moe_ffn",
    )(tile_expert, n_used, xs, w_gate_up, w_down)


def _moe_combine_kernel(pos_ref, x_ref, gate_ref, r_ref, y_hbm, o_ref, ybuf, sem):
    tm = x_ref.shape[0]
    _issue_row_copies(2 * tm, lambda r: pltpu.make_async_copy(
        y_hbm.at[pl.ds(pos_ref[0, 0, r], 1)], ybuf.at[pl.ds(r, 1)], sem))
    pltpu.make_async_copy(y_hbm.at[pl.ds(0, 2 * tm)], ybuf, sem).wait()
    route = r_ref[...]
    y = route[:, 2:3] * ybuf[0:tm, :] + route[:, 3:4] * ybuf[tm:2 * tm, :]
    o_ref[...] = x_ref[...] + gate_ref[...] * y


def moe_combine(pos_tiles, xa, gate, route, y_sorted, tm, seq, nb):
    ta, d = xa.shape
    return pl.pallas_call(
        _moe_combine_kernel,
        grid=(ta // tm,),
        in_specs=[
            pl.BlockSpec((1, 1, 2 * tm), lambda i: (i, 0, 0), memory_space=pltpu.SMEM),
            pl.BlockSpec((tm, d), lambda i: (i, 0)),
            pl.BlockSpec((None, 1, d), _mod_row_map(tm, seq, nb)),
            pl.BlockSpec((tm, LANES), lambda i: (i, 0)),
            pl.BlockSpec(memory_space=pl.ANY),
        ],
        out_specs=pl.BlockSpec((tm, d), lambda i: (i, 0)),
        out_shape=jax.ShapeDtypeStruct((ta, d), F32),
        scratch_shapes=[pltpu.VMEM((2 * tm, d), F32), pltpu.SemaphoreType.DMA(())],
        compiler_params=_cparams(("arbitrary",)),
        name="moe_combine",
    )(pos_tiles, xa, gate, route, y_sorted)


def moe_slots(route, counts, tm_ffn, tm_tok):
    ta = route.shape[0]
    ids = route[:, 0:TOP_K].astype(jnp.int32)
    rank = route[:, 2 * TOP_K:3 * TOP_K].astype(jnp.int32)
    counts = counts[0, N_GROUPS:N_GROUPS + N_EXPERTS].astype(jnp.int32)
    padded = ((counts + tm_ffn - 1) // tm_ffn) * tm_ffn
    ends = jnp.cumsum(padded)
    starts = ends - padded
    pos = starts[ids] + rank
    n_tiles = (TOP_K * ta + N_EXPERTS * (tm_ffn - 1)) // tm_ffn
    tile_start = jnp.arange(n_tiles, dtype=jnp.int32) * tm_ffn
    tile_expert = jnp.minimum(jnp.searchsorted(ends, tile_start, side="right"), N_EXPERTS - 1).astype(jnp.int32)
    n_used = (ends[-1] // tm_ffn).astype(jnp.int32).reshape(1)
    pos_tiles = pos.reshape(ta // tm_tok, tm_tok, TOP_K).transpose(0, 2, 1).reshape(ta // tm_tok, 1, TOP_K * tm_tok)
    return tile_expert, n_used, n_tiles * tm_ffn, pos_tiles


def _seq_flags(t_lat, seq, tc, cseq, tm):
    starts = np.arange(0, t_lat + tc, tm)
    first = np.where(starts < t_lat, starts % seq == 0, (starts - t_lat) % cseq == 0)
    ends = starts + tm
    last = np.where(starts < t_lat, ends % seq == 0, (ends - t_lat) % cseq == 0)
    return jnp.asarray(first.astype(np.int32)), jnp.asarray(last.astype(np.int32))


def kernel(x, c, ctx, c_ctx, w_ada, b_ada, norm_mix, norm_ffn, ev_w_in, ev_q_gain, ev_k_gain, ev_decay_f,
           ev_decay_b, ev_w_out, od_w_in, od_conv, od_a_log_f, od_a_log_b, od_dt_bias_f, od_dt_bias_b,
           od_out_gain, od_w_out, moe_w_group, moe_b_group, moe_w_expert, moe_b_expert, moe_w_gate_up,
           moe_w_down, final_norm_gain):
    nb, seq, d = x.shape
    cseq = ctx.shape[1]
    depth = w_ada.shape[0]
    t_lat = nb * seq
    tc = nb * cseq
    assert nb + 1 <= 8 and seq % cseq == 0 and cseq % RET_CHUNK == 0 and seq % GRID_W == 0

    tm = 512 if tc % 512 == 0 else cseq
    tm_prep = min(256, cseq)
    tq = min(256, cseq)
    tk = min(256, cseq)
    tl = 2 * DN_CHUNK
    tm_ffn = 256
    tm_comb = min(256, cseq)

    xa = jnp.concatenate([x.reshape(t_lat, d), ctx.reshape(tc, d)], axis=0)
    c8 = jnp.zeros((8, d), F32).at[:nb].set(c).at[nb].set(c_ctx)
    mod = adaln(c8, w_ada, b_ada)

    tabs = rope_tables(seq, tm_prep)
    first_flags, last_flags = _seq_flags(t_lat, seq, tc, cseq, tm_prep)
    ret_zero = jnp.zeros((nb, RET_HEADS, RET_DK, RET_DV), F32)
    dn_zero = jnp.zeros((nb, DN_HEADS, DN_DK, DN_DV), F32)

    for layer in range(depth):
        last = layer == depth - 1
        m = mod[layer].reshape(8, 6, 1, d)
        sh1, sc1, g1, sh2, sc2, g2 = (m[:, j] for j in range(6))
        i = layer // 2
        if layer % 2 == 0:
            w_in = ev_w_in[i].astype(BF16)
            p = norm_mod_matmul(xa, norm_mix[layer], sh1, sc1, w_in, tm, seq, nb)
            rq, rk, aq, ak, av = prep_even(p, tabs, ev_q_gain[i], ev_k_gain[i], tm_prep, seq, t_lat)
            dec = jnp.stack([ev_decay_f[i], ev_decay_b[i]]).astype(F32)
            oc, scf, scb = retention(dec, rq, rk, p, ret_zero, ret_zero, nb, cseq, t_lat // cseq)
            ol, _, _ = retention(dec, rq, rk, p, scf, scb, nb, seq, 0)
            kcat = jnp.concatenate([ak[:, :t_lat].reshape(ATT_KV_HEADS, nb, seq, ATT_HD),
                                    ak[:, t_lat:].reshape(ATT_KV_HEADS, nb, cseq, ATT_HD)], axis=2)
            vcat = jnp.concatenate([av[:, :t_lat].reshape(ATT_KV_HEADS, nb, seq, ATT_HD),
                                    av[:, t_lat:].reshape(ATT_KV_HEADS, nb, cseq, ATT_HD)], axis=2)
            lk = seq + cseq
            vtcat = jnp.concatenate([vcat.transpose(0, 1, 3, 2),
                                     jnp.ones((ATT_KV_HEADS, nb, ATT_VT_ROWS - ATT_HD, lk), BF16)], axis=2)
            vtcat = vtcat.reshape(ATT_KV_HEADS, nb, ATT_VT_ROWS, lk // tk, tk).transpose(0, 1, 3, 2, 4)
            att_l = attention(aq, kcat, vtcat, seq, 0, 0, tq, tk)
            att_c = attention(aq, kcat, vtcat, cseq, t_lat // tq, seq // tk, tq, tk)
            w_out = ev_w_out[i].astype(BF16)
            k1 = RET_HEADS * RET_DV
            xa = outproj_even(ol, oc, att_l, att_c, w_out[:k1], w_out[k1:], xa, g1, tm, seq, nb)
        else:
            w_in = jnp.pad(od_w_in[i], ((0, 0), (0, ODD_IN_PAD - ODD_IN))).astype(BF16)
            p = norm_mod_matmul(xa, norm_mix[layer], sh1, sc1, w_in, tm, seq, nb)
            zpad = jnp.zeros((LANES - 2 * DN_HEADS,), F32)
            arow = jnp.concatenate([od_a_log_f[i], od_a_log_b[i], zpad]).reshape(1, LANES).astype(F32)
            brow = jnp.concatenate([od_dt_bias_f[i], od_dt_bias_b[i], zpad]).reshape(1, LANES).astype(F32)
            q, k, v, gb = prep_odd(p, od_conv[i].astype(F32), arow, brow, first_flags, last_flags, tm_prep)
            ta = t_lat + tc
            gbt = gb.reshape(ta // DN_CHUNK, DN_CHUNK, LANES)[:, :, :4 * DN_HEADS].transpose(0, 2, 1)
            oc_f, sc_f = deltanet(q, k, v, gb, gbt, dn_zero, nb, cseq, t_lat, False, tl)
            oc_b, sc_b = deltanet(q, k, v, gb, gbt, dn_zero, nb, cseq, t_lat, True, tl)
            ol_f, _ = deltanet(q, k, v, gb, gbt, sc_f, nb, seq, 0, False, tl)
            ol_b, _ = deltanet(q, k, v, gb, gbt, sc_b, nb, seq, 0, True, tl)
            xa = outproj_odd(ol_f, oc_f, ol_b, oc_b, p, od_out_gain[i], od_w_out[i].astype(BF16), xa, g1, tm, seq,
                             nb)

        w_router = jnp.pad(jnp.concatenate([moe_w_group[layer], moe_w_expert[layer]], axis=1),
                           ((0, 0), (0, LANES - N_GROUPS - N_EXPERTS))).astype(BF16)
        b_router = jnp.pad(jnp.concatenate([moe_b_group[layer], moe_b_expert[layer]]),
                           (0, LANES - N_GROUPS - N_EXPERTS)).reshape(1, LANES).astype(F32)
        f, route, counts = moe_router(xa, norm_ffn[layer], sh2, sc2, w_router, b_router, tm, seq, nb)
        tile_expert, n_used, n_pad, pos_tiles = moe_slots(route, counts, tm_ffn, tm_comb)
        xs = moe_scatter(pos_tiles, f, jnp.zeros((n_pad, d), F32), tm_comb)
        y_sorted = moe_ffn(tile_expert, n_used, xs, moe_w_gate_up[layer], moe_w_down[layer], tm_ffn)
        xa = moe_combine(pos_tiles, xa, g2, route, y_sorted, tm_comb, seq, nb)

    out = final_norm(xa, final_norm_gain, t_lat, tm)
    return out.reshape(nb, seq, d)
```

```python
import functools
import math

import numpy as np
import jax
import jax.numpy as jnp
from jax import lax
from jax.experimental import pallas as pl
from jax.experimental.pallas import tpu as pltpu

F32 = jnp.float32
BF16 = jnp.bfloat16
HIGHEST = lax.Precision.HIGHEST

EPS = 1e-6
GRID_W = 64
ROPE_BASE = 10000.0
RET_HEADS, RET_DK, RET_DV, RET_CHUNK = 8, 64, 128, 128
ATT_HEADS, ATT_KV_HEADS, ATT_HD = 8, 2, 64
DN_HEADS, DN_DK, DN_DV, DN_CHUNK, DN_CONV = 8, 128, 128, 64, 3
N_GROUPS, EXPERTS_PER_GROUP, TOP_K = 4, 8, 2
N_EXPERTS = N_GROUPS * EXPERTS_PER_GROUP

EVEN_IN = 2 * RET_HEADS * RET_DK + 2 * RET_HEADS * RET_DV + (ATT_HEADS + 2 * ATT_KV_HEADS) * ATT_HD
EVEN_ATT_COL = 2 * RET_HEADS * RET_DK + 2 * RET_HEADS * RET_DV
EVEN_ATT_W = (ATT_HEADS + 2 * ATT_KV_HEADS) * ATT_HD
DN_QKV = 2 * DN_HEADS * DN_DK + DN_HEADS * DN_DV
ODD_IN = DN_QKV + DN_HEADS * DN_DV + 4 * DN_HEADS
ODD_IN_PAD = ((ODD_IN + 127) // 128) * 128

LANES = 128
VMEM_LIMIT = 56 * 1024 * 1024

NT_DIMS = (((1,), (1,)), ((), ()))
TN_DIMS = (((0,), (0,)), ((), ()))


def _cparams(sem):
    return pltpu.CompilerParams(dimension_semantics=sem, vmem_limit_bytes=VMEM_LIMIT)


def _silu(x):
    return x / (1.0 + jnp.exp(-x))


def _dot(a, b):
    return jnp.dot(a, b, preferred_element_type=F32)


def _adaln_kernel(c_ref, w_ref, b_ref, o_ref):
    s = _silu(c_ref[...])
    o_ref[...] = _dot(s.astype(BF16), w_ref[...].astype(BF16)) + b_ref[...]


def adaln(c8, w_ada, b_ada):
    depth, d, n6 = w_ada.shape
    tn = min(n6, 1536)
    return pl.pallas_call(
        _adaln_kernel,
        grid=(depth, n6 // tn),
        in_specs=[
            pl.BlockSpec((8, d), lambda l, j: (0, 0)),
            pl.BlockSpec((None, d, tn), lambda l, j: (l, 0, j)),
            pl.BlockSpec((None, 1, tn), lambda l, j: (l, 0, j)),
        ],
        out_specs=pl.BlockSpec((None, 8, tn), lambda l, j: (l, 0, j)),
        out_shape=jax.ShapeDtypeStruct((depth, 8, n6), F32),
        compiler_params=_cparams(("arbitrary", "arbitrary")),
        name="adaln",
    )(c8, w_ada, b_ada.reshape(depth, 1, n6))


def _norm_mod(x, gain, shift, scale):
    ms = jnp.mean(x * x, axis=-1, keepdims=True)
    h = x * lax.rsqrt(ms + EPS) * gain
    return h * (1.0 + scale) + shift


def _nmm_kernel(x_ref, g_ref, sh_ref, sc_ref, w_ref, o_ref, *, nchunk):
    hb = _norm_mod(x_ref[...], g_ref[...], sh_ref[...], sc_ref[...]).astype(BF16)
    n = o_ref.shape[-1]
    for n0 in range(0, n, nchunk):
        o_ref[:, n0:n0 + nchunk] = _dot(hb, w_ref[:, n0:n0 + nchunk]).astype(o_ref.dtype)


def _mod_row_map(tm, seq, n_lat_batches):
    return lambda i: (jnp.minimum((i * tm) // seq, n_lat_batches), 0, 0)


def norm_mod_matmul(xa, gain, shift, scale, w, tm, seq, nb):
    ta, d = xa.shape
    n = w.shape[1]
    nchunk = 512 if n % 512 == 0 else 384
    mrow = _mod_row_map(tm, seq, nb)
    return pl.pallas_call(
        functools.partial(_nmm_kernel, nchunk=nchunk),
        grid=(ta // tm,),
        in_specs=[
            pl.BlockSpec((tm, d), lambda i: (i, 0)),
            pl.BlockSpec((1, d), lambda i: (0, 0)),
            pl.BlockSpec((None, 1, d), mrow),
            pl.BlockSpec((None, 1, d), mrow),
            pl.BlockSpec((d, n), lambda i: (0, 0)),
        ],
        out_specs=pl.BlockSpec((tm, n), lambda i: (i, 0)),
        out_shape=jax.ShapeDtypeStruct((ta, n), BF16),
        compiler_params=_cparams(("arbitrary",)),
        name="norm_mod_matmul",
    )(xa, gain.reshape(1, d), shift, scale, w)


def _final_norm_kernel(x_ref, g_ref, o_ref):
    x = x_ref[...]
    ms = jnp.mean(x * x, axis=-1, keepdims=True)
    o_ref[...] = x * lax.rsqrt(ms + EPS) * g_ref[...]


def final_norm(xa, gain, t_rows, tm):
    d = xa.shape[1]
    return pl.pallas_call(
        _final_norm_kernel,
        grid=(t_rows // tm,),
        in_specs=[pl.BlockSpec((tm, d), lambda i: (i, 0)), pl.BlockSpec((1, d), lambda i: (0, 0))],
        out_specs=pl.BlockSpec((tm, d), lambda i: (i, 0)),
        out_shape=jax.ShapeDtypeStruct((t_rows, d), F32),
        compiler_params=_cparams(("arbitrary",)),
        name="final_norm",
    )(xa, gain.reshape(1, d))


def _prep_even_kernel(qk_ref, att_ref, cos_ref, s1_ref, s2_ref, qg_ref, kg_ref, bd_ref,
                      rq_ref, rk_ref, aq_ref, ak_ref, av_ref):
    cos = cos_ref[...]
    s1 = s1_ref[...]
    s2 = s2_ref[...]
    bd = bd_ref[...]
    half = ATT_HD

    def rope(x):
        return x * cos + pltpu.roll(x, LANES - 16, 1) * s1 + pltpu.roll(x, 16, 1) * s2

    def head_norm(x, gain):
        ms = jnp.dot(x * x, bd, preferred_element_type=F32, precision=HIGHEST)
        return x * lax.rsqrt(ms + EPS) * gain

    nq = RET_HEADS * RET_DK // LANES
    for j in range(nq):
        cs = slice(j * LANES, (j + 1) * LANES)
        rq_ref[:, cs] = rope(qk_ref[:, cs].astype(F32)).astype(BF16)
        ks = slice(nq * LANES + j * LANES, nq * LANES + (j + 1) * LANES)
        rk_ref[:, cs] = (rope(qk_ref[:, ks].astype(F32)) * RET_DK ** -0.5).astype(BF16)

    qg = qg_ref[...]
    kg = kg_ref[...]
    for j in range(ATT_HEADS * ATT_HD // LANES):
        x = att_ref[:, j * LANES:(j + 1) * LANES].astype(F32)
        y = (rope(head_norm(x, qg)) * (ATT_HD ** -0.5 * math.log2(math.e))).astype(BF16)
        aq_ref[2 * j] = y[:, :half]
        aq_ref[2 * j + 1] = y[:, half:]
    c0 = ATT_HEADS * ATT_HD
    y = rope(head_norm(att_ref[:, c0:c0 + LANES].astype(F32), kg)).astype(BF16)
    ak_ref[0] = y[:, :half]
    ak_ref[1] = y[:, half:]
    v = att_ref[:, c0 + LANES:c0 + 2 * LANES]
    av_ref[0] = v[:, :half]
    av_ref[1] = v[:, half:]


def prep_even(p, tabs, q_gain, k_gain, tm, seq, t_lat):
    ta = p.shape[0]
    cos_t, s1_t, s2_t = tabs
    n_tab = seq // tm

    def tab_map(i):
        r = i * tm
        return (jnp.where(r < t_lat, (r % seq) // tm, n_tab), 0)

    ii = np.arange(LANES)
    bd = jnp.asarray((ii[:, None] // ATT_HD == ii[None, :] // ATT_HD).astype(np.float32) / ATT_HD)
    qg = jnp.tile(q_gain.astype(F32), LANES // ATT_HD).reshape(1, LANES)
    kg = jnp.tile(k_gain.astype(F32), LANES // ATT_HD).reshape(1, LANES)
    qkw = 2 * RET_HEADS * RET_DK
    tab_spec = pl.BlockSpec((tm, LANES), tab_map)
    one = lambda i: (0, 0)
    return pl.pallas_call(
        _prep_even_kernel,
        grid=(ta // tm,),
        in_specs=[
            pl.BlockSpec((tm, qkw), lambda i: (i, 0)),
            pl.BlockSpec((tm, EVEN_ATT_W), lambda i: (i, EVEN_ATT_COL // EVEN_ATT_W)),
            tab_spec, tab_spec, tab_spec,
            pl.BlockSpec((1, LANES), one), pl.BlockSpec((1, LANES), one),
            pl.BlockSpec((LANES, LANES), one),
        ],
        out_specs=[
            pl.BlockSpec((tm, RET_HEADS * RET_DK), lambda i: (i, 0)),
            pl.BlockSpec((tm, RET_HEADS * RET_DK), lambda i: (i, 0)),
            pl.BlockSpec((ATT_HEADS, tm, ATT_HD), lambda i: (0, i, 0)),
            pl.BlockSpec((ATT_KV_HEADS, tm, ATT_HD), lambda i: (0, i, 0)),
            pl.BlockSpec((ATT_KV_HEADS, tm, ATT_HD), lambda i: (0, i, 0)),
        ],
        out_shape=[
            jax.ShapeDtypeStruct((ta, RET_HEADS * RET_DK), BF16),
            jax.ShapeDtypeStruct((ta, RET_HEADS * RET_DK), BF16),
            jax.ShapeDtypeStruct((ATT_HEADS, ta, ATT_HD), BF16),
            jax.ShapeDtypeStruct((ATT_KV_HEADS, ta, ATT_HD), BF16),
            jax.ShapeDtypeStruct((ATT_KV_HEADS, ta, ATT_HD), BF16),
        ],
        compiler_params=_cparams(("arbitrary",)),
        name="prep_even",
    )(p, p, cos_t, s1_t, s2_t, qg, kg, bd)


def rope_tables(seq, tm):
    nf = ATT_HD // 4
    t = jnp.arange(seq)
    rows = (t // GRID_W).astype(F32)
    cols = (t % GRID_W).astype(F32)
    inv = ROPE_BASE ** (-jnp.arange(nf, dtype=F32) / nf)
    lane = np.arange(LANES)
    axis = (lane % ATT_HD) // (ATT_HD // 2)
    f = lane % nf
    upper = ((lane % (ATT_HD // 2)) >= nf)
    pos = jnp.where(jnp.asarray(axis)[None, :] == 0, rows[:, None], cols[:, None])
    ang = pos * inv[jnp.asarray(f)][None, :]
    cos = jnp.cos(ang)
    sin = jnp.sin(ang)
    s1 = jnp.where(jnp.asarray(upper)[None, :], 0.0, -sin)
    s2 = jnp.where(jnp.asarray(upper)[None, :], sin, 0.0)
    pad1 = jnp.ones((tm, LANES), F32)
    pad0 = jnp.zeros((tm, LANES), F32)
    return (jnp.concatenate([cos, pad1]), jnp.concatenate([s1, pad0]), jnp.concatenate([s2, pad0]))


def _retention_kernel(dec_ref, q_ref, k_ref, v_ref, g_ref, s0f_ref, s0b_ref,
                      o_ref, sff_ref, sfb_ref, st_ref, *, n_chunks, unroll):
    hp = pl.program_id(1)
    C = RET_CHUNK
    dk, dv = RET_DK, RET_DV
    pos = lax.broadcasted_iota(jnp.int32, (C, dk), 0).astype(F32)
    ii = lax.broadcasted_iota(jnp.int32, (C, C), 0)
    jj = lax.broadcasted_iota(jnp.int32, (C, C), 1)
    dpos = (ii - jj).astype(F32)
    heads = range(2)
    qs = [slice(hh * dk, (hh + 1) * dk) for hh in heads]
    vs = [slice(hh * dv, (hh + 1) * dv) for hh in heads]
    w_out, w_in, gcf, gcb, mask = [], [], [], [], []
    for hh in heads:
        h = 2 * hp + hh
        df = dec_ref[0, h]
        db = dec_ref[1, h]
        lf = -jnp.exp(jnp.full((C, C), df, F32))
        lb = -jnp.exp(jnp.full((C, C), db, F32))
        lfk = -jnp.exp(jnp.full((C, dk), df, F32))
        lbk = -jnp.exp(jnp.full((C, dk), db, F32))
        w_out.append(jnp.concatenate([jnp.exp(lfk * (C - 1.0 - pos)), jnp.exp(lbk * pos)], axis=1))
        w_in.append(jnp.concatenate([jnp.exp(lfk * (pos + 1.0)), jnp.exp(lbk * (C - pos))], axis=1))
        gcf.append(jnp.exp(-jnp.exp(jnp.full((dk, dv), df, F32)) * C))
        gcb.append(jnp.exp(-jnp.exp(jnp.full((dk, dv), db, F32)) * C))
        mask.append(jnp.where(dpos > 0, jnp.exp(lf * jnp.maximum(dpos, 0.0)),
                              jnp.where(dpos < 0, jnp.exp(lb * jnp.maximum(-dpos, 0.0)), 2.0)))

    def rows(n):
        return pl.ds(pl.multiple_of(n * C, C), C)

    items = [(u, hh) for u in range(unroll) for hh in heads]

    def sums_body(i, carry):
        kk = {}
        for u, hh in items:
            k = k_ref[rows(i * unroll + u), qs[hh]].astype(F32)
            kk[(u, hh)] = (jnp.concatenate([k, k], axis=1) * w_out[hh]).astype(BF16)
        kv = {(u, hh): lax.dot_general(kk[(u, hh)], v_ref[rows(i * unroll + u), vs[hh]], TN_DIMS,
                                       preferred_element_type=F32) for u, hh in items}
        for u, hh in items:
            st_ref[hh, i * unroll + u] = kv[(u, hh)]
        return carry

    lax.fori_loop(0, n_chunks // unroll, sums_body, 0)

    def scan_body(n, carry):
        n_rev = n_chunks - 1 - n
        out = []
        for hh in heads:
            sf, sb = carry[2 * hh], carry[2 * hh + 1]
            kvf = st_ref[hh, n, 0:dk, :]
            kvb = st_ref[hh, n_rev, dk:2 * dk, :]
            st_ref[hh, n, 0:dk, :] = sf
            st_ref[hh, n_rev, dk:2 * dk, :] = sb
            out += [gcf[hh] * sf + kvf, gcb[hh] * sb + kvb]
        return tuple(out)

    init = tuple(x for hh in heads for x in (s0f_ref[hh], s0b_ref[hh]))
    fin = lax.fori_loop(0, n_chunks, scan_body, init)
    for hh in heads:
        sff_ref[hh] = fin[2 * hh]
        sfb_ref[hh] = fin[2 * hh + 1]

    def out_body(i, carry):
        ns = [i * unroll + u for u in range(unroll)]
        qb = {(u, hh): q_ref[rows(ns[u]), qs[hh]] for u, hh in items}
        sc = {(u, hh): lax.dot_general(qb[(u, hh)], k_ref[rows(ns[u]), qs[hh]], NT_DIMS,
                                       preferred_element_type=F32) for u, hh in items}
        qw = {}
        for it in items:
            q = qb[it].astype(F32)
            qw[it] = (jnp.concatenate([q, q], axis=1) * w_in[it[1]]).astype(BF16)
        o1 = {(u, hh): _dot((sc[(u, hh)] * mask[hh]).astype(BF16), v_ref[rows(ns[u]), vs[hh]]) for u, hh in items}
        o2 = {(u, hh): _dot(qw[(u, hh)], st_ref[hh, ns[u]].astype(BF16)) for u, hh in items}
        for it in items:
            u, hh = it
            n = ns[u]
            o = o1[it] + o2[it]
            o = o * lax.rsqrt(jnp.mean(o * o, axis=-1, keepdims=True) + EPS)
            gate = g_ref[rows(n), vs[hh]].astype(F32)
            o_ref[rows(n), vs[hh]] = (_silu(gate) * o).astype(o_ref.dtype)
        return carry

    lax.fori_loop(0, n_chunks // unroll, out_body, 0)


def retention(dec, rq, rk, p, s0f, s0b, nb, seq, row_off_blocks):
    n_chunks = seq // RET_CHUNK
    hp_n = RET_HEADS // 2
    vcol = RET_HEADS * RET_DK * 2 // (2 * RET_DV)
    gcol = vcol + RET_HEADS * RET_DV // (2 * RET_DV)
    ta = rq.shape[0]
    st_spec = pl.BlockSpec((None, 2, RET_DK, RET_DV), lambda b, hp, *_: (b, hp, 0, 0))
    grid_spec = pltpu.PrefetchScalarGridSpec(
        num_scalar_prefetch=1,
        grid=(nb, hp_n),
        in_specs=[
            pl.BlockSpec((seq, 2 * RET_DK), lambda b, hp, *_: (row_off_blocks + b, hp)),
            pl.BlockSpec((seq, 2 * RET_DK), lambda b, hp, *_: (row_off_blocks + b, hp)),
            pl.BlockSpec((seq, 2 * RET_DV), lambda b, hp, *_: (row_off_blocks + b, vcol + hp)),
            pl.BlockSpec((seq, 2 * RET_DV), lambda b, hp, *_: (row_off_blocks + b, gcol + hp)),
            st_spec, st_spec,
        ],
        out_specs=[
            pl.BlockSpec((seq, 2 * RET_DV), lambda b, hp, *_: (b, hp)),
            st_spec, st_spec,
        ],
        scratch_shapes=[pltpu.VMEM((2, n_chunks, 2 * RET_DK, RET_DV), F32)],
    )
    st_shape = jax.ShapeDtypeStruct((nb, RET_HEADS, RET_DK, RET_DV), F32)
    return pl.pallas_call(
        functools.partial(_retention_kernel, n_chunks=n_chunks, unroll=math.gcd(n_chunks, 4)),
        grid_spec=grid_spec,
        out_shape=[jax.ShapeDtypeStruct((nb * seq, RET_HEADS * RET_DV), BF16), st_shape, st_shape],
        compiler_params=_cparams(("arbitrary", "arbitrary")),
        name="retention",
    )(dec, rq, rk, p, p, s0f, s0b)


ATT_VT_ROWS = ATT_HD + 16


ATT_PAIRS_PER_TRIP = 4


def _attn_kernel(q_ref, k_ref, vt_ref, o_ref, *s_refs, tk, c_start, c_end, rep):
    tq = q_ref.shape[1]
    sets = (s_refs[:rep], s_refs[rep:])
    last = c_end - 1

    def scores(bufs, j):
        j = jnp.minimum(j, last)
        c0 = pl.multiple_of(j * tk, tk)
        k = k_ref[pl.ds(c0, tk), :]
        mxs = []
        for r in range(rep):
            s = lax.dot_general(k, q_ref[r], NT_DIMS, preferred_element_type=F32)
            bufs[r][...] = s
            mxs.append(jnp.max(s, axis=0, keepdims=True))
        return tuple(mxs)

    def softmax_pv(bufs, j, mxs, ms, accs):
        vt = vt_ref[j]
        new_m, new_acc = [], []
        for r in range(rep):
            m_new = jnp.maximum(ms[r], mxs[r])
            a = jnp.exp2(ms[r] - m_new)
            p = jnp.exp2(bufs[r][...] - m_new).astype(BF16)
            new_acc.append(a * accs[r] + _dot(vt, p))
            new_m.append(m_new)
        return tuple(new_m), tuple(new_acc)

    def pair(j, mx0, ms, accs):
        mx1 = scores(sets[1], j + 1)
        ms, accs = softmax_pv(sets[0], j, mx0, ms, accs)
        mx0 = scores(sets[0], j + 2)
        ms, accs = softmax_pv(sets[1], j + 1, mx1, ms, accs)
        return mx0, ms, accs

    def trip(t, carry):
        for u in range(ATT_PAIRS_PER_TRIP):
            carry = pair(c_start + 2 * (ATT_PAIRS_PER_TRIP * t + u), *carry)
        return carry

    n_pairs = (c_end - c_start) // 2
    n_trips = n_pairs // ATT_PAIRS_PER_TRIP
    ms = tuple(jnp.full((1, tq), -1e30, F32) for _ in range(rep))
    accs = tuple(jnp.zeros((ATT_VT_ROWS, tq), F32) for _ in range(rep))
    carry = (scores(sets[0], c_start), ms, accs)
    if n_trips:
        carry = lax.fori_loop(0, n_trips, trip, carry)
    for u in range(n_trips * ATT_PAIRS_PER_TRIP, n_pairs):
        carry = pair(c_start + 2 * u, *carry)
    mx0, ms, accs = carry
    if (c_end - c_start) % 2:
        ms, accs = softmax_pv(sets[0], last, mx0, ms, accs)
    outs = [(acc[:ATT_HD, :] / acc[ATT_HD:ATT_HD + 1, :]).T for acc in accs]
    o_ref[...] = jnp.concatenate(outs, axis=-1).astype(o_ref.dtype)


def attention(aq, kcat, vtcat, seq_q, q_off_blocks, c_start, tq, tk):
    rep = ATT_HEADS // ATT_KV_HEADS
    _, nb, lk, _ = kcat.shape
    nq = seq_q // tq
    n_chunks = lk // tk
    return pl.pallas_call(
        functools.partial(_attn_kernel, tk=tk, c_start=c_start, c_end=n_chunks, rep=rep),
        grid=(nb, ATT_KV_HEADS, nq),
        in_specs=[
            pl.BlockSpec((rep, tq, ATT_HD), lambda b, g, i: (g, q_off_blocks + b * nq + i, 0)),
            pl.BlockSpec((None, None, lk, ATT_HD), lambda b, g, i: (g, b, 0, 0)),
            pl.BlockSpec((None, None, n_chunks, ATT_VT_ROWS, tk), lambda b, g, i: (g, b, 0, 0, 0)),
        ],
        out_specs=pl.BlockSpec((tq, rep * ATT_HD), lambda b, g, i: (b * nq + i, g)),
        out_shape=jax.ShapeDtypeStruct((nb * seq_q, ATT_HEADS * ATT_HD), BF16),
        scratch_shapes=[pltpu.VMEM((tk, tq), F32) for _ in range(2 * rep)],
        compiler_params=_cparams(("arbitrary", "arbitrary", "arbitrary")),
        name="attention",
    )(aq, kcat, vtcat)


def _lat_ctx_specs(tm, width, n_lat_tiles):
    return [pl.BlockSpec((tm, width), lambda i: (jnp.minimum(i, n_lat_tiles - 1), 0)),
            pl.BlockSpec((tm, width), lambda i: (jnp.maximum(i - n_lat_tiles, 0), 0))]


def _outproj_even_kernel(r_lat, r_ctx, a_lat, a_ctx, w1_ref, w2_ref, res_ref, gate_ref, o_ref, *, n_lat_tiles):
    is_lat = pl.program_id(0) < n_lat_tiles
    a1 = jnp.where(is_lat, r_lat[...], r_ctx[...])
    a2 = jnp.where(is_lat, a_lat[...], a_ctx[...])
    y = _dot(a1, w1_ref[...]) + _dot(a2, w2_ref[...])
    o_ref[...] = res_ref[...] + gate_ref[...] * y


def outproj_even(ret_lat, ret_ctx, att_lat, att_ctx, w1, w2, xa, gate, tm, seq, nb):
    ta, d = xa.shape
    k1, k2 = w1.shape[0], w2.shape[0]
    n_lat_tiles = ret_lat.shape[0] // tm
    return pl.pallas_call(
        functools.partial(_outproj_even_kernel, n_lat_tiles=n_lat_tiles),
        grid=(ta // tm,),
        in_specs=_lat_ctx_specs(tm, k1, n_lat_tiles) + _lat_ctx_specs(tm, k2, n_lat_tiles) + [
            pl.BlockSpec((k1, d), lambda i: (0, 0)),
            pl.BlockSpec((k2, d), lambda i: (0, 0)),
            pl.BlockSpec((tm, d), lambda i: (i, 0)),
            pl.BlockSpec((None, 1, d), _mod_row_map(tm, seq, nb)),
        ],
        out_specs=pl.BlockSpec((tm, d), lambda i: (i, 0)),
        out_shape=jax.ShapeDtypeStruct((ta, d), F32),
        compiler_params=_cparams(("arbitrary",)),
        name="outproj_even",
    )(ret_lat, ret_ctx, att_lat, att_ctx, w1, w2, xa, gate)


def _outproj_odd_kernel(f_lat, f_ctx, b_lat, b_ctx, z_ref, og_ref, w_ref, res_ref, gate_ref, o_ref,
                        *, n_lat_tiles):
    is_lat = pl.program_id(0) < n_lat_tiles
    og = og_ref[...]
    parts = []
    for h in range(DN_HEADS):
        cs = slice(h * DN_DV, (h + 1) * DN_DV)
        of = jnp.where(is_lat, f_lat[:, cs], f_ctx[:, cs]).astype(F32)
        ob = jnp.where(is_lat, b_lat[:, cs], b_ctx[:, cs]).astype(F32)
        o = of + ob
        o = o * lax.rsqrt(jnp.mean(o * o, axis=-1, keepdims=True) + EPS) * og
        parts.append((o * _silu(z_ref[:, cs].astype(F32))).astype(BF16))
    a = jnp.concatenate(parts, axis=-1)
    o_ref[...] = res_ref[...] + gate_ref[...] * _dot(a, w_ref[...])


def outproj_odd(of_lat, of_ctx, ob_lat, ob_ctx, p, out_gain, w, xa, gate, tm, seq, nb):
    ta, d = xa.shape
    kdim = DN_HEADS * DN_DV
    n_lat_tiles = of_lat.shape[0] // tm
    return pl.pallas_call(
        functools.partial(_outproj_odd_kernel, n_lat_tiles=n_lat_tiles),
        grid=(ta // tm,),
        in_specs=_lat_ctx_specs(tm, kdim, n_lat_tiles) + _lat_ctx_specs(tm, kdim, n_lat_tiles) + [
            pl.BlockSpec((tm, kdim), lambda i: (i, DN_QKV // kdim)),
            pl.BlockSpec((1, DN_DV), lambda i: (0, 0)),
            pl.BlockSpec((kdim, d), lambda i: (0, 0)),
            pl.BlockSpec((tm, d), lambda i: (i, 0)),
            pl.BlockSpec((None, 1, d), _mod_row_map(tm, seq, nb)),
        ],
        out_specs=pl.BlockSpec((tm, d), lambda i: (i, 0)),
        out_shape=jax.ShapeDtypeStruct((ta, d), F32),
        compiler_params=_cparams(("arbitrary",)),
        name="outproj_odd",
    )(of_lat, of_ctx, ob_lat, ob_ctx, p, out_gain.reshape(1, DN_DV).astype(F32), w, xa, gate)


def _prep_odd_kernel(first_ref, last_ref, x_ref, prev_ref, next_ref, ab_ref, cw_ref, arow_ref, brow_ref,
                     q_ref, k_ref, v_ref, gb_ref):
    i = pl.program_id(0)
    tm = x_ref.shape[0]
    hrows = prev_ref.shape[0]
    keep_prev = 1.0 - first_ref[i].astype(F32)
    keep_next = 1.0 - last_ref[i].astype(F32)
    row = lax.broadcasted_iota(jnp.int32, (tm, LANES), 0)
    is_first = row == 0
    is_last = row == tm - 1
    n_qk = 2 * DN_HEADS * DN_DK // LANES
    n_q = DN_HEADS * DN_DK // LANES
    outs = (q_ref, k_ref, v_ref)
    for j in range(DN_QKV // LANES):
        cs = slice(j * LANES, (j + 1) * LANES)
        x = x_ref[:, cs].astype(F32)
        xp = prev_ref[:, cs].astype(F32)[hrows - 1:hrows, :] * keep_prev
        xn = next_ref[:, cs].astype(F32)[0:1, :] * keep_next
        x_dn = jnp.where(is_first, xp, pltpu.roll(x, 1, 0))
        x_up = jnp.where(is_last, xn, pltpu.roll(x, tm - 1, 0))
        w = cw_ref[:, cs]
        y = _silu(x_dn * w[0:1, :] + x * w[1:2, :] + x_up * w[2:3, :])
        if j < n_qk:
            y = y * lax.rsqrt(jnp.sum(y * y, axis=-1, keepdims=True) + EPS)
            if j < n_q:
                y = y * DN_DK ** -0.5
        lj = j % n_q
        outs[j // n_q][:, lj * LANES:(lj + 1) * LANES] = y.astype(BF16)

    a = ab_ref[...].astype(F32)
    lane = lax.broadcasted_iota(jnp.int32, (tm, LANES), 1)
    z = a + brow_ref[...]
    softplus = jnp.maximum(z, 0.0) + jnp.log(1.0 + jnp.exp(-jnp.abs(z)))
    g = -jnp.exp(arow_ref[...]) * softplus
    beta = 1.0 / (1.0 + jnp.exp(-a))
    gb_ref[...] = jnp.where(lane < 2 * DN_HEADS, g, jnp.where(lane < 4 * DN_HEADS, beta, 0.0))


def prep_odd(p, conv_w, arow, brow, first_flags, last_flags, tm):
    ta = p.shape[0]
    halo = 16
    hb = tm // halo
    n_h = ta // halo
    kdim = DN_HEADS * DN_DK
    grid_spec = pltpu.PrefetchScalarGridSpec(
        num_scalar_prefetch=2,
        grid=(ta // tm,),
        in_specs=[
            pl.BlockSpec((tm, DN_QKV), lambda i, *_: (i, 0)),
            pl.BlockSpec((halo, DN_QKV), lambda i, *_: (jnp.maximum(i * hb - 1, 0), 0)),
            pl.BlockSpec((halo, DN_QKV), lambda i, *_: (jnp.minimum((i + 1) * hb, n_h - 1), 0)),
            pl.BlockSpec((tm, LANES), lambda i, *_: (i, (DN_QKV + DN_HEADS * DN_DV) // LANES)),
            pl.BlockSpec((DN_CONV, DN_QKV), lambda i, *_: (0, 0)),
            pl.BlockSpec((1, LANES), lambda i, *_: (0, 0)),
            pl.BlockSpec((1, LANES), lambda i, *_: (0, 0)),
        ],
        out_specs=[
            pl.BlockSpec((tm, kdim), lambda i, *_: (i, 0)),
            pl.BlockSpec((tm, kdim), lambda i, *_: (i, 0)),
            pl.BlockSpec((tm, kdim), lambda i, *_: (i, 0)),
            pl.BlockSpec((tm, LANES), lambda i, *_: (i, 0)),
        ],
    )
    return pl.pallas_call(
        _prep_odd_kernel,
        grid_spec=grid_spec,
        out_shape=[
            jax.ShapeDtypeStruct((ta, kdim), BF16),
            jax.ShapeDtypeStruct((ta, kdim), BF16),
            jax.ShapeDtypeStruct((ta, kdim), BF16),
            jax.ShapeDtypeStruct((ta, LANES), F32),
        ],
        compiler_params=_cparams(("arbitrary",)),
        name="prep_odd",
    )(first_flags, last_flags, p, p, p, p, conv_w, arow, brow)


def _deltanet_kernel(q_ref, k_ref, v_ref, gb_ref, gbt_ref, s0_ref, o_ref, sf_ref, s_ref,
                     *, reverse, n_chunks, dir_off):
    t = pl.program_id(1)

    @pl.when(t == 0)
    def _():
        s_ref[...] = s0_ref[...]

    C = DN_CHUNK
    ii = lax.broadcasted_iota(jnp.int32, (C, C), 0)
    jj = lax.broadcasted_iota(jnp.int32, (C, C), 1)
    if reverse:
        incl = ii <= jj
        strict = ii < jj
    else:
        incl = ii >= jj
        strict = ii > jj
    tri = jnp.where(incl, 1.0, 0.0).astype(F32)
    if reverse:
        tri_t = jnp.where(ii >= jj, 1.0, 0.0).astype(F32)
    else:
        tri_t = jnp.where(ii <= jj, 1.0, 0.0).astype(F32)
    eye = jnp.where(ii == jj, 1.0, 0.0).astype(F32)
    blk = ii ^ jj

    order = list(range(n_chunks - 1, -1, -1) if reverse else range(n_chunks))
    items = [(c, h) for c in order for h in range(DN_HEADS)]
    gcols, grows, gbs = {}, {}, {}
    for c in order:
        gb_c = gb_ref[c * C:(c + 1) * C, :]
        gbs[c] = gb_c
        gcols[c] = jnp.dot(tri, gb_c, preferred_element_type=F32, precision=HIGHEST)
        grows[c] = jnp.dot(gbt_ref[c], tri_t, preferred_element_type=F32, precision=HIGHEST)

    qb, kb16, decay, kbeta, egc, kd, gl, rhs = {}, {}, {}, {}, {}, {}, {}, {}
    for it in items:
        c, h = it
        gi = dir_off + h
        bi = 2 * DN_HEADS + dir_off + h
        rows = slice(c * C, (c + 1) * C)
        cs = slice(h * DN_DK, (h + 1) * DN_DK)
        gc = gcols[c][:, gi:gi + 1]
        gr = grows[c][gi:gi + 1, :]
        beta = gbs[c][:, bi:bi + 1]
        qb[it] = q_ref[rows, cs]
        kb16[it] = k_ref[rows, cs]
        kf = kb16[it].astype(F32)
        decay[it] = jnp.where(incl, jnp.exp(jnp.where(incl, gc - gr, 0.0)), 0.0)
        kbeta[it] = kf * beta
        egc[it] = jnp.exp(gc)
        glast = gc[0:1, :] if reverse else gc[C - 1:C, :]
        kd[it] = (kf * jnp.exp(glast - gc)).astype(BF16)
        gl[it] = jnp.exp(glast)
        rhs[it] = jnp.concatenate([v_ref[rows, cs].astype(F32) * beta, kbeta[it] * egc[it]], axis=1).astype(BF16)

    kk = {it: lax.dot_general(kbeta[it].astype(BF16), kb16[it], NT_DIMS, preferred_element_type=F32)
          for it in items}
    qk = {it: lax.dot_general(qb[it], kb16[it], NT_DIMS, preferred_element_type=F32) for it in items}
    lm = {it: jnp.where(strict, kk[it] * decay[it], 0.0) for it in items}
    attn = {it: jnp.where(incl, qk[it] * decay[it], 0.0).astype(BF16) for it in items}
    dinv = {it: eye - jnp.where(blk < 2, lm[it], 0.0) for it in items}
    s = 2
    while s < C:
        in_band = jnp.logical_and(blk >= s, blk < 2 * s)
        tmp = {it: _dot(dinv[it].astype(BF16), jnp.where(in_band, lm[it], 0.0).astype(BF16)) for it in items}
        dinv = {it: dinv[it] - _dot(tmp[it].astype(BF16), dinv[it].astype(BF16)) for it in items}
        s *= 2
    uw = {it: _dot(dinv[it].astype(BF16), rhs[it]) for it in items}
    wq = {it: jnp.concatenate([uw[it][:, DN_DV:], qb[it].astype(F32) * egc[it]], axis=0).astype(BF16)
          for it in items}

    states = [s_ref[h] for h in range(DN_HEADS)]
    for c in order:
        its = [(c, h) for h in range(DN_HEADS)]
        r = {it: _dot(wq[it], states[it[1]].astype(BF16)) for it in its}
        v_new = {it: (uw[it][:, :DN_DV] - r[it][:C]).astype(BF16) for it in its}
        o = {it: r[it][C:] + _dot(attn[it], v_new[it]) for it in its}
        for it in its:
            h = it[1]
            states[h] = states[h] * gl[it] + lax.dot_general(kd[it], v_new[it], TN_DIMS,
                                                             preferred_element_type=F32)
        for it in its:
            h = it[1]
            o_ref[c * C:(c + 1) * C, h * DN_DK:(h + 1) * DN_DK] = o[it].astype(o_ref.dtype)
    for h in range(DN_HEADS):
        s_ref[h] = states[h]

    @pl.when(t == pl.num_programs(1) - 1)
    def _():
        sf_ref[...] = s_ref[...]


def deltanet(q, k, v, gb, gbt, s0, nb, seq, row_off, reverse, tl):
    nblk = seq // tl
    n_chunks = tl // DN_CHUNK
    off_b = row_off // tl
    kdim = DN_HEADS * DN_DK

    def rb(b, t):
        tt = nblk - 1 - t if reverse else t
        return off_b + b * nblk + tt

    seq_spec = pl.BlockSpec((tl, kdim), lambda b, t: (rb(b, t), 0))
    st_spec = pl.BlockSpec((None, DN_HEADS, DN_DK, DN_DV), lambda b, t: (b, 0, 0, 0))
    return pl.pallas_call(
        functools.partial(_deltanet_kernel, reverse=reverse, n_chunks=n_chunks,
                          dir_off=DN_HEADS if reverse else 0),
        grid=(nb, nblk),
        in_specs=[
            seq_spec, seq_spec, seq_spec,
            pl.BlockSpec((tl, LANES), lambda b, t: (rb(b, t), 0)),
            pl.BlockSpec((n_chunks, 4 * DN_HEADS, DN_CHUNK), lambda b, t: (rb(b, t), 0, 0)),
            st_spec,
        ],
        out_specs=[
            pl.BlockSpec((tl, kdim), lambda b, t: (b * nblk + (nblk - 1 - t if reverse else t), 0)),
            st_spec,
        ],
        out_shape=[
            jax.ShapeDtypeStruct((nb * seq, kdim), BF16),
            jax.ShapeDtypeStruct((nb, DN_HEADS, DN_DK, DN_DV), F32),
        ],
        scratch_shapes=[pltpu.VMEM((DN_HEADS, DN_DK, DN_DV), F32)],
        compiler_params=_cparams(("arbitrary", "arbitrary")),
        name="deltanet_bwd" if reverse else "deltanet_fwd",
    )(q, k, v, gb, gbt, s0)


def _deltanet_bidir_kernel(qf_ref, kf_ref, vf_ref, gbf_ref, gbtf_ref, qb_ref, kb_ref, vb_ref, gbb_ref, gbtb_ref,
                           s0f_ref, s0b_ref, of_ref, ob_ref, sff_ref, sfb_ref, sf_scr, sb_scr, *, n_chunks):
    t = pl.program_id(1)

    @pl.when(t == 0)
    def _():
        sf_scr[...] = s0f_ref[...]
        sb_scr[...] = s0b_ref[...]

    C = DN_CHUNK
    ii = lax.broadcasted_iota(jnp.int32, (C, C), 0)
    jj = lax.broadcasted_iota(jnp.int32, (C, C), 1)
    lower, upper = ii >= jj, ii <= jj
    eye = jnp.where(ii == jj, 1.0, 0.0).astype(F32)
    blk = ii ^ jj
    dirs = (
        dict(rev=False, incl=lower, strict=ii > jj, q=qf_ref, k=kf_ref, v=vf_ref, gb=gbf_ref, gbt=gbtf_ref,
             o=of_ref, scr=sf_scr, off=0, order=list(range(n_chunks))),
        dict(rev=True, incl=upper, strict=ii < jj, q=qb_ref, k=kb_ref, v=vb_ref, gb=gbb_ref, gbt=gbtb_ref,
             o=ob_ref, scr=sb_scr, off=DN_HEADS, order=list(range(n_chunks - 1, -1, -1))),
    )
    items = [(d, c, h) for d in range(2) for c in dirs[d]["order"] for h in range(DN_HEADS)]

    gcols, grows, gbs = {}, {}, {}
    for d, dr in enumerate(dirs):
        tri = jnp.where(dr["incl"], 1.0, 0.0).astype(F32)
        tri_t = jnp.where(upper if not dr["rev"] else lower, 1.0, 0.0).astype(F32)
        for c in dr["order"]:
            gb_c = dr["gb"][c * C:(c + 1) * C, :]
            gbs[(d, c)] = gb_c
            gcols[(d, c)] = jnp.dot(tri, gb_c, preferred_element_type=F32, precision=HIGHEST)
            grows[(d, c)] = jnp.dot(dr["gbt"][c], tri_t, preferred_element_type=F32, precision=HIGHEST)

    qb, kb16, decay, kbeta, egc, kd, gl, rhs = {}, {}, {}, {}, {}, {}, {}, {}
    for it in items:
        d, c, h = it
        dr = dirs[d]
        gi = dr["off"] + h
        bi = 2 * DN_HEADS + dr["off"] + h
        rows = slice(c * C, (c + 1) * C)
        cs = slice(h * DN_DK, (h + 1) * DN_DK)
        gc = gcols[(d, c)][:, gi:gi + 1]
        gr = grows[(d, c)][gi:gi + 1, :]
        beta = gbs[(d, c)][:, bi:bi + 1]
        qb[it] = dr["q"][rows, cs]
        kb16[it] = dr["k"][rows, cs]
        kf = kb16[it].astype(F32)
        decay[it] = jnp.where(dr["incl"], jnp.exp(jnp.where(dr["incl"], gc - gr, 0.0)), 0.0)
        kbeta[it] = kf * beta
        egc[it] = jnp.exp(gc)
        glast = gc[0:1, :] if dr["rev"] else gc[C - 1:C, :]
        kd[it] = (kf * jnp.exp(glast - gc)).astype(BF16)
        gl[it] = jnp.exp(glast)
        rhs[it] = jnp.concatenate([dr["v"][rows, cs].astype(F32) * beta, kbeta[it] * egc[it]], axis=1).astype(BF16)

    kk = {it: lax.dot_general(kbeta[it].astype(BF16), kb16[it], NT_DIMS, preferred_element_type=F32)
          for it in items}
    qk = {it: lax.dot_general(qb[it], kb16[it], NT_DIMS, preferred_element_type=F32) for it in items}
    lm = {it: jnp.where(dirs[it[0]]["strict"], kk[it] * decay[it], 0.0) for it in items}
    attn = {it: jnp.where(dirs[it[0]]["incl"], qk[it] * decay[it], 0.0).astype(BF16) for it in items}
    dinv = {it: eye - jnp.where(blk < 2, lm[it], 0.0) for it in items}
    s = 2
    while s < C:
        in_band = jnp.logical_and(blk >= s, blk < 2 * s)
        tmp = {it: _dot(dinv[it].astype(BF16), jnp.where(in_band, lm[it], 0.0).astype(BF16)) for it in items}
        dinv = {it: dinv[it] - _dot(tmp[it].astype(BF16), dinv[it].astype(BF16)) for it in items}
        s *= 2
    uw = {it: _dot(dinv[it].astype(BF16), rhs[it]) for it in items}
    wq = {it: jnp.concatenate([uw[it][:, DN_DV:], qb[it].astype(F32) * egc[it]], axis=0).astype(BF16)
          for it in items}

    states = {(d, h): dirs[d]["scr"][h] for d in range(2) for h in range(DN_HEADS)}
    for step in range(n_chunks):
        its = [(d, dirs[d]["order"][step], h) for d in range(2) for h in range(DN_HEADS)]
        r = {it: _dot(wq[it], states[(it[0], it[2])].astype(BF16)) for it in its}
        v_new = {it: (uw[it][:, :DN_DV] - r[it][:C]).astype(BF16) for it in its}
        o = {it: r[it][C:] + _dot(attn[it], v_new[it]) for it in its}
        for it in its:
            key = (it[0], it[2])
            states[key] = states[key] * gl[it] + lax.dot_general(kd[it], v_new[it], TN_DIMS,
                                                                 preferred_element_type=F32)
        for it in its:
            d, c, h = it
            dirs[d]["o"][c * C:(c + 1) * C, h * DN_DK:(h + 1) * DN_DK] = o[it].astype(of_ref.dtype)
    for (d, h), st in states.items():
        dirs[d]["scr"][h] = st

    @pl.when(t == pl.num_programs(1) - 1)
    def _():
        sff_ref[...] = sf_scr[...]
        sfb_ref[...] = sb_scr[...]


def deltanet_bidir(q, k, v, gb, gbt, s0f, s0b, nb, seq, row_off, tl):
    nblk = seq // tl
    n_chunks = tl // DN_CHUNK
    off_b = row_off // tl
    kdim = DN_HEADS * DN_DK

    def fwd_rb(b, t):
        return off_b + b * nblk + t

    def bwd_rb(b, t):
        return off_b + b * nblk + (nblk - 1 - t)

    def seq_specs(rb):
        spec = pl.BlockSpec((tl, kdim), lambda b, t: (rb(b, t), 0))
        return [spec, spec, spec,
                pl.BlockSpec((tl, LANES), lambda b, t: (rb(b, t), 0)),
                pl.BlockSpec((n_chunks, 4 * DN_HEADS, DN_CHUNK), lambda b, t: (rb(b, t), 0, 0))]

    st_spec = pl.BlockSpec((None, DN_HEADS, DN_DK, DN_DV), lambda b, t: (b, 0, 0, 0))
    st_shape = jax.ShapeDtypeStruct((nb, DN_HEADS, DN_DK, DN_DV), F32)
    o_shape = jax.ShapeDtypeStruct((nb * seq, kdim), BF16)
    return pl.pallas_call(
        functools.partial(_deltanet_bidir_kernel, n_chunks=n_chunks),
        grid=(nb, nblk),
        in_specs=seq_specs(fwd_rb) + seq_specs(bwd_rb) + [st_spec, st_spec],
        out_specs=[
            pl.BlockSpec((tl, kdim), lambda b, t: (b * nblk + t, 0)),
            pl.BlockSpec((tl, kdim), lambda b, t: (b * nblk + nblk - 1 - t, 0)),
            st_spec, st_spec,
        ],
        out_shape=[o_shape, o_shape, st_shape, st_shape],
        scratch_shapes=[pltpu.VMEM((DN_HEADS, DN_DK, DN_DV), F32), pltpu.VMEM((DN_HEADS, DN_DK, DN_DV), F32)],
        compiler_params=_cparams(("arbitrary", "arbitrary")),
        name="deltanet_bidir",
    )(q, k, v, gb, gbt, q, k, v, gb, gbt, s0f, s0b)


def _router_kernel(x_ref, g_ref, sh_ref, sc_ref, wr_ref, br_ref, ltri_ref, f_ref, r_ref, cnt_ref, base_ref):
    @pl.when(pl.program_id(0) == 0)
    def _():
        base_ref[...] = jnp.zeros_like(base_ref)

    h = _norm_mod(x_ref[...], g_ref[...], sh_ref[...], sc_ref[...])
    f_ref[...] = h
    logits = _dot(h.astype(BF16), wr_ref[...]) + br_ref[...]
    tm = logits.shape[0]
    lane = lax.broadcasted_iota(jnp.int32, (tm, LANES), 1)
    neg = -1e30
    big = 4 * LANES
    is_g = lane < N_GROUPS
    gl = jnp.where(is_g, logits, neg)
    gm = jnp.max(gl, axis=-1, keepdims=True)
    grp = jnp.min(jnp.where(gl == gm, lane, big), axis=-1, keepdims=True)
    psum = jnp.sum(jnp.where(is_g, jnp.exp(gl - gm), 0.0), axis=-1, keepdims=True)
    p_grp = 1.0 / psum
    e_lane = lane - N_GROUPS
    in_grp = jnp.logical_and(jnp.logical_and(e_lane >= 0, e_lane < N_EXPERTS),
                             (e_lane // EXPERTS_PER_GROUP) == grp)
    el = jnp.where(in_grp, logits, neg)
    m1 = jnp.max(el, axis=-1, keepdims=True)
    i1 = jnp.min(jnp.where(el == m1, lane, big), axis=-1, keepdims=True)
    el2 = jnp.where(lane == i1, neg, el)
    m2 = jnp.max(el2, axis=-1, keepdims=True)
    i2 = jnp.min(jnp.where(el2 == m2, lane, big), axis=-1, keepdims=True)
    e21 = jnp.exp(m2 - m1)
    w1 = p_grp / (1.0 + e21)
    w2 = p_grp * e21 / (1.0 + e21)
    e1 = (i1 - N_GROUPS).astype(F32)
    e2 = (i2 - N_GROUPS).astype(F32)
    oh1 = lane == i1
    oh2 = lane == i2
    oh1f = jnp.where(oh1, 1.0, 0.0)
    oh2f = jnp.where(oh2, 1.0, 0.0)
    ltri = ltri_ref[...]
    before1 = _dot(ltri, oh1f.astype(BF16))
    before2 = _dot(ltri, oh2f.astype(BF16))
    cnt1 = jnp.sum(oh1f, axis=0, keepdims=True)
    cnt2 = jnp.sum(oh2f, axis=0, keepdims=True)
    base = base_ref[0:1, :]
    rank1 = jnp.sum(jnp.where(oh1, base + before1, 0.0), axis=-1, keepdims=True)
    rank2 = jnp.sum(jnp.where(oh2, base + cnt1 + before2, 0.0), axis=-1, keepdims=True)
    total = base + cnt1 + cnt2
    base_ref[...] = jnp.broadcast_to(total, base_ref.shape)
    cnt_ref[...] = jnp.broadcast_to(total, cnt_ref.shape)
    vals = (e1, e2, w1, w2, rank1, rank2)
    out = jnp.zeros((tm, LANES), F32)
    for idx, val in enumerate(vals):
        out = jnp.where(lane == idx, val, out)
    r_ref[...] = out


def moe_router(xa, gain, shift, scale, w_router, b_router, tm, seq, nb):
    ta, d = xa.shape
    mrow = _mod_row_map(tm, seq, nb)
    ii = np.arange(tm)
    ltri = jnp.asarray((ii[:, None] > ii[None, :]).astype(np.float32)).astype(BF16)
    return pl.pallas_call(
        _router_kernel,
        grid=(ta // tm,),
        in_specs=[
            pl.BlockSpec((tm, d), lambda i: (i, 0)),
            pl.BlockSpec((1, d), lambda i: (0, 0)),
            pl.BlockSpec((None, 1, d), mrow),
            pl.BlockSpec((None, 1, d), mrow),
            pl.BlockSpec((d, LANES), lambda i: (0, 0)),
            pl.BlockSpec((1, LANES), lambda i: (0, 0)),
            pl.BlockSpec((tm, tm), lambda i: (0, 0)),
        ],
        out_specs=[pl.BlockSpec((tm, d), lambda i: (i, 0)), pl.BlockSpec((tm, LANES), lambda i: (i, 0)),
                   pl.BlockSpec((8, LANES), lambda i: (0, 0))],
        out_shape=[jax.ShapeDtypeStruct((ta, d), F32), jax.ShapeDtypeStruct((ta, LANES), F32),
                   jax.ShapeDtypeStruct((8, LANES), F32)],
        scratch_shapes=[pltpu.VMEM((8, LANES), F32)],
        compiler_params=_cparams(("arbitrary",)),
        name="moe_router",
    )(xa, gain.reshape(1, d), shift, scale, w_router, b_router, ltri)


ROW_DMA_UNROLL = 8


def _issue_row_copies(n_rows, make_copy):
    def trip(i, carry):
        for u in range(ROW_DMA_UNROLL):
            make_copy(i * ROW_DMA_UNROLL + u).start(priority=u % 2)
        return carry

    lax.fori_loop(0, n_rows // ROW_DMA_UNROLL, trip, 0)


def _moe_scatter_kernel(pos_ref, f_ref, xs_in, xs_out, sem):
    del xs_in
    tm = f_ref.shape[0]
    _issue_row_copies(2 * tm, lambda r: pltpu.make_async_copy(
        f_ref.at[pl.ds(r % tm, 1)], xs_out.at[pl.ds(pos_ref[0, 0, r], 1)], sem))
    for _ in range(2):
        pltpu.make_async_copy(f_ref, xs_out.at[pl.ds(0, tm)], sem).wait()


def moe_scatter(pos_tiles, f, xs_zero, tm):
    ta, d = f.shape
    return pl.pallas_call(
        _moe_scatter_kernel,
        grid=(ta // tm,),
        in_specs=[
            pl.BlockSpec((1, 1, 2 * tm), lambda i: (i, 0, 0), memory_space=pltpu.SMEM),
            pl.BlockSpec((tm, d), lambda i: (i, 0)),
            pl.BlockSpec(memory_space=pl.ANY),
        ],
        out_specs=pl.BlockSpec(memory_space=pl.ANY),
        out_shape=jax.ShapeDtypeStruct(xs_zero.shape, xs_zero.dtype),
        scratch_shapes=[pltpu.SemaphoreType.DMA(())],
        input_output_aliases={2: 0},
        compiler_params=_cparams(("arbitrary",)),
        name="moe_scatter",
    )(pos_tiles, f, xs_zero)


def _moe_ffn_kernel(te_ref, nu_ref, x_ref, wgu_ref, wd_ref, o_ref, wgu_bf, wd_bf):
    i = pl.program_id(0)
    fdim = wd_bf.shape[0]

    @pl.when(i < nu_ref[0])
    def _():
        prev = te_ref[jnp.maximum(i - 1, 0)]
        changed = jnp.logical_or(i == 0, te_ref[i] != prev)

        @pl.when(changed)
        def _():
            wgu_bf[...] = wgu_ref[...].astype(BF16)
            wd_bf[...] = wd_ref[...].astype(BF16)

        gu = _dot(x_ref[...].astype(BF16), wgu_bf[...])
        hmid = _silu(gu[:, :fdim]) * gu[:, fdim:]
        o_ref[...] = _dot(hmid.astype(BF16), wd_bf[...])

    @pl.when(i >= nu_ref[0])
    def _():
        o_ref[...] = jnp.zeros_like(o_ref)


def moe_ffn(tile_expert, n_used, xs, w_gate_up, w_down, tm):
    n_pad, d = xs.shape
    _, _, f2 = w_gate_up.shape
    fdim = w_down.shape[1]
    grid_spec = pltpu.PrefetchScalarGridSpec(
        num_scalar_prefetch=2,
        grid=(n_pad // tm,),
        in_specs=[
            pl.BlockSpec((tm, d), lambda i, te, nu: (i, 0)),
            pl.BlockSpec((None, d, f2), lambda i, te, nu: (te[i], 0, 0)),
            pl.BlockSpec((None, fdim, d), lambda i, te, nu: (te[i], 0, 0)),
        ],
        out_specs=pl.BlockSpec((tm, d), lambda i, te, nu: (i, 0)),
        scratch_shapes=[pltpu.VMEM((d, f2), BF16), pltpu.VMEM((fdim, d), BF16)],
    )
    return pl.pallas_call(
        _moe_ffn_kernel,
        grid_spec=grid_spec,
        out_shape=jax.ShapeDtypeStruct((n_pad, d), F32),
        compiler_params=_cparams(("arbitrary",)),
        name="moe_ffn",
    )(tile_expert, n_used, xs, w_gate_up, w_down)


def _moe_combine_kernel(pos_ref, x_ref, gate_ref, r_ref, y_hbm, o_ref, ybuf, sem):
    tm = x_ref.shape[0]
    _issue_row_copies(2 * tm, lambda r: pltpu.make_async_copy(
        y_hbm.at[pl.ds(pos_ref[0, 0, r], 1)], ybuf.at[pl.ds(r, 1)], sem))
    pltpu.make_async_copy(y_hbm.at[pl.ds(0, 2 * tm)], ybuf, sem).wait()
    route = r_ref[...]
    y = route[:, 2:3] * ybuf[0:tm, :] + route[:, 3:4] * ybuf[tm:2 * tm, :]
    o_ref[...] = x_ref[...] + gate_ref[...] * y


def moe_combine(pos_tiles, xa, gate, route, y_sorted, tm, seq, nb):
    ta, d = xa.shape
    return pl.pallas_call(
        _moe_combine_kernel,
        grid=(ta // tm,),
        in_specs=[
            pl.BlockSpec((1, 1, 2 * tm), lambda i: (i, 0, 0), memory_space=pltpu.SMEM),
            pl.BlockSpec((tm, d), lambda i: (i, 0)),
            pl.BlockSpec((None, 1, d), _mod_row_map(tm, seq, nb)),
            pl.BlockSpec((tm, LANES), lambda i: (i, 0)),
            pl.BlockSpec(memory_space=pl.ANY),
        ],
        out_specs=pl.BlockSpec((tm, d), lambda i: (i, 0)),
        out_shape=jax.ShapeDtypeStruct((ta, d), F32),
        scratch_shapes=[pltpu.VMEM((2 * tm, d), F32), pltpu.SemaphoreType.DMA(())],
        compiler_params=_cparams(("arbitrary",)),
        name="moe_combine",
    )(pos_tiles, xa, gate, route, y_sorted)


def moe_slots(route, counts, tm_ffn, tm_tok):
    ta = route.shape[0]
    ids = route[:, 0:TOP_K].astype(jnp.int32)
    rank = route[:, 2 * TOP_K:3 * TOP_K].astype(jnp.int32)
    counts = counts[0, N_GROUPS:N_GROUPS + N_EXPERTS].astype(jnp.int32)
    padded = ((counts + tm_ffn - 1) // tm_ffn) * tm_ffn
    ends = jnp.cumsum(padded)
    starts = ends - padded
    pos = starts[ids] + rank
    n_tiles = (TOP_K * ta + N_EXPERTS * (tm_ffn - 1)) // tm_ffn
    tile_start = jnp.arange(n_tiles, dtype=jnp.int32) * tm_ffn
    tile_expert = jnp.sum((tile_start[:, None] >= ends[None, :]).astype(jnp.int32), axis=1)
    tile_expert = jnp.minimum(tile_expert, N_EXPERTS - 1)
    n_used = (ends[-1] // tm_ffn).astype(jnp.int32).reshape(1)
    pos_tiles = pos.reshape(ta // tm_tok, tm_tok, TOP_K).transpose(0, 2, 1).reshape(ta // tm_tok, 1, TOP_K * tm_tok)
    return tile_expert, n_used, n_tiles * tm_ffn, pos_tiles


def _seq_flags(t_lat, seq, tc, cseq, tm):
    starts = np.arange(0, t_lat + tc, tm)
    first = np.where(starts < t_lat, starts % seq == 0, (starts - t_lat) % cseq == 0)
    ends = starts + tm
    last = np.where(starts < t_lat, ends % seq == 0, (ends - t_lat) % cseq == 0)
    return jnp.asarray(first.astype(np.int32)), jnp.asarray(last.astype(np.int32))


def kernel(x, c, ctx, c_ctx, w_ada, b_ada, norm_mix, norm_ffn, ev_w_in, ev_q_gain, ev_k_gain, ev_decay_f,
           ev_decay_b, ev_w_out, od_w_in, od_conv, od_a_log_f, od_a_log_b, od_dt_bias_f, od_dt_bias_b,
           od_out_gain, od_w_out, moe_w_group, moe_b_group, moe_w_expert, moe_b_expert, moe_w_gate_up,
           moe_w_down, final_norm_gain):
    nb, seq, d = x.shape
    cseq = ctx.shape[1]
    depth = w_ada.shape[0]
    t_lat = nb * seq
    tc = nb * cseq
    assert nb + 1 <= 8 and seq % cseq == 0 and cseq % RET_CHUNK == 0 and seq % GRID_W == 0

    tm = 512 if tc % 512 == 0 else cseq
    tm_prep = min(256, cseq)
    tq = min(256, cseq)
    tk = min(256, cseq)
    tl = 2 * DN_CHUNK
    tm_ffn = 256
    tm_comb = min(256, cseq)

    xa = jnp.concatenate([x.reshape(t_lat, d), ctx.reshape(tc, d)], axis=0)
    c8 = jnp.zeros((8, d), F32).at[:nb].set(c).at[nb].set(c_ctx)
    mod = adaln(c8, w_ada, b_ada)

    tabs = rope_tables(seq, tm_prep)
    first_flags, last_flags = _seq_flags(t_lat, seq, tc, cseq, tm_prep)
    ret_zero = jnp.zeros((nb, RET_HEADS, RET_DK, RET_DV), F32)
    dn_zero = jnp.zeros((nb, DN_HEADS, DN_DK, DN_DV), F32)

    for layer in range(depth):
        last = layer == depth - 1
        m = mod[layer].reshape(8, 6, 1, d)
        sh1, sc1, g1, sh2, sc2, g2 = (m[:, j] for j in range(6))
        i = layer // 2
        if layer % 2 == 0:
            w_in = ev_w_in[i].astype(BF16)
            p = norm_mod_matmul(xa, norm_mix[layer], sh1, sc1, w_in, tm, seq, nb)
            rq, rk, aq, ak, av = prep_even(p, tabs, ev_q_gain[i], ev_k_gain[i], tm_prep, seq, t_lat)
            dec = jnp.stack([ev_decay_f[i], ev_decay_b[i]]).astype(F32)
            oc, scf, scb = retention(dec, rq, rk, p, ret_zero, ret_zero, nb, cseq, t_lat // cseq)
            ol, _, _ = retention(dec, rq, rk, p, scf, scb, nb, seq, 0)
            kcat = jnp.concatenate([ak[:, :t_lat].reshape(ATT_KV_HEADS, nb, seq, ATT_HD),
                                    ak[:, t_lat:].reshape(ATT_KV_HEADS, nb, cseq, ATT_HD)], axis=2)
            vcat = jnp.concatenate([av[:, :t_lat].reshape(ATT_KV_HEADS, nb, seq, ATT_HD),
                                    av[:, t_lat:].reshape(ATT_KV_HEADS, nb, cseq, ATT_HD)], axis=2)
            lk = seq + cseq
            vtcat = jnp.concatenate([vcat.transpose(0, 1, 3, 2),
                                     jnp.ones((ATT_KV_HEADS, nb, ATT_VT_ROWS - ATT_HD, lk), BF16)], axis=2)
            vtcat = vtcat.reshape(ATT_KV_HEADS, nb, ATT_VT_ROWS, lk // tk, tk).transpose(0, 1, 3, 2, 4)
            att_l = attention(aq, kcat, vtcat, seq, 0, 0, tq, tk)
            att_c = attention(aq, kcat, vtcat, cseq, t_lat // tq, seq // tk, tq, tk)
            w_out = ev_w_out[i].astype(BF16)
            k1 = RET_HEADS * RET_DV
            xa = outproj_even(ol, oc, att_l, att_c, w_out[:k1], w_out[k1:], xa, g1, tm, seq, nb)
        else:
            w_in = jnp.pad(od_w_in[i], ((0, 0), (0, ODD_IN_PAD - ODD_IN))).astype(BF16)
            p = norm_mod_matmul(xa, norm_mix[layer], sh1, sc1, w_in, tm, seq, nb)
            zpad = jnp.zeros((LANES - 2 * DN_HEADS,), F32)
            arow = jnp.concatenate([od_a_log_f[i], od_a_log_b[i], zpad]).reshape(1, LANES).astype(F32)
            brow = jnp.concatenate([od_dt_bias_f[i], od_dt_bias_b[i], zpad]).reshape(1, LANES).astype(F32)
            q, k, v, gb = prep_odd(p, od_conv[i].astype(F32), arow, brow, first_flags, last_flags, tm_prep)
            ta = t_lat + tc
            gbt = gb.reshape(ta // DN_CHUNK, DN_CHUNK, LANES)[:, :, :4 * DN_HEADS].transpose(0, 2, 1)
            oc_f, oc_b, sc_f, sc_b = deltanet_bidir(q, k, v, gb, gbt, dn_zero, dn_zero, nb, cseq, t_lat, tl)
            ol_f, ol_b, _, _ = deltanet_bidir(q, k, v, gb, gbt, sc_f, sc_b, nb, seq, 0, tl)
            xa = outproj_odd(ol_f, oc_f, ol_b, oc_b, p, od_out_gain[i], od_w_out[i].astype(BF16), xa, g1, tm, seq,
                             nb)

        w_router = jnp.pad(jnp.concatenate([moe_w_group[layer], moe_w_expert[layer]], axis=1),
                           ((0, 0), (0, LANES - N_GROUPS - N_EXPERTS))).astype(BF16)
        b_router = jnp.pad(jnp.concatenate([moe_b_group[layer], moe_b_expert[layer]]),
                           (0, LANES - N_GROUPS - N_EXPERTS)).reshape(1, LANES).astype(F32)
        f, route, counts = moe_router(xa, norm_ffn[layer], sh2, sc2, w_router, b_router, tm, seq, nb)
        tile_expert, n_used, n_pad, pos_tiles = moe_slots(route, counts, tm_ffn, tm_comb)
        xs = moe_scatter(pos_tiles, f, jnp.zeros((n_pad, d), F32), tm_comb)
        y_sorted = moe_ffn(tile_expert, n_used, xs, moe_w_gate_up[layer], moe_w_down[layer], tm_ffn)
        xa = moe_combine(pos_tiles, xa, g2, route, y_sorted, tm_comb, seq, nb)

    out = final_norm(xa, final_norm_gain, t_lat, tm)
    return out.reshape(nb, seq, d)
```

```python
import functools
import math

import numpy as np
import jax
import jax.numpy as jnp
from jax import lax
from jax.experimental import pallas as pl
from jax.experimental.pallas import tpu as pltpu

F32 = jnp.float32
BF16 = jnp.bfloat16
HIGHEST = lax.Precision.HIGHEST

EPS = 1e-6
GRID_W = 64
ROPE_BASE = 10000.0
RET_HEADS, RET_DK, RET_DV, RET_CHUNK = 8, 64, 128, 128
ATT_HEADS, ATT_KV_HEADS, ATT_HD = 8, 2, 64
DN_HEADS, DN_DK, DN_DV, DN_CHUNK, DN_CONV = 8, 128, 128, 64, 3
N_GROUPS, EXPERTS_PER_GROUP, TOP_K = 4, 8, 2
N_EXPERTS = N_GROUPS * EXPERTS_PER_GROUP

EVEN_IN = 2 * RET_HEADS * RET_DK + 2 * RET_HEADS * RET_DV + (ATT_HEADS + 2 * ATT_KV_HEADS) * ATT_HD
EVEN_ATT_COL = 2 * RET_HEADS * RET_DK + 2 * RET_HEADS * RET_DV
EVEN_ATT_W = (ATT_HEADS + 2 * ATT_KV_HEADS) * ATT_HD
DN_QKV = 2 * DN_HEADS * DN_DK + DN_HEADS * DN_DV
ODD_IN = DN_QKV + DN_HEADS * DN_DV + 4 * DN_HEADS
ODD_IN_PAD = ((ODD_IN + 127) // 128) * 128

LANES = 128
VMEM_LIMIT = 56 * 1024 * 1024

NT_DIMS = (((1,), (1,)), ((), ()))
TN_DIMS = (((0,), (0,)), ((), ()))


def _cparams(sem):
    return pltpu.CompilerParams(dimension_semantics=sem, vmem_limit_bytes=VMEM_LIMIT)


def _silu(x):
    return x / (1.0 + jnp.exp(-x))


def _dot(a, b):
    return jnp.dot(a, b, preferred_element_type=F32)


def _adaln_kernel(c_ref, w_ref, b_ref, o_ref):
    s = _silu(c_ref[...])
    o_ref[...] = _dot(s.astype(BF16), w_ref[...].astype(BF16)) + b_ref[...]


def adaln(c8, w_ada, b_ada):
    depth, d, n6 = w_ada.shape
    tn = min(n6, 1536)
    return pl.pallas_call(
        _adaln_kernel,
        grid=(depth, n6 // tn),
        in_specs=[
            pl.BlockSpec((8, d), lambda l, j: (0, 0)),
            pl.BlockSpec((None, d, tn), lambda l, j: (l, 0, j)),
            pl.BlockSpec((None, 1, tn), lambda l, j: (l, 0, j)),
        ],
        out_specs=pl.BlockSpec((None, 8, tn), lambda l, j: (l, 0, j)),
        out_shape=jax.ShapeDtypeStruct((depth, 8, n6), F32),
        compiler_params=_cparams(("arbitrary", "arbitrary")),
        name="adaln",
    )(c8, w_ada, b_ada.reshape(depth, 1, n6))


def _norm_mod(x, gain, shift, scale):
    ms = jnp.mean(x * x, axis=-1, keepdims=True)
    h = x * lax.rsqrt(ms + EPS) * gain
    return h * (1.0 + scale) + shift


def _nmm_kernel(x_ref, g_ref, sh_ref, sc_ref, w_ref, o_ref, *, nchunk):
    hb = _norm_mod(x_ref[...], g_ref[...], sh_ref[...], sc_ref[...]).astype(BF16)
    n = o_ref.shape[-1]
    for n0 in range(0, n, nchunk):
        n1 = min(n0 + nchunk, n)
        o_ref[:, n0:n1] = _dot(hb, w_ref[:, n0:n1]).astype(o_ref.dtype)


def _mod_row_map(tm, seq, n_lat_batches):
    return lambda i: (jnp.minimum((i * tm) // seq, n_lat_batches), 0, 0)


def norm_mod_matmul(xa, gain, shift, scale, w, tm, seq, nb):
    ta, d = xa.shape
    n = w.shape[1]
    nchunk = 512
    mrow = _mod_row_map(tm, seq, nb)
    return pl.pallas_call(
        functools.partial(_nmm_kernel, nchunk=nchunk),
        grid=(ta // tm,),
        in_specs=[
            pl.BlockSpec((tm, d), lambda i: (i, 0)),
            pl.BlockSpec((1, d), lambda i: (0, 0)),
            pl.BlockSpec((None, 1, d), mrow),
            pl.BlockSpec((None, 1, d), mrow),
            pl.BlockSpec((d, n), lambda i: (0, 0)),
        ],
        out_specs=pl.BlockSpec((tm, n), lambda i: (i, 0)),
        out_shape=jax.ShapeDtypeStruct((ta, n), BF16),
        compiler_params=_cparams(("arbitrary",)),
        name="norm_mod_matmul",
    )(xa, gain.reshape(1, d), shift, scale, w)


def _final_norm_kernel(x_ref, g_ref, o_ref):
    x = x_ref[...]
    ms = jnp.mean(x * x, axis=-1, keepdims=True)
    o_ref[...] = x * lax.rsqrt(ms + EPS) * g_ref[...]


def final_norm(xa, gain, t_rows, tm):
    d = xa.shape[1]
    return pl.pallas_call(
        _final_norm_kernel,
        grid=(t_rows // tm,),
        in_specs=[pl.BlockSpec((tm, d), lambda i: (i, 0)), pl.BlockSpec((1, d), lambda i: (0, 0))],
        out_specs=pl.BlockSpec((tm, d), lambda i: (i, 0)),
        out_shape=jax.ShapeDtypeStruct((t_rows, d), F32),
        compiler_params=_cparams(("arbitrary",)),
        name="final_norm",
    )(xa, gain.reshape(1, d))


def _inproj_even_kernel(x_ref, g_ref, sh_ref, sc_ref, w_ref, cos_ref, s1_ref, s2_ref, qg_ref, kg_ref, bd_ref,
                        rq_ref, rk_ref, vg_ref, aq_ref, ak_ref, av_ref):
    hb = _norm_mod(x_ref[...], g_ref[...], sh_ref[...], sc_ref[...]).astype(BF16)
    cos = cos_ref[...]
    s1 = s1_ref[...]
    s2 = s2_ref[...]
    bd = bd_ref[...]
    half = ATT_HD

    def proj(c0, width=LANES):
        return _dot(hb, w_ref[:, c0:c0 + width])

    def rope(x):
        return x * cos + pltpu.roll(x, LANES - 16, 1) * s1 + pltpu.roll(x, 16, 1) * s2

    def head_norm(x, gain):
        sq = x * x
        hi = sq.astype(BF16)
        lo = (sq - hi.astype(F32)).astype(BF16)
        ms = _dot(hi, bd) + _dot(lo, bd)
        return x * lax.rsqrt(ms + EPS) * gain

    wide = 2 * LANES
    qw = RET_HEADS * RET_DK
    for c0 in range(0, qw, wide):
        yq = proj(c0, wide)
        yk = proj(qw + c0, wide)
        for u in range(2):
            cs = slice(c0 + u * LANES, c0 + (u + 1) * LANES)
            us = slice(u * LANES, (u + 1) * LANES)
            rq_ref[:, cs] = rope(yq[:, us]).astype(BF16)
            rk_ref[:, cs] = (rope(yk[:, us]) * RET_DK ** -0.5).astype(BF16)
    vgw = 2 * RET_HEADS * RET_DV
    for c0 in range(0, vgw, 512):
        vg_ref[:, c0:c0 + 512] = proj(2 * qw + c0, 512).astype(BF16)

    qg = qg_ref[...]
    kg = kg_ref[...]
    a0 = EVEN_ATT_COL
    for c0 in range(0, ATT_HEADS * ATT_HD, wide):
        ya = proj(a0 + c0, wide)
        for u in range(2):
            y = rope(head_norm(ya[:, u * LANES:(u + 1) * LANES], qg)) * (ATT_HD ** -0.5 * math.log2(math.e))
            y = y.astype(BF16)
            hd0 = (c0 + u * LANES) // ATT_HD
            aq_ref[hd0] = y[:, :half]
            aq_ref[hd0 + 1] = y[:, half:]
    ykv = proj(a0 + ATT_HEADS * ATT_HD, wide)
    y = rope(head_norm(ykv[:, :LANES], kg)).astype(BF16)
    ak_ref[0] = y[:, :half]
    ak_ref[1] = y[:, half:]
    v = ykv[:, LANES:].astype(BF16)
    av_ref[0] = v[:, :half]
    av_ref[1] = v[:, half:]


def inproj_even(xa, gain, shift, scale, w, tabs, q_gain, k_gain, tm, seq, nb):
    ta, d = xa.shape
    t_lat = nb * seq
    cos_t, s1_t, s2_t = tabs
    n_tab = seq // tm

    def tab_map(i):
        r = i * tm
        return (jnp.where(r < t_lat, (r % seq) // tm, n_tab), 0)

    ii = np.arange(LANES)
    bd = jnp.asarray((ii[:, None] // ATT_HD == ii[None, :] // ATT_HD).astype(np.float32) / ATT_HD).astype(BF16)
    qg = jnp.tile(q_gain.astype(F32), LANES // ATT_HD).reshape(1, LANES)
    kg = jnp.tile(k_gain.astype(F32), LANES // ATT_HD).reshape(1, LANES)
    mrow = _mod_row_map(tm, seq, nb)
    tab_spec = pl.BlockSpec((tm, LANES), tab_map)
    one = lambda i: (0, 0)
    qw = RET_HEADS * RET_DK
    vgw = 2 * RET_HEADS * RET_DV
    return pl.pallas_call(
        _inproj_even_kernel,
        grid=(ta // tm,),
        in_specs=[
            pl.BlockSpec((tm, d), lambda i: (i, 0)),
            pl.BlockSpec((1, d), one),
            pl.BlockSpec((None, 1, d), mrow),
            pl.BlockSpec((None, 1, d), mrow),
            pl.BlockSpec((d, EVEN_IN), one),
            tab_spec, tab_spec, tab_spec,
            pl.BlockSpec((1, LANES), one), pl.BlockSpec((1, LANES), one),
            pl.BlockSpec((LANES, LANES), one),
        ],
        out_specs=[
            pl.BlockSpec((tm, qw), lambda i: (i, 0)),
            pl.BlockSpec((tm, qw), lambda i: (i, 0)),
            pl.BlockSpec((tm, vgw), lambda i: (i, 0)),
            pl.BlockSpec((ATT_HEADS, tm, ATT_HD), lambda i: (0, i, 0)),
            pl.BlockSpec((ATT_KV_HEADS, tm, ATT_HD), lambda i: (0, i, 0)),
            pl.BlockSpec((ATT_KV_HEADS, tm, ATT_HD), lambda i: (0, i, 0)),
        ],
        out_shape=[
            jax.ShapeDtypeStruct((ta, qw), BF16),
            jax.ShapeDtypeStruct((ta, qw), BF16),
            jax.ShapeDtypeStruct((ta, vgw), BF16),
            jax.ShapeDtypeStruct((ATT_HEADS, ta, ATT_HD), BF16),
            jax.ShapeDtypeStruct((ATT_KV_HEADS, ta, ATT_HD), BF16),
            jax.ShapeDtypeStruct((ATT_KV_HEADS, ta, ATT_HD), BF16),
        ],
        compiler_params=_cparams(("arbitrary",)),
        name="inproj_even",
    )(xa, gain.reshape(1, d), shift, scale, w, cos_t, s1_t, s2_t, qg, kg, bd)


def rope_tables(seq, tm):
    nf = ATT_HD // 4
    t = jnp.arange(seq)
    rows = (t // GRID_W).astype(F32)
    cols = (t % GRID_W).astype(F32)
    inv = ROPE_BASE ** (-jnp.arange(nf, dtype=F32) / nf)
    lane = np.arange(LANES)
    axis = (lane % ATT_HD) // (ATT_HD // 2)
    f = lane % nf
    upper = ((lane % (ATT_HD // 2)) >= nf)
    pos = jnp.where(jnp.asarray(axis)[None, :] == 0, rows[:, None], cols[:, None])
    ang = pos * inv[jnp.asarray(f)][None, :]
    cos = jnp.cos(ang)
    sin = jnp.sin(ang)
    s1 = jnp.where(jnp.asarray(upper)[None, :], 0.0, -sin)
    s2 = jnp.where(jnp.asarray(upper)[None, :], sin, 0.0)
    pad1 = jnp.ones((tm, LANES), F32)
    pad0 = jnp.zeros((tm, LANES), F32)
    return (jnp.concatenate([cos, pad1]), jnp.concatenate([s1, pad0]), jnp.concatenate([s2, pad0]))


def _retention_kernel(dec_ref, q_ref, k_ref, v_ref, g_ref, s0f_ref, s0b_ref,
                      o_ref, sff_ref, sfb_ref, st_ref, *, n_chunks, unroll):
    hp = pl.program_id(1)
    C = RET_CHUNK
    dk, dv = RET_DK, RET_DV
    pos = lax.broadcasted_iota(jnp.int32, (C, dk), 0).astype(F32)
    ii = lax.broadcasted_iota(jnp.int32, (C, C), 0)
    jj = lax.broadcasted_iota(jnp.int32, (C, C), 1)
    dpos = (ii - jj).astype(F32)
    heads = range(2)
    qs = [slice(hh * dk, (hh + 1) * dk) for hh in heads]
    vs = [slice(hh * dv, (hh + 1) * dv) for hh in heads]
    w_out, w_in, gcf, gcb, mask = [], [], [], [], []
    for hh in heads:
        h = 2 * hp + hh
        df = dec_ref[0, h]
        db = dec_ref[1, h]
        lf = -jnp.exp(jnp.full((C, C), df, F32))
        lb = -jnp.exp(jnp.full((C, C), db, F32))
        lfk = -jnp.exp(jnp.full((C, dk), df, F32))
        lbk = -jnp.exp(jnp.full((C, dk), db, F32))
        w_out.append(jnp.concatenate([jnp.exp(lfk * (C - 1.0 - pos)), jnp.exp(lbk * pos)], axis=1))
        w_in.append(jnp.concatenate([jnp.exp(lfk * (pos + 1.0)), jnp.exp(lbk * (C - pos))], axis=1))
        gcf.append(jnp.exp(-jnp.exp(jnp.full((dk, dv), df, F32)) * C))
        gcb.append(jnp.exp(-jnp.exp(jnp.full((dk, dv), db, F32)) * C))
        mask.append(jnp.where(dpos > 0, jnp.exp(lf * jnp.maximum(dpos, 0.0)),
                              jnp.where(dpos < 0, jnp.exp(lb * jnp.maximum(-dpos, 0.0)), 2.0)))

    def rows(n):
        return pl.ds(pl.multiple_of(n * C, C), C)

    items = [(u, hh) for u in range(unroll) for hh in heads]

    def sums_body(i, carry):
        kk = {}
        for u, hh in items:
            k = k_ref[rows(i * unroll + u), qs[hh]].astype(F32)
            kk[(u, hh)] = (jnp.concatenate([k, k], axis=1) * w_out[hh]).astype(BF16)
        kv = {(u, hh): lax.dot_general(kk[(u, hh)], v_ref[rows(i * unroll + u), vs[hh]], TN_DIMS,
                                       preferred_element_type=F32) for u, hh in items}
        for u, hh in items:
            st_ref[hh, i * unroll + u] = kv[(u, hh)]
        return carry

    lax.fori_loop(0, n_chunks // unroll, sums_body, 0)

    def scan_body(n, carry):
        n_rev = n_chunks - 1 - n
        out = []
        for hh in heads:
            sf, sb = carry[2 * hh], carry[2 * hh + 1]
            kvf = st_ref[hh, n, 0:dk, :]
            kvb = st_ref[hh, n_rev, dk:2 * dk, :]
            st_ref[hh, n, 0:dk, :] = sf
            st_ref[hh, n_rev, dk:2 * dk, :] = sb
            out += [gcf[hh] * sf + kvf, gcb[hh] * sb + kvb]
        return tuple(out)

    init = tuple(x for hh in heads for x in (s0f_ref[hh], s0b_ref[hh]))
    fin = lax.fori_loop(0, n_chunks, scan_body, init)
    for hh in heads:
        sff_ref[hh] = fin[2 * hh]
        sfb_ref[hh] = fin[2 * hh + 1]

    def out_body(i, carry):
        ns = [i * unroll + u for u in range(unroll)]
        qb = {(u, hh): q_ref[rows(ns[u]), qs[hh]] for u, hh in items}
        sc = {(u, hh): lax.dot_general(qb[(u, hh)], k_ref[rows(ns[u]), qs[hh]], NT_DIMS,
                                       preferred_element_type=F32) for u, hh in items}
        qw = {}
        for it in items:
            q = qb[it].astype(F32)
            qw[it] = (jnp.concatenate([q, q], axis=1) * w_in[it[1]]).astype(BF16)
        o1 = {(u, hh): _dot((sc[(u, hh)] * mask[hh]).astype(BF16), v_ref[rows(ns[u]), vs[hh]]) for u, hh in items}
        o2 = {(u, hh): _dot(qw[(u, hh)], st_ref[hh, ns[u]].astype(BF16)) for u, hh in items}
        for it in items:
            u, hh = it
            n = ns[u]
            o = o1[it] + o2[it]
            o = o * lax.rsqrt(jnp.mean(o * o, axis=-1, keepdims=True) + EPS)
            gate = g_ref[rows(n), vs[hh]].astype(F32)
            o_ref[rows(n), vs[hh]] = (_silu(gate) * o).astype(o_ref.dtype)
        return carry

    lax.fori_loop(0, n_chunks // unroll, out_body, 0)


def retention(dec, rq, rk, p, s0f, s0b, nb, seq, row_off_blocks):
    n_chunks = seq // RET_CHUNK
    hp_n = RET_HEADS // 2
    vcol = 0
    gcol = vcol + RET_HEADS * RET_DV // (2 * RET_DV)
    ta = rq.shape[0]
    st_spec = pl.BlockSpec((None, 2, RET_DK, RET_DV), lambda b, hp, *_: (b, hp, 0, 0))
    grid_spec = pltpu.PrefetchScalarGridSpec(
        num_scalar_prefetch=1,
        grid=(nb, hp_n),
        in_specs=[
            pl.BlockSpec((seq, 2 * RET_DK), lambda b, hp, *_: (row_off_blocks + b, hp)),
            pl.BlockSpec((seq, 2 * RET_DK), lambda b, hp, *_: (row_off_blocks + b, hp)),
            pl.BlockSpec((seq, 2 * RET_DV), lambda b, hp, *_: (row_off_blocks + b, vcol + hp)),
            pl.BlockSpec((seq, 2 * RET_DV), lambda b, hp, *_: (row_off_blocks + b, gcol + hp)),
            st_spec, st_spec,
        ],
        out_specs=[
            pl.BlockSpec((seq, 2 * RET_DV), lambda b, hp, *_: (b, hp)),
            st_spec, st_spec,
        ],
        scratch_shapes=[pltpu.VMEM((2, n_chunks, 2 * RET_DK, RET_DV), F32)],
    )
    st_shape = jax.ShapeDtypeStruct((nb, RET_HEADS, RET_DK, RET_DV), F32)
    return pl.pallas_call(
        functools.partial(_retention_kernel, n_chunks=n_chunks, unroll=math.gcd(n_chunks, 4)),
        grid_spec=grid_spec,
        out_shape=[jax.ShapeDtypeStruct((nb * seq, RET_HEADS * RET_DV), BF16), st_shape, st_shape],
        compiler_params=_cparams(("arbitrary", "arbitrary")),
        name="retention",
    )(dec, rq, rk, p, p, s0f, s0b)


ATT_VT_ROWS = ATT_HD + 16


ATT_PAIRS_PER_TRIP = 4


def _attn_kernel(q_ref, k_ref, vt_ref, o_ref, *s_refs, tk, c_start, c_end, rep):
    tq = q_ref.shape[1]
    sets = (s_refs[:rep], s_refs[rep:])
    last = c_end - 1

    def scores(bufs, j):
        j = jnp.minimum(j, last)
        c0 = pl.multiple_of(j * tk, tk)
        k = k_ref[pl.ds(c0, tk), :]
        mxs = []
        for r in range(rep):
            s = lax.dot_general(k, q_ref[r], NT_DIMS, preferred_element_type=F32)
            bufs[r][...] = s
            mxs.append(jnp.max(s, axis=0, keepdims=True))
        return tuple(mxs)

    def softmax_pv(bufs, j, mxs, ms, accs):
        vt = vt_ref[j]
        new_m, new_acc = [], []
        for r in range(rep):
            m_new = jnp.maximum(ms[r], mxs[r])
            a = jnp.exp2(ms[r] - m_new)
            p = jnp.exp2(bufs[r][...] - m_new).astype(BF16)
            new_acc.append(a * accs[r] + _dot(vt, p))
            new_m.append(m_new)
        return tuple(new_m), tuple(new_acc)

    def pair(j, mx0, ms, accs):
        mx1 = scores(sets[1], j + 1)
        ms, accs = softmax_pv(sets[0], j, mx0, ms, accs)
        mx0 = scores(sets[0], j + 2)
        ms, accs = softmax_pv(sets[1], j + 1, mx1, ms, accs)
        return mx0, ms, accs

    def trip(t, carry):
        for u in range(ATT_PAIRS_PER_TRIP):
            carry = pair(c_start + 2 * (ATT_PAIRS_PER_TRIP * t + u), *carry)
        return carry

    n_pairs = (c_end - c_start) // 2
    n_trips = n_pairs // ATT_PAIRS_PER_TRIP
    ms = tuple(jnp.full((1, tq), -1e30, F32) for _ in range(rep))
    accs = tuple(jnp.zeros((ATT_VT_ROWS, tq), F32) for _ in range(rep))
    carry = (scores(sets[0], c_start), ms, accs)
    if n_trips:
        carry = lax.fori_loop(0, n_trips, trip, carry)
    for u in range(n_trips * ATT_PAIRS_PER_TRIP, n_pairs):
        carry = pair(c_start + 2 * u, *carry)
    mx0, ms, accs = carry
    if (c_end - c_start) % 2:
        ms, accs = softmax_pv(sets[0], last, mx0, ms, accs)
    outs = [(acc[:ATT_HD, :] / acc[ATT_HD:ATT_HD + 1, :]).T for acc in accs]
    o_ref[...] = jnp.concatenate(outs, axis=-1).astype(o_ref.dtype)


def attention(aq, kcat, vtcat, seq_q, q_off_blocks, c_start, tq, tk):
    rep = ATT_HEADS // ATT_KV_HEADS
    _, nb, lk, _ = kcat.shape
    nq = seq_q // tq
    n_chunks = lk // tk
    return pl.pallas_call(
        functools.partial(_attn_kernel, tk=tk, c_start=c_start, c_end=n_chunks, rep=rep),
        grid=(nb, ATT_KV_HEADS, nq),
        in_specs=[
            pl.BlockSpec((rep, tq, ATT_HD), lambda b, g, i: (g, q_off_blocks + b * nq + i, 0)),
            pl.BlockSpec((None, None, lk, ATT_HD), lambda b, g, i: (g, b, 0, 0)),
            pl.BlockSpec((None, None, n_chunks, ATT_VT_ROWS, tk), lambda b, g, i: (g, b, 0, 0, 0)),
        ],
        out_specs=pl.BlockSpec((tq, rep * ATT_HD), lambda b, g, i: (b * nq + i, g)),
        out_shape=jax.ShapeDtypeStruct((nb * seq_q, ATT_HEADS * ATT_HD), BF16),
        scratch_shapes=[pltpu.VMEM((tk, tq), F32) for _ in range(2 * rep)],
        compiler_params=_cparams(("arbitrary", "arbitrary", "arbitrary")),
        name="attention",
    )(aq, kcat, vtcat)


def _lat_ctx_specs(tm, width, n_lat_tiles):
    return [pl.BlockSpec((tm, width), lambda i: (jnp.minimum(i, n_lat_tiles - 1), 0)),
            pl.BlockSpec((tm, width), lambda i: (jnp.maximum(i - n_lat_tiles, 0), 0))]


def _outproj_even_kernel(r_lat, r_ctx, a_lat, a_ctx, w1_ref, w2_ref, res_ref, gate_ref, o_ref, *, n_lat_tiles):
    is_lat = pl.program_id(0) < n_lat_tiles
    a1 = jnp.where(is_lat, r_lat[...], r_ctx[...])
    a2 = jnp.where(is_lat, a_lat[...], a_ctx[...])
    y = _dot(a1, w1_ref[...]) + _dot(a2, w2_ref[...])
    o_ref[...] = res_ref[...] + gate_ref[...] * y


def outproj_even(ret_lat, ret_ctx, att_lat, att_ctx, w1, w2, xa, gate, tm, seq, nb):
    ta, d = xa.shape
    k1, k2 = w1.shape[0], w2.shape[0]
    n_lat_tiles = ret_lat.shape[0] // tm
    return pl.pallas_call(
        functools.partial(_outproj_even_kernel, n_lat_tiles=n_lat_tiles),
        grid=(ta // tm,),
        in_specs=_lat_ctx_specs(tm, k1, n_lat_tiles) + _lat_ctx_specs(tm, k2, n_lat_tiles) + [
            pl.BlockSpec((k1, d), lambda i: (0, 0)),
            pl.BlockSpec((k2, d), lambda i: (0, 0)),
            pl.BlockSpec((tm, d), lambda i: (i, 0)),
            pl.BlockSpec((None, 1, d), _mod_row_map(tm, seq, nb)),
        ],
        out_specs=pl.BlockSpec((tm, d), lambda i: (i, 0)),
        out_shape=jax.ShapeDtypeStruct((ta, d), F32),
        compiler_params=_cparams(("arbitrary",)),
        name="outproj_even",
    )(ret_lat, ret_ctx, att_lat, att_ctx, w1, w2, xa, gate)


def _outproj_odd_kernel(f_lat, f_ctx, b_lat, b_ctx, z_ref, og_ref, w_ref, res_ref, gate_ref, o_ref,
                        *, n_lat_tiles):
    is_lat = pl.program_id(0) < n_lat_tiles
    og = og_ref[...]
    parts = []
    for h in range(DN_HEADS):
        cs = slice(h * DN_DV, (h + 1) * DN_DV)
        of = jnp.where(is_lat, f_lat[:, cs], f_ctx[:, cs]).astype(F32)
        ob = jnp.where(is_lat, b_lat[:, cs], b_ctx[:, cs]).astype(F32)
        o = of + ob
        o = o * lax.rsqrt(jnp.mean(o * o, axis=-1, keepdims=True) + EPS) * og
        parts.append((o * _silu(z_ref[:, cs].astype(F32))).astype(BF16))
    a = jnp.concatenate(parts, axis=-1)
    o_ref[...] = res_ref[...] + gate_ref[...] * _dot(a, w_ref[...])


def outproj_odd(of_lat, of_ctx, ob_lat, ob_ctx, p, out_gain, w, xa, gate, tm, seq, nb):
    ta, d = xa.shape
    kdim = DN_HEADS * DN_DV
    n_lat_tiles = of_lat.shape[0] // tm
    return pl.pallas_call(
        functools.partial(_outproj_odd_kernel, n_lat_tiles=n_lat_tiles),
        grid=(ta // tm,),
        in_specs=_lat_ctx_specs(tm, kdim, n_lat_tiles) + _lat_ctx_specs(tm, kdim, n_lat_tiles) + [
            pl.BlockSpec((tm, kdim), lambda i: (i, DN_QKV // kdim)),
            pl.BlockSpec((1, DN_DV), lambda i: (0, 0)),
            pl.BlockSpec((kdim, d), lambda i: (0, 0)),
            pl.BlockSpec((tm, d), lambda i: (i, 0)),
            pl.BlockSpec((None, 1, d), _mod_row_map(tm, seq, nb)),
        ],
        out_specs=pl.BlockSpec((tm, d), lambda i: (i, 0)),
        out_shape=jax.ShapeDtypeStruct((ta, d), F32),
        compiler_params=_cparams(("arbitrary",)),
        name="outproj_odd",
    )(of_lat, of_ctx, ob_lat, ob_ctx, p, out_gain.reshape(1, DN_DV).astype(F32), w, xa, gate)


def _prep_odd_kernel(first_ref, last_ref, x_ref, prev_ref, next_ref, ab_ref, cw_ref, arow_ref, brow_ref,
                     q_ref, k_ref, v_ref, gb_ref):
    i = pl.program_id(0)
    tm = x_ref.shape[0]
    hrows = prev_ref.shape[0]
    keep_prev = 1.0 - first_ref[i].astype(F32)
    keep_next = 1.0 - last_ref[i].astype(F32)
    row = lax.broadcasted_iota(jnp.int32, (tm, LANES), 0)
    is_first = row == 0
    is_last = row == tm - 1
    n_qk = 2 * DN_HEADS * DN_DK // LANES
    n_q = DN_HEADS * DN_DK // LANES
    outs = (q_ref, k_ref, v_ref)
    for j in range(DN_QKV // LANES):
        cs = slice(j * LANES, (j + 1) * LANES)
        x = x_ref[:, cs].astype(F32)
        xp = prev_ref[:, cs].astype(F32)[hrows - 1:hrows, :] * keep_prev
        xn = next_ref[:, cs].astype(F32)[0:1, :] * keep_next
        x_dn = jnp.where(is_first, xp, pltpu.roll(x, 1, 0))
        x_up = jnp.where(is_last, xn, pltpu.roll(x, tm - 1, 0))
        w = cw_ref[:, cs]
        y = _silu(x_dn * w[0:1, :] + x * w[1:2, :] + x_up * w[2:3, :])
        if j < n_qk:
            y = y * lax.rsqrt(jnp.sum(y * y, axis=-1, keepdims=True) + EPS)
            if j < n_q:
                y = y * DN_DK ** -0.5
        lj = j % n_q
        outs[j // n_q][:, lj * LANES:(lj + 1) * LANES] = y.astype(BF16)

    a = ab_ref[...].astype(F32)
    lane = lax.broadcasted_iota(jnp.int32, (tm, LANES), 1)
    z = a + brow_ref[...]
    softplus = jnp.maximum(z, 0.0) + jnp.log(1.0 + jnp.exp(-jnp.abs(z)))
    g = -jnp.exp(arow_ref[...]) * softplus
    beta = 1.0 / (1.0 + jnp.exp(-a))
    gb_ref[...] = jnp.where(lane < 2 * DN_HEADS, g, jnp.where(lane < 4 * DN_HEADS, beta, 0.0))


def prep_odd(p, conv_w, arow, brow, first_flags, last_flags, tm):
    ta = p.shape[0]
    halo = 16
    hb = tm // halo
    n_h = ta // halo
    kdim = DN_HEADS * DN_DK
    grid_spec = pltpu.PrefetchScalarGridSpec(
        num_scalar_prefetch=2,
        grid=(ta // tm,),
        in_specs=[
            pl.BlockSpec((tm, DN_QKV), lambda i, *_: (i, 0)),
            pl.BlockSpec((halo, DN_QKV), lambda i, *_: (jnp.maximum(i * hb - 1, 0), 0)),
            pl.BlockSpec((halo, DN_QKV), lambda i, *_: (jnp.minimum((i + 1) * hb, n_h - 1), 0)),
            pl.BlockSpec((tm, LANES), lambda i, *_: (i, (DN_QKV + DN_HEADS * DN_DV) // LANES)),
            pl.BlockSpec((DN_CONV, DN_QKV), lambda i, *_: (0, 0)),
            pl.BlockSpec((1, LANES), lambda i, *_: (0, 0)),
            pl.BlockSpec((1, LANES), lambda i, *_: (0, 0)),
        ],
        out_specs=[
            pl.BlockSpec((tm, kdim), lambda i, *_: (i, 0)),
            pl.BlockSpec((tm, kdim), lambda i, *_: (i, 0)),
            pl.BlockSpec((tm, kdim), lambda i, *_: (i, 0)),
            pl.BlockSpec((tm, LANES), lambda i, *_: (i, 0)),
        ],
    )
    return pl.pallas_call(
        _prep_odd_kernel,
        grid_spec=grid_spec,
        out_shape=[
            jax.ShapeDtypeStruct((ta, kdim), BF16),
            jax.ShapeDtypeStruct((ta, kdim), BF16),
            jax.ShapeDtypeStruct((ta, kdim), BF16),
            jax.ShapeDtypeStruct((ta, LANES), F32),
        ],
        compiler_params=_cparams(("arbitrary",)),
        name="prep_odd",
    )(first_flags, last_flags, p, p, p, p, conv_w, arow, brow)


def _deltanet_kernel(q_ref, k_ref, v_ref, gb_ref, gbt_ref, s0_ref, o_ref, sf_ref, s_ref,
                     *, reverse, n_chunks, dir_off):
    t = pl.program_id(1)

    @pl.when(t == 0)
    def _():
        s_ref[...] = s0_ref[...]

    C = DN_CHUNK
    ii = lax.broadcasted_iota(jnp.int32, (C, C), 0)
    jj = lax.broadcasted_iota(jnp.int32, (C, C), 1)
    if reverse:
        incl = ii <= jj
        strict = ii < jj
    else:
        incl = ii >= jj
        strict = ii > jj
    tri = jnp.where(incl, 1.0, 0.0).astype(F32)
    if reverse:
        tri_t = jnp.where(ii >= jj, 1.0, 0.0).astype(F32)
    else:
        tri_t = jnp.where(ii <= jj, 1.0, 0.0).astype(F32)
    eye = jnp.where(ii == jj, 1.0, 0.0).astype(F32)
    blk = ii ^ jj

    order = list(range(n_chunks - 1, -1, -1) if reverse else range(n_chunks))
    items = [(c, h) for c in order for h in range(DN_HEADS)]
    gcols, grows, gbs = {}, {}, {}
    for c in order:
        gb_c = gb_ref[c * C:(c + 1) * C, :]
        gbs[c] = gb_c
        gcols[c] = jnp.dot(tri, gb_c, preferred_element_type=F32, precision=HIGHEST)
        grows[c] = jnp.dot(gbt_ref[c], tri_t, preferred_element_type=F32, precision=HIGHEST)

    qb, kb16, decay, kbeta, egc, kd, gl, rhs = {}, {}, {}, {}, {}, {}, {}, {}
    for it in items:
        c, h = it
        gi = dir_off + h
        bi = 2 * DN_HEADS + dir_off + h
        rows = slice(c * C, (c + 1) * C)
        cs = slice(h * DN_DK, (h + 1) * DN_DK)
        gc = gcols[c][:, gi:gi + 1]
        gr = grows[c][gi:gi + 1, :]
        beta = gbs[c][:, bi:bi + 1]
        qb[it] = q_ref[rows, cs]
        kb16[it] = k_ref[rows, cs]
        kf = kb16[it].astype(F32)
        decay[it] = jnp.where(incl, jnp.exp(jnp.where(incl, gc - gr, 0.0)), 0.0)
        kbeta[it] = kf * beta
        egc[it] = jnp.exp(gc)
        glast = gc[0:1, :] if reverse else gc[C - 1:C, :]
        kd[it] = (kf * jnp.exp(glast - gc)).astype(BF16)
        gl[it] = jnp.exp(glast)
        rhs[it] = jnp.concatenate([v_ref[rows, cs].astype(F32) * beta, kbeta[it] * egc[it]], axis=1).astype(BF16)

    kk = {it: lax.dot_general(kbeta[it].astype(BF16), kb16[it], NT_DIMS, preferred_element_type=F32)
          for it in items}
    qk = {it: lax.dot_general(qb[it], kb16[it], NT_DIMS, preferred_element_type=F32) for it in items}
    lm = {it: jnp.where(strict, kk[it] * decay[it], 0.0) for it in items}
    attn = {it: jnp.where(incl, qk[it] * decay[it], 0.0).astype(BF16) for it in items}
    dinv = {it: eye - jnp.where(blk < 2, lm[it], 0.0) for it in items}
    s = 2
    while s < C:
        in_band = jnp.logical_and(blk >= s, blk < 2 * s)
        tmp = {it: _dot(dinv[it].astype(BF16), jnp.where(in_band, lm[it], 0.0).astype(BF16)) for it in items}
        dinv = {it: dinv[it] - _dot(tmp[it].astype(BF16), dinv[it].astype(BF16)) for it in items}
        s *= 2
    uw = {it: _dot(dinv[it].astype(BF16), rhs[it]) for it in items}
    wq = {it: jnp.concatenate([uw[it][:, DN_DV:], qb[it].astype(F32) * egc[it]], axis=0).astype(BF16)
          for it in items}

    states = [s_ref[h] for h in range(DN_HEADS)]
    for c in order:
        its = [(c, h) for h in range(DN_HEADS)]
        r = {it: _dot(wq[it], states[it[1]].astype(BF16)) for it in its}
        v_new = {it: (uw[it][:, :DN_DV] - r[it][:C]).astype(BF16) for it in its}
        o = {it: r[it][C:] + _dot(attn[it], v_new[it]) for it in its}
        for it in its:
            h = it[1]
            states[h] = states[h] * gl[it] + lax.dot_general(kd[it], v_new[it], TN_DIMS,
                                                             preferred_element_type=F32)
        for it in its:
            h = it[1]
            o_ref[c * C:(c + 1) * C, h * DN_DK:(h + 1) * DN_DK] = o[it].astype(o_ref.dtype)
    for h in range(DN_HEADS):
        s_ref[h] = states[h]

    @pl.when(t == pl.num_programs(1) - 1)
    def _():
        sf_ref[...] = s_ref[...]


def deltanet(q, k, v, gb, gbt, s0, nb, seq, row_off, reverse, tl):
    nblk = seq // tl
    n_chunks = tl // DN_CHUNK
    off_b = row_off // tl
    kdim = DN_HEADS * DN_DK

    def rb(b, t):
        tt = nblk - 1 - t if reverse else t
        return off_b + b * nblk + tt

    seq_spec = pl.BlockSpec((tl, kdim), lambda b, t: (rb(b, t), 0))
    st_spec = pl.BlockSpec((None, DN_HEADS, DN_DK, DN_DV), lambda b, t: (b, 0, 0, 0))
    return pl.pallas_call(
        functools.partial(_deltanet_kernel, reverse=reverse, n_chunks=n_chunks,
                          dir_off=DN_HEADS if reverse else 0),
        grid=(nb, nblk),
        in_specs=[
            seq_spec, seq_spec, seq_spec,
            pl.BlockSpec((tl, LANES), lambda b, t: (rb(b, t), 0)),
            pl.BlockSpec((n_chunks, 4 * DN_HEADS, DN_CHUNK), lambda b, t: (rb(b, t), 0, 0)),
            st_spec,
        ],
        out_specs=[
            pl.BlockSpec((tl, kdim), lambda b, t: (b * nblk + (nblk - 1 - t if reverse else t), 0)),
            st_spec,
        ],
        out_shape=[
            jax.ShapeDtypeStruct((nb * seq, kdim), BF16),
            jax.ShapeDtypeStruct((nb, DN_HEADS, DN_DK, DN_DV), F32),
        ],
        scratch_shapes=[pltpu.VMEM((DN_HEADS, DN_DK, DN_DV), F32)],
        compiler_params=_cparams(("arbitrary", "arbitrary")),
        name="deltanet_bwd" if reverse else "deltanet_fwd",
    )(q, k, v, gb, gbt, s0)


def _deltanet_bidir_kernel(qf_ref, kf_ref, vf_ref, gbf_ref, gbtf_ref, qb_ref, kb_ref, vb_ref, gbb_ref, gbtb_ref,
                           s0f_ref, s0b_ref, of_ref, ob_ref, sff_ref, sfb_ref, sf_scr, sb_scr, *, n_chunks):
    t = pl.program_id(1)

    @pl.when(t == 0)
    def _():
        sf_scr[...] = s0f_ref[...]
        sb_scr[...] = s0b_ref[...]

    C = DN_CHUNK
    ii = lax.broadcasted_iota(jnp.int32, (C, C), 0)
    jj = lax.broadcasted_iota(jnp.int32, (C, C), 1)
    lower, upper = ii >= jj, ii <= jj
    eye = jnp.where(ii == jj, 1.0, 0.0).astype(F32)
    blk = ii ^ jj
    dirs = (
        dict(rev=False, incl=lower, strict=ii > jj, q=qf_ref, k=kf_ref, v=vf_ref, gb=gbf_ref, gbt=gbtf_ref,
             o=of_ref, scr=sf_scr, off=0, order=list(range(n_chunks))),
        dict(rev=True, incl=upper, strict=ii < jj, q=qb_ref, k=kb_ref, v=vb_ref, gb=gbb_ref, gbt=gbtb_ref,
             o=ob_ref, scr=sb_scr, off=DN_HEADS, order=list(range(n_chunks - 1, -1, -1))),
    )
    items = [(d, c, h) for d in range(2) for c in dirs[d]["order"] for h in range(DN_HEADS)]

    gcols, grows, gbs = {}, {}, {}
    for d, dr in enumerate(dirs):
        tri = jnp.where(dr["incl"], 1.0, 0.0).astype(F32)
        tri_t = jnp.where(upper if not dr["rev"] else lower, 1.0, 0.0).astype(F32)
        for c in dr["order"]:
            gb_c = dr["gb"][c * C:(c + 1) * C, :]
            gbs[(d, c)] = gb_c
            gcols[(d, c)] = jnp.dot(tri, gb_c, preferred_element_type=F32, precision=HIGHEST)
            grows[(d, c)] = jnp.dot(dr["gbt"][c], tri_t, preferred_element_type=F32, precision=HIGHEST)

    qb, kb16, decay, kbeta, egc, kd, gl, rhs = {}, {}, {}, {}, {}, {}, {}, {}
    for it in items:
        d, c, h = it
        dr = dirs[d]
        gi = dr["off"] + h
        bi = 2 * DN_HEADS + dr["off"] + h
        rows = slice(c * C, (c + 1) * C)
        cs = slice(h * DN_DK, (h + 1) * DN_DK)
        gc = gcols[(d, c)][:, gi:gi + 1]
        gr = grows[(d, c)][gi:gi + 1, :]
        beta = gbs[(d, c)][:, bi:bi + 1]
        qb[it] = dr["q"][rows, cs]
        kb16[it] = dr["k"][rows, cs]
        kf = kb16[it].astype(F32)
        decay[it] = jnp.where(dr["incl"], jnp.exp(jnp.where(dr["incl"], gc - gr, 0.0)), 0.0)
        kbeta[it] = kf * beta
        egc[it] = jnp.exp(gc)
        glast = gc[0:1, :] if dr["rev"] else gc[C - 1:C, :]
        kd[it] = (kf * jnp.exp(glast - gc)).astype(BF16)
        gl[it] = jnp.exp(glast)
        rhs[it] = jnp.concatenate([dr["v"][rows, cs].astype(F32) * beta, kbeta[it] * egc[it]], axis=1).astype(BF16)

    kk = {it: lax.dot_general(kbeta[it].astype(BF16), kb16[it], NT_DIMS, preferred_element_type=F32)
          for it in items}
    qk = {it: lax.dot_general(qb[it], kb16[it], NT_DIMS, preferred_element_type=F32) for it in items}
    lm = {it: jnp.where(dirs[it[0]]["strict"], kk[it] * decay[it], 0.0) for it in items}
    attn = {it: jnp.where(dirs[it[0]]["incl"], qk[it] * decay[it], 0.0).astype(BF16) for it in items}
    dinv = {it: eye - jnp.where(blk < 2, lm[it], 0.0) for it in items}
    s = 2
    while s < C:
        in_band = jnp.logical_and(blk >= s, blk < 2 * s)
        tmp = {it: _dot(dinv[it].astype(BF16), jnp.where(in_band, lm[it], 0.0).astype(BF16)) for it in items}
        dinv = {it: dinv[it] - _dot(tmp[it].astype(BF16), dinv[it].astype(BF16)) for it in items}
        s *= 2
    uw = {it: _dot(dinv[it].astype(BF16), rhs[it]) for it in items}
    wq = {it: jnp.concatenate([uw[it][:, DN_DV:], qb[it].astype(F32) * egc[it]], axis=0).astype(BF16)
          for it in items}

    states = {(d, h): dirs[d]["scr"][h] for d in range(2) for h in range(DN_HEADS)}
    for step in range(n_chunks):
        its = [(d, dirs[d]["order"][step], h) for d in range(2) for h in range(DN_HEADS)]
        r = {it: _dot(wq[it], states[(it[0], it[2])].astype(BF16)) for it in its}
        v_new = {it: (uw[it][:, :DN_DV] - r[it][:C]).astype(BF16) for it in its}
        o = {it: r[it][C:] + _dot(attn[it], v_new[it]) for it in its}
        for it in its:
            key = (it[0], it[2])
            states[key] = states[key] * gl[it] + lax.dot_general(kd[it], v_new[it], TN_DIMS,
                                                                 preferred_element_type=F32)
        for it in its:
            d, c, h = it
            dirs[d]["o"][c * C:(c + 1) * C, h * DN_DK:(h + 1) * DN_DK] = o[it].astype(of_ref.dtype)
    for (d, h), st in states.items():
        dirs[d]["scr"][h] = st

    @pl.when(t == pl.num_programs(1) - 1)
    def _():
        sff_ref[...] = sf_scr[...]
        sfb_ref[...] = sb_scr[...]


def deltanet_bidir(q, k, v, gb, gbt, s0f, s0b, nb, seq, row_off, tl):
    nblk = seq // tl
    n_chunks = tl // DN_CHUNK
    off_b = row_off // tl
    kdim = DN_HEADS * DN_DK

    def fwd_rb(b, t):
        return off_b + b * nblk + t

    def bwd_rb(b, t):
        return off_b + b * nblk + (nblk - 1 - t)

    def seq_specs(rb):
        spec = pl.BlockSpec((tl, kdim), lambda b, t: (rb(b, t), 0))
        return [spec, spec, spec,
                pl.BlockSpec((tl, LANES), lambda b, t: (rb(b, t), 0)),
                pl.BlockSpec((n_chunks, 4 * DN_HEADS, DN_CHUNK), lambda b, t: (rb(b, t), 0, 0))]

    st_spec = pl.BlockSpec((None, DN_HEADS, DN_DK, DN_DV), lambda b, t: (b, 0, 0, 0))
    st_shape = jax.ShapeDtypeStruct((nb, DN_HEADS, DN_DK, DN_DV), F32)
    o_shape = jax.ShapeDtypeStruct((nb * seq, kdim), BF16)
    return pl.pallas_call(
        functools.partial(_deltanet_bidir_kernel, n_chunks=n_chunks),
        grid=(nb, nblk),
        in_specs=seq_specs(fwd_rb) + seq_specs(bwd_rb) + [st_spec, st_spec],
        out_specs=[
            pl.BlockSpec((tl, kdim), lambda b, t: (b * nblk + t, 0)),
            pl.BlockSpec((tl, kdim), lambda b, t: (b * nblk + nblk - 1 - t, 0)),
            st_spec, st_spec,
        ],
        out_shape=[o_shape, o_shape, st_shape, st_shape],
        scratch_shapes=[pltpu.VMEM((DN_HEADS, DN_DK, DN_DV), F32), pltpu.VMEM((DN_HEADS, DN_DK, DN_DV), F32)],
        compiler_params=_cparams(("arbitrary", "arbitrary")),
        name="deltanet_bidir",
    )(q, k, v, gb, gbt, q, k, v, gb, gbt, s0f, s0b)


def _router_kernel(x_ref, g_ref, sh_ref, sc_ref, wr_ref, br_ref, ltri_ref, f_ref, r_ref, cnt_ref, base_ref):
    @pl.when(pl.program_id(0) == 0)
    def _():
        base_ref[...] = jnp.zeros_like(base_ref)

    h = _norm_mod(x_ref[...], g_ref[...], sh_ref[...], sc_ref[...])
    f_ref[...] = h
    logits = _dot(h.astype(BF16), wr_ref[...]) + br_ref[...]
    tm = logits.shape[0]
    lane = lax.broadcasted_iota(jnp.int32, (tm, LANES), 1)
    neg = -1e30
    big = 4 * LANES
    is_g = lane < N_GROUPS
    gl = jnp.where(is_g, logits, neg)
    gm = jnp.max(gl, axis=-1, keepdims=True)
    grp = jnp.min(jnp.where(gl == gm, lane, big), axis=-1, keepdims=True)
    psum = jnp.sum(jnp.where(is_g, jnp.exp(gl - gm), 0.0), axis=-1, keepdims=True)
    p_grp = 1.0 / psum
    e_lane = lane - N_GROUPS
    in_grp = jnp.logical_and(jnp.logical_and(e_lane >= 0, e_lane < N_EXPERTS),
                             (e_lane // EXPERTS_PER_GROUP) == grp)
    el = jnp.where(in_grp, logits, neg)
    m1 = jnp.max(el, axis=-1, keepdims=True)
    i1 = jnp.min(jnp.where(el == m1, lane, big), axis=-1, keepdims=True)
    el2 = jnp.where(lane == i1, neg, el)
    m2 = jnp.max(el2, axis=-1, keepdims=True)
    i2 = jnp.min(jnp.where(el2 == m2, lane, big), axis=-1, keepdims=True)
    e21 = jnp.exp(m2 - m1)
    w1 = p_grp / (1.0 + e21)
    w2 = p_grp * e21 / (1.0 + e21)
    e1 = (i1 - N_GROUPS).astype(F32)
    e2 = (i2 - N_GROUPS).astype(F32)
    oh1 = lane == i1
    oh2 = lane == i2
    oh1f = jnp.where(oh1, 1.0, 0.0)
    oh2f = jnp.where(oh2, 1.0, 0.0)
    ltri = ltri_ref[...]
    before1 = _dot(ltri, oh1f.astype(BF16))
    before2 = _dot(ltri, oh2f.astype(BF16))
    cnt1 = jnp.sum(oh1f, axis=0, keepdims=True)
    cnt2 = jnp.sum(oh2f, axis=0, keepdims=True)
    base = base_ref[0:1, :]
    rank1 = jnp.sum(jnp.where(oh1, base + before1, 0.0), axis=-1, keepdims=True)
    rank2 = jnp.sum(jnp.where(oh2, base + cnt1 + before2, 0.0), axis=-1, keepdims=True)
    total = base + cnt1 + cnt2
    base_ref[...] = jnp.broadcast_to(total, base_ref.shape)
    cnt_ref[...] = jnp.broadcast_to(total, cnt_ref.shape)
    vals = (e1, e2, w1, w2, rank1, rank2)
    out = jnp.zeros((tm, LANES), F32)
    for idx, val in enumerate(vals):
        out = jnp.where(lane == idx, val, out)
    r_ref[...] = out


def moe_router(xa, gain, shift, scale, w_router, b_router, tm, seq, nb):
    ta, d = xa.shape
    mrow = _mod_row_map(tm, seq, nb)
    ii = np.arange(tm)
    ltri = jnp.asarray((ii[:, None] > ii[None, :]).astype(np.float32)).astype(BF16)
    return pl.pallas_call(
        _router_kernel,
        grid=(ta // tm,),
        in_specs=[
            pl.BlockSpec((tm, d), lambda i: (i, 0)),
            pl.BlockSpec((1, d), lambda i: (0, 0)),
            pl.BlockSpec((None, 1, d), mrow),
            pl.BlockSpec((None, 1, d), mrow),
            pl.BlockSpec((d, LANES), lambda i: (0, 0)),
            pl.BlockSpec((1, LANES), lambda i: (0, 0)),
            pl.BlockSpec((tm, tm), lambda i: (0, 0)),
        ],
        out_specs=[pl.BlockSpec((tm, d), lambda i: (i, 0)), pl.BlockSpec((tm, LANES), lambda i: (i, 0)),
                   pl.BlockSpec((8, LANES), lambda i: (0, 0))],
        out_shape=[jax.ShapeDtypeStruct((ta, d), F32), jax.ShapeDtypeStruct((ta, LANES), F32),
                   jax.ShapeDtypeStruct((8, LANES), F32)],
        scratch_shapes=[pltpu.VMEM((8, LANES), F32)],
        compiler_params=_cparams(("arbitrary",)),
        name="moe_router",
    )(xa, gain.reshape(1, d), shift, scale, w_router, b_router, ltri)


ROW_DMA_UNROLL = 8


def _issue_row_copies(n_rows, make_copy):
    def trip(i, carry):
        for u in range(ROW_DMA_UNROLL):
            make_copy(i * ROW_DMA_UNROLL + u).start(priority=u % 2)
        return carry

    lax.fori_loop(0, n_rows // ROW_DMA_UNROLL, trip, 0)


def _moe_scatter_kernel(pos_ref, f_ref, xs_in, xs_out, sem):
    del xs_in
    tm = f_ref.shape[0]
    for k in range(TOP_K):
        _issue_row_copies(tm, lambda r, k=k: pltpu.make_async_copy(
            f_ref.at[pl.ds(r, 1)], xs_out.at[pl.ds(pos_ref[0, 0, k * tm + r], 1)], sem))
    for _ in range(2):
        pltpu.make_async_copy(f_ref, xs_out.at[pl.ds(0, tm)], sem).wait()


def moe_scatter(pos_tiles, f, xs_zero, tm):
    ta, d = f.shape
    return pl.pallas_call(
        _moe_scatter_kernel,
        grid=(ta // tm,),
        in_specs=[
            pl.BlockSpec((1, 1, 2 * tm), lambda i: (i, 0, 0), memory_space=pltpu.SMEM),
            pl.BlockSpec((tm, d), lambda i: (i, 0)),
            pl.BlockSpec(memory_space=pl.ANY),
        ],
        out_specs=pl.BlockSpec(memory_space=pl.ANY),
        out_shape=jax.ShapeDtypeStruct(xs_zero.shape, xs_zero.dtype),
        scratch_shapes=[pltpu.SemaphoreType.DMA(())],
        input_output_aliases={2: 0},
        compiler_params=_cparams(("arbitrary",)),
        name="moe_scatter",
    )(pos_tiles, f, xs_zero)


def _moe_ffn_kernel(te_ref, nu_ref, x_ref, wgu_ref, wd_ref, o_ref, wgu_bf, wd_bf):
    i = pl.program_id(0)
    fdim = wd_bf.shape[0]

    @pl.when(i < nu_ref[0])
    def _():
        prev = te_ref[jnp.maximum(i - 1, 0)]
        changed = jnp.logical_or(i == 0, te_ref[i] != prev)

        @pl.when(changed)
        def _():
            wgu_bf[...] = wgu_ref[...].astype(BF16)
            wd_bf[...] = wd_ref[...].astype(BF16)

        gu = _dot(x_ref[...].astype(BF16), wgu_bf[...])
        hmid = _silu(gu[:, :fdim]) * gu[:, fdim:]
        o_ref[...] = _dot(hmid.astype(BF16), wd_bf[...])

    @pl.when(i >= nu_ref[0])
    def _():
        o_ref[...] = jnp.zeros_like(o_ref)


def moe_ffn(tile_expert, n_used, xs, w_gate_up, w_down, layer, tm):
    n_pad, d = xs.shape
    f2 = w_gate_up.shape[-1]
    fdim = w_down.shape[-2]
    grid_spec = pltpu.PrefetchScalarGridSpec(
        num_scalar_prefetch=2,
        grid=(n_pad // tm,),
        in_specs=[
            pl.BlockSpec((tm, d), lambda i, te, nu: (i, 0)),
            pl.BlockSpec((None, None, d, f2), lambda i, te, nu: (layer, te[i], 0, 0)),
            pl.BlockSpec((None, None, fdim, d), lambda i, te, nu: (layer, te[i], 0, 0)),
        ],
        out_specs=pl.BlockSpec((tm, d), lambda i, te, nu: (i, 0)),
        scratch_shapes=[pltpu.VMEM((d, f2), BF16), pltpu.VMEM((fdim, d), BF16)],
    )
    return pl.pallas_call(
        _moe_ffn_kernel,
        grid_spec=grid_spec,
        out_shape=jax.ShapeDtypeStruct((n_pad, d), F32),
        compiler_params=_cparams(("arbitrary",)),
        name="moe_ffn",
    )(tile_expert, n_used, xs, w_gate_up, w_down)


def _moe_combine_kernel(pos_ref, x_ref, gate_ref, r_ref, y_hbm, o_ref, ybuf, sem):
    tm = x_ref.shape[0]
    _issue_row_copies(2 * tm, lambda r: pltpu.make_async_copy(
        y_hbm.at[pl.ds(pos_ref[0, 0, r], 1)], ybuf.at[pl.ds(r, 1)], sem))
    pltpu.make_async_copy(y_hbm.at[pl.ds(0, 2 * tm)], ybuf, sem).wait()
    route = r_ref[...]
    y = route[:, 2:3] * ybuf[0:tm, :] + route[:, 3:4] * ybuf[tm:2 * tm, :]
    o_ref[...] = x_ref[...] + gate_ref[...] * y


def moe_combine(pos_tiles, xa, gate, route, y_sorted, tm, seq, nb):
    ta, d = xa.shape
    return pl.pallas_call(
        _moe_combine_kernel,
        grid=(ta // tm,),
        in_specs=[
            pl.BlockSpec((1, 1, 2 * tm), lambda i: (i, 0, 0), memory_space=pltpu.SMEM),
            pl.BlockSpec((tm, d), lambda i: (i, 0)),
            pl.BlockSpec((None, 1, d), _mod_row_map(tm, seq, nb)),
            pl.BlockSpec((tm, LANES), lambda i: (i, 0)),
            pl.BlockSpec(memory_space=pl.ANY),
        ],
        out_specs=pl.BlockSpec((tm, d), lambda i: (i, 0)),
        out_shape=jax.ShapeDtypeStruct((ta, d), F32),
        scratch_shapes=[pltpu.VMEM((2 * tm, d), F32), pltpu.SemaphoreType.DMA(())],
        compiler_params=_cparams(("arbitrary",)),
        name="moe_combine",
    )(pos_tiles, xa, gate, route, y_sorted)


def moe_slots(route, counts, tm_ffn, tm_tok):
    ta = route.shape[0]
    ids = route[:, 0:TOP_K].astype(jnp.int32)
    rank = route[:, 2 * TOP_K:3 * TOP_K].astype(jnp.int32)
    counts = counts[0, N_GROUPS:N_GROUPS + N_EXPERTS].astype(jnp.int32)
    padded = ((counts + tm_ffn - 1) // tm_ffn) * tm_ffn
    ends = jnp.cumsum(padded)
    starts = ends - padded
    pos = starts[ids] + rank
    n_tiles = (TOP_K * ta + N_EXPERTS * (tm_ffn - 1)) // tm_ffn
    tile_start = jnp.arange(n_tiles, dtype=jnp.int32) * tm_ffn
    tile_expert = jnp.sum((tile_start[:, None] >= ends[None, :]).astype(jnp.int32), axis=1)
    tile_expert = jnp.minimum(tile_expert, N_EXPERTS - 1)
    n_used = (ends[-1] // tm_ffn).astype(jnp.int32).reshape(1)
    pos_tiles = pos.reshape(ta // tm_tok, tm_tok, TOP_K).transpose(0, 2, 1).reshape(ta // tm_tok, 1, TOP_K * tm_tok)
    return tile_expert, n_used, n_tiles * tm_ffn, pos_tiles


def _seq_flags(t_lat, seq, tc, cseq, tm):
    starts = np.arange(0, t_lat + tc, tm)
    first = np.where(starts < t_lat, starts % seq == 0, (starts - t_lat) % cseq == 0)
    ends = starts + tm
    last = np.where(starts < t_lat, ends % seq == 0, (ends - t_lat) % cseq == 0)
    return jnp.asarray(first.astype(np.int32)), jnp.asarray(last.astype(np.int32))


def kernel(x, c, ctx, c_ctx, w_ada, b_ada, norm_mix, norm_ffn, ev_w_in, ev_q_gain, ev_k_gain, ev_decay_f,
           ev_decay_b, ev_w_out, od_w_in, od_conv, od_a_log_f, od_a_log_b, od_dt_bias_f, od_dt_bias_b,
           od_out_gain, od_w_out, moe_w_group, moe_b_group, moe_w_expert, moe_b_expert, moe_w_gate_up,
           moe_w_down, final_norm_gain):
    nb, seq, d = x.shape
    cseq = ctx.shape[1]
    depth = w_ada.shape[0]
    t_lat = nb * seq
    tc = nb * cseq
    assert nb + 1 <= 8 and seq % cseq == 0 and cseq % RET_CHUNK == 0 and seq % GRID_W == 0

    tm = 512 if tc % 512 == 0 else cseq
    tm_prep = min(256, cseq)
    tq = min(256, cseq)
    tk = min(256, cseq)
    tl = 2 * DN_CHUNK
    tm_ffn = 256
    tm_comb = min(256, cseq)

    xa = jnp.concatenate([x.reshape(t_lat, d), ctx.reshape(tc, d)], axis=0)
    c8 = jnp.zeros((8, d), F32).at[:nb].set(c).at[nb].set(c_ctx)
    mod = adaln(c8, w_ada, b_ada)

    tabs = rope_tables(seq, tm)
    first_flags, last_flags = _seq_flags(t_lat, seq, tc, cseq, tm_prep)
    ret_zero = jnp.zeros((nb, RET_HEADS, RET_DK, RET_DV), F32)
    dn_zero = jnp.zeros((nb, DN_HEADS, DN_DK, DN_DV), F32)

    for layer in range(depth):
        last = layer == depth - 1
        m = mod[layer].reshape(8, 6, 1, d)
        sh1, sc1, g1, sh2, sc2, g2 = (m[:, j] for j in range(6))
        i = layer // 2
        if layer % 2 == 0:
            w_in = ev_w_in[i].astype(BF16)
            rq, rk, p, aq, ak, av = inproj_even(xa, norm_mix[layer], sh1, sc1, w_in, tabs, ev_q_gain[i],
                                                ev_k_gain[i], tm, seq, nb)
            dec = jnp.stack([ev_decay_f[i], ev_decay_b[i]]).astype(F32)
            oc, scf, scb = retention(dec, rq, rk, p, ret_zero, ret_zero, nb, cseq, t_lat // cseq)
            ol, _, _ = retention(dec, rq, rk, p, scf, scb, nb, seq, 0)
            kcat = jnp.concatenate([ak[:, :t_lat].reshape(ATT_KV_HEADS, nb, seq, ATT_HD),
                                    ak[:, t_lat:].reshape(ATT_KV_HEADS, nb, cseq, ATT_HD)], axis=2)
            vcat = jnp.concatenate([av[:, :t_lat].reshape(ATT_KV_HEADS, nb, seq, ATT_HD),
                                    av[:, t_lat:].reshape(ATT_KV_HEADS, nb, cseq, ATT_HD)], axis=2)
            lk = seq + cseq
            vtcat = jnp.concatenate([vcat.transpose(0, 1, 3, 2),
                                     jnp.ones((ATT_KV_HEADS, nb, ATT_VT_ROWS - ATT_HD, lk), BF16)], axis=2)
            vtcat = vtcat.reshape(ATT_KV_HEADS, nb, ATT_VT_ROWS, lk // tk, tk).transpose(0, 1, 3, 2, 4)
            att_l = attention(aq, kcat, vtcat, seq, 0, 0, tq, tk)
            att_c = attention(aq, kcat, vtcat, cseq, t_lat // tq, seq // tk, tq, tk)
            w_out = ev_w_out[i].astype(BF16)
            k1 = RET_HEADS * RET_DV
            xa = outproj_even(ol, oc, att_l, att_c, w_out[:k1], w_out[k1:], xa, g1, tm, seq, nb)
        else:
            w_in = jnp.pad(od_w_in[i], ((0, 0), (0, ODD_IN_PAD - ODD_IN))).astype(BF16)
            p = norm_mod_matmul(xa, norm_mix[layer], sh1, sc1, w_in, tm, seq, nb)
            zpad = jnp.zeros((LANES - 2 * DN_HEADS,), F32)
            arow = jnp.concatenate([od_a_log_f[i], od_a_log_b[i], zpad]).reshape(1, LANES).astype(F32)
            brow = jnp.concatenate([od_dt_bias_f[i], od_dt_bias_b[i], zpad]).reshape(1, LANES).astype(F32)
            q, k, v, gb = prep_odd(p, od_conv[i].astype(F32), arow, brow, first_flags, last_flags, tm_prep)
            ta = t_lat + tc
            gbt = gb.reshape(ta // DN_CHUNK, DN_CHUNK, LANES)[:, :, :4 * DN_HEADS].transpose(0, 2, 1)
            oc_f, oc_b, sc_f, sc_b = deltanet_bidir(q, k, v, gb, gbt, dn_zero, dn_zero, nb, cseq, t_lat, tl)
            ol_f, ol_b, _, _ = deltanet_bidir(q, k, v, gb, gbt, sc_f, sc_b, nb, seq, 0, tl)
            xa = outproj_odd(ol_f, oc_f, ol_b, oc_b, p, od_out_gain[i], od_w_out[i].astype(BF16), xa, g1, tm, seq,
                             nb)

        w_router = jnp.pad(jnp.concatenate([moe_w_group[layer], moe_w_expert[layer]], axis=1),
                           ((0, 0), (0, LANES - N_GROUPS - N_EXPERTS))).astype(BF16)
        b_router = jnp.pad(jnp.concatenate([moe_b_group[layer], moe_b_expert[layer]]),
                           (0, LANES - N_GROUPS - N_EXPERTS)).reshape(1, LANES).astype(F32)
        f, route, counts = moe_router(xa, norm_ffn[layer], sh2, sc2, w_router, b_router, tm, seq, nb)
        tile_expert, n_used, n_pad, pos_tiles = moe_slots(route, counts, tm_ffn, tm_comb)
        xs = moe_scatter(pos_tiles, f, jnp.zeros((n_pad, d), F32), tm_comb)
        y_sorted = moe_ffn(tile_expert, n_used, xs, moe_w_gate_up, moe_w_down, layer, tm_ffn)
        xa = moe_combine(pos_tiles, xa, g2, route, y_sorted, tm_comb, seq, nb)

    out = final_norm(xa, final_norm_gain, t_lat, tm)
    return out.reshape(nb, seq, d)
```

```python
import functools
import math

import numpy as np
import jax
import jax.numpy as jnp
from jax import lax
from jax.experimental import pallas as pl
from jax.experimental.pallas import tpu as pltpu

F32 = jnp.float32
BF16 = jnp.bfloat16
HIGHEST = lax.Precision.HIGHEST

EPS = 1e-6
GRID_W = 64
ROPE_BASE = 10000.0
RET_HEADS, RET_DK, RET_DV, RET_CHUNK = 8, 64, 128, 128
ATT_HEADS, ATT_KV_HEADS, ATT_HD = 8, 2, 64
DN_HEADS, DN_DK, DN_DV, DN_CHUNK, DN_CONV = 8, 128, 128, 64, 3
N_GROUPS, EXPERTS_PER_GROUP, TOP_K = 4, 8, 2
N_EXPERTS = N_GROUPS * EXPERTS_PER_GROUP

EVEN_IN = 2 * RET_HEADS * RET_DK + 2 * RET_HEADS * RET_DV + (ATT_HEADS + 2 * ATT_KV_HEADS) * ATT_HD
EVEN_ATT_COL = 2 * RET_HEADS * RET_DK + 2 * RET_HEADS * RET_DV
EVEN_ATT_W = (ATT_HEADS + 2 * ATT_KV_HEADS) * ATT_HD
DN_QKV = 2 * DN_HEADS * DN_DK + DN_HEADS * DN_DV
ODD_IN = DN_QKV + DN_HEADS * DN_DV + 4 * DN_HEADS
ODD_IN_PAD = ((ODD_IN + 127) // 128) * 128

LANES = 128
VMEM_LIMIT = 56 * 1024 * 1024

NT_DIMS = (((1,), (1,)), ((), ()))
TN_DIMS = (((0,), (0,)), ((), ()))


def _cparams(sem):
    return pltpu.CompilerParams(dimension_semantics=sem, vmem_limit_bytes=VMEM_LIMIT)


def _silu(x):
    return x / (1.0 + jnp.exp(-x))


def _dot(a, b):
    return jnp.dot(a, b, preferred_element_type=F32)


def _adaln_kernel(c_ref, w_ref, b_ref, o_ref):
    s = _silu(c_ref[...])
    o_ref[...] = _dot(s.astype(BF16), w_ref[...].astype(BF16)) + b_ref[...]


def adaln(c8, w_ada, b_ada):
    depth, d, n6 = w_ada.shape
    tn = min(n6, 1536)
    return pl.pallas_call(
        _adaln_kernel,
        grid=(depth, n6 // tn),
        in_specs=[
            pl.BlockSpec((8, d), lambda l, j: (0, 0)),
            pl.BlockSpec((None, d, tn), lambda l, j: (l, 0, j)),
            pl.BlockSpec((None, 1, tn), lambda l, j: (l, 0, j)),
        ],
        out_specs=pl.BlockSpec((None, 8, tn), lambda l, j: (l, 0, j)),
        out_shape=jax.ShapeDtypeStruct((depth, 8, n6), F32),
        compiler_params=_cparams(("arbitrary", "arbitrary")),
        name="adaln",
    )(c8, w_ada, b_ada.reshape(depth, 1, n6))


def _norm_mod(x, gain, shift, scale):
    ms = jnp.mean(x * x, axis=-1, keepdims=True)
    h = x * lax.rsqrt(ms + EPS) * gain
    return h * (1.0 + scale) + shift


def _nmm_kernel(x_ref, g_ref, sh_ref, sc_ref, w_ref, o_ref, *, nchunk):
    hb = _norm_mod(x_ref[...], g_ref[...], sh_ref[...], sc_ref[...]).astype(BF16)
    n = o_ref.shape[-1]
    for n0 in range(0, n, nchunk):
        n1 = min(n0 + nchunk, n)
        o_ref[:, n0:n1] = _dot(hb, w_ref[:, n0:n1]).astype(o_ref.dtype)


def _mod_row_map(tm, seq, n_lat_batches):
    return lambda i: (jnp.minimum((i * tm) // seq, n_lat_batches), 0, 0)


def norm_mod_matmul(xa, gain, shift, scale, w, tm, seq, nb):
    ta, d = xa.shape
    n = w.shape[1]
    nchunk = 512
    mrow = _mod_row_map(tm, seq, nb)
    return pl.pallas_call(
        functools.partial(_nmm_kernel, nchunk=nchunk),
        grid=(ta // tm,),
        in_specs=[
            pl.BlockSpec((tm, d), lambda i: (i, 0)),
            pl.BlockSpec((1, d), lambda i: (0, 0)),
            pl.BlockSpec((None, 1, d), mrow),
            pl.BlockSpec((None, 1, d), mrow),
            pl.BlockSpec((d, n), lambda i: (0, 0)),
        ],
        out_specs=pl.BlockSpec((tm, n), lambda i: (i, 0)),
        out_shape=jax.ShapeDtypeStruct((ta, n), BF16),
        compiler_params=_cparams(("arbitrary",)),
        name="norm_mod_matmul",
    )(xa, gain.reshape(1, d), shift, scale, w)


def _final_norm_kernel(x_ref, g_ref, o_ref):
    x = x_ref[...]
    ms = jnp.mean(x * x, axis=-1, keepdims=True)
    o_ref[...] = x * lax.rsqrt(ms + EPS) * g_ref[...]


def final_norm(xa, gain, t_rows, tm):
    d = xa.shape[1]
    return pl.pallas_call(
        _final_norm_kernel,
        grid=(t_rows // tm,),
        in_specs=[pl.BlockSpec((tm, d), lambda i: (i, 0)), pl.BlockSpec((1, d), lambda i: (0, 0))],
        out_specs=pl.BlockSpec((tm, d), lambda i: (i, 0)),
        out_shape=jax.ShapeDtypeStruct((t_rows, d), F32),
        compiler_params=_cparams(("arbitrary",)),
        name="final_norm",
    )(xa, gain.reshape(1, d))


def _inproj_even_kernel(x_ref, g_ref, sh_ref, sc_ref, w_ref, cos_ref, s1_ref, s2_ref, qg_ref, kg_ref, bd_ref,
                        rq_ref, rk_ref, vg_ref, aq_ref, ak_ref, av_ref):
    hb = _norm_mod(x_ref[...], g_ref[...], sh_ref[...], sc_ref[...]).astype(BF16)
    cos = cos_ref[...]
    s1 = s1_ref[...]
    s2 = s2_ref[...]
    bd = bd_ref[...]
    half = ATT_HD

    def proj(c0, width=LANES):
        return _dot(hb, w_ref[:, c0:c0 + width])

    def rope(x):
        return x * cos + pltpu.roll(x, LANES - 16, 1) * s1 + pltpu.roll(x, 16, 1) * s2

    def head_norm(x, gain):
        sq = x * x
        hi = sq.astype(BF16)
        lo = (sq - hi.astype(F32)).astype(BF16)
        ms = _dot(hi, bd) + _dot(lo, bd)
        return x * lax.rsqrt(ms + EPS) * gain

    wide = 2 * LANES
    qw = RET_HEADS * RET_DK
    for c0 in range(0, qw, wide):
        yq = proj(c0, wide)
        yk = proj(qw + c0, wide)
        for u in range(2):
            cs = slice(c0 + u * LANES, c0 + (u + 1) * LANES)
            us = slice(u * LANES, (u + 1) * LANES)
            rq_ref[:, cs] = rope(yq[:, us]).astype(BF16)
            rk_ref[:, cs] = (rope(yk[:, us]) * RET_DK ** -0.5).astype(BF16)
    vgw = 2 * RET_HEADS * RET_DV
    for c0 in range(0, vgw, 512):
        vg_ref[:, c0:c0 + 512] = proj(2 * qw + c0, 512).astype(BF16)

    qg = qg_ref[...]
    kg = kg_ref[...]
    a0 = EVEN_ATT_COL
    for c0 in range(0, ATT_HEADS * ATT_HD, wide):
        ya = proj(a0 + c0, wide)
        for u in range(2):
            y = rope(head_norm(ya[:, u * LANES:(u + 1) * LANES], qg)) * (ATT_HD ** -0.5 * math.log2(math.e))
            y = y.astype(BF16)
            hd0 = (c0 + u * LANES) // ATT_HD
            aq_ref[hd0] = y[:, :half]
            aq_ref[hd0 + 1] = y[:, half:]
    ykv = proj(a0 + ATT_HEADS * ATT_HD, wide)
    y = rope(head_norm(ykv[:, :LANES], kg)).astype(BF16)
    ak_ref[0] = y[:, :half]
    ak_ref[1] = y[:, half:]
    v = ykv[:, LANES:].astype(BF16)
    av_ref[0] = v[:, :half]
    av_ref[1] = v[:, half:]


def inproj_even(xa, gain, shift, scale, w, tabs, q_gain, k_gain, tm, seq, nb):
    ta, d = xa.shape
    t_lat = nb * seq
    cos_t, s1_t, s2_t = tabs
    n_tab = seq // tm

    def tab_map(i):
        r = i * tm
        return (jnp.where(r < t_lat, (r % seq) // tm, n_tab), 0)

    ii = np.arange(LANES)
    bd = jnp.asarray((ii[:, None] // ATT_HD == ii[None, :] // ATT_HD).astype(np.float32) / ATT_HD).astype(BF16)
    qg = jnp.tile(q_gain.astype(F32), LANES // ATT_HD).reshape(1, LANES)
    kg = jnp.tile(k_gain.astype(F32), LANES // ATT_HD).reshape(1, LANES)
    mrow = _mod_row_map(tm, seq, nb)
    tab_spec = pl.BlockSpec((tm, LANES), tab_map)
    one = lambda i: (0, 0)
    qw = RET_HEADS * RET_DK
    vgw = 2 * RET_HEADS * RET_DV
    return pl.pallas_call(
        _inproj_even_kernel,
        grid=(ta // tm,),
        in_specs=[
            pl.BlockSpec((tm, d), lambda i: (i, 0)),
            pl.BlockSpec((1, d), one),
            pl.BlockSpec((None, 1, d), mrow),
            pl.BlockSpec((None, 1, d), mrow),
            pl.BlockSpec((d, EVEN_IN), one),
            tab_spec, tab_spec, tab_spec,
            pl.BlockSpec((1, LANES), one), pl.BlockSpec((1, LANES), one),
            pl.BlockSpec((LANES, LANES), one),
        ],
        out_specs=[
            pl.BlockSpec((tm, qw), lambda i: (i, 0)),
            pl.BlockSpec((tm, qw), lambda i: (i, 0)),
            pl.BlockSpec((tm, vgw), lambda i: (i, 0)),
            pl.BlockSpec((ATT_HEADS, tm, ATT_HD), lambda i: (0, i, 0)),
            pl.BlockSpec((ATT_KV_HEADS, tm, ATT_HD), lambda i: (0, i, 0)),
            pl.BlockSpec((ATT_KV_HEADS, tm, ATT_HD), lambda i: (0, i, 0)),
        ],
        out_shape=[
            jax.ShapeDtypeStruct((ta, qw), BF16),
            jax.ShapeDtypeStruct((ta, qw), BF16),
            jax.ShapeDtypeStruct((ta, vgw), BF16),
            jax.ShapeDtypeStruct((ATT_HEADS, ta, ATT_HD), BF16),
            jax.ShapeDtypeStruct((ATT_KV_HEADS, ta, ATT_HD), BF16),
            jax.ShapeDtypeStruct((ATT_KV_HEADS, ta, ATT_HD), BF16),
        ],
        compiler_params=_cparams(("arbitrary",)),
        name="inproj_even",
    )(xa, gain.reshape(1, d), shift, scale, w, cos_t, s1_t, s2_t, qg, kg, bd)


def rope_tables(seq, tm):
    nf = ATT_HD // 4
    t = jnp.arange(seq)
    rows = (t // GRID_W).astype(F32)
    cols = (t % GRID_W).astype(F32)
    inv = ROPE_BASE ** (-jnp.arange(nf, dtype=F32) / nf)
    lane = np.arange(LANES)
    axis = (lane % ATT_HD) // (ATT_HD // 2)
    f = lane % nf
    upper = ((lane % (ATT_HD // 2)) >= nf)
    pos = jnp.where(jnp.asarray(axis)[None, :] == 0, rows[:, None], cols[:, None])
    ang = pos * inv[jnp.asarray(f)][None, :]
    cos = jnp.cos(ang)
    sin = jnp.sin(ang)
    s1 = jnp.where(jnp.asarray(upper)[None, :], 0.0, -sin)
    s2 = jnp.where(jnp.asarray(upper)[None, :], sin, 0.0)
    pad1 = jnp.ones((tm, LANES), F32)
    pad0 = jnp.zeros((tm, LANES), F32)
    return (jnp.concatenate([cos, pad1]), jnp.concatenate([s1, pad0]), jnp.concatenate([s2, pad0]))


def _retention_kernel(dec_ref, q_ref, k_ref, v_ref, g_ref, s0f_ref, s0b_ref,
                      o_ref, sff_ref, sfb_ref, st_ref, *, n_chunks, unroll):
    hp = pl.program_id(1)
    C = RET_CHUNK
    dk, dv = RET_DK, RET_DV
    pos = lax.broadcasted_iota(jnp.int32, (C, dk), 0).astype(F32)
    ii = lax.broadcasted_iota(jnp.int32, (C, C), 0)
    jj = lax.broadcasted_iota(jnp.int32, (C, C), 1)
    dpos = (ii - jj).astype(F32)
    heads = range(2)
    qs = [slice(hh * dk, (hh + 1) * dk) for hh in heads]
    vs = [slice(hh * dv, (hh + 1) * dv) for hh in heads]
    w_out, w_in, gcf, gcb, mask = [], [], [], [], []
    for hh in heads:
        h = 2 * hp + hh
        df = dec_ref[0, h]
        db = dec_ref[1, h]
        lf = -jnp.exp(jnp.full((C, C), df, F32))
        lb = -jnp.exp(jnp.full((C, C), db, F32))
        lfk = -jnp.exp(jnp.full((C, dk), df, F32))
        lbk = -jnp.exp(jnp.full((C, dk), db, F32))
        w_out.append(jnp.concatenate([jnp.exp(lfk * (C - 1.0 - pos)), jnp.exp(lbk * pos)], axis=1))
        w_in.append(jnp.concatenate([jnp.exp(lfk * (pos + 1.0)), jnp.exp(lbk * (C - pos))], axis=1))
        gcf.append(jnp.exp(-jnp.exp(jnp.full((dk, dv), df, F32)) * C))
        gcb.append(jnp.exp(-jnp.exp(jnp.full((dk, dv), db, F32)) * C))
        mask.append(jnp.where(dpos > 0, jnp.exp(lf * jnp.maximum(dpos, 0.0)),
                              jnp.where(dpos < 0, jnp.exp(lb * jnp.maximum(-dpos, 0.0)), 2.0)))

    def rows(n):
        return pl.ds(pl.multiple_of(n * C, C), C)

    items = [(u, hh) for u in range(unroll) for hh in heads]

    def sums_body(i, carry):
        kk = {}
        for u, hh in items:
            k = k_ref[rows(i * unroll + u), qs[hh]].astype(F32)
            kk[(u, hh)] = (jnp.concatenate([k, k], axis=1) * w_out[hh]).astype(BF16)
        kv = {(u, hh): lax.dot_general(kk[(u, hh)], v_ref[rows(i * unroll + u), vs[hh]], TN_DIMS,
                                       preferred_element_type=F32) for u, hh in items}
        for u, hh in items:
            st_ref[hh, i * unroll + u] = kv[(u, hh)]
        return carry

    lax.fori_loop(0, n_chunks // unroll, sums_body, 0)

    def scan_body(n, carry):
        n_rev = n_chunks - 1 - n
        out = []
        for hh in heads:
            sf, sb = carry[2 * hh], carry[2 * hh + 1]
            kvf = st_ref[hh, n, 0:dk, :]
            kvb = st_ref[hh, n_rev, dk:2 * dk, :]
            st_ref[hh, n, 0:dk, :] = sf
            st_ref[hh, n_rev, dk:2 * dk, :] = sb
            out += [gcf[hh] * sf + kvf, gcb[hh] * sb + kvb]
        return tuple(out)

    init = tuple(x for hh in heads for x in (s0f_ref[hh], s0b_ref[hh]))
    fin = lax.fori_loop(0, n_chunks, scan_body, init)
    for hh in heads:
        sff_ref[hh] = fin[2 * hh]
        sfb_ref[hh] = fin[2 * hh + 1]

    def out_body(i, carry):
        ns = [i * unroll + u for u in range(unroll)]
        qb = {(u, hh): q_ref[rows(ns[u]), qs[hh]] for u, hh in items}
        sc = {(u, hh): lax.dot_general(qb[(u, hh)], k_ref[rows(ns[u]), qs[hh]], NT_DIMS,
                                       preferred_element_type=F32) for u, hh in items}
        qw = {}
        for it in items:
            q = qb[it].astype(F32)
            qw[it] = (jnp.concatenate([q, q], axis=1) * w_in[it[1]]).astype(BF16)
        o1 = {(u, hh): _dot((sc[(u, hh)] * mask[hh]).astype(BF16), v_ref[rows(ns[u]), vs[hh]]) for u, hh in items}
        o2 = {(u, hh): _dot(qw[(u, hh)], st_ref[hh, ns[u]].astype(BF16)) for u, hh in items}
        for it in items:
            u, hh = it
            n = ns[u]
            o = o1[it] + o2[it]
            o = o * lax.rsqrt(jnp.mean(o * o, axis=-1, keepdims=True) + EPS)
            gate = g_ref[rows(n), vs[hh]].astype(F32)
            o_ref[rows(n), vs[hh]] = (_silu(gate) * o).astype(o_ref.dtype)
        return carry

    lax.fori_loop(0, n_chunks // unroll, out_body, 0)


def retention(dec, rq, rk, p, s0f, s0b, nb, seq, row_off_blocks):
    n_chunks = seq // RET_CHUNK
    hp_n = RET_HEADS // 2
    vcol = 0
    gcol = vcol + RET_HEADS * RET_DV // (2 * RET_DV)
    ta = rq.shape[0]
    st_spec = pl.BlockSpec((None, 2, RET_DK, RET_DV), lambda b, hp, *_: (b, hp, 0, 0))
    grid_spec = pltpu.PrefetchScalarGridSpec(
        num_scalar_prefetch=1,
        grid=(nb, hp_n),
        in_specs=[
            pl.BlockSpec((seq, 2 * RET_DK), lambda b, hp, *_: (row_off_blocks + b, hp)),
            pl.BlockSpec((seq, 2 * RET_DK), lambda b, hp, *_: (row_off_blocks + b, hp)),
            pl.BlockSpec((seq, 2 * RET_DV), lambda b, hp, *_: (row_off_blocks + b, vcol + hp)),
            pl.BlockSpec((seq, 2 * RET_DV), lambda b, hp, *_: (row_off_blocks + b, gcol + hp)),
            st_spec, st_spec,
        ],
        out_specs=[
            pl.BlockSpec((seq, 2 * RET_DV), lambda b, hp, *_: (b, hp)),
            st_spec, st_spec,
        ],
        scratch_shapes=[pltpu.VMEM((2, n_chunks, 2 * RET_DK, RET_DV), F32)],
    )
    st_shape = jax.ShapeDtypeStruct((nb, RET_HEADS, RET_DK, RET_DV), F32)
    return pl.pallas_call(
        functools.partial(_retention_kernel, n_chunks=n_chunks, unroll=math.gcd(n_chunks, 4)),
        grid_spec=grid_spec,
        out_shape=[jax.ShapeDtypeStruct((nb * seq, RET_HEADS * RET_DV), BF16), st_shape, st_shape],
        compiler_params=_cparams(("arbitrary", "arbitrary")),
        name="retention",
    )(dec, rq, rk, p, p, s0f, s0b)


ATT_VT_ROWS = ATT_HD + 16


ATT_PAIRS_PER_TRIP = 4


def _attn_kernel(q_ref, k_ref, vt_ref, o_ref, *s_refs, tk, c_start, c_end, rep):
    tq = q_ref.shape[1]
    sets = (s_refs[:rep], s_refs[rep:])
    last = c_end - 1

    def scores(bufs, j):
        j = jnp.minimum(j, last)
        c0 = pl.multiple_of(j * tk, tk)
        k = k_ref[pl.ds(c0, tk), :]
        mxs = []
        for r in range(rep):
            s = lax.dot_general(k, q_ref[r], NT_DIMS, preferred_element_type=F32)
            bufs[r][...] = s
            mxs.append(jnp.max(s, axis=0, keepdims=True))
        return tuple(mxs)

    def softmax_pv(bufs, j, mxs, ms, accs):
        vt = vt_ref[j]
        new_m, new_acc = [], []
        for r in range(rep):
            m_new = jnp.maximum(ms[r], mxs[r])
            a = jnp.exp2(ms[r] - m_new)
            p = jnp.exp2(bufs[r][...] - m_new).astype(BF16)
            new_acc.append(a * accs[r] + _dot(vt, p))
            new_m.append(m_new)
        return tuple(new_m), tuple(new_acc)

    def pair(j, mx0, ms, accs):
        mx1 = scores(sets[1], j + 1)
        ms, accs = softmax_pv(sets[0], j, mx0, ms, accs)
        mx0 = scores(sets[0], j + 2)
        ms, accs = softmax_pv(sets[1], j + 1, mx1, ms, accs)
        return mx0, ms, accs

    def trip(t, carry):
        for u in range(ATT_PAIRS_PER_TRIP):
            carry = pair(c_start + 2 * (ATT_PAIRS_PER_TRIP * t + u), *carry)
        return carry

    n_pairs = (c_end - c_start) // 2
    n_trips = n_pairs // ATT_PAIRS_PER_TRIP
    ms = tuple(jnp.full((1, tq), -1e30, F32) for _ in range(rep))
    accs = tuple(jnp.zeros((ATT_VT_ROWS, tq), F32) for _ in range(rep))
    carry = (scores(sets[0], c_start), ms, accs)
    if n_trips:
        carry = lax.fori_loop(0, n_trips, trip, carry)
    for u in range(n_trips * ATT_PAIRS_PER_TRIP, n_pairs):
        carry = pair(c_start + 2 * u, *carry)
    mx0, ms, accs = carry
    if (c_end - c_start) % 2:
        ms, accs = softmax_pv(sets[0], last, mx0, ms, accs)
    outs = [(acc[:ATT_HD, :] / acc[ATT_HD:ATT_HD + 1, :]).T for acc in accs]
    o_ref[...] = jnp.concatenate(outs, axis=-1).astype(o_ref.dtype)


def attention(aq, kcat, vtcat, seq_q, q_off_blocks, c_start, tq, tk):
    rep = ATT_HEADS // ATT_KV_HEADS
    _, nb, lk, _ = kcat.shape
    nq = seq_q // tq
    n_chunks = lk // tk
    return pl.pallas_call(
        functools.partial(_attn_kernel, tk=tk, c_start=c_start, c_end=n_chunks, rep=rep),
        grid=(nb, ATT_KV_HEADS, nq),
        in_specs=[
            pl.BlockSpec((rep, tq, ATT_HD), lambda b, g, i: (g, q_off_blocks + b * nq + i, 0)),
            pl.BlockSpec((None, None, lk, ATT_HD), lambda b, g, i: (g, b, 0, 0)),
            pl.BlockSpec((None, None, n_chunks, ATT_VT_ROWS, tk), lambda b, g, i: (g, b, 0, 0, 0)),
        ],
        out_specs=pl.BlockSpec((tq, rep * ATT_HD), lambda b, g, i: (b * nq + i, g)),
        out_shape=jax.ShapeDtypeStruct((nb * seq_q, ATT_HEADS * ATT_HD), BF16),
        scratch_shapes=[pltpu.VMEM((tk, tq), F32) for _ in range(2 * rep)],
        compiler_params=_cparams(("arbitrary", "arbitrary", "arbitrary")),
        name="attention",
    )(aq, kcat, vtcat)


def _lat_ctx_specs(tm, width, n_lat_tiles):
    return [pl.BlockSpec((tm, width), lambda i: (jnp.minimum(i, n_lat_tiles - 1), 0)),
            pl.BlockSpec((tm, width), lambda i: (jnp.maximum(i - n_lat_tiles, 0), 0))]


def _outproj_even_kernel(r_lat, r_ctx, a_lat, a_ctx, w1_ref, w2_ref, res_ref, gate_ref, o_ref, *, n_lat_tiles):
    is_lat = pl.program_id(0) < n_lat_tiles
    a1 = jnp.where(is_lat, r_lat[...], r_ctx[...])
    a2 = jnp.where(is_lat, a_lat[...], a_ctx[...])
    y = _dot(a1, w1_ref[...]) + _dot(a2, w2_ref[...])
    o_ref[...] = res_ref[...] + gate_ref[...] * y


def outproj_even(ret_lat, ret_ctx, att_lat, att_ctx, w1, w2, xa, gate, tm, seq, nb):
    ta, d = xa.shape
    k1, k2 = w1.shape[0], w2.shape[0]
    n_lat_tiles = ret_lat.shape[0] // tm
    return pl.pallas_call(
        functools.partial(_outproj_even_kernel, n_lat_tiles=n_lat_tiles),
        grid=(ta // tm,),
        in_specs=_lat_ctx_specs(tm, k1, n_lat_tiles) + _lat_ctx_specs(tm, k2, n_lat_tiles) + [
            pl.BlockSpec((k1, d), lambda i: (0, 0)),
            pl.BlockSpec((k2, d), lambda i: (0, 0)),
            pl.BlockSpec((tm, d), lambda i: (i, 0)),
            pl.BlockSpec((None, 1, d), _mod_row_map(tm, seq, nb)),
        ],
        out_specs=pl.BlockSpec((tm, d), lambda i: (i, 0)),
        out_shape=jax.ShapeDtypeStruct((ta, d), F32),
        compiler_params=_cparams(("arbitrary",)),
        name="outproj_even",
    )(ret_lat, ret_ctx, att_lat, att_ctx, w1, w2, xa, gate)


def _outproj_odd_kernel(f_lat, f_ctx, b_lat, b_ctx, z_ref, og_ref, w_ref, res_ref, gate_ref, o_ref,
                        *, n_lat_tiles):
    is_lat = pl.program_id(0) < n_lat_tiles
    og = og_ref[...]
    parts = []
    for h in range(DN_HEADS):
        cs = slice(h * DN_DV, (h + 1) * DN_DV)
        of = jnp.where(is_lat, f_lat[:, cs], f_ctx[:, cs]).astype(F32)
        ob = jnp.where(is_lat, b_lat[:, cs], b_ctx[:, cs]).astype(F32)
        o = of + ob
        o = o * lax.rsqrt(jnp.mean(o * o, axis=-1, keepdims=True) + EPS) * og
        parts.append((o * _silu(z_ref[:, cs].astype(F32))).astype(BF16))
    a = jnp.concatenate(parts, axis=-1)
    o_ref[...] = res_ref[...] + gate_ref[...] * _dot(a, w_ref[...])


def outproj_odd(of_lat, of_ctx, ob_lat, ob_ctx, p, out_gain, w, xa, gate, tm, seq, nb):
    ta, d = xa.shape
    kdim = DN_HEADS * DN_DV
    n_lat_tiles = of_lat.shape[0] // tm
    return pl.pallas_call(
        functools.partial(_outproj_odd_kernel, n_lat_tiles=n_lat_tiles),
        grid=(ta // tm,),
        in_specs=_lat_ctx_specs(tm, kdim, n_lat_tiles) + _lat_ctx_specs(tm, kdim, n_lat_tiles) + [
            pl.BlockSpec((tm, kdim), lambda i: (i, DN_QKV // kdim)),
            pl.BlockSpec((1, DN_DV), lambda i: (0, 0)),
            pl.BlockSpec((kdim, d), lambda i: (0, 0)),
            pl.BlockSpec((tm, d), lambda i: (i, 0)),
            pl.BlockSpec((None, 1, d), _mod_row_map(tm, seq, nb)),
        ],
        out_specs=pl.BlockSpec((tm, d), lambda i: (i, 0)),
        out_shape=jax.ShapeDtypeStruct((ta, d), F32),
        compiler_params=_cparams(("arbitrary",)),
        name="outproj_odd",
    )(of_lat, of_ctx, ob_lat, ob_ctx, p, out_gain.reshape(1, DN_DV).astype(F32), w, xa, gate)


def _prep_odd_kernel(first_ref, last_ref, x_ref, prev_ref, next_ref, ab_ref, cw_ref, arow_ref, brow_ref,
                     q_ref, k_ref, v_ref, gb_ref):
    i = pl.program_id(0)
    tm = x_ref.shape[0]
    hrows = prev_ref.shape[0]
    keep_prev = 1.0 - first_ref[i].astype(F32)
    keep_next = 1.0 - last_ref[i].astype(F32)
    row = lax.broadcasted_iota(jnp.int32, (tm, LANES), 0)
    is_first = row == 0
    is_last = row == tm - 1
    n_qk = 2 * DN_HEADS * DN_DK // LANES
    n_q = DN_HEADS * DN_DK // LANES
    outs = (q_ref, k_ref, v_ref)
    for j in range(DN_QKV // LANES):
        cs = slice(j * LANES, (j + 1) * LANES)
        x = x_ref[:, cs].astype(F32)
        xp = prev_ref[:, cs].astype(F32)[hrows - 1:hrows, :] * keep_prev
        xn = next_ref[:, cs].astype(F32)[0:1, :] * keep_next
        x_dn = jnp.where(is_first, xp, pltpu.roll(x, 1, 0))
        x_up = jnp.where(is_last, xn, pltpu.roll(x, tm - 1, 0))
        w = cw_ref[:, cs]
        y = _silu(x_dn * w[0:1, :] + x * w[1:2, :] + x_up * w[2:3, :])
        if j < n_qk:
            y = y * lax.rsqrt(jnp.sum(y * y, axis=-1, keepdims=True) + EPS)
            if j < n_q:
                y = y * DN_DK ** -0.5
        lj = j % n_q
        outs[j // n_q][:, lj * LANES:(lj + 1) * LANES] = y.astype(BF16)

    a = ab_ref[...].astype(F32)
    lane = lax.broadcasted_iota(jnp.int32, (tm, LANES), 1)
    z = a + brow_ref[...]
    softplus = jnp.maximum(z, 0.0) + jnp.log(1.0 + jnp.exp(-jnp.abs(z)))
    g = -jnp.exp(arow_ref[...]) * softplus
    beta = 1.0 / (1.0 + jnp.exp(-a))
    gb_ref[...] = jnp.where(lane < 2 * DN_HEADS, g, jnp.where(lane < 4 * DN_HEADS, beta, 0.0))


def prep_odd(p, conv_w, arow, brow, first_flags, last_flags, tm):
    ta = p.shape[0]
    halo = 16
    hb = tm // halo
    n_h = ta // halo
    kdim = DN_HEADS * DN_DK
    grid_spec = pltpu.PrefetchScalarGridSpec(
        num_scalar_prefetch=2,
        grid=(ta // tm,),
        in_specs=[
            pl.BlockSpec((tm, DN_QKV), lambda i, *_: (i, 0)),
            pl.BlockSpec((halo, DN_QKV), lambda i, *_: (jnp.maximum(i * hb - 1, 0), 0)),
            pl.BlockSpec((halo, DN_QKV), lambda i, *_: (jnp.minimum((i + 1) * hb, n_h - 1), 0)),
            pl.BlockSpec((tm, LANES), lambda i, *_: (i, (DN_QKV + DN_HEADS * DN_DV) // LANES)),
            pl.BlockSpec((DN_CONV, DN_QKV), lambda i, *_: (0, 0)),
            pl.BlockSpec((1, LANES), lambda i, *_: (0, 0)),
            pl.BlockSpec((1, LANES), lambda i, *_: (0, 0)),
        ],
        out_specs=[
            pl.BlockSpec((tm, kdim), lambda i, *_: (i, 0)),
            pl.BlockSpec((tm, kdim), lambda i, *_: (i, 0)),
            pl.BlockSpec((tm, kdim), lambda i, *_: (i, 0)),
            pl.BlockSpec((tm, LANES), lambda i, *_: (i, 0)),
        ],
    )
    return pl.pallas_call(
        _prep_odd_kernel,
        grid_spec=grid_spec,
        out_shape=[
            jax.ShapeDtypeStruct((ta, kdim), BF16),
            jax.ShapeDtypeStruct((ta, kdim), BF16),
            jax.ShapeDtypeStruct((ta, kdim), BF16),
            jax.ShapeDtypeStruct((ta, LANES), F32),
        ],
        compiler_params=_cparams(("arbitrary",)),
        name="prep_odd",
    )(first_flags, last_flags, p, p, p, p, conv_w, arow, brow)


def _deltanet_kernel(q_ref, k_ref, v_ref, gb_ref, gbt_ref, s0_ref, o_ref, sf_ref, s_ref,
                     *, reverse, n_chunks, dir_off):
    t = pl.program_id(1)

    @pl.when(t == 0)
    def _():
        s_ref[...] = s0_ref[...]

    C = DN_CHUNK
    ii = lax.broadcasted_iota(jnp.int32, (C, C), 0)
    jj = lax.broadcasted_iota(jnp.int32, (C, C), 1)
    if reverse:
        incl = ii <= jj
        strict = ii < jj
    else:
        incl = ii >= jj
        strict = ii > jj
    tri = jnp.where(incl, 1.0, 0.0).astype(F32)
    if reverse:
        tri_t = jnp.where(ii >= jj, 1.0, 0.0).astype(F32)
    else:
        tri_t = jnp.where(ii <= jj, 1.0, 0.0).astype(F32)
    eye = jnp.where(ii == jj, 1.0, 0.0).astype(F32)
    blk = ii ^ jj

    order = list(range(n_chunks - 1, -1, -1) if reverse else range(n_chunks))
    items = [(c, h) for c in order for h in range(DN_HEADS)]
    gcols, grows, gbs = {}, {}, {}
    for c in order:
        gb_c = gb_ref[c * C:(c + 1) * C, :]
        gbs[c] = gb_c
        gcols[c] = jnp.dot(tri, gb_c, preferred_element_type=F32, precision=HIGHEST)
        grows[c] = jnp.dot(gbt_ref[c], tri_t, preferred_element_type=F32, precision=HIGHEST)

    qb, kb16, decay, kbeta, egc, kd, gl, rhs = {}, {}, {}, {}, {}, {}, {}, {}
    for it in items:
        c, h = it
        gi = dir_off + h
        bi = 2 * DN_HEADS + dir_off + h
        rows = slice(c * C, (c + 1) * C)
        cs = slice(h * DN_DK, (h + 1) * DN_DK)
        gc = gcols[c][:, gi:gi + 1]
        gr = grows[c][gi:gi + 1, :]
        beta = gbs[c][:, bi:bi + 1]
        qb[it] = q_ref[rows, cs]
        kb16[it] = k_ref[rows, cs]
        kf = kb16[it].astype(F32)
        decay[it] = jnp.where(incl, jnp.exp(jnp.where(incl, gc - gr, 0.0)), 0.0)
        kbeta[it] = kf * beta
        egc[it] = jnp.exp(gc)
        glast = gc[0:1, :] if reverse else gc[C - 1:C, :]
        kd[it] = (kf * jnp.exp(glast - gc)).astype(BF16)
        gl[it] = jnp.exp(glast)
        rhs[it] = jnp.concatenate([v_ref[rows, cs].astype(F32) * beta, kbeta[it] * egc[it]], axis=1).astype(BF16)

    kk = {it: lax.dot_general(kbeta[it].astype(BF16), kb16[it], NT_DIMS, preferred_element_type=F32)
          for it in items}
    qk = {it: lax.dot_general(qb[it], kb16[it], NT_DIMS, preferred_element_type=F32) for it in items}
    lm = {it: jnp.where(strict, kk[it] * decay[it], 0.0) for it in items}
    attn = {it: jnp.where(incl, qk[it] * decay[it], 0.0).astype(BF16) for it in items}
    dinv = {it: eye - jnp.where(blk < 2, lm[it], 0.0) for it in items}
    s = 2
    while s < C:
        in_band = jnp.logical_and(blk >= s, blk < 2 * s)
        tmp = {it: _dot(dinv[it].astype(BF16), jnp.where(in_band, lm[it], 0.0).astype(BF16)) for it in items}
        dinv = {it: dinv[it] - _dot(tmp[it].astype(BF16), dinv[it].astype(BF16)) for it in items}
        s *= 2
    uw = {it: _dot(dinv[it].astype(BF16), rhs[it]) for it in items}
    wq = {it: jnp.concatenate([uw[it][:, DN_DV:], qb[it].astype(F32) * egc[it]], axis=0).astype(BF16)
          for it in items}

    states = [s_ref[h] for h in range(DN_HEADS)]
    for c in order:
        its = [(c, h) for h in range(DN_HEADS)]
        r = {it: _dot(wq[it], states[it[1]].astype(BF16)) for it in its}
        v_new = {it: (uw[it][:, :DN_DV] - r[it][:C]).astype(BF16) for it in its}
        o = {it: r[it][C:] + _dot(attn[it], v_new[it]) for it in its}
        for it in its:
            h = it[1]
            states[h] = states[h] * gl[it] + lax.dot_general(kd[it], v_new[it], TN_DIMS,
                                                             preferred_element_type=F32)
        for it in its:
            h = it[1]
            o_ref[c * C:(c + 1) * C, h * DN_DK:(h + 1) * DN_DK] = o[it].astype(o_ref.dtype)
    for h in range(DN_HEADS):
        s_ref[h] = states[h]

    @pl.when(t == pl.num_programs(1) - 1)
    def _():
        sf_ref[...] = s_ref[...]


def deltanet(q, k, v, gb, gbt, s0, nb, seq, row_off, reverse, tl):
    nblk = seq // tl
    n_chunks = tl // DN_CHUNK
    off_b = row_off // tl
    kdim = DN_HEADS * DN_DK

    def rb(b, t):
        tt = nblk - 1 - t if reverse else t
        return off_b + b * nblk + tt

    seq_spec = pl.BlockSpec((tl, kdim), lambda b, t: (rb(b, t), 0))
    st_spec = pl.BlockSpec((None, DN_HEADS, DN_DK, DN_DV), lambda b, t: (b, 0, 0, 0))
    return pl.pallas_call(
        functools.partial(_deltanet_kernel, reverse=reverse, n_chunks=n_chunks,
                          dir_off=DN_HEADS if reverse else 0),
        grid=(nb, nblk),
        in_specs=[
            seq_spec, seq_spec, seq_spec,
            pl.BlockSpec((tl, LANES), lambda b, t: (rb(b, t), 0)),
            pl.BlockSpec((n_chunks, 4 * DN_HEADS, DN_CHUNK), lambda b, t: (rb(b, t), 0, 0)),
            st_spec,
        ],
        out_specs=[
            pl.BlockSpec((tl, kdim), lambda b, t: (b * nblk + (nblk - 1 - t if reverse else t), 0)),
            st_spec,
        ],
        out_shape=[
            jax.ShapeDtypeStruct((nb * seq, kdim), BF16),
            jax.ShapeDtypeStruct((nb, DN_HEADS, DN_DK, DN_DV), F32),
        ],
        scratch_shapes=[pltpu.VMEM((DN_HEADS, DN_DK, DN_DV), F32)],
        compiler_params=_cparams(("arbitrary", "arbitrary")),
        name="deltanet_bwd" if reverse else "deltanet_fwd",
    )(q, k, v, gb, gbt, s0)


def _deltanet_bidir_kernel(qf_ref, kf_ref, vf_ref, gbf_ref, gbtf_ref, qb_ref, kb_ref, vb_ref, gbb_ref, gbtb_ref,
                           s0f_ref, s0b_ref, of_ref, ob_ref, sff_ref, sfb_ref, sf_scr, sb_scr, *, n_chunks):
    t = pl.program_id(1)

    @pl.when(t == 0)
    def _():
        sf_scr[...] = s0f_ref[...]
        sb_scr[...] = s0b_ref[...]

    C = DN_CHUNK
    ii = lax.broadcasted_iota(jnp.int32, (C, C), 0)
    jj = lax.broadcasted_iota(jnp.int32, (C, C), 1)
    lower, upper = ii >= jj, ii <= jj
    eye = jnp.where(ii == jj, 1.0, 0.0).astype(F32)
    blk = ii ^ jj
    dirs = (
        dict(rev=False, incl=lower, strict=ii > jj, q=qf_ref, k=kf_ref, v=vf_ref, gb=gbf_ref, gbt=gbtf_ref,
             o=of_ref, scr=sf_scr, off=0, order=list(range(n_chunks))),
        dict(rev=True, incl=upper, strict=ii < jj, q=qb_ref, k=kb_ref, v=vb_ref, gb=gbb_ref, gbt=gbtb_ref,
             o=ob_ref, scr=sb_scr, off=DN_HEADS, order=list(range(n_chunks - 1, -1, -1))),
    )
    items = [(d, c, h) for d in range(2) for c in dirs[d]["order"] for h in range(DN_HEADS)]

    gcols, grows, gbs = {}, {}, {}
    for d, dr in enumerate(dirs):
        tri = jnp.where(dr["incl"], 1.0, 0.0).astype(F32)
        tri_t = jnp.where(upper if not dr["rev"] else lower, 1.0, 0.0).astype(F32)
        for c in dr["order"]:
            gb_c = dr["gb"][c * C:(c + 1) * C, :]
            gbs[(d, c)] = gb_c
            gcols[(d, c)] = jnp.dot(tri, gb_c, preferred_element_type=F32, precision=HIGHEST)
            grows[(d, c)] = jnp.dot(dr["gbt"][c], tri_t, preferred_element_type=F32, precision=HIGHEST)

    qb, kb16, decay, kbeta, egc, kd, gl, rhs = {}, {}, {}, {}, {}, {}, {}, {}
    for it in items:
        d, c, h = it
        dr = dirs[d]
        gi = dr["off"] + h
        bi = 2 * DN_HEADS + dr["off"] + h
        rows = slice(c * C, (c + 1) * C)
        cs = slice(h * DN_DK, (h + 1) * DN_DK)
        gc = gcols[(d, c)][:, gi:gi + 1]
        gr = grows[(d, c)][gi:gi + 1, :]
        beta = gbs[(d, c)][:, bi:bi + 1]
        qb[it] = dr["q"][rows, cs]
        kb16[it] = dr["k"][rows, cs]
        kf = kb16[it].astype(F32)
        decay[it] = jnp.where(dr["incl"], jnp.exp(jnp.where(dr["incl"], gc - gr, 0.0)), 0.0)
        kbeta[it] = kf * beta
        egc[it] = jnp.exp(gc)
        glast = gc[0:1, :] if dr["rev"] else gc[C - 1:C, :]
        kd[it] = (kf * jnp.exp(glast - gc)).astype(BF16)
        gl[it] = jnp.exp(glast)
        rhs[it] = jnp.concatenate([dr["v"][rows, cs].astype(F32) * beta, kbeta[it] * egc[it]], axis=1).astype(BF16)

    kk = {it: lax.dot_general(kbeta[it].astype(BF16), kb16[it], NT_DIMS, preferred_element_type=F32)
          for it in items}
    qk = {it: lax.dot_general(qb[it], kb16[it], NT_DIMS, preferred_element_type=F32) for it in items}
    lm = {it: jnp.where(dirs[it[0]]["strict"], kk[it] * decay[it], 0.0) for it in items}
    attn = {it: jnp.where(dirs[it[0]]["incl"], qk[it] * decay[it], 0.0).astype(BF16) for it in items}
    dinv = {it: eye - jnp.where(blk < 2, lm[it], 0.0) for it in items}
    s = 2
    while s < C:
        in_band = jnp.logical_and(blk >= s, blk < 2 * s)
        tmp = {it: _dot(dinv[it].astype(BF16), jnp.where(in_band, lm[it], 0.0).astype(BF16)) for it in items}
        dinv = {it: dinv[it] - _dot(tmp[it].astype(BF16), dinv[it].astype(BF16)) for it in items}
        s *= 2
    uw = {it: _dot(dinv[it].astype(BF16), rhs[it]) for it in items}
    wq = {it: jnp.concatenate([uw[it][:, DN_DV:], qb[it].astype(F32) * egc[it]], axis=0).astype(BF16)
          for it in items}

    states = {(d, h): dirs[d]["scr"][h] for d in range(2) for h in range(DN_HEADS)}
    for step in range(n_chunks):
        its = [(d, dirs[d]["order"][step], h) for d in range(2) for h in range(DN_HEADS)]
        r = {it: _dot(wq[it], states[(it[0], it[2])].astype(BF16)) for it in its}
        v_new = {it: (uw[it][:, :DN_DV] - r[it][:C]).astype(BF16) for it in its}
        o = {it: r[it][C:] + _dot(attn[it], v_new[it]) for it in its}
        for it in its:
            key = (it[0], it[2])
            states[key] = states[key] * gl[it] + lax.dot_general(kd[it], v_new[it], TN_DIMS,
                                                                 preferred_element_type=F32)
        for it in its:
            d, c, h = it
            dirs[d]["o"][c * C:(c + 1) * C, h * DN_DK:(h + 1) * DN_DK] = o[it].astype(of_ref.dtype)
    for (d, h), st in states.items():
        dirs[d]["scr"][h] = st

    @pl.when(t == pl.num_programs(1) - 1)
    def _():
        sff_ref[...] = sf_scr[...]
        sfb_ref[...] = sb_scr[...]


def deltanet_bidir(q, k, v, gb, gbt, s0f, s0b, nb, seq, row_off, tl):
    nblk = seq // tl
    n_chunks = tl // DN_CHUNK
    off_b = row_off // tl
    kdim = DN_HEADS * DN_DK

    def fwd_rb(b, t):
        return off_b + b * nblk + t

    def bwd_rb(b, t):
        return off_b + b * nblk + (nblk - 1 - t)

    def seq_specs(rb):
        spec = pl.BlockSpec((tl, kdim), lambda b, t: (rb(b, t), 0))
        return [spec, spec, spec,
                pl.BlockSpec((tl, LANES), lambda b, t: (rb(b, t), 0)),
                pl.BlockSpec((n_chunks, 4 * DN_HEADS, DN_CHUNK), lambda b, t: (rb(b, t), 0, 0))]

    st_spec = pl.BlockSpec((None, DN_HEADS, DN_DK, DN_DV), lambda b, t: (b, 0, 0, 0))
    st_shape = jax.ShapeDtypeStruct((nb, DN_HEADS, DN_DK, DN_DV), F32)
    o_shape = jax.ShapeDtypeStruct((nb * seq, kdim), BF16)
    return pl.pallas_call(
        functools.partial(_deltanet_bidir_kernel, n_chunks=n_chunks),
        grid=(nb, nblk),
        in_specs=seq_specs(fwd_rb) + seq_specs(bwd_rb) + [st_spec, st_spec],
        out_specs=[
            pl.BlockSpec((tl, kdim), lambda b, t: (b * nblk + t, 0)),
            pl.BlockSpec((tl, kdim), lambda b, t: (b * nblk + nblk - 1 - t, 0)),
            st_spec, st_spec,
        ],
        out_shape=[o_shape, o_shape, st_shape, st_shape],
        scratch_shapes=[pltpu.VMEM((DN_HEADS, DN_DK, DN_DV), F32), pltpu.VMEM((DN_HEADS, DN_DK, DN_DV), F32)],
        compiler_params=_cparams(("arbitrary", "arbitrary")),
        name="deltanet_bidir",
    )(q, k, v, gb, gbt, q, k, v, gb, gbt, s0f, s0b)


def _router_kernel(x_ref, g_ref, sh_ref, sc_ref, wr_ref, br_ref, ltri_ref, f_ref, r_ref, cnt_ref, base_ref):
    @pl.when(pl.program_id(0) == 0)
    def _():
        base_ref[...] = jnp.zeros_like(base_ref)

    h = _norm_mod(x_ref[...], g_ref[...], sh_ref[...], sc_ref[...])
    f_ref[...] = h
    logits = _dot(h.astype(BF16), wr_ref[...]) + br_ref[...]
    tm = logits.shape[0]
    lane = lax.broadcasted_iota(jnp.int32, (tm, LANES), 1)
    neg = -1e30
    big = 4 * LANES
    is_g = lane < N_GROUPS
    gl = jnp.where(is_g, logits, neg)
    gm = jnp.max(gl, axis=-1, keepdims=True)
    grp = jnp.min(jnp.where(gl == gm, lane, big), axis=-1, keepdims=True)
    psum = jnp.sum(jnp.where(is_g, jnp.exp(gl - gm), 0.0), axis=-1, keepdims=True)
    p_grp = 1.0 / psum
    e_lane = lane - N_GROUPS
    in_grp = jnp.logical_and(jnp.logical_and(e_lane >= 0, e_lane < N_EXPERTS),
                             (e_lane // EXPERTS_PER_GROUP) == grp)
    el = jnp.where(in_grp, logits, neg)
    m1 = jnp.max(el, axis=-1, keepdims=True)
    i1 = jnp.min(jnp.where(el == m1, lane, big), axis=-1, keepdims=True)
    el2 = jnp.where(lane == i1, neg, el)
    m2 = jnp.max(el2, axis=-1, keepdims=True)
    i2 = jnp.min(jnp.where(el2 == m2, lane, big), axis=-1, keepdims=True)
    e21 = jnp.exp(m2 - m1)
    w1 = p_grp / (1.0 + e21)
    w2 = p_grp * e21 / (1.0 + e21)
    e1 = (i1 - N_GROUPS).astype(F32)
    e2 = (i2 - N_GROUPS).astype(F32)
    oh1 = lane == i1
    oh2 = lane == i2
    oh1f = jnp.where(oh1, 1.0, 0.0)
    oh2f = jnp.where(oh2, 1.0, 0.0)
    ltri = ltri_ref[...]
    before1 = _dot(ltri, oh1f.astype(BF16))
    before2 = _dot(ltri, oh2f.astype(BF16))
    cnt1 = jnp.sum(oh1f, axis=0, keepdims=True)
    cnt2 = jnp.sum(oh2f, axis=0, keepdims=True)
    base = base_ref[0:1, :]
    rank1 = jnp.sum(jnp.where(oh1, base + before1, 0.0), axis=-1, keepdims=True)
    rank2 = jnp.sum(jnp.where(oh2, base + cnt1 + before2, 0.0), axis=-1, keepdims=True)
    total = base + cnt1 + cnt2
    base_ref[...] = jnp.broadcast_to(total, base_ref.shape)
    cnt_ref[...] = jnp.broadcast_to(total, cnt_ref.shape)
    vals = (e1, e2, w1, w2, rank1, rank2)
    out = jnp.zeros((tm, LANES), F32)
    for idx, val in enumerate(vals):
        out = jnp.where(lane == idx, val, out)
    r_ref[...] = out


def moe_router(xa, gain, shift, scale, w_router, b_router, tm, seq, nb):
    ta, d = xa.shape
    mrow = _mod_row_map(tm, seq, nb)
    ii = np.arange(tm)
    ltri = jnp.asarray((ii[:, None] > ii[None, :]).astype(np.float32)).astype(BF16)
    return pl.pallas_call(
        _router_kernel,
        grid=(ta // tm,),
        in_specs=[
            pl.BlockSpec((tm, d), lambda i: (i, 0)),
            pl.BlockSpec((1, d), lambda i: (0, 0)),
            pl.BlockSpec((None, 1, d), mrow),
            pl.BlockSpec((None, 1, d), mrow),
            pl.BlockSpec((d, LANES), lambda i: (0, 0)),
            pl.BlockSpec((1, LANES), lambda i: (0, 0)),
            pl.BlockSpec((tm, tm), lambda i: (0, 0)),
        ],
        out_specs=[pl.BlockSpec((tm, d), lambda i: (i, 0)), pl.BlockSpec((tm, LANES), lambda i: (i, 0)),
                   pl.BlockSpec((8, LANES), lambda i: (0, 0))],
        out_shape=[jax.ShapeDtypeStruct((ta, d), F32), jax.ShapeDtypeStruct((ta, LANES), F32),
                   jax.ShapeDtypeStruct((8, LANES), F32)],
        scratch_shapes=[pltpu.VMEM((8, LANES), F32)],
        compiler_params=_cparams(("arbitrary",)),
        name="moe_router",
    )(xa, gain.reshape(1, d), shift, scale, w_router, b_router, ltri)


ROW_DMA_UNROLL = 8


def _issue_row_copies(n_rows, make_copy):
    def trip(i, carry):
        for u in range(ROW_DMA_UNROLL):
            make_copy(i * ROW_DMA_UNROLL + u).start(priority=u % 2)
        return carry

    lax.fori_loop(0, n_rows // ROW_DMA_UNROLL, trip, 0)


def _moe_scatter_kernel(pos_ref, f_ref, xs_in, xs_out, sem):
    del xs_in
    tm = f_ref.shape[0]
    for k in range(TOP_K):
        _issue_row_copies(tm, lambda r, k=k: pltpu.make_async_copy(
            f_ref.at[pl.ds(r, 1)], xs_out.at[pl.ds(pos_ref[0, 0, k * tm + r], 1)], sem))
    for _ in range(2):
        pltpu.make_async_copy(f_ref, xs_out.at[pl.ds(0, tm)], sem).wait()


def moe_scatter(pos_tiles, f, xs_zero, tm):
    ta, d = f.shape
    return pl.pallas_call(
        _moe_scatter_kernel,
        grid=(ta // tm,),
        in_specs=[
            pl.BlockSpec((1, 1, 2 * tm), lambda i: (i, 0, 0), memory_space=pltpu.SMEM),
            pl.BlockSpec((tm, d), lambda i: (i, 0)),
            pl.BlockSpec(memory_space=pl.ANY),
        ],
        out_specs=pl.BlockSpec(memory_space=pl.ANY),
        out_shape=jax.ShapeDtypeStruct(xs_zero.shape, xs_zero.dtype),
        scratch_shapes=[pltpu.SemaphoreType.DMA(())],
        input_output_aliases={2: 0},
        compiler_params=_cparams(("arbitrary",)),
        name="moe_scatter",
    )(pos_tiles, f, xs_zero)


def _moe_ffn_kernel(te_ref, nu_ref, x_ref, wgu_ref, wd_ref, o_ref, wgu_bf, wd_bf):
    i = pl.program_id(0)
    fdim = wd_bf.shape[0]

    @pl.when(i < nu_ref[0])
    def _():
        prev = te_ref[jnp.maximum(i - 1, 0)]
        changed = jnp.logical_or(i == 0, te_ref[i] != prev)

        @pl.when(changed)
        def _():
            wgu_bf[...] = wgu_ref[...].astype(BF16)
            wd_bf[...] = wd_ref[...].astype(BF16)

        gu = _dot(x_ref[...].astype(BF16), wgu_bf[...])
        hmid = _silu(gu[:, :fdim]) * gu[:, fdim:]
        o_ref[...] = _dot(hmid.astype(BF16), wd_bf[...])

    @pl.when(i >= nu_ref[0])
    def _():
        o_ref[...] = jnp.zeros_like(o_ref)


def moe_ffn(tile_expert, n_used, xs, w_gate_up, w_down, layer, tm):
    n_pad, d = xs.shape
    f2 = w_gate_up.shape[-1]
    fdim = w_down.shape[-2]
    grid_spec = pltpu.PrefetchScalarGridSpec(
        num_scalar_prefetch=2,
        grid=(n_pad // tm,),
        in_specs=[
            pl.BlockSpec((tm, d), lambda i, te, nu: (i, 0)),
            pl.BlockSpec((None, None, d, f2), lambda i, te, nu: (layer, te[i], 0, 0)),
            pl.BlockSpec((None, None, fdim, d), lambda i, te, nu: (layer, te[i], 0, 0)),
        ],
        out_specs=pl.BlockSpec((tm, d), lambda i, te, nu: (i, 0)),
        scratch_shapes=[pltpu.VMEM((d, f2), BF16), pltpu.VMEM((fdim, d), BF16)],
    )
    return pl.pallas_call(
        _moe_ffn_kernel,
        grid_spec=grid_spec,
        out_shape=jax.ShapeDtypeStruct((n_pad, d), F32),
        compiler_params=_cparams(("arbitrary",)),
        name="moe_ffn",
    )(tile_expert, n_used, xs, w_gate_up, w_down)


def _moe_combine_kernel(pos_ref, x_ref, gate_ref, r_ref, y_hbm, o_ref, ybuf, sem):
    tm = x_ref.shape[0]
    _issue_row_copies(2 * tm, lambda r: pltpu.make_async_copy(
        y_hbm.at[pl.ds(pos_ref[0, 0, r], 1)], ybuf.at[pl.ds(r, 1)], sem))
    pltpu.make_async_copy(y_hbm.at[pl.ds(0, 2 * tm)], ybuf, sem).wait()
    route = r_ref[...]
    y = route[:, 2:3] * ybuf[0:tm, :] + route[:, 3:4] * ybuf[tm:2 * tm, :]
    o_ref[...] = x_ref[...] + gate_ref[...] * y


def moe_combine(pos_tiles, xa, gate, route, y_sorted, tm, seq, nb):
    ta, d = xa.shape
    return pl.pallas_call(
        _moe_combine_kernel,
        grid=(ta // tm,),
        in_specs=[
            pl.BlockSpec((1, 1, 2 * tm), lambda i: (i, 0, 0), memory_space=pltpu.SMEM),
            pl.BlockSpec((tm, d), lambda i: (i, 0)),
            pl.BlockSpec((None, 1, d), _mod_row_map(tm, seq, nb)),
            pl.BlockSpec((tm, LANES), lambda i: (i, 0)),
            pl.BlockSpec(memory_space=pl.ANY),
        ],
        out_specs=pl.BlockSpec((tm, d), lambda i: (i, 0)),
        out_shape=jax.ShapeDtypeStruct((ta, d), F32),
        scratch_shapes=[pltpu.VMEM((2 * tm, d), F32), pltpu.SemaphoreType.DMA(())],
        compiler_params=_cparams(("arbitrary",)),
        name="moe_combine",
    )(pos_tiles, xa, gate, route, y_sorted)


def moe_slots(route, counts, tm_ffn, tm_tok):
    ta = route.shape[0]
    ids = route[:, 0:TOP_K].astype(jnp.int32)
    rank = route[:, 2 * TOP_K:3 * TOP_K].astype(jnp.int32)
    counts = counts[0, N_GROUPS:N_GROUPS + N_EXPERTS].astype(jnp.int32)
    padded = ((counts + tm_ffn - 1) // tm_ffn) * tm_ffn
    ends = jnp.cumsum(padded)
    starts = ends - padded
    experts = jnp.arange(N_EXPERTS, dtype=jnp.int32)
    pos = jnp.sum(jnp.where(ids[..., None] == experts, starts, 0), axis=-1) + rank
    n_tiles = (TOP_K * ta + N_EXPERTS * (tm_ffn - 1)) // tm_ffn
    tile_start = jnp.arange(n_tiles, dtype=jnp.int32) * tm_ffn
    tile_expert = jnp.sum((tile_start[:, None] >= ends[None, :]).astype(jnp.int32), axis=1)
    tile_expert = jnp.minimum(tile_expert, N_EXPERTS - 1)
    n_used = (ends[-1] // tm_ffn).astype(jnp.int32).reshape(1)
    pos_tiles = pos.reshape(ta // tm_tok, tm_tok, TOP_K).transpose(0, 2, 1).reshape(ta // tm_tok, 1, TOP_K * tm_tok)
    return tile_expert, n_used, n_tiles * tm_ffn, pos_tiles


def _seq_flags(t_lat, seq, tc, cseq, tm):
    starts = np.arange(0, t_lat + tc, tm)
    first = np.where(starts < t_lat, starts % seq == 0, (starts - t_lat) % cseq == 0)
    ends = starts + tm
    last = np.where(starts < t_lat, ends % seq == 0, (ends - t_lat) % cseq == 0)
    return jnp.asarray(first.astype(np.int32)), jnp.asarray(last.astype(np.int32))


def kernel(x, c, ctx, c_ctx, w_ada, b_ada, norm_mix, norm_ffn, ev_w_in, ev_q_gain, ev_k_gain, ev_decay_f,
           ev_decay_b, ev_w_out, od_w_in, od_conv, od_a_log_f, od_a_log_b, od_dt_bias_f, od_dt_bias_b,
           od_out_gain, od_w_out, moe_w_group, moe_b_group, moe_w_expert, moe_b_expert, moe_w_gate_up,
           moe_w_down, final_norm_gain):
    nb, seq, d = x.shape
    cseq = ctx.shape[1]
    depth = w_ada.shape[0]
    t_lat = nb * seq
    tc = nb * cseq
    assert nb + 1 <= 8 and seq % cseq == 0 and cseq % RET_CHUNK == 0 and seq % GRID_W == 0

    tm = 512 if tc % 512 == 0 else cseq
    tm_prep = min(256, cseq)
    tq = min(256, cseq)
    tk = min(256, cseq)
    tl = 2 * DN_CHUNK
    tm_ffn = 512
    tm_comb = tm

    xa = jnp.concatenate([x.reshape(t_lat, d), ctx.reshape(tc, d)], axis=0)
    c8 = jnp.zeros((8, d), F32).at[:nb].set(c).at[nb].set(c_ctx)
    mod = adaln(c8, w_ada, b_ada)

    tabs = rope_tables(seq, tm)
    first_flags, last_flags = _seq_flags(t_lat, seq, tc, cseq, tm_prep)
    ret_zero = jnp.zeros((nb, RET_HEADS, RET_DK, RET_DV), F32)
    dn_zero = jnp.zeros((nb, DN_HEADS, DN_DK, DN_DV), F32)

    xs = None
    for layer in range(depth):
        m = mod[layer].reshape(8, 6, 1, d)
        sh1, sc1, g1, sh2, sc2, g2 = (m[:, j] for j in range(6))
        i = layer // 2
        if layer % 2 == 0:
            w_in = ev_w_in[i].astype(BF16)
            rq, rk, p, aq, ak, av = inproj_even(xa, norm_mix[layer], sh1, sc1, w_in, tabs, ev_q_gain[i],
                                                ev_k_gain[i], tm, seq, nb)
            dec = jnp.stack([ev_decay_f[i], ev_decay_b[i]]).astype(F32)
            oc, scf, scb = retention(dec, rq, rk, p, ret_zero, ret_zero, nb, cseq, t_lat // cseq)
            ol, _, _ = retention(dec, rq, rk, p, scf, scb, nb, seq, 0)
            kcat = jnp.concatenate([ak[:, :t_lat].reshape(ATT_KV_HEADS, nb, seq, ATT_HD),
                                    ak[:, t_lat:].reshape(ATT_KV_HEADS, nb, cseq, ATT_HD)], axis=2)
            vcat = jnp.concatenate([av[:, :t_lat].reshape(ATT_KV_HEADS, nb, seq, ATT_HD),
                                    av[:, t_lat:].reshape(ATT_KV_HEADS, nb, cseq, ATT_HD)], axis=2)
            lk = seq + cseq
            vtcat = jnp.concatenate([vcat.transpose(0, 1, 3, 2),
                                     jnp.ones((ATT_KV_HEADS, nb, ATT_VT_ROWS - ATT_HD, lk), BF16)], axis=2)
            vtcat = vtcat.reshape(ATT_KV_HEADS, nb, ATT_VT_ROWS, lk // tk, tk).transpose(0, 1, 3, 2, 4)
            att_l = attention(aq, kcat, vtcat, seq, 0, 0, tq, tk)
            att_c = attention(aq, kcat, vtcat, cseq, t_lat // tq, seq // tk, tq, tk)
            w_out = ev_w_out[i].astype(BF16)
            k1 = RET_HEADS * RET_DV
            xa = outproj_even(ol, oc, att_l, att_c, w_out[:k1], w_out[k1:], xa, g1, tm, seq, nb)
        else:
            w_in = jnp.pad(od_w_in[i], ((0, 0), (0, ODD_IN_PAD - ODD_IN))).astype(BF16)
            p = norm_mod_matmul(xa, norm_mix[layer], sh1, sc1, w_in, tm, seq, nb)
            zpad = jnp.zeros((LANES - 2 * DN_HEADS,), F32)
            arow = jnp.concatenate([od_a_log_f[i], od_a_log_b[i], zpad]).reshape(1, LANES).astype(F32)
            brow = jnp.concatenate([od_dt_bias_f[i], od_dt_bias_b[i], zpad]).reshape(1, LANES).astype(F32)
            q, k, v, gb = prep_odd(p, od_conv[i].astype(F32), arow, brow, first_flags, last_flags, tm_prep)
            ta = t_lat + tc
            gbt = gb.reshape(ta // DN_CHUNK, DN_CHUNK, LANES)[:, :, :4 * DN_HEADS].transpose(0, 2, 1)
            oc_f, oc_b, sc_f, sc_b = deltanet_bidir(q, k, v, gb, gbt, dn_zero, dn_zero, nb, cseq, t_lat, tl)
            ol_f, ol_b, _, _ = deltanet_bidir(q, k, v, gb, gbt, sc_f, sc_b, nb, seq, 0, tl)
            xa = outproj_odd(ol_f, oc_f, ol_b, oc_b, p, od_out_gain[i], od_w_out[i].astype(BF16), xa, g1, tm, seq,
                             nb)

        w_router = jnp.pad(jnp.concatenate([moe_w_group[layer], moe_w_expert[layer]], axis=1),
                           ((0, 0), (0, LANES - N_GROUPS - N_EXPERTS))).astype(BF16)
        b_router = jnp.pad(jnp.concatenate([moe_b_group[layer], moe_b_expert[layer]]),
                           (0, LANES - N_GROUPS - N_EXPERTS)).reshape(1, LANES).astype(F32)
        f, route, counts = moe_router(xa, norm_ffn[layer], sh2, sc2, w_router, b_router, tm, seq, nb)
        tile_expert, n_used, n_pad, pos_tiles = moe_slots(route, counts, tm_ffn, tm_comb)
        xs = moe_scatter(pos_tiles, f, jnp.zeros((n_pad, d), F32) if xs is None else xs, tm_comb)
        y_sorted = moe_ffn(tile_expert, n_used, xs, moe_w_gate_up, moe_w_down, layer, tm_ffn)
        xa = moe_combine(pos_tiles, xa, g2, route, y_sorted, tm_comb, seq, nb)

    out = final_norm(xa, final_norm_gain, t_lat, tm)
    return out.reshape(nb, seq, d)
```

```python
import functools
import math

import numpy as np
import jax
import jax.numpy as jnp
from jax import lax
from jax.experimental import pallas as pl
from jax.experimental.pallas import tpu as pltpu

F32 = jnp.float32
BF16 = jnp.bfloat16
HIGHEST = lax.Precision.HIGHEST

EPS = 1e-6
GRID_W = 64
ROPE_BASE = 10000.0
RET_HEADS, RET_DK, RET_DV, RET_CHUNK = 8, 64, 128, 128
ATT_HEADS, ATT_KV_HEADS, ATT_HD = 8, 2, 64
DN_HEADS, DN_DK, DN_DV, DN_CHUNK, DN_CONV = 8, 128, 128, 64, 3
N_GROUPS, EXPERTS_PER_GROUP, TOP_K = 4, 8, 2
N_EXPERTS = N_GROUPS * EXPERTS_PER_GROUP

EVEN_IN = 2 * RET_HEADS * RET_DK + 2 * RET_HEADS * RET_DV + (ATT_HEADS + 2 * ATT_KV_HEADS) * ATT_HD
EVEN_ATT_COL = 2 * RET_HEADS * RET_DK + 2 * RET_HEADS * RET_DV
EVEN_ATT_W = (ATT_HEADS + 2 * ATT_KV_HEADS) * ATT_HD
DN_QKV = 2 * DN_HEADS * DN_DK + DN_HEADS * DN_DV
ODD_IN = DN_QKV + DN_HEADS * DN_DV + 4 * DN_HEADS
ODD_IN_PAD = ((ODD_IN + 127) // 128) * 128

LANES = 128
VMEM_LIMIT = 56 * 1024 * 1024

NT_DIMS = (((1,), (1,)), ((), ()))
TN_DIMS = (((0,), (0,)), ((), ()))


def _cparams(sem):
    return pltpu.CompilerParams(dimension_semantics=sem, vmem_limit_bytes=VMEM_LIMIT)


def _silu(x):
    return x / (1.0 + jnp.exp(-x))


def _dot(a, b):
    return jnp.dot(a, b, preferred_element_type=F32)


def _adaln_kernel(c_ref, w_ref, b_ref, o_ref):
    s = _silu(c_ref[...])
    o_ref[...] = _dot(s.astype(BF16), w_ref[...].astype(BF16)) + b_ref[...]


def adaln(c8, w_ada, b_ada):
    depth, d, n6 = w_ada.shape
    tn = min(n6, 1536)
    return pl.pallas_call(
        _adaln_kernel,
        grid=(depth, n6 // tn),
        in_specs=[
            pl.BlockSpec((8, d), lambda l, j: (0, 0)),
            pl.BlockSpec((None, d, tn), lambda l, j: (l, 0, j)),
            pl.BlockSpec((None, 1, tn), lambda l, j: (l, 0, j)),
        ],
        out_specs=pl.BlockSpec((None, 8, tn), lambda l, j: (l, 0, j)),
        out_shape=jax.ShapeDtypeStruct((depth, 8, n6), F32),
        compiler_params=_cparams(("arbitrary", "arbitrary")),
        name="adaln",
    )(c8, w_ada, b_ada.reshape(depth, 1, n6))


def _norm_mod(x, gain, shift, scale):
    ms = jnp.mean(x * x, axis=-1, keepdims=True)
    h = x * lax.rsqrt(ms + EPS) * gain
    return h * (1.0 + scale) + shift


def _nmm_kernel(x_ref, g_ref, sh_ref, sc_ref, w_ref, o_ref, *, nchunk):
    hb = _norm_mod(x_ref[...], g_ref[...], sh_ref[...], sc_ref[...]).astype(BF16)
    n = o_ref.shape[-1]
    for n0 in range(0, n, nchunk):
        n1 = min(n0 + nchunk, n)
        o_ref[:, n0:n1] = _dot(hb, w_ref[:, n0:n1]).astype(o_ref.dtype)


def _mod_row_map(tm, seq, n_lat_batches):
    return lambda i: (jnp.minimum((i * tm) // seq, n_lat_batches), 0, 0)


def norm_mod_matmul(xa, gain, shift, scale, w, tm, seq, nb):
    ta, d = xa.shape
    n = w.shape[1]
    nchunk = 512
    mrow = _mod_row_map(tm, seq, nb)
    return pl.pallas_call(
        functools.partial(_nmm_kernel, nchunk=nchunk),
        grid=(ta // tm,),
        in_specs=[
            pl.BlockSpec((tm, d), lambda i: (i, 0)),
            pl.BlockSpec((1, d), lambda i: (0, 0)),
            pl.BlockSpec((None, 1, d), mrow),
            pl.BlockSpec((None, 1, d), mrow),
            pl.BlockSpec((d, n), lambda i: (0, 0)),
        ],
        out_specs=pl.BlockSpec((tm, n), lambda i: (i, 0)),
        out_shape=jax.ShapeDtypeStruct((ta, n), BF16),
        compiler_params=_cparams(("arbitrary",)),
        name="norm_mod_matmul",
    )(xa, gain.reshape(1, d), shift, scale, w)


def _final_norm_kernel(x_ref, g_ref, o_ref):
    x = x_ref[...]
    ms = jnp.mean(x * x, axis=-1, keepdims=True)
    o_ref[...] = x * lax.rsqrt(ms + EPS) * g_ref[...]


def final_norm(xa, gain, t_rows, tm):
    d = xa.shape[1]
    return pl.pallas_call(
        _final_norm_kernel,
        grid=(t_rows // tm,),
        in_specs=[pl.BlockSpec((tm, d), lambda i: (i, 0)), pl.BlockSpec((1, d), lambda i: (0, 0))],
        out_specs=pl.BlockSpec((tm, d), lambda i: (i, 0)),
        out_shape=jax.ShapeDtypeStruct((t_rows, d), F32),
        compiler_params=_cparams(("arbitrary",)),
        name="final_norm",
    )(xa, gain.reshape(1, d))


def _inproj_even_kernel(x_ref, g_ref, sh_ref, sc_ref, w_ref, cos_ref, s1_ref, s2_ref, qg_ref, kg_ref, bd_ref,
                        rq_ref, rk_ref, vg_ref, aq_ref, ak_ref, av_ref):
    hb = _norm_mod(x_ref[...], g_ref[...], sh_ref[...], sc_ref[...]).astype(BF16)
    cos = cos_ref[...]
    s1 = s1_ref[...]
    s2 = s2_ref[...]
    bd = bd_ref[...]
    half = ATT_HD

    def proj(c0, width=LANES):
        return _dot(hb, w_ref[:, c0:c0 + width])

    def rope(x):
        return x * cos + pltpu.roll(x, LANES - 16, 1) * s1 + pltpu.roll(x, 16, 1) * s2

    def head_norm(x, gain):
        sq = x * x
        hi = sq.astype(BF16)
        lo = (sq - hi.astype(F32)).astype(BF16)
        ms = _dot(hi, bd) + _dot(lo, bd)
        return x * lax.rsqrt(ms + EPS) * gain

    wide = 2 * LANES
    qw = RET_HEADS * RET_DK
    for c0 in range(0, qw, wide):
        yq = proj(c0, wide)
        yk = proj(qw + c0, wide)
        for u in range(2):
            cs = slice(c0 + u * LANES, c0 + (u + 1) * LANES)
            us = slice(u * LANES, (u + 1) * LANES)
            rq_ref[:, cs] = rope(yq[:, us]).astype(BF16)
            rk_ref[:, cs] = (rope(yk[:, us]) * RET_DK ** -0.5).astype(BF16)
    vgw = 2 * RET_HEADS * RET_DV
    for c0 in range(0, vgw, 512):
        vg_ref[:, c0:c0 + 512] = proj(2 * qw + c0, 512).astype(BF16)

    qg = qg_ref[...]
    kg = kg_ref[...]
    a0 = EVEN_ATT_COL
    for c0 in range(0, ATT_HEADS * ATT_HD, wide):
        ya = proj(a0 + c0, wide)
        for u in range(2):
            y = rope(head_norm(ya[:, u * LANES:(u + 1) * LANES], qg)) * (ATT_HD ** -0.5 * math.log2(math.e))
            y = y.astype(BF16)
            hd0 = (c0 + u * LANES) // ATT_HD
            aq_ref[hd0] = y[:, :half]
            aq_ref[hd0 + 1] = y[:, half:]
    ykv = proj(a0 + ATT_HEADS * ATT_HD, wide)
    y = rope(head_norm(ykv[:, :LANES], kg)).astype(BF16)
    ak_ref[0] = y[:, :half]
    ak_ref[1] = y[:, half:]
    v = ykv[:, LANES:].astype(BF16)
    av_ref[0] = v[:, :half]
    av_ref[1] = v[:, half:]


def inproj_even(xa, gain, shift, scale, w, tabs, q_gain, k_gain, tm, seq, nb):
    ta, d = xa.shape
    t_lat = nb * seq
    cos_t, s1_t, s2_t = tabs
    n_tab = seq // tm

    def tab_map(i):
        r = i * tm
        return (jnp.where(r < t_lat, (r % seq) // tm, n_tab), 0)

    ii = np.arange(LANES)
    bd = jnp.asarray((ii[:, None] // ATT_HD == ii[None, :] // ATT_HD).astype(np.float32) / ATT_HD).astype(BF16)
    qg = jnp.tile(q_gain.astype(F32), LANES // ATT_HD).reshape(1, LANES)
    kg = jnp.tile(k_gain.astype(F32), LANES // ATT_HD).reshape(1, LANES)
    mrow = _mod_row_map(tm, seq, nb)
    tab_spec = pl.BlockSpec((tm, LANES), tab_map)
    one = lambda i: (0, 0)
    qw = RET_HEADS * RET_DK
    vgw = 2 * RET_HEADS * RET_DV
    return pl.pallas_call(
        _inproj_even_kernel,
        grid=(ta // tm,),
        in_specs=[
            pl.BlockSpec((tm, d), lambda i: (i, 0)),
            pl.BlockSpec((1, d), one),
            pl.BlockSpec((None, 1, d), mrow),
            pl.BlockSpec((None, 1, d), mrow),
            pl.BlockSpec((d, EVEN_IN), one),
            tab_spec, tab_spec, tab_spec,
            pl.BlockSpec((1, LANES), one), pl.BlockSpec((1, LANES), one),
            pl.BlockSpec((LANES, LANES), one),
        ],
        out_specs=[
            pl.BlockSpec((tm, qw), lambda i: (i, 0)),
            pl.BlockSpec((tm, qw), lambda i: (i, 0)),
            pl.BlockSpec((tm, vgw), lambda i: (i, 0)),
            pl.BlockSpec((ATT_HEADS, tm, ATT_HD), lambda i: (0, i, 0)),
            pl.BlockSpec((ATT_KV_HEADS, tm, ATT_HD), lambda i: (0, i, 0)),
            pl.BlockSpec((ATT_KV_HEADS, tm, ATT_HD), lambda i: (0, i, 0)),
        ],
        out_shape=[
            jax.ShapeDtypeStruct((ta, qw), BF16),
            jax.ShapeDtypeStruct((ta, qw), BF16),
            jax.ShapeDtypeStruct((ta, vgw), BF16),
            jax.ShapeDtypeStruct((ATT_HEADS, ta, ATT_HD), BF16),
            jax.ShapeDtypeStruct((ATT_KV_HEADS, ta, ATT_HD), BF16),
            jax.ShapeDtypeStruct((ATT_KV_HEADS, ta, ATT_HD), BF16),
        ],
        compiler_params=_cparams(("arbitrary",)),
        name="inproj_even",
    )(xa, gain.reshape(1, d), shift, scale, w, cos_t, s1_t, s2_t, qg, kg, bd)


def rope_tables(seq, tm):
    nf = ATT_HD // 4
    t = jnp.arange(seq)
    rows = (t // GRID_W).astype(F32)
    cols = (t % GRID_W).astype(F32)
    inv = ROPE_BASE ** (-jnp.arange(nf, dtype=F32) / nf)
    lane = np.arange(LANES)
    axis = (lane % ATT_HD) // (ATT_HD // 2)
    f = lane % nf
    upper = ((lane % (ATT_HD // 2)) >= nf)
    pos = jnp.where(jnp.asarray(axis)[None, :] == 0, rows[:, None], cols[:, None])
    ang = pos * inv[jnp.asarray(f)][None, :]
    cos = jnp.cos(ang)
    sin = jnp.sin(ang)
    s1 = jnp.where(jnp.asarray(upper)[None, :], 0.0, -sin)
    s2 = jnp.where(jnp.asarray(upper)[None, :], sin, 0.0)
    pad1 = jnp.ones((tm, LANES), F32)
    pad0 = jnp.zeros((tm, LANES), F32)
    return (jnp.concatenate([cos, pad1]), jnp.concatenate([s1, pad0]), jnp.concatenate([s2, pad0]))


def _retention_kernel(dec_ref, q_ref, k_ref, v_ref, g_ref, s0f_ref, s0b_ref,
                      o_ref, sff_ref, sfb_ref, st_ref, *, n_chunks, unroll):
    hp = pl.program_id(1)
    C = RET_CHUNK
    dk, dv = RET_DK, RET_DV
    pos = lax.broadcasted_iota(jnp.int32, (C, dk), 0).astype(F32)
    ii = lax.broadcasted_iota(jnp.int32, (C, C), 0)
    jj = lax.broadcasted_iota(jnp.int32, (C, C), 1)
    dpos = (ii - jj).astype(F32)
    heads = range(2)
    qs = [slice(hh * dk, (hh + 1) * dk) for hh in heads]
    vs = [slice(hh * dv, (hh + 1) * dv) for hh in heads]
    w_out, w_in, gcf, gcb, mask = [], [], [], [], []
    for hh in heads:
        h = 2 * hp + hh
        df = dec_ref[0, h]
        db = dec_ref[1, h]
        lf = -jnp.exp(jnp.full((C, C), df, F32))
        lb = -jnp.exp(jnp.full((C, C), db, F32))
        lfk = -jnp.exp(jnp.full((C, dk), df, F32))
        lbk = -jnp.exp(jnp.full((C, dk), db, F32))
        w_out.append(jnp.concatenate([jnp.exp(lfk * (C - 1.0 - pos)), jnp.exp(lbk * pos)], axis=1))
        w_in.append(jnp.concatenate([jnp.exp(lfk * (pos + 1.0)), jnp.exp(lbk * (C - pos))], axis=1))
        gcf.append(jnp.exp(-jnp.exp(jnp.full((dk, dv), df, F32)) * C))
        gcb.append(jnp.exp(-jnp.exp(jnp.full((dk, dv), db, F32)) * C))
        mask.append(jnp.where(dpos > 0, jnp.exp(lf * jnp.maximum(dpos, 0.0)),
                              jnp.where(dpos < 0, jnp.exp(lb * jnp.maximum(-dpos, 0.0)), 2.0)))

    def rows(n):
        return pl.ds(pl.multiple_of(n * C, C), C)

    items = [(u, hh) for u in range(unroll) for hh in heads]

    def sums_body(i, carry):
        kk = {}
        for u, hh in items:
            k = k_ref[rows(i * unroll + u), qs[hh]].astype(F32)
            kk[(u, hh)] = (jnp.concatenate([k, k], axis=1) * w_out[hh]).astype(BF16)
        kv = {(u, hh): lax.dot_general(kk[(u, hh)], v_ref[rows(i * unroll + u), vs[hh]], TN_DIMS,
                                       preferred_element_type=F32) for u, hh in items}
        for u, hh in items:
            st_ref[hh, i * unroll + u] = kv[(u, hh)]
        return carry

    lax.fori_loop(0, n_chunks // unroll, sums_body, 0)

    def scan_body(n, carry):
        n_rev = n_chunks - 1 - n
        out = []
        for hh in heads:
            sf, sb = carry[2 * hh], carry[2 * hh + 1]
            kvf = st_ref[hh, n, 0:dk, :]
            kvb = st_ref[hh, n_rev, dk:2 * dk, :]
            st_ref[hh, n, 0:dk, :] = sf
            st_ref[hh, n_rev, dk:2 * dk, :] = sb
            out += [gcf[hh] * sf + kvf, gcb[hh] * sb + kvb]
        return tuple(out)

    init = tuple(x for hh in heads for x in (s0f_ref[hh], s0b_ref[hh]))
    fin = lax.fori_loop(0, n_chunks, scan_body, init)
    for hh in heads:
        sff_ref[hh] = fin[2 * hh]
        sfb_ref[hh] = fin[2 * hh + 1]

    def out_body(i, carry):
        ns = [i * unroll + u for u in range(unroll)]
        qb = {(u, hh): q_ref[rows(ns[u]), qs[hh]] for u, hh in items}
        sc = {(u, hh): lax.dot_general(qb[(u, hh)], k_ref[rows(ns[u]), qs[hh]], NT_DIMS,
                                       preferred_element_type=F32) for u, hh in items}
        qw = {}
        for it in items:
            q = qb[it].astype(F32)
            qw[it] = (jnp.concatenate([q, q], axis=1) * w_in[it[1]]).astype(BF16)
        o1 = {(u, hh): _dot((sc[(u, hh)] * mask[hh]).astype(BF16), v_ref[rows(ns[u]), vs[hh]]) for u, hh in items}
        o2 = {(u, hh): _dot(qw[(u, hh)], st_ref[hh, ns[u]].astype(BF16)) for u, hh in items}
        for it in items:
            u, hh = it
            n = ns[u]
            o = o1[it] + o2[it]
            o = o * lax.rsqrt(jnp.mean(o * o, axis=-1, keepdims=True) + EPS)
            gate = g_ref[rows(n), vs[hh]].astype(F32)
            o_ref[rows(n), vs[hh]] = (_silu(gate) * o).astype(o_ref.dtype)
        return carry

    lax.fori_loop(0, n_chunks // unroll, out_body, 0)


def retention(dec, rq, rk, p, s0f, s0b, nb, seq, row_off_blocks):
    n_chunks = seq // RET_CHUNK
    hp_n = RET_HEADS // 2
    vcol = 0
    gcol = vcol + RET_HEADS * RET_DV // (2 * RET_DV)
    ta = rq.shape[0]
    st_spec = pl.BlockSpec((None, 2, RET_DK, RET_DV), lambda b, hp, *_: (b, hp, 0, 0))
    grid_spec = pltpu.PrefetchScalarGridSpec(
        num_scalar_prefetch=1,
        grid=(nb, hp_n),
        in_specs=[
            pl.BlockSpec((seq, 2 * RET_DK), lambda b, hp, *_: (row_off_blocks + b, hp)),
            pl.BlockSpec((seq, 2 * RET_DK), lambda b, hp, *_: (row_off_blocks + b, hp)),
            pl.BlockSpec((seq, 2 * RET_DV), lambda b, hp, *_: (row_off_blocks + b, vcol + hp)),
            pl.BlockSpec((seq, 2 * RET_DV), lambda b, hp, *_: (row_off_blocks + b, gcol + hp)),
            st_spec, st_spec,
        ],
        out_specs=[
            pl.BlockSpec((seq, 2 * RET_DV), lambda b, hp, *_: (b, hp)),
            st_spec, st_spec,
        ],
        scratch_shapes=[pltpu.VMEM((2, n_chunks, 2 * RET_DK, RET_DV), F32)],
    )
    st_shape = jax.ShapeDtypeStruct((nb, RET_HEADS, RET_DK, RET_DV), F32)
    return pl.pallas_call(
        functools.partial(_retention_kernel, n_chunks=n_chunks, unroll=math.gcd(n_chunks, 4)),
        grid_spec=grid_spec,
        out_shape=[jax.ShapeDtypeStruct((nb * seq, RET_HEADS * RET_DV), BF16), st_shape, st_shape],
        compiler_params=_cparams(("arbitrary", "arbitrary")),
        name="retention",
    )(dec, rq, rk, p, p, s0f, s0b)


ATT_VT_ROWS = ATT_HD + 16


ATT_PAIRS_PER_TRIP = 4


def _attn_kernel(q_ref, k_ref, vt_ref, o_ref, *s_refs, tk, c_start, c_end, rep):
    tq = q_ref.shape[1]
    sets = (s_refs[:rep], s_refs[rep:])
    last = c_end - 1

    def scores(bufs, j):
        j = jnp.minimum(j, last)
        c0 = pl.multiple_of(j * tk, tk)
        k = k_ref[pl.ds(c0, tk), :]
        mxs = []
        for r in range(rep):
            s = lax.dot_general(k, q_ref[r], NT_DIMS, preferred_element_type=F32)
            bufs[r][...] = s
            mxs.append(jnp.max(s, axis=0, keepdims=True))
        return tuple(mxs)

    def softmax_pv(bufs, j, mxs, ms, accs):
        vt = vt_ref[j]
        new_m, new_acc = [], []
        for r in range(rep):
            m_new = jnp.maximum(ms[r], mxs[r])
            a = jnp.exp2(ms[r] - m_new)
            p = jnp.exp2(bufs[r][...] - m_new).astype(BF16)
            new_acc.append(a * accs[r] + _dot(vt, p))
            new_m.append(m_new)
        return tuple(new_m), tuple(new_acc)

    def pair(j, mx0, ms, accs):
        mx1 = scores(sets[1], j + 1)
        ms, accs = softmax_pv(sets[0], j, mx0, ms, accs)
        mx0 = scores(sets[0], j + 2)
        ms, accs = softmax_pv(sets[1], j + 1, mx1, ms, accs)
        return mx0, ms, accs

    def trip(t, carry):
        for u in range(ATT_PAIRS_PER_TRIP):
            carry = pair(c_start + 2 * (ATT_PAIRS_PER_TRIP * t + u), *carry)
        return carry

    n_pairs = (c_end - c_start) // 2
    n_trips = n_pairs // ATT_PAIRS_PER_TRIP
    ms = tuple(jnp.full((1, tq), -1e30, F32) for _ in range(rep))
    accs = tuple(jnp.zeros((ATT_VT_ROWS, tq), F32) for _ in range(rep))
    carry = (scores(sets[0], c_start), ms, accs)
    if n_trips:
        carry = lax.fori_loop(0, n_trips, trip, carry)
    for u in range(n_trips * ATT_PAIRS_PER_TRIP, n_pairs):
        carry = pair(c_start + 2 * u, *carry)
    mx0, ms, accs = carry
    if (c_end - c_start) % 2:
        ms, accs = softmax_pv(sets[0], last, mx0, ms, accs)
    outs = [(acc[:ATT_HD, :] / acc[ATT_HD:ATT_HD + 1, :]).T for acc in accs]
    o_ref[...] = jnp.concatenate(outs, axis=-1).astype(o_ref.dtype)


def attention(aq, kcat, vtcat, seq_q, q_off_blocks, c_start, tq, tk):
    rep = ATT_HEADS // ATT_KV_HEADS
    _, nb, lk, _ = kcat.shape
    nq = seq_q // tq
    n_chunks = lk // tk
    return pl.pallas_call(
        functools.partial(_attn_kernel, tk=tk, c_start=c_start, c_end=n_chunks, rep=rep),
        grid=(nb, ATT_KV_HEADS, nq),
        in_specs=[
            pl.BlockSpec((rep, tq, ATT_HD), lambda b, g, i: (g, q_off_blocks + b * nq + i, 0)),
            pl.BlockSpec((None, None, lk, ATT_HD), lambda b, g, i: (g, b, 0, 0)),
            pl.BlockSpec((None, None, n_chunks, ATT_VT_ROWS, tk), lambda b, g, i: (g, b, 0, 0, 0)),
        ],
        out_specs=pl.BlockSpec((tq, rep * ATT_HD), lambda b, g, i: (b * nq + i, g)),
        out_shape=jax.ShapeDtypeStruct((nb * seq_q, ATT_HEADS * ATT_HD), BF16),
        scratch_shapes=[pltpu.VMEM((tk, tq), F32) for _ in range(2 * rep)],
        compiler_params=_cparams(("arbitrary", "arbitrary", "arbitrary")),
        name="attention",
    )(aq, kcat, vtcat)


def _lat_ctx_specs(tm, width, n_lat_tiles):
    return [pl.BlockSpec((tm, width), lambda i: (jnp.minimum(i, n_lat_tiles - 1), 0)),
            pl.BlockSpec((tm, width), lambda i: (jnp.maximum(i - n_lat_tiles, 0), 0))]


def _outproj_even_kernel(r_lat, r_ctx, a_lat, a_ctx, w1_ref, w2_ref, res_ref, gate_ref, o_ref, *, n_lat_tiles):
    is_lat = pl.program_id(0) < n_lat_tiles
    a1 = jnp.where(is_lat, r_lat[...], r_ctx[...])
    a2 = jnp.where(is_lat, a_lat[...], a_ctx[...])
    y = _dot(a1, w1_ref[...]) + _dot(a2, w2_ref[...])
    o_ref[...] = res_ref[...] + gate_ref[...] * y


def outproj_even(ret_lat, ret_ctx, att_lat, att_ctx, w1, w2, xa, gate, tm, seq, nb):
    ta, d = xa.shape
    k1, k2 = w1.shape[0], w2.shape[0]
    n_lat_tiles = ret_lat.shape[0] // tm
    return pl.pallas_call(
        functools.partial(_outproj_even_kernel, n_lat_tiles=n_lat_tiles),
        grid=(ta // tm,),
        in_specs=_lat_ctx_specs(tm, k1, n_lat_tiles) + _lat_ctx_specs(tm, k2, n_lat_tiles) + [
            pl.BlockSpec((k1, d), lambda i: (0, 0)),
            pl.BlockSpec((k2, d), lambda i: (0, 0)),
            pl.BlockSpec((tm, d), lambda i: (i, 0)),
            pl.BlockSpec((None, 1, d), _mod_row_map(tm, seq, nb)),
        ],
        out_specs=pl.BlockSpec((tm, d), lambda i: (i, 0)),
        out_shape=jax.ShapeDtypeStruct((ta, d), F32),
        compiler_params=_cparams(("arbitrary",)),
        name="outproj_even",
    )(ret_lat, ret_ctx, att_lat, att_ctx, w1, w2, xa, gate)


def _outproj_odd_kernel(f_lat, f_ctx, b_lat, b_ctx, z_ref, og_ref, w_ref, res_ref, gate_ref, o_ref,
                        *, n_lat_tiles):
    is_lat = pl.program_id(0) < n_lat_tiles
    og = og_ref[...]
    parts = []
    for h in range(DN_HEADS):
        cs = slice(h * DN_DV, (h + 1) * DN_DV)
        of = jnp.where(is_lat, f_lat[:, cs], f_ctx[:, cs]).astype(F32)
        ob = jnp.where(is_lat, b_lat[:, cs], b_ctx[:, cs]).astype(F32)
        o = of + ob
        o = o * lax.rsqrt(jnp.mean(o * o, axis=-1, keepdims=True) + EPS) * og
        parts.append((o * _silu(z_ref[:, cs].astype(F32))).astype(BF16))
    a = jnp.concatenate(parts, axis=-1)
    o_ref[...] = res_ref[...] + gate_ref[...] * _dot(a, w_ref[...])


def outproj_odd(of_lat, of_ctx, ob_lat, ob_ctx, p, out_gain, w, xa, gate, tm, seq, nb):
    ta, d = xa.shape
    kdim = DN_HEADS * DN_DV
    n_lat_tiles = of_lat.shape[0] // tm
    return pl.pallas_call(
        functools.partial(_outproj_odd_kernel, n_lat_tiles=n_lat_tiles),
        grid=(ta // tm,),
        in_specs=_lat_ctx_specs(tm, kdim, n_lat_tiles) + _lat_ctx_specs(tm, kdim, n_lat_tiles) + [
            pl.BlockSpec((tm, kdim), lambda i: (i, 0)),
            pl.BlockSpec((1, DN_DV), lambda i: (0, 0)),
            pl.BlockSpec((kdim, d), lambda i: (0, 0)),
            pl.BlockSpec((tm, d), lambda i: (i, 0)),
            pl.BlockSpec((None, 1, d), _mod_row_map(tm, seq, nb)),
        ],
        out_specs=pl.BlockSpec((tm, d), lambda i: (i, 0)),
        out_shape=jax.ShapeDtypeStruct((ta, d), F32),
        compiler_params=_cparams(("arbitrary",)),
        name="outproj_odd",
    )(of_lat, of_ctx, ob_lat, ob_ctx, p, out_gain.reshape(1, DN_DV).astype(F32), w, xa, gate)


def _inproj_odd_kernel(first_ref, last_ref, x_ref, xp_ref, xn_ref, g_ref, sh_ref, sc_ref, w_ref, cw_ref,
                       arow_ref, brow_ref, q_ref, k_ref, v_ref, z_ref, gb_ref, *, ctx_tile0, cseq):
    i = pl.program_id(0)
    tm = x_ref.shape[0]
    gain, shift, scale = g_ref[...], sh_ref[...], sc_ref[...]
    hrows = xp_ref.shape[0]
    hall = _norm_mod(jnp.concatenate([x_ref[...], xp_ref[...], xn_ref[...]], axis=0), gain, shift, scale).astype(BF16)
    hb = hall[:tm]
    keep_prev = 1.0 - first_ref[i].astype(F32)
    keep_next = 1.0 - last_ref[i].astype(F32)
    row = lax.broadcasted_iota(jnp.int32, (tm, LANES), 0)
    is_first = row == 0
    is_last = row == tm - 1
    inner = cseq < tm
    if inner:
        in_ctx = i >= ctx_tile0
        local = row & (cseq - 1)
        zero_dn = jnp.logical_and(in_ctx, local == 0)
        zero_up = jnp.logical_and(in_ctx, local == cseq - 1)
    n_qk = 2 * DN_HEADS * DN_DK // LANES
    n_q = DN_HEADS * DN_DK // LANES
    outs = (q_ref, k_ref, v_ref)
    wide = 2 * LANES
    for c0 in range(0, DN_QKV, wide):
        yall = _dot(hall, w_ref[:, c0:c0 + wide])
        y2 = yall[:tm]
        yh2 = yall[tm:]
        for u in range(2):
            j = c0 // LANES + u
            us = slice(u * LANES, (u + 1) * LANES)
            x = y2[:, us]
            xp = yh2[hrows - 1:hrows, us] * keep_prev
            xn = yh2[hrows:hrows + 1, us] * keep_next
            x_dn = jnp.where(is_first, xp, pltpu.roll(x, 1, 0))
            x_up = jnp.where(is_last, xn, pltpu.roll(x, tm - 1, 0))
            if inner:
                x_dn = jnp.where(zero_dn, 0.0, x_dn)
                x_up = jnp.where(zero_up, 0.0, x_up)
            w = cw_ref[:, j * LANES:(j + 1) * LANES]
            y = _silu(x_dn * w[0:1, :] + x * w[1:2, :] + x_up * w[2:3, :])
            if j < n_qk:
                y = y * lax.rsqrt(jnp.sum(y * y, axis=-1, keepdims=True) + EPS)
                if j < n_q:
                    y = y * DN_DK ** -0.5
            lj = j % n_q
            outs[j // n_q][:, lj * LANES:(lj + 1) * LANES] = y.astype(BF16)
    zw = DN_HEADS * DN_DV
    for c0 in range(0, zw, 512):
        z_ref[:, c0:c0 + 512] = _dot(hb, w_ref[:, DN_QKV + c0:DN_QKV + c0 + 512]).astype(BF16)

    a = _dot(hb, w_ref[:, DN_QKV + zw:DN_QKV + zw + LANES])
    lane = lax.broadcasted_iota(jnp.int32, (tm, LANES), 1)
    zz = a + brow_ref[...]
    softplus = jnp.maximum(zz, 0.0) + jnp.log(1.0 + jnp.exp(-jnp.abs(zz)))
    g = -jnp.exp(arow_ref[...]) * softplus
    beta = 1.0 / (1.0 + jnp.exp(-a))
    gb_ref[...] = jnp.where(lane < 2 * DN_HEADS, g, jnp.where(lane < 4 * DN_HEADS, beta, 0.0))


def inproj_odd(xa, gain, shift, scale, w, conv_w, arow, brow, first_flags, last_flags, tm, seq, nb, cseq):
    ta, d = xa.shape
    halo = 8
    hb = tm // halo
    n_h = ta // halo
    kdim = DN_HEADS * DN_DK
    assert cseq >= tm or (tm % cseq == 0 and cseq & (cseq - 1) == 0)
    mrow = lambda i, *_: _mod_row_map(tm, seq, nb)(i)
    one = lambda i, *_: (0, 0)
    row_blk = lambda i, *_: (i, 0)
    grid_spec = pltpu.PrefetchScalarGridSpec(
        num_scalar_prefetch=2,
        grid=(ta // tm,),
        in_specs=[
            pl.BlockSpec((tm, d), row_blk),
            pl.BlockSpec((halo, d), lambda i, *_: (jnp.maximum(i * hb - 1, 0), 0)),
            pl.BlockSpec((halo, d), lambda i, *_: (jnp.minimum((i + 1) * hb, n_h - 1), 0)),
            pl.BlockSpec((1, d), one),
            pl.BlockSpec((None, 1, d), mrow),
            pl.BlockSpec((None, 1, d), mrow),
            pl.BlockSpec((d, ODD_IN_PAD), one),
            pl.BlockSpec((DN_CONV, DN_QKV), one),
            pl.BlockSpec((1, LANES), one),
            pl.BlockSpec((1, LANES), one),
        ],
        out_specs=[pl.BlockSpec((tm, kdim), row_blk)] * 4 + [pl.BlockSpec((tm, LANES), row_blk)],
    )
    return pl.pallas_call(
        functools.partial(_inproj_odd_kernel, ctx_tile0=nb * seq // tm, cseq=cseq),
        grid_spec=grid_spec,
        out_shape=[jax.ShapeDtypeStruct((ta, kdim), BF16)] * 4 + [jax.ShapeDtypeStruct((ta, LANES), F32)],
        compiler_params=_cparams(("arbitrary",)),
        name="inproj_odd",
    )(first_flags, last_flags, xa, xa, xa, gain.reshape(1, d), shift, scale, w, conv_w, arow, brow)


def _prep_odd_kernel(first_ref, last_ref, x_ref, prev_ref, next_ref, ab_ref, cw_ref, arow_ref, brow_ref,
                     q_ref, k_ref, v_ref, gb_ref):
    i = pl.program_id(0)
    tm = x_ref.shape[0]
    hrows = prev_ref.shape[0]
    keep_prev = 1.0 - first_ref[i].astype(F32)
    keep_next = 1.0 - last_ref[i].astype(F32)
    row = lax.broadcasted_iota(jnp.int32, (tm, LANES), 0)
    is_first = row == 0
    is_last = row == tm - 1
    n_qk = 2 * DN_HEADS * DN_DK // LANES
    n_q = DN_HEADS * DN_DK // LANES
    outs = (q_ref, k_ref, v_ref)
    for j in range(DN_QKV // LANES):
        cs = slice(j * LANES, (j + 1) * LANES)
        x = x_ref[:, cs].astype(F32)
        xp = prev_ref[:, cs].astype(F32)[hrows - 1:hrows, :] * keep_prev
        xn = next_ref[:, cs].astype(F32)[0:1, :] * keep_next
        x_dn = jnp.where(is_first, xp, pltpu.roll(x, 1, 0))
        x_up = jnp.where(is_last, xn, pltpu.roll(x, tm - 1, 0))
        w = cw_ref[:, cs]
        y = _silu(x_dn * w[0:1, :] + x * w[1:2, :] + x_up * w[2:3, :])
        if j < n_qk:
            y = y * lax.rsqrt(jnp.sum(y * y, axis=-1, keepdims=True) + EPS)
            if j < n_q:
                y = y * DN_DK ** -0.5
        lj = j % n_q
        outs[j // n_q][:, lj * LANES:(lj + 1) * LANES] = y.astype(BF16)

    a = ab_ref[...].astype(F32)
    lane = lax.broadcasted_iota(jnp.int32, (tm, LANES), 1)
    z = a + brow_ref[...]
    softplus = jnp.maximum(z, 0.0) + jnp.log(1.0 + jnp.exp(-jnp.abs(z)))
    g = -jnp.exp(arow_ref[...]) * softplus
    beta = 1.0 / (1.0 + jnp.exp(-a))
    gb_ref[...] = jnp.where(lane < 2 * DN_HEADS, g, jnp.where(lane < 4 * DN_HEADS, beta, 0.0))


def prep_odd(p, conv_w, arow, brow, first_flags, last_flags, tm):
    ta = p.shape[0]
    halo = 16
    hb = tm // halo
    n_h = ta // halo
    kdim = DN_HEADS * DN_DK
    grid_spec = pltpu.PrefetchScalarGridSpec(
        num_scalar_prefetch=2,
        grid=(ta // tm,),
        in_specs=[
            pl.BlockSpec((tm, DN_QKV), lambda i, *_: (i, 0)),
            pl.BlockSpec((halo, DN_QKV), lambda i, *_: (jnp.maximum(i * hb - 1, 0), 0)),
            pl.BlockSpec((halo, DN_QKV), lambda i, *_: (jnp.minimum((i + 1) * hb, n_h - 1), 0)),
            pl.BlockSpec((tm, LANES), lambda i, *_: (i, (DN_QKV + DN_HEADS * DN_DV) // LANES)),
            pl.BlockSpec((DN_CONV, DN_QKV), lambda i, *_: (0, 0)),
            pl.BlockSpec((1, LANES), lambda i, *_: (0, 0)),
            pl.BlockSpec((1, LANES), lambda i, *_: (0, 0)),
        ],
        out_specs=[
            pl.BlockSpec((tm, kdim), lambda i, *_: (i, 0)),
            pl.BlockSpec((tm, kdim), lambda i, *_: (i, 0)),
            pl.BlockSpec((tm, kdim), lambda i, *_: (i, 0)),
            pl.BlockSpec((tm, LANES), lambda i, *_: (i, 0)),
        ],
    )
    return pl.pallas_call(
        _prep_odd_kernel,
        grid_spec=grid_spec,
        out_shape=[
            jax.ShapeDtypeStruct((ta, kdim), BF16),
            jax.ShapeDtypeStruct((ta, kdim), BF16),
            jax.ShapeDtypeStruct((ta, kdim), BF16),
            jax.ShapeDtypeStruct((ta, LANES), F32),
        ],
        compiler_params=_cparams(("arbitrary",)),
        name="prep_odd",
    )(first_flags, last_flags, p, p, p, p, conv_w, arow, brow)


def _deltanet_kernel(q_ref, k_ref, v_ref, gb_ref, gbt_ref, s0_ref, o_ref, sf_ref, s_ref,
                     *, reverse, n_chunks, dir_off):
    t = pl.program_id(1)

    @pl.when(t == 0)
    def _():
        s_ref[...] = s0_ref[...]

    C = DN_CHUNK
    ii = lax.broadcasted_iota(jnp.int32, (C, C), 0)
    jj = lax.broadcasted_iota(jnp.int32, (C, C), 1)
    if reverse:
        incl = ii <= jj
        strict = ii < jj
    else:
        incl = ii >= jj
        strict = ii > jj
    tri = jnp.where(incl, 1.0, 0.0).astype(F32)
    if reverse:
        tri_t = jnp.where(ii >= jj, 1.0, 0.0).astype(F32)
    else:
        tri_t = jnp.where(ii <= jj, 1.0, 0.0).astype(F32)
    eye = jnp.where(ii == jj, 1.0, 0.0).astype(F32)
    blk = ii ^ jj

    order = list(range(n_chunks - 1, -1, -1) if reverse else range(n_chunks))
    items = [(c, h) for c in order for h in range(DN_HEADS)]
    gcols, grows, gbs = {}, {}, {}
    for c in order:
        gb_c = gb_ref[c * C:(c + 1) * C, :]
        gbs[c] = gb_c
        gcols[c] = jnp.dot(tri, gb_c, preferred_element_type=F32, precision=HIGHEST)
        grows[c] = jnp.dot(gbt_ref[c], tri_t, preferred_element_type=F32, precision=HIGHEST)

    qb, kb16, decay, kbeta, egc, kd, gl, rhs = {}, {}, {}, {}, {}, {}, {}, {}
    for it in items:
        c, h = it
        gi = dir_off + h
        bi = 2 * DN_HEADS + dir_off + h
        rows = slice(c * C, (c + 1) * C)
        cs = slice(h * DN_DK, (h + 1) * DN_DK)
        gc = gcols[c][:, gi:gi + 1]
        gr = grows[c][gi:gi + 1, :]
        beta = gbs[c][:, bi:bi + 1]
        qb[it] = q_ref[rows, cs]
        kb16[it] = k_ref[rows, cs]
        kf = kb16[it].astype(F32)
        decay[it] = jnp.where(incl, jnp.exp(jnp.where(incl, gc - gr, 0.0)), 0.0)
        kbeta[it] = kf * beta
        egc[it] = jnp.exp(gc)
        glast = gc[0:1, :] if reverse else gc[C - 1:C, :]
        kd[it] = (kf * jnp.exp(glast - gc)).astype(BF16)
        gl[it] = jnp.exp(glast)
        rhs[it] = jnp.concatenate([v_ref[rows, cs].astype(F32) * beta, kbeta[it] * egc[it]], axis=1).astype(BF16)

    kk = {it: lax.dot_general(kbeta[it].astype(BF16), kb16[it], NT_DIMS, preferred_element_type=F32)
          for it in items}
    qk = {it: lax.dot_general(qb[it], kb16[it], NT_DIMS, preferred_element_type=F32) for it in items}
    lm = {it: jnp.where(strict, kk[it] * decay[it], 0.0) for it in items}
    attn = {it: jnp.where(incl, qk[it] * decay[it], 0.0).astype(BF16) for it in items}
    dinv = {it: eye - jnp.where(blk < 2, lm[it], 0.0) for it in items}
    s = 2
    while s < C:
        in_band = jnp.logical_and(blk >= s, blk < 2 * s)
        tmp = {it: _dot(dinv[it].astype(BF16), jnp.where(in_band, lm[it], 0.0).astype(BF16)) for it in items}
        dinv = {it: dinv[it] - _dot(tmp[it].astype(BF16), dinv[it].astype(BF16)) for it in items}
        s *= 2
    uw = {it: _dot(dinv[it].astype(BF16), rhs[it]) for it in items}
    wq = {it: jnp.concatenate([uw[it][:, DN_DV:], qb[it].astype(F32) * egc[it]], axis=0).astype(BF16)
          for it in items}

    states = [s_ref[h] for h in range(DN_HEADS)]
    for c in order:
        its = [(c, h) for h in range(DN_HEADS)]
        r = {it: _dot(wq[it], states[it[1]].astype(BF16)) for it in its}
        v_new = {it: (uw[it][:, :DN_DV] - r[it][:C]).astype(BF16) for it in its}
        o = {it: r[it][C:] + _dot(attn[it], v_new[it]) for it in its}
        for it in its:
            h = it[1]
            states[h] = states[h] * gl[it] + lax.dot_general(kd[it], v_new[it], TN_DIMS,
                                                             preferred_element_type=F32)
        for it in its:
            h = it[1]
            o_ref[c * C:(c + 1) * C, h * DN_DK:(h + 1) * DN_DK] = o[it].astype(o_ref.dtype)
    for h in range(DN_HEADS):
        s_ref[h] = states[h]

    @pl.when(t == pl.num_programs(1) - 1)
    def _():
        sf_ref[...] = s_ref[...]


def deltanet(q, k, v, gb, gbt, s0, nb, seq, row_off, reverse, tl):
    nblk = seq // tl
    n_chunks = tl // DN_CHUNK
    off_b = row_off // tl
    kdim = DN_HEADS * DN_DK

    def rb(b, t):
        tt = nblk - 1 - t if reverse else t
        return off_b + b * nblk + tt

    seq_spec = pl.BlockSpec((tl, kdim), lambda b, t: (rb(b, t), 0))
    st_spec = pl.BlockSpec((None, DN_HEADS, DN_DK, DN_DV), lambda b, t: (b, 0, 0, 0))
    return pl.pallas_call(
        functools.partial(_deltanet_kernel, reverse=reverse, n_chunks=n_chunks,
                          dir_off=DN_HEADS if reverse else 0),
        grid=(nb, nblk),
        in_specs=[
            seq_spec, seq_spec, seq_spec,
            pl.BlockSpec((tl, LANES), lambda b, t: (rb(b, t), 0)),
            pl.BlockSpec((n_chunks, 4 * DN_HEADS, DN_CHUNK), lambda b, t: (rb(b, t), 0, 0)),
            st_spec,
        ],
        out_specs=[
            pl.BlockSpec((tl, kdim), lambda b, t: (b * nblk + (nblk - 1 - t if reverse else t), 0)),
            st_spec,
        ],
        out_shape=[
            jax.ShapeDtypeStruct((nb * seq, kdim), BF16),
            jax.ShapeDtypeStruct((nb, DN_HEADS, DN_DK, DN_DV), F32),
        ],
        scratch_shapes=[pltpu.VMEM((DN_HEADS, DN_DK, DN_DV), F32)],
        compiler_params=_cparams(("arbitrary", "arbitrary")),
        name="deltanet_bwd" if reverse else "deltanet_fwd",
    )(q, k, v, gb, gbt, s0)


def _deltanet_bidir_kernel(qf_ref, kf_ref, vf_ref, gbf_ref, gbtf_ref, qb_ref, kb_ref, vb_ref, gbb_ref, gbtb_ref,
                           s0f_ref, s0b_ref, of_ref, ob_ref, sff_ref, sfb_ref, sf_scr, sb_scr, *, n_chunks):
    t = pl.program_id(1)

    @pl.when(t == 0)
    def _():
        sf_scr[...] = s0f_ref[...]
        sb_scr[...] = s0b_ref[...]

    C = DN_CHUNK
    ii = lax.broadcasted_iota(jnp.int32, (C, C), 0)
    jj = lax.broadcasted_iota(jnp.int32, (C, C), 1)
    lower, upper = ii >= jj, ii <= jj
    eye = jnp.where(ii == jj, 1.0, 0.0).astype(F32)
    blk = ii ^ jj
    dirs = (
        dict(rev=False, incl=lower, strict=ii > jj, q=qf_ref, k=kf_ref, v=vf_ref, gb=gbf_ref, gbt=gbtf_ref,
             o=of_ref, scr=sf_scr, off=0, order=list(range(n_chunks))),
        dict(rev=True, incl=upper, strict=ii < jj, q=qb_ref, k=kb_ref, v=vb_ref, gb=gbb_ref, gbt=gbtb_ref,
             o=ob_ref, scr=sb_scr, off=DN_HEADS, order=list(range(n_chunks - 1, -1, -1))),
    )
    items = [(d, c, h) for d in range(2) for c in dirs[d]["order"] for h in range(DN_HEADS)]

    gcols, grows, gbs = {}, {}, {}
    for d, dr in enumerate(dirs):
        tri = jnp.where(dr["incl"], 1.0, 0.0).astype(F32)
        tri_t = jnp.where(upper if not dr["rev"] else lower, 1.0, 0.0).astype(F32)
        for c in dr["order"]:
            gb_c = dr["gb"][c * C:(c + 1) * C, :]
            gbs[(d, c)] = gb_c
            gcols[(d, c)] = jnp.dot(tri, gb_c, preferred_element_type=F32, precision=HIGHEST)
            grows[(d, c)] = jnp.dot(dr["gbt"][c], tri_t, preferred_element_type=F32, precision=HIGHEST)

    qb, kb16, decay, kbeta, egc, kd, gl, rhs = {}, {}, {}, {}, {}, {}, {}, {}
    for it in items:
        d, c, h = it
        dr = dirs[d]
        gi = dr["off"] + h
        bi = 2 * DN_HEADS + dr["off"] + h
        rows = slice(c * C, (c + 1) * C)
        cs = slice(h * DN_DK, (h + 1) * DN_DK)
        gc = gcols[(d, c)][:, gi:gi + 1]
        gr = grows[(d, c)][gi:gi + 1, :]
        beta = gbs[(d, c)][:, bi:bi + 1]
        qb[it] = dr["q"][rows, cs]
        kb16[it] = dr["k"][rows, cs]
        kf = kb16[it].astype(F32)
        decay[it] = jnp.where(dr["incl"], jnp.exp(jnp.where(dr["incl"], gc - gr, 0.0)), 0.0)
        kbeta[it] = kf * beta
        egc[it] = jnp.exp(gc)
        glast = gc[0:1, :] if dr["rev"] else gc[C - 1:C, :]
        kd[it] = (kf * jnp.exp(glast - gc)).astype(BF16)
        gl[it] = jnp.exp(glast)
        rhs[it] = jnp.concatenate([dr["v"][rows, cs].astype(F32) * beta, kbeta[it] * egc[it]], axis=1).astype(BF16)

    kk = {it: lax.dot_general(kbeta[it].astype(BF16), kb16[it], NT_DIMS, preferred_element_type=F32)
          for it in items}
    qk = {it: lax.dot_general(qb[it], kb16[it], NT_DIMS, preferred_element_type=F32) for it in items}
    lm = {it: jnp.where(dirs[it[0]]["strict"], kk[it] * decay[it], 0.0) for it in items}
    attn = {it: jnp.where(dirs[it[0]]["incl"], qk[it] * decay[it], 0.0).astype(BF16) for it in items}
    dinv = {it: eye - jnp.where(blk < 2, lm[it], 0.0) for it in items}
    s = 2
    while s < C:
        in_band = jnp.logical_and(blk >= s, blk < 2 * s)
        tmp = {it: _dot(dinv[it].astype(BF16), jnp.where(in_band, lm[it], 0.0).astype(BF16)) for it in items}
        dinv = {it: dinv[it] - _dot(tmp[it].astype(BF16), dinv[it].astype(BF16)) for it in items}
        s *= 2
    uw = {it: _dot(dinv[it].astype(BF16), rhs[it]) for it in items}
    wq = {it: jnp.concatenate([uw[it][:, DN_DV:], qb[it].astype(F32) * egc[it]], axis=0).astype(BF16)
          for it in items}

    states = {(d, h): dirs[d]["scr"][h] for d in range(2) for h in range(DN_HEADS)}
    for step in range(n_chunks):
        its = [(d, dirs[d]["order"][step], h) for d in range(2) for h in range(DN_HEADS)]
        r = {it: _dot(wq[it], states[(it[0], it[2])].astype(BF16)) for it in its}
        v_new = {it: (uw[it][:, :DN_DV] - r[it][:C]).astype(BF16) for it in its}
        o = {it: r[it][C:] + _dot(attn[it], v_new[it]) for it in its}
        for it in its:
            key = (it[0], it[2])
            states[key] = states[key] * gl[it] + lax.dot_general(kd[it], v_new[it], TN_DIMS,
                                                                 preferred_element_type=F32)
        for it in its:
            d, c, h = it
            dirs[d]["o"][c * C:(c + 1) * C, h * DN_DK:(h + 1) * DN_DK] = o[it].astype(of_ref.dtype)
    for (d, h), st in states.items():
        dirs[d]["scr"][h] = st

    @pl.when(t == pl.num_programs(1) - 1)
    def _():
        sff_ref[...] = sf_scr[...]
        sfb_ref[...] = sb_scr[...]


def deltanet_bidir(q, k, v, gb, gbt, s0f, s0b, nb, seq, row_off, tl):
    nblk = seq // tl
    n_chunks = tl // DN_CHUNK
    off_b = row_off // tl
    kdim = DN_HEADS * DN_DK

    def fwd_rb(b, t):
        return off_b + b * nblk + t

    def bwd_rb(b, t):
        return off_b + b * nblk + (nblk - 1 - t)

    def seq_specs(rb):
        spec = pl.BlockSpec((tl, kdim), lambda b, t: (rb(b, t), 0))
        return [spec, spec, spec,
                pl.BlockSpec((tl, LANES), lambda b, t: (rb(b, t), 0)),
                pl.BlockSpec((n_chunks, 4 * DN_HEADS, DN_CHUNK), lambda b, t: (rb(b, t), 0, 0))]

    st_spec = pl.BlockSpec((None, DN_HEADS, DN_DK, DN_DV), lambda b, t: (b, 0, 0, 0))
    st_shape = jax.ShapeDtypeStruct((nb, DN_HEADS, DN_DK, DN_DV), F32)
    o_shape = jax.ShapeDtypeStruct((nb * seq, kdim), BF16)
    return pl.pallas_call(
        functools.partial(_deltanet_bidir_kernel, n_chunks=n_chunks),
        grid=(nb, nblk),
        in_specs=seq_specs(fwd_rb) + seq_specs(bwd_rb) + [st_spec, st_spec],
        out_specs=[
            pl.BlockSpec((tl, kdim), lambda b, t: (b * nblk + t, 0)),
            pl.BlockSpec((tl, kdim), lambda b, t: (b * nblk + nblk - 1 - t, 0)),
            st_spec, st_spec,
        ],
        out_shape=[o_shape, o_shape, st_shape, st_shape],
        scratch_shapes=[pltpu.VMEM((DN_HEADS, DN_DK, DN_DV), F32), pltpu.VMEM((DN_HEADS, DN_DK, DN_DV), F32)],
        compiler_params=_cparams(("arbitrary", "arbitrary")),
        name="deltanet_bidir",
    )(q, k, v, gb, gbt, q, k, v, gb, gbt, s0f, s0b)


def _router_kernel(x_ref, g_ref, sh_ref, sc_ref, wr_ref, br_ref, ltri_ref, f_ref, r_ref, cnt_ref, base_ref):
    @pl.when(pl.program_id(0) == 0)
    def _():
        base_ref[...] = jnp.zeros_like(base_ref)

    h = _norm_mod(x_ref[...], g_ref[...], sh_ref[...], sc_ref[...])
    f_ref[...] = h
    logits = _dot(h.astype(BF16), wr_ref[...]) + br_ref[...]
    tm = logits.shape[0]
    lane = lax.broadcasted_iota(jnp.int32, (tm, LANES), 1)
    neg = -1e30
    big = 4 * LANES
    is_g = lane < N_GROUPS
    gl = jnp.where(is_g, logits, neg)
    gm = jnp.max(gl, axis=-1, keepdims=True)
    grp = jnp.min(jnp.where(gl == gm, lane, big), axis=-1, keepdims=True)
    psum = jnp.sum(jnp.where(is_g, jnp.exp(gl - gm), 0.0), axis=-1, keepdims=True)
    p_grp = 1.0 / psum
    e_lane = lane - N_GROUPS
    in_grp = jnp.logical_and(jnp.logical_and(e_lane >= 0, e_lane < N_EXPERTS),
                             (e_lane // EXPERTS_PER_GROUP) == grp)
    el = jnp.where(in_grp, logits, neg)
    m1 = jnp.max(el, axis=-1, keepdims=True)
    i1 = jnp.min(jnp.where(el == m1, lane, big), axis=-1, keepdims=True)
    el2 = jnp.where(lane == i1, neg, el)
    m2 = jnp.max(el2, axis=-1, keepdims=True)
    i2 = jnp.min(jnp.where(el2 == m2, lane, big), axis=-1, keepdims=True)
    e21 = jnp.exp(m2 - m1)
    w1 = p_grp / (1.0 + e21)
    w2 = p_grp * e21 / (1.0 + e21)
    e1 = (i1 - N_GROUPS).astype(F32)
    e2 = (i2 - N_GROUPS).astype(F32)
    oh1 = lane == i1
    oh2 = lane == i2
    oh1f = jnp.where(oh1, 1.0, 0.0)
    oh2f = jnp.where(oh2, 1.0, 0.0)
    ltri = ltri_ref[...]
    before1 = _dot(ltri, oh1f.astype(BF16))
    before2 = _dot(ltri, oh2f.astype(BF16))
    cnt1 = jnp.sum(oh1f, axis=0, keepdims=True)
    cnt2 = jnp.sum(oh2f, axis=0, keepdims=True)
    base = base_ref[0:1, :]
    rank1 = jnp.sum(jnp.where(oh1, base + before1, 0.0), axis=-1, keepdims=True)
    rank2 = jnp.sum(jnp.where(oh2, base + cnt1 + before2, 0.0), axis=-1, keepdims=True)
    total = base + cnt1 + cnt2
    base_ref[...] = jnp.broadcast_to(total, base_ref.shape)
    cnt_ref[...] = jnp.broadcast_to(total, cnt_ref.shape)
    vals = (e1, e2, w1, w2, rank1, rank2)
    out = jnp.zeros((tm, LANES), F32)
    for idx, val in enumerate(vals):
        out = jnp.where(lane == idx, val, out)
    r_ref[...] = out


def moe_router(xa, gain, shift, scale, w_router, b_router, tm, seq, nb):
    ta, d = xa.shape
    mrow = _mod_row_map(tm, seq, nb)
    ii = np.arange(tm)
    ltri = jnp.asarray((ii[:, None] > ii[None, :]).astype(np.float32)).astype(BF16)
    return pl.pallas_call(
        _router_kernel,
        grid=(ta // tm,),
        in_specs=[
            pl.BlockSpec((tm, d), lambda i: (i, 0)),
            pl.BlockSpec((1, d), lambda i: (0, 0)),
            pl.BlockSpec((None, 1, d), mrow),
            pl.BlockSpec((None, 1, d), mrow),
            pl.BlockSpec((d, LANES), lambda i: (0, 0)),
            pl.BlockSpec((1, LANES), lambda i: (0, 0)),
            pl.BlockSpec((tm, tm), lambda i: (0, 0)),
        ],
        out_specs=[pl.BlockSpec((tm, d), lambda i: (i, 0)), pl.BlockSpec((tm, LANES), lambda i: (i, 0)),
                   pl.BlockSpec((8, LANES), lambda i: (0, 0))],
        out_shape=[jax.ShapeDtypeStruct((ta, d), F32), jax.ShapeDtypeStruct((ta, LANES), F32),
                   jax.ShapeDtypeStruct((8, LANES), F32)],
        scratch_shapes=[pltpu.VMEM((8, LANES), F32)],
        compiler_params=_cparams(("arbitrary",)),
        name="moe_router",
    )(xa, gain.reshape(1, d), shift, scale, w_router, b_router, ltri)


ROW_DMA_UNROLL = 8


def _issue_row_copies(n_rows, make_copy):
    def trip(i, carry):
        for u in range(ROW_DMA_UNROLL):
            make_copy(i * ROW_DMA_UNROLL + u).start(priority=u % 2)
        return carry

    lax.fori_loop(0, n_rows // ROW_DMA_UNROLL, trip, 0)


def _moe_scatter_kernel(pos_ref, f_ref, xs_in, xs_out, sem):
    del xs_in
    tm = f_ref.shape[0]
    for k in range(TOP_K):
        _issue_row_copies(tm, lambda r, k=k: pltpu.make_async_copy(
            f_ref.at[pl.ds(r, 1)], xs_out.at[pl.ds(pos_ref[0, 0, k * tm + r], 1)], sem))
    for _ in range(2):
        pltpu.make_async_copy(f_ref, xs_out.at[pl.ds(0, tm)], sem).wait()


def moe_scatter(pos_tiles, f, xs_zero, tm):
    ta, d = f.shape
    return pl.pallas_call(
        _moe_scatter_kernel,
        grid=(ta // tm,),
        in_specs=[
            pl.BlockSpec((1, 1, 2 * tm), lambda i: (i, 0, 0), memory_space=pltpu.SMEM),
            pl.BlockSpec((tm, d), lambda i: (i, 0)),
            pl.BlockSpec(memory_space=pl.ANY),
        ],
        out_specs=pl.BlockSpec(memory_space=pl.ANY),
        out_shape=jax.ShapeDtypeStruct(xs_zero.shape, xs_zero.dtype),
        scratch_shapes=[pltpu.SemaphoreType.DMA(())],
        input_output_aliases={2: 0},
        compiler_params=_cparams(("arbitrary",)),
        name="moe_scatter",
    )(pos_tiles, f, xs_zero)


def _moe_ffn_kernel(te_ref, nu_ref, x_ref, wgu_ref, wd_ref, o_ref, wgu_bf, wd_bf):
    i = pl.program_id(0)
    fdim = wd_bf.shape[0]

    @pl.when(i < nu_ref[0])
    def _():
        prev = te_ref[jnp.maximum(i - 1, 0)]
        changed = jnp.logical_or(i == 0, te_ref[i] != prev)

        @pl.when(changed)
        def _():
            wgu_bf[...] = wgu_ref[...].astype(BF16)
            wd_bf[...] = wd_ref[...].astype(BF16)

        gu = _dot(x_ref[...].astype(BF16), wgu_bf[...])
        hmid = _silu(gu[:, :fdim]) * gu[:, fdim:]
        o_ref[...] = _dot(hmid.astype(BF16), wd_bf[...])

    @pl.when(i >= nu_ref[0])
    def _():
        o_ref[...] = jnp.zeros_like(o_ref)


def moe_ffn(tile_expert, n_used, xs, w_gate_up, w_down, layer, tm):
    n_pad, d = xs.shape
    f2 = w_gate_up.shape[-1]
    fdim = w_down.shape[-2]
    grid_spec = pltpu.PrefetchScalarGridSpec(
        num_scalar_prefetch=2,
        grid=(n_pad // tm,),
        in_specs=[
            pl.BlockSpec((tm, d), lambda i, te, nu: (i, 0)),
            pl.BlockSpec((None, None, d, f2), lambda i, te, nu: (layer, te[i], 0, 0)),
            pl.BlockSpec((None, None, fdim, d), lambda i, te, nu: (layer, te[i], 0, 0)),
        ],
        out_specs=pl.BlockSpec((tm, d), lambda i, te, nu: (i, 0)),
        scratch_shapes=[pltpu.VMEM((d, f2), BF16), pltpu.VMEM((fdim, d), BF16)],
    )
    return pl.pallas_call(
        _moe_ffn_kernel,
        grid_spec=grid_spec,
        out_shape=jax.ShapeDtypeStruct((n_pad, d), F32),
        compiler_params=_cparams(("arbitrary",)),
        name="moe_ffn",
    )(tile_expert, n_used, xs, w_gate_up, w_down)


def _moe_combine_kernel(pos_ref, x_ref, gate_ref, r_ref, y_hbm, o_ref, ybuf, sem):
    tm = x_ref.shape[0]
    _issue_row_copies(2 * tm, lambda r: pltpu.make_async_copy(
        y_hbm.at[pl.ds(pos_ref[0, 0, r], 1)], ybuf.at[pl.ds(r, 1)], sem))
    pltpu.make_async_copy(y_hbm.at[pl.ds(0, 2 * tm)], ybuf, sem).wait()
    route = r_ref[...]
    y = route[:, 2:3] * ybuf[0:tm, :] + route[:, 3:4] * ybuf[tm:2 * tm, :]
    o_ref[...] = x_ref[...] + gate_ref[...] * y


def moe_combine(pos_tiles, xa, gate, route, y_sorted, tm, seq, nb):
    ta, d = xa.shape
    return pl.pallas_call(
        _moe_combine_kernel,
        grid=(ta // tm,),
        in_specs=[
            pl.BlockSpec((1, 1, 2 * tm), lambda i: (i, 0, 0), memory_space=pltpu.SMEM),
            pl.BlockSpec((tm, d), lambda i: (i, 0)),
            pl.BlockSpec((None, 1, d), _mod_row_map(tm, seq, nb)),
            pl.BlockSpec((tm, LANES), lambda i: (i, 0)),
            pl.BlockSpec(memory_space=pl.ANY),
        ],
        out_specs=pl.BlockSpec((tm, d), lambda i: (i, 0)),
        out_shape=jax.ShapeDtypeStruct((ta, d), F32),
        scratch_shapes=[pltpu.VMEM((2 * tm, d), F32), pltpu.SemaphoreType.DMA(())],
        compiler_params=_cparams(("arbitrary",)),
        name="moe_combine",
    )(pos_tiles, xa, gate, route, y_sorted)


def moe_slots(route, counts, tm_ffn, tm_tok):
    ta = route.shape[0]
    ids = route[:, 0:TOP_K].astype(jnp.int32)
    rank = route[:, 2 * TOP_K:3 * TOP_K].astype(jnp.int32)
    counts = counts[0, N_GROUPS:N_GROUPS + N_EXPERTS].astype(jnp.int32)
    padded = ((counts + tm_ffn - 1) // tm_ffn) * tm_ffn
    ends = jnp.cumsum(padded)
    starts = ends - padded
    experts = jnp.arange(N_EXPERTS, dtype=jnp.int32)
    pos = jnp.sum(jnp.where(ids[..., None] == experts, starts, 0), axis=-1) + rank
    n_tiles = (TOP_K * ta + N_EXPERTS * (tm_ffn - 1)) // tm_ffn
    tile_start = jnp.arange(n_tiles, dtype=jnp.int32) * tm_ffn
    tile_expert = jnp.sum((tile_start[:, None] >= ends[None, :]).astype(jnp.int32), axis=1)
    tile_expert = jnp.minimum(tile_expert, N_EXPERTS - 1)
    n_used = (ends[-1] // tm_ffn).astype(jnp.int32).reshape(1)
    pos_tiles = pos.reshape(ta // tm_tok, tm_tok, TOP_K).transpose(0, 2, 1).reshape(ta // tm_tok, 1, TOP_K * tm_tok)
    return tile_expert, n_used, n_tiles * tm_ffn, pos_tiles


def _seq_flags(t_lat, seq, tc, cseq, tm):
    starts = np.arange(0, t_lat + tc, tm)
    first = np.where(starts < t_lat, starts % seq == 0, (starts - t_lat) % cseq == 0)
    ends = starts + tm
    last = np.where(starts < t_lat, ends % seq == 0, (ends - t_lat) % cseq == 0)
    return jnp.asarray(first.astype(np.int32)), jnp.asarray(last.astype(np.int32))


def kernel(x, c, ctx, c_ctx, w_ada, b_ada, norm_mix, norm_ffn, ev_w_in, ev_q_gain, ev_k_gain, ev_decay_f,
           ev_decay_b, ev_w_out, od_w_in, od_conv, od_a_log_f, od_a_log_b, od_dt_bias_f, od_dt_bias_b,
           od_out_gain, od_w_out, moe_w_group, moe_b_group, moe_w_expert, moe_b_expert, moe_w_gate_up,
           moe_w_down, final_norm_gain):
    nb, seq, d = x.shape
    cseq = ctx.shape[1]
    depth = w_ada.shape[0]
    t_lat = nb * seq
    tc = nb * cseq
    assert nb + 1 <= 8 and seq % cseq == 0 and cseq % RET_CHUNK == 0 and seq % GRID_W == 0

    tm = 512 if tc % 512 == 0 else cseq
    tm_prep = min(256, cseq)
    tq = min(256, cseq)
    tk = min(256, cseq)
    tl = 2 * DN_CHUNK
    tm_ffn = 512
    tm_comb = tm

    xa = jnp.concatenate([x.reshape(t_lat, d), ctx.reshape(tc, d)], axis=0)
    c8 = jnp.zeros((8, d), F32).at[:nb].set(c).at[nb].set(c_ctx)
    mod = adaln(c8, w_ada, b_ada)

    tabs = rope_tables(seq, tm)
    first_flags, last_flags = _seq_flags(t_lat, seq, tc, cseq, tm)
    ret_zero = jnp.zeros((nb, RET_HEADS, RET_DK, RET_DV), F32)
    dn_zero = jnp.zeros((nb, DN_HEADS, DN_DK, DN_DV), F32)

    xs = None
    for layer in range(depth):
        m = mod[layer].reshape(8, 6, 1, d)
        sh1, sc1, g1, sh2, sc2, g2 = (m[:, j] for j in range(6))
        i = layer // 2
        if layer % 2 == 0:
            w_in = ev_w_in[i].astype(BF16)
            rq, rk, p, aq, ak, av = inproj_even(xa, norm_mix[layer], sh1, sc1, w_in, tabs, ev_q_gain[i],
                                                ev_k_gain[i], tm, seq, nb)
            dec = jnp.stack([ev_decay_f[i], ev_decay_b[i]]).astype(F32)
            oc, scf, scb = retention(dec, rq, rk, p, ret_zero, ret_zero, nb, cseq, t_lat // cseq)
            ol, _, _ = retention(dec, rq, rk, p, scf, scb, nb, seq, 0)
            kcat = jnp.concatenate([ak[:, :t_lat].reshape(ATT_KV_HEADS, nb, seq, ATT_HD),
                                    ak[:, t_lat:].reshape(ATT_KV_HEADS, nb, cseq, ATT_HD)], axis=2)
            vcat = jnp.concatenate([av[:, :t_lat].reshape(ATT_KV_HEADS, nb, seq, ATT_HD),
                                    av[:, t_lat:].reshape(ATT_KV_HEADS, nb, cseq, ATT_HD)], axis=2)
            lk = seq + cseq
            vtcat = jnp.concatenate([vcat.transpose(0, 1, 3, 2),
                                     jnp.ones((ATT_KV_HEADS, nb, ATT_VT_ROWS - ATT_HD, lk), BF16)], axis=2)
            vtcat = vtcat.reshape(ATT_KV_HEADS, nb, ATT_VT_ROWS, lk // tk, tk).transpose(0, 1, 3, 2, 4)
            att_l = attention(aq, kcat, vtcat, seq, 0, 0, tq, tk)
            att_c = attention(aq, kcat, vtcat, cseq, t_lat // tq, seq // tk, tq, tk)
            w_out = ev_w_out[i].astype(BF16)
            k1 = RET_HEADS * RET_DV
            xa = outproj_even(ol, oc, att_l, att_c, w_out[:k1], w_out[k1:], xa, g1, tm, seq, nb)
        else:
            w_in = jnp.pad(od_w_in[i], ((0, 0), (0, ODD_IN_PAD - ODD_IN))).astype(BF16)
            zpad = jnp.zeros((LANES - 2 * DN_HEADS,), F32)
            arow = jnp.concatenate([od_a_log_f[i], od_a_log_b[i], zpad]).reshape(1, LANES).astype(F32)
            brow = jnp.concatenate([od_dt_bias_f[i], od_dt_bias_b[i], zpad]).reshape(1, LANES).astype(F32)
            q, k, v, p, gb = inproj_odd(xa, norm_mix[layer], sh1, sc1, w_in, od_conv[i].astype(F32), arow, brow,
                                        first_flags, last_flags, tm, seq, nb, cseq)
            ta = t_lat + tc
            gbt = gb.reshape(ta // DN_CHUNK, DN_CHUNK, LANES)[:, :, :4 * DN_HEADS].transpose(0, 2, 1)
            oc_f, oc_b, sc_f, sc_b = deltanet_bidir(q, k, v, gb, gbt, dn_zero, dn_zero, nb, cseq, t_lat, tl)
            ol_f, ol_b, _, _ = deltanet_bidir(q, k, v, gb, gbt, sc_f, sc_b, nb, seq, 0, tl)
            xa = outproj_odd(ol_f, oc_f, ol_b, oc_b, p, od_out_gain[i], od_w_out[i].astype(BF16), xa, g1, tm, seq,
                             nb)

        w_router = jnp.pad(jnp.concatenate([moe_w_group[layer], moe_w_expert[layer]], axis=1),
                           ((0, 0), (0, LANES - N_GROUPS - N_EXPERTS))).astype(BF16)
        b_router = jnp.pad(jnp.concatenate([moe_b_group[layer], moe_b_expert[layer]]),
                           (0, LANES - N_GROUPS - N_EXPERTS)).reshape(1, LANES).astype(F32)
        f, route, counts = moe_router(xa, norm_ffn[layer], sh2, sc2, w_router, b_router, tm, seq, nb)
        tile_expert, n_used, n_pad, pos_tiles = moe_slots(route, counts, tm_ffn, tm_comb)
        xs = moe_scatter(pos_tiles, f, jnp.zeros((n_pad, d), F32) if xs is None else xs, tm_comb)
        y_sorted = moe_ffn(tile_expert, n_used, xs, moe_w_gate_up, moe_w_down, layer, tm_ffn)
        xa = moe_combine(pos_tiles, xa, g2, route, y_sorted, tm_comb, seq, nb)

    out = final_norm(xa, final_norm_gain, t_lat, tm)
    return out.reshape(nb, seq, d)
```

```python
import functools
import math

import numpy as np
import jax
import jax.numpy as jnp
from jax import lax
from jax.experimental import pallas as pl
from jax.experimental.pallas import tpu as pltpu

F32 = jnp.float32
BF16 = jnp.bfloat16
HIGHEST = lax.Precision.HIGHEST

EPS = 1e-6
GRID_W = 64
ROPE_BASE = 10000.0
RET_HEADS, RET_DK, RET_DV, RET_CHUNK = 8, 64, 128, 128
ATT_HEADS, ATT_KV_HEADS, ATT_HD = 8, 2, 64
DN_HEADS, DN_DK, DN_DV, DN_CHUNK, DN_CONV = 8, 128, 128, 64, 3
N_GROUPS, EXPERTS_PER_GROUP, TOP_K = 4, 8, 2
N_EXPERTS = N_GROUPS * EXPERTS_PER_GROUP

EVEN_IN = 2 * RET_HEADS * RET_DK + 2 * RET_HEADS * RET_DV + (ATT_HEADS + 2 * ATT_KV_HEADS) * ATT_HD
EVEN_ATT_COL = 2 * RET_HEADS * RET_DK + 2 * RET_HEADS * RET_DV
DN_QKV = 2 * DN_HEADS * DN_DK + DN_HEADS * DN_DV
ODD_IN = DN_QKV + DN_HEADS * DN_DV + 4 * DN_HEADS
ODD_IN_PAD = ((ODD_IN + 127) // 128) * 128

LANES = 128
VMEM_LIMIT = 56 * 1024 * 1024

NT_DIMS = (((1,), (1,)), ((), ()))
TN_DIMS = (((0,), (0,)), ((), ()))


def _cparams(sem):
    return pltpu.CompilerParams(dimension_semantics=sem, vmem_limit_bytes=VMEM_LIMIT)


def _silu(x):
    return x / (1.0 + jnp.exp(-x))


def _dot(a, b):
    return jnp.dot(a, b, preferred_element_type=F32)


def _adaln_kernel(c_ref, w_ref, b_ref, o_ref):
    s = _silu(c_ref[...])
    o_ref[...] = _dot(s.astype(BF16), w_ref[...].astype(BF16)) + b_ref[...]


def adaln(c8, w_ada, b_ada):
    depth, d, n6 = w_ada.shape
    tn = min(n6, 1536)
    return pl.pallas_call(
        _adaln_kernel,
        grid=(depth, n6 // tn),
        in_specs=[
            pl.BlockSpec((8, d), lambda l, j: (0, 0)),
            pl.BlockSpec((None, d, tn), lambda l, j: (l, 0, j)),
            pl.BlockSpec((None, 1, tn), lambda l, j: (l, 0, j)),
        ],
        out_specs=pl.BlockSpec((None, 8, tn), lambda l, j: (l, 0, j)),
        out_shape=jax.ShapeDtypeStruct((depth, 8, n6), F32),
        compiler_params=_cparams(("arbitrary", "arbitrary")),
        name="adaln",
    )(c8, w_ada, b_ada.reshape(depth, 1, n6))


def _norm_mod(x, gain, shift, scale):
    ms = jnp.mean(x * x, axis=-1, keepdims=True)
    h = x * lax.rsqrt(ms + EPS) * gain
    return h * (1.0 + scale) + shift


def _mod_row_map(tm, seq, n_lat_batches):
    return lambda i: (jnp.minimum((i * tm) // seq, n_lat_batches), 0, 0)


def _final_norm_kernel(x_ref, g_ref, o_ref):
    x = x_ref[...]
    ms = jnp.mean(x * x, axis=-1, keepdims=True)
    o_ref[...] = x * lax.rsqrt(ms + EPS) * g_ref[...]


def final_norm(xa, gain, t_rows, tm):
    d = xa.shape[1]
    return pl.pallas_call(
        _final_norm_kernel,
        grid=(t_rows // tm,),
        in_specs=[pl.BlockSpec((tm, d), lambda i: (i, 0)), pl.BlockSpec((1, d), lambda i: (0, 0))],
        out_specs=pl.BlockSpec((tm, d), lambda i: (i, 0)),
        out_shape=jax.ShapeDtypeStruct((t_rows, d), F32),
        compiler_params=_cparams(("arbitrary",)),
        name="final_norm",
    )(xa, gain.reshape(1, d))


def _inproj_even_kernel(x_ref, g_ref, sh_ref, sc_ref, w_ref, cos_ref, s1_ref, s2_ref, qg_ref, kg_ref, bd_ref,
                        rq_ref, rk_ref, vg_ref, aq_ref, ak_ref, av_ref):
    hb = _norm_mod(x_ref[...], g_ref[...], sh_ref[...], sc_ref[...]).astype(BF16)
    cos = cos_ref[...]
    s1 = s1_ref[...]
    s2 = s2_ref[...]
    bd = bd_ref[...]
    half = ATT_HD

    def proj(c0, width=LANES):
        return _dot(hb, w_ref[:, c0:c0 + width])

    def rope(x):
        return x * cos + pltpu.roll(x, LANES - 16, 1) * s1 + pltpu.roll(x, 16, 1) * s2

    def head_norm(x, gain):
        sq = x * x
        hi = sq.astype(BF16)
        lo = (sq - hi.astype(F32)).astype(BF16)
        ms = _dot(hi, bd) + _dot(lo, bd)
        return x * lax.rsqrt(ms + EPS) * gain

    wide = 2 * LANES
    qw = RET_HEADS * RET_DK
    for c0 in range(0, qw, wide):
        yq = proj(c0, wide)
        yk = proj(qw + c0, wide)
        for u in range(2):
            cs = slice(c0 + u * LANES, c0 + (u + 1) * LANES)
            us = slice(u * LANES, (u + 1) * LANES)
            rq_ref[:, cs] = rope(yq[:, us]).astype(BF16)
            rk_ref[:, cs] = (rope(yk[:, us]) * RET_DK ** -0.5).astype(BF16)
    vgw = 2 * RET_HEADS * RET_DV
    for c0 in range(0, vgw, 512):
        vg_ref[:, c0:c0 + 512] = proj(2 * qw + c0, 512).astype(BF16)

    qg = qg_ref[...]
    kg = kg_ref[...]
    a0 = EVEN_ATT_COL
    for c0 in range(0, ATT_HEADS * ATT_HD, wide):
        ya = proj(a0 + c0, wide)
        for u in range(2):
            y = rope(head_norm(ya[:, u * LANES:(u + 1) * LANES], qg)) * (ATT_HD ** -0.5 * math.log2(math.e))
            y = y.astype(BF16)
            hd0 = (c0 + u * LANES) // ATT_HD
            aq_ref[hd0] = y[:, :half]
            aq_ref[hd0 + 1] = y[:, half:]
    ykv = proj(a0 + ATT_HEADS * ATT_HD, wide)
    y = rope(head_norm(ykv[:, :LANES], kg)).astype(BF16)
    ak_ref[0] = y[:, :half]
    ak_ref[1] = y[:, half:]
    v = ykv[:, LANES:].astype(BF16)
    av_ref[0] = v[:, :half]
    av_ref[1] = v[:, half:]


def inproj_even(xa, gain, shift, scale, w, tabs, q_gain, k_gain, tm, seq, nb):
    ta, d = xa.shape
    t_lat = nb * seq
    cos_t, s1_t, s2_t = tabs
    n_tab = seq // tm

    def tab_map(i):
        r = i * tm
        return (jnp.where(r < t_lat, (r % seq) // tm, n_tab), 0)

    ii = np.arange(LANES)
    bd = jnp.asarray((ii[:, None] // ATT_HD == ii[None, :] // ATT_HD).astype(np.float32) / ATT_HD).astype(BF16)
    qg = jnp.tile(q_gain.astype(F32), LANES // ATT_HD).reshape(1, LANES)
    kg = jnp.tile(k_gain.astype(F32), LANES // ATT_HD).reshape(1, LANES)
    mrow = _mod_row_map(tm, seq, nb)
    tab_spec = pl.BlockSpec((tm, LANES), tab_map)
    one = lambda i: (0, 0)
    qw = RET_HEADS * RET_DK
    vgw = 2 * RET_HEADS * RET_DV
    return pl.pallas_call(
        _inproj_even_kernel,
        grid=(ta // tm,),
        in_specs=[
            pl.BlockSpec((tm, d), lambda i: (i, 0)),
            pl.BlockSpec((1, d), one),
            pl.BlockSpec((None, 1, d), mrow),
            pl.BlockSpec((None, 1, d), mrow),
            pl.BlockSpec((d, EVEN_IN), one),
            tab_spec, tab_spec, tab_spec,
            pl.BlockSpec((1, LANES), one), pl.BlockSpec((1, LANES), one),
            pl.BlockSpec((LANES, LANES), one),
        ],
        out_specs=[
            pl.BlockSpec((tm, qw), lambda i: (i, 0)),
            pl.BlockSpec((tm, qw), lambda i: (i, 0)),
            pl.BlockSpec((tm, vgw), lambda i: (i, 0)),
            pl.BlockSpec((ATT_HEADS, tm, ATT_HD), lambda i: (0, i, 0)),
            pl.BlockSpec((ATT_KV_HEADS, tm, ATT_HD), lambda i: (0, i, 0)),
            pl.BlockSpec((ATT_KV_HEADS, tm, ATT_HD), lambda i: (0, i, 0)),
        ],
        out_shape=[
            jax.ShapeDtypeStruct((ta, qw), BF16),
            jax.ShapeDtypeStruct((ta, qw), BF16),
            jax.ShapeDtypeStruct((ta, vgw), BF16),
            jax.ShapeDtypeStruct((ATT_HEADS, ta, ATT_HD), BF16),
            jax.ShapeDtypeStruct((ATT_KV_HEADS, ta, ATT_HD), BF16),
            jax.ShapeDtypeStruct((ATT_KV_HEADS, ta, ATT_HD), BF16),
        ],
        compiler_params=_cparams(("arbitrary",)),
        name="inproj_even",
    )(xa, gain.reshape(1, d), shift, scale, w, cos_t, s1_t, s2_t, qg, kg, bd)


def rope_tables(seq, tm):
    nf = ATT_HD // 4
    t = jnp.arange(seq)
    rows = (t // GRID_W).astype(F32)
    cols = (t % GRID_W).astype(F32)
    inv = ROPE_BASE ** (-jnp.arange(nf, dtype=F32) / nf)
    lane = np.arange(LANES)
    axis = (lane % ATT_HD) // (ATT_HD // 2)
    f = lane % nf
    upper = ((lane % (ATT_HD // 2)) >= nf)
    pos = jnp.where(jnp.asarray(axis)[None, :] == 0, rows[:, None], cols[:, None])
    ang = pos * inv[jnp.asarray(f)][None, :]
    cos = jnp.cos(ang)
    sin = jnp.sin(ang)
    s1 = jnp.where(jnp.asarray(upper)[None, :], 0.0, -sin)
    s2 = jnp.where(jnp.asarray(upper)[None, :], sin, 0.0)
    pad1 = jnp.ones((tm, LANES), F32)
    pad0 = jnp.zeros((tm, LANES), F32)
    return (jnp.concatenate([cos, pad1]), jnp.concatenate([s1, pad0]), jnp.concatenate([s2, pad0]))


def _retention_kernel(dec_ref, q_ref, k_ref, v_ref, g_ref, s0f_ref, s0b_ref,
                      o_ref, sff_ref, sfb_ref, st_ref, *, n_chunks, unroll):
    hp = pl.program_id(1)
    C = RET_CHUNK
    dk, dv = RET_DK, RET_DV
    pos = lax.broadcasted_iota(jnp.int32, (C, dk), 0).astype(F32)
    ii = lax.broadcasted_iota(jnp.int32, (C, C), 0)
    jj = lax.broadcasted_iota(jnp.int32, (C, C), 1)
    dpos = (ii - jj).astype(F32)
    heads = range(2)
    qs = [slice(hh * dk, (hh + 1) * dk) for hh in heads]
    vs = [slice(hh * dv, (hh + 1) * dv) for hh in heads]
    w_out, w_in, gcf, gcb, mask = [], [], [], [], []
    for hh in heads:
        h = 2 * hp + hh
        df = dec_ref[0, h]
        db = dec_ref[1, h]
        lf = -jnp.exp(jnp.full((C, C), df, F32))
        lb = -jnp.exp(jnp.full((C, C), db, F32))
        lfk = -jnp.exp(jnp.full((C, dk), df, F32))
        lbk = -jnp.exp(jnp.full((C, dk), db, F32))
        w_out.append(jnp.concatenate([jnp.exp(lfk * (C - 1.0 - pos)), jnp.exp(lbk * pos)], axis=1))
        w_in.append(jnp.concatenate([jnp.exp(lfk * (pos + 1.0)), jnp.exp(lbk * (C - pos))], axis=1))
        gcf.append(jnp.exp(-jnp.exp(jnp.full((dk, dv), df, F32)) * C))
        gcb.append(jnp.exp(-jnp.exp(jnp.full((dk, dv), db, F32)) * C))
        mask.append(jnp.where(dpos > 0, jnp.exp(lf * jnp.maximum(dpos, 0.0)),
                              jnp.where(dpos < 0, jnp.exp(lb * jnp.maximum(-dpos, 0.0)), 2.0)))

    def rows(n):
        return pl.ds(pl.multiple_of(n * C, C), C)

    items = [(u, hh) for u in range(unroll) for hh in heads]

    def sums_body(i, carry):
        kk = {}
        for u, hh in items:
            k = k_ref[rows(i * unroll + u), qs[hh]].astype(F32)
            kk[(u, hh)] = (jnp.concatenate([k, k], axis=1) * w_out[hh]).astype(BF16)
        kv = {(u, hh): lax.dot_general(kk[(u, hh)], v_ref[rows(i * unroll + u), vs[hh]], TN_DIMS,
                                       preferred_element_type=F32) for u, hh in items}
        for u, hh in items:
            st_ref[hh, i * unroll + u] = kv[(u, hh)]
        return carry

    lax.fori_loop(0, n_chunks // unroll, sums_body, 0)

    def scan_body(n, carry):
        n_rev = n_chunks - 1 - n
        out = []
        for hh in heads:
            sf, sb = carry[2 * hh], carry[2 * hh + 1]
            kvf = st_ref[hh, n, 0:dk, :]
            kvb = st_ref[hh, n_rev, dk:2 * dk, :]
            st_ref[hh, n, 0:dk, :] = sf
            st_ref[hh, n_rev, dk:2 * dk, :] = sb
            out += [gcf[hh] * sf + kvf, gcb[hh] * sb + kvb]
        return tuple(out)

    init = tuple(x for hh in heads for x in (s0f_ref[hh], s0b_ref[hh]))
    fin = lax.fori_loop(0, n_chunks, scan_body, init)
    for hh in heads:
        sff_ref[hh] = fin[2 * hh]
        sfb_ref[hh] = fin[2 * hh + 1]

    def out_body(i, carry):
        ns = [i * unroll + u for u in range(unroll)]
        qb = {(u, hh): q_ref[rows(ns[u]), qs[hh]] for u, hh in items}
        sc = {(u, hh): lax.dot_general(qb[(u, hh)], k_ref[rows(ns[u]), qs[hh]], NT_DIMS,
                                       preferred_element_type=F32) for u, hh in items}
        qw = {}
        for it in items:
            q = qb[it].astype(F32)
            qw[it] = (jnp.concatenate([q, q], axis=1) * w_in[it[1]]).astype(BF16)
        o1 = {(u, hh): _dot((sc[(u, hh)] * mask[hh]).astype(BF16), v_ref[rows(ns[u]), vs[hh]]) for u, hh in items}
        o2 = {(u, hh): _dot(qw[(u, hh)], st_ref[hh, ns[u]].astype(BF16)) for u, hh in items}
        for it in items:
            u, hh = it
            n = ns[u]
            o = o1[it] + o2[it]
            o = o * lax.rsqrt(jnp.mean(o * o, axis=-1, keepdims=True) + EPS)
            gate = g_ref[rows(n), vs[hh]].astype(F32)
            o_ref[rows(n), vs[hh]] = (_silu(gate) * o).astype(o_ref.dtype)
        return carry

    lax.fori_loop(0, n_chunks // unroll, out_body, 0)


def retention(dec, rq, rk, p, s0f, s0b, nb, seq, row_off_blocks):
    n_chunks = seq // RET_CHUNK
    hp_n = RET_HEADS // 2
    vcol = 0
    gcol = vcol + RET_HEADS * RET_DV // (2 * RET_DV)
    ta = rq.shape[0]
    st_spec = pl.BlockSpec((None, 2, RET_DK, RET_DV), lambda b, hp, *_: (b, hp, 0, 0))
    grid_spec = pltpu.PrefetchScalarGridSpec(
        num_scalar_prefetch=1,
        grid=(nb, hp_n),
        in_specs=[
            pl.BlockSpec((seq, 2 * RET_DK), lambda b, hp, *_: (row_off_blocks + b, hp)),
            pl.BlockSpec((seq, 2 * RET_DK), lambda b, hp, *_: (row_off_blocks + b, hp)),
            pl.BlockSpec((seq, 2 * RET_DV), lambda b, hp, *_: (row_off_blocks + b, vcol + hp)),
            pl.BlockSpec((seq, 2 * RET_DV), lambda b, hp, *_: (row_off_blocks + b, gcol + hp)),
            st_spec, st_spec,
        ],
        out_specs=[
            pl.BlockSpec((seq, 2 * RET_DV), lambda b, hp, *_: (b, hp)),
            st_spec, st_spec,
        ],
        scratch_shapes=[pltpu.VMEM((2, n_chunks, 2 * RET_DK, RET_DV), F32)],
    )
    st_shape = jax.ShapeDtypeStruct((nb, RET_HEADS, RET_DK, RET_DV), F32)
    return pl.pallas_call(
        functools.partial(_retention_kernel, n_chunks=n_chunks, unroll=math.gcd(n_chunks, 4)),
        grid_spec=grid_spec,
        out_shape=[jax.ShapeDtypeStruct((nb * seq, RET_HEADS * RET_DV), BF16), st_shape, st_shape],
        compiler_params=_cparams(("arbitrary", "arbitrary")),
        name="retention",
    )(dec, rq, rk, p, p, s0f, s0b)


ATT_VT_ROWS = ATT_HD + 16


ATT_PAIRS_PER_TRIP = 4


def _attn_kernel(q_ref, k_ref, vt_ref, o_ref, *s_refs, tk, c_start, c_end, rep):
    tq = q_ref.shape[2]
    sets = (s_refs[:rep], s_refs[rep:])
    last = c_end - 1

    def scores(bufs, j):
        j = jnp.minimum(j, last)
        c0 = pl.multiple_of(j * tk, tk)
        k = k_ref[pl.ds(c0, tk), :]
        mxs = []
        for r in range(rep):
            s = _dot(k, q_ref[r])
            bufs[r][...] = s
            mxs.append(jnp.max(s, axis=0, keepdims=True))
        return tuple(mxs)

    def softmax_pv(bufs, j, mxs, ms, accs):
        vt = vt_ref[j]
        new_m, new_acc = [], []
        for r in range(rep):
            m_new = jnp.maximum(ms[r], mxs[r])
            a = jnp.exp2(ms[r] - m_new)
            p = jnp.exp2(bufs[r][...] - m_new).astype(BF16)
            new_acc.append(a * accs[r] + _dot(vt, p))
            new_m.append(m_new)
        return tuple(new_m), tuple(new_acc)

    def pair(j, mx0, ms, accs):
        mx1 = scores(sets[1], j + 1)
        ms, accs = softmax_pv(sets[0], j, mx0, ms, accs)
        mx0 = scores(sets[0], j + 2)
        ms, accs = softmax_pv(sets[1], j + 1, mx1, ms, accs)
        return mx0, ms, accs

    def trip(t, carry):
        for u in range(ATT_PAIRS_PER_TRIP):
            carry = pair(c_start + 2 * (ATT_PAIRS_PER_TRIP * t + u), *carry)
        return carry

    n_pairs = (c_end - c_start) // 2
    n_trips = n_pairs // ATT_PAIRS_PER_TRIP
    ms = tuple(jnp.full((1, tq), -1e30, F32) for _ in range(rep))
    accs = tuple(jnp.zeros((ATT_VT_ROWS, tq), F32) for _ in range(rep))
    carry = (scores(sets[0], c_start), ms, accs)
    if n_trips:
        carry = lax.fori_loop(0, n_trips, trip, carry)
    for u in range(n_trips * ATT_PAIRS_PER_TRIP, n_pairs):
        carry = pair(c_start + 2 * u, *carry)
    mx0, ms, accs = carry
    if (c_end - c_start) % 2:
        ms, accs = softmax_pv(sets[0], last, mx0, ms, accs)
    outs = [(acc[:ATT_HD, :] / acc[ATT_HD:ATT_HD + 1, :]).T for acc in accs]
    o_ref[...] = jnp.concatenate(outs, axis=-1).astype(o_ref.dtype)


def attention(aq, kcat, vtcat, seq_q, q_off_blocks, c_start, tq, tk):
    rep = ATT_HEADS // ATT_KV_HEADS
    _, nb, lk, _ = kcat.shape
    nq = seq_q // tq
    n_chunks = lk // tk
    return pl.pallas_call(
        functools.partial(_attn_kernel, tk=tk, c_start=c_start, c_end=n_chunks, rep=rep),
        grid=(nb, ATT_KV_HEADS, nq),
        in_specs=[
            pl.BlockSpec((rep, ATT_HD, tq), lambda b, g, i: (g, 0, q_off_blocks + b * nq + i)),
            pl.BlockSpec((None, None, lk, ATT_HD), lambda b, g, i: (g, b, 0, 0)),
            pl.BlockSpec((None, None, n_chunks, ATT_VT_ROWS, tk), lambda b, g, i: (g, b, 0, 0, 0)),
        ],
        out_specs=pl.BlockSpec((tq, rep * ATT_HD), lambda b, g, i: (b * nq + i, g)),
        out_shape=jax.ShapeDtypeStruct((nb * seq_q, ATT_HEADS * ATT_HD), BF16),
        scratch_shapes=[pltpu.VMEM((tk, tq), F32) for _ in range(2 * rep)],
        compiler_params=_cparams(("arbitrary", "arbitrary", "arbitrary")),
        name="attention",
    )(aq, kcat, vtcat)


def _lat_ctx_specs(tm, width, n_lat_tiles):
    return [pl.BlockSpec((tm, width), lambda i: (jnp.minimum(i, n_lat_tiles - 1), 0)),
            pl.BlockSpec((tm, width), lambda i: (jnp.maximum(i - n_lat_tiles, 0), 0))]


def _outproj_even_kernel(r_lat, r_ctx, a_lat, a_ctx, w1_ref, w2_ref, res_ref, gate_ref, o_ref, *, n_lat_tiles):
    is_lat = pl.program_id(0) < n_lat_tiles
    a1 = jnp.where(is_lat, r_lat[...], r_ctx[...])
    a2 = jnp.where(is_lat, a_lat[...], a_ctx[...])
    y = _dot(a1, w1_ref[...]) + _dot(a2, w2_ref[...])
    o_ref[...] = res_ref[...] + gate_ref[...] * y


def outproj_even(ret_lat, ret_ctx, att_lat, att_ctx, w1, w2, xa, gate, tm, seq, nb):
    ta, d = xa.shape
    k1, k2 = w1.shape[0], w2.shape[0]
    n_lat_tiles = ret_lat.shape[0] // tm
    return pl.pallas_call(
        functools.partial(_outproj_even_kernel, n_lat_tiles=n_lat_tiles),
        grid=(ta // tm,),
        in_specs=_lat_ctx_specs(tm, k1, n_lat_tiles) + _lat_ctx_specs(tm, k2, n_lat_tiles) + [
            pl.BlockSpec((k1, d), lambda i: (0, 0)),
            pl.BlockSpec((k2, d), lambda i: (0, 0)),
            pl.BlockSpec((tm, d), lambda i: (i, 0)),
            pl.BlockSpec((None, 1, d), _mod_row_map(tm, seq, nb)),
        ],
        out_specs=pl.BlockSpec((tm, d), lambda i: (i, 0)),
        out_shape=jax.ShapeDtypeStruct((ta, d), F32),
        compiler_params=_cparams(("arbitrary",)),
        name="outproj_even",
    )(ret_lat, ret_ctx, att_lat, att_ctx, w1, w2, xa, gate)


def _outproj_odd_kernel(f_lat, f_ctx, b_lat, b_ctx, z_ref, og_ref, w_ref, res_ref, gate_ref, o_ref,
                        *, n_lat_tiles):
    is_lat = pl.program_id(0) < n_lat_tiles
    og = og_ref[...]
    parts = []
    for h in range(DN_HEADS):
        cs = slice(h * DN_DV, (h + 1) * DN_DV)
        of = jnp.where(is_lat, f_lat[:, cs], f_ctx[:, cs]).astype(F32)
        ob = jnp.where(is_lat, b_lat[:, cs], b_ctx[:, cs]).astype(F32)
        o = of + ob
        o = o * lax.rsqrt(jnp.mean(o * o, axis=-1, keepdims=True) + EPS) * og
        parts.append((o * _silu(z_ref[:, cs].astype(F32))).astype(BF16))
    a = jnp.concatenate(parts, axis=-1)
    o_ref[...] = res_ref[...] + gate_ref[...] * _dot(a, w_ref[...])


def outproj_odd(of_lat, of_ctx, ob_lat, ob_ctx, p, out_gain, w, xa, gate, tm, seq, nb):
    ta, d = xa.shape
    kdim = DN_HEADS * DN_DV
    n_lat_tiles = of_lat.shape[0] // tm
    return pl.pallas_call(
        functools.partial(_outproj_odd_kernel, n_lat_tiles=n_lat_tiles),
        grid=(ta // tm,),
        in_specs=_lat_ctx_specs(tm, kdim, n_lat_tiles) + _lat_ctx_specs(tm, kdim, n_lat_tiles) + [
            pl.BlockSpec((tm, kdim), lambda i: (i, 0)),
            pl.BlockSpec((1, DN_DV), lambda i: (0, 0)),
            pl.BlockSpec((kdim, d), lambda i: (0, 0)),
            pl.BlockSpec((tm, d), lambda i: (i, 0)),
            pl.BlockSpec((None, 1, d), _mod_row_map(tm, seq, nb)),
        ],
        out_specs=pl.BlockSpec((tm, d), lambda i: (i, 0)),
        out_shape=jax.ShapeDtypeStruct((ta, d), F32),
        compiler_params=_cparams(("arbitrary",)),
        name="outproj_odd",
    )(of_lat, of_ctx, ob_lat, ob_ctx, p, out_gain.reshape(1, DN_DV).astype(F32), w, xa, gate)


def _inproj_odd_kernel(first_ref, last_ref, x_ref, xp_ref, xn_ref, g_ref, sh_ref, sc_ref, w_ref, cw_ref,
                       arow_ref, brow_ref, q_ref, k_ref, v_ref, z_ref, gb_ref, *, ctx_tile0, cseq):
    i = pl.program_id(0)
    tm = x_ref.shape[0]
    gain, shift, scale = g_ref[...], sh_ref[...], sc_ref[...]
    hrows = xp_ref.shape[0]
    hall = _norm_mod(jnp.concatenate([x_ref[...], xp_ref[...], xn_ref[...]], axis=0), gain, shift, scale).astype(BF16)
    hb = hall[:tm]
    keep_prev = 1.0 - first_ref[i].astype(F32)
    keep_next = 1.0 - last_ref[i].astype(F32)
    row = lax.broadcasted_iota(jnp.int32, (tm, LANES), 0)
    is_first = row == 0
    is_last = row == tm - 1
    inner = cseq < tm
    if inner:
        in_ctx = i >= ctx_tile0
        local = row & (cseq - 1)
        zero_dn = jnp.logical_and(in_ctx, local == 0)
        zero_up = jnp.logical_and(in_ctx, local == cseq - 1)
    n_qk = 2 * DN_HEADS * DN_DK // LANES
    n_q = DN_HEADS * DN_DK // LANES
    outs = (q_ref, k_ref, v_ref)
    wide = 2 * LANES
    for c0 in range(0, DN_QKV, wide):
        yall = _dot(hall, w_ref[:, c0:c0 + wide])
        y2 = yall[:tm]
        yh2 = yall[tm:]
        for u in range(2):
            j = c0 // LANES + u
            us = slice(u * LANES, (u + 1) * LANES)
            x = y2[:, us]
            xp = yh2[hrows - 1:hrows, us] * keep_prev
            xn = yh2[hrows:hrows + 1, us] * keep_next
            x_dn = jnp.where(is_first, xp, pltpu.roll(x, 1, 0))
            x_up = jnp.where(is_last, xn, pltpu.roll(x, tm - 1, 0))
            if inner:
                x_dn = jnp.where(zero_dn, 0.0, x_dn)
                x_up = jnp.where(zero_up, 0.0, x_up)
            w = cw_ref[:, j * LANES:(j + 1) * LANES]
            y = _silu(x_dn * w[0:1, :] + x * w[1:2, :] + x_up * w[2:3, :])
            if j < n_qk:
                y = y * lax.rsqrt(jnp.sum(y * y, axis=-1, keepdims=True) + EPS)
                if j < n_q:
                    y = y * DN_DK ** -0.5
            lj = j % n_q
            outs[j // n_q][:, lj * LANES:(lj + 1) * LANES] = y.astype(BF16)
    zw = DN_HEADS * DN_DV
    for c0 in range(0, zw, 512):
        z_ref[:, c0:c0 + 512] = _dot(hb, w_ref[:, DN_QKV + c0:DN_QKV + c0 + 512]).astype(BF16)

    a = _dot(hb, w_ref[:, DN_QKV + zw:DN_QKV + zw + LANES])
    lane = lax.broadcasted_iota(jnp.int32, (tm, LANES), 1)
    zz = a + brow_ref[...]
    softplus = jnp.maximum(zz, 0.0) + jnp.log(1.0 + jnp.exp(-jnp.abs(zz)))
    g = -jnp.exp(arow_ref[...]) * softplus
    beta = 1.0 / (1.0 + jnp.exp(-a))
    gb_ref[...] = jnp.where(lane < 2 * DN_HEADS, g, jnp.where(lane < 4 * DN_HEADS, beta, 0.0))


def inproj_odd(xa, gain, shift, scale, w, conv_w, arow, brow, first_flags, last_flags, tm, seq, nb, cseq):
    ta, d = xa.shape
    halo = 8
    hb = tm // halo
    n_h = ta // halo
    kdim = DN_HEADS * DN_DK
    assert cseq >= tm or (tm % cseq == 0 and cseq & (cseq - 1) == 0)
    mrow = lambda i, *_: _mod_row_map(tm, seq, nb)(i)
    one = lambda i, *_: (0, 0)
    row_blk = lambda i, *_: (i, 0)
    grid_spec = pltpu.PrefetchScalarGridSpec(
        num_scalar_prefetch=2,
        grid=(ta // tm,),
        in_specs=[
            pl.BlockSpec((tm, d), row_blk),
            pl.BlockSpec((halo, d), lambda i, *_: (jnp.maximum(i * hb - 1, 0), 0)),
            pl.BlockSpec((halo, d), lambda i, *_: (jnp.minimum((i + 1) * hb, n_h - 1), 0)),
            pl.BlockSpec((1, d), one),
            pl.BlockSpec((None, 1, d), mrow),
            pl.BlockSpec((None, 1, d), mrow),
            pl.BlockSpec((d, ODD_IN_PAD), one),
            pl.BlockSpec((DN_CONV, DN_QKV), one),
            pl.BlockSpec((1, LANES), one),
            pl.BlockSpec((1, LANES), one),
        ],
        out_specs=[pl.BlockSpec((tm, kdim), row_blk)] * 4 + [pl.BlockSpec((tm, LANES), row_blk)],
    )
    return pl.pallas_call(
        functools.partial(_inproj_odd_kernel, ctx_tile0=nb * seq // tm, cseq=cseq),
        grid_spec=grid_spec,
        out_shape=[jax.ShapeDtypeStruct((ta, kdim), BF16)] * 4 + [jax.ShapeDtypeStruct((ta, LANES), F32)],
        compiler_params=_cparams(("arbitrary",)),
        name="inproj_odd",
    )(first_flags, last_flags, xa, xa, xa, gain.reshape(1, d), shift, scale, w, conv_w, arow, brow)


def _deltanet_bidir_kernel(qf_ref, kf_ref, vf_ref, gbf_ref, gbtf_ref, qb_ref, kb_ref, vb_ref, gbb_ref, gbtb_ref,
                           s0f_ref, s0b_ref, of_ref, ob_ref, sff_ref, sfb_ref, sf_scr, sb_scr, *, n_chunks):
    t = pl.program_id(1)

    @pl.when(t == 0)
    def _():
        sf_scr[...] = s0f_ref[...]
        sb_scr[...] = s0b_ref[...]

    C = DN_CHUNK
    ii = lax.broadcasted_iota(jnp.int32, (C, C), 0)
    jj = lax.broadcasted_iota(jnp.int32, (C, C), 1)
    lower, upper = ii >= jj, ii <= jj
    eye = jnp.where(ii == jj, 1.0, 0.0).astype(F32)
    blk = ii ^ jj
    dirs = (
        dict(rev=False, incl=lower, strict=ii > jj, q=qf_ref, k=kf_ref, v=vf_ref, gb=gbf_ref, gbt=gbtf_ref,
             o=of_ref, scr=sf_scr, off=0, order=list(range(n_chunks))),
        dict(rev=True, incl=upper, strict=ii < jj, q=qb_ref, k=kb_ref, v=vb_ref, gb=gbb_ref, gbt=gbtb_ref,
             o=ob_ref, scr=sb_scr, off=DN_HEADS, order=list(range(n_chunks - 1, -1, -1))),
    )
    items = [(d, c, h) for d in range(2) for c in dirs[d]["order"] for h in range(DN_HEADS)]

    gcols, grows, gbs = {}, {}, {}
    for d, dr in enumerate(dirs):
        tri = jnp.where(dr["incl"], 1.0, 0.0).astype(F32)
        tri_t = jnp.where(upper if not dr["rev"] else lower, 1.0, 0.0).astype(F32)
        for c in dr["order"]:
            gb_c = dr["gb"][c * C:(c + 1) * C, :]
            gbs[(d, c)] = gb_c
            gcols[(d, c)] = jnp.dot(tri, gb_c, preferred_element_type=F32, precision=HIGHEST)
            grows[(d, c)] = jnp.dot(dr["gbt"][c], tri_t, preferred_element_type=F32, precision=HIGHEST)

    qb, kb16, decay, kbeta, egc, kd, gl, rhs = {}, {}, {}, {}, {}, {}, {}, {}
    for it in items:
        d, c, h = it
        dr = dirs[d]
        gi = dr["off"] + h
        bi = 2 * DN_HEADS + dr["off"] + h
        rows = slice(c * C, (c + 1) * C)
        cs = slice(h * DN_DK, (h + 1) * DN_DK)
        gc = gcols[(d, c)][:, gi:gi + 1]
        gr = grows[(d, c)][gi:gi + 1, :]
        beta = gbs[(d, c)][:, bi:bi + 1]
        qb[it] = dr["q"][rows, cs]
        kb16[it] = dr["k"][rows, cs]
        kf = kb16[it].astype(F32)
        decay[it] = jnp.where(dr["incl"], jnp.exp(jnp.where(dr["incl"], gc - gr, 0.0)), 0.0)
        kbeta[it] = kf * beta
        egc[it] = jnp.exp(gc)
        glast = gc[0:1, :] if dr["rev"] else gc[C - 1:C, :]
        kd[it] = (kf * jnp.exp(glast - gc)).astype(BF16)
        gl[it] = jnp.exp(glast)
        rhs[it] = jnp.concatenate([dr["v"][rows, cs].astype(F32) * beta, kbeta[it] * egc[it]], axis=1).astype(BF16)

    kk = {it: lax.dot_general(kbeta[it].astype(BF16), kb16[it], NT_DIMS, preferred_element_type=F32)
          for it in items}
    qk = {it: lax.dot_general(qb[it], kb16[it], NT_DIMS, preferred_element_type=F32) for it in items}
    lm = {it: jnp.where(dirs[it[0]]["strict"], kk[it] * decay[it], 0.0) for it in items}
    attn = {it: jnp.where(dirs[it[0]]["incl"], qk[it] * decay[it], 0.0).astype(BF16) for it in items}
    dinv = {it: eye - jnp.where(blk < 2, lm[it], 0.0) for it in items}
    s = 2
    while s < C:
        in_band = jnp.logical_and(blk >= s, blk < 2 * s)
        tmp = {it: _dot(dinv[it].astype(BF16), jnp.where(in_band, lm[it], 0.0).astype(BF16)) for it in items}
        dinv = {it: dinv[it] - _dot(tmp[it].astype(BF16), dinv[it].astype(BF16)) for it in items}
        s *= 2
    uw = {it: _dot(dinv[it].astype(BF16), rhs[it]) for it in items}
    wq = {it: jnp.concatenate([uw[it][:, DN_DV:], qb[it].astype(F32) * egc[it]], axis=0).astype(BF16)
          for it in items}

    states = {(d, h): dirs[d]["scr"][h] for d in range(2) for h in range(DN_HEADS)}
    for step in range(n_chunks):
        its = [(d, dirs[d]["order"][step], h) for d in range(2) for h in range(DN_HEADS)]
        r = {it: _dot(wq[it], states[(it[0], it[2])].astype(BF16)) for it in its}
        v_new = {it: (uw[it][:, :DN_DV] - r[it][:C]).astype(BF16) for it in its}
        o = {it: r[it][C:] + _dot(attn[it], v_new[it]) for it in its}
        for it in its:
            key = (it[0], it[2])
            states[key] = states[key] * gl[it] + lax.dot_general(kd[it], v_new[it], TN_DIMS,
                                                                 preferred_element_type=F32)
        for it in its:
            d, c, h = it
            dirs[d]["o"][c * C:(c + 1) * C, h * DN_DK:(h + 1) * DN_DK] = o[it].astype(of_ref.dtype)
    for (d, h), st in states.items():
        dirs[d]["scr"][h] = st

    @pl.when(t == pl.num_programs(1) - 1)
    def _():
        sff_ref[...] = sf_scr[...]
        sfb_ref[...] = sb_scr[...]


def deltanet_bidir(q, k, v, gb, gbt, s0f, s0b, nb, seq, row_off, tl):
    nblk = seq // tl
    n_chunks = tl // DN_CHUNK
    off_b = row_off // tl
    kdim = DN_HEADS * DN_DK

    def fwd_rb(b, t):
        return off_b + b * nblk + t

    def bwd_rb(b, t):
        return off_b + b * nblk + (nblk - 1 - t)

    def seq_specs(rb):
        spec = pl.BlockSpec((tl, kdim), lambda b, t: (rb(b, t), 0))
        return [spec, spec, spec,
                pl.BlockSpec((tl, LANES), lambda b, t: (rb(b, t), 0)),
                pl.BlockSpec((n_chunks, 4 * DN_HEADS, DN_CHUNK), lambda b, t: (rb(b, t), 0, 0))]

    st_spec = pl.BlockSpec((None, DN_HEADS, DN_DK, DN_DV), lambda b, t: (b, 0, 0, 0))
    st_shape = jax.ShapeDtypeStruct((nb, DN_HEADS, DN_DK, DN_DV), F32)
    o_shape = jax.ShapeDtypeStruct((nb * seq, kdim), BF16)
    return pl.pallas_call(
        functools.partial(_deltanet_bidir_kernel, n_chunks=n_chunks),
        grid=(nb, nblk),
        in_specs=seq_specs(fwd_rb) + seq_specs(bwd_rb) + [st_spec, st_spec],
        out_specs=[
            pl.BlockSpec((tl, kdim), lambda b, t: (b * nblk + t, 0)),
            pl.BlockSpec((tl, kdim), lambda b, t: (b * nblk + nblk - 1 - t, 0)),
            st_spec, st_spec,
        ],
        out_shape=[o_shape, o_shape, st_shape, st_shape],
        scratch_shapes=[pltpu.VMEM((DN_HEADS, DN_DK, DN_DV), F32), pltpu.VMEM((DN_HEADS, DN_DK, DN_DV), F32)],
        compiler_params=_cparams(("arbitrary", "arbitrary")),
        name="deltanet_bidir",
    )(q, k, v, gb, gbt, q, k, v, gb, gbt, s0f, s0b)


def _router_kernel(x_ref, g_ref, sh_ref, sc_ref, wr_ref, br_ref, ltri_ref, f_ref, r_ref, cnt_ref, base_ref):
    @pl.when(pl.program_id(0) == 0)
    def _():
        base_ref[...] = jnp.zeros_like(base_ref)

    h = _norm_mod(x_ref[...], g_ref[...], sh_ref[...], sc_ref[...])
    f_ref[...] = h
    logits = _dot(h.astype(BF16), wr_ref[...]) + br_ref[...]
    tm = logits.shape[0]
    lane = lax.broadcasted_iota(jnp.int32, (tm, LANES), 1)
    neg = -1e30
    big = 4 * LANES
    is_g = lane < N_GROUPS
    gl = jnp.where(is_g, logits, neg)
    gm = jnp.max(gl, axis=-1, keepdims=True)
    grp = jnp.min(jnp.where(gl == gm, lane, big), axis=-1, keepdims=True)
    psum = jnp.sum(jnp.where(is_g, jnp.exp(gl - gm), 0.0), axis=-1, keepdims=True)
    p_grp = 1.0 / psum
    e_lane = lane - N_GROUPS
    in_grp = jnp.logical_and(jnp.logical_and(e_lane >= 0, e_lane < N_EXPERTS),
                             (e_lane // EXPERTS_PER_GROUP) == grp)
    el = jnp.where(in_grp, logits, neg)
    m1 = jnp.max(el, axis=-1, keepdims=True)
    i1 = jnp.min(jnp.where(el == m1, lane, big), axis=-1, keepdims=True)
    el2 = jnp.where(lane == i1, neg, el)
    m2 = jnp.max(el2, axis=-1, keepdims=True)
    i2 = jnp.min(jnp.where(el2 == m2, lane, big), axis=-1, keepdims=True)
    e21 = jnp.exp(m2 - m1)
    w1 = p_grp / (1.0 + e21)
    w2 = p_grp * e21 / (1.0 + e21)
    e1 = (i1 - N_GROUPS).astype(F32)
    e2 = (i2 - N_GROUPS).astype(F32)
    oh1 = lane == i1
    oh2 = lane == i2
    oh1f = jnp.where(oh1, 1.0, 0.0)
    oh2f = jnp.where(oh2, 1.0, 0.0)
    ltri = ltri_ref[...]
    before1 = _dot(ltri, oh1f.astype(BF16))
    before2 = _dot(ltri, oh2f.astype(BF16))
    cnt1 = jnp.sum(oh1f, axis=0, keepdims=True)
    cnt2 = jnp.sum(oh2f, axis=0, keepdims=True)
    base = base_ref[0:1, :]
    rank1 = jnp.sum(jnp.where(oh1, base + before1, 0.0), axis=-1, keepdims=True)
    rank2 = jnp.sum(jnp.where(oh2, base + cnt1 + before2, 0.0), axis=-1, keepdims=True)
    total = base + cnt1 + cnt2
    base_ref[...] = jnp.broadcast_to(total, base_ref.shape)
    cnt_ref[...] = jnp.broadcast_to(total, cnt_ref.shape)
    vals = (e1, e2, w1, w2, rank1, rank2)
    out = jnp.zeros((tm, LANES), F32)
    for idx, val in enumerate(vals):
        out = jnp.where(lane == idx, val, out)
    r_ref[...] = out


def moe_router(xa, gain, shift, scale, w_router, b_router, tm, seq, nb):
    ta, d = xa.shape
    mrow = _mod_row_map(tm, seq, nb)
    ii = np.arange(tm)
    ltri = jnp.asarray((ii[:, None] > ii[None, :]).astype(np.float32)).astype(BF16)
    return pl.pallas_call(
        _router_kernel,
        grid=(ta // tm,),
        in_specs=[
            pl.BlockSpec((tm, d), lambda i: (i, 0)),
            pl.BlockSpec((1, d), lambda i: (0, 0)),
            pl.BlockSpec((None, 1, d), mrow),
            pl.BlockSpec((None, 1, d), mrow),
            pl.BlockSpec((d, LANES), lambda i: (0, 0)),
            pl.BlockSpec((1, LANES), lambda i: (0, 0)),
            pl.BlockSpec((tm, tm), lambda i: (0, 0)),
        ],
        out_specs=[pl.BlockSpec((tm, d), lambda i: (i, 0)), pl.BlockSpec((tm, LANES), lambda i: (i, 0)),
                   pl.BlockSpec((8, LANES), lambda i: (0, 0))],
        out_shape=[jax.ShapeDtypeStruct((ta, d), F32), jax.ShapeDtypeStruct((ta, LANES), F32),
                   jax.ShapeDtypeStruct((8, LANES), F32)],
        scratch_shapes=[pltpu.VMEM((8, LANES), F32)],
        compiler_params=_cparams(("arbitrary",)),
        name="moe_router",
    )(xa, gain.reshape(1, d), shift, scale, w_router, b_router, ltri)


ROW_DMA_UNROLL = 8


def _issue_row_copies(n_rows, make_copy):
    def trip(i, carry):
        for u in range(ROW_DMA_UNROLL):
            make_copy(i * ROW_DMA_UNROLL + u).start(priority=u % 2)
        return carry

    lax.fori_loop(0, n_rows // ROW_DMA_UNROLL, trip, 0)


def _moe_scatter_kernel(pos_ref, f_ref, xs_in, xs_out, sem):
    del xs_in
    tm = f_ref.shape[0]
    for k in range(TOP_K):
        _issue_row_copies(tm, lambda r, k=k: pltpu.make_async_copy(
            f_ref.at[pl.ds(r, 1)], xs_out.at[pl.ds(pos_ref[0, 0, k * tm + r], 1)], sem))
    for _ in range(2):
        pltpu.make_async_copy(f_ref, xs_out.at[pl.ds(0, tm)], sem).wait()


def moe_scatter(pos_tiles, f, xs_zero, tm):
    ta, d = f.shape
    return pl.pallas_call(
        _moe_scatter_kernel,
        grid=(ta // tm,),
        in_specs=[
            pl.BlockSpec((1, 1, 2 * tm), lambda i: (i, 0, 0), memory_space=pltpu.SMEM),
            pl.BlockSpec((tm, d), lambda i: (i, 0)),
            pl.BlockSpec(memory_space=pl.ANY),
        ],
        out_specs=pl.BlockSpec(memory_space=pl.ANY),
        out_shape=jax.ShapeDtypeStruct(xs_zero.shape, xs_zero.dtype),
        scratch_shapes=[pltpu.SemaphoreType.DMA(())],
        input_output_aliases={2: 0},
        compiler_params=_cparams(("arbitrary",)),
        name="moe_scatter",
    )(pos_tiles, f, xs_zero)


def _moe_ffn_kernel(te_ref, nu_ref, x_ref, wgu_ref, wd_ref, o_ref, wgu_bf, wd_bf):
    i = pl.program_id(0)
    fdim = wd_bf.shape[0]

    @pl.when(i < nu_ref[0])
    def _():
        prev = te_ref[jnp.maximum(i - 1, 0)]
        changed = jnp.logical_or(i == 0, te_ref[i] != prev)

        @pl.when(changed)
        def _():
            wgu_bf[...] = wgu_ref[...].astype(BF16)
            wd_bf[...] = wd_ref[...].astype(BF16)

        gu = _dot(x_ref[...].astype(BF16), wgu_bf[...])
        hmid = _silu(gu[:, :fdim]) * gu[:, fdim:]
        o_ref[...] = _dot(hmid.astype(BF16), wd_bf[...])

    @pl.when(i >= nu_ref[0])
    def _():
        o_ref[...] = jnp.zeros_like(o_ref)


def moe_ffn(tile_expert, n_used, xs, w_gate_up, w_down, layer, tm):
    n_pad, d = xs.shape
    f2 = w_gate_up.shape[-1]
    fdim = w_down.shape[-2]
    grid_spec = pltpu.PrefetchScalarGridSpec(
        num_scalar_prefetch=2,
        grid=(n_pad // tm,),
        in_specs=[
            pl.BlockSpec((tm, d), lambda i, te, nu: (i, 0)),
            pl.BlockSpec((None, None, d, f2), lambda i, te, nu: (layer, te[i], 0, 0)),
            pl.BlockSpec((None, None, fdim, d), lambda i, te, nu: (layer, te[i], 0, 0)),
        ],
        out_specs=pl.BlockSpec((tm, d), lambda i, te, nu: (i, 0)),
        scratch_shapes=[pltpu.VMEM((d, f2), BF16), pltpu.VMEM((fdim, d), BF16)],
    )
    return pl.pallas_call(
        _moe_ffn_kernel,
        grid_spec=grid_spec,
        out_shape=jax.ShapeDtypeStruct((n_pad, d), F32),
        compiler_params=_cparams(("arbitrary",)),
        name="moe_ffn",
    )(tile_expert, n_used, xs, w_gate_up, w_down)


def _moe_combine_kernel(pos_ref, x_ref, gate_ref, r_ref, y_hbm, o_ref, ybuf, sem):
    tm = x_ref.shape[0]
    _issue_row_copies(2 * tm, lambda r: pltpu.make_async_copy(
        y_hbm.at[pl.ds(pos_ref[0, 0, r], 1)], ybuf.at[pl.ds(r, 1)], sem))
    pltpu.make_async_copy(y_hbm.at[pl.ds(0, 2 * tm)], ybuf, sem).wait()
    route = r_ref[...]
    y = route[:, 2:3] * ybuf[0:tm, :] + route[:, 3:4] * ybuf[tm:2 * tm, :]
    o_ref[...] = x_ref[...] + gate_ref[...] * y


def moe_combine(pos_tiles, xa, gate, route, y_sorted, tm, seq, nb):
    ta, d = xa.shape
    return pl.pallas_call(
        _moe_combine_kernel,
        grid=(ta // tm,),
        in_specs=[
            pl.BlockSpec((1, 1, 2 * tm), lambda i: (i, 0, 0), memory_space=pltpu.SMEM),
            pl.BlockSpec((tm, d), lambda i: (i, 0)),
            pl.BlockSpec((None, 1, d), _mod_row_map(tm, seq, nb)),
            pl.BlockSpec((tm, LANES), lambda i: (i, 0)),
            pl.BlockSpec(memory_space=pl.ANY),
        ],
        out_specs=pl.BlockSpec((tm, d), lambda i: (i, 0)),
        out_shape=jax.ShapeDtypeStruct((ta, d), F32),
        scratch_shapes=[pltpu.VMEM((2 * tm, d), F32), pltpu.SemaphoreType.DMA(())],
        compiler_params=_cparams(("arbitrary",)),
        name="moe_combine",
    )(pos_tiles, xa, gate, route, y_sorted)


def moe_slots(route, counts, tm_ffn, tm_tok):
    ta = route.shape[0]
    ids = route[:, 0:TOP_K].astype(jnp.int32)
    rank = route[:, 2 * TOP_K:3 * TOP_K].astype(jnp.int32)
    counts = counts[0, N_GROUPS:N_GROUPS + N_EXPERTS].astype(jnp.int32)
    padded = ((counts + tm_ffn - 1) // tm_ffn) * tm_ffn
    ends = jnp.cumsum(padded)
    starts = ends - padded
    experts = jnp.arange(N_EXPERTS, dtype=jnp.int32)
    pos = jnp.sum(jnp.where(ids[..., None] == experts, starts, 0), axis=-1) + rank
    n_tiles = (TOP_K * ta + N_EXPERTS * (tm_ffn - 1)) // tm_ffn
    tile_start = jnp.arange(n_tiles, dtype=jnp.int32) * tm_ffn
    tile_expert = jnp.sum((tile_start[:, None] >= ends[None, :]).astype(jnp.int32), axis=1)
    tile_expert = jnp.minimum(tile_expert, N_EXPERTS - 1)
    n_used = (ends[-1] // tm_ffn).astype(jnp.int32).reshape(1)
    pos_tiles = pos.reshape(ta // tm_tok, tm_tok, TOP_K).transpose(0, 2, 1).reshape(ta // tm_tok, 1, TOP_K * tm_tok)
    return tile_expert, n_used, n_tiles * tm_ffn, pos_tiles


def _seq_flags(t_lat, seq, tc, cseq, tm):
    starts = np.arange(0, t_lat + tc, tm)
    first = np.where(starts < t_lat, starts % seq == 0, (starts - t_lat) % cseq == 0)
    ends = starts + tm
    last = np.where(starts < t_lat, ends % seq == 0, (ends - t_lat) % cseq == 0)
    return jnp.asarray(first.astype(np.int32)), jnp.asarray(last.astype(np.int32))


def kernel(x, c, ctx, c_ctx, w_ada, b_ada, norm_mix, norm_ffn, ev_w_in, ev_q_gain, ev_k_gain, ev_decay_f,
           ev_decay_b, ev_w_out, od_w_in, od_conv, od_a_log_f, od_a_log_b, od_dt_bias_f, od_dt_bias_b,
           od_out_gain, od_w_out, moe_w_group, moe_b_group, moe_w_expert, moe_b_expert, moe_w_gate_up,
           moe_w_down, final_norm_gain):
    nb, seq, d = x.shape
    cseq = ctx.shape[1]
    depth = w_ada.shape[0]
    t_lat = nb * seq
    tc = nb * cseq
    assert nb + 1 <= 8 and seq % cseq == 0 and cseq % RET_CHUNK == 0 and seq % GRID_W == 0

    tm = 512 if tc % 512 == 0 else cseq
    tq = min(256, cseq)
    tk = min(256, cseq)
    tl = 2 * DN_CHUNK
    tm_ffn = 512
    tm_comb = tm

    xa = jnp.concatenate([x.reshape(t_lat, d), ctx.reshape(tc, d)], axis=0)
    c8 = jnp.zeros((8, d), F32).at[:nb].set(c).at[nb].set(c_ctx)
    mod = adaln(c8, w_ada, b_ada)

    tabs = rope_tables(seq, tm)
    first_flags, last_flags = _seq_flags(t_lat, seq, tc, cseq, tm)
    ret_zero = jnp.zeros((nb, RET_HEADS, RET_DK, RET_DV), F32)
    dn_zero = jnp.zeros((nb, DN_HEADS, DN_DK, DN_DV), F32)

    xs = None
    for layer in range(depth):
        m = mod[layer].reshape(8, 6, 1, d)
        sh1, sc1, g1, sh2, sc2, g2 = (m[:, j] for j in range(6))
        i = layer // 2
        if layer % 2 == 0:
            w_in = ev_w_in[i].astype(BF16)
            rq, rk, p, aq, ak, av = inproj_even(xa, norm_mix[layer], sh1, sc1, w_in, tabs, ev_q_gain[i],
                                                ev_k_gain[i], tm, seq, nb)
            dec = jnp.stack([ev_decay_f[i], ev_decay_b[i]]).astype(F32)
            oc, scf, scb = retention(dec, rq, rk, p, ret_zero, ret_zero, nb, cseq, t_lat // cseq)
            ol, _, _ = retention(dec, rq, rk, p, scf, scb, nb, seq, 0)
            kcat = jnp.concatenate([ak[:, :t_lat].reshape(ATT_KV_HEADS, nb, seq, ATT_HD),
                                    ak[:, t_lat:].reshape(ATT_KV_HEADS, nb, cseq, ATT_HD)], axis=2)
            vcat = jnp.concatenate([av[:, :t_lat].reshape(ATT_KV_HEADS, nb, seq, ATT_HD),
                                    av[:, t_lat:].reshape(ATT_KV_HEADS, nb, cseq, ATT_HD)], axis=2)
            lk = seq + cseq
            vtcat = jnp.concatenate([vcat.transpose(0, 1, 3, 2),
                                     jnp.ones((ATT_KV_HEADS, nb, ATT_VT_ROWS - ATT_HD, lk), BF16)], axis=2)
            vtcat = vtcat.reshape(ATT_KV_HEADS, nb, ATT_VT_ROWS, lk // tk, tk).transpose(0, 1, 3, 2, 4)
            aqt = aq.transpose(0, 2, 1)
            att_l = attention(aqt, kcat, vtcat, seq, 0, 0, tq, tk)
            att_c = attention(aqt, kcat, vtcat, cseq, t_lat // tq, seq // tk, tq, tk)
            w_out = ev_w_out[i].astype(BF16)
            k1 = RET_HEADS * RET_DV
            xa = outproj_even(ol, oc, att_l, att_c, w_out[:k1], w_out[k1:], xa, g1, tm, seq, nb)
        else:
            w_in = jnp.pad(od_w_in[i], ((0, 0), (0, ODD_IN_PAD - ODD_IN))).astype(BF16)
            zpad = jnp.zeros((LANES - 2 * DN_HEADS,), F32)
            arow = jnp.concatenate([od_a_log_f[i], od_a_log_b[i], zpad]).reshape(1, LANES).astype(F32)
            brow = jnp.concatenate([od_dt_bias_f[i], od_dt_bias_b[i], zpad]).reshape(1, LANES).astype(F32)
            q, k, v, p, gb = inproj_odd(xa, norm_mix[layer], sh1, sc1, w_in, od_conv[i].astype(F32), arow, brow,
                                        first_flags, last_flags, tm, seq, nb, cseq)
            ta = t_lat + tc
            gbt = gb.reshape(ta // DN_CHUNK, DN_CHUNK, LANES)[:, :, :4 * DN_HEADS].transpose(0, 2, 1)
            oc_f, oc_b, sc_f, sc_b = deltanet_bidir(q, k, v, gb, gbt, dn_zero, dn_zero, nb, cseq, t_lat, tl)
            ol_f, ol_b, _, _ = deltanet_bidir(q, k, v, gb, gbt, sc_f, sc_b, nb, seq, 0, tl)
            xa = outproj_odd(ol_f, oc_f, ol_b, oc_b, p, od_out_gain[i], od_w_out[i].astype(BF16), xa, g1, tm, seq,
                             nb)

        w_router = jnp.pad(jnp.concatenate([moe_w_group[layer], moe_w_expert[layer]], axis=1),
                           ((0, 0), (0, LANES - N_GROUPS - N_EXPERTS))).astype(BF16)
        b_router = jnp.pad(jnp.concatenate([moe_b_group[layer], moe_b_expert[layer]]),
                           (0, LANES - N_GROUPS - N_EXPERTS)).reshape(1, LANES).astype(F32)
        f, route, counts = moe_router(xa, norm_ffn[layer], sh2, sc2, w_router, b_router, tm, seq, nb)
        tile_expert, n_used, n_pad, pos_tiles = moe_slots(route, counts, tm_ffn, tm_comb)
        xs = moe_scatter(pos_tiles, f, jnp.zeros((n_pad, d), F32) if xs is None else xs, tm_comb)
        y_sorted = moe_ffn(tile_expert, n_used, xs, moe_w_gate_up, moe_w_down, layer, tm_ffn)
        xa = moe_combine(pos_tiles, xa, g2, route, y_sorted, tm_comb, seq, nb)

    out = final_norm(xa, final_norm_gain, t_lat, tm)
    return out.reshape(nb, seq, d)
```

```python
import functools
import math

import numpy as np
import jax
import jax.numpy as jnp
from jax import lax
from jax.experimental import pallas as pl
from jax.experimental.pallas import tpu as pltpu

F32 = jnp.float32
BF16 = jnp.bfloat16
U32 = jnp.uint32
HIGHEST = lax.Precision.HIGHEST

EPS = 1e-6
GRID_W = 64
ROPE_BASE = 10000.0
RET_HEADS, RET_DK, RET_DV, RET_CHUNK = 8, 64, 128, 128
ATT_HEADS, ATT_KV_HEADS, ATT_HD = 8, 2, 64
DN_HEADS, DN_DK, DN_DV, DN_CHUNK, DN_CONV = 8, 128, 128, 64, 3
N_GROUPS, EXPERTS_PER_GROUP, TOP_K = 4, 8, 2
N_EXPERTS = N_GROUPS * EXPERTS_PER_GROUP

EVEN_IN = 2 * RET_HEADS * RET_DK + 2 * RET_HEADS * RET_DV + (ATT_HEADS + 2 * ATT_KV_HEADS) * ATT_HD
EVEN_ATT_COL = 2 * RET_HEADS * RET_DK + 2 * RET_HEADS * RET_DV
DN_QKV = 2 * DN_HEADS * DN_DK + DN_HEADS * DN_DV
ODD_IN = DN_QKV + DN_HEADS * DN_DV + 4 * DN_HEADS
ODD_IN_PAD = ((ODD_IN + 127) // 128) * 128

LANES = 128
VMEM_LIMIT = 56 * 1024 * 1024

NT_DIMS = (((1,), (1,)), ((), ()))
TN_DIMS = (((0,), (0,)), ((), ()))


def _cparams(sem):
    return pltpu.CompilerParams(dimension_semantics=sem, vmem_limit_bytes=VMEM_LIMIT)


def _silu(x):
    return x / (1.0 + jnp.exp(-x))


def _dot(a, b):
    return jnp.dot(a, b, preferred_element_type=F32)


def _adaln_kernel(c_ref, w_ref, b_ref, o_ref):
    s = _silu(c_ref[...])
    o_ref[...] = _dot(s.astype(BF16), w_ref[...].astype(BF16)) + b_ref[...]


def adaln(c8, w_ada, b_ada):
    depth, d, n6 = w_ada.shape
    tn = min(n6, 1536)
    return pl.pallas_call(
        _adaln_kernel,
        grid=(depth, n6 // tn),
        in_specs=[
            pl.BlockSpec((8, d), lambda l, j: (0, 0)),
            pl.BlockSpec((None, d, tn), lambda l, j: (l, 0, j)),
            pl.BlockSpec((None, 1, tn), lambda l, j: (l, 0, j)),
        ],
        out_specs=pl.BlockSpec((None, 8, tn), lambda l, j: (l, 0, j)),
        out_shape=jax.ShapeDtypeStruct((depth, 8, n6), F32),
        compiler_params=_cparams(("arbitrary", "arbitrary")),
        name="adaln",
    )(c8, w_ada, b_ada.reshape(depth, 1, n6))


def _norm_mod(x, gain, shift, scale):
    ms = jnp.mean(x * x, axis=-1, keepdims=True)
    h = x * lax.rsqrt(ms + EPS) * gain
    return h * (1.0 + scale) + shift


def _pack_bf16_pairs(x):
    n = x.shape[1] // 2
    hi = lax.bitcast_convert_type(x[:, :n].astype(BF16).astype(F32), U32)
    lo = lax.bitcast_convert_type(x[:, n:].astype(BF16).astype(F32), U32)
    return hi | (lo >> 16)


def _unpack_bf16_pairs(p):
    hi = lax.bitcast_convert_type(p & jnp.uint32(0xFFFF0000), F32)
    lo = lax.bitcast_convert_type(p << 16, F32)
    return hi, lo


def _mod_row_map(tm, seq, n_lat_batches):
    return lambda i: (jnp.minimum((i * tm) // seq, n_lat_batches), 0, 0)


def _final_norm_kernel(x_ref, g_ref, o_ref):
    x = x_ref[...]
    ms = jnp.mean(x * x, axis=-1, keepdims=True)
    o_ref[...] = x * lax.rsqrt(ms + EPS) * g_ref[...]


def final_norm(xa, gain, t_rows, tm):
    d = xa.shape[1]
    return pl.pallas_call(
        _final_norm_kernel,
        grid=(t_rows // tm,),
        in_specs=[pl.BlockSpec((tm, d), lambda i: (i, 0)), pl.BlockSpec((1, d), lambda i: (0, 0))],
        out_specs=pl.BlockSpec((tm, d), lambda i: (i, 0)),
        out_shape=jax.ShapeDtypeStruct((t_rows, d), F32),
        compiler_params=_cparams(("arbitrary",)),
        name="final_norm",
    )(xa, gain.reshape(1, d))


def _inproj_even_kernel(x_ref, g_ref, sh_ref, sc_ref, w_ref, cos_ref, s1_ref, s2_ref, qg_ref, kg_ref, bd_ref,
                        rq_ref, rk_ref, vg_ref, aq_ref, ak_ref, av_ref):
    hb = _norm_mod(x_ref[...], g_ref[...], sh_ref[...], sc_ref[...]).astype(BF16)
    cos = cos_ref[...]
    s1 = s1_ref[...]
    s2 = s2_ref[...]
    bd = bd_ref[...]
    half = ATT_HD

    def proj(c0, width=LANES):
        return _dot(hb, w_ref[:, c0:c0 + width])

    def rope(x):
        return x * cos + pltpu.roll(x, LANES - 16, 1) * s1 + pltpu.roll(x, 16, 1) * s2

    def head_norm(x, gain):
        sq = x * x
        hi = sq.astype(BF16)
        lo = (sq - hi.astype(F32)).astype(BF16)
        ms = _dot(hi, bd) + _dot(lo, bd)
        return x * lax.rsqrt(ms + EPS) * gain

    wide = 2 * LANES
    qw = RET_HEADS * RET_DK
    for c0 in range(0, qw, wide):
        yq = proj(c0, wide)
        yk = proj(qw + c0, wide)
        for u in range(2):
            cs = slice(c0 + u * LANES, c0 + (u + 1) * LANES)
            us = slice(u * LANES, (u + 1) * LANES)
            rq_ref[:, cs] = rope(yq[:, us]).astype(BF16)
            rk_ref[:, cs] = (rope(yk[:, us]) * RET_DK ** -0.5).astype(BF16)
    vgw = 2 * RET_HEADS * RET_DV
    for c0 in range(0, vgw, 512):
        vg_ref[:, c0:c0 + 512] = proj(2 * qw + c0, 512).astype(BF16)

    qg = qg_ref[...]
    kg = kg_ref[...]
    a0 = EVEN_ATT_COL
    for c0 in range(0, ATT_HEADS * ATT_HD, wide):
        ya = proj(a0 + c0, wide)
        for u in range(2):
            y = rope(head_norm(ya[:, u * LANES:(u + 1) * LANES], qg)) * (ATT_HD ** -0.5 * math.log2(math.e))
            y = y.astype(BF16)
            hd0 = (c0 + u * LANES) // ATT_HD
            aq_ref[hd0] = y[:, :half]
            aq_ref[hd0 + 1] = y[:, half:]
    ykv = proj(a0 + ATT_HEADS * ATT_HD, wide)
    y = rope(head_norm(ykv[:, :LANES], kg)).astype(BF16)
    ak_ref[0] = y[:, :half]
    ak_ref[1] = y[:, half:]
    v = ykv[:, LANES:].astype(BF16)
    av_ref[0] = v[:, :half]
    av_ref[1] = v[:, half:]


def inproj_even(xa, gain, shift, scale, w, tabs, q_gain, k_gain, tm, seq, nb):
    ta, d = xa.shape
    t_lat = nb * seq
    cos_t, s1_t, s2_t = tabs
    n_tab = seq // tm

    def tab_map(i):
        r = i * tm
        return (jnp.where(r < t_lat, (r % seq) // tm, n_tab), 0)

    ii = np.arange(LANES)
    bd = jnp.asarray((ii[:, None] // ATT_HD == ii[None, :] // ATT_HD).astype(np.float32) / ATT_HD).astype(BF16)
    qg = jnp.tile(q_gain.astype(F32), LANES // ATT_HD).reshape(1, LANES)
    kg = jnp.tile(k_gain.astype(F32), LANES // ATT_HD).reshape(1, LANES)
    mrow = _mod_row_map(tm, seq, nb)
    tab_spec = pl.BlockSpec((tm, LANES), tab_map)
    one = lambda i: (0, 0)
    qw = RET_HEADS * RET_DK
    vgw = 2 * RET_HEADS * RET_DV
    return pl.pallas_call(
        _inproj_even_kernel,
        grid=(ta // tm,),
        in_specs=[
            pl.BlockSpec((tm, d), lambda i: (i, 0)),
            pl.BlockSpec((1, d), one),
            pl.BlockSpec((None, 1, d), mrow),
            pl.BlockSpec((None, 1, d), mrow),
            pl.BlockSpec((d, EVEN_IN), one),
            tab_spec, tab_spec, tab_spec,
            pl.BlockSpec((1, LANES), one), pl.BlockSpec((1, LANES), one),
            pl.BlockSpec((LANES, LANES), one),
        ],
        out_specs=[
            pl.BlockSpec((tm, qw), lambda i: (i, 0)),
            pl.BlockSpec((tm, qw), lambda i: (i, 0)),
            pl.BlockSpec((tm, vgw), lambda i: (i, 0)),
            pl.BlockSpec((ATT_HEADS, tm, ATT_HD), lambda i: (0, i, 0)),
            pl.BlockSpec((ATT_KV_HEADS, tm, ATT_HD), lambda i: (0, i, 0)),
            pl.BlockSpec((ATT_KV_HEADS, tm, ATT_HD), lambda i: (0, i, 0)),
        ],
        out_shape=[
            jax.ShapeDtypeStruct((ta, qw), BF16),
            jax.ShapeDtypeStruct((ta, qw), BF16),
            jax.ShapeDtypeStruct((ta, vgw), BF16),
            jax.ShapeDtypeStruct((ATT_HEADS, ta, ATT_HD), BF16),
            jax.ShapeDtypeStruct((ATT_KV_HEADS, ta, ATT_HD), BF16),
            jax.ShapeDtypeStruct((ATT_KV_HEADS, ta, ATT_HD), BF16),
        ],
        compiler_params=_cparams(("arbitrary",)),
        name="inproj_even",
    )(xa, gain.reshape(1, d), shift, scale, w, cos_t, s1_t, s2_t, qg, kg, bd)


def rope_tables(seq, tm):
    nf = ATT_HD // 4
    t = jnp.arange(seq)
    rows = (t // GRID_W).astype(F32)
    cols = (t % GRID_W).astype(F32)
    inv = ROPE_BASE ** (-jnp.arange(nf, dtype=F32) / nf)
    lane = np.arange(LANES)
    axis = (lane % ATT_HD) // (ATT_HD // 2)
    f = lane % nf
    upper = ((lane % (ATT_HD // 2)) >= nf)
    pos = jnp.where(jnp.asarray(axis)[None, :] == 0, rows[:, None], cols[:, None])
    ang = pos * inv[jnp.asarray(f)][None, :]
    cos = jnp.cos(ang)
    sin = jnp.sin(ang)
    s1 = jnp.where(jnp.asarray(upper)[None, :], 0.0, -sin)
    s2 = jnp.where(jnp.asarray(upper)[None, :], sin, 0.0)
    pad1 = jnp.ones((tm, LANES), F32)
    pad0 = jnp.zeros((tm, LANES), F32)
    return (jnp.concatenate([cos, pad1]), jnp.concatenate([s1, pad0]), jnp.concatenate([s2, pad0]))


def _retention_kernel(dec_ref, q_ref, k_ref, v_ref, g_ref, s0f_ref, s0b_ref,
                      o_ref, sff_ref, sfb_ref, st_ref, *, n_chunks, unroll):
    hp = pl.program_id(1)
    C = RET_CHUNK
    dk, dv = RET_DK, RET_DV
    pos = lax.broadcasted_iota(jnp.int32, (C, dk), 0).astype(F32)
    ii = lax.broadcasted_iota(jnp.int32, (C, C), 0)
    jj = lax.broadcasted_iota(jnp.int32, (C, C), 1)
    dpos = (ii - jj).astype(F32)
    heads = range(2)
    qs = [slice(hh * dk, (hh + 1) * dk) for hh in heads]
    vs = [slice(hh * dv, (hh + 1) * dv) for hh in heads]
    w_out, w_in, gcf, gcb, mask = [], [], [], [], []
    for hh in heads:
        h = 2 * hp + hh
        df = dec_ref[0, h]
        db = dec_ref[1, h]
        lf = -jnp.exp(jnp.full((C, C), df, F32))
        lb = -jnp.exp(jnp.full((C, C), db, F32))
        lfk = -jnp.exp(jnp.full((C, dk), df, F32))
        lbk = -jnp.exp(jnp.full((C, dk), db, F32))
        w_out.append(jnp.concatenate([jnp.exp(lfk * (C - 1.0 - pos)), jnp.exp(lbk * pos)], axis=1))
        w_in.append(jnp.concatenate([jnp.exp(lfk * (pos + 1.0)), jnp.exp(lbk * (C - pos))], axis=1))
        gcf.append(jnp.exp(-jnp.exp(jnp.full((dk, dv), df, F32)) * C))
        gcb.append(jnp.exp(-jnp.exp(jnp.full((dk, dv), db, F32)) * C))
        mask.append(jnp.where(dpos > 0, jnp.exp(lf * jnp.maximum(dpos, 0.0)),
                              jnp.where(dpos < 0, jnp.exp(lb * jnp.maximum(-dpos, 0.0)), 2.0)))

    def rows(n):
        return pl.ds(pl.multiple_of(n * C, C), C)

    items = [(u, hh) for u in range(unroll) for hh in heads]

    def sums_body(i, carry):
        kk = {}
        for u, hh in items:
            k = k_ref[rows(i * unroll + u), qs[hh]].astype(F32)
            kk[(u, hh)] = (jnp.concatenate([k, k], axis=1) * w_out[hh]).astype(BF16)
        kv = {(u, hh): lax.dot_general(kk[(u, hh)], v_ref[rows(i * unroll + u), vs[hh]], TN_DIMS,
                                       preferred_element_type=F32) for u, hh in items}
        for u, hh in items:
            st_ref[hh, i * unroll + u] = kv[(u, hh)]
        return carry

    lax.fori_loop(0, n_chunks // unroll, sums_body, 0)

    def scan_body(n, carry):
        n_rev = n_chunks - 1 - n
        out = []
        for hh in heads:
            sf, sb = carry[2 * hh], carry[2 * hh + 1]
            kvf = st_ref[hh, n, 0:dk, :]
            kvb = st_ref[hh, n_rev, dk:2 * dk, :]
            st_ref[hh, n, 0:dk, :] = sf
            st_ref[hh, n_rev, dk:2 * dk, :] = sb
            out += [gcf[hh] * sf + kvf, gcb[hh] * sb + kvb]
        return tuple(out)

    init = tuple(x for hh in heads for x in (s0f_ref[hh], s0b_ref[hh]))
    fin = lax.fori_loop(0, n_chunks, scan_body, init)
    for hh in heads:
        sff_ref[hh] = fin[2 * hh]
        sfb_ref[hh] = fin[2 * hh + 1]

    def out_body(i, carry):
        ns = [i * unroll + u for u in range(unroll)]
        qb = {(u, hh): q_ref[rows(ns[u]), qs[hh]] for u, hh in items}
        sc = {(u, hh): lax.dot_general(qb[(u, hh)], k_ref[rows(ns[u]), qs[hh]], NT_DIMS,
                                       preferred_element_type=F32) for u, hh in items}
        qw = {}
        for it in items:
            q = qb[it].astype(F32)
            qw[it] = (jnp.concatenate([q, q], axis=1) * w_in[it[1]]).astype(BF16)
        o1 = {(u, hh): _dot((sc[(u, hh)] * mask[hh]).astype(BF16), v_ref[rows(ns[u]), vs[hh]]) for u, hh in items}
        o2 = {(u, hh): _dot(qw[(u, hh)], st_ref[hh, ns[u]].astype(BF16)) for u, hh in items}
        for it in items:
            u, hh = it
            n = ns[u]
            o = o1[it] + o2[it]
            o = o * lax.rsqrt(jnp.mean(o * o, axis=-1, keepdims=True) + EPS)
            gate = g_ref[rows(n), vs[hh]].astype(F32)
            o_ref[rows(n), vs[hh]] = (_silu(gate) * o).astype(o_ref.dtype)
        return carry

    lax.fori_loop(0, n_chunks // unroll, out_body, 0)


def retention(dec, rq, rk, p, s0f, s0b, nb, seq, row_off_blocks):
    n_chunks = seq // RET_CHUNK
    hp_n = RET_HEADS // 2
    vcol = 0
    gcol = vcol + RET_HEADS * RET_DV // (2 * RET_DV)
    ta = rq.shape[0]
    st_spec = pl.BlockSpec((None, 2, RET_DK, RET_DV), lambda b, hp, *_: (b, hp, 0, 0))
    grid_spec = pltpu.PrefetchScalarGridSpec(
        num_scalar_prefetch=1,
        grid=(nb, hp_n),
        in_specs=[
            pl.BlockSpec((seq, 2 * RET_DK), lambda b, hp, *_: (row_off_blocks + b, hp)),
            pl.BlockSpec((seq, 2 * RET_DK), lambda b, hp, *_: (row_off_blocks + b, hp)),
            pl.BlockSpec((seq, 2 * RET_DV), lambda b, hp, *_: (row_off_blocks + b, vcol + hp)),
            pl.BlockSpec((seq, 2 * RET_DV), lambda b, hp, *_: (row_off_blocks + b, gcol + hp)),
            st_spec, st_spec,
        ],
        out_specs=[
            pl.BlockSpec((seq, 2 * RET_DV), lambda b, hp, *_: (b, hp)),
            st_spec, st_spec,
        ],
        scratch_shapes=[pltpu.VMEM((2, n_chunks, 2 * RET_DK, RET_DV), F32)],
    )
    st_shape = jax.ShapeDtypeStruct((nb, RET_HEADS, RET_DK, RET_DV), F32)
    return pl.pallas_call(
        functools.partial(_retention_kernel, n_chunks=n_chunks, unroll=math.gcd(n_chunks, 4)),
        grid_spec=grid_spec,
        out_shape=[jax.ShapeDtypeStruct((nb * seq, RET_HEADS * RET_DV), BF16), st_shape, st_shape],
        compiler_params=_cparams(("arbitrary", "arbitrary")),
        name="retention",
    )(dec, rq, rk, p, p, s0f, s0b)


ATT_VT_ROWS = ATT_HD + 16


ATT_PAIRS_PER_TRIP = 4


def _attn_kernel(q_ref, k_ref, vt_ref, o_ref, *s_refs, tk, c_start, c_end, rep):
    tq = q_ref.shape[2]
    sets = (s_refs[:rep], s_refs[rep:])
    last = c_end - 1

    def scores(bufs, j):
        j = jnp.minimum(j, last)
        c0 = pl.multiple_of(j * tk, tk)
        k = k_ref[pl.ds(c0, tk), :]
        mxs = []
        for r in range(rep):
            s = _dot(k, q_ref[r])
            bufs[r][...] = s
            mxs.append(jnp.max(s, axis=0, keepdims=True))
        return tuple(mxs)

    def softmax_pv(bufs, j, mxs, ms, accs):
        vt = vt_ref[j]
        new_m, new_acc = [], []
        for r in range(rep):
            m_new = jnp.maximum(ms[r], mxs[r])
            a = jnp.exp2(ms[r] - m_new)
            p = jnp.exp2(bufs[r][...] - m_new).astype(BF16)
            new_acc.append(a * accs[r] + _dot(vt, p))
            new_m.append(m_new)
        return tuple(new_m), tuple(new_acc)

    def pair(j, mx0, ms, accs):
        mx1 = scores(sets[1], j + 1)
        ms, accs = softmax_pv(sets[0], j, mx0, ms, accs)
        mx0 = scores(sets[0], j + 2)
        ms, accs = softmax_pv(sets[1], j + 1, mx1, ms, accs)
        return mx0, ms, accs

    def trip(t, carry):
        for u in range(ATT_PAIRS_PER_TRIP):
            carry = pair(c_start + 2 * (ATT_PAIRS_PER_TRIP * t + u), *carry)
        return carry

    n_pairs = (c_end - c_start) // 2
    n_trips = n_pairs // ATT_PAIRS_PER_TRIP
    ms = tuple(jnp.full((1, tq), -1e30, F32) for _ in range(rep))
    accs = tuple(jnp.zeros((ATT_VT_ROWS, tq), F32) for _ in range(rep))
    carry = (scores(sets[0], c_start), ms, accs)
    if n_trips:
        carry = lax.fori_loop(0, n_trips, trip, carry)
    for u in range(n_trips * ATT_PAIRS_PER_TRIP, n_pairs):
        carry = pair(c_start + 2 * u, *carry)
    mx0, ms, accs = carry
    if (c_end - c_start) % 2:
        ms, accs = softmax_pv(sets[0], last, mx0, ms, accs)
    outs = [(acc[:ATT_HD, :] / acc[ATT_HD:ATT_HD + 1, :]).T for acc in accs]
    o_ref[...] = jnp.concatenate(outs, axis=-1).astype(o_ref.dtype)


def attention(aq, kcat, vtcat, seq_q, q_off_blocks, c_start, tq, tk):
    rep = ATT_HEADS // ATT_KV_HEADS
    _, nb, lk, _ = kcat.shape
    nq = seq_q // tq
    n_chunks = lk // tk
    return pl.pallas_call(
        functools.partial(_attn_kernel, tk=tk, c_start=c_start, c_end=n_chunks, rep=rep),
        grid=(nb, ATT_KV_HEADS, nq),
        in_specs=[
            pl.BlockSpec((rep, ATT_HD, tq), lambda b, g, i: (g, 0, q_off_blocks + b * nq + i)),
            pl.BlockSpec((None, None, lk, ATT_HD), lambda b, g, i: (g, b, 0, 0)),
            pl.BlockSpec((None, None, n_chunks, ATT_VT_ROWS, tk), lambda b, g, i: (g, b, 0, 0, 0)),
        ],
        out_specs=pl.BlockSpec((tq, rep * ATT_HD), lambda b, g, i: (b * nq + i, g)),
        out_shape=jax.ShapeDtypeStruct((nb * seq_q, ATT_HEADS * ATT_HD), BF16),
        scratch_shapes=[pltpu.VMEM((tk, tq), F32) for _ in range(2 * rep)],
        compiler_params=_cparams(("arbitrary", "arbitrary", "arbitrary")),
        name="attention",
    )(aq, kcat, vtcat)


def _lat_ctx_specs(tm, width, n_lat_tiles):
    return [pl.BlockSpec((tm, width), lambda i: (jnp.minimum(i, n_lat_tiles - 1), 0)),
            pl.BlockSpec((tm, width), lambda i: (jnp.maximum(i - n_lat_tiles, 0), 0))]


def _outproj_even_kernel(r_lat, r_ctx, a_lat, a_ctx, w1_ref, w2_ref, res_ref, gate_ref, o_ref, *, n_lat_tiles):
    is_lat = pl.program_id(0) < n_lat_tiles
    a1 = jnp.where(is_lat, r_lat[...], r_ctx[...])
    a2 = jnp.where(is_lat, a_lat[...], a_ctx[...])
    y = _dot(a1, w1_ref[...]) + _dot(a2, w2_ref[...])
    o_ref[...] = res_ref[...] + gate_ref[...] * y


def outproj_even(ret_lat, ret_ctx, att_lat, att_ctx, w1, w2, xa, gate, tm, seq, nb):
    ta, d = xa.shape
    k1, k2 = w1.shape[0], w2.shape[0]
    n_lat_tiles = ret_lat.shape[0] // tm
    return pl.pallas_call(
        functools.partial(_outproj_even_kernel, n_lat_tiles=n_lat_tiles),
        grid=(ta // tm,),
        in_specs=_lat_ctx_specs(tm, k1, n_lat_tiles) + _lat_ctx_specs(tm, k2, n_lat_tiles) + [
            pl.BlockSpec((k1, d), lambda i: (0, 0)),
            pl.BlockSpec((k2, d), lambda i: (0, 0)),
            pl.BlockSpec((tm, d), lambda i: (i, 0)),
            pl.BlockSpec((None, 1, d), _mod_row_map(tm, seq, nb)),
        ],
        out_specs=pl.BlockSpec((tm, d), lambda i: (i, 0)),
        out_shape=jax.ShapeDtypeStruct((ta, d), F32),
        compiler_params=_cparams(("arbitrary",)),
        name="outproj_even",
    )(ret_lat, ret_ctx, att_lat, att_ctx, w1, w2, xa, gate)


def _outproj_odd_kernel(f_lat, f_ctx, b_lat, b_ctx, z_ref, og_ref, w_ref, res_ref, gate_ref, o_ref,
                        *, n_lat_tiles):
    is_lat = pl.program_id(0) < n_lat_tiles
    og = og_ref[...]
    parts = []
    for h in range(DN_HEADS):
        cs = slice(h * DN_DV, (h + 1) * DN_DV)
        of = jnp.where(is_lat, f_lat[:, cs], f_ctx[:, cs]).astype(F32)
        ob = jnp.where(is_lat, b_lat[:, cs], b_ctx[:, cs]).astype(F32)
        o = of + ob
        o = o * lax.rsqrt(jnp.mean(o * o, axis=-1, keepdims=True) + EPS) * og
        parts.append((o * _silu(z_ref[:, cs].astype(F32))).astype(BF16))
    a = jnp.concatenate(parts, axis=-1)
    o_ref[...] = res_ref[...] + gate_ref[...] * _dot(a, w_ref[...])


def outproj_odd(of_lat, of_ctx, ob_lat, ob_ctx, p, out_gain, w, xa, gate, tm, seq, nb):
    ta, d = xa.shape
    kdim = DN_HEADS * DN_DV
    n_lat_tiles = of_lat.shape[0] // tm
    return pl.pallas_call(
        functools.partial(_outproj_odd_kernel, n_lat_tiles=n_lat_tiles),
        grid=(ta // tm,),
        in_specs=_lat_ctx_specs(tm, kdim, n_lat_tiles) + _lat_ctx_specs(tm, kdim, n_lat_tiles) + [
            pl.BlockSpec((tm, kdim), lambda i: (i, 0)),
            pl.BlockSpec((1, DN_DV), lambda i: (0, 0)),
            pl.BlockSpec((kdim, d), lambda i: (0, 0)),
            pl.BlockSpec((tm, d), lambda i: (i, 0)),
            pl.BlockSpec((None, 1, d), _mod_row_map(tm, seq, nb)),
        ],
        out_specs=pl.BlockSpec((tm, d), lambda i: (i, 0)),
        out_shape=jax.ShapeDtypeStruct((ta, d), F32),
        compiler_params=_cparams(("arbitrary",)),
        name="outproj_odd",
    )(of_lat, of_ctx, ob_lat, ob_ctx, p, out_gain.reshape(1, DN_DV).astype(F32), w, xa, gate)


def _inproj_odd_kernel(first_ref, last_ref, x_ref, xp_ref, xn_ref, g_ref, sh_ref, sc_ref, w_ref, cw_ref,
                       arow_ref, brow_ref, q_ref, k_ref, v_ref, z_ref, gb_ref, *, ctx_tile0, cseq):
    i = pl.program_id(0)
    tm = x_ref.shape[0]
    gain, shift, scale = g_ref[...], sh_ref[...], sc_ref[...]
    hrows = xp_ref.shape[0]
    hall = _norm_mod(jnp.concatenate([x_ref[...], xp_ref[...], xn_ref[...]], axis=0), gain, shift, scale).astype(BF16)
    hb = hall[:tm]
    keep_prev = 1.0 - first_ref[i].astype(F32)
    keep_next = 1.0 - last_ref[i].astype(F32)
    row = lax.broadcasted_iota(jnp.int32, (tm, LANES), 0)
    is_first = row == 0
    is_last = row == tm - 1
    inner = cseq < tm
    if inner:
        in_ctx = i >= ctx_tile0
        local = row & (cseq - 1)
        zero_dn = jnp.logical_and(in_ctx, local == 0)
        zero_up = jnp.logical_and(in_ctx, local == cseq - 1)
    n_qk = 2 * DN_HEADS * DN_DK // LANES
    n_q = DN_HEADS * DN_DK // LANES
    outs = (q_ref, k_ref, v_ref)
    wide = 2 * LANES
    for c0 in range(0, DN_QKV, wide):
        yall = _dot(hall, w_ref[:, c0:c0 + wide])
        y2 = yall[:tm]
        yh2 = yall[tm:]
        for u in range(2):
            j = c0 // LANES + u
            us = slice(u * LANES, (u + 1) * LANES)
            x = y2[:, us]
            xp = yh2[hrows - 1:hrows, us] * keep_prev
            xn = yh2[hrows:hrows + 1, us] * keep_next
            x_dn = jnp.where(is_first, xp, pltpu.roll(x, 1, 0))
            x_up = jnp.where(is_last, xn, pltpu.roll(x, tm - 1, 0))
            if inner:
                x_dn = jnp.where(zero_dn, 0.0, x_dn)
                x_up = jnp.where(zero_up, 0.0, x_up)
            w = cw_ref[:, j * LANES:(j + 1) * LANES]
            y = _silu(x_dn * w[0:1, :] + x * w[1:2, :] + x_up * w[2:3, :])
            if j < n_qk:
                y = y * lax.rsqrt(jnp.sum(y * y, axis=-1, keepdims=True) + EPS)
                if j < n_q:
                    y = y * DN_DK ** -0.5
            lj = j % n_q
            outs[j // n_q][:, lj * LANES:(lj + 1) * LANES] = y.astype(BF16)
    zw = DN_HEADS * DN_DV
    for c0 in range(0, zw, 512):
        z_ref[:, c0:c0 + 512] = _dot(hb, w_ref[:, DN_QKV + c0:DN_QKV + c0 + 512]).astype(BF16)

    a = _dot(hb, w_ref[:, DN_QKV + zw:DN_QKV + zw + LANES])
    lane = lax.broadcasted_iota(jnp.int32, (tm, LANES), 1)
    zz = a + brow_ref[...]
    softplus = jnp.maximum(zz, 0.0) + jnp.log(1.0 + jnp.exp(-jnp.abs(zz)))
    g = -jnp.exp(arow_ref[...]) * softplus
    beta = 1.0 / (1.0 + jnp.exp(-a))
    gb_ref[...] = jnp.where(lane < 2 * DN_HEADS, g, jnp.where(lane < 4 * DN_HEADS, beta, 0.0))


def inproj_odd(xa, gain, shift, scale, w, conv_w, arow, brow, first_flags, last_flags, tm, seq, nb, cseq):
    ta, d = xa.shape
    halo = 8
    hb = tm // halo
    n_h = ta // halo
    kdim = DN_HEADS * DN_DK
    assert cseq >= tm or (tm % cseq == 0 and cseq & (cseq - 1) == 0)
    mrow = lambda i, *_: _mod_row_map(tm, seq, nb)(i)
    one = lambda i, *_: (0, 0)
    row_blk = lambda i, *_: (i, 0)
    grid_spec = pltpu.PrefetchScalarGridSpec(
        num_scalar_prefetch=2,
        grid=(ta // tm,),
        in_specs=[
            pl.BlockSpec((tm, d), row_blk),
            pl.BlockSpec((halo, d), lambda i, *_: (jnp.maximum(i * hb - 1, 0), 0)),
            pl.BlockSpec((halo, d), lambda i, *_: (jnp.minimum((i + 1) * hb, n_h - 1), 0)),
            pl.BlockSpec((1, d), one),
            pl.BlockSpec((None, 1, d), mrow),
            pl.BlockSpec((None, 1, d), mrow),
            pl.BlockSpec((d, ODD_IN_PAD), one),
            pl.BlockSpec((DN_CONV, DN_QKV), one),
            pl.BlockSpec((1, LANES), one),
            pl.BlockSpec((1, LANES), one),
        ],
        out_specs=[pl.BlockSpec((tm, kdim), row_blk)] * 4 + [pl.BlockSpec((tm, LANES), row_blk)],
    )
    return pl.pallas_call(
        functools.partial(_inproj_odd_kernel, ctx_tile0=nb * seq // tm, cseq=cseq),
        grid_spec=grid_spec,
        out_shape=[jax.ShapeDtypeStruct((ta, kdim), BF16)] * 4 + [jax.ShapeDtypeStruct((ta, LANES), F32)],
        compiler_params=_cparams(("arbitrary",)),
        name="inproj_odd",
    )(first_flags, last_flags, xa, xa, xa, gain.reshape(1, d), shift, scale, w, conv_w, arow, brow)


def _deltanet_bidir_kernel(qf_ref, kf_ref, vf_ref, gbf_ref, gbtf_ref, qb_ref, kb_ref, vb_ref, gbb_ref, gbtb_ref,
                           s0f_ref, s0b_ref, of_ref, ob_ref, sff_ref, sfb_ref, sf_scr, sb_scr, *, n_chunks):
    t = pl.program_id(1)

    @pl.when(t == 0)
    def _():
        sf_scr[...] = s0f_ref[...]
        sb_scr[...] = s0b_ref[...]

    C = DN_CHUNK
    ii = lax.broadcasted_iota(jnp.int32, (C, C), 0)
    jj = lax.broadcasted_iota(jnp.int32, (C, C), 1)
    lower, upper = ii >= jj, ii <= jj
    eye = jnp.where(ii == jj, 1.0, 0.0).astype(F32)
    blk = ii ^ jj
    dirs = (
        dict(rev=False, incl=lower, strict=ii > jj, q=qf_ref, k=kf_ref, v=vf_ref, gb=gbf_ref, gbt=gbtf_ref,
             o=of_ref, scr=sf_scr, off=0, order=list(range(n_chunks))),
        dict(rev=True, incl=upper, strict=ii < jj, q=qb_ref, k=kb_ref, v=vb_ref, gb=gbb_ref, gbt=gbtb_ref,
             o=ob_ref, scr=sb_scr, off=DN_HEADS, order=list(range(n_chunks - 1, -1, -1))),
    )
    items = [(d, c, h) for d in range(2) for c in dirs[d]["order"] for h in range(DN_HEADS)]

    gcols, grows, gbs = {}, {}, {}
    for d, dr in enumerate(dirs):
        tri = jnp.where(dr["incl"], 1.0, 0.0).astype(F32)
        tri_t = jnp.where(upper if not dr["rev"] else lower, 1.0, 0.0).astype(F32)
        for c in dr["order"]:
            gb_c = dr["gb"][c * C:(c + 1) * C, :]
            gbs[(d, c)] = gb_c
            gcols[(d, c)] = jnp.dot(tri, gb_c, preferred_element_type=F32, precision=HIGHEST)
            grows[(d, c)] = jnp.dot(dr["gbt"][c], tri_t, preferred_element_type=F32, precision=HIGHEST)

    qb, kb16, decay, kbeta, egc, kd, gl, rhs = {}, {}, {}, {}, {}, {}, {}, {}
    for it in items:
        d, c, h = it
        dr = dirs[d]
        gi = dr["off"] + h
        bi = 2 * DN_HEADS + dr["off"] + h
        rows = slice(c * C, (c + 1) * C)
        cs = slice(h * DN_DK, (h + 1) * DN_DK)
        gc = gcols[(d, c)][:, gi:gi + 1]
        gr = grows[(d, c)][gi:gi + 1, :]
        beta = gbs[(d, c)][:, bi:bi + 1]
        qb[it] = dr["q"][rows, cs]
        kb16[it] = dr["k"][rows, cs]
        kf = kb16[it].astype(F32)
        decay[it] = jnp.where(dr["incl"], jnp.exp(jnp.where(dr["incl"], gc - gr, 0.0)), 0.0)
        kbeta[it] = kf * beta
        egc[it] = jnp.exp(gc)
        glast = gc[0:1, :] if dr["rev"] else gc[C - 1:C, :]
        kd[it] = (kf * jnp.exp(glast - gc)).astype(BF16)
        gl[it] = jnp.exp(glast)
        rhs[it] = jnp.concatenate([dr["v"][rows, cs].astype(F32) * beta, kbeta[it] * egc[it]], axis=1).astype(BF16)

    kk = {it: lax.dot_general(kbeta[it].astype(BF16), kb16[it], NT_DIMS, preferred_element_type=F32)
          for it in items}
    qk = {it: lax.dot_general(qb[it], kb16[it], NT_DIMS, preferred_element_type=F32) for it in items}
    lm = {it: jnp.where(dirs[it[0]]["strict"], kk[it] * decay[it], 0.0) for it in items}
    attn = {it: jnp.where(dirs[it[0]]["incl"], qk[it] * decay[it], 0.0).astype(BF16) for it in items}
    dinv = {it: eye - jnp.where(blk < 2, lm[it], 0.0) for it in items}
    s = 2
    while s < C:
        in_band = jnp.logical_and(blk >= s, blk < 2 * s)
        tmp = {it: _dot(dinv[it].astype(BF16), jnp.where(in_band, lm[it], 0.0).astype(BF16)) for it in items}
        dinv = {it: dinv[it] - _dot(tmp[it].astype(BF16), dinv[it].astype(BF16)) for it in items}
        s *= 2
    uw = {it: _dot(dinv[it].astype(BF16), rhs[it]) for it in items}
    wq = {it: jnp.concatenate([uw[it][:, DN_DV:], qb[it].astype(F32) * egc[it]], axis=0).astype(BF16)
          for it in items}

    states = {(d, h): dirs[d]["scr"][h] for d in range(2) for h in range(DN_HEADS)}
    for step in range(n_chunks):
        its = [(d, dirs[d]["order"][step], h) for d in range(2) for h in range(DN_HEADS)]
        r = {it: _dot(wq[it], states[(it[0], it[2])].astype(BF16)) for it in its}
        v_new = {it: (uw[it][:, :DN_DV] - r[it][:C]).astype(BF16) for it in its}
        o = {it: r[it][C:] + _dot(attn[it], v_new[it]) for it in its}
        for it in its:
            key = (it[0], it[2])
            states[key] = states[key] * gl[it] + lax.dot_general(kd[it], v_new[it], TN_DIMS,
                                                                 preferred_element_type=F32)
        for it in its:
            d, c, h = it
            dirs[d]["o"][c * C:(c + 1) * C, h * DN_DK:(h + 1) * DN_DK] = o[it].astype(of_ref.dtype)
    for (d, h), st in states.items():
        dirs[d]["scr"][h] = st

    @pl.when(t == pl.num_programs(1) - 1)
    def _():
        sff_ref[...] = sf_scr[...]
        sfb_ref[...] = sb_scr[...]


def deltanet_bidir(q, k, v, gb, gbt, s0f, s0b, nb, seq, row_off, tl):
    nblk = seq // tl
    n_chunks = tl // DN_CHUNK
    off_b = row_off // tl
    kdim = DN_HEADS * DN_DK

    def fwd_rb(b, t):
        return off_b + b * nblk + t

    def bwd_rb(b, t):
        return off_b + b * nblk + (nblk - 1 - t)

    def seq_specs(rb):
        spec = pl.BlockSpec((tl, kdim), lambda b, t: (rb(b, t), 0))
        return [spec, spec, spec,
                pl.BlockSpec((tl, LANES), lambda b, t: (rb(b, t), 0)),
                pl.BlockSpec((n_chunks, 4 * DN_HEADS, DN_CHUNK), lambda b, t: (rb(b, t), 0, 0))]

    st_spec = pl.BlockSpec((None, DN_HEADS, DN_DK, DN_DV), lambda b, t: (b, 0, 0, 0))
    st_shape = jax.ShapeDtypeStruct((nb, DN_HEADS, DN_DK, DN_DV), F32)
    o_shape = jax.ShapeDtypeStruct((nb * seq, kdim), BF16)
    return pl.pallas_call(
        functools.partial(_deltanet_bidir_kernel, n_chunks=n_chunks),
        grid=(nb, nblk),
        in_specs=seq_specs(fwd_rb) + seq_specs(bwd_rb) + [st_spec, st_spec],
        out_specs=[
            pl.BlockSpec((tl, kdim), lambda b, t: (b * nblk + t, 0)),
            pl.BlockSpec((tl, kdim), lambda b, t: (b * nblk + nblk - 1 - t, 0)),
            st_spec, st_spec,
        ],
        out_shape=[o_shape, o_shape, st_shape, st_shape],
        scratch_shapes=[pltpu.VMEM((DN_HEADS, DN_DK, DN_DV), F32), pltpu.VMEM((DN_HEADS, DN_DK, DN_DV), F32)],
        compiler_params=_cparams(("arbitrary", "arbitrary")),
        name="deltanet_bidir",
    )(q, k, v, gb, gbt, q, k, v, gb, gbt, s0f, s0b)


def _router_kernel(x_ref, g_ref, sh_ref, sc_ref, wr_ref, br_ref, ltri_ref, f_ref, r_ref, cnt_ref, base_ref):
    @pl.when(pl.program_id(0) == 0)
    def _():
        base_ref[...] = jnp.zeros_like(base_ref)

    h = _norm_mod(x_ref[...], g_ref[...], sh_ref[...], sc_ref[...])
    f_ref[...] = _pack_bf16_pairs(h)
    logits = _dot(h.astype(BF16), wr_ref[...]) + br_ref[...]
    tm = logits.shape[0]
    lane = lax.broadcasted_iota(jnp.int32, (tm, LANES), 1)
    neg = -1e30
    big = 4 * LANES
    is_g = lane < N_GROUPS
    gl = jnp.where(is_g, logits, neg)
    gm = jnp.max(gl, axis=-1, keepdims=True)
    grp = jnp.min(jnp.where(gl == gm, lane, big), axis=-1, keepdims=True)
    psum = jnp.sum(jnp.where(is_g, jnp.exp(gl - gm), 0.0), axis=-1, keepdims=True)
    p_grp = 1.0 / psum
    e_lane = lane - N_GROUPS
    in_grp = jnp.logical_and(jnp.logical_and(e_lane >= 0, e_lane < N_EXPERTS),
                             (e_lane // EXPERTS_PER_GROUP) == grp)
    el = jnp.where(in_grp, logits, neg)
    m1 = jnp.max(el, axis=-1, keepdims=True)
    i1 = jnp.min(jnp.where(el == m1, lane, big), axis=-1, keepdims=True)
    el2 = jnp.where(lane == i1, neg, el)
    m2 = jnp.max(el2, axis=-1, keepdims=True)
    i2 = jnp.min(jnp.where(el2 == m2, lane, big), axis=-1, keepdims=True)
    e21 = jnp.exp(m2 - m1)
    w1 = p_grp / (1.0 + e21)
    w2 = p_grp * e21 / (1.0 + e21)
    e1 = (i1 - N_GROUPS).astype(F32)
    e2 = (i2 - N_GROUPS).astype(F32)
    oh1 = lane == i1
    oh2 = lane == i2
    oh1f = jnp.where(oh1, 1.0, 0.0)
    oh2f = jnp.where(oh2, 1.0, 0.0)
    ltri = ltri_ref[...]
    before1 = _dot(ltri, oh1f.astype(BF16))
    before2 = _dot(ltri, oh2f.astype(BF16))
    cnt1 = jnp.sum(oh1f, axis=0, keepdims=True)
    cnt2 = jnp.sum(oh2f, axis=0, keepdims=True)
    base = base_ref[0:1, :]
    rank1 = jnp.sum(jnp.where(oh1, base + before1, 0.0), axis=-1, keepdims=True)
    rank2 = jnp.sum(jnp.where(oh2, base + cnt1 + before2, 0.0), axis=-1, keepdims=True)
    total = base + cnt1 + cnt2
    base_ref[...] = jnp.broadcast_to(total, base_ref.shape)
    cnt_ref[...] = jnp.broadcast_to(total, cnt_ref.shape)
    vals = (e1, e2, w1, w2, rank1, rank2)
    out = jnp.zeros((tm, LANES), F32)
    for idx, val in enumerate(vals):
        out = jnp.where(lane == idx, val, out)
    r_ref[...] = out


def moe_router(xa, gain, shift, scale, w_router, b_router, tm, seq, nb):
    ta, d = xa.shape
    mrow = _mod_row_map(tm, seq, nb)
    ii = np.arange(tm)
    ltri = jnp.asarray((ii[:, None] > ii[None, :]).astype(np.float32)).astype(BF16)
    return pl.pallas_call(
        _router_kernel,
        grid=(ta // tm,),
        in_specs=[
            pl.BlockSpec((tm, d), lambda i: (i, 0)),
            pl.BlockSpec((1, d), lambda i: (0, 0)),
            pl.BlockSpec((None, 1, d), mrow),
            pl.BlockSpec((None, 1, d), mrow),
            pl.BlockSpec((d, LANES), lambda i: (0, 0)),
            pl.BlockSpec((1, LANES), lambda i: (0, 0)),
            pl.BlockSpec((tm, tm), lambda i: (0, 0)),
        ],
        out_specs=[pl.BlockSpec((tm, d // 2), lambda i: (i, 0)), pl.BlockSpec((tm, LANES), lambda i: (i, 0)),
                   pl.BlockSpec((8, LANES), lambda i: (0, 0))],
        out_shape=[jax.ShapeDtypeStruct((ta, d // 2), U32), jax.ShapeDtypeStruct((ta, LANES), F32),
                   jax.ShapeDtypeStruct((8, LANES), F32)],
        scratch_shapes=[pltpu.VMEM((8, LANES), F32)],
        compiler_params=_cparams(("arbitrary",)),
        name="moe_router",
    )(xa, gain.reshape(1, d), shift, scale, w_router, b_router, ltri)


ROW_DMA_UNROLL = 8


def _issue_row_copies(n_rows, make_copy):
    def trip(i, carry):
        for u in range(ROW_DMA_UNROLL):
            make_copy(i * ROW_DMA_UNROLL + u).start(priority=u % 2)
        return carry

    lax.fori_loop(0, n_rows // ROW_DMA_UNROLL, trip, 0)


def _moe_scatter_kernel(pos_ref, f_ref, xs_in, xs_out, sem):
    del xs_in
    tm = f_ref.shape[0]
    for k in range(TOP_K):
        _issue_row_copies(tm, lambda r, k=k: pltpu.make_async_copy(
            f_ref.at[pl.ds(r, 1)], xs_out.at[pl.ds(pos_ref[0, 0, k * tm + r], 1)], sem))
    for _ in range(2):
        pltpu.make_async_copy(f_ref, xs_out.at[pl.ds(0, tm)], sem).wait()


def moe_scatter(pos_tiles, f, xs_zero, tm):
    ta, d = f.shape
    return pl.pallas_call(
        _moe_scatter_kernel,
        grid=(ta // tm,),
        in_specs=[
            pl.BlockSpec((1, 1, 2 * tm), lambda i: (i, 0, 0), memory_space=pltpu.SMEM),
            pl.BlockSpec((tm, d), lambda i: (i, 0)),
            pl.BlockSpec(memory_space=pl.ANY),
        ],
        out_specs=pl.BlockSpec(memory_space=pl.ANY),
        out_shape=jax.ShapeDtypeStruct(xs_zero.shape, xs_zero.dtype),
        scratch_shapes=[pltpu.SemaphoreType.DMA(())],
        input_output_aliases={2: 0},
        compiler_params=_cparams(("arbitrary",)),
        name="moe_scatter",
    )(pos_tiles, f, xs_zero)


def _moe_ffn_kernel(te_ref, nu_ref, x_ref, wgu_ref, wd_ref, o_ref, wgu_bf, wd_bf):
    i = pl.program_id(0)
    fdim = wd_bf.shape[0]

    @pl.when(i < nu_ref[0])
    def _():
        prev = te_ref[jnp.maximum(i - 1, 0)]
        changed = jnp.logical_or(i == 0, te_ref[i] != prev)

        @pl.when(changed)
        def _():
            wgu_bf[...] = wgu_ref[...].astype(BF16)
            wd_bf[...] = wd_ref[...].astype(BF16)

        x_hi, x_lo = _unpack_bf16_pairs(x_ref[...])
        half = x_hi.shape[1]
        gu = _dot(x_hi.astype(BF16), wgu_bf[:half, :]) + _dot(x_lo.astype(BF16), wgu_bf[half:, :])
        hmid = _silu(gu[:, :fdim]) * gu[:, fdim:]
        o_ref[...] = _pack_bf16_pairs(_dot(hmid.astype(BF16), wd_bf[...]))

    @pl.when(i >= nu_ref[0])
    def _():
        o_ref[...] = jnp.zeros_like(o_ref)


def moe_ffn(tile_expert, n_used, xs, w_gate_up, w_down, layer, tm):
    n_pad, dh = xs.shape
    d = 2 * dh
    f2 = w_gate_up.shape[-1]
    fdim = w_down.shape[-2]
    grid_spec = pltpu.PrefetchScalarGridSpec(
        num_scalar_prefetch=2,
        grid=(n_pad // tm,),
        in_specs=[
            pl.BlockSpec((tm, dh), lambda i, te, nu: (i, 0)),
            pl.BlockSpec((None, None, d, f2), lambda i, te, nu: (layer, te[i], 0, 0)),
            pl.BlockSpec((None, None, fdim, d), lambda i, te, nu: (layer, te[i], 0, 0)),
        ],
        out_specs=pl.BlockSpec((tm, dh), lambda i, te, nu: (i, 0)),
        scratch_shapes=[pltpu.VMEM((d, f2), BF16), pltpu.VMEM((fdim, d), BF16)],
    )
    return pl.pallas_call(
        _moe_ffn_kernel,
        grid_spec=grid_spec,
        out_shape=jax.ShapeDtypeStruct((n_pad, dh), U32),
        compiler_params=_cparams(("arbitrary",)),
        name="moe_ffn",
    )(tile_expert, n_used, xs, w_gate_up, w_down)


def _moe_combine_kernel(pos_ref, x_ref, gate_ref, r_ref, y_hbm, o_ref, ybuf, sem):
    tm = x_ref.shape[0]
    _issue_row_copies(2 * tm, lambda r: pltpu.make_async_copy(
        y_hbm.at[pl.ds(pos_ref[0, 0, r], 1)], ybuf.at[pl.ds(r, 1)], sem))
    pltpu.make_async_copy(y_hbm.at[pl.ds(0, 2 * tm)], ybuf, sem).wait()
    route = r_ref[...]
    w0, w1 = route[:, 2:3], route[:, 3:4]
    y0_hi, y0_lo = _unpack_bf16_pairs(ybuf[0:tm, :])
    y1_hi, y1_lo = _unpack_bf16_pairs(ybuf[tm:2 * tm, :])
    y = jnp.concatenate([w0 * y0_hi + w1 * y1_hi, w0 * y0_lo + w1 * y1_lo], axis=-1)
    o_ref[...] = x_ref[...] + gate_ref[...] * y


def moe_combine(pos_tiles, xa, gate, route, y_sorted, tm, seq, nb):
    ta, d = xa.shape
    return pl.pallas_call(
        _moe_combine_kernel,
        grid=(ta // tm,),
        in_specs=[
            pl.BlockSpec((1, 1, 2 * tm), lambda i: (i, 0, 0), memory_space=pltpu.SMEM),
            pl.BlockSpec((tm, d), lambda i: (i, 0)),
            pl.BlockSpec((None, 1, d), _mod_row_map(tm, seq, nb)),
            pl.BlockSpec((tm, LANES), lambda i: (i, 0)),
            pl.BlockSpec(memory_space=pl.ANY),
        ],
        out_specs=pl.BlockSpec((tm, d), lambda i: (i, 0)),
        out_shape=jax.ShapeDtypeStruct((ta, d), F32),
        scratch_shapes=[pltpu.VMEM((2 * tm, d // 2), U32), pltpu.SemaphoreType.DMA(())],
        compiler_params=_cparams(("arbitrary",)),
        name="moe_combine",
    )(pos_tiles, xa, gate, route, y_sorted)


def moe_slots(route, counts, tm_ffn, tm_tok):
    ta = route.shape[0]
    ids = route[:, 0:TOP_K].astype(jnp.int32)
    rank = route[:, 2 * TOP_K:3 * TOP_K].astype(jnp.int32)
    counts = counts[0, N_GROUPS:N_GROUPS + N_EXPERTS].astype(jnp.int32)
    padded = ((counts + tm_ffn - 1) // tm_ffn) * tm_ffn
    ends = jnp.cumsum(padded)
    starts = ends - padded
    experts = jnp.arange(N_EXPERTS, dtype=jnp.int32)
    pos = jnp.sum(jnp.where(ids[..., None] == experts, starts, 0), axis=-1) + rank
    n_tiles = (TOP_K * ta + N_EXPERTS * (tm_ffn - 1)) // tm_ffn
    tile_start = jnp.arange(n_tiles, dtype=jnp.int32) * tm_ffn
    tile_expert = jnp.sum((tile_start[:, None] >= ends[None, :]).astype(jnp.int32), axis=1)
    tile_expert = jnp.minimum(tile_expert, N_EXPERTS - 1)
    n_used = (ends[-1] // tm_ffn).astype(jnp.int32).reshape(1)
    pos_tiles = pos.reshape(ta // tm_tok, tm_tok, TOP_K).transpose(0, 2, 1).reshape(ta // tm_tok, 1, TOP_K * tm_tok)
    return tile_expert, n_used, n_tiles * tm_ffn, pos_tiles


def _seq_flags(t_lat, seq, tc, cseq, tm):
    starts = np.arange(0, t_lat + tc, tm)
    first = np.where(starts < t_lat, starts % seq == 0, (starts - t_lat) % cseq == 0)
    ends = starts + tm
    last = np.where(starts < t_lat, ends % seq == 0, (ends - t_lat) % cseq == 0)
    return jnp.asarray(first.astype(np.int32)), jnp.asarray(last.astype(np.int32))


def kernel(x, c, ctx, c_ctx, w_ada, b_ada, norm_mix, norm_ffn, ev_w_in, ev_q_gain, ev_k_gain, ev_decay_f,
           ev_decay_b, ev_w_out, od_w_in, od_conv, od_a_log_f, od_a_log_b, od_dt_bias_f, od_dt_bias_b,
           od_out_gain, od_w_out, moe_w_group, moe_b_group, moe_w_expert, moe_b_expert, moe_w_gate_up,
           moe_w_down, final_norm_gain):
    nb, seq, d = x.shape
    cseq = ctx.shape[1]
    depth = w_ada.shape[0]
    t_lat = nb * seq
    tc = nb * cseq
    assert nb + 1 <= 8 and seq % cseq == 0 and cseq % RET_CHUNK == 0 and seq % GRID_W == 0

    tm = 512 if tc % 512 == 0 else cseq
    tq = min(256, cseq)
    tk = min(256, cseq)
    tl = 2 * DN_CHUNK
    tm_ffn = 512
    tm_comb = tm

    xa = jnp.concatenate([x.reshape(t_lat, d), ctx.reshape(tc, d)], axis=0)
    c8 = jnp.zeros((8, d), F32).at[:nb].set(c).at[nb].set(c_ctx)
    mod = adaln(c8, w_ada, b_ada)

    tabs = rope_tables(seq, tm)
    first_flags, last_flags = _seq_flags(t_lat, seq, tc, cseq, tm)
    ret_zero = jnp.zeros((nb, RET_HEADS, RET_DK, RET_DV), F32)
    dn_zero = jnp.zeros((nb, DN_HEADS, DN_DK, DN_DV), F32)

    xs = None
    for layer in range(depth):
        m = mod[layer].reshape(8, 6, 1, d)
        sh1, sc1, g1, sh2, sc2, g2 = (m[:, j] for j in range(6))
        i = layer // 2
        if layer % 2 == 0:
            w_in = ev_w_in[i].astype(BF16)
            rq, rk, p, aq, ak, av = inproj_even(xa, norm_mix[layer], sh1, sc1, w_in, tabs, ev_q_gain[i],
                                                ev_k_gain[i], tm, seq, nb)
            dec = jnp.stack([ev_decay_f[i], ev_decay_b[i]]).astype(F32)
            oc, scf, scb = retention(dec, rq, rk, p, ret_zero, ret_zero, nb, cseq, t_lat // cseq)
            ol, _, _ = retention(dec, rq, rk, p, scf, scb, nb, seq, 0)
            kcat = jnp.concatenate([ak[:, :t_lat].reshape(ATT_KV_HEADS, nb, seq, ATT_HD),
                                    ak[:, t_lat:].reshape(ATT_KV_HEADS, nb, cseq, ATT_HD)], axis=2)
            vcat = jnp.concatenate([av[:, :t_lat].reshape(ATT_KV_HEADS, nb, seq, ATT_HD),
                                    av[:, t_lat:].reshape(ATT_KV_HEADS, nb, cseq, ATT_HD)], axis=2)
            lk = seq + cseq
            vtcat = jnp.concatenate([vcat.transpose(0, 1, 3, 2),
                                     jnp.ones((ATT_KV_HEADS, nb, ATT_VT_ROWS - ATT_HD, lk), BF16)], axis=2)
            vtcat = vtcat.reshape(ATT_KV_HEADS, nb, ATT_VT_ROWS, lk // tk, tk).transpose(0, 1, 3, 2, 4)
            aqt = aq.transpose(0, 2, 1)
            att_l = attention(aqt, kcat, vtcat, seq, 0, 0, tq, tk)
            att_c = attention(aqt, kcat, vtcat, cseq, t_lat // tq, seq // tk, tq, tk)
            w_out = ev_w_out[i].astype(BF16)
            k1 = RET_HEADS * RET_DV
            xa = outproj_even(ol, oc, att_l, att_c, w_out[:k1], w_out[k1:], xa, g1, tm, seq, nb)
        else:
            w_in = jnp.pad(od_w_in[i], ((0, 0), (0, ODD_IN_PAD - ODD_IN))).astype(BF16)
            zpad = jnp.zeros((LANES - 2 * DN_HEADS,), F32)
            arow = jnp.concatenate([od_a_log_f[i], od_a_log_b[i], zpad]).reshape(1, LANES).astype(F32)
            brow = jnp.concatenate([od_dt_bias_f[i], od_dt_bias_b[i], zpad]).reshape(1, LANES).astype(F32)
            q, k, v, p, gb = inproj_odd(xa, norm_mix[layer], sh1, sc1, w_in, od_conv[i].astype(F32), arow, brow,
                                        first_flags, last_flags, tm, seq, nb, cseq)
            ta = t_lat + tc
            gbt = gb.reshape(ta // DN_CHUNK, DN_CHUNK, LANES)[:, :, :4 * DN_HEADS].transpose(0, 2, 1)
            oc_f, oc_b, sc_f, sc_b = deltanet_bidir(q, k, v, gb, gbt, dn_zero, dn_zero, nb, cseq, t_lat, tl)
            ol_f, ol_b, _, _ = deltanet_bidir(q, k, v, gb, gbt, sc_f, sc_b, nb, seq, 0, tl)
            xa = outproj_odd(ol_f, oc_f, ol_b, oc_b, p, od_out_gain[i], od_w_out[i].astype(BF16), xa, g1, tm, seq,
                             nb)

        w_router = jnp.pad(jnp.concatenate([moe_w_group[layer], moe_w_expert[layer]], axis=1),
                           ((0, 0), (0, LANES - N_GROUPS - N_EXPERTS))).astype(BF16)
        b_router = jnp.pad(jnp.concatenate([moe_b_group[layer], moe_b_expert[layer]]),
                           (0, LANES - N_GROUPS - N_EXPERTS)).reshape(1, LANES).astype(F32)
        f, route, counts = moe_router(xa, norm_ffn[layer], sh2, sc2, w_router, b_router, tm, seq, nb)
        tile_expert, n_used, n_pad, pos_tiles = moe_slots(route, counts, tm_ffn, tm_comb)
        xs = moe_scatter(pos_tiles, f, jnp.zeros((n_pad, d // 2), U32) if xs is None else xs, tm_comb)
        y_sorted = moe_ffn(tile_expert, n_used, xs, moe_w_gate_up, moe_w_down, layer, tm_ffn)
        xa = moe_combine(pos_tiles, xa, g2, route, y_sorted, tm_comb, seq, nb)

    out = final_norm(xa, final_norm_gain, t_lat, tm)
    return out.reshape(nb, seq, d)
```

```python
import functools
import math

import numpy as np
import jax
import jax.numpy as jnp
from jax import lax
from jax.experimental import pallas as pl
from jax.experimental.pallas import tpu as pltpu

F32 = jnp.float32
BF16 = jnp.bfloat16
U32 = jnp.uint32
HIGHEST = lax.Precision.HIGHEST

EPS = 1e-6
GRID_W = 64
ROPE_BASE = 10000.0
RET_HEADS, RET_DK, RET_DV, RET_CHUNK = 8, 64, 128, 128
ATT_HEADS, ATT_KV_HEADS, ATT_HD = 8, 2, 64
DN_HEADS, DN_DK, DN_DV, DN_CHUNK, DN_CONV = 8, 128, 128, 64, 3
N_GROUPS, EXPERTS_PER_GROUP, TOP_K = 4, 8, 2
N_EXPERTS = N_GROUPS * EXPERTS_PER_GROUP

EVEN_IN = 2 * RET_HEADS * RET_DK + 2 * RET_HEADS * RET_DV + (ATT_HEADS + 2 * ATT_KV_HEADS) * ATT_HD
EVEN_ATT_COL = 2 * RET_HEADS * RET_DK + 2 * RET_HEADS * RET_DV
DN_QKV = 2 * DN_HEADS * DN_DK + DN_HEADS * DN_DV
ODD_IN = DN_QKV + DN_HEADS * DN_DV + 4 * DN_HEADS
ODD_IN_PAD = ((ODD_IN + 127) // 128) * 128

LANES = 128
VMEM_LIMIT = 56 * 1024 * 1024

NT_DIMS = (((1,), (1,)), ((), ()))
TN_DIMS = (((0,), (0,)), ((), ()))


def _cparams(sem):
    return pltpu.CompilerParams(dimension_semantics=sem, vmem_limit_bytes=VMEM_LIMIT)


def _silu(x):
    return x / (1.0 + jnp.exp(-x))


def _dot(a, b):
    return jnp.dot(a, b, preferred_element_type=F32)


def _adaln_kernel(c_ref, w_ref, b_ref, o_ref):
    s = _silu(c_ref[...])
    o_ref[...] = _dot(s.astype(BF16), w_ref[...].astype(BF16)) + b_ref[...]


def adaln(c8, w_ada, b_ada):
    depth, d, n6 = w_ada.shape
    tn = min(n6, 1536)
    return pl.pallas_call(
        _adaln_kernel,
        grid=(depth, n6 // tn),
        in_specs=[
            pl.BlockSpec((8, d), lambda l, j: (0, 0)),
            pl.BlockSpec((None, d, tn), lambda l, j: (l, 0, j)),
            pl.BlockSpec((None, 1, tn), lambda l, j: (l, 0, j)),
        ],
        out_specs=pl.BlockSpec((None, 8, tn), lambda l, j: (l, 0, j)),
        out_shape=jax.ShapeDtypeStruct((depth, 8, n6), F32),
        compiler_params=_cparams(("arbitrary", "arbitrary")),
        name="adaln",
    )(c8, w_ada, b_ada.reshape(depth, 1, n6))


def _norm_mod(x, gain, shift, scale):
    ms = jnp.mean(x * x, axis=-1, keepdims=True)
    h = x * lax.rsqrt(ms + EPS) * gain
    return h * (1.0 + scale) + shift


def _pack_bf16_pairs(x):
    n = x.shape[1] // 2
    hi = lax.bitcast_convert_type(x[:, :n].astype(BF16).astype(F32), U32)
    lo = lax.bitcast_convert_type(x[:, n:].astype(BF16).astype(F32), U32)
    return hi | (lo >> 16)


def _unpack_bf16_pairs(p):
    hi = lax.bitcast_convert_type(p & jnp.uint32(0xFFFF0000), F32)
    lo = lax.bitcast_convert_type(p << 16, F32)
    return hi, lo


def _mod_row_map(tm, seq, n_lat_batches):
    return lambda i: (jnp.minimum((i * tm) // seq, n_lat_batches), 0, 0)


def _inproj_even_kernel(x_ref, g_ref, sh_ref, sc_ref, w_ref, cos_ref, s1_ref, s2_ref, qg_ref, kg_ref, bd_ref,
                        rq_ref, rk_ref, vg_ref, aq_ref, ak_ref, av_ref):
    hb = _norm_mod(x_ref[...], g_ref[...], sh_ref[...], sc_ref[...]).astype(BF16)
    cos = cos_ref[...]
    s1 = s1_ref[...]
    s2 = s2_ref[...]
    bd = bd_ref[...]
    half = ATT_HD

    def proj(c0, width=LANES):
        return _dot(hb, w_ref[:, c0:c0 + width])

    def rope(x):
        return x * cos + pltpu.roll(x, LANES - 16, 1) * s1 + pltpu.roll(x, 16, 1) * s2

    def head_norm(x, gain):
        sq = x * x
        hi = sq.astype(BF16)
        lo = (sq - hi.astype(F32)).astype(BF16)
        ms = _dot(hi, bd) + _dot(lo, bd)
        return x * lax.rsqrt(ms + EPS) * gain

    wide = 2 * LANES
    qw = RET_HEADS * RET_DK
    for c0 in range(0, qw, wide):
        yq = proj(c0, wide)
        yk = proj(qw + c0, wide)
        for u in range(2):
            cs = slice(c0 + u * LANES, c0 + (u + 1) * LANES)
            us = slice(u * LANES, (u + 1) * LANES)
            rq_ref[:, cs] = rope(yq[:, us]).astype(BF16)
            rk_ref[:, cs] = (rope(yk[:, us]) * RET_DK ** -0.5).astype(BF16)
    vgw = 2 * RET_HEADS * RET_DV
    for c0 in range(0, vgw, 512):
        vg_ref[:, c0:c0 + 512] = proj(2 * qw + c0, 512).astype(BF16)

    qg = qg_ref[...]
    kg = kg_ref[...]
    a0 = EVEN_ATT_COL
    for c0 in range(0, ATT_HEADS * ATT_HD, wide):
        ya = proj(a0 + c0, wide)
        for u in range(2):
            y = rope(head_norm(ya[:, u * LANES:(u + 1) * LANES], qg)) * (ATT_HD ** -0.5 * math.log2(math.e))
            y = y.astype(BF16)
            hd0 = (c0 + u * LANES) // ATT_HD
            aq_ref[hd0] = y[:, :half]
            aq_ref[hd0 + 1] = y[:, half:]
    ykv = proj(a0 + ATT_HEADS * ATT_HD, wide)
    y = rope(head_norm(ykv[:, :LANES], kg)).astype(BF16)
    ak_ref[0] = y[:, :half]
    ak_ref[1] = y[:, half:]
    v = ykv[:, LANES:].astype(BF16)
    av_ref[0] = v[:, :half]
    av_ref[1] = v[:, half:]


def inproj_even(xa, gain, shift, scale, w, tabs, q_gain, k_gain, tm, seq, nb):
    ta, d = xa.shape
    t_lat = nb * seq
    cos_t, s1_t, s2_t = tabs
    n_tab = seq // tm

    def tab_map(i):
        r = i * tm
        return (jnp.where(r < t_lat, (r % seq) // tm, n_tab), 0)

    ii = np.arange(LANES)
    bd = jnp.asarray((ii[:, None] // ATT_HD == ii[None, :] // ATT_HD).astype(np.float32) / ATT_HD).astype(BF16)
    qg = jnp.tile(q_gain.astype(F32), LANES // ATT_HD).reshape(1, LANES)
    kg = jnp.tile(k_gain.astype(F32), LANES // ATT_HD).reshape(1, LANES)
    mrow = _mod_row_map(tm, seq, nb)
    tab_spec = pl.BlockSpec((tm, LANES), tab_map)
    one = lambda i: (0, 0)
    qw = RET_HEADS * RET_DK
    vgw = 2 * RET_HEADS * RET_DV
    return pl.pallas_call(
        _inproj_even_kernel,
        grid=(ta // tm,),
        in_specs=[
            pl.BlockSpec((tm, d), lambda i: (i, 0)),
            pl.BlockSpec((1, d), one),
            pl.BlockSpec((None, 1, d), mrow),
            pl.BlockSpec((None, 1, d), mrow),
            pl.BlockSpec((d, EVEN_IN), one),
            tab_spec, tab_spec, tab_spec,
            pl.BlockSpec((1, LANES), one), pl.BlockSpec((1, LANES), one),
            pl.BlockSpec((LANES, LANES), one),
        ],
        out_specs=[
            pl.BlockSpec((tm, qw), lambda i: (i, 0)),
            pl.BlockSpec((tm, qw), lambda i: (i, 0)),
            pl.BlockSpec((tm, vgw), lambda i: (i, 0)),
            pl.BlockSpec((ATT_HEADS, tm, ATT_HD), lambda i: (0, i, 0)),
            pl.BlockSpec((ATT_KV_HEADS, tm, ATT_HD), lambda i: (0, i, 0)),
            pl.BlockSpec((ATT_KV_HEADS, tm, ATT_HD), lambda i: (0, i, 0)),
        ],
        out_shape=[
            jax.ShapeDtypeStruct((ta, qw), BF16),
            jax.ShapeDtypeStruct((ta, qw), BF16),
            jax.ShapeDtypeStruct((ta, vgw), BF16),
            jax.ShapeDtypeStruct((ATT_HEADS, ta, ATT_HD), BF16),
            jax.ShapeDtypeStruct((ATT_KV_HEADS, ta, ATT_HD), BF16),
            jax.ShapeDtypeStruct((ATT_KV_HEADS, ta, ATT_HD), BF16),
        ],
        compiler_params=_cparams(("arbitrary",)),
        name="inproj_even",
    )(xa, gain.reshape(1, d), shift, scale, w, cos_t, s1_t, s2_t, qg, kg, bd)


def rope_tables(seq, tm):
    nf = ATT_HD // 4
    t = jnp.arange(seq)
    rows = (t // GRID_W).astype(F32)
    cols = (t % GRID_W).astype(F32)
    inv = ROPE_BASE ** (-jnp.arange(nf, dtype=F32) / nf)
    lane = np.arange(LANES)
    axis = (lane % ATT_HD) // (ATT_HD // 2)
    f = lane % nf
    upper = ((lane % (ATT_HD // 2)) >= nf)
    pos = jnp.where(jnp.asarray(axis)[None, :] == 0, rows[:, None], cols[:, None])
    ang = pos * inv[jnp.asarray(f)][None, :]
    cos = jnp.cos(ang)
    sin = jnp.sin(ang)
    s1 = jnp.where(jnp.asarray(upper)[None, :], 0.0, -sin)
    s2 = jnp.where(jnp.asarray(upper)[None, :], sin, 0.0)
    pad1 = jnp.ones((tm, LANES), F32)
    pad0 = jnp.zeros((tm, LANES), F32)
    return (jnp.concatenate([cos, pad1]), jnp.concatenate([s1, pad0]), jnp.concatenate([s2, pad0]))


def _retention_kernel(dec_ref, q_ref, k_ref, v_ref, g_ref, s0f_ref, s0b_ref,
                      o_ref, sff_ref, sfb_ref, st_ref, *, n_chunks, unroll):
    hp = pl.program_id(1)
    C = RET_CHUNK
    dk, dv = RET_DK, RET_DV
    pos = lax.broadcasted_iota(jnp.int32, (C, dk), 0).astype(F32)
    ii = lax.broadcasted_iota(jnp.int32, (C, C), 0)
    jj = lax.broadcasted_iota(jnp.int32, (C, C), 1)
    dpos = (ii - jj).astype(F32)
    heads = range(2)
    qs = [slice(hh * dk, (hh + 1) * dk) for hh in heads]
    vs = [slice(hh * dv, (hh + 1) * dv) for hh in heads]
    w_out, w_in, gcf, gcb, mask = [], [], [], [], []
    for hh in heads:
        h = 2 * hp + hh
        df = dec_ref[0, h]
        db = dec_ref[1, h]
        lf = -jnp.exp(jnp.full((C, C), df, F32))
        lb = -jnp.exp(jnp.full((C, C), db, F32))
        lfk = -jnp.exp(jnp.full((C, dk), df, F32))
        lbk = -jnp.exp(jnp.full((C, dk), db, F32))
        w_out.append(jnp.concatenate([jnp.exp(lfk * (C - 1.0 - pos)), jnp.exp(lbk * pos)], axis=1))
        w_in.append(jnp.concatenate([jnp.exp(lfk * (pos + 1.0)), jnp.exp(lbk * (C - pos))], axis=1))
        gcf.append(jnp.exp(-jnp.exp(jnp.full((dk, dv), df, F32)) * C))
        gcb.append(jnp.exp(-jnp.exp(jnp.full((dk, dv), db, F32)) * C))
        mask.append(jnp.where(dpos > 0, jnp.exp(lf * jnp.maximum(dpos, 0.0)),
                              jnp.where(dpos < 0, jnp.exp(lb * jnp.maximum(-dpos, 0.0)), 2.0)))

    def rows(n):
        return pl.ds(pl.multiple_of(n * C, C), C)

    items = [(u, hh) for u in range(unroll) for hh in heads]

    def sums_body(i, carry):
        kk = {}
        for u, hh in items:
            k = k_ref[rows(i * unroll + u), qs[hh]].astype(F32)
            kk[(u, hh)] = (jnp.concatenate([k, k], axis=1) * w_out[hh]).astype(BF16)
        kv = {(u, hh): lax.dot_general(kk[(u, hh)], v_ref[rows(i * unroll + u), vs[hh]], TN_DIMS,
                                       preferred_element_type=F32) for u, hh in items}
        for u, hh in items:
            st_ref[hh, i * unroll + u] = kv[(u, hh)]
        return carry

    lax.fori_loop(0, n_chunks // unroll, sums_body, 0)

    def scan_body(n, carry):
        n_rev = n_chunks - 1 - n
        out = []
        for hh in heads:
            sf, sb = carry[2 * hh], carry[2 * hh + 1]
            kvf = st_ref[hh, n, 0:dk, :]
            kvb = st_ref[hh, n_rev, dk:2 * dk, :]
            st_ref[hh, n, 0:dk, :] = sf
            st_ref[hh, n_rev, dk:2 * dk, :] = sb
            out += [gcf[hh] * sf + kvf, gcb[hh] * sb + kvb]
        return tuple(out)

    init = tuple(x for hh in heads for x in (s0f_ref[hh], s0b_ref[hh]))
    fin = lax.fori_loop(0, n_chunks, scan_body, init)
    for hh in heads:
        sff_ref[hh] = fin[2 * hh]
        sfb_ref[hh] = fin[2 * hh + 1]

    def out_body(i, carry):
        ns = [i * unroll + u for u in range(unroll)]
        qb = {(u, hh): q_ref[rows(ns[u]), qs[hh]] for u, hh in items}
        sc = {(u, hh): lax.dot_general(qb[(u, hh)], k_ref[rows(ns[u]), qs[hh]], NT_DIMS,
                                       preferred_element_type=F32) for u, hh in items}
        qw = {}
        for it in items:
            q = qb[it].astype(F32)
            qw[it] = (jnp.concatenate([q, q], axis=1) * w_in[it[1]]).astype(BF16)
        o1 = {(u, hh): _dot((sc[(u, hh)] * mask[hh]).astype(BF16), v_ref[rows(ns[u]), vs[hh]]) for u, hh in items}
        o2 = {(u, hh): _dot(qw[(u, hh)], st_ref[hh, ns[u]].astype(BF16)) for u, hh in items}
        for it in items:
            u, hh = it
            n = ns[u]
            o = o1[it] + o2[it]
            o = o * lax.rsqrt(jnp.mean(o * o, axis=-1, keepdims=True) + EPS)
            gate = g_ref[rows(n), vs[hh]].astype(F32)
            o_ref[rows(n), vs[hh]] = (_silu(gate) * o).astype(o_ref.dtype)
        return carry

    lax.fori_loop(0, n_chunks // unroll, out_body, 0)


def retention(dec, rq, rk, p, s0f, s0b, nb, seq, row_off_blocks):
    n_chunks = seq // RET_CHUNK
    hp_n = RET_HEADS // 2
    vcol = 0
    gcol = vcol + RET_HEADS * RET_DV // (2 * RET_DV)
    ta = rq.shape[0]
    st_spec = pl.BlockSpec((None, 2, RET_DK, RET_DV), lambda b, hp, *_: (b, hp, 0, 0))
    grid_spec = pltpu.PrefetchScalarGridSpec(
        num_scalar_prefetch=1,
        grid=(nb, hp_n),
        in_specs=[
            pl.BlockSpec((seq, 2 * RET_DK), lambda b, hp, *_: (row_off_blocks + b, hp)),
            pl.BlockSpec((seq, 2 * RET_DK), lambda b, hp, *_: (row_off_blocks + b, hp)),
            pl.BlockSpec((seq, 2 * RET_DV), lambda b, hp, *_: (row_off_blocks + b, vcol + hp)),
            pl.BlockSpec((seq, 2 * RET_DV), lambda b, hp, *_: (row_off_blocks + b, gcol + hp)),
            st_spec, st_spec,
        ],
        out_specs=[
            pl.BlockSpec((seq, 2 * RET_DV), lambda b, hp, *_: (b, hp)),
            st_spec, st_spec,
        ],
        scratch_shapes=[pltpu.VMEM((2, n_chunks, 2 * RET_DK, RET_DV), F32)],
    )
    st_shape = jax.ShapeDtypeStruct((nb, RET_HEADS, RET_DK, RET_DV), F32)
    return pl.pallas_call(
        functools.partial(_retention_kernel, n_chunks=n_chunks, unroll=math.gcd(n_chunks, 8)),
        grid_spec=grid_spec,
        out_shape=[jax.ShapeDtypeStruct((nb * seq, RET_HEADS * RET_DV), BF16), st_shape, st_shape],
        compiler_params=_cparams(("arbitrary", "arbitrary")),
        name="retention",
    )(dec, rq, rk, p, p, s0f, s0b)


ATT_VT_ROWS = ATT_HD + 16


ATT_PAIRS_PER_TRIP = 8


def _attn_kernel(q_ref, k_ref, vt_ref, o_ref, *s_refs, tk, c_start, c_end, rep):
    tq = q_ref.shape[2]
    sets = (s_refs[:rep], s_refs[rep:])
    last = c_end - 1

    def scores(bufs, j):
        j = jnp.minimum(j, last)
        c0 = pl.multiple_of(j * tk, tk)
        k = k_ref[pl.ds(c0, tk), :]
        mxs = []
        for r in range(rep):
            s = _dot(k, q_ref[r])
            bufs[r][...] = s
            mxs.append(jnp.max(s, axis=0, keepdims=True))
        return tuple(mxs)

    def softmax_pv(bufs, j, mxs, ms, accs):
        vt = vt_ref[j]
        new_m, new_acc = [], []
        for r in range(rep):
            m_new = jnp.maximum(ms[r], mxs[r])
            a = jnp.exp2(ms[r] - m_new)
            p = jnp.exp2(bufs[r][...] - m_new).astype(BF16)
            new_acc.append(a * accs[r] + _dot(vt, p))
            new_m.append(m_new)
        return tuple(new_m), tuple(new_acc)

    def pair(j, mx0, ms, accs):
        mx1 = scores(sets[1], j + 1)
        ms, accs = softmax_pv(sets[0], j, mx0, ms, accs)
        mx0 = scores(sets[0], j + 2)
        ms, accs = softmax_pv(sets[1], j + 1, mx1, ms, accs)
        return mx0, ms, accs

    def trip(t, carry):
        for u in range(ATT_PAIRS_PER_TRIP):
            carry = pair(c_start + 2 * (ATT_PAIRS_PER_TRIP * t + u), *carry)
        return carry

    n_pairs = (c_end - c_start) // 2
    n_trips = n_pairs // ATT_PAIRS_PER_TRIP
    ms = tuple(jnp.full((1, tq), -1e30, F32) for _ in range(rep))
    accs = tuple(jnp.zeros((ATT_VT_ROWS, tq), F32) for _ in range(rep))
    carry = (scores(sets[0], c_start), ms, accs)
    if n_trips:
        carry = lax.fori_loop(0, n_trips, trip, carry)
    for u in range(n_trips * ATT_PAIRS_PER_TRIP, n_pairs):
        carry = pair(c_start + 2 * u, *carry)
    mx0, ms, accs = carry
    if (c_end - c_start) % 2:
        ms, accs = softmax_pv(sets[0], last, mx0, ms, accs)
    outs = [(acc[:ATT_HD, :] / acc[ATT_HD:ATT_HD + 1, :]).T for acc in accs]
    o_ref[...] = jnp.concatenate(outs, axis=-1).astype(o_ref.dtype)


def attention(aq, kcat, vtcat, seq_q, q_off_blocks, c_start, tq, tk):
    rep = ATT_HEADS // ATT_KV_HEADS
    _, nb, lk, _ = kcat.shape
    nq = seq_q // tq
    n_chunks = lk // tk
    return pl.pallas_call(
        functools.partial(_attn_kernel, tk=tk, c_start=c_start, c_end=n_chunks, rep=rep),
        grid=(nb, ATT_KV_HEADS, nq),
        in_specs=[
            pl.BlockSpec((rep, ATT_HD, tq), lambda b, g, i: (g, 0, q_off_blocks + b * nq + i)),
            pl.BlockSpec((None, None, lk, ATT_HD), lambda b, g, i: (g, b, 0, 0)),
            pl.BlockSpec((None, None, n_chunks, ATT_VT_ROWS, tk), lambda b, g, i: (g, b, 0, 0, 0)),
        ],
        out_specs=pl.BlockSpec((tq, rep * ATT_HD), lambda b, g, i: (b * nq + i, g)),
        out_shape=jax.ShapeDtypeStruct((nb * seq_q, ATT_HEADS * ATT_HD), BF16),
        scratch_shapes=[pltpu.VMEM((tk, tq), F32) for _ in range(2 * rep)],
        compiler_params=_cparams(("arbitrary", "arbitrary", "arbitrary")),
        name="attention",
    )(aq, kcat, vtcat)


def _lat_ctx_specs(tm, width, n_lat_tiles):
    return [pl.BlockSpec((tm, width), lambda i: (jnp.minimum(i, n_lat_tiles - 1), 0)),
            pl.BlockSpec((tm, width), lambda i: (jnp.maximum(i - n_lat_tiles, 0), 0))]


def _outproj_even_kernel(r_lat, r_ctx, a_lat, a_ctx, w1_ref, w2_ref, res_ref, gate_ref, o_ref, *, n_lat_tiles):
    is_lat = pl.program_id(0) < n_lat_tiles
    a1 = jnp.where(is_lat, r_lat[...], r_ctx[...])
    a2 = jnp.where(is_lat, a_lat[...], a_ctx[...])
    y = _dot(a1, w1_ref[...]) + _dot(a2, w2_ref[...])
    o_ref[...] = res_ref[...] + gate_ref[...] * y


def outproj_even(ret_lat, ret_ctx, att_lat, att_ctx, w1, w2, xa, gate, tm, seq, nb):
    ta, d = xa.shape
    k1, k2 = w1.shape[0], w2.shape[0]
    n_lat_tiles = ret_lat.shape[0] // tm
    return pl.pallas_call(
        functools.partial(_outproj_even_kernel, n_lat_tiles=n_lat_tiles),
        grid=(ta // tm,),
        in_specs=_lat_ctx_specs(tm, k1, n_lat_tiles) + _lat_ctx_specs(tm, k2, n_lat_tiles) + [
            pl.BlockSpec((k1, d), lambda i: (0, 0)),
            pl.BlockSpec((k2, d), lambda i: (0, 0)),
            pl.BlockSpec((tm, d), lambda i: (i, 0)),
            pl.BlockSpec((None, 1, d), _mod_row_map(tm, seq, nb)),
        ],
        out_specs=pl.BlockSpec((tm, d), lambda i: (i, 0)),
        out_shape=jax.ShapeDtypeStruct((ta, d), F32),
        compiler_params=_cparams(("arbitrary",)),
        name="outproj_even",
    )(ret_lat, ret_ctx, att_lat, att_ctx, w1, w2, xa, gate)


def _outproj_odd_kernel(f_lat, f_ctx, b_lat, b_ctx, z_ref, og_ref, w_ref, res_ref, gate_ref, o_ref,
                        *, n_lat_tiles):
    is_lat = pl.program_id(0) < n_lat_tiles
    og = og_ref[...]
    parts = []
    for h in range(DN_HEADS):
        cs = slice(h * DN_DV, (h + 1) * DN_DV)
        of = jnp.where(is_lat, f_lat[:, cs], f_ctx[:, cs]).astype(F32)
        ob = jnp.where(is_lat, b_lat[:, cs], b_ctx[:, cs]).astype(F32)
        o = of + ob
        o = o * lax.rsqrt(jnp.mean(o * o, axis=-1, keepdims=True) + EPS) * og
        parts.append((o * _silu(z_ref[:, cs].astype(F32))).astype(BF16))
    a = jnp.concatenate(parts, axis=-1)
    o_ref[...] = res_ref[...] + gate_ref[...] * _dot(a, w_ref[...])


def outproj_odd(of_lat, of_ctx, ob_lat, ob_ctx, p, out_gain, w, xa, gate, tm, seq, nb):
    ta, d = xa.shape
    kdim = DN_HEADS * DN_DV
    n_lat_tiles = of_lat.shape[0] // tm
    return pl.pallas_call(
        functools.partial(_outproj_odd_kernel, n_lat_tiles=n_lat_tiles),
        grid=(ta // tm,),
        in_specs=_lat_ctx_specs(tm, kdim, n_lat_tiles) + _lat_ctx_specs(tm, kdim, n_lat_tiles) + [
            pl.BlockSpec((tm, kdim), lambda i: (i, 0)),
            pl.BlockSpec((1, DN_DV), lambda i: (0, 0)),
            pl.BlockSpec((kdim, d), lambda i: (0, 0)),
            pl.BlockSpec((tm, d), lambda i: (i, 0)),
            pl.BlockSpec((None, 1, d), _mod_row_map(tm, seq, nb)),
        ],
        out_specs=pl.BlockSpec((tm, d), lambda i: (i, 0)),
        out_shape=jax.ShapeDtypeStruct((ta, d), F32),
        compiler_params=_cparams(("arbitrary",)),
        name="outproj_odd",
    )(of_lat, of_ctx, ob_lat, ob_ctx, p, out_gain.reshape(1, DN_DV).astype(F32), w, xa, gate)


def _inproj_odd_kernel(first_ref, last_ref, x_ref, xp_ref, xn_ref, g_ref, sh_ref, sc_ref, w_ref, cw_ref,
                       arow_ref, brow_ref, q_ref, k_ref, v_ref, z_ref, gb_ref, *, ctx_tile0, cseq):
    i = pl.program_id(0)
    tm = x_ref.shape[0]
    gain, shift, scale = g_ref[...], sh_ref[...], sc_ref[...]
    hrows = xp_ref.shape[0]
    hall = _norm_mod(jnp.concatenate([x_ref[...], xp_ref[...], xn_ref[...]], axis=0), gain, shift, scale).astype(BF16)
    hb = hall[:tm]
    keep_prev = 1.0 - first_ref[i].astype(F32)
    keep_next = 1.0 - last_ref[i].astype(F32)
    row = lax.broadcasted_iota(jnp.int32, (tm, LANES), 0)
    is_first = row == 0
    is_last = row == tm - 1
    inner = cseq < tm
    if inner:
        in_ctx = i >= ctx_tile0
        local = row & (cseq - 1)
        zero_dn = jnp.logical_and(in_ctx, local == 0)
        zero_up = jnp.logical_and(in_ctx, local == cseq - 1)
    n_qk = 2 * DN_HEADS * DN_DK // LANES
    n_q = DN_HEADS * DN_DK // LANES
    outs = (q_ref, k_ref, v_ref)
    wide = 2 * LANES
    for c0 in range(0, DN_QKV, wide):
        yall = _dot(hall, w_ref[:, c0:c0 + wide])
        y2 = yall[:tm]
        yh2 = yall[tm:]
        for u in range(2):
            j = c0 // LANES + u
            us = slice(u * LANES, (u + 1) * LANES)
            x = y2[:, us]
            xp = yh2[hrows - 1:hrows, us] * keep_prev
            xn = yh2[hrows:hrows + 1, us] * keep_next
            x_dn = jnp.where(is_first, xp, pltpu.roll(x, 1, 0))
            x_up = jnp.where(is_last, xn, pltpu.roll(x, tm - 1, 0))
            if inner:
                x_dn = jnp.where(zero_dn, 0.0, x_dn)
                x_up = jnp.where(zero_up, 0.0, x_up)
            w = cw_ref[:, j * LANES:(j + 1) * LANES]
            y = _silu(x_dn * w[0:1, :] + x * w[1:2, :] + x_up * w[2:3, :])
            if j < n_qk:
                y = y * lax.rsqrt(jnp.sum(y * y, axis=-1, keepdims=True) + EPS)
                if j < n_q:
                    y = y * DN_DK ** -0.5
            lj = j % n_q
            outs[j // n_q][:, lj * LANES:(lj + 1) * LANES] = y.astype(BF16)
    zw = DN_HEADS * DN_DV
    for c0 in range(0, zw, 512):
        z_ref[:, c0:c0 + 512] = _dot(hb, w_ref[:, DN_QKV + c0:DN_QKV + c0 + 512]).astype(BF16)

    a = _dot(hb, w_ref[:, DN_QKV + zw:DN_QKV + zw + LANES])
    lane = lax.broadcasted_iota(jnp.int32, (tm, LANES), 1)
    zz = a + brow_ref[...]
    softplus = jnp.maximum(zz, 0.0) + jnp.log(1.0 + jnp.exp(-jnp.abs(zz)))
    g = -jnp.exp(arow_ref[...]) * softplus
    beta = 1.0 / (1.0 + jnp.exp(-a))
    gb_ref[...] = jnp.where(lane < 2 * DN_HEADS, g, jnp.where(lane < 4 * DN_HEADS, beta, 0.0))


def inproj_odd(xa, gain, shift, scale, w, conv_w, arow, brow, first_flags, last_flags, tm, seq, nb, cseq):
    ta, d = xa.shape
    halo = 8
    hb = tm // halo
    n_h = ta // halo
    kdim = DN_HEADS * DN_DK
    assert cseq >= tm or (tm % cseq == 0 and cseq & (cseq - 1) == 0)
    mrow = lambda i, *_: _mod_row_map(tm, seq, nb)(i)
    one = lambda i, *_: (0, 0)
    row_blk = lambda i, *_: (i, 0)
    grid_spec = pltpu.PrefetchScalarGridSpec(
        num_scalar_prefetch=2,
        grid=(ta // tm,),
        in_specs=[
            pl.BlockSpec((tm, d), row_blk),
            pl.BlockSpec((halo, d), lambda i, *_: (jnp.maximum(i * hb - 1, 0), 0)),
            pl.BlockSpec((halo, d), lambda i, *_: (jnp.minimum((i + 1) * hb, n_h - 1), 0)),
            pl.BlockSpec((1, d), one),
            pl.BlockSpec((None, 1, d), mrow),
            pl.BlockSpec((None, 1, d), mrow),
            pl.BlockSpec((d, ODD_IN_PAD), one),
            pl.BlockSpec((DN_CONV, DN_QKV), one),
            pl.BlockSpec((1, LANES), one),
            pl.BlockSpec((1, LANES), one),
        ],
        out_specs=[pl.BlockSpec((tm, kdim), row_blk)] * 4 + [pl.BlockSpec((tm, LANES), row_blk)],
    )
    return pl.pallas_call(
        functools.partial(_inproj_odd_kernel, ctx_tile0=nb * seq // tm, cseq=cseq),
        grid_spec=grid_spec,
        out_shape=[jax.ShapeDtypeStruct((ta, kdim), BF16)] * 4 + [jax.ShapeDtypeStruct((ta, LANES), F32)],
        compiler_params=_cparams(("arbitrary",)),
        name="inproj_odd",
    )(first_flags, last_flags, xa, xa, xa, gain.reshape(1, d), shift, scale, w, conv_w, arow, brow)


def _deltanet_bidir_kernel(qf_ref, kf_ref, vf_ref, gbf_ref, gbtf_ref, qb_ref, kb_ref, vb_ref, gbb_ref, gbtb_ref,
                           s0f_ref, s0b_ref, of_ref, ob_ref, sff_ref, sfb_ref, sf_scr, sb_scr, *, n_chunks):
    t = pl.program_id(1)

    @pl.when(t == 0)
    def _():
        sf_scr[...] = s0f_ref[...]
        sb_scr[...] = s0b_ref[...]

    C = DN_CHUNK
    ii = lax.broadcasted_iota(jnp.int32, (C, C), 0)
    jj = lax.broadcasted_iota(jnp.int32, (C, C), 1)
    lower, upper = ii >= jj, ii <= jj
    eye = jnp.where(ii == jj, 1.0, 0.0).astype(F32)
    blk = ii ^ jj
    dirs = (
        dict(rev=False, incl=lower, strict=ii > jj, q=qf_ref, k=kf_ref, v=vf_ref, gb=gbf_ref, gbt=gbtf_ref,
             o=of_ref, scr=sf_scr, off=0, order=list(range(n_chunks))),
        dict(rev=True, incl=upper, strict=ii < jj, q=qb_ref, k=kb_ref, v=vb_ref, gb=gbb_ref, gbt=gbtb_ref,
             o=ob_ref, scr=sb_scr, off=DN_HEADS, order=list(range(n_chunks - 1, -1, -1))),
    )
    items = [(d, c, h) for d in range(2) for c in dirs[d]["order"] for h in range(DN_HEADS)]

    gcols, grows, gbs = {}, {}, {}
    for d, dr in enumerate(dirs):
        tri = jnp.where(dr["incl"], 1.0, 0.0).astype(F32)
        tri_t = jnp.where(upper if not dr["rev"] else lower, 1.0, 0.0).astype(F32)
        for c in dr["order"]:
            gb_c = dr["gb"][c * C:(c + 1) * C, :]
            gbs[(d, c)] = gb_c
            gcols[(d, c)] = jnp.dot(tri, gb_c, preferred_element_type=F32, precision=HIGHEST)
            grows[(d, c)] = jnp.dot(dr["gbt"][c], tri_t, preferred_element_type=F32, precision=HIGHEST)

    qb, kb16, decay, kbeta, egc, kd, gl, rhs = {}, {}, {}, {}, {}, {}, {}, {}
    for it in items:
        d, c, h = it
        dr = dirs[d]
        gi = dr["off"] + h
        bi = 2 * DN_HEADS + dr["off"] + h
        rows = slice(c * C, (c + 1) * C)
        cs = slice(h * DN_DK, (h + 1) * DN_DK)
        gc = gcols[(d, c)][:, gi:gi + 1]
        gr = grows[(d, c)][gi:gi + 1, :]
        beta = gbs[(d, c)][:, bi:bi + 1]
        qb[it] = dr["q"][rows, cs]
        kb16[it] = dr["k"][rows, cs]
        kf = kb16[it].astype(F32)
        decay[it] = jnp.where(dr["incl"], jnp.exp(jnp.where(dr["incl"], gc - gr, 0.0)), 0.0)
        kbeta[it] = kf * beta
        egc[it] = jnp.exp(gc)
        glast = gc[0:1, :] if dr["rev"] else gc[C - 1:C, :]
        kd[it] = (kf * jnp.exp(glast - gc)).astype(BF16)
        gl[it] = jnp.exp(glast)
        rhs[it] = jnp.concatenate([dr["v"][rows, cs].astype(F32) * beta, kbeta[it] * egc[it]], axis=1).astype(BF16)

    kk = {it: lax.dot_general(kbeta[it].astype(BF16), kb16[it], NT_DIMS, preferred_element_type=F32)
          for it in items}
    qk = {it: lax.dot_general(qb[it], kb16[it], NT_DIMS, preferred_element_type=F32) for it in items}
    lm = {it: jnp.where(dirs[it[0]]["strict"], kk[it] * decay[it], 0.0) for it in items}
    attn = {it: jnp.where(dirs[it[0]]["incl"], qk[it] * decay[it], 0.0).astype(BF16) for it in items}
    dinv = {it: eye - jnp.where(blk < 2, lm[it], 0.0) for it in items}
    s = 2
    while s < C:
        in_band = jnp.logical_and(blk >= s, blk < 2 * s)
        tmp = {it: _dot(dinv[it].astype(BF16), jnp.where(in_band, lm[it], 0.0).astype(BF16)) for it in items}
        dinv = {it: dinv[it] - _dot(tmp[it].astype(BF16), dinv[it].astype(BF16)) for it in items}
        s *= 2
    uw = {it: _dot(dinv[it].astype(BF16), rhs[it]) for it in items}
    wq = {it: jnp.concatenate([uw[it][:, DN_DV:], qb[it].astype(F32) * egc[it]], axis=0).astype(BF16)
          for it in items}

    states = {(d, h): dirs[d]["scr"][h] for d in range(2) for h in range(DN_HEADS)}
    for step in range(n_chunks):
        its = [(d, dirs[d]["order"][step], h) for d in range(2) for h in range(DN_HEADS)]
        r = {it: _dot(wq[it], states[(it[0], it[2])].astype(BF16)) for it in its}
        v_new = {it: (uw[it][:, :DN_DV] - r[it][:C]).astype(BF16) for it in its}
        o = {it: r[it][C:] + _dot(attn[it], v_new[it]) for it in its}
        for it in its:
            key = (it[0], it[2])
            states[key] = states[key] * gl[it] + lax.dot_general(kd[it], v_new[it], TN_DIMS,
                                                                 preferred_element_type=F32)
        for it in its:
            d, c, h = it
            dirs[d]["o"][c * C:(c + 1) * C, h * DN_DK:(h + 1) * DN_DK] = o[it].astype(of_ref.dtype)
    for (d, h), st in states.items():
        dirs[d]["scr"][h] = st

    @pl.when(t == pl.num_programs(1) - 1)
    def _():
        sff_ref[...] = sf_scr[...]
        sfb_ref[...] = sb_scr[...]


def deltanet_bidir(q, k, v, gb, gbt, s0f, s0b, nb, seq, row_off, tl):
    nblk = seq // tl
    n_chunks = tl // DN_CHUNK
    off_b = row_off // tl
    kdim = DN_HEADS * DN_DK

    def fwd_rb(b, t):
        return off_b + b * nblk + t

    def bwd_rb(b, t):
        return off_b + b * nblk + (nblk - 1 - t)

    def seq_specs(rb):
        spec = pl.BlockSpec((tl, kdim), lambda b, t: (rb(b, t), 0))
        return [spec, spec, spec,
                pl.BlockSpec((tl, LANES), lambda b, t: (rb(b, t), 0)),
                pl.BlockSpec((n_chunks, 4 * DN_HEADS, DN_CHUNK), lambda b, t: (rb(b, t), 0, 0))]

    st_spec = pl.BlockSpec((None, DN_HEADS, DN_DK, DN_DV), lambda b, t: (b, 0, 0, 0))
    st_shape = jax.ShapeDtypeStruct((nb, DN_HEADS, DN_DK, DN_DV), F32)
    o_shape = jax.ShapeDtypeStruct((nb * seq, kdim), BF16)
    return pl.pallas_call(
        functools.partial(_deltanet_bidir_kernel, n_chunks=n_chunks),
        grid=(nb, nblk),
        in_specs=seq_specs(fwd_rb) + seq_specs(bwd_rb) + [st_spec, st_spec],
        out_specs=[
            pl.BlockSpec((tl, kdim), lambda b, t: (b * nblk + t, 0)),
            pl.BlockSpec((tl, kdim), lambda b, t: (b * nblk + nblk - 1 - t, 0)),
            st_spec, st_spec,
        ],
        out_shape=[o_shape, o_shape, st_shape, st_shape],
        scratch_shapes=[pltpu.VMEM((DN_HEADS, DN_DK, DN_DV), F32), pltpu.VMEM((DN_HEADS, DN_DK, DN_DV), F32)],
        compiler_params=_cparams(("arbitrary", "arbitrary")),
        name="deltanet_bidir",
    )(q, k, v, gb, gbt, q, k, v, gb, gbt, s0f, s0b)


def _router_kernel(x_ref, g_ref, sh_ref, sc_ref, wr_ref, br_ref, ltri_ref, f_ref, r_ref, cnt_ref, base_ref):
    @pl.when(pl.program_id(0) == 0)
    def _():
        base_ref[...] = jnp.zeros_like(base_ref)

    h = _norm_mod(x_ref[...], g_ref[...], sh_ref[...], sc_ref[...])
    f_ref[...] = _pack_bf16_pairs(h)
    logits = _dot(h.astype(BF16), wr_ref[...]) + br_ref[...]
    tm = logits.shape[0]
    lane = lax.broadcasted_iota(jnp.int32, (tm, LANES), 1)
    neg = -1e30
    big = 4 * LANES
    is_g = lane < N_GROUPS
    gl = jnp.where(is_g, logits, neg)
    gm = jnp.max(gl, axis=-1, keepdims=True)
    grp = jnp.min(jnp.where(gl == gm, lane, big), axis=-1, keepdims=True)
    psum = jnp.sum(jnp.where(is_g, jnp.exp(gl - gm), 0.0), axis=-1, keepdims=True)
    p_grp = 1.0 / psum
    e_lane = lane - N_GROUPS
    in_grp = jnp.logical_and(jnp.logical_and(e_lane >= 0, e_lane < N_EXPERTS),
                             (e_lane // EXPERTS_PER_GROUP) == grp)
    el = jnp.where(in_grp, logits, neg)
    m1 = jnp.max(el, axis=-1, keepdims=True)
    i1 = jnp.min(jnp.where(el == m1, lane, big), axis=-1, keepdims=True)
    el2 = jnp.where(lane == i1, neg, el)
    m2 = jnp.max(el2, axis=-1, keepdims=True)
    i2 = jnp.min(jnp.where(el2 == m2, lane, big), axis=-1, keepdims=True)
    e21 = jnp.exp(m2 - m1)
    w1 = p_grp / (1.0 + e21)
    w2 = p_grp * e21 / (1.0 + e21)
    e1 = (i1 - N_GROUPS).astype(F32)
    e2 = (i2 - N_GROUPS).astype(F32)
    oh1 = lane == i1
    oh2 = lane == i2
    oh1f = jnp.where(oh1, 1.0, 0.0)
    oh2f = jnp.where(oh2, 1.0, 0.0)
    ltri = ltri_ref[...]
    before1 = _dot(ltri, oh1f.astype(BF16))
    before2 = _dot(ltri, oh2f.astype(BF16))
    cnt1 = jnp.sum(oh1f, axis=0, keepdims=True)
    cnt2 = jnp.sum(oh2f, axis=0, keepdims=True)
    base = base_ref[0:1, :]
    rank1 = jnp.sum(jnp.where(oh1, base + before1, 0.0), axis=-1, keepdims=True)
    rank2 = jnp.sum(jnp.where(oh2, base + cnt1 + before2, 0.0), axis=-1, keepdims=True)
    total = base + cnt1 + cnt2
    base_ref[...] = jnp.broadcast_to(total, base_ref.shape)
    cnt_ref[...] = jnp.broadcast_to(total, cnt_ref.shape)
    vals = (e1, e2, w1, w2, rank1, rank2)
    out = jnp.zeros((tm, LANES), F32)
    for idx, val in enumerate(vals):
        out = jnp.where(lane == idx, val, out)
    r_ref[...] = out


def moe_router(xa, gain, shift, scale, w_router, b_router, tm, seq, nb):
    ta, d = xa.shape
    mrow = _mod_row_map(tm, seq, nb)
    ii = np.arange(tm)
    ltri = jnp.asarray((ii[:, None] > ii[None, :]).astype(np.float32)).astype(BF16)
    return pl.pallas_call(
        _router_kernel,
        grid=(ta // tm,),
        in_specs=[
            pl.BlockSpec((tm, d), lambda i: (i, 0)),
            pl.BlockSpec((1, d), lambda i: (0, 0)),
            pl.BlockSpec((None, 1, d), mrow),
            pl.BlockSpec((None, 1, d), mrow),
            pl.BlockSpec((d, LANES), lambda i: (0, 0)),
            pl.BlockSpec((1, LANES), lambda i: (0, 0)),
            pl.BlockSpec((tm, tm), lambda i: (0, 0)),
        ],
        out_specs=[pl.BlockSpec((tm, d // 2), lambda i: (i, 0)), pl.BlockSpec((tm, LANES), lambda i: (i, 0)),
                   pl.BlockSpec((8, LANES), lambda i: (0, 0))],
        out_shape=[jax.ShapeDtypeStruct((ta, d // 2), U32), jax.ShapeDtypeStruct((ta, LANES), F32),
                   jax.ShapeDtypeStruct((8, LANES), F32)],
        scratch_shapes=[pltpu.VMEM((8, LANES), F32)],
        compiler_params=_cparams(("arbitrary",)),
        name="moe_router",
    )(xa, gain.reshape(1, d), shift, scale, w_router, b_router, ltri)


ROW_DMA_UNROLL = 8


def _issue_row_copies(n_rows, make_copy):
    def trip(i, carry):
        for u in range(ROW_DMA_UNROLL):
            make_copy(i * ROW_DMA_UNROLL + u).start(priority=u % 2)
        return carry

    lax.fori_loop(0, n_rows // ROW_DMA_UNROLL, trip, 0)


def _moe_scatter_kernel(pos_ref, f_ref, xs_in, xs_out, sem):
    del xs_in
    tm = f_ref.shape[0]
    for k in range(TOP_K):
        _issue_row_copies(tm, lambda r, k=k: pltpu.make_async_copy(
            f_ref.at[pl.ds(r, 1)], xs_out.at[pl.ds(pos_ref[0, 0, k * tm + r], 1)], sem))
    for _ in range(2):
        pltpu.make_async_copy(f_ref, xs_out.at[pl.ds(0, tm)], sem).wait()


def moe_scatter(pos_tiles, f, xs_zero, tm):
    ta, d = f.shape
    return pl.pallas_call(
        _moe_scatter_kernel,
        grid=(ta // tm,),
        in_specs=[
            pl.BlockSpec((1, 1, 2 * tm), lambda i: (i, 0, 0), memory_space=pltpu.SMEM),
            pl.BlockSpec((tm, d), lambda i: (i, 0)),
            pl.BlockSpec(memory_space=pl.ANY),
        ],
        out_specs=pl.BlockSpec(memory_space=pl.ANY),
        out_shape=jax.ShapeDtypeStruct(xs_zero.shape, xs_zero.dtype),
        scratch_shapes=[pltpu.SemaphoreType.DMA(())],
        input_output_aliases={2: 0},
        compiler_params=_cparams(("arbitrary",)),
        name="moe_scatter",
    )(pos_tiles, f, xs_zero)


def _moe_ffn_kernel(te_ref, nu_ref, x_ref, wgu_ref, wd_ref, o_ref, wgu_bf, wd_bf):
    i = pl.program_id(0)
    fdim = wd_bf.shape[0]

    @pl.when(i < nu_ref[0])
    def _():
        prev = te_ref[jnp.maximum(i - 1, 0)]
        changed = jnp.logical_or(i == 0, te_ref[i] != prev)

        @pl.when(changed)
        def _():
            wgu_bf[...] = wgu_ref[...].astype(BF16)
            wd_bf[...] = wd_ref[...].astype(BF16)

        x_hi, x_lo = _unpack_bf16_pairs(x_ref[...])
        half = x_hi.shape[1]
        gu = _dot(x_hi.astype(BF16), wgu_bf[:half, :]) + _dot(x_lo.astype(BF16), wgu_bf[half:, :])
        hmid = _silu(gu[:, :fdim]) * gu[:, fdim:]
        o_ref[...] = _pack_bf16_pairs(_dot(hmid.astype(BF16), wd_bf[...]))

    @pl.when(i >= nu_ref[0])
    def _():
        o_ref[...] = jnp.zeros_like(o_ref)


def moe_ffn(tile_expert, n_used, xs, w_gate_up, w_down, layer, tm):
    n_pad, dh = xs.shape
    d = 2 * dh
    f2 = w_gate_up.shape[-1]
    fdim = w_down.shape[-2]
    grid_spec = pltpu.PrefetchScalarGridSpec(
        num_scalar_prefetch=2,
        grid=(n_pad // tm,),
        in_specs=[
            pl.BlockSpec((tm, dh), lambda i, te, nu: (i, 0)),
            pl.BlockSpec((None, None, d, f2), lambda i, te, nu: (layer, te[i], 0, 0)),
            pl.BlockSpec((None, None, fdim, d), lambda i, te, nu: (layer, te[i], 0, 0)),
        ],
        out_specs=pl.BlockSpec((tm, dh), lambda i, te, nu: (i, 0)),
        scratch_shapes=[pltpu.VMEM((d, f2), BF16), pltpu.VMEM((fdim, d), BF16)],
    )
    return pl.pallas_call(
        _moe_ffn_kernel,
        grid_spec=grid_spec,
        out_shape=jax.ShapeDtypeStruct((n_pad, dh), U32),
        compiler_params=_cparams(("arbitrary",)),
        name="moe_ffn",
    )(tile_expert, n_used, xs, w_gate_up, w_down)


def _moe_combine_kernel(pos_ref, x_ref, gate_ref, r_ref, fg_ref, y_hbm, o_ref, ybuf, sem, *, final):
    tm = x_ref.shape[0]
    _issue_row_copies(2 * tm, lambda r: pltpu.make_async_copy(
        y_hbm.at[pl.ds(pos_ref[0, 0, r], 1)], ybuf.at[pl.ds(r, 1)], sem))
    pltpu.make_async_copy(y_hbm.at[pl.ds(0, 2 * tm)], ybuf, sem).wait()
    route = r_ref[...]
    w0, w1 = route[:, 2:3], route[:, 3:4]
    y0_hi, y0_lo = _unpack_bf16_pairs(ybuf[0:tm, :])
    y1_hi, y1_lo = _unpack_bf16_pairs(ybuf[tm:2 * tm, :])
    y = jnp.concatenate([w0 * y0_hi + w1 * y1_hi, w0 * y0_lo + w1 * y1_lo], axis=-1)
    out = x_ref[...] + gate_ref[...] * y
    if final:
        out = out * lax.rsqrt(jnp.mean(out * out, axis=-1, keepdims=True) + EPS) * fg_ref[...]
    o_ref[...] = out


def moe_combine(pos_tiles, xa, gate, route, y_sorted, final_gain, n_rows, final, tm, seq, nb):
    d = xa.shape[1]
    return pl.pallas_call(
        functools.partial(_moe_combine_kernel, final=final),
        grid=(n_rows // tm,),
        in_specs=[
            pl.BlockSpec((1, 1, 2 * tm), lambda i: (i, 0, 0), memory_space=pltpu.SMEM),
            pl.BlockSpec((tm, d), lambda i: (i, 0)),
            pl.BlockSpec((None, 1, d), _mod_row_map(tm, seq, nb)),
            pl.BlockSpec((tm, LANES), lambda i: (i, 0)),
            pl.BlockSpec((1, d), lambda i: (0, 0)),
            pl.BlockSpec(memory_space=pl.ANY),
        ],
        out_specs=pl.BlockSpec((tm, d), lambda i: (i, 0)),
        out_shape=jax.ShapeDtypeStruct((n_rows, d), F32),
        scratch_shapes=[pltpu.VMEM((2 * tm, d // 2), U32), pltpu.SemaphoreType.DMA(())],
        compiler_params=_cparams(("arbitrary",)),
        name="moe_combine",
    )(pos_tiles, xa, gate, route, final_gain.reshape(1, d).astype(F32), y_sorted)


def moe_slots(route, counts, tm_ffn, tm_tok):
    ta = route.shape[0]
    ids = route[:, 0:TOP_K].astype(jnp.int32)
    rank = route[:, 2 * TOP_K:3 * TOP_K].astype(jnp.int32)
    counts = counts[0, N_GROUPS:N_GROUPS + N_EXPERTS].astype(jnp.int32)
    padded = ((counts + tm_ffn - 1) // tm_ffn) * tm_ffn
    ends = jnp.cumsum(padded)
    starts = ends - padded
    experts = jnp.arange(N_EXPERTS, dtype=jnp.int32)
    pos = jnp.sum(jnp.where(ids[..., None] == experts, starts, 0), axis=-1) + rank
    n_tiles = (TOP_K * ta + N_EXPERTS * (tm_ffn - 1)) // tm_ffn
    tile_start = jnp.arange(n_tiles, dtype=jnp.int32) * tm_ffn
    tile_expert = jnp.sum((tile_start[:, None] >= ends[None, :]).astype(jnp.int32), axis=1)
    tile_expert = jnp.minimum(tile_expert, N_EXPERTS - 1)
    n_used = (ends[-1] // tm_ffn).astype(jnp.int32).reshape(1)
    pos_tiles = pos.reshape(ta // tm_tok, tm_tok, TOP_K).transpose(0, 2, 1).reshape(ta // tm_tok, 1, TOP_K * tm_tok)
    return tile_expert, n_used, n_tiles * tm_ffn, pos_tiles


def _seq_flags(t_lat, seq, tc, cseq, tm):
    starts = np.arange(0, t_lat + tc, tm)
    first = np.where(starts < t_lat, starts % seq == 0, (starts - t_lat) % cseq == 0)
    ends = starts + tm
    last = np.where(starts < t_lat, ends % seq == 0, (ends - t_lat) % cseq == 0)
    return jnp.asarray(first.astype(np.int32)), jnp.asarray(last.astype(np.int32))


def kernel(x, c, ctx, c_ctx, w_ada, b_ada, norm_mix, norm_ffn, ev_w_in, ev_q_gain, ev_k_gain, ev_decay_f,
           ev_decay_b, ev_w_out, od_w_in, od_conv, od_a_log_f, od_a_log_b, od_dt_bias_f, od_dt_bias_b,
           od_out_gain, od_w_out, moe_w_group, moe_b_group, moe_w_expert, moe_b_expert, moe_w_gate_up,
           moe_w_down, final_norm_gain):
    nb, seq, d = x.shape
    cseq = ctx.shape[1]
    depth = w_ada.shape[0]
    t_lat = nb * seq
    tc = nb * cseq
    assert nb + 1 <= 8 and seq % cseq == 0 and cseq % RET_CHUNK == 0 and seq % GRID_W == 0

    tm = 512 if tc % 512 == 0 else cseq
    tq = min(256, cseq)
    tk = min(256, cseq)
    tl = 2 * DN_CHUNK
    tm_ffn = 512
    tm_comb = tm

    xa = jnp.concatenate([x.reshape(t_lat, d), ctx.reshape(tc, d)], axis=0)
    c8 = jnp.zeros((8, d), F32).at[:nb].set(c).at[nb].set(c_ctx)
    mod = adaln(c8, w_ada, b_ada)

    tabs = rope_tables(seq, tm)
    first_flags, last_flags = _seq_flags(t_lat, seq, tc, cseq, tm)
    ret_zero = jnp.zeros((nb, RET_HEADS, RET_DK, RET_DV), F32)
    dn_zero = jnp.zeros((nb, DN_HEADS, DN_DK, DN_DV), F32)

    xs = None
    for layer in range(depth):
        m = mod[layer].reshape(8, 6, 1, d)
        sh1, sc1, g1, sh2, sc2, g2 = (m[:, j] for j in range(6))
        i = layer // 2
        if layer % 2 == 0:
            w_in = ev_w_in[i].astype(BF16)
            rq, rk, p, aq, ak, av = inproj_even(xa, norm_mix[layer], sh1, sc1, w_in, tabs, ev_q_gain[i],
                                                ev_k_gain[i], tm, seq, nb)
            dec = jnp.stack([ev_decay_f[i], ev_decay_b[i]]).astype(F32)
            oc, scf, scb = retention(dec, rq, rk, p, ret_zero, ret_zero, nb, cseq, t_lat // cseq)
            ol, _, _ = retention(dec, rq, rk, p, scf, scb, nb, seq, 0)
            kcat = jnp.concatenate([ak[:, :t_lat].reshape(ATT_KV_HEADS, nb, seq, ATT_HD),
                                    ak[:, t_lat:].reshape(ATT_KV_HEADS, nb, cseq, ATT_HD)], axis=2)
            vcat = jnp.concatenate([av[:, :t_lat].reshape(ATT_KV_HEADS, nb, seq, ATT_HD),
                                    av[:, t_lat:].reshape(ATT_KV_HEADS, nb, cseq, ATT_HD)], axis=2)
            lk = seq + cseq
            vtcat = jnp.concatenate([vcat.transpose(0, 1, 3, 2),
                                     jnp.ones((ATT_KV_HEADS, nb, ATT_VT_ROWS - ATT_HD, lk), BF16)], axis=2)
            vtcat = vtcat.reshape(ATT_KV_HEADS, nb, ATT_VT_ROWS, lk // tk, tk).transpose(0, 1, 3, 2, 4)
            aqt = aq.transpose(0, 2, 1)
            att_l = attention(aqt, kcat, vtcat, seq, 0, 0, tq, tk)
            att_c = attention(aqt, kcat, vtcat, cseq, t_lat // tq, seq // tk, tq, tk)
            w_out = ev_w_out[i].astype(BF16)
            k1 = RET_HEADS * RET_DV
            xa = outproj_even(ol, oc, att_l, att_c, w_out[:k1], w_out[k1:], xa, g1, tm, seq, nb)
        else:
            w_in = jnp.pad(od_w_in[i], ((0, 0), (0, ODD_IN_PAD - ODD_IN))).astype(BF16)
            zpad = jnp.zeros((LANES - 2 * DN_HEADS,), F32)
            arow = jnp.concatenate([od_a_log_f[i], od_a_log_b[i], zpad]).reshape(1, LANES).astype(F32)
            brow = jnp.concatenate([od_dt_bias_f[i], od_dt_bias_b[i], zpad]).reshape(1, LANES).astype(F32)
            q, k, v, p, gb = inproj_odd(xa, norm_mix[layer], sh1, sc1, w_in, od_conv[i].astype(F32), arow, brow,
                                        first_flags, last_flags, tm, seq, nb, cseq)
            ta = t_lat + tc
            gbt = gb.reshape(ta // DN_CHUNK, DN_CHUNK, LANES)[:, :, :4 * DN_HEADS].transpose(0, 2, 1)
            oc_f, oc_b, sc_f, sc_b = deltanet_bidir(q, k, v, gb, gbt, dn_zero, dn_zero, nb, cseq, t_lat, tl)
            ol_f, ol_b, _, _ = deltanet_bidir(q, k, v, gb, gbt, sc_f, sc_b, nb, seq, 0, tl)
            xa = outproj_odd(ol_f, oc_f, ol_b, oc_b, p, od_out_gain[i], od_w_out[i].astype(BF16), xa, g1, tm, seq,
                             nb)

        w_router = jnp.pad(jnp.concatenate([moe_w_group[layer], moe_w_expert[layer]], axis=1),
                           ((0, 0), (0, LANES - N_GROUPS - N_EXPERTS))).astype(BF16)
        b_router = jnp.pad(jnp.concatenate([moe_b_group[layer], moe_b_expert[layer]]),
                           (0, LANES - N_GROUPS - N_EXPERTS)).reshape(1, LANES).astype(F32)
        f, route, counts = moe_router(xa, norm_ffn[layer], sh2, sc2, w_router, b_router, tm, seq, nb)
        tile_expert, n_used, n_pad, pos_tiles = moe_slots(route, counts, tm_ffn, tm_comb)
        xs = moe_scatter(pos_tiles, f, jnp.zeros((n_pad, d // 2), U32) if xs is None else xs, tm_comb)
        y_sorted = moe_ffn(tile_expert, n_used, xs, moe_w_gate_up, moe_w_down, layer, tm_ffn)
        last = layer == depth - 1
        xa = moe_combine(pos_tiles, xa, g2, route, y_sorted, final_norm_gain, t_lat if last else t_lat + tc, last,
                         tm_comb, seq, nb)

    return xa.reshape(nb, seq, d)
```

```python
import functools
import math

import numpy as np
import jax
import jax.numpy as jnp
from jax import lax
from jax.experimental import pallas as pl
from jax.experimental.pallas import tpu as pltpu

F32 = jnp.float32
BF16 = jnp.bfloat16
U32 = jnp.uint32
HIGHEST = lax.Precision.HIGHEST

EPS = 1e-6
GRID_W = 64
ROPE_BASE = 10000.0
RET_HEADS, RET_DK, RET_DV, RET_CHUNK = 8, 64, 128, 128
ATT_HEADS, ATT_KV_HEADS, ATT_HD = 8, 2, 64
DN_HEADS, DN_DK, DN_DV, DN_CHUNK, DN_CONV = 8, 128, 128, 64, 3
N_GROUPS, EXPERTS_PER_GROUP, TOP_K = 4, 8, 2
N_EXPERTS = N_GROUPS * EXPERTS_PER_GROUP

EVEN_IN = 2 * RET_HEADS * RET_DK + 2 * RET_HEADS * RET_DV + (ATT_HEADS + 2 * ATT_KV_HEADS) * ATT_HD
EVEN_ATT_COL = 2 * RET_HEADS * RET_DK + 2 * RET_HEADS * RET_DV
DN_QKV = 2 * DN_HEADS * DN_DK + DN_HEADS * DN_DV
ODD_IN = DN_QKV + DN_HEADS * DN_DV + 4 * DN_HEADS
ODD_IN_PAD = ((ODD_IN + 127) // 128) * 128

LANES = 128
VMEM_LIMIT = 56 * 1024 * 1024

NT_DIMS = (((1,), (1,)), ((), ()))
TN_DIMS = (((0,), (0,)), ((), ()))


def _cparams(sem):
    return pltpu.CompilerParams(dimension_semantics=sem, vmem_limit_bytes=VMEM_LIMIT)


def _silu(x):
    return x / (1.0 + jnp.exp(-x))


def _dot(a, b):
    return jnp.dot(a, b, preferred_element_type=F32)


def _adaln_kernel(c_ref, w_ref, b_ref, o_ref):
    s = _silu(c_ref[...])
    o_ref[...] = _dot(s.astype(BF16), w_ref[...].astype(BF16)) + b_ref[...]


def adaln(c8, w_ada, b_ada):
    depth, d, n6 = w_ada.shape
    tn = min(n6, 1536)
    return pl.pallas_call(
        _adaln_kernel,
        grid=(depth, n6 // tn),
        in_specs=[
            pl.BlockSpec((8, d), lambda l, j: (0, 0)),
            pl.BlockSpec((None, d, tn), lambda l, j: (l, 0, j)),
            pl.BlockSpec((None, 1, tn), lambda l, j: (l, 0, j)),
        ],
        out_specs=pl.BlockSpec((None, 8, tn), lambda l, j: (l, 0, j)),
        out_shape=jax.ShapeDtypeStruct((depth, 8, n6), F32),
        compiler_params=_cparams(("arbitrary", "arbitrary")),
        name="adaln",
    )(c8, w_ada, b_ada.reshape(depth, 1, n6))


def _norm_mod(x, gain, shift, scale):
    ms = jnp.mean(x * x, axis=-1, keepdims=True)
    h = x * lax.rsqrt(ms + EPS) * gain
    return h * (1.0 + scale) + shift


def _pack_bf16_pairs(x):
    n = x.shape[1] // 2
    hi = lax.bitcast_convert_type(x[:, :n].astype(BF16).astype(F32), U32)
    lo = lax.bitcast_convert_type(x[:, n:].astype(BF16).astype(F32), U32)
    return hi | (lo >> 16)


def _unpack_bf16_pairs(p):
    hi = lax.bitcast_convert_type(p & jnp.uint32(0xFFFF0000), F32)
    lo = lax.bitcast_convert_type(p << 16, F32)
    return hi, lo


def _mod_row_map(tm, seq, n_lat_batches):
    return lambda i: (jnp.minimum((i * tm) // seq, n_lat_batches), 0, 0)


def _inproj_even_kernel(x_ref, g_ref, sh_ref, sc_ref, w_ref, cos_ref, s1_ref, s2_ref, qg_ref, kg_ref, bd_ref,
                        rq_ref, rk_ref, vg_ref, aq_ref, ak_ref, av_ref):
    hb = _norm_mod(x_ref[...], g_ref[...], sh_ref[...], sc_ref[...]).astype(BF16)
    cos = cos_ref[...]
    s1 = s1_ref[...]
    s2 = s2_ref[...]
    bd = bd_ref[...]
    half = ATT_HD

    def proj(c0, width=LANES):
        return _dot(hb, w_ref[:, c0:c0 + width])

    def rope(x):
        return x * cos + pltpu.roll(x, LANES - 16, 1) * s1 + pltpu.roll(x, 16, 1) * s2

    def head_norm(x, gain):
        sq = x * x
        hi = sq.astype(BF16)
        lo = (sq - hi.astype(F32)).astype(BF16)
        ms = _dot(hi, bd) + _dot(lo, bd)
        return x * lax.rsqrt(ms + EPS) * gain

    wide = 2 * LANES
    qw = RET_HEADS * RET_DK
    for c0 in range(0, qw, wide):
        yq = proj(c0, wide)
        yk = proj(qw + c0, wide)
        for u in range(2):
            cs = slice(c0 + u * LANES, c0 + (u + 1) * LANES)
            us = slice(u * LANES, (u + 1) * LANES)
            rq_ref[:, cs] = rope(yq[:, us]).astype(BF16)
            rk_ref[:, cs] = (rope(yk[:, us]) * RET_DK ** -0.5).astype(BF16)
    vgw = 2 * RET_HEADS * RET_DV
    for c0 in range(0, vgw, 512):
        vg_ref[:, c0:c0 + 512] = proj(2 * qw + c0, 512).astype(BF16)

    qg = qg_ref[...]
    kg = kg_ref[...]
    a0 = EVEN_ATT_COL
    for c0 in range(0, ATT_HEADS * ATT_HD, wide):
        ya = proj(a0 + c0, wide)
        for u in range(2):
            y = rope(head_norm(ya[:, u * LANES:(u + 1) * LANES], qg)) * (ATT_HD ** -0.5 * math.log2(math.e))
            y = y.astype(BF16)
            hd0 = (c0 + u * LANES) // ATT_HD
            aq_ref[hd0] = y[:, :half]
            aq_ref[hd0 + 1] = y[:, half:]
    ykv = proj(a0 + ATT_HEADS * ATT_HD, wide)
    y = rope(head_norm(ykv[:, :LANES], kg)).astype(BF16)
    ak_ref[0] = y[:, :half]
    ak_ref[1] = y[:, half:]
    v = ykv[:, LANES:].astype(BF16)
    av_ref[0] = v[:, :half]
    av_ref[1] = v[:, half:]


def inproj_even(xa, gain, shift, scale, w, tabs, q_gain, k_gain, tm, seq, nb):
    ta, d = xa.shape
    t_lat = nb * seq
    cos_t, s1_t, s2_t = tabs
    n_tab = seq // tm

    def tab_map(i):
        r = i * tm
        return (jnp.where(r < t_lat, (r % seq) // tm, n_tab), 0)

    ii = np.arange(LANES)
    bd = jnp.asarray((ii[:, None] // ATT_HD == ii[None, :] // ATT_HD).astype(np.float32) / ATT_HD).astype(BF16)
    qg = jnp.tile(q_gain.astype(F32), LANES // ATT_HD).reshape(1, LANES)
    kg = jnp.tile(k_gain.astype(F32), LANES // ATT_HD).reshape(1, LANES)
    mrow = _mod_row_map(tm, seq, nb)
    tab_spec = pl.BlockSpec((tm, LANES), tab_map)
    one = lambda i: (0, 0)
    qw = RET_HEADS * RET_DK
    vgw = 2 * RET_HEADS * RET_DV
    return pl.pallas_call(
        _inproj_even_kernel,
        grid=(ta // tm,),
        in_specs=[
            pl.BlockSpec((tm, d), lambda i: (i, 0)),
            pl.BlockSpec((1, d), one),
            pl.BlockSpec((None, 1, d), mrow),
            pl.BlockSpec((None, 1, d), mrow),
            pl.BlockSpec((d, EVEN_IN), one),
            tab_spec, tab_spec, tab_spec,
            pl.BlockSpec((1, LANES), one), pl.BlockSpec((1, LANES), one),
            pl.BlockSpec((LANES, LANES), one),
        ],
        out_specs=[
            pl.BlockSpec((tm, qw), lambda i: (i, 0)),
            pl.BlockSpec((tm, qw), lambda i: (i, 0)),
            pl.BlockSpec((tm, vgw), lambda i: (i, 0)),
            pl.BlockSpec((ATT_HEADS, tm, ATT_HD), lambda i: (0, i, 0)),
            pl.BlockSpec((ATT_KV_HEADS, tm, ATT_HD), lambda i: (0, i, 0)),
            pl.BlockSpec((ATT_KV_HEADS, tm, ATT_HD), lambda i: (0, i, 0)),
        ],
        out_shape=[
            jax.ShapeDtypeStruct((ta, qw), BF16),
            jax.ShapeDtypeStruct((ta, qw), BF16),
            jax.ShapeDtypeStruct((ta, vgw), BF16),
            jax.ShapeDtypeStruct((ATT_HEADS, ta, ATT_HD), BF16),
            jax.ShapeDtypeStruct((ATT_KV_HEADS, ta, ATT_HD), BF16),
            jax.ShapeDtypeStruct((ATT_KV_HEADS, ta, ATT_HD), BF16),
        ],
        compiler_params=_cparams(("arbitrary",)),
        name="inproj_even",
    )(xa, gain.reshape(1, d), shift, scale, w, cos_t, s1_t, s2_t, qg, kg, bd)


def rope_tables(seq, tm):
    nf = ATT_HD // 4
    t = jnp.arange(seq)
    rows = (t // GRID_W).astype(F32)
    cols = (t % GRID_W).astype(F32)
    inv = ROPE_BASE ** (-jnp.arange(nf, dtype=F32) / nf)
    lane = np.arange(LANES)
    axis = (lane % ATT_HD) // (ATT_HD // 2)
    f = lane % nf
    upper = ((lane % (ATT_HD // 2)) >= nf)
    pos = jnp.where(jnp.asarray(axis)[None, :] == 0, rows[:, None], cols[:, None])
    ang = pos * inv[jnp.asarray(f)][None, :]
    cos = jnp.cos(ang)
    sin = jnp.sin(ang)
    s1 = jnp.where(jnp.asarray(upper)[None, :], 0.0, -sin)
    s2 = jnp.where(jnp.asarray(upper)[None, :], sin, 0.0)
    pad1 = jnp.ones((tm, LANES), F32)
    pad0 = jnp.zeros((tm, LANES), F32)
    return (jnp.concatenate([cos, pad1]), jnp.concatenate([s1, pad0]), jnp.concatenate([s2, pad0]))


def _retention_kernel(dec_ref, q_ref, k_ref, v_ref, g_ref, s0f_ref, s0b_ref,
                      o_ref, sff_ref, sfb_ref, st_ref, *, n_chunks, unroll):
    hp = pl.program_id(1)
    C = RET_CHUNK
    dk, dv = RET_DK, RET_DV
    pos = lax.broadcasted_iota(jnp.int32, (C, dk), 0).astype(F32)
    ii = lax.broadcasted_iota(jnp.int32, (C, C), 0)
    jj = lax.broadcasted_iota(jnp.int32, (C, C), 1)
    dpos = (ii - jj).astype(F32)
    heads = range(2)
    qs = [slice(hh * dk, (hh + 1) * dk) for hh in heads]
    vs = [slice(hh * dv, (hh + 1) * dv) for hh in heads]
    w_out, w_in, gcf, gcb, mask = [], [], [], [], []
    for hh in heads:
        h = 2 * hp + hh
        df = dec_ref[0, h]
        db = dec_ref[1, h]
        lf = -jnp.exp(jnp.full((C, C), df, F32))
        lb = -jnp.exp(jnp.full((C, C), db, F32))
        lfk = -jnp.exp(jnp.full((C, dk), df, F32))
        lbk = -jnp.exp(jnp.full((C, dk), db, F32))
        w_out.append(jnp.concatenate([jnp.exp(lfk * (C - 1.0 - pos)), jnp.exp(lbk * pos)], axis=1))
        w_in.append(jnp.concatenate([jnp.exp(lfk * (pos + 1.0)), jnp.exp(lbk * (C - pos))], axis=1))
        gcf.append(jnp.exp(-jnp.exp(jnp.full((dk, dv), df, F32)) * C))
        gcb.append(jnp.exp(-jnp.exp(jnp.full((dk, dv), db, F32)) * C))
        mask.append(jnp.where(dpos > 0, jnp.exp(lf * jnp.maximum(dpos, 0.0)),
                              jnp.where(dpos < 0, jnp.exp(lb * jnp.maximum(-dpos, 0.0)), 2.0)))

    def rows(n):
        return pl.ds(pl.multiple_of(n * C, C), C)

    items = [(u, hh) for u in range(unroll) for hh in heads]

    def sums_body(i, carry):
        kk = {}
        for u, hh in items:
            k = k_ref[rows(i * unroll + u), qs[hh]].astype(F32)
            kk[(u, hh)] = (jnp.concatenate([k, k], axis=1) * w_out[hh]).astype(BF16)
        kv = {(u, hh): lax.dot_general(kk[(u, hh)], v_ref[rows(i * unroll + u), vs[hh]], TN_DIMS,
                                       preferred_element_type=F32) for u, hh in items}
        for u, hh in items:
            st_ref[hh, i * unroll + u] = kv[(u, hh)]
        return carry

    lax.fori_loop(0, n_chunks // unroll, sums_body, 0)

    def scan_body(n, carry):
        n_rev = n_chunks - 1 - n
        out = []
        for hh in heads:
            sf, sb = carry[2 * hh], carry[2 * hh + 1]
            kvf = st_ref[hh, n, 0:dk, :]
            kvb = st_ref[hh, n_rev, dk:2 * dk, :]
            st_ref[hh, n, 0:dk, :] = sf
            st_ref[hh, n_rev, dk:2 * dk, :] = sb
            out += [gcf[hh] * sf + kvf, gcb[hh] * sb + kvb]
        return tuple(out)

    init = tuple(x for hh in heads for x in (s0f_ref[hh], s0b_ref[hh]))
    fin = lax.fori_loop(0, n_chunks, scan_body, init)
    for hh in heads:
        sff_ref[hh] = fin[2 * hh]
        sfb_ref[hh] = fin[2 * hh + 1]

    def out_body(i, carry):
        ns = [i * unroll + u for u in range(unroll)]
        qb = {(u, hh): q_ref[rows(ns[u]), qs[hh]] for u, hh in items}
        sc = {(u, hh): lax.dot_general(qb[(u, hh)], k_ref[rows(ns[u]), qs[hh]], NT_DIMS,
                                       preferred_element_type=F32) for u, hh in items}
        qw = {}
        for it in items:
            q = qb[it].astype(F32)
            qw[it] = (jnp.concatenate([q, q], axis=1) * w_in[it[1]]).astype(BF16)
        o1 = {(u, hh): _dot((sc[(u, hh)] * mask[hh]).astype(BF16), v_ref[rows(ns[u]), vs[hh]]) for u, hh in items}
        o2 = {(u, hh): _dot(qw[(u, hh)], st_ref[hh, ns[u]].astype(BF16)) for u, hh in items}
        for it in items:
            u, hh = it
            n = ns[u]
            o = o1[it] + o2[it]
            o = o * lax.rsqrt(jnp.mean(o * o, axis=-1, keepdims=True) + EPS)
            gate = g_ref[rows(n), vs[hh]].astype(F32)
            o_ref[rows(n), vs[hh]] = (_silu(gate) * o).astype(o_ref.dtype)
        return carry

    lax.fori_loop(0, n_chunks // unroll, out_body, 0)


def retention(dec, rq, rk, p, s0f, s0b, nb, seq, row_off_blocks):
    n_chunks = seq // RET_CHUNK
    hp_n = RET_HEADS // 2
    vcol = 0
    gcol = vcol + RET_HEADS * RET_DV // (2 * RET_DV)
    ta = rq.shape[0]
    st_spec = pl.BlockSpec((None, 2, RET_DK, RET_DV), lambda b, hp, *_: (b, hp, 0, 0))
    grid_spec = pltpu.PrefetchScalarGridSpec(
        num_scalar_prefetch=1,
        grid=(nb, hp_n),
        in_specs=[
            pl.BlockSpec((seq, 2 * RET_DK), lambda b, hp, *_: (row_off_blocks + b, hp)),
            pl.BlockSpec((seq, 2 * RET_DK), lambda b, hp, *_: (row_off_blocks + b, hp)),
            pl.BlockSpec((seq, 2 * RET_DV), lambda b, hp, *_: (row_off_blocks + b, vcol + hp)),
            pl.BlockSpec((seq, 2 * RET_DV), lambda b, hp, *_: (row_off_blocks + b, gcol + hp)),
            st_spec, st_spec,
        ],
        out_specs=[
            pl.BlockSpec((seq, 2 * RET_DV), lambda b, hp, *_: (b, hp)),
            st_spec, st_spec,
        ],
        scratch_shapes=[pltpu.VMEM((2, n_chunks, 2 * RET_DK, RET_DV), F32)],
    )
    st_shape = jax.ShapeDtypeStruct((nb, RET_HEADS, RET_DK, RET_DV), F32)
    return pl.pallas_call(
        functools.partial(_retention_kernel, n_chunks=n_chunks, unroll=math.gcd(n_chunks, 8)),
        grid_spec=grid_spec,
        out_shape=[jax.ShapeDtypeStruct((nb * seq, RET_HEADS * RET_DV), BF16), st_shape, st_shape],
        compiler_params=_cparams(("arbitrary", "arbitrary")),
        name="retention",
    )(dec, rq, rk, p, p, s0f, s0b)


ATT_VT_ROWS = ATT_HD + 16


ATT_PAIRS_PER_TRIP = 8


def _attn_kernel(q_ref, k_ref, vt_ref, o_ref, *s_refs, tk, c_start, c_end, rep):
    tq = q_ref.shape[2]
    sets = (s_refs[:rep], s_refs[rep:])
    last = c_end - 1

    def scores(bufs, j):
        j = jnp.minimum(j, last)
        c0 = pl.multiple_of(j * tk, tk)
        k = k_ref[pl.ds(c0, tk), :]
        mxs = []
        for r in range(rep):
            s = _dot(k, q_ref[r])
            bufs[r][...] = s
            mxs.append(jnp.max(s, axis=0, keepdims=True))
        return tuple(mxs)

    def softmax_pv(bufs, j, mxs, ms, accs):
        vt = vt_ref[j]
        new_m, new_acc = [], []
        for r in range(rep):
            m_new = jnp.maximum(ms[r], mxs[r])
            a = jnp.exp2(ms[r] - m_new)
            p = jnp.exp2(bufs[r][...] - m_new).astype(BF16)
            new_acc.append(a * accs[r] + _dot(vt, p))
            new_m.append(m_new)
        return tuple(new_m), tuple(new_acc)

    def pair(j, mx0, ms, accs):
        mx1 = scores(sets[1], j + 1)
        ms, accs = softmax_pv(sets[0], j, mx0, ms, accs)
        mx0 = scores(sets[0], j + 2)
        ms, accs = softmax_pv(sets[1], j + 1, mx1, ms, accs)
        return mx0, ms, accs

    def trip(t, carry):
        for u in range(ATT_PAIRS_PER_TRIP):
            carry = pair(c_start + 2 * (ATT_PAIRS_PER_TRIP * t + u), *carry)
        return carry

    n_pairs = (c_end - c_start) // 2
    n_trips = n_pairs // ATT_PAIRS_PER_TRIP
    ms = tuple(jnp.full((1, tq), -1e30, F32) for _ in range(rep))
    accs = tuple(jnp.zeros((ATT_VT_ROWS, tq), F32) for _ in range(rep))
    carry = (scores(sets[0], c_start), ms, accs)
    if n_trips:
        carry = lax.fori_loop(0, n_trips, trip, carry)
    for u in range(n_trips * ATT_PAIRS_PER_TRIP, n_pairs):
        carry = pair(c_start + 2 * u, *carry)
    mx0, ms, accs = carry
    if (c_end - c_start) % 2:
        ms, accs = softmax_pv(sets[0], last, mx0, ms, accs)
    outs = [(acc[:ATT_HD, :] / acc[ATT_HD:ATT_HD + 1, :]).T for acc in accs]
    o_ref[...] = jnp.concatenate(outs, axis=-1).astype(o_ref.dtype)


def attention(aq, kcat, vtcat, seq_q, q_off_blocks, c_start, tq, tk):
    rep = ATT_HEADS // ATT_KV_HEADS
    _, nb, lk, _ = kcat.shape
    nq = seq_q // tq
    n_chunks = lk // tk
    return pl.pallas_call(
        functools.partial(_attn_kernel, tk=tk, c_start=c_start, c_end=n_chunks, rep=rep),
        grid=(nb, ATT_KV_HEADS, nq),
        in_specs=[
            pl.BlockSpec((rep, ATT_HD, tq), lambda b, g, i: (g, 0, q_off_blocks + b * nq + i)),
            pl.BlockSpec((None, None, lk, ATT_HD), lambda b, g, i: (g, b, 0, 0)),
            pl.BlockSpec((None, None, n_chunks, ATT_VT_ROWS, tk), lambda b, g, i: (g, b, 0, 0, 0)),
        ],
        out_specs=pl.BlockSpec((tq, rep * ATT_HD), lambda b, g, i: (b * nq + i, g)),
        out_shape=jax.ShapeDtypeStruct((nb * seq_q, ATT_HEADS * ATT_HD), BF16),
        scratch_shapes=[pltpu.VMEM((tk, tq), F32) for _ in range(2 * rep)],
        compiler_params=_cparams(("arbitrary", "arbitrary", "arbitrary")),
        name="attention",
    )(aq, kcat, vtcat)


def _lat_ctx_specs(tm, width, n_lat_tiles):
    return [pl.BlockSpec((tm, width), lambda i: (jnp.minimum(i, n_lat_tiles - 1), 0)),
            pl.BlockSpec((tm, width), lambda i: (jnp.maximum(i - n_lat_tiles, 0), 0))]


def _outproj_even_kernel(r_lat, r_ctx, a_lat, a_ctx, w1_ref, w2_ref, res_ref, gate_ref, *rest, n_lat_tiles):
    route_in, (o_ref, *route_out) = rest[:6], rest[6:]
    is_lat = pl.program_id(0) < n_lat_tiles
    a1 = jnp.where(is_lat, r_lat[...], r_ctx[...])
    a2 = jnp.where(is_lat, a_lat[...], a_ctx[...])
    y = _dot(a1, w1_ref[...]) + _dot(a2, w2_ref[...])
    out = res_ref[...] + gate_ref[...] * y
    o_ref[...] = out
    _route_tile(out, *route_in, *route_out)


def outproj_even(ret_lat, ret_ctx, att_lat, att_ctx, w1, w2, xa, gate, route_params, tm, seq, nb):
    ta, d = xa.shape
    k1, k2 = w1.shape[0], w2.shape[0]
    n_lat_tiles = ret_lat.shape[0] // tm
    r_in, r_args, r_out, r_shape, r_scratch = _route_io(ta, d, tm, seq, nb, *route_params)
    return pl.pallas_call(
        functools.partial(_outproj_even_kernel, n_lat_tiles=n_lat_tiles),
        grid=(ta // tm,),
        in_specs=_lat_ctx_specs(tm, k1, n_lat_tiles) + _lat_ctx_specs(tm, k2, n_lat_tiles) + [
            pl.BlockSpec((k1, d), lambda i: (0, 0)),
            pl.BlockSpec((k2, d), lambda i: (0, 0)),
            pl.BlockSpec((tm, d), lambda i: (i, 0)),
            pl.BlockSpec((None, 1, d), _mod_row_map(tm, seq, nb)),
        ] + r_in,
        out_specs=[pl.BlockSpec((tm, d), lambda i: (i, 0))] + r_out,
        out_shape=[jax.ShapeDtypeStruct((ta, d), F32)] + r_shape,
        scratch_shapes=r_scratch,
        compiler_params=_cparams(("arbitrary",)),
        name="outproj_even",
    )(ret_lat, ret_ctx, att_lat, att_ctx, w1, w2, xa, gate, *r_args)


def _outproj_odd_kernel(f_lat, f_ctx, b_lat, b_ctx, z_ref, og_ref, w_ref, res_ref, gate_ref, *rest, n_lat_tiles):
    route_in, (o_ref, *route_out) = rest[:6], rest[6:]
    is_lat = pl.program_id(0) < n_lat_tiles
    og = og_ref[...]
    parts = []
    for h in range(DN_HEADS):
        cs = slice(h * DN_DV, (h + 1) * DN_DV)
        of = jnp.where(is_lat, f_lat[:, cs], f_ctx[:, cs]).astype(F32)
        ob = jnp.where(is_lat, b_lat[:, cs], b_ctx[:, cs]).astype(F32)
        o = of + ob
        o = o * lax.rsqrt(jnp.mean(o * o, axis=-1, keepdims=True) + EPS) * og
        parts.append((o * _silu(z_ref[:, cs].astype(F32))).astype(BF16))
    a = jnp.concatenate(parts, axis=-1)
    out = res_ref[...] + gate_ref[...] * _dot(a, w_ref[...])
    o_ref[...] = out
    _route_tile(out, *route_in, *route_out)


def outproj_odd(of_lat, of_ctx, ob_lat, ob_ctx, p, out_gain, w, xa, gate, route_params, tm, seq, nb):
    ta, d = xa.shape
    kdim = DN_HEADS * DN_DV
    n_lat_tiles = of_lat.shape[0] // tm
    r_in, r_args, r_out, r_shape, r_scratch = _route_io(ta, d, tm, seq, nb, *route_params)
    return pl.pallas_call(
        functools.partial(_outproj_odd_kernel, n_lat_tiles=n_lat_tiles),
        grid=(ta // tm,),
        in_specs=_lat_ctx_specs(tm, kdim, n_lat_tiles) + _lat_ctx_specs(tm, kdim, n_lat_tiles) + [
            pl.BlockSpec((tm, kdim), lambda i: (i, 0)),
            pl.BlockSpec((1, DN_DV), lambda i: (0, 0)),
            pl.BlockSpec((kdim, d), lambda i: (0, 0)),
            pl.BlockSpec((tm, d), lambda i: (i, 0)),
            pl.BlockSpec((None, 1, d), _mod_row_map(tm, seq, nb)),
        ] + r_in,
        out_specs=[pl.BlockSpec((tm, d), lambda i: (i, 0))] + r_out,
        out_shape=[jax.ShapeDtypeStruct((ta, d), F32)] + r_shape,
        scratch_shapes=r_scratch,
        compiler_params=_cparams(("arbitrary",)),
        name="outproj_odd",
    )(of_lat, of_ctx, ob_lat, ob_ctx, p, out_gain.reshape(1, DN_DV).astype(F32), w, xa, gate, *r_args)


def _inproj_odd_kernel(first_ref, last_ref, x_ref, xp_ref, xn_ref, g_ref, sh_ref, sc_ref, w_ref, cw_ref,
                       arow_ref, brow_ref, q_ref, k_ref, v_ref, z_ref, gb_ref, *, ctx_tile0, cseq):
    i = pl.program_id(0)
    tm = x_ref.shape[0]
    gain, shift, scale = g_ref[...], sh_ref[...], sc_ref[...]
    hrows = xp_ref.shape[0]
    hall = _norm_mod(jnp.concatenate([x_ref[...], xp_ref[...], xn_ref[...]], axis=0), gain, shift, scale).astype(BF16)
    hb = hall[:tm]
    keep_prev = 1.0 - first_ref[i].astype(F32)
    keep_next = 1.0 - last_ref[i].astype(F32)
    row = lax.broadcasted_iota(jnp.int32, (tm, LANES), 0)
    is_first = row == 0
    is_last = row == tm - 1
    inner = cseq < tm
    if inner:
        in_ctx = i >= ctx_tile0
        local = row & (cseq - 1)
        zero_dn = jnp.logical_and(in_ctx, local == 0)
        zero_up = jnp.logical_and(in_ctx, local == cseq - 1)
    n_qk = 2 * DN_HEADS * DN_DK // LANES
    n_q = DN_HEADS * DN_DK // LANES
    outs = (q_ref, k_ref, v_ref)
    wide = 2 * LANES
    for c0 in range(0, DN_QKV, wide):
        yall = _dot(hall, w_ref[:, c0:c0 + wide])
        y2 = yall[:tm]
        yh2 = yall[tm:]
        for u in range(2):
            j = c0 // LANES + u
            us = slice(u * LANES, (u + 1) * LANES)
            x = y2[:, us]
            xp = yh2[hrows - 1:hrows, us] * keep_prev
            xn = yh2[hrows:hrows + 1, us] * keep_next
            x_dn = jnp.where(is_first, xp, pltpu.roll(x, 1, 0))
            x_up = jnp.where(is_last, xn, pltpu.roll(x, tm - 1, 0))
            if inner:
                x_dn = jnp.where(zero_dn, 0.0, x_dn)
                x_up = jnp.where(zero_up, 0.0, x_up)
            w = cw_ref[:, j * LANES:(j + 1) * LANES]
            y = _silu(x_dn * w[0:1, :] + x * w[1:2, :] + x_up * w[2:3, :])
            if j < n_qk:
                y = y * lax.rsqrt(jnp.sum(y * y, axis=-1, keepdims=True) + EPS)
                if j < n_q:
                    y = y * DN_DK ** -0.5
            lj = j % n_q
            outs[j // n_q][:, lj * LANES:(lj + 1) * LANES] = y.astype(BF16)
    zw = DN_HEADS * DN_DV
    for c0 in range(0, zw, 512):
        z_ref[:, c0:c0 + 512] = _dot(hb, w_ref[:, DN_QKV + c0:DN_QKV + c0 + 512]).astype(BF16)

    a = _dot(hb, w_ref[:, DN_QKV + zw:DN_QKV + zw + LANES])
    lane = lax.broadcasted_iota(jnp.int32, (tm, LANES), 1)
    zz = a + brow_ref[...]
    softplus = jnp.maximum(zz, 0.0) + jnp.log(1.0 + jnp.exp(-jnp.abs(zz)))
    g = -jnp.exp(arow_ref[...]) * softplus
    beta = 1.0 / (1.0 + jnp.exp(-a))
    gb_ref[...] = jnp.where(lane < 2 * DN_HEADS, g, jnp.where(lane < 4 * DN_HEADS, beta, 0.0))


def inproj_odd(xa, gain, shift, scale, w, conv_w, arow, brow, first_flags, last_flags, tm, seq, nb, cseq):
    ta, d = xa.shape
    halo = 8
    hb = tm // halo
    n_h = ta // halo
    kdim = DN_HEADS * DN_DK
    assert cseq >= tm or (tm % cseq == 0 and cseq & (cseq - 1) == 0)
    mrow = lambda i, *_: _mod_row_map(tm, seq, nb)(i)
    one = lambda i, *_: (0, 0)
    row_blk = lambda i, *_: (i, 0)
    grid_spec = pltpu.PrefetchScalarGridSpec(
        num_scalar_prefetch=2,
        grid=(ta // tm,),
        in_specs=[
            pl.BlockSpec((tm, d), row_blk),
            pl.BlockSpec((halo, d), lambda i, *_: (jnp.maximum(i * hb - 1, 0), 0)),
            pl.BlockSpec((halo, d), lambda i, *_: (jnp.minimum((i + 1) * hb, n_h - 1), 0)),
            pl.BlockSpec((1, d), one),
            pl.BlockSpec((None, 1, d), mrow),
            pl.BlockSpec((None, 1, d), mrow),
            pl.BlockSpec((d, ODD_IN_PAD), one),
            pl.BlockSpec((DN_CONV, DN_QKV), one),
            pl.BlockSpec((1, LANES), one),
            pl.BlockSpec((1, LANES), one),
        ],
        out_specs=[pl.BlockSpec((tm, kdim), row_blk)] * 4 + [pl.BlockSpec((tm, LANES), row_blk)],
    )
    return pl.pallas_call(
        functools.partial(_inproj_odd_kernel, ctx_tile0=nb * seq // tm, cseq=cseq),
        grid_spec=grid_spec,
        out_shape=[jax.ShapeDtypeStruct((ta, kdim), BF16)] * 4 + [jax.ShapeDtypeStruct((ta, LANES), F32)],
        compiler_params=_cparams(("arbitrary",)),
        name="inproj_odd",
    )(first_flags, last_flags, xa, xa, xa, gain.reshape(1, d), shift, scale, w, conv_w, arow, brow)


def _deltanet_bidir_kernel(qf_ref, kf_ref, vf_ref, gbf_ref, gbtf_ref, qb_ref, kb_ref, vb_ref, gbb_ref, gbtb_ref,
                           s0f_ref, s0b_ref, of_ref, ob_ref, sff_ref, sfb_ref, sf_scr, sb_scr, *, n_chunks):
    t = pl.program_id(1)

    @pl.when(t == 0)
    def _():
        sf_scr[...] = s0f_ref[...]
        sb_scr[...] = s0b_ref[...]

    C = DN_CHUNK
    ii = lax.broadcasted_iota(jnp.int32, (C, C), 0)
    jj = lax.broadcasted_iota(jnp.int32, (C, C), 1)
    lower, upper = ii >= jj, ii <= jj
    eye = jnp.where(ii == jj, 1.0, 0.0).astype(F32)
    blk = ii ^ jj
    dirs = (
        dict(rev=False, incl=lower, strict=ii > jj, q=qf_ref, k=kf_ref, v=vf_ref, gb=gbf_ref, gbt=gbtf_ref,
             o=of_ref, scr=sf_scr, off=0, order=list(range(n_chunks))),
        dict(rev=True, incl=upper, strict=ii < jj, q=qb_ref, k=kb_ref, v=vb_ref, gb=gbb_ref, gbt=gbtb_ref,
             o=ob_ref, scr=sb_scr, off=DN_HEADS, order=list(range(n_chunks - 1, -1, -1))),
    )
    items = [(d, c, h) for d in range(2) for c in dirs[d]["order"] for h in range(DN_HEADS)]

    gcols, grows, gbs = {}, {}, {}
    for d, dr in enumerate(dirs):
        tri = jnp.where(dr["incl"], 1.0, 0.0).astype(F32)
        tri_t = jnp.where(upper if not dr["rev"] else lower, 1.0, 0.0).astype(F32)
        for c in dr["order"]:
            gb_c = dr["gb"][c * C:(c + 1) * C, :]
            gbs[(d, c)] = gb_c
            gcols[(d, c)] = jnp.dot(tri, gb_c, preferred_element_type=F32, precision=HIGHEST)
            grows[(d, c)] = jnp.dot(dr["gbt"][c], tri_t, preferred_element_type=F32, precision=HIGHEST)

    qb, kb16, decay, kbeta, egc, kd, gl, rhs = {}, {}, {}, {}, {}, {}, {}, {}
    for it in items:
        d, c, h = it
        dr = dirs[d]
        gi = dr["off"] + h
        bi = 2 * DN_HEADS + dr["off"] + h
        rows = slice(c * C, (c + 1) * C)
        cs = slice(h * DN_DK, (h + 1) * DN_DK)
        gc = gcols[(d, c)][:, gi:gi + 1]
        gr = grows[(d, c)][gi:gi + 1, :]
        beta = gbs[(d, c)][:, bi:bi + 1]
        qb[it] = dr["q"][rows, cs]
        kb16[it] = dr["k"][rows, cs]
        kf = kb16[it].astype(F32)
        decay[it] = jnp.where(dr["incl"], jnp.exp(jnp.where(dr["incl"], gc - gr, 0.0)), 0.0)
        kbeta[it] = kf * beta
        egc[it] = jnp.exp(gc)
        glast = gc[0:1, :] if dr["rev"] else gc[C - 1:C, :]
        kd[it] = (kf * jnp.exp(glast - gc)).astype(BF16)
        gl[it] = jnp.exp(glast)
        rhs[it] = jnp.concatenate([dr["v"][rows, cs].astype(F32) * beta, kbeta[it] * egc[it]], axis=1).astype(BF16)

    kk = {it: lax.dot_general(kbeta[it].astype(BF16), kb16[it], NT_DIMS, preferred_element_type=F32)
          for it in items}
    qk = {it: lax.dot_general(qb[it], kb16[it], NT_DIMS, preferred_element_type=F32) for it in items}
    lm = {it: jnp.where(dirs[it[0]]["strict"], kk[it] * decay[it], 0.0) for it in items}
    attn = {it: jnp.where(dirs[it[0]]["incl"], qk[it] * decay[it], 0.0).astype(BF16) for it in items}
    dinv = {it: eye - jnp.where(blk < 2, lm[it], 0.0) for it in items}
    s = 2
    while s < C:
        in_band = jnp.logical_and(blk >= s, blk < 2 * s)
        tmp = {it: _dot(dinv[it].astype(BF16), jnp.where(in_band, lm[it], 0.0).astype(BF16)) for it in items}
        dinv = {it: dinv[it] - _dot(tmp[it].astype(BF16), dinv[it].astype(BF16)) for it in items}
        s *= 2
    uw = {it: _dot(dinv[it].astype(BF16), rhs[it]) for it in items}
    wq = {it: jnp.concatenate([uw[it][:, DN_DV:], qb[it].astype(F32) * egc[it]], axis=0).astype(BF16)
          for it in items}

    states = {(d, h): dirs[d]["scr"][h] for d in range(2) for h in range(DN_HEADS)}
    for step in range(n_chunks):
        its = [(d, dirs[d]["order"][step], h) for d in range(2) for h in range(DN_HEADS)]
        r = {it: _dot(wq[it], states[(it[0], it[2])].astype(BF16)) for it in its}
        v_new = {it: (uw[it][:, :DN_DV] - r[it][:C]).astype(BF16) for it in its}
        o = {it: r[it][C:] + _dot(attn[it], v_new[it]) for it in its}
        for it in its:
            key = (it[0], it[2])
            states[key] = states[key] * gl[it] + lax.dot_general(kd[it], v_new[it], TN_DIMS,
                                                                 preferred_element_type=F32)
        for it in its:
            d, c, h = it
            dirs[d]["o"][c * C:(c + 1) * C, h * DN_DK:(h + 1) * DN_DK] = o[it].astype(of_ref.dtype)
    for (d, h), st in states.items():
        dirs[d]["scr"][h] = st

    @pl.when(t == pl.num_programs(1) - 1)
    def _():
        sff_ref[...] = sf_scr[...]
        sfb_ref[...] = sb_scr[...]


def deltanet_bidir(q, k, v, gb, gbt, s0f, s0b, nb, seq, row_off, tl):
    nblk = seq // tl
    n_chunks = tl // DN_CHUNK
    off_b = row_off // tl
    kdim = DN_HEADS * DN_DK

    def fwd_rb(b, t):
        return off_b + b * nblk + t

    def bwd_rb(b, t):
        return off_b + b * nblk + (nblk - 1 - t)

    def seq_specs(rb):
        spec = pl.BlockSpec((tl, kdim), lambda b, t: (rb(b, t), 0))
        return [spec, spec, spec,
                pl.BlockSpec((tl, LANES), lambda b, t: (rb(b, t), 0)),
                pl.BlockSpec((n_chunks, 4 * DN_HEADS, DN_CHUNK), lambda b, t: (rb(b, t), 0, 0))]

    st_spec = pl.BlockSpec((None, DN_HEADS, DN_DK, DN_DV), lambda b, t: (b, 0, 0, 0))
    st_shape = jax.ShapeDtypeStruct((nb, DN_HEADS, DN_DK, DN_DV), F32)
    o_shape = jax.ShapeDtypeStruct((nb * seq, kdim), BF16)
    return pl.pallas_call(
        functools.partial(_deltanet_bidir_kernel, n_chunks=n_chunks),
        grid=(nb, nblk),
        in_specs=seq_specs(fwd_rb) + seq_specs(bwd_rb) + [st_spec, st_spec],
        out_specs=[
            pl.BlockSpec((tl, kdim), lambda b, t: (b * nblk + t, 0)),
            pl.BlockSpec((tl, kdim), lambda b, t: (b * nblk + nblk - 1 - t, 0)),
            st_spec, st_spec,
        ],
        out_shape=[o_shape, o_shape, st_shape, st_shape],
        scratch_shapes=[pltpu.VMEM((DN_HEADS, DN_DK, DN_DV), F32), pltpu.VMEM((DN_HEADS, DN_DK, DN_DV), F32)],
        compiler_params=_cparams(("arbitrary", "arbitrary")),
        name="deltanet_bidir",
    )(q, k, v, gb, gbt, q, k, v, gb, gbt, s0f, s0b)


def _route_tile(x, g_ref, sh_ref, sc_ref, wr_ref, br_ref, ltri_ref, f_ref, r_ref, cnt_ref, base_ref):
    @pl.when(pl.program_id(0) == 0)
    def _():
        base_ref[...] = jnp.zeros_like(base_ref)

    h = _norm_mod(x, g_ref[...], sh_ref[...], sc_ref[...])
    f_ref[...] = _pack_bf16_pairs(h)
    logits = _dot(h.astype(BF16), wr_ref[...]) + br_ref[...]
    tm = logits.shape[0]
    lane = lax.broadcasted_iota(jnp.int32, (tm, LANES), 1)
    neg = -1e30
    big = 4 * LANES
    is_g = lane < N_GROUPS
    gl = jnp.where(is_g, logits, neg)
    gm = jnp.max(gl, axis=-1, keepdims=True)
    grp = jnp.min(jnp.where(gl == gm, lane, big), axis=-1, keepdims=True)
    psum = jnp.sum(jnp.where(is_g, jnp.exp(gl - gm), 0.0), axis=-1, keepdims=True)
    p_grp = 1.0 / psum
    e_lane = lane - N_GROUPS
    in_grp = jnp.logical_and(jnp.logical_and(e_lane >= 0, e_lane < N_EXPERTS),
                             (e_lane // EXPERTS_PER_GROUP) == grp)
    el = jnp.where(in_grp, logits, neg)
    m1 = jnp.max(el, axis=-1, keepdims=True)
    i1 = jnp.min(jnp.where(el == m1, lane, big), axis=-1, keepdims=True)
    el2 = jnp.where(lane == i1, neg, el)
    m2 = jnp.max(el2, axis=-1, keepdims=True)
    i2 = jnp.min(jnp.where(el2 == m2, lane, big), axis=-1, keepdims=True)
    e21 = jnp.exp(m2 - m1)
    w1 = p_grp / (1.0 + e21)
    w2 = p_grp * e21 / (1.0 + e21)
    e1 = (i1 - N_GROUPS).astype(F32)
    e2 = (i2 - N_GROUPS).astype(F32)
    oh1 = lane == i1
    oh2 = lane == i2
    oh1f = jnp.where(oh1, 1.0, 0.0)
    oh2f = jnp.where(oh2, 1.0, 0.0)
    ltri = ltri_ref[...]
    before1 = _dot(ltri, oh1f.astype(BF16))
    before2 = _dot(ltri, oh2f.astype(BF16))
    cnt1 = jnp.sum(oh1f, axis=0, keepdims=True)
    cnt2 = jnp.sum(oh2f, axis=0, keepdims=True)
    base = base_ref[0:1, :]
    rank1 = jnp.sum(jnp.where(oh1, base + before1, 0.0), axis=-1, keepdims=True)
    rank2 = jnp.sum(jnp.where(oh2, base + cnt1 + before2, 0.0), axis=-1, keepdims=True)
    total = base + cnt1 + cnt2
    base_ref[...] = jnp.broadcast_to(total, base_ref.shape)
    cnt_ref[...] = jnp.broadcast_to(total, cnt_ref.shape)
    vals = (e1, e2, w1, w2, rank1, rank2)
    out = jnp.zeros((tm, LANES), F32)
    for idx, val in enumerate(vals):
        out = jnp.where(lane == idx, val, out)
    r_ref[...] = out


def _route_io(ta, d, tm, seq, nb, gain, shift, scale, w_router, b_router):
    mrow = _mod_row_map(tm, seq, nb)
    ii = np.arange(tm)
    ltri = jnp.asarray((ii[:, None] > ii[None, :]).astype(np.float32)).astype(BF16)
    in_specs = [
        pl.BlockSpec((1, d), lambda i: (0, 0)),
        pl.BlockSpec((None, 1, d), mrow),
        pl.BlockSpec((None, 1, d), mrow),
        pl.BlockSpec((d, LANES), lambda i: (0, 0)),
        pl.BlockSpec((1, LANES), lambda i: (0, 0)),
        pl.BlockSpec((tm, tm), lambda i: (0, 0)),
    ]
    args = [gain.reshape(1, d), shift, scale, w_router, b_router, ltri]
    out_specs = [pl.BlockSpec((tm, d // 2), lambda i: (i, 0)), pl.BlockSpec((tm, LANES), lambda i: (i, 0)),
                 pl.BlockSpec((8, LANES), lambda i: (0, 0))]
    out_shape = [jax.ShapeDtypeStruct((ta, d // 2), U32), jax.ShapeDtypeStruct((ta, LANES), F32),
                 jax.ShapeDtypeStruct((8, LANES), F32)]
    return in_specs, args, out_specs, out_shape, [pltpu.VMEM((8, LANES), F32)]


ROW_DMA_UNROLL = 8


def _issue_row_copies(n_rows, make_copy):
    def trip(i, carry):
        for u in range(ROW_DMA_UNROLL):
            make_copy(i * ROW_DMA_UNROLL + u).start(priority=u % 2)
        return carry

    lax.fori_loop(0, n_rows // ROW_DMA_UNROLL, trip, 0)


def _moe_scatter_kernel(pos_ref, f_ref, xs_in, xs_out, sem):
    del xs_in
    tm = f_ref.shape[0]
    for k in range(TOP_K):
        _issue_row_copies(tm, lambda r, k=k: pltpu.make_async_copy(
            f_ref.at[pl.ds(r, 1)], xs_out.at[pl.ds(pos_ref[0, 0, k * tm + r], 1)], sem))
    for _ in range(2):
        pltpu.make_async_copy(f_ref, xs_out.at[pl.ds(0, tm)], sem).wait()


def moe_scatter(pos_tiles, f, xs_zero, tm):
    ta, d = f.shape
    return pl.pallas_call(
        _moe_scatter_kernel,
        grid=(ta // tm,),
        in_specs=[
            pl.BlockSpec((1, 1, 2 * tm), lambda i: (i, 0, 0), memory_space=pltpu.SMEM),
            pl.BlockSpec((tm, d), lambda i: (i, 0)),
            pl.BlockSpec(memory_space=pl.ANY),
        ],
        out_specs=pl.BlockSpec(memory_space=pl.ANY),
        out_shape=jax.ShapeDtypeStruct(xs_zero.shape, xs_zero.dtype),
        scratch_shapes=[pltpu.SemaphoreType.DMA(())],
        input_output_aliases={2: 0},
        compiler_params=_cparams(("arbitrary",)),
        name="moe_scatter",
    )(pos_tiles, f, xs_zero)


def _moe_ffn_kernel(te_ref, nu_ref, x_ref, wgu_ref, wd_ref, o_ref, wgu_bf, wd_bf):
    i = pl.program_id(0)
    fdim = wd_bf.shape[0]

    @pl.when(i < nu_ref[0])
    def _():
        prev = te_ref[jnp.maximum(i - 1, 0)]
        changed = jnp.logical_or(i == 0, te_ref[i] != prev)

        @pl.when(changed)
        def _():
            wgu_bf[...] = wgu_ref[...].astype(BF16)
            wd_bf[...] = wd_ref[...].astype(BF16)

        x_hi, x_lo = _unpack_bf16_pairs(x_ref[...])
        half = x_hi.shape[1]
        gu = _dot(x_hi.astype(BF16), wgu_bf[:half, :]) + _dot(x_lo.astype(BF16), wgu_bf[half:, :])
        hmid = _silu(gu[:, :fdim]) * gu[:, fdim:]
        o_ref[...] = _pack_bf16_pairs(_dot(hmid.astype(BF16), wd_bf[...]))

    @pl.when(i >= nu_ref[0])
    def _():
        o_ref[...] = jnp.zeros_like(o_ref)


def moe_ffn(tile_expert, n_used, xs, w_gate_up, w_down, layer, tm):
    n_pad, dh = xs.shape
    d = 2 * dh
    f2 = w_gate_up.shape[-1]
    fdim = w_down.shape[-2]
    grid_spec = pltpu.PrefetchScalarGridSpec(
        num_scalar_prefetch=2,
        grid=(n_pad // tm,),
        in_specs=[
            pl.BlockSpec((tm, dh), lambda i, te, nu: (i, 0)),
            pl.BlockSpec((None, None, d, f2), lambda i, te, nu: (layer, te[i], 0, 0)),
            pl.BlockSpec((None, None, fdim, d), lambda i, te, nu: (layer, te[i], 0, 0)),
        ],
        out_specs=pl.BlockSpec((tm, dh), lambda i, te, nu: (i, 0)),
        scratch_shapes=[pltpu.VMEM((d, f2), BF16), pltpu.VMEM((fdim, d), BF16)],
    )
    return pl.pallas_call(
        _moe_ffn_kernel,
        grid_spec=grid_spec,
        out_shape=jax.ShapeDtypeStruct((n_pad, dh), U32),
        compiler_params=_cparams(("arbitrary",)),
        name="moe_ffn",
    )(tile_expert, n_used, xs, w_gate_up, w_down)


def _moe_combine_kernel(pos_ref, x_ref, gate_ref, r_ref, fg_ref, y_hbm, o_ref, ybuf, sem, *, final):
    tm = x_ref.shape[0]
    _issue_row_copies(2 * tm, lambda r: pltpu.make_async_copy(
        y_hbm.at[pl.ds(pos_ref[0, 0, r], 1)], ybuf.at[pl.ds(r, 1)], sem))
    pltpu.make_async_copy(y_hbm.at[pl.ds(0, 2 * tm)], ybuf, sem).wait()
    route = r_ref[...]
    w0, w1 = route[:, 2:3], route[:, 3:4]
    y0_hi, y0_lo = _unpack_bf16_pairs(ybuf[0:tm, :])
    y1_hi, y1_lo = _unpack_bf16_pairs(ybuf[tm:2 * tm, :])
    y = jnp.concatenate([w0 * y0_hi + w1 * y1_hi, w0 * y0_lo + w1 * y1_lo], axis=-1)
    out = x_ref[...] + gate_ref[...] * y
    if final:
        out = out * lax.rsqrt(jnp.mean(out * out, axis=-1, keepdims=True) + EPS) * fg_ref[...]
    o_ref[...] = out


def moe_combine(pos_tiles, xa, gate, route, y_sorted, final_gain, n_rows, final, tm, seq, nb):
    d = xa.shape[1]
    return pl.pallas_call(
        functools.partial(_moe_combine_kernel, final=final),
        grid=(n_rows // tm,),
        in_specs=[
            pl.BlockSpec((1, 1, 2 * tm), lambda i: (i, 0, 0), memory_space=pltpu.SMEM),
            pl.BlockSpec((tm, d), lambda i: (i, 0)),
            pl.BlockSpec((None, 1, d), _mod_row_map(tm, seq, nb)),
            pl.BlockSpec((tm, LANES), lambda i: (i, 0)),
            pl.BlockSpec((1, d), lambda i: (0, 0)),
            pl.BlockSpec(memory_space=pl.ANY),
        ],
        out_specs=pl.BlockSpec((tm, d), lambda i: (i, 0)),
        out_shape=jax.ShapeDtypeStruct((n_rows, d), F32),
        scratch_shapes=[pltpu.VMEM((2 * tm, d // 2), U32), pltpu.SemaphoreType.DMA(())],
        compiler_params=_cparams(("arbitrary",)),
        name="moe_combine",
    )(pos_tiles, xa, gate, route, final_gain.reshape(1, d).astype(F32), y_sorted)


def moe_slots(route, counts, tm_ffn, tm_tok):
    ta = route.shape[0]
    ids = route[:, 0:TOP_K].astype(jnp.int32)
    rank = route[:, 2 * TOP_K:3 * TOP_K].astype(jnp.int32)
    counts = counts[0, N_GROUPS:N_GROUPS + N_EXPERTS].astype(jnp.int32)
    padded = ((counts + tm_ffn - 1) // tm_ffn) * tm_ffn
    ends = jnp.cumsum(padded)
    starts = ends - padded
    experts = jnp.arange(N_EXPERTS, dtype=jnp.int32)
    pos = jnp.sum(jnp.where(ids[..., None] == experts, starts, 0), axis=-1) + rank
    n_tiles = (TOP_K * ta + N_EXPERTS * (tm_ffn - 1)) // tm_ffn
    tile_start = jnp.arange(n_tiles, dtype=jnp.int32) * tm_ffn
    tile_expert = jnp.sum((tile_start[:, None] >= ends[None, :]).astype(jnp.int32), axis=1)
    tile_expert = jnp.minimum(tile_expert, N_EXPERTS - 1)
    n_used = (ends[-1] // tm_ffn).astype(jnp.int32).reshape(1)
    pos_tiles = pos.reshape(ta // tm_tok, tm_tok, TOP_K).transpose(0, 2, 1).reshape(ta // tm_tok, 1, TOP_K * tm_tok)
    return tile_expert, n_used, n_tiles * tm_ffn, pos_tiles


def _seq_flags(t_lat, seq, tc, cseq, tm):
    starts = np.arange(0, t_lat + tc, tm)
    first = np.where(starts < t_lat, starts % seq == 0, (starts - t_lat) % cseq == 0)
    ends = starts + tm
    last = np.where(starts < t_lat, ends % seq == 0, (ends - t_lat) % cseq == 0)
    return jnp.asarray(first.astype(np.int32)), jnp.asarray(last.astype(np.int32))


def kernel(x, c, ctx, c_ctx, w_ada, b_ada, norm_mix, norm_ffn, ev_w_in, ev_q_gain, ev_k_gain, ev_decay_f,
           ev_decay_b, ev_w_out, od_w_in, od_conv, od_a_log_f, od_a_log_b, od_dt_bias_f, od_dt_bias_b,
           od_out_gain, od_w_out, moe_w_group, moe_b_group, moe_w_expert, moe_b_expert, moe_w_gate_up,
           moe_w_down, final_norm_gain):
    nb, seq, d = x.shape
    cseq = ctx.shape[1]
    depth = w_ada.shape[0]
    t_lat = nb * seq
    tc = nb * cseq
    assert nb + 1 <= 8 and seq % cseq == 0 and cseq % RET_CHUNK == 0 and seq % GRID_W == 0

    tm = 512 if tc % 512 == 0 else cseq
    tq = min(256, cseq)
    tk = min(256, cseq)
    tl = 2 * DN_CHUNK
    tm_ffn = 512
    tm_comb = tm

    xa = jnp.concatenate([x.reshape(t_lat, d), ctx.reshape(tc, d)], axis=0)
    c8 = jnp.zeros((8, d), F32).at[:nb].set(c).at[nb].set(c_ctx)
    mod = adaln(c8, w_ada, b_ada)

    tabs = rope_tables(seq, tm)
    first_flags, last_flags = _seq_flags(t_lat, seq, tc, cseq, tm)
    ret_zero = jnp.zeros((nb, RET_HEADS, RET_DK, RET_DV), F32)
    dn_zero = jnp.zeros((nb, DN_HEADS, DN_DK, DN_DV), F32)

    xs = None
    for layer in range(depth):
        m = mod[layer].reshape(8, 6, 1, d)
        sh1, sc1, g1, sh2, sc2, g2 = (m[:, j] for j in range(6))
        w_router = jnp.pad(jnp.concatenate([moe_w_group[layer], moe_w_expert[layer]], axis=1),
                           ((0, 0), (0, LANES - N_GROUPS - N_EXPERTS))).astype(BF16)
        b_router = jnp.pad(jnp.concatenate([moe_b_group[layer], moe_b_expert[layer]]),
                           (0, LANES - N_GROUPS - N_EXPERTS)).reshape(1, LANES).astype(F32)
        route_params = (norm_ffn[layer], sh2, sc2, w_router, b_router)
        i = layer // 2
        if layer % 2 == 0:
            w_in = ev_w_in[i].astype(BF16)
            rq, rk, p, aq, ak, av = inproj_even(xa, norm_mix[layer], sh1, sc1, w_in, tabs, ev_q_gain[i],
                                                ev_k_gain[i], tm, seq, nb)
            dec = jnp.stack([ev_decay_f[i], ev_decay_b[i]]).astype(F32)
            oc, scf, scb = retention(dec, rq, rk, p, ret_zero, ret_zero, nb, cseq, t_lat // cseq)
            ol, _, _ = retention(dec, rq, rk, p, scf, scb, nb, seq, 0)
            kcat = jnp.concatenate([ak[:, :t_lat].reshape(ATT_KV_HEADS, nb, seq, ATT_HD),
                                    ak[:, t_lat:].reshape(ATT_KV_HEADS, nb, cseq, ATT_HD)], axis=2)
            vcat = jnp.concatenate([av[:, :t_lat].reshape(ATT_KV_HEADS, nb, seq, ATT_HD),
                                    av[:, t_lat:].reshape(ATT_KV_HEADS, nb, cseq, ATT_HD)], axis=2)
            lk = seq + cseq
            vtcat = jnp.concatenate([vcat.transpose(0, 1, 3, 2),
                                     jnp.ones((ATT_KV_HEADS, nb, ATT_VT_ROWS - ATT_HD, lk), BF16)], axis=2)
            vtcat = vtcat.reshape(ATT_KV_HEADS, nb, ATT_VT_ROWS, lk // tk, tk).transpose(0, 1, 3, 2, 4)
            aqt = aq.transpose(0, 2, 1)
            att_l = attention(aqt, kcat, vtcat, seq, 0, 0, tq, tk)
            att_c = attention(aqt, kcat, vtcat, cseq, t_lat // tq, seq // tk, tq, tk)
            w_out = ev_w_out[i].astype(BF16)
            k1 = RET_HEADS * RET_DV
            xa, f, route, counts = outproj_even(ol, oc, att_l, att_c, w_out[:k1], w_out[k1:], xa, g1, route_params,
                                                tm, seq, nb)
        else:
            w_in = jnp.pad(od_w_in[i], ((0, 0), (0, ODD_IN_PAD - ODD_IN))).astype(BF16)
            zpad = jnp.zeros((LANES - 2 * DN_HEADS,), F32)
            arow = jnp.concatenate([od_a_log_f[i], od_a_log_b[i], zpad]).reshape(1, LANES).astype(F32)
            brow = jnp.concatenate([od_dt_bias_f[i], od_dt_bias_b[i], zpad]).reshape(1, LANES).astype(F32)
            q, k, v, p, gb = inproj_odd(xa, norm_mix[layer], sh1, sc1, w_in, od_conv[i].astype(F32), arow, brow,
                                        first_flags, last_flags, tm, seq, nb, cseq)
            ta = t_lat + tc
            gbt = gb.reshape(ta // DN_CHUNK, DN_CHUNK, LANES)[:, :, :4 * DN_HEADS].transpose(0, 2, 1)
            oc_f, oc_b, sc_f, sc_b = deltanet_bidir(q, k, v, gb, gbt, dn_zero, dn_zero, nb, cseq, t_lat, tl)
            ol_f, ol_b, _, _ = deltanet_bidir(q, k, v, gb, gbt, sc_f, sc_b, nb, seq, 0, tl)
            xa, f, route, counts = outproj_odd(ol_f, oc_f, ol_b, oc_b, p, od_out_gain[i], od_w_out[i].astype(BF16),
                                               xa, g1, route_params, tm, seq, nb)

        tile_expert, n_used, n_pad, pos_tiles = moe_slots(route, counts, tm_ffn, tm_comb)
        xs = moe_scatter(pos_tiles, f, jnp.zeros((n_pad, d // 2), U32) if xs is None else xs, tm_comb)
        y_sorted = moe_ffn(tile_expert, n_used, xs, moe_w_gate_up, moe_w_down, layer, tm_ffn)
        last = layer == depth - 1
        xa = moe_combine(pos_tiles, xa, g2, route, y_sorted, final_norm_gain, t_lat if last else t_lat + tc, last,
                         tm_comb, seq, nb)

    return xa.reshape(nb, seq, d)
```

```python
import functools
import math

import numpy as np
import jax
import jax.numpy as jnp
from jax import lax
from jax.experimental import pallas as pl
from jax.experimental.pallas import tpu as pltpu

F32 = jnp.float32
BF16 = jnp.bfloat16
U32 = jnp.uint32
HIGHEST = lax.Precision.HIGHEST

EPS = 1e-6
GRID_W = 64
ROPE_BASE = 10000.0
RET_HEADS, RET_DK, RET_DV, RET_CHUNK = 8, 64, 128, 128
ATT_HEADS, ATT_KV_HEADS, ATT_HD = 8, 2, 64
DN_HEADS, DN_DK, DN_DV, DN_CHUNK, DN_CONV = 8, 128, 128, 64, 3
N_GROUPS, EXPERTS_PER_GROUP, TOP_K = 4, 8, 2
N_EXPERTS = N_GROUPS * EXPERTS_PER_GROUP

EVEN_IN = 2 * RET_HEADS * RET_DK + 2 * RET_HEADS * RET_DV + (ATT_HEADS + 2 * ATT_KV_HEADS) * ATT_HD
EVEN_ATT_COL = 2 * RET_HEADS * RET_DK + 2 * RET_HEADS * RET_DV
DN_QKV = 2 * DN_HEADS * DN_DK + DN_HEADS * DN_DV
ODD_IN = DN_QKV + DN_HEADS * DN_DV + 4 * DN_HEADS
ODD_IN_PAD = ((ODD_IN + 127) // 128) * 128

LANES = 128
VMEM_LIMIT = 56 * 1024 * 1024

NT_DIMS = (((1,), (1,)), ((), ()))
TN_DIMS = (((0,), (0,)), ((), ()))


def _cparams(sem):
    return pltpu.CompilerParams(dimension_semantics=sem, vmem_limit_bytes=VMEM_LIMIT)


def _silu(x):
    return x / (1.0 + jnp.exp(-x))


def _dot(a, b):
    return jnp.dot(a, b, preferred_element_type=F32)


def _adaln_kernel(c_ref, w_ref, b_ref, o_ref):
    s = _silu(c_ref[...])
    o_ref[...] = _dot(s.astype(BF16), w_ref[...].astype(BF16)) + b_ref[...]


def adaln(c8, w_ada, b_ada):
    depth, d, n6 = w_ada.shape
    tn = min(n6, 1536)
    return pl.pallas_call(
        _adaln_kernel,
        grid=(depth, n6 // tn),
        in_specs=[
            pl.BlockSpec((8, d), lambda l, j: (0, 0)),
            pl.BlockSpec((None, d, tn), lambda l, j: (l, 0, j)),
            pl.BlockSpec((None, 1, tn), lambda l, j: (l, 0, j)),
        ],
        out_specs=pl.BlockSpec((None, 8, tn), lambda l, j: (l, 0, j)),
        out_shape=jax.ShapeDtypeStruct((depth, 8, n6), F32),
        compiler_params=_cparams(("arbitrary", "arbitrary")),
        name="adaln",
    )(c8, w_ada, b_ada.reshape(depth, 1, n6))


def _norm_mod(x, gain, shift, scale):
    ms = jnp.mean(x * x, axis=-1, keepdims=True)
    h = x * lax.rsqrt(ms + EPS) * gain
    return h * (1.0 + scale) + shift


def _pack_bf16_pairs(x):
    n = x.shape[1] // 2
    hi = lax.bitcast_convert_type(x[:, :n].astype(BF16).astype(F32), U32)
    lo = lax.bitcast_convert_type(x[:, n:].astype(BF16).astype(F32), U32)
    return hi | (lo >> 16)


def _unpack_bf16_pairs(p):
    hi = lax.bitcast_convert_type(p & jnp.uint32(0xFFFF0000), F32)
    lo = lax.bitcast_convert_type(p << 16, F32)
    return hi, lo


def _mod_row_map(tm, seq, n_lat_batches):
    return lambda i: (jnp.minimum((i * tm) // seq, n_lat_batches), 0, 0)


def _inproj_even_kernel(x_ref, g_ref, sh_ref, sc_ref, w_ref, cos_ref, s1_ref, s2_ref, qg_ref, kg_ref, bd_ref,
                        rq_ref, rk_ref, vg_ref, aq_ref, ak_ref, av_ref):
    hb = _norm_mod(x_ref[...], g_ref[...], sh_ref[...], sc_ref[...]).astype(BF16)
    cos = cos_ref[...]
    s1 = s1_ref[...]
    s2 = s2_ref[...]
    bd = bd_ref[...]
    half = ATT_HD

    def proj(c0, width=LANES):
        return _dot(hb, w_ref[:, c0:c0 + width])

    def rope(x):
        return x * cos + pltpu.roll(x, LANES - 16, 1) * s1 + pltpu.roll(x, 16, 1) * s2

    def head_norm(x, gain):
        sq = x * x
        hi = sq.astype(BF16)
        lo = (sq - hi.astype(F32)).astype(BF16)
        ms = _dot(hi, bd) + _dot(lo, bd)
        return x * lax.rsqrt(ms + EPS) * gain

    wide = 2 * LANES
    qw = RET_HEADS * RET_DK
    for c0 in range(0, qw, wide):
        yq = proj(c0, wide)
        yk = proj(qw + c0, wide)
        for u in range(2):
            cs = slice(c0 + u * LANES, c0 + (u + 1) * LANES)
            us = slice(u * LANES, (u + 1) * LANES)
            rq_ref[:, cs] = rope(yq[:, us]).astype(BF16)
            rk_ref[:, cs] = (rope(yk[:, us]) * RET_DK ** -0.5).astype(BF16)
    vgw = 2 * RET_HEADS * RET_DV
    for c0 in range(0, vgw, 512):
        vg_ref[:, c0:c0 + 512] = proj(2 * qw + c0, 512).astype(BF16)

    qg = qg_ref[...]
    kg = kg_ref[...]
    a0 = EVEN_ATT_COL
    for c0 in range(0, ATT_HEADS * ATT_HD, wide):
        ya = proj(a0 + c0, wide)
        for u in range(2):
            y = rope(head_norm(ya[:, u * LANES:(u + 1) * LANES], qg)) * (ATT_HD ** -0.5 * math.log2(math.e))
            y = y.astype(BF16)
            hd0 = (c0 + u * LANES) // ATT_HD
            aq_ref[hd0] = y[:, :half]
            aq_ref[hd0 + 1] = y[:, half:]
    ykv = proj(a0 + ATT_HEADS * ATT_HD, wide)
    y = rope(head_norm(ykv[:, :LANES], kg)).astype(BF16)
    ak_ref[0] = y[:, :half]
    ak_ref[1] = y[:, half:]
    v = ykv[:, LANES:].astype(BF16)
    av_ref[0] = v[:, :half]
    av_ref[1] = v[:, half:]


def inproj_even(xa, gain, shift, scale, w, tabs, q_gain, k_gain, tm, seq, nb):
    ta, d = xa.shape
    t_lat = nb * seq
    cos_t, s1_t, s2_t = tabs
    n_tab = seq // tm

    def tab_map(i):
        r = i * tm
        return (jnp.where(r < t_lat, (r % seq) // tm, n_tab), 0)

    ii = np.arange(LANES)
    bd = jnp.asarray((ii[:, None] // ATT_HD == ii[None, :] // ATT_HD).astype(np.float32) / ATT_HD).astype(BF16)
    qg = jnp.tile(q_gain.astype(F32), LANES // ATT_HD).reshape(1, LANES)
    kg = jnp.tile(k_gain.astype(F32), LANES // ATT_HD).reshape(1, LANES)
    mrow = _mod_row_map(tm, seq, nb)
    tab_spec = pl.BlockSpec((tm, LANES), tab_map)
    one = lambda i: (0, 0)
    qw = RET_HEADS * RET_DK
    vgw = 2 * RET_HEADS * RET_DV
    return pl.pallas_call(
        _inproj_even_kernel,
        grid=(ta // tm,),
        in_specs=[
            pl.BlockSpec((tm, d), lambda i: (i, 0)),
            pl.BlockSpec((1, d), one),
            pl.BlockSpec((None, 1, d), mrow),
            pl.BlockSpec((None, 1, d), mrow),
            pl.BlockSpec((d, EVEN_IN), one),
            tab_spec, tab_spec, tab_spec,
            pl.BlockSpec((1, LANES), one), pl.BlockSpec((1, LANES), one),
            pl.BlockSpec((LANES, LANES), one),
        ],
        out_specs=[
            pl.BlockSpec((tm, qw), lambda i: (i, 0)),
            pl.BlockSpec((tm, qw), lambda i: (i, 0)),
            pl.BlockSpec((tm, vgw), lambda i: (i, 0)),
            pl.BlockSpec((ATT_HEADS, tm, ATT_HD), lambda i: (0, i, 0)),
            pl.BlockSpec((ATT_KV_HEADS, tm, ATT_HD), lambda i: (0, i, 0)),
            pl.BlockSpec((ATT_KV_HEADS, tm, ATT_HD), lambda i: (0, i, 0)),
        ],
        out_shape=[
            jax.ShapeDtypeStruct((ta, qw), BF16),
            jax.ShapeDtypeStruct((ta, qw), BF16),
            jax.ShapeDtypeStruct((ta, vgw), BF16),
            jax.ShapeDtypeStruct((ATT_HEADS, ta, ATT_HD), BF16),
            jax.ShapeDtypeStruct((ATT_KV_HEADS, ta, ATT_HD), BF16),
            jax.ShapeDtypeStruct((ATT_KV_HEADS, ta, ATT_HD), BF16),
        ],
        compiler_params=_cparams(("arbitrary",)),
        name="inproj_even",
    )(xa, gain.reshape(1, d), shift, scale, w, cos_t, s1_t, s2_t, qg, kg, bd)


def rope_tables(seq, tm):
    nf = ATT_HD // 4
    t = jnp.arange(seq)
    rows = (t // GRID_W).astype(F32)
    cols = (t % GRID_W).astype(F32)
    inv = ROPE_BASE ** (-jnp.arange(nf, dtype=F32) / nf)
    lane = np.arange(LANES)
    axis = (lane % ATT_HD) // (ATT_HD // 2)
    f = lane % nf
    upper = ((lane % (ATT_HD // 2)) >= nf)
    pos = jnp.where(jnp.asarray(axis)[None, :] == 0, rows[:, None], cols[:, None])
    ang = pos * inv[jnp.asarray(f)][None, :]
    cos = jnp.cos(ang)
    sin = jnp.sin(ang)
    s1 = jnp.where(jnp.asarray(upper)[None, :], 0.0, -sin)
    s2 = jnp.where(jnp.asarray(upper)[None, :], sin, 0.0)
    pad1 = jnp.ones((tm, LANES), F32)
    pad0 = jnp.zeros((tm, LANES), F32)
    return (jnp.concatenate([cos, pad1]), jnp.concatenate([s1, pad0]), jnp.concatenate([s2, pad0]))


def _retention_kernel(dec_ref, q_ref, k_ref, v_ref, g_ref, s0f_ref, s0b_ref,
                      o_ref, sff_ref, sfb_ref, st_ref, *, n_chunks, unroll):
    hp = pl.program_id(1)
    C = RET_CHUNK
    dk, dv = RET_DK, RET_DV
    pos = lax.broadcasted_iota(jnp.int32, (C, dk), 0).astype(F32)
    ii = lax.broadcasted_iota(jnp.int32, (C, C), 0)
    jj = lax.broadcasted_iota(jnp.int32, (C, C), 1)
    dpos = (ii - jj).astype(F32)
    heads = range(2)
    qs = [slice(hh * dk, (hh + 1) * dk) for hh in heads]
    vs = [slice(hh * dv, (hh + 1) * dv) for hh in heads]
    w_out, w_in, gcf, gcb, mask = [], [], [], [], []
    for hh in heads:
        h = 2 * hp + hh
        df = dec_ref[0, h]
        db = dec_ref[1, h]
        lf = -jnp.exp(jnp.full((C, C), df, F32))
        lb = -jnp.exp(jnp.full((C, C), db, F32))
        lfk = -jnp.exp(jnp.full((C, dk), df, F32))
        lbk = -jnp.exp(jnp.full((C, dk), db, F32))
        w_out.append(jnp.concatenate([jnp.exp(lfk * (C - 1.0 - pos)), jnp.exp(lbk * pos)], axis=1))
        w_in.append(jnp.concatenate([jnp.exp(lfk * (pos + 1.0)), jnp.exp(lbk * (C - pos))], axis=1))
        gcf.append(jnp.exp(-jnp.exp(jnp.full((dk, dv), df, F32)) * C))
        gcb.append(jnp.exp(-jnp.exp(jnp.full((dk, dv), db, F32)) * C))
        mask.append(jnp.where(dpos > 0, jnp.exp(lf * jnp.maximum(dpos, 0.0)),
                              jnp.where(dpos < 0, jnp.exp(lb * jnp.maximum(-dpos, 0.0)), 2.0)))

    def rows(n):
        return pl.ds(pl.multiple_of(n * C, C), C)

    items = [(u, hh) for u in range(unroll) for hh in heads]

    def sums_body(i, carry):
        kk = {}
        for u, hh in items:
            k = k_ref[rows(i * unroll + u), qs[hh]].astype(F32)
            kk[(u, hh)] = (jnp.concatenate([k, k], axis=1) * w_out[hh]).astype(BF16)
        kv = {(u, hh): lax.dot_general(kk[(u, hh)], v_ref[rows(i * unroll + u), vs[hh]], TN_DIMS,
                                       preferred_element_type=F32) for u, hh in items}
        for u, hh in items:
            st_ref[hh, i * unroll + u] = kv[(u, hh)]
        return carry

    lax.fori_loop(0, n_chunks // unroll, sums_body, 0)

    def scan_body(n, carry):
        n_rev = n_chunks - 1 - n
        out = []
        for hh in heads:
            sf, sb = carry[2 * hh], carry[2 * hh + 1]
            kvf = st_ref[hh, n, 0:dk, :]
            kvb = st_ref[hh, n_rev, dk:2 * dk, :]
            st_ref[hh, n, 0:dk, :] = sf
            st_ref[hh, n_rev, dk:2 * dk, :] = sb
            out += [gcf[hh] * sf + kvf, gcb[hh] * sb + kvb]
        return tuple(out)

    init = tuple(x for hh in heads for x in (s0f_ref[hh], s0b_ref[hh]))
    fin = lax.fori_loop(0, n_chunks, scan_body, init)
    for hh in heads:
        sff_ref[hh] = fin[2 * hh]
        sfb_ref[hh] = fin[2 * hh + 1]

    def out_body(i, carry):
        ns = [i * unroll + u for u in range(unroll)]
        qb = {(u, hh): q_ref[rows(ns[u]), qs[hh]] for u, hh in items}
        sc = {(u, hh): lax.dot_general(qb[(u, hh)], k_ref[rows(ns[u]), qs[hh]], NT_DIMS,
                                       preferred_element_type=F32) for u, hh in items}
        qw = {}
        for it in items:
            q = qb[it].astype(F32)
            qw[it] = (jnp.concatenate([q, q], axis=1) * w_in[it[1]]).astype(BF16)
        o1 = {(u, hh): _dot((sc[(u, hh)] * mask[hh]).astype(BF16), v_ref[rows(ns[u]), vs[hh]]) for u, hh in items}
        o2 = {(u, hh): _dot(qw[(u, hh)], st_ref[hh, ns[u]].astype(BF16)) for u, hh in items}
        for it in items:
            u, hh = it
            n = ns[u]
            o = o1[it] + o2[it]
            o = o * lax.rsqrt(jnp.mean(o * o, axis=-1, keepdims=True) + EPS)
            gate = g_ref[rows(n), vs[hh]].astype(F32)
            o_ref[rows(n), vs[hh]] = (_silu(gate) * o).astype(o_ref.dtype)
        return carry

    lax.fori_loop(0, n_chunks // unroll, out_body, 0)


def retention(dec, rq, rk, p, s0f, s0b, nb, seq, row_off_blocks):
    n_chunks = seq // RET_CHUNK
    hp_n = RET_HEADS // 2
    vcol = 0
    gcol = vcol + RET_HEADS * RET_DV // (2 * RET_DV)
    ta = rq.shape[0]
    st_spec = pl.BlockSpec((None, 2, RET_DK, RET_DV), lambda b, hp, *_: (b, hp, 0, 0))
    grid_spec = pltpu.PrefetchScalarGridSpec(
        num_scalar_prefetch=1,
        grid=(nb, hp_n),
        in_specs=[
            pl.BlockSpec((seq, 2 * RET_DK), lambda b, hp, *_: (row_off_blocks + b, hp)),
            pl.BlockSpec((seq, 2 * RET_DK), lambda b, hp, *_: (row_off_blocks + b, hp)),
            pl.BlockSpec((seq, 2 * RET_DV), lambda b, hp, *_: (row_off_blocks + b, vcol + hp)),
            pl.BlockSpec((seq, 2 * RET_DV), lambda b, hp, *_: (row_off_blocks + b, gcol + hp)),
            st_spec, st_spec,
        ],
        out_specs=[
            pl.BlockSpec((seq, 2 * RET_DV), lambda b, hp, *_: (b, hp)),
            st_spec, st_spec,
        ],
        scratch_shapes=[pltpu.VMEM((2, n_chunks, 2 * RET_DK, RET_DV), F32)],
    )
    st_shape = jax.ShapeDtypeStruct((nb, RET_HEADS, RET_DK, RET_DV), F32)
    return pl.pallas_call(
        functools.partial(_retention_kernel, n_chunks=n_chunks, unroll=math.gcd(n_chunks, 8)),
        grid_spec=grid_spec,
        out_shape=[jax.ShapeDtypeStruct((nb * seq, RET_HEADS * RET_DV), BF16), st_shape, st_shape],
        compiler_params=_cparams(("arbitrary", "arbitrary")),
        name="retention",
    )(dec, rq, rk, p, p, s0f, s0b)


ATT_VT_ROWS = ATT_HD + 16


ATT_PAIRS_PER_TRIP = 8


def _attn_kernel(q_ref, k_ref, vt_ref, o_ref, *s_refs, tk, c_start, c_end, rep):
    tq = q_ref.shape[2]
    sets = (s_refs[:rep], s_refs[rep:])
    last = c_end - 1

    def scores(bufs, j):
        j = jnp.minimum(j, last)
        c0 = pl.multiple_of(j * tk, tk)
        k = k_ref[pl.ds(c0, tk), :]
        mxs = []
        for r in range(rep):
            s = _dot(k, q_ref[r])
            bufs[r][...] = s
            mxs.append(jnp.max(s, axis=0, keepdims=True))
        return tuple(mxs)

    def softmax_pv(bufs, j, mxs, ms, accs):
        vt = vt_ref[j]
        new_m, new_acc = [], []
        for r in range(rep):
            m_new = jnp.maximum(ms[r], mxs[r])
            a = jnp.exp2(ms[r] - m_new)
            p = jnp.exp2(bufs[r][...] - m_new).astype(BF16)
            new_acc.append(a * accs[r] + _dot(vt, p))
            new_m.append(m_new)
        return tuple(new_m), tuple(new_acc)

    def pair(j, mx0, ms, accs):
        mx1 = scores(sets[1], j + 1)
        ms, accs = softmax_pv(sets[0], j, mx0, ms, accs)
        mx0 = scores(sets[0], j + 2)
        ms, accs = softmax_pv(sets[1], j + 1, mx1, ms, accs)
        return mx0, ms, accs

    def trip(t, carry):
        for u in range(ATT_PAIRS_PER_TRIP):
            carry = pair(c_start + 2 * (ATT_PAIRS_PER_TRIP * t + u), *carry)
        return carry

    n_pairs = (c_end - c_start) // 2
    n_trips = n_pairs // ATT_PAIRS_PER_TRIP
    ms = tuple(jnp.full((1, tq), -1e30, F32) for _ in range(rep))
    accs = tuple(jnp.zeros((ATT_VT_ROWS, tq), F32) for _ in range(rep))
    carry = (scores(sets[0], c_start), ms, accs)
    if n_trips:
        carry = lax.fori_loop(0, n_trips, trip, carry)
    for u in range(n_trips * ATT_PAIRS_PER_TRIP, n_pairs):
        carry = pair(c_start + 2 * u, *carry)
    mx0, ms, accs = carry
    if (c_end - c_start) % 2:
        ms, accs = softmax_pv(sets[0], last, mx0, ms, accs)
    outs = [(acc[:ATT_HD, :] / acc[ATT_HD:ATT_HD + 1, :]).T for acc in accs]
    o_ref[...] = jnp.concatenate(outs, axis=-1).astype(o_ref.dtype)


def attention(aq, kcat, vtcat, seq_q, q_off_blocks, c_start, tq, tk):
    rep = ATT_HEADS // ATT_KV_HEADS
    _, nb, lk, _ = kcat.shape
    nq = seq_q // tq
    n_chunks = lk // tk
    return pl.pallas_call(
        functools.partial(_attn_kernel, tk=tk, c_start=c_start, c_end=n_chunks, rep=rep),
        grid=(nb, ATT_KV_HEADS, nq),
        in_specs=[
            pl.BlockSpec((rep, ATT_HD, tq), lambda b, g, i: (g, 0, q_off_blocks + b * nq + i)),
            pl.BlockSpec((None, None, lk, ATT_HD), lambda b, g, i: (g, b, 0, 0)),
            pl.BlockSpec((None, None, n_chunks, ATT_VT_ROWS, tk), lambda b, g, i: (g, b, 0, 0, 0)),
        ],
        out_specs=pl.BlockSpec((tq, rep * ATT_HD), lambda b, g, i: (b * nq + i, g)),
        out_shape=jax.ShapeDtypeStruct((nb * seq_q, ATT_HEADS * ATT_HD), BF16),
        scratch_shapes=[pltpu.VMEM((tk, tq), F32) for _ in range(2 * rep)],
        compiler_params=_cparams(("arbitrary", "arbitrary", "arbitrary")),
        name="attention",
    )(aq, kcat, vtcat)


def _lat_ctx_specs(tm, width, n_lat_tiles):
    return [pl.BlockSpec((tm, width), lambda i: (jnp.minimum(i, n_lat_tiles - 1), 0)),
            pl.BlockSpec((tm, width), lambda i: (jnp.maximum(i - n_lat_tiles, 0), 0))]


def _outproj_even_kernel(r_lat, r_ctx, a_lat, a_ctx, w1_ref, w2_ref, res_ref, gate_ref, *rest, n_lat_tiles):
    route_in, (o_ref, *route_out) = rest[:6], rest[6:]
    is_lat = pl.program_id(0) < n_lat_tiles
    a1 = jnp.where(is_lat, r_lat[...], r_ctx[...])
    a2 = jnp.where(is_lat, a_lat[...], a_ctx[...])
    y = _dot(a1, w1_ref[...]) + _dot(a2, w2_ref[...])
    out = res_ref[...] + gate_ref[...] * y
    o_ref[...] = out
    _route_tile(out, *route_in, *route_out)


def outproj_even(ret_lat, ret_ctx, att_lat, att_ctx, w1, w2, xa, gate, route_params, tm, seq, nb):
    ta, d = xa.shape
    k1, k2 = w1.shape[0], w2.shape[0]
    n_lat_tiles = ret_lat.shape[0] // tm
    r_in, r_args, r_out, r_shape, r_scratch = _route_io(ta, d, tm, seq, nb, *route_params)
    return pl.pallas_call(
        functools.partial(_outproj_even_kernel, n_lat_tiles=n_lat_tiles),
        grid=(ta // tm,),
        in_specs=_lat_ctx_specs(tm, k1, n_lat_tiles) + _lat_ctx_specs(tm, k2, n_lat_tiles) + [
            pl.BlockSpec((k1, d), lambda i: (0, 0)),
            pl.BlockSpec((k2, d), lambda i: (0, 0)),
            pl.BlockSpec((tm, d), lambda i: (i, 0)),
            pl.BlockSpec((None, 1, d), _mod_row_map(tm, seq, nb)),
        ] + r_in,
        out_specs=[pl.BlockSpec((tm, d), lambda i: (i, 0))] + r_out,
        out_shape=[jax.ShapeDtypeStruct((ta, d), F32)] + r_shape,
        scratch_shapes=r_scratch,
        compiler_params=_cparams(("arbitrary",)),
        name="outproj_even",
    )(ret_lat, ret_ctx, att_lat, att_ctx, w1, w2, xa, gate, *r_args)


def _outproj_odd_kernel(f_lat, f_ctx, b_lat, b_ctx, z_ref, og_ref, w_ref, res_ref, gate_ref, *rest, n_lat_tiles):
    route_in, (o_ref, *route_out) = rest[:6], rest[6:]
    is_lat = pl.program_id(0) < n_lat_tiles
    og = og_ref[...]
    parts = []
    for h in range(DN_HEADS):
        cs = slice(h * DN_DV, (h + 1) * DN_DV)
        of = jnp.where(is_lat, f_lat[:, cs], f_ctx[:, cs]).astype(F32)
        ob = jnp.where(is_lat, b_lat[:, cs], b_ctx[:, cs]).astype(F32)
        o = of + ob
        o = o * lax.rsqrt(jnp.mean(o * o, axis=-1, keepdims=True) + EPS) * og
        parts.append((o * _silu(z_ref[:, cs].astype(F32))).astype(BF16))
    a = jnp.concatenate(parts, axis=-1)
    out = res_ref[...] + gate_ref[...] * _dot(a, w_ref[...])
    o_ref[...] = out
    _route_tile(out, *route_in, *route_out)


def outproj_odd(of_lat, of_ctx, ob_lat, ob_ctx, p, out_gain, w, xa, gate, route_params, tm, seq, nb):
    ta, d = xa.shape
    kdim = DN_HEADS * DN_DV
    n_lat_tiles = of_lat.shape[0] // tm
    r_in, r_args, r_out, r_shape, r_scratch = _route_io(ta, d, tm, seq, nb, *route_params)
    return pl.pallas_call(
        functools.partial(_outproj_odd_kernel, n_lat_tiles=n_lat_tiles),
        grid=(ta // tm,),
        in_specs=_lat_ctx_specs(tm, kdim, n_lat_tiles) + _lat_ctx_specs(tm, kdim, n_lat_tiles) + [
            pl.BlockSpec((tm, kdim), lambda i: (i, 0)),
            pl.BlockSpec((1, DN_DV), lambda i: (0, 0)),
            pl.BlockSpec((kdim, d), lambda i: (0, 0)),
            pl.BlockSpec((tm, d), lambda i: (i, 0)),
            pl.BlockSpec((None, 1, d), _mod_row_map(tm, seq, nb)),
        ] + r_in,
        out_specs=[pl.BlockSpec((tm, d), lambda i: (i, 0))] + r_out,
        out_shape=[jax.ShapeDtypeStruct((ta, d), F32)] + r_shape,
        scratch_shapes=r_scratch,
        compiler_params=_cparams(("arbitrary",)),
        name="outproj_odd",
    )(of_lat, of_ctx, ob_lat, ob_ctx, p, out_gain.reshape(1, DN_DV).astype(F32), w, xa, gate, *r_args)


def _inproj_odd_kernel(first_ref, last_ref, x_ref, xp_ref, xn_ref, g_ref, sh_ref, sc_ref, w_ref, cw_ref,
                       arow_ref, brow_ref, q_ref, k_ref, v_ref, z_ref, gb_ref, *, ctx_tile0, cseq):
    i = pl.program_id(0)
    tm = x_ref.shape[0]
    gain, shift, scale = g_ref[...], sh_ref[...], sc_ref[...]
    hrows = xp_ref.shape[0]
    hall = _norm_mod(jnp.concatenate([x_ref[...], xp_ref[...], xn_ref[...]], axis=0), gain, shift, scale).astype(BF16)
    hb = hall[:tm]
    keep_prev = 1.0 - first_ref[i].astype(F32)
    keep_next = 1.0 - last_ref[i].astype(F32)
    row = lax.broadcasted_iota(jnp.int32, (tm, LANES), 0)
    is_first = row == 0
    is_last = row == tm - 1
    inner = cseq < tm
    if inner:
        in_ctx = i >= ctx_tile0
        local = row & (cseq - 1)
        zero_dn = jnp.logical_and(in_ctx, local == 0)
        zero_up = jnp.logical_and(in_ctx, local == cseq - 1)
    n_qk = 2 * DN_HEADS * DN_DK // LANES
    n_q = DN_HEADS * DN_DK // LANES
    outs = (q_ref, k_ref, v_ref)
    wide = 2 * LANES
    for c0 in range(0, DN_QKV, wide):
        yall = _dot(hall, w_ref[:, c0:c0 + wide])
        y2 = yall[:tm]
        yh2 = yall[tm:]
        for u in range(2):
            j = c0 // LANES + u
            us = slice(u * LANES, (u + 1) * LANES)
            x = y2[:, us]
            xp = yh2[hrows - 1:hrows, us] * keep_prev
            xn = yh2[hrows:hrows + 1, us] * keep_next
            x_dn = jnp.where(is_first, xp, pltpu.roll(x, 1, 0))
            x_up = jnp.where(is_last, xn, pltpu.roll(x, tm - 1, 0))
            if inner:
                x_dn = jnp.where(zero_dn, 0.0, x_dn)
                x_up = jnp.where(zero_up, 0.0, x_up)
            w = cw_ref[:, j * LANES:(j + 1) * LANES]
            y = _silu(x_dn * w[0:1, :] + x * w[1:2, :] + x_up * w[2:3, :])
            if j < n_qk:
                y = y * lax.rsqrt(jnp.sum(y * y, axis=-1, keepdims=True) + EPS)
                if j < n_q:
                    y = y * DN_DK ** -0.5
            lj = j % n_q
            outs[j // n_q][:, lj * LANES:(lj + 1) * LANES] = y.astype(BF16)
    zw = DN_HEADS * DN_DV
    for c0 in range(0, zw, 512):
        z_ref[:, c0:c0 + 512] = _dot(hb, w_ref[:, DN_QKV + c0:DN_QKV + c0 + 512]).astype(BF16)

    a = _dot(hb, w_ref[:, DN_QKV + zw:DN_QKV + zw + LANES])
    lane = lax.broadcasted_iota(jnp.int32, (tm, LANES), 1)
    zz = a + brow_ref[...]
    softplus = jnp.maximum(zz, 0.0) + jnp.log(1.0 + jnp.exp(-jnp.abs(zz)))
    g = -jnp.exp(arow_ref[...]) * softplus
    beta = 1.0 / (1.0 + jnp.exp(-a))
    gb_ref[...] = jnp.where(lane < 2 * DN_HEADS, g, jnp.where(lane < 4 * DN_HEADS, beta, 0.0))


def inproj_odd(xa, gain, shift, scale, w, conv_w, arow, brow, first_flags, last_flags, tm, seq, nb, cseq):
    ta, d = xa.shape
    halo = 8
    hb = tm // halo
    n_h = ta // halo
    kdim = DN_HEADS * DN_DK
    assert cseq >= tm or (tm % cseq == 0 and cseq & (cseq - 1) == 0)
    mrow = lambda i, *_: _mod_row_map(tm, seq, nb)(i)
    one = lambda i, *_: (0, 0)
    row_blk = lambda i, *_: (i, 0)
    grid_spec = pltpu.PrefetchScalarGridSpec(
        num_scalar_prefetch=2,
        grid=(ta // tm,),
        in_specs=[
            pl.BlockSpec((tm, d), row_blk),
            pl.BlockSpec((halo, d), lambda i, *_: (jnp.maximum(i * hb - 1, 0), 0)),
            pl.BlockSpec((halo, d), lambda i, *_: (jnp.minimum((i + 1) * hb, n_h - 1), 0)),
            pl.BlockSpec((1, d), one),
            pl.BlockSpec((None, 1, d), mrow),
            pl.BlockSpec((None, 1, d), mrow),
            pl.BlockSpec((d, ODD_IN_PAD), one),
            pl.BlockSpec((DN_CONV, DN_QKV), one),
            pl.BlockSpec((1, LANES), one),
            pl.BlockSpec((1, LANES), one),
        ],
        out_specs=[pl.BlockSpec((tm, kdim), row_blk)] * 4 + [pl.BlockSpec((tm, LANES), row_blk)],
    )
    return pl.pallas_call(
        functools.partial(_inproj_odd_kernel, ctx_tile0=nb * seq // tm, cseq=cseq),
        grid_spec=grid_spec,
        out_shape=[jax.ShapeDtypeStruct((ta, kdim), BF16)] * 4 + [jax.ShapeDtypeStruct((ta, LANES), F32)],
        compiler_params=_cparams(("arbitrary",)),
        name="inproj_odd",
    )(first_flags, last_flags, xa, xa, xa, gain.reshape(1, d), shift, scale, w, conv_w, arow, brow)


def _deltanet_bidir_kernel(qf_ref, kf_ref, vf_ref, gbf_ref, gbtf_ref, qb_ref, kb_ref, vb_ref, gbb_ref, gbtb_ref,
                           s0f_ref, s0b_ref, of_ref, ob_ref, sff_ref, sfb_ref, sf_scr, sb_scr, *, n_chunks):
    t = pl.program_id(1)

    @pl.when(t == 0)
    def _():
        sf_scr[...] = s0f_ref[...]
        sb_scr[...] = s0b_ref[...]

    C = DN_CHUNK
    ii = lax.broadcasted_iota(jnp.int32, (C, C), 0)
    jj = lax.broadcasted_iota(jnp.int32, (C, C), 1)
    lower, upper = ii >= jj, ii <= jj
    eye = jnp.where(ii == jj, 1.0, 0.0).astype(F32)
    blk = ii ^ jj
    dirs = (
        dict(rev=False, incl=lower, strict=ii > jj, q=qf_ref, k=kf_ref, v=vf_ref, gb=gbf_ref, gbt=gbtf_ref,
             o=of_ref, scr=sf_scr, off=0, order=list(range(n_chunks))),
        dict(rev=True, incl=upper, strict=ii < jj, q=qb_ref, k=kb_ref, v=vb_ref, gb=gbb_ref, gbt=gbtb_ref,
             o=ob_ref, scr=sb_scr, off=DN_HEADS, order=list(range(n_chunks - 1, -1, -1))),
    )
    items = [(d, c, h) for d in range(2) for c in dirs[d]["order"] for h in range(DN_HEADS)]

    gcols, grows, gbs = {}, {}, {}
    for d, dr in enumerate(dirs):
        tri = jnp.where(dr["incl"], 1.0, 0.0).astype(F32)
        tri_t = jnp.where(upper if not dr["rev"] else lower, 1.0, 0.0).astype(F32)
        for c in dr["order"]:
            gb_c = dr["gb"][c * C:(c + 1) * C, :]
            gbs[(d, c)] = gb_c
            gcols[(d, c)] = jnp.dot(tri, gb_c, preferred_element_type=F32, precision=HIGHEST)
            grows[(d, c)] = jnp.dot(dr["gbt"][c], tri_t, preferred_element_type=F32, precision=HIGHEST)

    qb, kb16, decay, kbeta, egc, kd, gl, rhs = {}, {}, {}, {}, {}, {}, {}, {}
    for it in items:
        d, c, h = it
        dr = dirs[d]
        gi = dr["off"] + h
        bi = 2 * DN_HEADS + dr["off"] + h
        rows = slice(c * C, (c + 1) * C)
        cs = slice(h * DN_DK, (h + 1) * DN_DK)
        gc = gcols[(d, c)][:, gi:gi + 1]
        gr = grows[(d, c)][gi:gi + 1, :]
        beta = gbs[(d, c)][:, bi:bi + 1]
        qb[it] = dr["q"][rows, cs]
        kb16[it] = dr["k"][rows, cs]
        kf = kb16[it].astype(F32)
        decay[it] = jnp.where(dr["incl"], jnp.exp(jnp.where(dr["incl"], gc - gr, 0.0)), 0.0)
        kbeta[it] = kf * beta
        egc[it] = jnp.exp(gc)
        glast = gc[0:1, :] if dr["rev"] else gc[C - 1:C, :]
        kd[it] = (kf * jnp.exp(glast - gc)).astype(BF16)
        gl[it] = jnp.exp(glast)
        rhs[it] = jnp.concatenate([dr["v"][rows, cs].astype(F32) * beta, kbeta[it] * egc[it]], axis=1).astype(BF16)

    kk = {it: lax.dot_general(kbeta[it].astype(BF16), kb16[it], NT_DIMS, preferred_element_type=F32)
          for it in items}
    qk = {it: lax.dot_general(qb[it], kb16[it], NT_DIMS, preferred_element_type=F32) for it in items}
    lm = {it: jnp.where(dirs[it[0]]["strict"], kk[it] * decay[it], 0.0) for it in items}
    attn = {it: jnp.where(dirs[it[0]]["incl"], qk[it] * decay[it], 0.0).astype(BF16) for it in items}
    dinv = {it: eye - jnp.where(blk < 2, lm[it], 0.0) for it in items}
    s = 2
    while s < C:
        in_band = jnp.logical_and(blk >= s, blk < 2 * s)
        tmp = {it: _dot(dinv[it].astype(BF16), jnp.where(in_band, lm[it], 0.0).astype(BF16)) for it in items}
        dinv = {it: dinv[it] - _dot(tmp[it].astype(BF16), dinv[it].astype(BF16)) for it in items}
        s *= 2
    uw = {it: _dot(dinv[it].astype(BF16), rhs[it]) for it in items}
    wq = {it: jnp.concatenate([uw[it][:, DN_DV:], qb[it].astype(F32) * egc[it]], axis=0).astype(BF16)
          for it in items}

    states = {(d, h): dirs[d]["scr"][h] for d in range(2) for h in range(DN_HEADS)}
    for step in range(n_chunks):
        its = [(d, dirs[d]["order"][step], h) for d in range(2) for h in range(DN_HEADS)]
        r = {it: _dot(wq[it], states[(it[0], it[2])].astype(BF16)) for it in its}
        v_new = {it: (uw[it][:, :DN_DV] - r[it][:C]).astype(BF16) for it in its}
        o = {it: r[it][C:] + _dot(attn[it], v_new[it]) for it in its}
        for it in its:
            key = (it[0], it[2])
            states[key] = states[key] * gl[it] + lax.dot_general(kd[it], v_new[it], TN_DIMS,
                                                                 preferred_element_type=F32)
        for it in its:
            d, c, h = it
            dirs[d]["o"][c * C:(c + 1) * C, h * DN_DK:(h + 1) * DN_DK] = o[it].astype(of_ref.dtype)
    for (d, h), st in states.items():
        dirs[d]["scr"][h] = st

    @pl.when(t == pl.num_programs(1) - 1)
    def _():
        sff_ref[...] = sf_scr[...]
        sfb_ref[...] = sb_scr[...]


def deltanet_bidir(q, k, v, gb, gbt, s0f, s0b, nb, seq, row_off, tl):
    nblk = seq // tl
    n_chunks = tl // DN_CHUNK
    off_b = row_off // tl
    kdim = DN_HEADS * DN_DK

    def fwd_rb(b, t):
        return off_b + b * nblk + t

    def bwd_rb(b, t):
        return off_b + b * nblk + (nblk - 1 - t)

    def seq_specs(rb):
        spec = pl.BlockSpec((tl, kdim), lambda b, t: (rb(b, t), 0))
        return [spec, spec, spec,
                pl.BlockSpec((tl, LANES), lambda b, t: (rb(b, t), 0)),
                pl.BlockSpec((n_chunks, 4 * DN_HEADS, DN_CHUNK), lambda b, t: (rb(b, t), 0, 0))]

    st_spec = pl.BlockSpec((None, DN_HEADS, DN_DK, DN_DV), lambda b, t: (b, 0, 0, 0))
    st_shape = jax.ShapeDtypeStruct((nb, DN_HEADS, DN_DK, DN_DV), F32)
    o_shape = jax.ShapeDtypeStruct((nb * seq, kdim), BF16)
    return pl.pallas_call(
        functools.partial(_deltanet_bidir_kernel, n_chunks=n_chunks),
        grid=(nb, nblk),
        in_specs=seq_specs(fwd_rb) + seq_specs(bwd_rb) + [st_spec, st_spec],
        out_specs=[
            pl.BlockSpec((tl, kdim), lambda b, t: (b * nblk + t, 0)),
            pl.BlockSpec((tl, kdim), lambda b, t: (b * nblk + nblk - 1 - t, 0)),
            st_spec, st_spec,
        ],
        out_shape=[o_shape, o_shape, st_shape, st_shape],
        scratch_shapes=[pltpu.VMEM((DN_HEADS, DN_DK, DN_DV), F32), pltpu.VMEM((DN_HEADS, DN_DK, DN_DV), F32)],
        compiler_params=_cparams(("arbitrary", "arbitrary")),
        name="deltanet_bidir",
    )(q, k, v, gb, gbt, q, k, v, gb, gbt, s0f, s0b)


def _route_tile(x, g_ref, sh_ref, sc_ref, wr_ref, br_ref, ltri_ref, f_ref, r_ref, cnt_ref, base_ref):
    @pl.when(pl.program_id(0) == 0)
    def _():
        base_ref[...] = jnp.zeros_like(base_ref)

    h = _norm_mod(x, g_ref[...], sh_ref[...], sc_ref[...])
    f_ref[...] = _pack_bf16_pairs(h)
    logits = _dot(h.astype(BF16), wr_ref[...]) + br_ref[...]
    tm = logits.shape[0]
    lane = lax.broadcasted_iota(jnp.int32, (tm, LANES), 1)
    neg = -1e30
    big = 4 * LANES
    is_g = lane < N_GROUPS
    gl = jnp.where(is_g, logits, neg)
    gm = jnp.max(gl, axis=-1, keepdims=True)
    grp = jnp.min(jnp.where(gl == gm, lane, big), axis=-1, keepdims=True)
    psum = jnp.sum(jnp.where(is_g, jnp.exp(gl - gm), 0.0), axis=-1, keepdims=True)
    p_grp = 1.0 / psum
    e_lane = lane - N_GROUPS
    in_grp = jnp.logical_and(jnp.logical_and(e_lane >= 0, e_lane < N_EXPERTS),
                             (e_lane // EXPERTS_PER_GROUP) == grp)
    el = jnp.where(in_grp, logits, neg)
    m1 = jnp.max(el, axis=-1, keepdims=True)
    i1 = jnp.min(jnp.where(el == m1, lane, big), axis=-1, keepdims=True)
    el2 = jnp.where(lane == i1, neg, el)
    m2 = jnp.max(el2, axis=-1, keepdims=True)
    i2 = jnp.min(jnp.where(el2 == m2, lane, big), axis=-1, keepdims=True)
    e21 = jnp.exp(m2 - m1)
    w1 = p_grp / (1.0 + e21)
    w2 = p_grp * e21 / (1.0 + e21)
    e1 = (i1 - N_GROUPS).astype(F32)
    e2 = (i2 - N_GROUPS).astype(F32)
    oh1 = lane == i1
    oh2 = lane == i2
    oh1f = jnp.where(oh1, 1.0, 0.0)
    oh2f = jnp.where(oh2, 1.0, 0.0)
    ltri = ltri_ref[...]
    before1 = _dot(ltri, oh1f.astype(BF16))
    before2 = _dot(ltri, oh2f.astype(BF16))
    cnt1 = jnp.sum(oh1f, axis=0, keepdims=True)
    cnt2 = jnp.sum(oh2f, axis=0, keepdims=True)
    base = base_ref[0:1, :]
    rank1 = jnp.sum(jnp.where(oh1, base + before1, 0.0), axis=-1, keepdims=True)
    rank2 = jnp.sum(jnp.where(oh2, base + cnt1 + before2, 0.0), axis=-1, keepdims=True)
    total = base + cnt1 + cnt2
    base_ref[...] = jnp.broadcast_to(total, base_ref.shape)
    cnt_ref[...] = jnp.broadcast_to(total, cnt_ref.shape)
    vals = (e1, e2, w1, w2, rank1, rank2)
    out = jnp.zeros((tm, LANES), F32)
    for idx, val in enumerate(vals):
        out = jnp.where(lane == idx, val, out)
    r_ref[...] = out


def _route_io(ta, d, tm, seq, nb, gain, shift, scale, w_router, b_router):
    mrow = _mod_row_map(tm, seq, nb)
    ii = np.arange(tm)
    ltri = jnp.asarray((ii[:, None] > ii[None, :]).astype(np.float32)).astype(BF16)
    in_specs = [
        pl.BlockSpec((1, d), lambda i: (0, 0)),
        pl.BlockSpec((None, 1, d), mrow),
        pl.BlockSpec((None, 1, d), mrow),
        pl.BlockSpec((d, LANES), lambda i: (0, 0)),
        pl.BlockSpec((1, LANES), lambda i: (0, 0)),
        pl.BlockSpec((tm, tm), lambda i: (0, 0)),
    ]
    args = [gain.reshape(1, d), shift, scale, w_router, b_router, ltri]
    out_specs = [pl.BlockSpec((tm, d // 2), lambda i: (i, 0)), pl.BlockSpec((tm, LANES), lambda i: (i, 0)),
                 pl.BlockSpec((8, LANES), lambda i: (0, 0))]
    out_shape = [jax.ShapeDtypeStruct((ta, d // 2), U32), jax.ShapeDtypeStruct((ta, LANES), F32),
                 jax.ShapeDtypeStruct((8, LANES), F32)]
    return in_specs, args, out_specs, out_shape, [pltpu.VMEM((8, LANES), F32)]


ROW_DMA_UNROLL = 8


def _issue_row_copies(n_rows, make_copy):
    def trip(i, carry):
        for u in range(ROW_DMA_UNROLL):
            make_copy(i * ROW_DMA_UNROLL + u).start(priority=u % 2)
        return carry

    lax.fori_loop(0, n_rows // ROW_DMA_UNROLL, trip, 0)


def _moe_scatter_kernel(pos_ref, f_ref, xs_in, xs_out, sem):
    del xs_in
    tm = f_ref.shape[0]
    for k in range(TOP_K):
        _issue_row_copies(tm, lambda r, k=k: pltpu.make_async_copy(
            f_ref.at[pl.ds(r, 1)], xs_out.at[pl.ds(pos_ref[0, 0, k * tm + r], 1)], sem))
    for _ in range(2):
        pltpu.make_async_copy(f_ref, xs_out.at[pl.ds(0, tm)], sem).wait()


def moe_scatter(pos_tiles, f, xs_zero, tm):
    ta, d = f.shape
    return pl.pallas_call(
        _moe_scatter_kernel,
        grid=(ta // tm,),
        in_specs=[
            pl.BlockSpec((1, 1, 2 * tm), lambda i: (i, 0, 0), memory_space=pltpu.SMEM),
            pl.BlockSpec((tm, d), lambda i: (i, 0)),
            pl.BlockSpec(memory_space=pl.ANY),
        ],
        out_specs=pl.BlockSpec(memory_space=pl.ANY),
        out_shape=jax.ShapeDtypeStruct(xs_zero.shape, xs_zero.dtype),
        scratch_shapes=[pltpu.SemaphoreType.DMA(())],
        input_output_aliases={2: 0},
        compiler_params=_cparams(("arbitrary",)),
        name="moe_scatter",
    )(pos_tiles, f, xs_zero)


def _moe_ffn_kernel(te_ref, nu_ref, x_ref, wgu_ref, wd_ref, o_ref, wgu_bf, wd_bf):
    i = pl.program_id(0)
    fdim = wd_bf.shape[0]

    @pl.when(i < nu_ref[0])
    def _():
        prev = te_ref[jnp.maximum(i - 1, 0)]
        changed = jnp.logical_or(i == 0, te_ref[i] != prev)

        @pl.when(changed)
        def _():
            wgu_bf[...] = wgu_ref[...].astype(BF16)
            wd_bf[...] = wd_ref[...].astype(BF16)

        x_hi, x_lo = _unpack_bf16_pairs(x_ref[...])
        half = x_hi.shape[1]
        gu = _dot(x_hi.astype(BF16), wgu_bf[:half, :]) + _dot(x_lo.astype(BF16), wgu_bf[half:, :])
        hmid = _silu(gu[:, :fdim]) * gu[:, fdim:]
        o_ref[...] = _pack_bf16_pairs(_dot(hmid.astype(BF16), wd_bf[...]))

    @pl.when(i >= nu_ref[0])
    def _():
        o_ref[...] = jnp.zeros_like(o_ref)


def moe_ffn(tile_expert, n_used, xs, w_gate_up, w_down, layer, tm):
    n_pad, dh = xs.shape
    d = 2 * dh
    f2 = w_gate_up.shape[-1]
    fdim = w_down.shape[-2]
    grid_spec = pltpu.PrefetchScalarGridSpec(
        num_scalar_prefetch=2,
        grid=(n_pad // tm,),
        in_specs=[
            pl.BlockSpec((tm, dh), lambda i, te, nu: (i, 0)),
            pl.BlockSpec((None, None, d, f2), lambda i, te, nu: (layer, te[i], 0, 0)),
            pl.BlockSpec((None, None, fdim, d), lambda i, te, nu: (layer, te[i], 0, 0)),
        ],
        out_specs=pl.BlockSpec((tm, dh), lambda i, te, nu: (i, 0)),
        scratch_shapes=[pltpu.VMEM((d, f2), BF16), pltpu.VMEM((fdim, d), BF16)],
    )
    return pl.pallas_call(
        _moe_ffn_kernel,
        grid_spec=grid_spec,
        out_shape=jax.ShapeDtypeStruct((n_pad, dh), U32),
        compiler_params=_cparams(("arbitrary",)),
        name="moe_ffn",
    )(tile_expert, n_used, xs, w_gate_up, w_down)


def _moe_combine_kernel(pos_ref, posn_ref, x_ref, gate_ref, r_ref, fg_ref, y_hbm, o_ref, ybuf, sem, *, final):
    tm = x_ref.shape[0]
    i = pl.program_id(0)
    n = pl.num_programs(0)

    def gather(p_ref, slot):
        _issue_row_copies(2 * tm, lambda r: pltpu.make_async_copy(
            y_hbm.at[pl.ds(p_ref[0, 0, r], 1)], ybuf.at[slot, pl.ds(r, 1)], sem.at[slot]))

    @pl.when(i == 0)
    def _():
        gather(pos_ref, 0)

    @pl.when(i + 1 < n)
    def _():
        gather(posn_ref, (i + 1) % 2)

    slot = i % 2
    pltpu.make_async_copy(y_hbm.at[pl.ds(0, 2 * tm)], ybuf.at[slot], sem.at[slot]).wait()
    route = r_ref[...]
    w0, w1 = route[:, 2:3], route[:, 3:4]
    y0_hi, y0_lo = _unpack_bf16_pairs(ybuf[slot, 0:tm, :])
    y1_hi, y1_lo = _unpack_bf16_pairs(ybuf[slot, tm:2 * tm, :])
    y = jnp.concatenate([w0 * y0_hi + w1 * y1_hi, w0 * y0_lo + w1 * y1_lo], axis=-1)
    out = x_ref[...] + gate_ref[...] * y
    if final:
        out = out * lax.rsqrt(jnp.mean(out * out, axis=-1, keepdims=True) + EPS) * fg_ref[...]
    o_ref[...] = out


def moe_combine(pos_tiles, xa, gate, route, y_sorted, final_gain, n_rows, final, tm, seq, nb):
    d = xa.shape[1]
    n_tiles = n_rows // tm
    return pl.pallas_call(
        functools.partial(_moe_combine_kernel, final=final),
        grid=(n_tiles,),
        in_specs=[
            pl.BlockSpec((1, 1, 2 * tm), lambda i: (i, 0, 0), memory_space=pltpu.SMEM),
            pl.BlockSpec((1, 1, 2 * tm), lambda i: (jnp.minimum(i + 1, n_tiles - 1), 0, 0),
                         memory_space=pltpu.SMEM),
            pl.BlockSpec((tm, d), lambda i: (i, 0)),
            pl.BlockSpec((None, 1, d), _mod_row_map(tm, seq, nb)),
            pl.BlockSpec((tm, LANES), lambda i: (i, 0)),
            pl.BlockSpec((1, d), lambda i: (0, 0)),
            pl.BlockSpec(memory_space=pl.ANY),
        ],
        out_specs=pl.BlockSpec((tm, d), lambda i: (i, 0)),
        out_shape=jax.ShapeDtypeStruct((n_rows, d), F32),
        scratch_shapes=[pltpu.VMEM((2, 2 * tm, d // 2), U32), pltpu.SemaphoreType.DMA((2,))],
        compiler_params=_cparams(("arbitrary",)),
        name="moe_combine",
    )(pos_tiles, pos_tiles, xa, gate, route, final_gain.reshape(1, d).astype(F32), y_sorted)


def moe_slots(route, counts, tm_ffn, tm_tok):
    ta = route.shape[0]
    ids = route[:, 0:TOP_K].astype(jnp.int32)
    rank = route[:, 2 * TOP_K:3 * TOP_K].astype(jnp.int32)
    counts = counts[0, N_GROUPS:N_GROUPS + N_EXPERTS].astype(jnp.int32)
    padded = ((counts + tm_ffn - 1) // tm_ffn) * tm_ffn
    ends = jnp.cumsum(padded)
    starts = ends - padded
    experts = jnp.arange(N_EXPERTS, dtype=jnp.int32)
    pos = jnp.sum(jnp.where(ids[..., None] == experts, starts, 0), axis=-1) + rank
    n_tiles = (TOP_K * ta + N_EXPERTS * (tm_ffn - 1)) // tm_ffn
    tile_start = jnp.arange(n_tiles, dtype=jnp.int32) * tm_ffn
    tile_expert = jnp.sum((tile_start[:, None] >= ends[None, :]).astype(jnp.int32), axis=1)
    tile_expert = jnp.minimum(tile_expert, N_EXPERTS - 1)
    n_used = (ends[-1] // tm_ffn).astype(jnp.int32).reshape(1)
    pos_tiles = pos.reshape(ta // tm_tok, tm_tok, TOP_K).transpose(0, 2, 1).reshape(ta // tm_tok, 1, TOP_K * tm_tok)
    return tile_expert, n_used, n_tiles * tm_ffn, pos_tiles


def _seq_flags(t_lat, seq, tc, cseq, tm):
    starts = np.arange(0, t_lat + tc, tm)
    first = np.where(starts < t_lat, starts % seq == 0, (starts - t_lat) % cseq == 0)
    ends = starts + tm
    last = np.where(starts < t_lat, ends % seq == 0, (ends - t_lat) % cseq == 0)
    return jnp.asarray(first.astype(np.int32)), jnp.asarray(last.astype(np.int32))


def kernel(x, c, ctx, c_ctx, w_ada, b_ada, norm_mix, norm_ffn, ev_w_in, ev_q_gain, ev_k_gain, ev_decay_f,
           ev_decay_b, ev_w_out, od_w_in, od_conv, od_a_log_f, od_a_log_b, od_dt_bias_f, od_dt_bias_b,
           od_out_gain, od_w_out, moe_w_group, moe_b_group, moe_w_expert, moe_b_expert, moe_w_gate_up,
           moe_w_down, final_norm_gain):
    nb, seq, d = x.shape
    cseq = ctx.shape[1]
    depth = w_ada.shape[0]
    t_lat = nb * seq
    tc = nb * cseq
    assert nb + 1 <= 8 and seq % cseq == 0 and cseq % RET_CHUNK == 0 and seq % GRID_W == 0

    tm = 512 if tc % 512 == 0 else cseq
    tq = min(256, cseq)
    tk = min(256, cseq)
    tl = 2 * DN_CHUNK
    tm_ffn = 512
    tm_comb = tm

    xa = jnp.concatenate([x.reshape(t_lat, d), ctx.reshape(tc, d)], axis=0)
    c8 = jnp.zeros((8, d), F32).at[:nb].set(c).at[nb].set(c_ctx)
    mod = adaln(c8, w_ada, b_ada)

    tabs = rope_tables(seq, tm)
    first_flags, last_flags = _seq_flags(t_lat, seq, tc, cseq, tm)
    ret_zero = jnp.zeros((nb, RET_HEADS, RET_DK, RET_DV), F32)
    dn_zero = jnp.zeros((nb, DN_HEADS, DN_DK, DN_DV), F32)

    xs = None
    for layer in range(depth):
        m = mod[layer].reshape(8, 6, 1, d)
        sh1, sc1, g1, sh2, sc2, g2 = (m[:, j] for j in range(6))
        w_router = jnp.pad(jnp.concatenate([moe_w_group[layer], moe_w_expert[layer]], axis=1),
                           ((0, 0), (0, LANES - N_GROUPS - N_EXPERTS))).astype(BF16)
        b_router = jnp.pad(jnp.concatenate([moe_b_group[layer], moe_b_expert[layer]]),
                           (0, LANES - N_GROUPS - N_EXPERTS)).reshape(1, LANES).astype(F32)
        route_params = (norm_ffn[layer], sh2, sc2, w_router, b_router)
        i = layer // 2
        if layer % 2 == 0:
            w_in = ev_w_in[i].astype(BF16)
            rq, rk, p, aq, ak, av = inproj_even(xa, norm_mix[layer], sh1, sc1, w_in, tabs, ev_q_gain[i],
                                                ev_k_gain[i], tm, seq, nb)
            dec = jnp.stack([ev_decay_f[i], ev_decay_b[i]]).astype(F32)
            oc, scf, scb = retention(dec, rq, rk, p, ret_zero, ret_zero, nb, cseq, t_lat // cseq)
            ol, _, _ = retention(dec, rq, rk, p, scf, scb, nb, seq, 0)
            kcat = jnp.concatenate([ak[:, :t_lat].reshape(ATT_KV_HEADS, nb, seq, ATT_HD),
                                    ak[:, t_lat:].reshape(ATT_KV_HEADS, nb, cseq, ATT_HD)], axis=2)
            vcat = jnp.concatenate([av[:, :t_lat].reshape(ATT_KV_HEADS, nb, seq, ATT_HD),
                                    av[:, t_lat:].reshape(ATT_KV_HEADS, nb, cseq, ATT_HD)], axis=2)
            lk = seq + cseq
            vtcat = jnp.concatenate([vcat.transpose(0, 1, 3, 2),
                                     jnp.ones((ATT_KV_HEADS, nb, ATT_VT_ROWS - ATT_HD, lk), BF16)], axis=2)
            vtcat = vtcat.reshape(ATT_KV_HEADS, nb, ATT_VT_ROWS, lk // tk, tk).transpose(0, 1, 3, 2, 4)
            aqt = aq.transpose(0, 2, 1)
            att_l = attention(aqt, kcat, vtcat, seq, 0, 0, tq, tk)
            att_c = attention(aqt, kcat, vtcat, cseq, t_lat // tq, seq // tk, tq, tk)
            w_out = ev_w_out[i].astype(BF16)
            k1 = RET_HEADS * RET_DV
            xa, f, route, counts = outproj_even(ol, oc, att_l, att_c, w_out[:k1], w_out[k1:], xa, g1, route_params,
                                                tm, seq, nb)
        else:
            w_in = jnp.pad(od_w_in[i], ((0, 0), (0, ODD_IN_PAD - ODD_IN))).astype(BF16)
            zpad = jnp.zeros((LANES - 2 * DN_HEADS,), F32)
            arow = jnp.concatenate([od_a_log_f[i], od_a_log_b[i], zpad]).reshape(1, LANES).astype(F32)
            brow = jnp.concatenate([od_dt_bias_f[i], od_dt_bias_b[i], zpad]).reshape(1, LANES).astype(F32)
            q, k, v, p, gb = inproj_odd(xa, norm_mix[layer], sh1, sc1, w_in, od_conv[i].astype(F32), arow, brow,
                                        first_flags, last_flags, tm, seq, nb, cseq)
            ta = t_lat + tc
            gbt = gb.reshape(ta // DN_CHUNK, DN_CHUNK, LANES)[:, :, :4 * DN_HEADS].transpose(0, 2, 1)
            oc_f, oc_b, sc_f, sc_b = deltanet_bidir(q, k, v, gb, gbt, dn_zero, dn_zero, nb, cseq, t_lat, tl)
            ol_f, ol_b, _, _ = deltanet_bidir(q, k, v, gb, gbt, sc_f, sc_b, nb, seq, 0, tl)
            xa, f, route, counts = outproj_odd(ol_f, oc_f, ol_b, oc_b, p, od_out_gain[i], od_w_out[i].astype(BF16),
                                               xa, g1, route_params, tm, seq, nb)

        tile_expert, n_used, n_pad, pos_tiles = moe_slots(route, counts, tm_ffn, tm_comb)
        xs = moe_scatter(pos_tiles, f, jnp.zeros((n_pad, d // 2), U32) if xs is None else xs, tm_comb)
        y_sorted = moe_ffn(tile_expert, n_used, xs, moe_w_gate_up, moe_w_down, layer, tm_ffn)
        last = layer == depth - 1
        xa = moe_combine(pos_tiles, xa, g2, route, y_sorted, final_norm_gain, t_lat if last else t_lat + tc, last,
                         tm_comb, seq, nb)

    return xa.reshape(nb, seq, d)
```

```python
import functools
import math

import numpy as np
import jax
import jax.numpy as jnp
from jax import lax
from jax.experimental import pallas as pl
from jax.experimental.pallas import tpu as pltpu

F32 = jnp.float32
BF16 = jnp.bfloat16
U32 = jnp.uint32
HIGHEST = lax.Precision.HIGHEST

EPS = 1e-6
GRID_W = 64
ROPE_BASE = 10000.0
RET_HEADS, RET_DK, RET_DV, RET_CHUNK = 8, 64, 128, 128
ATT_HEADS, ATT_KV_HEADS, ATT_HD = 8, 2, 64
DN_HEADS, DN_DK, DN_DV, DN_CHUNK, DN_CONV = 8, 128, 128, 64, 3
N_GROUPS, EXPERTS_PER_GROUP, TOP_K = 4, 8, 2
N_EXPERTS = N_GROUPS * EXPERTS_PER_GROUP

EVEN_IN = 2 * RET_HEADS * RET_DK + 2 * RET_HEADS * RET_DV + (ATT_HEADS + 2 * ATT_KV_HEADS) * ATT_HD
EVEN_ATT_COL = 2 * RET_HEADS * RET_DK + 2 * RET_HEADS * RET_DV
DN_QKV = 2 * DN_HEADS * DN_DK + DN_HEADS * DN_DV
ODD_IN = DN_QKV + DN_HEADS * DN_DV + 4 * DN_HEADS
ODD_IN_PAD = ((ODD_IN + 127) // 128) * 128

LANES = 128
VMEM_LIMIT = 56 * 1024 * 1024

NT_DIMS = (((1,), (1,)), ((), ()))
TN_DIMS = (((0,), (0,)), ((), ()))


def _cparams(sem):
    return pltpu.CompilerParams(dimension_semantics=sem, vmem_limit_bytes=VMEM_LIMIT)


def _silu(x):
    return x / (1.0 + jnp.exp(-x))


def _dot(a, b):
    return jnp.dot(a, b, preferred_element_type=F32)


def _adaln_kernel(c_ref, w_ref, b_ref, o_ref):
    s = _silu(c_ref[...])
    o_ref[...] = _dot(s.astype(BF16), w_ref[...].astype(BF16)) + b_ref[...]


def adaln(c8, w_ada, b_ada):
    depth, d, n6 = w_ada.shape
    tn = min(n6, 1536)
    return pl.pallas_call(
        _adaln_kernel,
        grid=(depth, n6 // tn),
        in_specs=[
            pl.BlockSpec((8, d), lambda l, j: (0, 0)),
            pl.BlockSpec((None, d, tn), lambda l, j: (l, 0, j)),
            pl.BlockSpec((None, 1, tn), lambda l, j: (l, 0, j)),
        ],
        out_specs=pl.BlockSpec((None, 8, tn), lambda l, j: (l, 0, j)),
        out_shape=jax.ShapeDtypeStruct((depth, 8, n6), F32),
        compiler_params=_cparams(("arbitrary", "arbitrary")),
        name="adaln",
    )(c8, w_ada, b_ada.reshape(depth, 1, n6))


def _norm_mod(x, gain, shift, scale):
    ms = jnp.mean(x * x, axis=-1, keepdims=True)
    h = x * lax.rsqrt(ms + EPS) * gain
    return h * (1.0 + scale) + shift


def _pack_bf16_pairs(x):
    n = x.shape[1] // 2
    hi = lax.bitcast_convert_type(x[:, :n].astype(BF16).astype(F32), U32)
    lo = lax.bitcast_convert_type(x[:, n:].astype(BF16).astype(F32), U32)
    return hi | (lo >> 16)


def _unpack_bf16_pairs(p):
    hi = lax.bitcast_convert_type(p & jnp.uint32(0xFFFF0000), F32)
    lo = lax.bitcast_convert_type(p << 16, F32)
    return hi, lo


def _mod_row_map(tm, seq, n_lat_batches):
    return lambda i: (jnp.minimum((i * tm) // seq, n_lat_batches), 0, 0)


def _inproj_even_kernel(x_ref, g_ref, sh_ref, sc_ref, w_ref, cos_ref, s1_ref, s2_ref, qg_ref, kg_ref, bd_ref,
                        rq_ref, rk_ref, vg_ref, aq_ref, ak_ref, av_ref):
    hb = _norm_mod(x_ref[...], g_ref[...], sh_ref[...], sc_ref[...]).astype(BF16)
    cos = cos_ref[...]
    s1 = s1_ref[...]
    s2 = s2_ref[...]
    bd = bd_ref[...]
    half = ATT_HD

    def proj(c0, width=LANES):
        return _dot(hb, w_ref[:, c0:c0 + width])

    def rope(x):
        return x * cos + pltpu.roll(x, LANES - 16, 1) * s1 + pltpu.roll(x, 16, 1) * s2

    def head_norm(x, gain):
        sq = x * x
        hi = sq.astype(BF16)
        lo = (sq - hi.astype(F32)).astype(BF16)
        ms = _dot(hi, bd) + _dot(lo, bd)
        return x * lax.rsqrt(ms + EPS) * gain

    wide = 2 * LANES
    qw = RET_HEADS * RET_DK
    for c0 in range(0, qw, wide):
        yq = proj(c0, wide)
        yk = proj(qw + c0, wide)
        for u in range(2):
            cs = slice(c0 + u * LANES, c0 + (u + 1) * LANES)
            us = slice(u * LANES, (u + 1) * LANES)
            rq_ref[:, cs] = rope(yq[:, us]).astype(BF16)
            rk_ref[:, cs] = (rope(yk[:, us]) * RET_DK ** -0.5).astype(BF16)
    vgw = 2 * RET_HEADS * RET_DV
    for c0 in range(0, vgw, 512):
        vg_ref[:, c0:c0 + 512] = proj(2 * qw + c0, 512).astype(BF16)

    qg = qg_ref[...]
    kg = kg_ref[...]
    a0 = EVEN_ATT_COL
    for c0 in range(0, ATT_HEADS * ATT_HD, wide):
        ya = proj(a0 + c0, wide)
        for u in range(2):
            y = rope(head_norm(ya[:, u * LANES:(u + 1) * LANES], qg)) * (ATT_HD ** -0.5 * math.log2(math.e))
            y = y.astype(BF16)
            hd0 = (c0 + u * LANES) // ATT_HD
            aq_ref[hd0] = y[:, :half]
            aq_ref[hd0 + 1] = y[:, half:]
    ykv = proj(a0 + ATT_HEADS * ATT_HD, wide)
    y = rope(head_norm(ykv[:, :LANES], kg)).astype(BF16)
    ak_ref[0] = y[:, :half]
    ak_ref[1] = y[:, half:]
    v = ykv[:, LANES:].astype(BF16)
    av_ref[0] = v[:, :half]
    av_ref[1] = v[:, half:]


def inproj_even(xa, gain, shift, scale, w, tabs, q_gain, k_gain, tm, seq, nb):
    ta, d = xa.shape
    t_lat = nb * seq
    cos_t, s1_t, s2_t = tabs
    n_tab = seq // tm

    def tab_map(i):
        r = i * tm
        return (jnp.where(r < t_lat, (r % seq) // tm, n_tab), 0)

    ii = np.arange(LANES)
    bd = jnp.asarray((ii[:, None] // ATT_HD == ii[None, :] // ATT_HD).astype(np.float32) / ATT_HD).astype(BF16)
    qg = jnp.tile(q_gain.astype(F32), LANES // ATT_HD).reshape(1, LANES)
    kg = jnp.tile(k_gain.astype(F32), LANES // ATT_HD).reshape(1, LANES)
    mrow = _mod_row_map(tm, seq, nb)
    tab_spec = pl.BlockSpec((tm, LANES), tab_map)
    one = lambda i: (0, 0)
    qw = RET_HEADS * RET_DK
    vgw = 2 * RET_HEADS * RET_DV
    return pl.pallas_call(
        _inproj_even_kernel,
        grid=(ta // tm,),
        in_specs=[
            pl.BlockSpec((tm, d), lambda i: (i, 0)),
            pl.BlockSpec((1, d), one),
            pl.BlockSpec((None, 1, d), mrow),
            pl.BlockSpec((None, 1, d), mrow),
            pl.BlockSpec((d, EVEN_IN), one),
            tab_spec, tab_spec, tab_spec,
            pl.BlockSpec((1, LANES), one), pl.BlockSpec((1, LANES), one),
            pl.BlockSpec((LANES, LANES), one),
        ],
        out_specs=[
            pl.BlockSpec((tm, qw), lambda i: (i, 0)),
            pl.BlockSpec((tm, qw), lambda i: (i, 0)),
            pl.BlockSpec((tm, vgw), lambda i: (i, 0)),
            pl.BlockSpec((ATT_HEADS, tm, ATT_HD), lambda i: (0, i, 0)),
            pl.BlockSpec((ATT_KV_HEADS, tm, ATT_HD), lambda i: (0, i, 0)),
            pl.BlockSpec((ATT_KV_HEADS, tm, ATT_HD), lambda i: (0, i, 0)),
        ],
        out_shape=[
            jax.ShapeDtypeStruct((ta, qw), BF16),
            jax.ShapeDtypeStruct((ta, qw), BF16),
            jax.ShapeDtypeStruct((ta, vgw), BF16),
            jax.ShapeDtypeStruct((ATT_HEADS, ta, ATT_HD), BF16),
            jax.ShapeDtypeStruct((ATT_KV_HEADS, ta, ATT_HD), BF16),
            jax.ShapeDtypeStruct((ATT_KV_HEADS, ta, ATT_HD), BF16),
        ],
        compiler_params=_cparams(("arbitrary",)),
        name="inproj_even",
    )(xa, gain.reshape(1, d), shift, scale, w, cos_t, s1_t, s2_t, qg, kg, bd)


def rope_tables(seq, tm):
    nf = ATT_HD // 4
    t = jnp.arange(seq)
    rows = (t // GRID_W).astype(F32)
    cols = (t % GRID_W).astype(F32)
    inv = ROPE_BASE ** (-jnp.arange(nf, dtype=F32) / nf)
    lane = np.arange(LANES)
    axis = (lane % ATT_HD) // (ATT_HD // 2)
    f = lane % nf
    upper = ((lane % (ATT_HD // 2)) >= nf)
    pos = jnp.where(jnp.asarray(axis)[None, :] == 0, rows[:, None], cols[:, None])
    ang = pos * inv[jnp.asarray(f)][None, :]
    cos = jnp.cos(ang)
    sin = jnp.sin(ang)
    s1 = jnp.where(jnp.asarray(upper)[None, :], 0.0, -sin)
    s2 = jnp.where(jnp.asarray(upper)[None, :], sin, 0.0)
    pad1 = jnp.ones((tm, LANES), F32)
    pad0 = jnp.zeros((tm, LANES), F32)
    return (jnp.concatenate([cos, pad1]), jnp.concatenate([s1, pad0]), jnp.concatenate([s2, pad0]))


def _retention_kernel(dec_ref, q_ref, k_ref, v_ref, g_ref, s0f_ref, s0b_ref,
                      o_ref, sff_ref, sfb_ref, st_ref, *, n_chunks, unroll):
    hp = pl.program_id(1)
    C = RET_CHUNK
    dk, dv = RET_DK, RET_DV
    pos = lax.broadcasted_iota(jnp.int32, (C, dk), 0).astype(F32)
    ii = lax.broadcasted_iota(jnp.int32, (C, C), 0)
    jj = lax.broadcasted_iota(jnp.int32, (C, C), 1)
    dpos = (ii - jj).astype(F32)
    heads = range(2)
    qs = [slice(hh * dk, (hh + 1) * dk) for hh in heads]
    vs = [slice(hh * dv, (hh + 1) * dv) for hh in heads]
    w_out, w_in, gcf, gcb, mask = [], [], [], [], []
    for hh in heads:
        h = 2 * hp + hh
        df = dec_ref[0, h]
        db = dec_ref[1, h]
        lf = -jnp.exp(jnp.full((C, C), df, F32))
        lb = -jnp.exp(jnp.full((C, C), db, F32))
        lfk = -jnp.exp(jnp.full((C, dk), df, F32))
        lbk = -jnp.exp(jnp.full((C, dk), db, F32))
        w_out.append(jnp.concatenate([jnp.exp(lfk * (C - 1.0 - pos)), jnp.exp(lbk * pos)], axis=1))
        w_in.append(jnp.concatenate([jnp.exp(lfk * (pos + 1.0)), jnp.exp(lbk * (C - pos))], axis=1))
        gcf.append(jnp.exp(-jnp.exp(jnp.full((dk, dv), df, F32)) * C))
        gcb.append(jnp.exp(-jnp.exp(jnp.full((dk, dv), db, F32)) * C))
        mask.append(jnp.where(dpos > 0, jnp.exp(lf * jnp.maximum(dpos, 0.0)),
                              jnp.where(dpos < 0, jnp.exp(lb * jnp.maximum(-dpos, 0.0)), 2.0)))

    def rows(n):
        return pl.ds(pl.multiple_of(n * C, C), C)

    items = [(u, hh) for u in range(unroll) for hh in heads]

    def sums_body(i, carry):
        kk = {}
        for u, hh in items:
            k = k_ref[rows(i * unroll + u), qs[hh]].astype(F32)
            kk[(u, hh)] = (jnp.concatenate([k, k], axis=1) * w_out[hh]).astype(BF16)
        kv = {(u, hh): lax.dot_general(kk[(u, hh)], v_ref[rows(i * unroll + u), vs[hh]], TN_DIMS,
                                       preferred_element_type=F32) for u, hh in items}
        for u, hh in items:
            st_ref[hh, i * unroll + u] = kv[(u, hh)]
        return carry

    lax.fori_loop(0, n_chunks // unroll, sums_body, 0)

    def scan_body(n, carry):
        n_rev = n_chunks - 1 - n
        out = []
        for hh in heads:
            sf, sb = carry[2 * hh], carry[2 * hh + 1]
            kvf = st_ref[hh, n, 0:dk, :]
            kvb = st_ref[hh, n_rev, dk:2 * dk, :]
            st_ref[hh, n, 0:dk, :] = sf
            st_ref[hh, n_rev, dk:2 * dk, :] = sb
            out += [gcf[hh] * sf + kvf, gcb[hh] * sb + kvb]
        return tuple(out)

    init = tuple(x for hh in heads for x in (s0f_ref[hh], s0b_ref[hh]))
    fin = lax.fori_loop(0, n_chunks, scan_body, init)
    for hh in heads:
        sff_ref[hh] = fin[2 * hh]
        sfb_ref[hh] = fin[2 * hh + 1]

    def out_body(i, carry):
        ns = [i * unroll + u for u in range(unroll)]
        qb = {(u, hh): q_ref[rows(ns[u]), qs[hh]] for u, hh in items}
        sc = {(u, hh): lax.dot_general(qb[(u, hh)], k_ref[rows(ns[u]), qs[hh]], NT_DIMS,
                                       preferred_element_type=F32) for u, hh in items}
        qw = {}
        for it in items:
            q = qb[it].astype(F32)
            qw[it] = (jnp.concatenate([q, q], axis=1) * w_in[it[1]]).astype(BF16)
        o1 = {(u, hh): _dot((sc[(u, hh)] * mask[hh]).astype(BF16), v_ref[rows(ns[u]), vs[hh]]) for u, hh in items}
        o2 = {(u, hh): _dot(qw[(u, hh)], st_ref[hh, ns[u]].astype(BF16)) for u, hh in items}
        for it in items:
            u, hh = it
            n = ns[u]
            o = o1[it] + o2[it]
            o = o * lax.rsqrt(jnp.mean(o * o, axis=-1, keepdims=True) + EPS)
            gate = g_ref[rows(n), vs[hh]].astype(F32)
            o_ref[rows(n), vs[hh]] = (_silu(gate) * o).astype(o_ref.dtype)
        return carry

    lax.fori_loop(0, n_chunks // unroll, out_body, 0)


def retention(dec, rq, rk, p, s0f, s0b, nb, seq, row_off_blocks):
    n_chunks = seq // RET_CHUNK
    hp_n = RET_HEADS // 2
    vcol = 0
    gcol = vcol + RET_HEADS * RET_DV // (2 * RET_DV)
    ta = rq.shape[0]
    st_spec = pl.BlockSpec((None, 2, RET_DK, RET_DV), lambda b, hp, *_: (b, hp, 0, 0))
    grid_spec = pltpu.PrefetchScalarGridSpec(
        num_scalar_prefetch=1,
        grid=(nb, hp_n),
        in_specs=[
            pl.BlockSpec((seq, 2 * RET_DK), lambda b, hp, *_: (row_off_blocks + b, hp)),
            pl.BlockSpec((seq, 2 * RET_DK), lambda b, hp, *_: (row_off_blocks + b, hp)),
            pl.BlockSpec((seq, 2 * RET_DV), lambda b, hp, *_: (row_off_blocks + b, vcol + hp)),
            pl.BlockSpec((seq, 2 * RET_DV), lambda b, hp, *_: (row_off_blocks + b, gcol + hp)),
            st_spec, st_spec,
        ],
        out_specs=[
            pl.BlockSpec((seq, 2 * RET_DV), lambda b, hp, *_: (b, hp)),
            st_spec, st_spec,
        ],
        scratch_shapes=[pltpu.VMEM((2, n_chunks, 2 * RET_DK, RET_DV), F32)],
    )
    st_shape = jax.ShapeDtypeStruct((nb, RET_HEADS, RET_DK, RET_DV), F32)
    return pl.pallas_call(
        functools.partial(_retention_kernel, n_chunks=n_chunks, unroll=math.gcd(n_chunks, 8)),
        grid_spec=grid_spec,
        out_shape=[jax.ShapeDtypeStruct((nb * seq, RET_HEADS * RET_DV), BF16), st_shape, st_shape],
        compiler_params=_cparams(("arbitrary", "arbitrary")),
        name="retention",
    )(dec, rq, rk, p, p, s0f, s0b)


ATT_VT_ROWS = ATT_HD + 16


ATT_PAIRS_PER_TRIP = 8


def _attn_kernel(q_ref, k_ref, vt_ref, o_ref, *s_refs, tk, c_start, c_end, rep):
    tq = q_ref.shape[2]
    sets = (s_refs[:rep], s_refs[rep:])
    last = c_end - 1

    def scores(bufs, j):
        j = jnp.minimum(j, last)
        c0 = pl.multiple_of(j * tk, tk)
        k = k_ref[pl.ds(c0, tk), :]
        mxs = []
        for r in range(rep):
            s = _dot(k, q_ref[r])
            bufs[r][...] = s
            mxs.append(jnp.max(s, axis=0, keepdims=True))
        return tuple(mxs)

    def softmax_pv(bufs, j, mxs, ms, accs):
        vt = vt_ref[j]
        new_m, new_acc = [], []
        for r in range(rep):
            m_new = jnp.maximum(ms[r], mxs[r])
            a = jnp.exp2(ms[r] - m_new)
            p = jnp.exp2(bufs[r][...] - m_new).astype(BF16)
            new_acc.append(a * accs[r] + _dot(vt, p))
            new_m.append(m_new)
        return tuple(new_m), tuple(new_acc)

    def pair(j, mx0, ms, accs):
        mx1 = scores(sets[1], j + 1)
        ms, accs = softmax_pv(sets[0], j, mx0, ms, accs)
        mx0 = scores(sets[0], j + 2)
        ms, accs = softmax_pv(sets[1], j + 1, mx1, ms, accs)
        return mx0, ms, accs

    def trip(t, carry):
        for u in range(ATT_PAIRS_PER_TRIP):
            carry = pair(c_start + 2 * (ATT_PAIRS_PER_TRIP * t + u), *carry)
        return carry

    n_pairs = (c_end - c_start) // 2
    n_trips = n_pairs // ATT_PAIRS_PER_TRIP
    ms = tuple(jnp.full((1, tq), -1e30, F32) for _ in range(rep))
    accs = tuple(jnp.zeros((ATT_VT_ROWS, tq), F32) for _ in range(rep))
    carry = (scores(sets[0], c_start), ms, accs)
    if n_trips:
        carry = lax.fori_loop(0, n_trips, trip, carry)
    for u in range(n_trips * ATT_PAIRS_PER_TRIP, n_pairs):
        carry = pair(c_start + 2 * u, *carry)
    mx0, ms, accs = carry
    if (c_end - c_start) % 2:
        ms, accs = softmax_pv(sets[0], last, mx0, ms, accs)
    outs = [(acc[:ATT_HD, :] / acc[ATT_HD:ATT_HD + 1, :]).T for acc in accs]
    o_ref[...] = jnp.concatenate(outs, axis=-1).astype(o_ref.dtype)


def attention(aq, kcat, vtcat, seq_q, q_off_blocks, c_start, tq, tk):
    rep = ATT_HEADS // ATT_KV_HEADS
    _, nb, lk, _ = kcat.shape
    nq = seq_q // tq
    n_chunks = lk // tk
    return pl.pallas_call(
        functools.partial(_attn_kernel, tk=tk, c_start=c_start, c_end=n_chunks, rep=rep),
        grid=(nb, ATT_KV_HEADS, nq),
        in_specs=[
            pl.BlockSpec((rep, ATT_HD, tq), lambda b, g, i: (g, 0, q_off_blocks + b * nq + i)),
            pl.BlockSpec((None, None, lk, ATT_HD), lambda b, g, i: (g, b, 0, 0)),
            pl.BlockSpec((None, None, n_chunks, ATT_VT_ROWS, tk), lambda b, g, i: (g, b, 0, 0, 0)),
        ],
        out_specs=pl.BlockSpec((tq, rep * ATT_HD), lambda b, g, i: (b * nq + i, g)),
        out_shape=jax.ShapeDtypeStruct((nb * seq_q, ATT_HEADS * ATT_HD), BF16),
        scratch_shapes=[pltpu.VMEM((tk, tq), F32) for _ in range(2 * rep)],
        compiler_params=_cparams(("arbitrary", "arbitrary", "arbitrary")),
        name="attention",
    )(aq, kcat, vtcat)


def _lat_ctx_specs(tm, width, n_lat_tiles):
    return [pl.BlockSpec((tm, width), lambda i: (jnp.minimum(i, n_lat_tiles - 1), 0)),
            pl.BlockSpec((tm, width), lambda i: (jnp.maximum(i - n_lat_tiles, 0), 0))]


def _outproj_even_kernel(r_lat, r_ctx, a_lat, a_ctx, w1_ref, w2_ref, res_ref, gate_ref, *rest, n_lat_tiles):
    route_in, (o_ref, *route_out) = rest[:6], rest[6:]
    is_lat = pl.program_id(0) < n_lat_tiles
    a1 = jnp.where(is_lat, r_lat[...], r_ctx[...])
    a2 = jnp.where(is_lat, a_lat[...], a_ctx[...])
    y = _dot(a1, w1_ref[...]) + _dot(a2, w2_ref[...])
    out = res_ref[...] + gate_ref[...] * y
    o_ref[...] = out
    _route_tile(out, *route_in, *route_out)


def outproj_even(ret_lat, ret_ctx, att_lat, att_ctx, w1, w2, xa, gate, route_params, tm, seq, nb):
    ta, d = xa.shape
    k1, k2 = w1.shape[0], w2.shape[0]
    n_lat_tiles = ret_lat.shape[0] // tm
    r_in, r_args, r_out, r_shape, r_scratch = _route_io(ta, d, tm, seq, nb, *route_params)
    return pl.pallas_call(
        functools.partial(_outproj_even_kernel, n_lat_tiles=n_lat_tiles),
        grid=(ta // tm,),
        in_specs=_lat_ctx_specs(tm, k1, n_lat_tiles) + _lat_ctx_specs(tm, k2, n_lat_tiles) + [
            pl.BlockSpec((k1, d), lambda i: (0, 0)),
            pl.BlockSpec((k2, d), lambda i: (0, 0)),
            pl.BlockSpec((tm, d), lambda i: (i, 0)),
            pl.BlockSpec((None, 1, d), _mod_row_map(tm, seq, nb)),
        ] + r_in,
        out_specs=[pl.BlockSpec((tm, d), lambda i: (i, 0))] + r_out,
        out_shape=[jax.ShapeDtypeStruct((ta, d), F32)] + r_shape,
        scratch_shapes=r_scratch,
        compiler_params=_cparams(("arbitrary",)),
        name="outproj_even",
    )(ret_lat, ret_ctx, att_lat, att_ctx, w1, w2, xa, gate, *r_args)


def _outproj_odd_kernel(f_lat, f_ctx, b_lat, b_ctx, z_ref, og_ref, w_ref, res_ref, gate_ref, *rest, n_lat_tiles):
    route_in, (o_ref, *route_out) = rest[:6], rest[6:]
    is_lat = pl.program_id(0) < n_lat_tiles
    og = og_ref[...]
    parts = []
    for h in range(DN_HEADS):
        cs = slice(h * DN_DV, (h + 1) * DN_DV)
        of = jnp.where(is_lat, f_lat[:, cs], f_ctx[:, cs]).astype(F32)
        ob = jnp.where(is_lat, b_lat[:, cs], b_ctx[:, cs]).astype(F32)
        o = of + ob
        o = o * lax.rsqrt(jnp.mean(o * o, axis=-1, keepdims=True) + EPS) * og
        parts.append((o * _silu(z_ref[:, cs].astype(F32))).astype(BF16))
    a = jnp.concatenate(parts, axis=-1)
    out = res_ref[...] + gate_ref[...] * _dot(a, w_ref[...])
    o_ref[...] = out
    _route_tile(out, *route_in, *route_out)


def outproj_odd(of_lat, of_ctx, ob_lat, ob_ctx, p, out_gain, w, xa, gate, route_params, tm, seq, nb):
    ta, d = xa.shape
    kdim = DN_HEADS * DN_DV
    n_lat_tiles = of_lat.shape[0] // tm
    r_in, r_args, r_out, r_shape, r_scratch = _route_io(ta, d, tm, seq, nb, *route_params)
    return pl.pallas_call(
        functools.partial(_outproj_odd_kernel, n_lat_tiles=n_lat_tiles),
        grid=(ta // tm,),
        in_specs=_lat_ctx_specs(tm, kdim, n_lat_tiles) + _lat_ctx_specs(tm, kdim, n_lat_tiles) + [
            pl.BlockSpec((tm, kdim), lambda i: (i, 0)),
            pl.BlockSpec((1, DN_DV), lambda i: (0, 0)),
            pl.BlockSpec((kdim, d), lambda i: (0, 0)),
            pl.BlockSpec((tm, d), lambda i: (i, 0)),
            pl.BlockSpec((None, 1, d), _mod_row_map(tm, seq, nb)),
        ] + r_in,
        out_specs=[pl.BlockSpec((tm, d), lambda i: (i, 0))] + r_out,
        out_shape=[jax.ShapeDtypeStruct((ta, d), F32)] + r_shape,
        scratch_shapes=r_scratch,
        compiler_params=_cparams(("arbitrary",)),
        name="outproj_odd",
    )(of_lat, of_ctx, ob_lat, ob_ctx, p, out_gain.reshape(1, DN_DV).astype(F32), w, xa, gate, *r_args)


def _inproj_odd_kernel(first_ref, last_ref, x_ref, xp_ref, xn_ref, g_ref, sh_ref, sc_ref, w_ref, cw_ref,
                       arow_ref, brow_ref, q_ref, k_ref, v_ref, z_ref, gb_ref, *, ctx_tile0, cseq):
    i = pl.program_id(0)
    tm = x_ref.shape[0]
    gain, shift, scale = g_ref[...], sh_ref[...], sc_ref[...]
    hrows = xp_ref.shape[0]
    hall = _norm_mod(jnp.concatenate([x_ref[...], xp_ref[...], xn_ref[...]], axis=0), gain, shift, scale).astype(BF16)
    hb = hall[:tm]
    keep_prev = 1.0 - first_ref[i].astype(F32)
    keep_next = 1.0 - last_ref[i].astype(F32)
    row = lax.broadcasted_iota(jnp.int32, (tm, LANES), 0)
    is_first = row == 0
    is_last = row == tm - 1
    inner = cseq < tm
    if inner:
        in_ctx = i >= ctx_tile0
        local = row & (cseq - 1)
        zero_dn = jnp.logical_and(in_ctx, local == 0)
        zero_up = jnp.logical_and(in_ctx, local == cseq - 1)
    n_qk = 2 * DN_HEADS * DN_DK // LANES
    n_q = DN_HEADS * DN_DK // LANES
    outs = (q_ref, k_ref, v_ref)
    wide = 2 * LANES
    for c0 in range(0, DN_QKV, wide):
        yall = _dot(hall, w_ref[:, c0:c0 + wide])
        y2 = yall[:tm]
        yh2 = yall[tm:]
        for u in range(2):
            j = c0 // LANES + u
            us = slice(u * LANES, (u + 1) * LANES)
            x = y2[:, us]
            xp = yh2[hrows - 1:hrows, us] * keep_prev
            xn = yh2[hrows:hrows + 1, us] * keep_next
            x_dn = jnp.where(is_first, xp, pltpu.roll(x, 1, 0))
            x_up = jnp.where(is_last, xn, pltpu.roll(x, tm - 1, 0))
            if inner:
                x_dn = jnp.where(zero_dn, 0.0, x_dn)
                x_up = jnp.where(zero_up, 0.0, x_up)
            w = cw_ref[:, j * LANES:(j + 1) * LANES]
            y = _silu(x_dn * w[0:1, :] + x * w[1:2, :] + x_up * w[2:3, :])
            if j < n_qk:
                y = y * lax.rsqrt(jnp.sum(y * y, axis=-1, keepdims=True) + EPS)
                if j < n_q:
                    y = y * DN_DK ** -0.5
            lj = j % n_q
            outs[j // n_q][:, lj * LANES:(lj + 1) * LANES] = y.astype(BF16)
    zw = DN_HEADS * DN_DV
    for c0 in range(0, zw, 512):
        z_ref[:, c0:c0 + 512] = _dot(hb, w_ref[:, DN_QKV + c0:DN_QKV + c0 + 512]).astype(BF16)

    a = _dot(hb, w_ref[:, DN_QKV + zw:DN_QKV + zw + LANES])
    lane = lax.broadcasted_iota(jnp.int32, (tm, LANES), 1)
    zz = a + brow_ref[...]
    softplus = jnp.maximum(zz, 0.0) + jnp.log(1.0 + jnp.exp(-jnp.abs(zz)))
    g = -jnp.exp(arow_ref[...]) * softplus
    beta = 1.0 / (1.0 + jnp.exp(-a))
    gb_ref[...] = jnp.where(lane < 2 * DN_HEADS, g, jnp.where(lane < 4 * DN_HEADS, beta, 0.0))


def inproj_odd(xa, gain, shift, scale, w, conv_w, arow, brow, first_flags, last_flags, tm, seq, nb, cseq):
    ta, d = xa.shape
    halo = 8
    hb = tm // halo
    n_h = ta // halo
    kdim = DN_HEADS * DN_DK
    assert cseq >= tm or (tm % cseq == 0 and cseq & (cseq - 1) == 0)
    mrow = lambda i, *_: _mod_row_map(tm, seq, nb)(i)
    one = lambda i, *_: (0, 0)
    row_blk = lambda i, *_: (i, 0)
    grid_spec = pltpu.PrefetchScalarGridSpec(
        num_scalar_prefetch=2,
        grid=(ta // tm,),
        in_specs=[
            pl.BlockSpec((tm, d), row_blk),
            pl.BlockSpec((halo, d), lambda i, *_: (jnp.maximum(i * hb - 1, 0), 0)),
            pl.BlockSpec((halo, d), lambda i, *_: (jnp.minimum((i + 1) * hb, n_h - 1), 0)),
            pl.BlockSpec((1, d), one),
            pl.BlockSpec((None, 1, d), mrow),
            pl.BlockSpec((None, 1, d), mrow),
            pl.BlockSpec((d, ODD_IN_PAD), one),
            pl.BlockSpec((DN_CONV, DN_QKV), one),
            pl.BlockSpec((1, LANES), one),
            pl.BlockSpec((1, LANES), one),
        ],
        out_specs=[pl.BlockSpec((tm, kdim), row_blk)] * 4 + [pl.BlockSpec((tm, LANES), row_blk)],
    )
    return pl.pallas_call(
        functools.partial(_inproj_odd_kernel, ctx_tile0=nb * seq // tm, cseq=cseq),
        grid_spec=grid_spec,
        out_shape=[jax.ShapeDtypeStruct((ta, kdim), BF16)] * 4 + [jax.ShapeDtypeStruct((ta, LANES), F32)],
        compiler_params=_cparams(("arbitrary",)),
        name="inproj_odd",
    )(first_flags, last_flags, xa, xa, xa, gain.reshape(1, d), shift, scale, w, conv_w, arow, brow)


def _deltanet_bidir_kernel(qf_ref, kf_ref, vf_ref, gbf_ref, gbtf_ref, qb_ref, kb_ref, vb_ref, gbb_ref, gbtb_ref,
                           s0f_ref, s0b_ref, of_ref, ob_ref, sff_ref, sfb_ref, sf_scr, sb_scr, *, n_chunks):
    t = pl.program_id(1)

    @pl.when(t == 0)
    def _():
        sf_scr[...] = s0f_ref[...]
        sb_scr[...] = s0b_ref[...]

    C = DN_CHUNK
    ii = lax.broadcasted_iota(jnp.int32, (C, C), 0)
    jj = lax.broadcasted_iota(jnp.int32, (C, C), 1)
    lower, upper = ii >= jj, ii <= jj
    eye = jnp.where(ii == jj, 1.0, 0.0).astype(F32)
    blk = ii ^ jj
    dirs = (
        dict(rev=False, incl=lower, strict=ii > jj, q=qf_ref, k=kf_ref, v=vf_ref, gb=gbf_ref, gbt=gbtf_ref,
             o=of_ref, scr=sf_scr, off=0, order=list(range(n_chunks))),
        dict(rev=True, incl=upper, strict=ii < jj, q=qb_ref, k=kb_ref, v=vb_ref, gb=gbb_ref, gbt=gbtb_ref,
             o=ob_ref, scr=sb_scr, off=DN_HEADS, order=list(range(n_chunks - 1, -1, -1))),
    )
    items = [(d, c, h) for d in range(2) for c in dirs[d]["order"] for h in range(DN_HEADS)]

    gcols, grows, gbs = {}, {}, {}
    for d, dr in enumerate(dirs):
        tri = jnp.where(dr["incl"], 1.0, 0.0).astype(F32)
        tri_t = jnp.where(upper if not dr["rev"] else lower, 1.0, 0.0).astype(F32)
        for c in dr["order"]:
            gb_c = dr["gb"][c * C:(c + 1) * C, :]
            gbs[(d, c)] = gb_c
            gcols[(d, c)] = jnp.dot(tri, gb_c, preferred_element_type=F32, precision=HIGHEST)
            grows[(d, c)] = jnp.dot(dr["gbt"][c], tri_t, preferred_element_type=F32, precision=HIGHEST)

    qb, kb16, decay, kbeta, egc, kd, gl, rhs = {}, {}, {}, {}, {}, {}, {}, {}
    for it in items:
        d, c, h = it
        dr = dirs[d]
        gi = dr["off"] + h
        bi = 2 * DN_HEADS + dr["off"] + h
        rows = slice(c * C, (c + 1) * C)
        cs = slice(h * DN_DK, (h + 1) * DN_DK)
        gc = gcols[(d, c)][:, gi:gi + 1]
        gr = grows[(d, c)][gi:gi + 1, :]
        beta = gbs[(d, c)][:, bi:bi + 1]
        qb[it] = dr["q"][rows, cs]
        kb16[it] = dr["k"][rows, cs]
        kf = kb16[it].astype(F32)
        decay[it] = jnp.where(dr["incl"], jnp.exp(jnp.where(dr["incl"], gc - gr, 0.0)), 0.0)
        kbeta[it] = kf * beta
        egc[it] = jnp.exp(gc)
        glast = gc[0:1, :] if dr["rev"] else gc[C - 1:C, :]
        kd[it] = (kf * jnp.exp(glast - gc)).astype(BF16)
        gl[it] = jnp.exp(glast)
        rhs[it] = jnp.concatenate([dr["v"][rows, cs].astype(F32) * beta, kbeta[it] * egc[it]], axis=1).astype(BF16)

    kk = {it: lax.dot_general(kbeta[it].astype(BF16), kb16[it], NT_DIMS, preferred_element_type=F32)
          for it in items}
    qk = {it: lax.dot_general(qb[it], kb16[it], NT_DIMS, preferred_element_type=F32) for it in items}
    lm = {it: jnp.where(dirs[it[0]]["strict"], kk[it] * decay[it], 0.0) for it in items}
    attn = {it: jnp.where(dirs[it[0]]["incl"], qk[it] * decay[it], 0.0).astype(BF16) for it in items}
    dinv = {it: eye - jnp.where(blk < 2, lm[it], 0.0) for it in items}
    s = 2
    while s < C:
        in_band = jnp.logical_and(blk >= s, blk < 2 * s)
        tmp = {it: _dot(dinv[it].astype(BF16), jnp.where(in_band, lm[it], 0.0).astype(BF16)) for it in items}
        dinv = {it: dinv[it] - _dot(tmp[it].astype(BF16), dinv[it].astype(BF16)) for it in items}
        s *= 2
    uw = {it: _dot(dinv[it].astype(BF16), rhs[it]) for it in items}
    wq = {it: jnp.concatenate([uw[it][:, DN_DV:], qb[it].astype(F32) * egc[it]], axis=0).astype(BF16)
          for it in items}

    states = {(d, h): dirs[d]["scr"][h] for d in range(2) for h in range(DN_HEADS)}
    for step in range(n_chunks):
        its = [(d, dirs[d]["order"][step], h) for d in range(2) for h in range(DN_HEADS)]
        r = {it: _dot(wq[it], states[(it[0], it[2])].astype(BF16)) for it in its}
        v_new = {it: (uw[it][:, :DN_DV] - r[it][:C]).astype(BF16) for it in its}
        o = {it: r[it][C:] + _dot(attn[it], v_new[it]) for it in its}
        for it in its:
            key = (it[0], it[2])
            states[key] = states[key] * gl[it] + lax.dot_general(kd[it], v_new[it], TN_DIMS,
                                                                 preferred_element_type=F32)
        for it in its:
            d, c, h = it
            dirs[d]["o"][c * C:(c + 1) * C, h * DN_DK:(h + 1) * DN_DK] = o[it].astype(of_ref.dtype)
    for (d, h), st in states.items():
        dirs[d]["scr"][h] = st

    @pl.when(t == pl.num_programs(1) - 1)
    def _():
        sff_ref[...] = sf_scr[...]
        sfb_ref[...] = sb_scr[...]


def deltanet_bidir(q, k, v, gb, gbt, s0f, s0b, nb, seq, row_off, tl):
    nblk = seq // tl
    n_chunks = tl // DN_CHUNK
    off_b = row_off // tl
    kdim = DN_HEADS * DN_DK

    def fwd_rb(b, t):
        return off_b + b * nblk + t

    def bwd_rb(b, t):
        return off_b + b * nblk + (nblk - 1 - t)

    def seq_specs(rb):
        spec = pl.BlockSpec((tl, kdim), lambda b, t: (rb(b, t), 0))
        return [spec, spec, spec,
                pl.BlockSpec((tl, LANES), lambda b, t: (rb(b, t), 0)),
                pl.BlockSpec((n_chunks, 4 * DN_HEADS, DN_CHUNK), lambda b, t: (rb(b, t), 0, 0))]

    st_spec = pl.BlockSpec((None, DN_HEADS, DN_DK, DN_DV), lambda b, t: (b, 0, 0, 0))
    st_shape = jax.ShapeDtypeStruct((nb, DN_HEADS, DN_DK, DN_DV), F32)
    o_shape = jax.ShapeDtypeStruct((nb * seq, kdim), BF16)
    return pl.pallas_call(
        functools.partial(_deltanet_bidir_kernel, n_chunks=n_chunks),
        grid=(nb, nblk),
        in_specs=seq_specs(fwd_rb) + seq_specs(bwd_rb) + [st_spec, st_spec],
        out_specs=[
            pl.BlockSpec((tl, kdim), lambda b, t: (b * nblk + t, 0)),
            pl.BlockSpec((tl, kdim), lambda b, t: (b * nblk + nblk - 1 - t, 0)),
            st_spec, st_spec,
        ],
        out_shape=[o_shape, o_shape, st_shape, st_shape],
        scratch_shapes=[pltpu.VMEM((DN_HEADS, DN_DK, DN_DV), F32), pltpu.VMEM((DN_HEADS, DN_DK, DN_DV), F32)],
        compiler_params=_cparams(("arbitrary", "arbitrary")),
        name="deltanet_bidir",
    )(q, k, v, gb, gbt, q, k, v, gb, gbt, s0f, s0b)


def _route_tile(x, g_ref, sh_ref, sc_ref, wr_ref, br_ref, ltri_ref, f_ref, r_ref, cnt_ref, base_ref):
    @pl.when(pl.program_id(0) == 0)
    def _():
        base_ref[...] = jnp.zeros_like(base_ref)

    h = _norm_mod(x, g_ref[...], sh_ref[...], sc_ref[...])
    f_ref[...] = _pack_bf16_pairs(h)
    logits = _dot(h.astype(BF16), wr_ref[...]) + br_ref[...]
    tm = logits.shape[0]
    lane = lax.broadcasted_iota(jnp.int32, (tm, LANES), 1)
    neg = -1e30
    big = 4 * LANES
    is_g = lane < N_GROUPS
    gl = jnp.where(is_g, logits, neg)
    gm = jnp.max(gl, axis=-1, keepdims=True)
    grp = jnp.min(jnp.where(gl == gm, lane, big), axis=-1, keepdims=True)
    psum = jnp.sum(jnp.where(is_g, jnp.exp(gl - gm), 0.0), axis=-1, keepdims=True)
    p_grp = 1.0 / psum
    e_lane = lane - N_GROUPS
    in_grp = jnp.logical_and(jnp.logical_and(e_lane >= 0, e_lane < N_EXPERTS),
                             (e_lane // EXPERTS_PER_GROUP) == grp)
    el = jnp.where(in_grp, logits, neg)
    m1 = jnp.max(el, axis=-1, keepdims=True)
    i1 = jnp.min(jnp.where(el == m1, lane, big), axis=-1, keepdims=True)
    el2 = jnp.where(lane == i1, neg, el)
    m2 = jnp.max(el2, axis=-1, keepdims=True)
    i2 = jnp.min(jnp.where(el2 == m2, lane, big), axis=-1, keepdims=True)
    e21 = jnp.exp(m2 - m1)
    w1 = p_grp / (1.0 + e21)
    w2 = p_grp * e21 / (1.0 + e21)
    e1 = (i1 - N_GROUPS).astype(F32)
    e2 = (i2 - N_GROUPS).astype(F32)
    oh1 = lane == i1
    oh2 = lane == i2
    oh1f = jnp.where(oh1, 1.0, 0.0)
    oh2f = jnp.where(oh2, 1.0, 0.0)
    ltri = ltri_ref[...]
    before1 = _dot(ltri, oh1f.astype(BF16))
    before2 = _dot(ltri, oh2f.astype(BF16))
    cnt1 = jnp.sum(oh1f, axis=0, keepdims=True)
    cnt2 = jnp.sum(oh2f, axis=0, keepdims=True)
    base = base_ref[0:1, :]
    rank1 = jnp.sum(jnp.where(oh1, base + before1, 0.0), axis=-1, keepdims=True)
    rank2 = jnp.sum(jnp.where(oh2, base + cnt1 + before2, 0.0), axis=-1, keepdims=True)
    total = base + cnt1 + cnt2
    base_ref[...] = jnp.broadcast_to(total, base_ref.shape)
    cnt_ref[...] = jnp.broadcast_to(total, cnt_ref.shape)
    vals = (e1, e2, w1, w2, rank1, rank2)
    out = jnp.zeros((tm, LANES), F32)
    for idx, val in enumerate(vals):
        out = jnp.where(lane == idx, val, out)
    r_ref[...] = out


def _route_io(ta, d, tm, seq, nb, gain, shift, scale, w_router, b_router):
    mrow = _mod_row_map(tm, seq, nb)
    ii = np.arange(tm)
    ltri = jnp.asarray((ii[:, None] > ii[None, :]).astype(np.float32)).astype(BF16)
    in_specs = [
        pl.BlockSpec((1, d), lambda i: (0, 0)),
        pl.BlockSpec((None, 1, d), mrow),
        pl.BlockSpec((None, 1, d), mrow),
        pl.BlockSpec((d, LANES), lambda i: (0, 0)),
        pl.BlockSpec((1, LANES), lambda i: (0, 0)),
        pl.BlockSpec((tm, tm), lambda i: (0, 0)),
    ]
    args = [gain.reshape(1, d), shift, scale, w_router, b_router, ltri]
    out_specs = [pl.BlockSpec((tm, d // 2), lambda i: (i, 0)), pl.BlockSpec((tm, LANES), lambda i: (i, 0)),
                 pl.BlockSpec((8, LANES), lambda i: (0, 0))]
    out_shape = [jax.ShapeDtypeStruct((ta, d // 2), U32), jax.ShapeDtypeStruct((ta, LANES), F32),
                 jax.ShapeDtypeStruct((8, LANES), F32)]
    return in_specs, args, out_specs, out_shape, [pltpu.VMEM((8, LANES), F32)]


ROW_DMA_UNROLL = 8


def _issue_row_copies(n_rows, make_copy):
    def trip(i, carry):
        for u in range(ROW_DMA_UNROLL):
            make_copy(i * ROW_DMA_UNROLL + u).start(priority=u % 2)
        return carry

    lax.fori_loop(0, n_rows // ROW_DMA_UNROLL, trip, 0)


def _moe_scatter_kernel(pos_ref, f_ref, xs_in, xs_out, sem):
    del xs_in
    tm = f_ref.shape[0]
    for k in range(TOP_K):
        _issue_row_copies(tm, lambda r, k=k: pltpu.make_async_copy(
            f_ref.at[pl.ds(r, 1)], xs_out.at[pl.ds(pos_ref[0, 0, k * tm + r], 1)], sem))
    for _ in range(2):
        pltpu.make_async_copy(f_ref, xs_out.at[pl.ds(0, tm)], sem).wait()


def moe_scatter(pos_tiles, f, xs_zero, tm):
    ta, d = f.shape
    return pl.pallas_call(
        _moe_scatter_kernel,
        grid=(ta // tm,),
        in_specs=[
            pl.BlockSpec((1, 1, 2 * tm), lambda i: (i, 0, 0), memory_space=pltpu.SMEM),
            pl.BlockSpec((tm, d), lambda i: (i, 0)),
            pl.BlockSpec(memory_space=pl.ANY),
        ],
        out_specs=pl.BlockSpec(memory_space=pl.ANY),
        out_shape=jax.ShapeDtypeStruct(xs_zero.shape, xs_zero.dtype),
        scratch_shapes=[pltpu.SemaphoreType.DMA(())],
        input_output_aliases={2: 0},
        compiler_params=_cparams(("arbitrary",)),
        name="moe_scatter",
    )(pos_tiles, f, xs_zero)


def _moe_ffn_kernel(te_ref, nu_ref, x_ref, wgu_ref, wd_ref, o_ref, wgu_bf, wd_bf):
    i = pl.program_id(0)
    fdim = wd_bf.shape[0]

    @pl.when(i < nu_ref[0])
    def _():
        prev = te_ref[jnp.maximum(i - 1, 0)]
        changed = jnp.logical_or(i == 0, te_ref[i] != prev)

        @pl.when(changed)
        def _():
            wgu_bf[...] = wgu_ref[...].astype(BF16)
            wd_bf[...] = wd_ref[...].astype(BF16)

        x_hi, x_lo = _unpack_bf16_pairs(x_ref[...])
        half = x_hi.shape[1]
        gu = _dot(x_hi.astype(BF16), wgu_bf[:half, :]) + _dot(x_lo.astype(BF16), wgu_bf[half:, :])
        hmid = _silu(gu[:, :fdim]) * gu[:, fdim:]
        o_ref[...] = _pack_bf16_pairs(_dot(hmid.astype(BF16), wd_bf[...]))

    @pl.when(i >= nu_ref[0])
    def _():
        o_ref[...] = jnp.zeros_like(o_ref)


def moe_ffn(tile_expert, n_used, xs, w_gate_up, w_down, layer, tm):
    n_pad, dh = xs.shape
    d = 2 * dh
    f2 = w_gate_up.shape[-1]
    fdim = w_down.shape[-2]
    grid_spec = pltpu.PrefetchScalarGridSpec(
        num_scalar_prefetch=2,
        grid=(n_pad // tm,),
        in_specs=[
            pl.BlockSpec((tm, dh), lambda i, te, nu: (i, 0)),
            pl.BlockSpec((None, None, d, f2), lambda i, te, nu: (layer, te[i], 0, 0)),
            pl.BlockSpec((None, None, fdim, d), lambda i, te, nu: (layer, te[i], 0, 0)),
        ],
        out_specs=pl.BlockSpec((tm, dh), lambda i, te, nu: (i, 0)),
        scratch_shapes=[pltpu.VMEM((d, f2), BF16), pltpu.VMEM((fdim, d), BF16)],
    )
    return pl.pallas_call(
        _moe_ffn_kernel,
        grid_spec=grid_spec,
        out_shape=jax.ShapeDtypeStruct((n_pad, dh), U32),
        compiler_params=_cparams(("arbitrary",)),
        name="moe_ffn",
    )(tile_expert, n_used, xs, w_gate_up, w_down)


def _moe_combine_kernel(pos_ref, posn_ref, x_ref, gate_ref, r_ref, fg_ref, y_hbm, o_ref, ybuf, sem, *, final):
    tm = x_ref.shape[0]
    i = pl.program_id(0)
    n = pl.num_programs(0)

    def gather(p_ref, slot):
        _issue_row_copies(2 * tm, lambda r: pltpu.make_async_copy(
            y_hbm.at[pl.ds(p_ref[0, 0, r], 1)], ybuf.at[slot, pl.ds(r, 1)], sem.at[slot]))

    @pl.when(i == 0)
    def _():
        gather(pos_ref, 0)

    @pl.when(i + 1 < n)
    def _():
        gather(posn_ref, (i + 1) % 2)

    slot = i % 2
    pltpu.make_async_copy(y_hbm.at[pl.ds(0, 2 * tm)], ybuf.at[slot], sem.at[slot]).wait()
    route = r_ref[...]
    w0, w1 = route[:, 2:3], route[:, 3:4]
    y0_hi, y0_lo = _unpack_bf16_pairs(ybuf[slot, 0:tm, :])
    y1_hi, y1_lo = _unpack_bf16_pairs(ybuf[slot, tm:2 * tm, :])
    y = jnp.concatenate([w0 * y0_hi + w1 * y1_hi, w0 * y0_lo + w1 * y1_lo], axis=-1)
    out = x_ref[...] + gate_ref[...] * y
    if final:
        out = out * lax.rsqrt(jnp.mean(out * out, axis=-1, keepdims=True) + EPS) * fg_ref[...]
    o_ref[...] = out


def moe_combine(pos_tiles, xa, gate, route, y_sorted, final_gain, n_rows, final, tm, seq, nb):
    d = xa.shape[1]
    n_tiles = n_rows // tm
    return pl.pallas_call(
        functools.partial(_moe_combine_kernel, final=final),
        grid=(n_tiles,),
        in_specs=[
            pl.BlockSpec((1, 1, 2 * tm), lambda i: (i, 0, 0), memory_space=pltpu.SMEM),
            pl.BlockSpec((1, 1, 2 * tm), lambda i: (jnp.minimum(i + 1, n_tiles - 1), 0, 0),
                         memory_space=pltpu.SMEM),
            pl.BlockSpec((tm, d), lambda i: (i, 0)),
            pl.BlockSpec((None, 1, d), _mod_row_map(tm, seq, nb)),
            pl.BlockSpec((tm, LANES), lambda i: (i, 0)),
            pl.BlockSpec((1, d), lambda i: (0, 0)),
            pl.BlockSpec(memory_space=pl.ANY),
        ],
        out_specs=pl.BlockSpec((tm, d), lambda i: (i, 0)),
        out_shape=jax.ShapeDtypeStruct((n_rows, d), F32),
        scratch_shapes=[pltpu.VMEM((2, 2 * tm, d // 2), U32), pltpu.SemaphoreType.DMA((2,))],
        compiler_params=_cparams(("arbitrary",)),
        name="moe_combine",
    )(pos_tiles, pos_tiles, xa, gate, route, final_gain.reshape(1, d).astype(F32), y_sorted)


def moe_slots(route, counts, tm_ffn, tm_tok):
    ta = route.shape[0]
    ids = route[:, 0:TOP_K].astype(jnp.int32)
    rank = route[:, 2 * TOP_K:3 * TOP_K].astype(jnp.int32)
    counts = counts[0, N_GROUPS:N_GROUPS + N_EXPERTS].astype(jnp.int32)
    padded = ((counts + tm_ffn - 1) // tm_ffn) * tm_ffn
    ends = jnp.cumsum(padded)
    starts = ends - padded
    experts = jnp.arange(N_EXPERTS, dtype=jnp.int32)
    pos = jnp.sum(jnp.where(ids[..., None] == experts, starts, 0), axis=-1) + rank
    n_tiles = (TOP_K * ta + N_EXPERTS * (tm_ffn - 1)) // tm_ffn
    tile_start = jnp.arange(n_tiles, dtype=jnp.int32) * tm_ffn
    tile_expert = jnp.sum((tile_start[:, None] >= ends[None, :]).astype(jnp.int32), axis=1)
    tile_expert = jnp.minimum(tile_expert, N_EXPERTS - 1)
    n_used = (ends[-1] // tm_ffn).astype(jnp.int32).reshape(1)
    pos_tiles = pos.reshape(ta // tm_tok, tm_tok, TOP_K).transpose(0, 2, 1).reshape(ta // tm_tok, 1, TOP_K * tm_tok)
    return tile_expert, n_used, n_tiles * tm_ffn, pos_tiles


def _seq_flags(t_lat, seq, tc, cseq, tm):
    starts = np.arange(0, t_lat + tc, tm)
    first = np.where(starts < t_lat, starts % seq == 0, (starts - t_lat) % cseq == 0)
    ends = starts + tm
    last = np.where(starts < t_lat, ends % seq == 0, (ends - t_lat) % cseq == 0)
    return jnp.asarray(first.astype(np.int32)), jnp.asarray(last.astype(np.int32))


def kernel(x, c, ctx, c_ctx, w_ada, b_ada, norm_mix, norm_ffn, ev_w_in, ev_q_gain, ev_k_gain, ev_decay_f,
           ev_decay_b, ev_w_out, od_w_in, od_conv, od_a_log_f, od_a_log_b, od_dt_bias_f, od_dt_bias_b,
           od_out_gain, od_w_out, moe_w_group, moe_b_group, moe_w_expert, moe_b_expert, moe_w_gate_up,
           moe_w_down, final_norm_gain):
    nb, seq, d = x.shape
    cseq = ctx.shape[1]
    depth = w_ada.shape[0]
    t_lat = nb * seq
    tc = nb * cseq
    assert nb + 1 <= 8 and seq % cseq == 0 and cseq % RET_CHUNK == 0 and seq % GRID_W == 0

    tm = 512 if tc % 512 == 0 else cseq
    tq = min(256, cseq)
    tk = min(256, cseq)
    tl = 2 * DN_CHUNK
    tm_ffn = 256
    tm_comb = tm

    xa = jnp.concatenate([x.reshape(t_lat, d), ctx.reshape(tc, d)], axis=0)
    c8 = jnp.zeros((8, d), F32).at[:nb].set(c).at[nb].set(c_ctx)
    mod = adaln(c8, w_ada, b_ada)

    tabs = rope_tables(seq, tm)
    first_flags, last_flags = _seq_flags(t_lat, seq, tc, cseq, tm)
    ret_zero = jnp.zeros((nb, RET_HEADS, RET_DK, RET_DV), F32)
    dn_zero = jnp.zeros((nb, DN_HEADS, DN_DK, DN_DV), F32)

    xs = None
    for layer in range(depth):
        m = mod[layer].reshape(8, 6, 1, d)
        sh1, sc1, g1, sh2, sc2, g2 = (m[:, j] for j in range(6))
        w_router = jnp.pad(jnp.concatenate([moe_w_group[layer], moe_w_expert[layer]], axis=1),
                           ((0, 0), (0, LANES - N_GROUPS - N_EXPERTS))).astype(BF16)
        b_router = jnp.pad(jnp.concatenate([moe_b_group[layer], moe_b_expert[layer]]),
                           (0, LANES - N_GROUPS - N_EXPERTS)).reshape(1, LANES).astype(F32)
        route_params = (norm_ffn[layer], sh2, sc2, w_router, b_router)
        i = layer // 2
        if layer % 2 == 0:
            w_in = ev_w_in[i].astype(BF16)
            rq, rk, p, aq, ak, av = inproj_even(xa, norm_mix[layer], sh1, sc1, w_in, tabs, ev_q_gain[i],
                                                ev_k_gain[i], tm, seq, nb)
            dec = jnp.stack([ev_decay_f[i], ev_decay_b[i]]).astype(F32)
            oc, scf, scb = retention(dec, rq, rk, p, ret_zero, ret_zero, nb, cseq, t_lat // cseq)
            ol, _, _ = retention(dec, rq, rk, p, scf, scb, nb, seq, 0)
            kcat = jnp.concatenate([ak[:, :t_lat].reshape(ATT_KV_HEADS, nb, seq, ATT_HD),
                                    ak[:, t_lat:].reshape(ATT_KV_HEADS, nb, cseq, ATT_HD)], axis=2)
            vcat = jnp.concatenate([av[:, :t_lat].reshape(ATT_KV_HEADS, nb, seq, ATT_HD),
                                    av[:, t_lat:].reshape(ATT_KV_HEADS, nb, cseq, ATT_HD)], axis=2)
            lk = seq + cseq
            vtcat = jnp.concatenate([vcat.transpose(0, 1, 3, 2),
                                     jnp.ones((ATT_KV_HEADS, nb, ATT_VT_ROWS - ATT_HD, lk), BF16)], axis=2)
            vtcat = vtcat.reshape(ATT_KV_HEADS, nb, ATT_VT_ROWS, lk // tk, tk).transpose(0, 1, 3, 2, 4)
            aqt = aq.transpose(0, 2, 1)
            att_l = attention(aqt, kcat, vtcat, seq, 0, 0, tq, tk)
            att_c = attention(aqt, kcat, vtcat, cseq, t_lat // tq, seq // tk, tq, tk)
            w_out = ev_w_out[i].astype(BF16)
            k1 = RET_HEADS * RET_DV
            xa, f, route, counts = outproj_even(ol, oc, att_l, att_c, w_out[:k1], w_out[k1:], xa, g1, route_params,
                                                tm, seq, nb)
        else:
            w_in = jnp.pad(od_w_in[i], ((0, 0), (0, ODD_IN_PAD - ODD_IN))).astype(BF16)
            zpad = jnp.zeros((LANES - 2 * DN_HEADS,), F32)
            arow = jnp.concatenate([od_a_log_f[i], od_a_log_b[i], zpad]).reshape(1, LANES).astype(F32)
            brow = jnp.concatenate([od_dt_bias_f[i], od_dt_bias_b[i], zpad]).reshape(1, LANES).astype(F32)
            q, k, v, p, gb = inproj_odd(xa, norm_mix[layer], sh1, sc1, w_in, od_conv[i].astype(F32), arow, brow,
                                        first_flags, last_flags, tm, seq, nb, cseq)
            ta = t_lat + tc
            gbt = gb.reshape(ta // DN_CHUNK, DN_CHUNK, LANES)[:, :, :4 * DN_HEADS].transpose(0, 2, 1)
            oc_f, oc_b, sc_f, sc_b = deltanet_bidir(q, k, v, gb, gbt, dn_zero, dn_zero, nb, cseq, t_lat, tl)
            ol_f, ol_b, _, _ = deltanet_bidir(q, k, v, gb, gbt, sc_f, sc_b, nb, seq, 0, tl)
            xa, f, route, counts = outproj_odd(ol_f, oc_f, ol_b, oc_b, p, od_out_gain[i], od_w_out[i].astype(BF16),
                                               xa, g1, route_params, tm, seq, nb)

        tile_expert, n_used, n_pad, pos_tiles = moe_slots(route, counts, tm_ffn, tm_comb)
        xs = moe_scatter(pos_tiles, f, jnp.zeros((n_pad, d // 2), U32) if xs is None else xs, tm_comb)
        y_sorted = moe_ffn(tile_expert, n_used, xs, moe_w_gate_up, moe_w_down, layer, tm_ffn)
        last = layer == depth - 1
        xa = moe_combine(pos_tiles, xa, g2, route, y_sorted, final_norm_gain, t_lat if last else t_lat + tc, last,
                         tm_comb, seq, nb)

    return xa.reshape(nb, seq, d)
```

```python
import functools
import math

import numpy as np
import jax
import jax.numpy as jnp
from jax import lax
from jax.experimental import pallas as pl
from jax.experimental.pallas import tpu as pltpu

F32 = jnp.float32
BF16 = jnp.bfloat16
U32 = jnp.uint32
HIGHEST = lax.Precision.HIGHEST

EPS = 1e-6
GRID_W = 64
ROPE_BASE = 10000.0
RET_HEADS, RET_DK, RET_DV, RET_CHUNK = 8, 64, 128, 128
ATT_HEADS, ATT_KV_HEADS, ATT_HD = 8, 2, 64
DN_HEADS, DN_DK, DN_DV, DN_CHUNK, DN_CONV = 8, 128, 128, 64, 3
N_GROUPS, EXPERTS_PER_GROUP, TOP_K = 4, 8, 2
N_EXPERTS = N_GROUPS * EXPERTS_PER_GROUP

EVEN_IN = 2 * RET_HEADS * RET_DK + 2 * RET_HEADS * RET_DV + (ATT_HEADS + 2 * ATT_KV_HEADS) * ATT_HD
EVEN_ATT_COL = 2 * RET_HEADS * RET_DK + 2 * RET_HEADS * RET_DV
DN_QKV = 2 * DN_HEADS * DN_DK + DN_HEADS * DN_DV
ODD_IN = DN_QKV + DN_HEADS * DN_DV + 4 * DN_HEADS
ODD_IN_PAD = ((ODD_IN + 127) // 128) * 128

LANES = 128
VMEM_LIMIT = 56 * 1024 * 1024

NT_DIMS = (((1,), (1,)), ((), ()))
TN_DIMS = (((0,), (0,)), ((), ()))


def _cparams(sem):
    return pltpu.CompilerParams(dimension_semantics=sem, vmem_limit_bytes=VMEM_LIMIT)


def _silu(x):
    return x / (1.0 + jnp.exp(-x))


def _dot(a, b):
    return jnp.dot(a, b, preferred_element_type=F32)


def _adaln_kernel(c_ref, w_ref, b_ref, o_ref):
    s = _silu(c_ref[...])
    o_ref[...] = _dot(s.astype(BF16), w_ref[...].astype(BF16)) + b_ref[...]


def adaln(c8, w_ada, b_ada):
    depth, d, n6 = w_ada.shape
    tn = min(n6, 1536)
    return pl.pallas_call(
        _adaln_kernel,
        grid=(depth, n6 // tn),
        in_specs=[
            pl.BlockSpec((8, d), lambda l, j: (0, 0)),
            pl.BlockSpec((None, d, tn), lambda l, j: (l, 0, j)),
            pl.BlockSpec((None, 1, tn), lambda l, j: (l, 0, j)),
        ],
        out_specs=pl.BlockSpec((None, 8, tn), lambda l, j: (l, 0, j)),
        out_shape=jax.ShapeDtypeStruct((depth, 8, n6), F32),
        compiler_params=_cparams(("arbitrary", "arbitrary")),
        name="adaln",
    )(c8, w_ada, b_ada.reshape(depth, 1, n6))


def _norm_mod(x, gain, shift, scale):
    ms = jnp.mean(x * x, axis=-1, keepdims=True)
    h = x * lax.rsqrt(ms + EPS) * gain
    return h * (1.0 + scale) + shift


def _pack_bf16_pairs(x):
    n = x.shape[1] // 2
    hi = lax.bitcast_convert_type(x[:, :n].astype(BF16).astype(F32), U32)
    lo = lax.bitcast_convert_type(x[:, n:].astype(BF16).astype(F32), U32)
    return hi | (lo >> 16)


def _unpack_bf16_pairs(p):
    hi = lax.bitcast_convert_type(p & jnp.uint32(0xFFFF0000), F32)
    lo = lax.bitcast_convert_type(p << 16, F32)
    return hi, lo


def _mod_row_map(tm, seq, n_lat_batches):
    return lambda i: (jnp.minimum((i * tm) // seq, n_lat_batches), 0, 0)


def _inproj_even_kernel(x_ref, g_ref, sh_ref, sc_ref, w_ref, cos_ref, s1_ref, s2_ref, qg_ref, kg_ref, bd_ref,
                        rq_ref, rk_ref, vg_ref, aq_ref, ak_ref, av_ref):
    hb = _norm_mod(x_ref[...], g_ref[...], sh_ref[...], sc_ref[...]).astype(BF16)
    cos = cos_ref[...]
    s1 = s1_ref[...]
    s2 = s2_ref[...]
    bd = bd_ref[...]
    half = ATT_HD

    def proj(c0, width=LANES):
        return _dot(hb, w_ref[:, c0:c0 + width])

    def rope(x):
        return x * cos + pltpu.roll(x, LANES - 16, 1) * s1 + pltpu.roll(x, 16, 1) * s2

    def head_norm(x, gain):
        sq = x * x
        hi = sq.astype(BF16)
        lo = (sq - hi.astype(F32)).astype(BF16)
        ms = _dot(hi, bd) + _dot(lo, bd)
        return x * lax.rsqrt(ms + EPS) * gain

    wide = 2 * LANES
    qw = RET_HEADS * RET_DK
    for c0 in range(0, qw, wide):
        yq = proj(c0, wide)
        yk = proj(qw + c0, wide)
        for u in range(2):
            cs = slice(c0 + u * LANES, c0 + (u + 1) * LANES)
            us = slice(u * LANES, (u + 1) * LANES)
            rq_ref[:, cs] = rope(yq[:, us]).astype(BF16)
            rk_ref[:, cs] = (rope(yk[:, us]) * RET_DK ** -0.5).astype(BF16)
    vgw = 2 * RET_HEADS * RET_DV
    for c0 in range(0, vgw, 512):
        vg_ref[:, c0:c0 + 512] = proj(2 * qw + c0, 512).astype(BF16)

    qg = qg_ref[...]
    kg = kg_ref[...]
    a0 = EVEN_ATT_COL
    for c0 in range(0, ATT_HEADS * ATT_HD, wide):
        ya = proj(a0 + c0, wide)
        for u in range(2):
            y = rope(head_norm(ya[:, u * LANES:(u + 1) * LANES], qg)) * (ATT_HD ** -0.5 * math.log2(math.e))
            y = y.astype(BF16)
            hd0 = (c0 + u * LANES) // ATT_HD
            aq_ref[hd0] = y[:, :half]
            aq_ref[hd0 + 1] = y[:, half:]
    ykv = proj(a0 + ATT_HEADS * ATT_HD, wide)
    y = rope(head_norm(ykv[:, :LANES], kg)).astype(BF16)
    ak_ref[0] = y[:, :half]
    ak_ref[1] = y[:, half:]
    v = ykv[:, LANES:].astype(BF16)
    av_ref[0] = v[:, :half]
    av_ref[1] = v[:, half:]


def inproj_even(xa, gain, shift, scale, w, tabs, q_gain, k_gain, tm, seq, nb):
    ta, d = xa.shape
    t_lat = nb * seq
    cos_t, s1_t, s2_t = tabs
    n_tab = seq // tm

    def tab_map(i):
        r = i * tm
        return (jnp.where(r < t_lat, (r % seq) // tm, n_tab), 0)

    ii = np.arange(LANES)
    bd = jnp.asarray((ii[:, None] // ATT_HD == ii[None, :] // ATT_HD).astype(np.float32) / ATT_HD).astype(BF16)
    qg = jnp.tile(q_gain.astype(F32), LANES // ATT_HD).reshape(1, LANES)
    kg = jnp.tile(k_gain.astype(F32), LANES // ATT_HD).reshape(1, LANES)
    mrow = _mod_row_map(tm, seq, nb)
    tab_spec = pl.BlockSpec((tm, LANES), tab_map)
    one = lambda i: (0, 0)
    qw = RET_HEADS * RET_DK
    vgw = 2 * RET_HEADS * RET_DV
    return pl.pallas_call(
        _inproj_even_kernel,
        grid=(ta // tm,),
        in_specs=[
            pl.BlockSpec((tm, d), lambda i: (i, 0)),
            pl.BlockSpec((1, d), one),
            pl.BlockSpec((None, 1, d), mrow),
            pl.BlockSpec((None, 1, d), mrow),
            pl.BlockSpec((d, EVEN_IN), one),
            tab_spec, tab_spec, tab_spec,
            pl.BlockSpec((1, LANES), one), pl.BlockSpec((1, LANES), one),
            pl.BlockSpec((LANES, LANES), one),
        ],
        out_specs=[
            pl.BlockSpec((tm, qw), lambda i: (i, 0)),
            pl.BlockSpec((tm, qw), lambda i: (i, 0)),
            pl.BlockSpec((tm, vgw), lambda i: (i, 0)),
            pl.BlockSpec((ATT_HEADS, tm, ATT_HD), lambda i: (0, i, 0)),
            pl.BlockSpec((ATT_KV_HEADS, tm, ATT_HD), lambda i: (0, i, 0)),
            pl.BlockSpec((ATT_KV_HEADS, tm, ATT_HD), lambda i: (0, i, 0)),
        ],
        out_shape=[
            jax.ShapeDtypeStruct((ta, qw), BF16),
            jax.ShapeDtypeStruct((ta, qw), BF16),
            jax.ShapeDtypeStruct((ta, vgw), BF16),
            jax.ShapeDtypeStruct((ATT_HEADS, ta, ATT_HD), BF16),
            jax.ShapeDtypeStruct((ATT_KV_HEADS, ta, ATT_HD), BF16),
            jax.ShapeDtypeStruct((ATT_KV_HEADS, ta, ATT_HD), BF16),
        ],
        compiler_params=_cparams(("arbitrary",)),
        name="inproj_even",
    )(xa, gain.reshape(1, d), shift, scale, w, cos_t, s1_t, s2_t, qg, kg, bd)


def rope_tables(seq, tm):
    nf = ATT_HD // 4
    t = jnp.arange(seq)
    rows = (t // GRID_W).astype(F32)
    cols = (t % GRID_W).astype(F32)
    inv = ROPE_BASE ** (-jnp.arange(nf, dtype=F32) / nf)
    lane = np.arange(LANES)
    axis = (lane % ATT_HD) // (ATT_HD // 2)
    f = lane % nf
    upper = ((lane % (ATT_HD // 2)) >= nf)
    pos = jnp.where(jnp.asarray(axis)[None, :] == 0, rows[:, None], cols[:, None])
    ang = pos * inv[jnp.asarray(f)][None, :]
    cos = jnp.cos(ang)
    sin = jnp.sin(ang)
    s1 = jnp.where(jnp.asarray(upper)[None, :], 0.0, -sin)
    s2 = jnp.where(jnp.asarray(upper)[None, :], sin, 0.0)
    pad1 = jnp.ones((tm, LANES), F32)
    pad0 = jnp.zeros((tm, LANES), F32)
    return (jnp.concatenate([cos, pad1]), jnp.concatenate([s1, pad0]), jnp.concatenate([s2, pad0]))


def _retention_kernel(dec_ref, q_ref, k_ref, v_ref, g_ref, s0f_ref, s0b_ref,
                      o_ref, sff_ref, sfb_ref, st_ref, *, n_chunks, unroll):
    hp = pl.program_id(1)
    C = RET_CHUNK
    dk, dv = RET_DK, RET_DV
    pos = lax.broadcasted_iota(jnp.int32, (C, dk), 0).astype(F32)
    ii = lax.broadcasted_iota(jnp.int32, (C, C), 0)
    jj = lax.broadcasted_iota(jnp.int32, (C, C), 1)
    dpos = (ii - jj).astype(F32)
    heads = range(2)
    qs = [slice(hh * dk, (hh + 1) * dk) for hh in heads]
    vs = [slice(hh * dv, (hh + 1) * dv) for hh in heads]
    w_out, w_in, gcf, gcb, mask = [], [], [], [], []
    for hh in heads:
        h = 2 * hp + hh
        df = dec_ref[0, h]
        db = dec_ref[1, h]
        lf = -jnp.exp(jnp.full((C, C), df, F32))
        lb = -jnp.exp(jnp.full((C, C), db, F32))
        lfk = -jnp.exp(jnp.full((C, dk), df, F32))
        lbk = -jnp.exp(jnp.full((C, dk), db, F32))
        w_out.append(jnp.concatenate([jnp.exp(lfk * (C - 1.0 - pos)), jnp.exp(lbk * pos)], axis=1))
        w_in.append(jnp.concatenate([jnp.exp(lfk * (pos + 1.0)), jnp.exp(lbk * (C - pos))], axis=1))
        gcf.append(jnp.exp(-jnp.exp(jnp.full((dk, dv), df, F32)) * C))
        gcb.append(jnp.exp(-jnp.exp(jnp.full((dk, dv), db, F32)) * C))
        mask.append(jnp.where(dpos > 0, jnp.exp(lf * jnp.maximum(dpos, 0.0)),
                              jnp.where(dpos < 0, jnp.exp(lb * jnp.maximum(-dpos, 0.0)), 2.0)))

    def rows(n):
        return pl.ds(pl.multiple_of(n * C, C), C)

    items = [(u, hh) for u in range(unroll) for hh in heads]

    def sums_body(i, carry):
        kk = {}
        for u, hh in items:
            k = k_ref[rows(i * unroll + u), qs[hh]].astype(F32)
            kk[(u, hh)] = (jnp.concatenate([k, k], axis=1) * w_out[hh]).astype(BF16)
        kv = {(u, hh): lax.dot_general(kk[(u, hh)], v_ref[rows(i * unroll + u), vs[hh]], TN_DIMS,
                                       preferred_element_type=F32) for u, hh in items}
        for u, hh in items:
            st_ref[hh, i * unroll + u] = kv[(u, hh)]
        return carry

    lax.fori_loop(0, n_chunks // unroll, sums_body, 0)

    def scan_body(n, carry):
        n_rev = n_chunks - 1 - n
        out = []
        for hh in heads:
            sf, sb = carry[2 * hh], carry[2 * hh + 1]
            kvf = st_ref[hh, n, 0:dk, :]
            kvb = st_ref[hh, n_rev, dk:2 * dk, :]
            st_ref[hh, n, 0:dk, :] = sf
            st_ref[hh, n_rev, dk:2 * dk, :] = sb
            out += [gcf[hh] * sf + kvf, gcb[hh] * sb + kvb]
        return tuple(out)

    init = tuple(x for hh in heads for x in (s0f_ref[hh], s0b_ref[hh]))
    fin = lax.fori_loop(0, n_chunks, scan_body, init)
    for hh in heads:
        sff_ref[hh] = fin[2 * hh]
        sfb_ref[hh] = fin[2 * hh + 1]

    def out_body(i, carry):
        ns = [i * unroll + u for u in range(unroll)]
        qb = {(u, hh): q_ref[rows(ns[u]), qs[hh]] for u, hh in items}
        sc = {(u, hh): lax.dot_general(qb[(u, hh)], k_ref[rows(ns[u]), qs[hh]], NT_DIMS,
                                       preferred_element_type=F32) for u, hh in items}
        qw = {}
        for it in items:
            q = qb[it].astype(F32)
            qw[it] = (jnp.concatenate([q, q], axis=1) * w_in[it[1]]).astype(BF16)
        o1 = {(u, hh): _dot((sc[(u, hh)] * mask[hh]).astype(BF16), v_ref[rows(ns[u]), vs[hh]]) for u, hh in items}
        o2 = {(u, hh): _dot(qw[(u, hh)], st_ref[hh, ns[u]].astype(BF16)) for u, hh in items}
        for it in items:
            u, hh = it
            n = ns[u]
            o = o1[it] + o2[it]
            o = o * lax.rsqrt(jnp.mean(o * o, axis=-1, keepdims=True) + EPS)
            gate = g_ref[rows(n), vs[hh]].astype(F32)
            o_ref[rows(n), vs[hh]] = (_silu(gate) * o).astype(o_ref.dtype)
        return carry

    lax.fori_loop(0, n_chunks // unroll, out_body, 0)


def retention(dec, rq, rk, p, s0f, s0b, nb, seq, row_off_blocks):
    n_chunks = seq // RET_CHUNK
    hp_n = RET_HEADS // 2
    vcol = 0
    gcol = vcol + RET_HEADS * RET_DV // (2 * RET_DV)
    ta = rq.shape[0]
    st_spec = pl.BlockSpec((None, 2, RET_DK, RET_DV), lambda b, hp, *_: (b, hp, 0, 0))
    grid_spec = pltpu.PrefetchScalarGridSpec(
        num_scalar_prefetch=1,
        grid=(nb, hp_n),
        in_specs=[
            pl.BlockSpec((seq, 2 * RET_DK), lambda b, hp, *_: (row_off_blocks + b, hp)),
            pl.BlockSpec((seq, 2 * RET_DK), lambda b, hp, *_: (row_off_blocks + b, hp)),
            pl.BlockSpec((seq, 2 * RET_DV), lambda b, hp, *_: (row_off_blocks + b, vcol + hp)),
            pl.BlockSpec((seq, 2 * RET_DV), lambda b, hp, *_: (row_off_blocks + b, gcol + hp)),
            st_spec, st_spec,
        ],
        out_specs=[
            pl.BlockSpec((seq, 2 * RET_DV), lambda b, hp, *_: (b, hp)),
            st_spec, st_spec,
        ],
        scratch_shapes=[pltpu.VMEM((2, n_chunks, 2 * RET_DK, RET_DV), F32)],
    )
    st_shape = jax.ShapeDtypeStruct((nb, RET_HEADS, RET_DK, RET_DV), F32)
    return pl.pallas_call(
        functools.partial(_retention_kernel, n_chunks=n_chunks, unroll=math.gcd(n_chunks, 8)),
        grid_spec=grid_spec,
        out_shape=[jax.ShapeDtypeStruct((nb * seq, RET_HEADS * RET_DV), BF16), st_shape, st_shape],
        compiler_params=_cparams(("arbitrary", "arbitrary")),
        name="retention",
    )(dec, rq, rk, p, p, s0f, s0b)


ATT_VT_ROWS = ATT_HD + 16


ATT_PAIRS_PER_TRIP = 8


def _attn_kernel(q_ref, k_ref, vt_ref, o_ref, *s_refs, tk, c_start, c_end, rep):
    tq = q_ref.shape[2]
    sets = (s_refs[:rep], s_refs[rep:])
    last = c_end - 1

    def scores(bufs, j):
        j = jnp.minimum(j, last)
        c0 = pl.multiple_of(j * tk, tk)
        k = k_ref[pl.ds(c0, tk), :]
        mxs = []
        for r in range(rep):
            s = _dot(k, q_ref[r])
            bufs[r][...] = s
            mxs.append(jnp.max(s, axis=0, keepdims=True))
        return tuple(mxs)

    def softmax_pv(bufs, j, mxs, ms, accs):
        vt = vt_ref[j]
        new_m, new_acc = [], []
        for r in range(rep):
            m_new = jnp.maximum(ms[r], mxs[r])
            a = jnp.exp2(ms[r] - m_new)
            p = jnp.exp2(bufs[r][...] - m_new).astype(BF16)
            new_acc.append(a * accs[r] + _dot(vt, p))
            new_m.append(m_new)
        return tuple(new_m), tuple(new_acc)

    def pair(j, mx0, ms, accs):
        mx1 = scores(sets[1], j + 1)
        ms, accs = softmax_pv(sets[0], j, mx0, ms, accs)
        mx0 = scores(sets[0], j + 2)
        ms, accs = softmax_pv(sets[1], j + 1, mx1, ms, accs)
        return mx0, ms, accs

    def trip(t, carry):
        for u in range(ATT_PAIRS_PER_TRIP):
            carry = pair(c_start + 2 * (ATT_PAIRS_PER_TRIP * t + u), *carry)
        return carry

    n_pairs = (c_end - c_start) // 2
    n_trips = n_pairs // ATT_PAIRS_PER_TRIP
    ms = tuple(jnp.full((1, tq), -1e30, F32) for _ in range(rep))
    accs = tuple(jnp.zeros((ATT_VT_ROWS, tq), F32) for _ in range(rep))
    carry = (scores(sets[0], c_start), ms, accs)
    if n_trips:
        carry = lax.fori_loop(0, n_trips, trip, carry)
    for u in range(n_trips * ATT_PAIRS_PER_TRIP, n_pairs):
        carry = pair(c_start + 2 * u, *carry)
    mx0, ms, accs = carry
    if (c_end - c_start) % 2:
        ms, accs = softmax_pv(sets[0], last, mx0, ms, accs)
    outs = [(acc[:ATT_HD, :] / acc[ATT_HD:ATT_HD + 1, :]).T for acc in accs]
    o_ref[...] = jnp.concatenate(outs, axis=-1).astype(o_ref.dtype)


def attention(aq, kcat, vtcat, seq_q, q_off_blocks, c_start, tq, tk):
    rep = ATT_HEADS // ATT_KV_HEADS
    _, nb, lk, _ = kcat.shape
    nq = seq_q // tq
    n_chunks = lk // tk
    return pl.pallas_call(
        functools.partial(_attn_kernel, tk=tk, c_start=c_start, c_end=n_chunks, rep=rep),
        grid=(nb, ATT_KV_HEADS, nq),
        in_specs=[
            pl.BlockSpec((rep, ATT_HD, tq), lambda b, g, i: (g, 0, q_off_blocks + b * nq + i)),
            pl.BlockSpec((None, None, lk, ATT_HD), lambda b, g, i: (g, b, 0, 0)),
            pl.BlockSpec((None, None, n_chunks, ATT_VT_ROWS, tk), lambda b, g, i: (g, b, 0, 0, 0)),
        ],
        out_specs=pl.BlockSpec((tq, rep * ATT_HD), lambda b, g, i: (b * nq + i, g)),
        out_shape=jax.ShapeDtypeStruct((nb * seq_q, ATT_HEADS * ATT_HD), BF16),
        scratch_shapes=[pltpu.VMEM((tk, tq), F32) for _ in range(2 * rep)],
        compiler_params=_cparams(("arbitrary", "arbitrary", "arbitrary")),
        name="attention",
    )(aq, kcat, vtcat)


def _lat_ctx_specs(tm, width, n_lat_tiles):
    return [pl.BlockSpec((tm, width), lambda i: (jnp.minimum(i, n_lat_tiles - 1), 0)),
            pl.BlockSpec((tm, width), lambda i: (jnp.maximum(i - n_lat_tiles, 0), 0))]


def _outproj_even_kernel(r_lat, r_ctx, a_lat, a_ctx, w1_ref, w2_ref, res_ref, gate_ref, *rest, n_lat_tiles):
    route_in, (o_ref, *route_out) = rest[:6], rest[6:]
    is_lat = pl.program_id(0) < n_lat_tiles
    a1 = jnp.where(is_lat, r_lat[...], r_ctx[...])
    a2 = jnp.where(is_lat, a_lat[...], a_ctx[...])
    y = _dot(a1, w1_ref[...]) + _dot(a2, w2_ref[...])
    out = res_ref[...] + gate_ref[...] * y
    o_ref[...] = out
    _route_tile(out, *route_in, *route_out)


def outproj_even(ret_lat, ret_ctx, att_lat, att_ctx, w1, w2, xa, gate, route_params, tm, seq, nb):
    ta, d = xa.shape
    k1, k2 = w1.shape[0], w2.shape[0]
    n_lat_tiles = ret_lat.shape[0] // tm
    r_in, r_args, r_out, r_shape, r_scratch = _route_io(ta, d, tm, seq, nb, *route_params)
    return pl.pallas_call(
        functools.partial(_outproj_even_kernel, n_lat_tiles=n_lat_tiles),
        grid=(ta // tm,),
        in_specs=_lat_ctx_specs(tm, k1, n_lat_tiles) + _lat_ctx_specs(tm, k2, n_lat_tiles) + [
            pl.BlockSpec((k1, d), lambda i: (0, 0)),
            pl.BlockSpec((k2, d), lambda i: (0, 0)),
            pl.BlockSpec((tm, d), lambda i: (i, 0)),
            pl.BlockSpec((None, 1, d), _mod_row_map(tm, seq, nb)),
        ] + r_in,
        out_specs=[pl.BlockSpec((tm, d), lambda i: (i, 0))] + r_out,
        out_shape=[jax.ShapeDtypeStruct((ta, d), F32)] + r_shape,
        scratch_shapes=r_scratch,
        compiler_params=_cparams(("arbitrary",)),
        name="outproj_even",
    )(ret_lat, ret_ctx, att_lat, att_ctx, w1, w2, xa, gate, *r_args)


def _outproj_odd_kernel(f_lat, f_ctx, b_lat, b_ctx, z_ref, og_ref, w_ref, res_ref, gate_ref, *rest, n_lat_tiles):
    route_in, (o_ref, *route_out) = rest[:6], rest[6:]
    is_lat = pl.program_id(0) < n_lat_tiles
    og = og_ref[...]
    parts = []
    for h in range(DN_HEADS):
        cs = slice(h * DN_DV, (h + 1) * DN_DV)
        of = jnp.where(is_lat, f_lat[:, cs], f_ctx[:, cs]).astype(F32)
        ob = jnp.where(is_lat, b_lat[:, cs], b_ctx[:, cs]).astype(F32)
        o = of + ob
        o = o * lax.rsqrt(jnp.mean(o * o, axis=-1, keepdims=True) + EPS) * og
        parts.append((o * _silu(z_ref[:, cs].astype(F32))).astype(BF16))
    a = jnp.concatenate(parts, axis=-1)
    out = res_ref[...] + gate_ref[...] * _dot(a, w_ref[...])
    o_ref[...] = out
    _route_tile(out, *route_in, *route_out)


def outproj_odd(of_lat, of_ctx, ob_lat, ob_ctx, p, out_gain, w, xa, gate, route_params, tm, seq, nb):
    ta, d = xa.shape
    kdim = DN_HEADS * DN_DV
    n_lat_tiles = of_lat.shape[0] // tm
    r_in, r_args, r_out, r_shape, r_scratch = _route_io(ta, d, tm, seq, nb, *route_params)
    return pl.pallas_call(
        functools.partial(_outproj_odd_kernel, n_lat_tiles=n_lat_tiles),
        grid=(ta // tm,),
        in_specs=_lat_ctx_specs(tm, kdim, n_lat_tiles) + _lat_ctx_specs(tm, kdim, n_lat_tiles) + [
            pl.BlockSpec((tm, kdim), lambda i: (i, 0)),
            pl.BlockSpec((1, DN_DV), lambda i: (0, 0)),
            pl.BlockSpec((kdim, d), lambda i: (0, 0)),
            pl.BlockSpec((tm, d), lambda i: (i, 0)),
            pl.BlockSpec((None, 1, d), _mod_row_map(tm, seq, nb)),
        ] + r_in,
        out_specs=[pl.BlockSpec((tm, d), lambda i: (i, 0))] + r_out,
        out_shape=[jax.ShapeDtypeStruct((ta, d), F32)] + r_shape,
        scratch_shapes=r_scratch,
        compiler_params=_cparams(("arbitrary",)),
        name="outproj_odd",
    )(of_lat, of_ctx, ob_lat, ob_ctx, p, out_gain.reshape(1, DN_DV).astype(F32), w, xa, gate, *r_args)


def _inproj_odd_kernel(first_ref, last_ref, x_ref, xp_ref, xn_ref, g_ref, sh_ref, sc_ref, w_ref, cw_ref,
                       arow_ref, brow_ref, q_ref, k_ref, v_ref, z_ref, gb_ref, *, ctx_tile0, cseq):
    i = pl.program_id(0)
    tm = x_ref.shape[0]
    gain, shift, scale = g_ref[...], sh_ref[...], sc_ref[...]
    hrows = xp_ref.shape[0]
    hall = _norm_mod(jnp.concatenate([x_ref[...], xp_ref[...], xn_ref[...]], axis=0), gain, shift, scale).astype(BF16)
    hb = hall[:tm]
    keep_prev = 1.0 - first_ref[i].astype(F32)
    keep_next = 1.0 - last_ref[i].astype(F32)
    row = lax.broadcasted_iota(jnp.int32, (tm, LANES), 0)
    is_first = row == 0
    is_last = row == tm - 1
    inner = cseq < tm
    if inner:
        in_ctx = i >= ctx_tile0
        local = row & (cseq - 1)
        zero_dn = jnp.logical_and(in_ctx, local == 0)
        zero_up = jnp.logical_and(in_ctx, local == cseq - 1)
    n_qk = 2 * DN_HEADS * DN_DK // LANES
    n_q = DN_HEADS * DN_DK // LANES
    outs = (q_ref, k_ref, v_ref)
    wide = 2 * LANES
    for c0 in range(0, DN_QKV, wide):
        yall = _dot(hall, w_ref[:, c0:c0 + wide])
        y2 = yall[:tm]
        yh2 = yall[tm:]
        for u in range(2):
            j = c0 // LANES + u
            us = slice(u * LANES, (u + 1) * LANES)
            x = y2[:, us]
            xp = yh2[hrows - 1:hrows, us] * keep_prev
            xn = yh2[hrows:hrows + 1, us] * keep_next
            x_dn = jnp.where(is_first, xp, pltpu.roll(x, 1, 0))
            x_up = jnp.where(is_last, xn, pltpu.roll(x, tm - 1, 0))
            if inner:
                x_dn = jnp.where(zero_dn, 0.0, x_dn)
                x_up = jnp.where(zero_up, 0.0, x_up)
            w = cw_ref[:, j * LANES:(j + 1) * LANES]
            y = _silu(x_dn * w[0:1, :] + x * w[1:2, :] + x_up * w[2:3, :])
            if j < n_qk:
                y = y * lax.rsqrt(jnp.sum(y * y, axis=-1, keepdims=True) + EPS)
                if j < n_q:
                    y = y * DN_DK ** -0.5
            lj = j % n_q
            outs[j // n_q][:, lj * LANES:(lj + 1) * LANES] = y.astype(BF16)
    zw = DN_HEADS * DN_DV
    for c0 in range(0, zw, 512):
        z_ref[:, c0:c0 + 512] = _dot(hb, w_ref[:, DN_QKV + c0:DN_QKV + c0 + 512]).astype(BF16)

    a = _dot(hb, w_ref[:, DN_QKV + zw:DN_QKV + zw + LANES])
    lane = lax.broadcasted_iota(jnp.int32, (tm, LANES), 1)
    zz = a + brow_ref[...]
    softplus = jnp.maximum(zz, 0.0) + jnp.log(1.0 + jnp.exp(-jnp.abs(zz)))
    g = -jnp.exp(arow_ref[...]) * softplus
    beta = 1.0 / (1.0 + jnp.exp(-a))
    gb_ref[...] = jnp.where(lane < 2 * DN_HEADS, g, jnp.where(lane < 4 * DN_HEADS, beta, 0.0))


def inproj_odd(xa, gain, shift, scale, w, conv_w, arow, brow, first_flags, last_flags, tm, seq, nb, cseq):
    ta, d = xa.shape
    halo = 8
    hb = tm // halo
    n_h = ta // halo
    kdim = DN_HEADS * DN_DK
    assert cseq >= tm or (tm % cseq == 0 and cseq & (cseq - 1) == 0)
    mrow = lambda i, *_: _mod_row_map(tm, seq, nb)(i)
    one = lambda i, *_: (0, 0)
    row_blk = lambda i, *_: (i, 0)
    grid_spec = pltpu.PrefetchScalarGridSpec(
        num_scalar_prefetch=2,
        grid=(ta // tm,),
        in_specs=[
            pl.BlockSpec((tm, d), row_blk),
            pl.BlockSpec((halo, d), lambda i, *_: (jnp.maximum(i * hb - 1, 0), 0)),
            pl.BlockSpec((halo, d), lambda i, *_: (jnp.minimum((i + 1) * hb, n_h - 1), 0)),
            pl.BlockSpec((1, d), one),
            pl.BlockSpec((None, 1, d), mrow),
            pl.BlockSpec((None, 1, d), mrow),
            pl.BlockSpec((d, ODD_IN_PAD), one),
            pl.BlockSpec((DN_CONV, DN_QKV), one),
            pl.BlockSpec((1, LANES), one),
            pl.BlockSpec((1, LANES), one),
        ],
        out_specs=[pl.BlockSpec((tm, kdim), row_blk)] * 4 + [pl.BlockSpec((tm, LANES), row_blk)],
    )
    return pl.pallas_call(
        functools.partial(_inproj_odd_kernel, ctx_tile0=nb * seq // tm, cseq=cseq),
        grid_spec=grid_spec,
        out_shape=[jax.ShapeDtypeStruct((ta, kdim), BF16)] * 4 + [jax.ShapeDtypeStruct((ta, LANES), F32)],
        compiler_params=_cparams(("arbitrary",)),
        name="inproj_odd",
    )(first_flags, last_flags, xa, xa, xa, gain.reshape(1, d), shift, scale, w, conv_w, arow, brow)


def _deltanet_bidir_kernel(qf_ref, kf_ref, vf_ref, gbf_ref, gbtf_ref, qb_ref, kb_ref, vb_ref, gbb_ref, gbtb_ref,
                           s0f_ref, s0b_ref, of_ref, ob_ref, sff_ref, sfb_ref, sf_scr, sb_scr, *, n_chunks):
    t = pl.program_id(1)

    @pl.when(t == 0)
    def _():
        sf_scr[...] = s0f_ref[...]
        sb_scr[...] = s0b_ref[...]

    C = DN_CHUNK
    ii = lax.broadcasted_iota(jnp.int32, (C, C), 0)
    jj = lax.broadcasted_iota(jnp.int32, (C, C), 1)
    lower, upper = ii >= jj, ii <= jj
    eye = jnp.where(ii == jj, 1.0, 0.0).astype(F32)
    blk = ii ^ jj
    dirs = (
        dict(rev=False, incl=lower, strict=ii > jj, q=qf_ref, k=kf_ref, v=vf_ref, gb=gbf_ref, gbt=gbtf_ref,
             o=of_ref, scr=sf_scr, off=0, order=list(range(n_chunks))),
        dict(rev=True, incl=upper, strict=ii < jj, q=qb_ref, k=kb_ref, v=vb_ref, gb=gbb_ref, gbt=gbtb_ref,
             o=ob_ref, scr=sb_scr, off=DN_HEADS, order=list(range(n_chunks - 1, -1, -1))),
    )
    items = [(d, c, h) for d in range(2) for c in dirs[d]["order"] for h in range(DN_HEADS)]

    gcols, grows, gbs = {}, {}, {}
    for d, dr in enumerate(dirs):
        tri = jnp.where(dr["incl"], 1.0, 0.0).astype(F32)
        tri_t = jnp.where(upper if not dr["rev"] else lower, 1.0, 0.0).astype(F32)
        for c in dr["order"]:
            gb_c = dr["gb"][c * C:(c + 1) * C, :]
            gbs[(d, c)] = gb_c
            gcols[(d, c)] = jnp.dot(tri, gb_c, preferred_element_type=F32, precision=HIGHEST)
            grows[(d, c)] = jnp.dot(dr["gbt"][c], tri_t, preferred_element_type=F32, precision=HIGHEST)

    qb, kb16, decay, kbeta, egc, kd, gl, rhs = {}, {}, {}, {}, {}, {}, {}, {}
    for it in items:
        d, c, h = it
        dr = dirs[d]
        gi = dr["off"] + h
        bi = 2 * DN_HEADS + dr["off"] + h
        rows = slice(c * C, (c + 1) * C)
        cs = slice(h * DN_DK, (h + 1) * DN_DK)
        gc = gcols[(d, c)][:, gi:gi + 1]
        gr = grows[(d, c)][gi:gi + 1, :]
        beta = gbs[(d, c)][:, bi:bi + 1]
        qb[it] = dr["q"][rows, cs]
        kb16[it] = dr["k"][rows, cs]
        kf = kb16[it].astype(F32)
        decay[it] = jnp.where(dr["incl"], jnp.exp(jnp.where(dr["incl"], gc - gr, 0.0)), 0.0)
        kbeta[it] = kf * beta
        egc[it] = jnp.exp(gc)
        glast = gc[0:1, :] if dr["rev"] else gc[C - 1:C, :]
        kd[it] = (kf * jnp.exp(glast - gc)).astype(BF16)
        gl[it] = jnp.exp(glast)
        rhs[it] = jnp.concatenate([dr["v"][rows, cs].astype(F32) * beta, kbeta[it] * egc[it]], axis=1).astype(BF16)

    kk = {it: lax.dot_general(kbeta[it].astype(BF16), kb16[it], NT_DIMS, preferred_element_type=F32)
          for it in items}
    qk = {it: lax.dot_general(qb[it], kb16[it], NT_DIMS, preferred_element_type=F32) for it in items}
    lm = {it: jnp.where(dirs[it[0]]["strict"], kk[it] * decay[it], 0.0) for it in items}
    attn = {it: jnp.where(dirs[it[0]]["incl"], qk[it] * decay[it], 0.0).astype(BF16) for it in items}
    dinv = {it: eye - jnp.where(blk < 2, lm[it], 0.0) for it in items}
    s = 2
    while s < C:
        in_band = jnp.logical_and(blk >= s, blk < 2 * s)
        tmp = {it: _dot(dinv[it].astype(BF16), jnp.where(in_band, lm[it], 0.0).astype(BF16)) for it in items}
        dinv = {it: dinv[it] - _dot(tmp[it].astype(BF16), dinv[it].astype(BF16)) for it in items}
        s *= 2
    uw = {it: _dot(dinv[it].astype(BF16), rhs[it]) for it in items}
    wq = {it: jnp.concatenate([uw[it][:, DN_DV:], qb[it].astype(F32) * egc[it]], axis=0).astype(BF16)
          for it in items}

    states = {(d, h): dirs[d]["scr"][h] for d in range(2) for h in range(DN_HEADS)}
    for step in range(n_chunks):
        its = [(d, dirs[d]["order"][step], h) for d in range(2) for h in range(DN_HEADS)]
        r = {it: _dot(wq[it], states[(it[0], it[2])].astype(BF16)) for it in its}
        v_new = {it: (uw[it][:, :DN_DV] - r[it][:C]).astype(BF16) for it in its}
        o = {it: r[it][C:] + _dot(attn[it], v_new[it]) for it in its}
        for it in its:
            key = (it[0], it[2])
            states[key] = states[key] * gl[it] + lax.dot_general(kd[it], v_new[it], TN_DIMS,
                                                                 preferred_element_type=F32)
        for it in its:
            d, c, h = it
            dirs[d]["o"][c * C:(c + 1) * C, h * DN_DK:(h + 1) * DN_DK] = o[it].astype(of_ref.dtype)
    for (d, h), st in states.items():
        dirs[d]["scr"][h] = st

    @pl.when(t == pl.num_programs(1) - 1)
    def _():
        sff_ref[...] = sf_scr[...]
        sfb_ref[...] = sb_scr[...]


def deltanet_bidir(q, k, v, gb, gbt, s0f, s0b, nb, seq, row_off, tl):
    nblk = seq // tl
    n_chunks = tl // DN_CHUNK
    off_b = row_off // tl
    kdim = DN_HEADS * DN_DK

    def fwd_rb(b, t):
        return off_b + b * nblk + t

    def bwd_rb(b, t):
        return off_b + b * nblk + (nblk - 1 - t)

    def seq_specs(rb):
        spec = pl.BlockSpec((tl, kdim), lambda b, t: (rb(b, t), 0))
        return [spec, spec, spec,
                pl.BlockSpec((tl, LANES), lambda b, t: (rb(b, t), 0)),
                pl.BlockSpec((n_chunks, 4 * DN_HEADS, DN_CHUNK), lambda b, t: (rb(b, t), 0, 0))]

    st_spec = pl.BlockSpec((None, DN_HEADS, DN_DK, DN_DV), lambda b, t: (b, 0, 0, 0))
    st_shape = jax.ShapeDtypeStruct((nb, DN_HEADS, DN_DK, DN_DV), F32)
    o_shape = jax.ShapeDtypeStruct((nb * seq, kdim), BF16)
    return pl.pallas_call(
        functools.partial(_deltanet_bidir_kernel, n_chunks=n_chunks),
        grid=(nb, nblk),
        in_specs=seq_specs(fwd_rb) + seq_specs(bwd_rb) + [st_spec, st_spec],
        out_specs=[
            pl.BlockSpec((tl, kdim), lambda b, t: (b * nblk + t, 0)),
            pl.BlockSpec((tl, kdim), lambda b, t: (b * nblk + nblk - 1 - t, 0)),
            st_spec, st_spec,
        ],
        out_shape=[o_shape, o_shape, st_shape, st_shape],
        scratch_shapes=[pltpu.VMEM((DN_HEADS, DN_DK, DN_DV), F32), pltpu.VMEM((DN_HEADS, DN_DK, DN_DV), F32)],
        compiler_params=_cparams(("arbitrary", "arbitrary")),
        name="deltanet_bidir",
    )(q, k, v, gb, gbt, q, k, v, gb, gbt, s0f, s0b)


def _route_tile(x, g_ref, sh_ref, sc_ref, wr_ref, br_ref, ltri_ref, f_ref, r_ref, cnt_ref, base_ref):
    @pl.when(pl.program_id(0) == 0)
    def _():
        base_ref[...] = jnp.zeros_like(base_ref)

    h = _norm_mod(x, g_ref[...], sh_ref[...], sc_ref[...])
    f_ref[...] = _pack_bf16_pairs(h)
    logits = _dot(h.astype(BF16), wr_ref[...]) + br_ref[...]
    tm = logits.shape[0]
    lane = lax.broadcasted_iota(jnp.int32, (tm, LANES), 1)
    neg = -1e30
    big = 4 * LANES
    is_g = lane < N_GROUPS
    gl = jnp.where(is_g, logits, neg)
    gm = jnp.max(gl, axis=-1, keepdims=True)
    grp = jnp.min(jnp.where(gl == gm, lane, big), axis=-1, keepdims=True)
    psum = jnp.sum(jnp.where(is_g, jnp.exp(gl - gm), 0.0), axis=-1, keepdims=True)
    p_grp = 1.0 / psum
    e_lane = lane - N_GROUPS
    in_grp = jnp.logical_and(jnp.logical_and(e_lane >= 0, e_lane < N_EXPERTS),
                             (e_lane // EXPERTS_PER_GROUP) == grp)
    el = jnp.where(in_grp, logits, neg)
    m1 = jnp.max(el, axis=-1, keepdims=True)
    i1 = jnp.min(jnp.where(el == m1, lane, big), axis=-1, keepdims=True)
    el2 = jnp.where(lane == i1, neg, el)
    m2 = jnp.max(el2, axis=-1, keepdims=True)
    i2 = jnp.min(jnp.where(el2 == m2, lane, big), axis=-1, keepdims=True)
    e21 = jnp.exp(m2 - m1)
    w1 = p_grp / (1.0 + e21)
    w2 = p_grp * e21 / (1.0 + e21)
    e1 = (i1 - N_GROUPS).astype(F32)
    e2 = (i2 - N_GROUPS).astype(F32)
    oh1 = lane == i1
    oh2 = lane == i2
    oh1f = jnp.where(oh1, 1.0, 0.0)
    oh2f = jnp.where(oh2, 1.0, 0.0)
    ltri = ltri_ref[...]
    before1 = _dot(ltri, oh1f.astype(BF16))
    before2 = _dot(ltri, oh2f.astype(BF16))
    cnt1 = jnp.sum(oh1f, axis=0, keepdims=True)
    cnt2 = jnp.sum(oh2f, axis=0, keepdims=True)
    base = base_ref[0:1, :]
    rank1 = jnp.sum(jnp.where(oh1, base + before1, 0.0), axis=-1, keepdims=True)
    rank2 = jnp.sum(jnp.where(oh2, base + cnt1 + before2, 0.0), axis=-1, keepdims=True)
    total = base + cnt1 + cnt2
    base_ref[...] = jnp.broadcast_to(total, base_ref.shape)
    cnt_ref[...] = jnp.broadcast_to(total, cnt_ref.shape)
    vals = (e1, e2, w1, w2, rank1, rank2)
    out = jnp.zeros((tm, LANES), F32)
    for idx, val in enumerate(vals):
        out = jnp.where(lane == idx, val, out)
    r_ref[...] = out


def _route_io(ta, d, tm, seq, nb, gain, shift, scale, w_router, b_router):
    mrow = _mod_row_map(tm, seq, nb)
    ii = np.arange(tm)
    ltri = jnp.asarray((ii[:, None] > ii[None, :]).astype(np.float32)).astype(BF16)
    in_specs = [
        pl.BlockSpec((1, d), lambda i: (0, 0)),
        pl.BlockSpec((None, 1, d), mrow),
        pl.BlockSpec((None, 1, d), mrow),
        pl.BlockSpec((d, LANES), lambda i: (0, 0)),
        pl.BlockSpec((1, LANES), lambda i: (0, 0)),
        pl.BlockSpec((tm, tm), lambda i: (0, 0)),
    ]
    args = [gain.reshape(1, d), shift, scale, w_router, b_router, ltri]
    out_specs = [pl.BlockSpec((tm, d // 2), lambda i: (i, 0)), pl.BlockSpec((tm, LANES), lambda i: (i, 0)),
                 pl.BlockSpec((8, LANES), lambda i: (0, 0))]
    out_shape = [jax.ShapeDtypeStruct((ta, d // 2), U32), jax.ShapeDtypeStruct((ta, LANES), F32),
                 jax.ShapeDtypeStruct((8, LANES), F32)]
    return in_specs, args, out_specs, out_shape, [pltpu.VMEM((8, LANES), F32)]


ROW_DMA_UNROLL = 32


def _issue_row_copies(n_rows, make_copy):
    def trip(i, carry):
        for u in range(ROW_DMA_UNROLL):
            make_copy(i * ROW_DMA_UNROLL + u).start(priority=u % 2)
        return carry

    lax.fori_loop(0, n_rows // ROW_DMA_UNROLL, trip, 0)


def _moe_scatter_kernel(pos_ref, f_ref, xs_in, xs_out, sem):
    del xs_in
    tm = f_ref.shape[0]
    for k in range(TOP_K):
        _issue_row_copies(tm, lambda r, k=k: pltpu.make_async_copy(
            f_ref.at[pl.ds(r, 1)], xs_out.at[pl.ds(pos_ref[0, 0, k * tm + r], 1)], sem))
    for _ in range(2):
        pltpu.make_async_copy(f_ref, xs_out.at[pl.ds(0, tm)], sem).wait()


def moe_scatter(pos_tiles, f, xs_zero, tm):
    ta, d = f.shape
    return pl.pallas_call(
        _moe_scatter_kernel,
        grid=(ta // tm,),
        in_specs=[
            pl.BlockSpec((1, 1, 2 * tm), lambda i: (i, 0, 0), memory_space=pltpu.SMEM),
            pl.BlockSpec((tm, d), lambda i: (i, 0)),
            pl.BlockSpec(memory_space=pl.ANY),
        ],
        out_specs=pl.BlockSpec(memory_space=pl.ANY),
        out_shape=jax.ShapeDtypeStruct(xs_zero.shape, xs_zero.dtype),
        scratch_shapes=[pltpu.SemaphoreType.DMA(())],
        input_output_aliases={2: 0},
        compiler_params=_cparams(("arbitrary",)),
        name="moe_scatter",
    )(pos_tiles, f, xs_zero)


def _moe_ffn_kernel(te_ref, nu_ref, x_ref, wgu_ref, wd_ref, o_ref, wgu_bf, wd_bf):
    i = pl.program_id(0)
    fdim = wd_bf.shape[0]

    @pl.when(i < nu_ref[0])
    def _():
        prev = te_ref[jnp.maximum(i - 1, 0)]
        changed = jnp.logical_or(i == 0, te_ref[i] != prev)

        @pl.when(changed)
        def _():
            wgu_bf[...] = wgu_ref[...].astype(BF16)
            wd_bf[...] = wd_ref[...].astype(BF16)

        x_hi, x_lo = _unpack_bf16_pairs(x_ref[...])
        half = x_hi.shape[1]
        gu = _dot(x_hi.astype(BF16), wgu_bf[:half, :]) + _dot(x_lo.astype(BF16), wgu_bf[half:, :])
        hmid = _silu(gu[:, :fdim]) * gu[:, fdim:]
        o_ref[...] = _pack_bf16_pairs(_dot(hmid.astype(BF16), wd_bf[...]))

    @pl.when(i >= nu_ref[0])
    def _():
        o_ref[...] = jnp.zeros_like(o_ref)


def moe_ffn(tile_expert, n_used, xs, w_gate_up, w_down, layer, tm):
    n_pad, dh = xs.shape
    d = 2 * dh
    f2 = w_gate_up.shape[-1]
    fdim = w_down.shape[-2]
    grid_spec = pltpu.PrefetchScalarGridSpec(
        num_scalar_prefetch=2,
        grid=(n_pad // tm,),
        in_specs=[
            pl.BlockSpec((tm, dh), lambda i, te, nu: (i, 0)),
            pl.BlockSpec((None, None, d, f2), lambda i, te, nu: (layer, te[i], 0, 0)),
            pl.BlockSpec((None, None, fdim, d), lambda i, te, nu: (layer, te[i], 0, 0)),
        ],
        out_specs=pl.BlockSpec((tm, dh), lambda i, te, nu: (i, 0)),
        scratch_shapes=[pltpu.VMEM((d, f2), BF16), pltpu.VMEM((fdim, d), BF16)],
    )
    return pl.pallas_call(
        _moe_ffn_kernel,
        grid_spec=grid_spec,
        out_shape=jax.ShapeDtypeStruct((n_pad, dh), U32),
        compiler_params=_cparams(("arbitrary",)),
        name="moe_ffn",
    )(tile_expert, n_used, xs, w_gate_up, w_down)


def _moe_combine_kernel(pos_ref, posn_ref, x_ref, gate_ref, r_ref, fg_ref, y_hbm, o_ref, ybuf, sem, *, final):
    tm = x_ref.shape[0]
    i = pl.program_id(0)
    n = pl.num_programs(0)

    def gather(p_ref, slot):
        _issue_row_copies(2 * tm, lambda r: pltpu.make_async_copy(
            y_hbm.at[pl.ds(p_ref[0, 0, r], 1)], ybuf.at[slot, pl.ds(r, 1)], sem.at[slot]))

    @pl.when(i == 0)
    def _():
        gather(pos_ref, 0)

    @pl.when(i + 1 < n)
    def _():
        gather(posn_ref, (i + 1) % 2)

    slot = i % 2
    pltpu.make_async_copy(y_hbm.at[pl.ds(0, 2 * tm)], ybuf.at[slot], sem.at[slot]).wait()
    route = r_ref[...]
    w0, w1 = route[:, 2:3], route[:, 3:4]
    y0_hi, y0_lo = _unpack_bf16_pairs(ybuf[slot, 0:tm, :])
    y1_hi, y1_lo = _unpack_bf16_pairs(ybuf[slot, tm:2 * tm, :])
    y = jnp.concatenate([w0 * y0_hi + w1 * y1_hi, w0 * y0_lo + w1 * y1_lo], axis=-1)
    out = x_ref[...] + gate_ref[...] * y
    if final:
        out = out * lax.rsqrt(jnp.mean(out * out, axis=-1, keepdims=True) + EPS) * fg_ref[...]
    o_ref[...] = out


def moe_combine(pos_tiles, xa, gate, route, y_sorted, final_gain, n_rows, final, tm, seq, nb):
    d = xa.shape[1]
    n_tiles = n_rows // tm
    return pl.pallas_call(
        functools.partial(_moe_combine_kernel, final=final),
        grid=(n_tiles,),
        in_specs=[
            pl.BlockSpec((1, 1, 2 * tm), lambda i: (i, 0, 0), memory_space=pltpu.SMEM),
            pl.BlockSpec((1, 1, 2 * tm), lambda i: (jnp.minimum(i + 1, n_tiles - 1), 0, 0),
                         memory_space=pltpu.SMEM),
            pl.BlockSpec((tm, d), lambda i: (i, 0)),
            pl.BlockSpec((None, 1, d), _mod_row_map(tm, seq, nb)),
            pl.BlockSpec((tm, LANES), lambda i: (i, 0)),
            pl.BlockSpec((1, d), lambda i: (0, 0)),
            pl.BlockSpec(memory_space=pl.ANY),
        ],
        out_specs=pl.BlockSpec((tm, d), lambda i: (i, 0)),
        out_shape=jax.ShapeDtypeStruct((n_rows, d), F32),
        scratch_shapes=[pltpu.VMEM((2, 2 * tm, d // 2), U32), pltpu.SemaphoreType.DMA((2,))],
        compiler_params=_cparams(("arbitrary",)),
        name="moe_combine",
    )(pos_tiles, pos_tiles, xa, gate, route, final_gain.reshape(1, d).astype(F32), y_sorted)


def moe_slots(route, counts, tm_ffn, tm_tok):
    ta = route.shape[0]
    ids = route[:, 0:TOP_K].astype(jnp.int32)
    rank = route[:, 2 * TOP_K:3 * TOP_K].astype(jnp.int32)
    counts = counts[0, N_GROUPS:N_GROUPS + N_EXPERTS].astype(jnp.int32)
    padded = ((counts + tm_ffn - 1) // tm_ffn) * tm_ffn
    ends = jnp.cumsum(padded)
    starts = ends - padded
    experts = jnp.arange(N_EXPERTS, dtype=jnp.int32)
    pos = jnp.sum(jnp.where(ids[..., None] == experts, starts, 0), axis=-1) + rank
    n_tiles = (TOP_K * ta + N_EXPERTS * (tm_ffn - 1)) // tm_ffn
    tile_start = jnp.arange(n_tiles, dtype=jnp.int32) * tm_ffn
    tile_expert = jnp.sum((tile_start[:, None] >= ends[None, :]).astype(jnp.int32), axis=1)
    tile_expert = jnp.minimum(tile_expert, N_EXPERTS - 1)
    n_used = (ends[-1] // tm_ffn).astype(jnp.int32).reshape(1)
    pos_tiles = pos.reshape(ta // tm_tok, tm_tok, TOP_K).transpose(0, 2, 1).reshape(ta // tm_tok, 1, TOP_K * tm_tok)
    return tile_expert, n_used, n_tiles * tm_ffn, pos_tiles


def _seq_flags(t_lat, seq, tc, cseq, tm):
    starts = np.arange(0, t_lat + tc, tm)
    first = np.where(starts < t_lat, starts % seq == 0, (starts - t_lat) % cseq == 0)
    ends = starts + tm
    last = np.where(starts < t_lat, ends % seq == 0, (ends - t_lat) % cseq == 0)
    return jnp.asarray(first.astype(np.int32)), jnp.asarray(last.astype(np.int32))


def kernel(x, c, ctx, c_ctx, w_ada, b_ada, norm_mix, norm_ffn, ev_w_in, ev_q_gain, ev_k_gain, ev_decay_f,
           ev_decay_b, ev_w_out, od_w_in, od_conv, od_a_log_f, od_a_log_b, od_dt_bias_f, od_dt_bias_b,
           od_out_gain, od_w_out, moe_w_group, moe_b_group, moe_w_expert, moe_b_expert, moe_w_gate_up,
           moe_w_down, final_norm_gain):
    nb, seq, d = x.shape
    cseq = ctx.shape[1]
    depth = w_ada.shape[0]
    t_lat = nb * seq
    tc = nb * cseq
    assert nb + 1 <= 8 and seq % cseq == 0 and cseq % RET_CHUNK == 0 and seq % GRID_W == 0

    tm = 512 if tc % 512 == 0 else cseq
    tq = min(256, cseq)
    tk = min(256, cseq)
    tl = 2 * DN_CHUNK
    tm_ffn = 512
    tm_comb = tm

    xa = jnp.concatenate([x.reshape(t_lat, d), ctx.reshape(tc, d)], axis=0)
    c8 = jnp.zeros((8, d), F32).at[:nb].set(c).at[nb].set(c_ctx)
    mod = adaln(c8, w_ada, b_ada)

    tabs = rope_tables(seq, tm)
    first_flags, last_flags = _seq_flags(t_lat, seq, tc, cseq, tm)
    ret_zero = jnp.zeros((nb, RET_HEADS, RET_DK, RET_DV), F32)
    dn_zero = jnp.zeros((nb, DN_HEADS, DN_DK, DN_DV), F32)

    xs = None
    for layer in range(depth):
        m = mod[layer].reshape(8, 6, 1, d)
        sh1, sc1, g1, sh2, sc2, g2 = (m[:, j] for j in range(6))
        w_router = jnp.pad(jnp.concatenate([moe_w_group[layer], moe_w_expert[layer]], axis=1),
                           ((0, 0), (0, LANES - N_GROUPS - N_EXPERTS))).astype(BF16)
        b_router = jnp.pad(jnp.concatenate([moe_b_group[layer], moe_b_expert[layer]]),
                           (0, LANES - N_GROUPS - N_EXPERTS)).reshape(1, LANES).astype(F32)
        route_params = (norm_ffn[layer], sh2, sc2, w_router, b_router)
        i = layer // 2
        if layer % 2 == 0:
            w_in = ev_w_in[i].astype(BF16)
            rq, rk, p, aq, ak, av = inproj_even(xa, norm_mix[layer], sh1, sc1, w_in, tabs, ev_q_gain[i],
                                                ev_k_gain[i], tm, seq, nb)
            dec = jnp.stack([ev_decay_f[i], ev_decay_b[i]]).astype(F32)
            oc, scf, scb = retention(dec, rq, rk, p, ret_zero, ret_zero, nb, cseq, t_lat // cseq)
            ol, _, _ = retention(dec, rq, rk, p, scf, scb, nb, seq, 0)
            kcat = jnp.concatenate([ak[:, :t_lat].reshape(ATT_KV_HEADS, nb, seq, ATT_HD),
                                    ak[:, t_lat:].reshape(ATT_KV_HEADS, nb, cseq, ATT_HD)], axis=2)
            vcat = jnp.concatenate([av[:, :t_lat].reshape(ATT_KV_HEADS, nb, seq, ATT_HD),
                                    av[:, t_lat:].reshape(ATT_KV_HEADS, nb, cseq, ATT_HD)], axis=2)
            lk = seq + cseq
            vtcat = jnp.concatenate([vcat.transpose(0, 1, 3, 2),
                                     jnp.ones((ATT_KV_HEADS, nb, ATT_VT_ROWS - ATT_HD, lk), BF16)], axis=2)
            vtcat = vtcat.reshape(ATT_KV_HEADS, nb, ATT_VT_ROWS, lk // tk, tk).transpose(0, 1, 3, 2, 4)
            aqt = aq.transpose(0, 2, 1)
            att_l = attention(aqt, kcat, vtcat, seq, 0, 0, tq, tk)
            att_c = attention(aqt, kcat, vtcat, cseq, t_lat // tq, seq // tk, tq, tk)
            w_out = ev_w_out[i].astype(BF16)
            k1 = RET_HEADS * RET_DV
            xa, f, route, counts = outproj_even(ol, oc, att_l, att_c, w_out[:k1], w_out[k1:], xa, g1, route_params,
                                                tm, seq, nb)
        else:
            w_in = jnp.pad(od_w_in[i], ((0, 0), (0, ODD_IN_PAD - ODD_IN))).astype(BF16)
            zpad = jnp.zeros((LANES - 2 * DN_HEADS,), F32)
            arow = jnp.concatenate([od_a_log_f[i], od_a_log_b[i], zpad]).reshape(1, LANES).astype(F32)
            brow = jnp.concatenate([od_dt_bias_f[i], od_dt_bias_b[i], zpad]).reshape(1, LANES).astype(F32)
            q, k, v, p, gb = inproj_odd(xa, norm_mix[layer], sh1, sc1, w_in, od_conv[i].astype(F32), arow, brow,
                                        first_flags, last_flags, tm, seq, nb, cseq)
            ta = t_lat + tc
            gbt = gb.reshape(ta // DN_CHUNK, DN_CHUNK, LANES)[:, :, :4 * DN_HEADS].transpose(0, 2, 1)
            oc_f, oc_b, sc_f, sc_b = deltanet_bidir(q, k, v, gb, gbt, dn_zero, dn_zero, nb, cseq, t_lat, tl)
            ol_f, ol_b, _, _ = deltanet_bidir(q, k, v, gb, gbt, sc_f, sc_b, nb, seq, 0, tl)
            xa, f, route, counts = outproj_odd(ol_f, oc_f, ol_b, oc_b, p, od_out_gain[i], od_w_out[i].astype(BF16),
                                               xa, g1, route_params, tm, seq, nb)

        tile_expert, n_used, n_pad, pos_tiles = moe_slots(route, counts, tm_ffn, tm_comb)
        xs = moe_scatter(pos_tiles, f, jnp.zeros((n_pad, d // 2), U32) if xs is None else xs, tm_comb)
        y_sorted = moe_ffn(tile_expert, n_used, xs, moe_w_gate_up, moe_w_down, layer, tm_ffn)
        last = layer == depth - 1
        xa = moe_combine(pos_tiles, xa, g2, route, y_sorted, final_norm_gain, t_lat if last else t_lat + tc, last,
                         tm_comb, seq, nb)

    return xa.reshape(nb, seq, d)
```

```python
import functools
import math

import numpy as np
import jax
import jax.numpy as jnp
from jax import lax
from jax.experimental import pallas as pl
from jax.experimental.pallas import tpu as pltpu

F32 = jnp.float32
BF16 = jnp.bfloat16
U32 = jnp.uint32
HIGHEST = lax.Precision.HIGHEST

EPS = 1e-6
GRID_W = 64
ROPE_BASE = 10000.0
RET_HEADS, RET_DK, RET_DV, RET_CHUNK = 8, 64, 128, 128
ATT_HEADS, ATT_KV_HEADS, ATT_HD = 8, 2, 64
DN_HEADS, DN_DK, DN_DV, DN_CHUNK, DN_CONV = 8, 128, 128, 64, 3
N_GROUPS, EXPERTS_PER_GROUP, TOP_K = 4, 8, 2
N_EXPERTS = N_GROUPS * EXPERTS_PER_GROUP

EVEN_IN = 2 * RET_HEADS * RET_DK + 2 * RET_HEADS * RET_DV + (ATT_HEADS + 2 * ATT_KV_HEADS) * ATT_HD
EVEN_ATT_COL = 2 * RET_HEADS * RET_DK + 2 * RET_HEADS * RET_DV
DN_QKV = 2 * DN_HEADS * DN_DK + DN_HEADS * DN_DV
ODD_IN = DN_QKV + DN_HEADS * DN_DV + 4 * DN_HEADS
ODD_IN_PAD = ((ODD_IN + 127) // 128) * 128

LANES = 128
VMEM_LIMIT = 56 * 1024 * 1024

NT_DIMS = (((1,), (1,)), ((), ()))
TN_DIMS = (((0,), (0,)), ((), ()))


def _cparams(sem):
    return pltpu.CompilerParams(dimension_semantics=sem, vmem_limit_bytes=VMEM_LIMIT)


def _silu(x):
    return x / (1.0 + jnp.exp(-x))


def _dot(a, b):
    return jnp.dot(a, b, preferred_element_type=F32)


def _adaln_kernel(c_ref, w_ref, b_ref, o_ref):
    s = _silu(c_ref[...])
    o_ref[...] = _dot(s.astype(BF16), w_ref[...].astype(BF16)) + b_ref[...]


def adaln(c8, w_ada, b_ada):
    depth, d, n6 = w_ada.shape
    tn = min(n6, 1536)
    return pl.pallas_call(
        _adaln_kernel,
        grid=(depth, n6 // tn),
        in_specs=[
            pl.BlockSpec((8, d), lambda l, j: (0, 0)),
            pl.BlockSpec((None, d, tn), lambda l, j: (l, 0, j)),
            pl.BlockSpec((None, 1, tn), lambda l, j: (l, 0, j)),
        ],
        out_specs=pl.BlockSpec((None, 8, tn), lambda l, j: (l, 0, j)),
        out_shape=jax.ShapeDtypeStruct((depth, 8, n6), F32),
        compiler_params=_cparams(("arbitrary", "arbitrary")),
        name="adaln",
    )(c8, w_ada, b_ada.reshape(depth, 1, n6))


def _norm_mod(x, gain, shift, scale):
    ms = jnp.mean(x * x, axis=-1, keepdims=True)
    h = x * lax.rsqrt(ms + EPS) * gain
    return h * (1.0 + scale) + shift


def _pack_bf16_pairs(x):
    n = x.shape[1] // 2
    hi = lax.bitcast_convert_type(x[:, :n].astype(BF16).astype(F32), U32)
    lo = lax.bitcast_convert_type(x[:, n:].astype(BF16).astype(F32), U32)
    return hi | (lo >> 16)


def _unpack_bf16_pairs(p):
    hi = lax.bitcast_convert_type(p & jnp.uint32(0xFFFF0000), F32)
    lo = lax.bitcast_convert_type(p << 16, F32)
    return hi, lo


def _mod_row_map(tm, seq, n_lat_batches):
    return lambda i: (jnp.minimum((i * tm) // seq, n_lat_batches), 0, 0)


def _inproj_even_kernel(x_ref, g_ref, sh_ref, sc_ref, w_ref, cos_ref, s1_ref, s2_ref, qg_ref, kg_ref, bd_ref,
                        rq_ref, rk_ref, vg_ref, aq_ref, ak_ref, av_ref):
    hb = _norm_mod(x_ref[...], g_ref[...], sh_ref[...], sc_ref[...]).astype(BF16)
    cos = cos_ref[...]
    s1 = s1_ref[...]
    s2 = s2_ref[...]
    bd = bd_ref[...]
    half = ATT_HD

    def proj(c0, width=LANES):
        return _dot(hb, w_ref[:, c0:c0 + width])

    def rope(x):
        return x * cos + pltpu.roll(x, LANES - 16, 1) * s1 + pltpu.roll(x, 16, 1) * s2

    def head_norm(x, gain):
        sq = x * x
        hi = sq.astype(BF16)
        lo = (sq - hi.astype(F32)).astype(BF16)
        ms = _dot(hi, bd) + _dot(lo, bd)
        return x * lax.rsqrt(ms + EPS) * gain

    wide = 2 * LANES
    qw = RET_HEADS * RET_DK
    for c0 in range(0, qw, wide):
        yq = proj(c0, wide)
        yk = proj(qw + c0, wide)
        for u in range(2):
            cs = slice(c0 + u * LANES, c0 + (u + 1) * LANES)
            us = slice(u * LANES, (u + 1) * LANES)
            rq_ref[:, cs] = rope(yq[:, us]).astype(BF16)
            rk_ref[:, cs] = (rope(yk[:, us]) * RET_DK ** -0.5).astype(BF16)
    vgw = 2 * RET_HEADS * RET_DV
    for c0 in range(0, vgw, 512):
        vg_ref[:, c0:c0 + 512] = proj(2 * qw + c0, 512).astype(BF16)

    qg = qg_ref[...]
    kg = kg_ref[...]
    a0 = EVEN_ATT_COL
    for c0 in range(0, ATT_HEADS * ATT_HD, wide):
        ya = proj(a0 + c0, wide)
        for u in range(2):
            y = rope(head_norm(ya[:, u * LANES:(u + 1) * LANES], qg)) * (ATT_HD ** -0.5 * math.log2(math.e))
            y = y.astype(BF16)
            hd0 = (c0 + u * LANES) // ATT_HD
            aq_ref[hd0] = y[:, :half]
            aq_ref[hd0 + 1] = y[:, half:]
    ykv = proj(a0 + ATT_HEADS * ATT_HD, wide)
    y = rope(head_norm(ykv[:, :LANES], kg)).astype(BF16)
    ak_ref[0] = y[:, :half]
    ak_ref[1] = y[:, half:]
    v = ykv[:, LANES:].astype(BF16)
    av_ref[0] = v[:, :half]
    av_ref[1] = v[:, half:]


def inproj_even(xa, gain, shift, scale, w, tabs, q_gain, k_gain, tm, seq, nb):
    ta, d = xa.shape
    t_lat = nb * seq
    cos_t, s1_t, s2_t = tabs
    n_tab = seq // tm

    def tab_map(i):
        r = i * tm
        return (jnp.where(r < t_lat, (r % seq) // tm, n_tab), 0)

    ii = np.arange(LANES)
    bd = jnp.asarray((ii[:, None] // ATT_HD == ii[None, :] // ATT_HD).astype(np.float32) / ATT_HD).astype(BF16)
    qg = jnp.tile(q_gain.astype(F32), LANES // ATT_HD).reshape(1, LANES)
    kg = jnp.tile(k_gain.astype(F32), LANES // ATT_HD).reshape(1, LANES)
    mrow = _mod_row_map(tm, seq, nb)
    tab_spec = pl.BlockSpec((tm, LANES), tab_map)
    one = lambda i: (0, 0)
    qw = RET_HEADS * RET_DK
    vgw = 2 * RET_HEADS * RET_DV
    return pl.pallas_call(
        _inproj_even_kernel,
        grid=(ta // tm,),
        in_specs=[
            pl.BlockSpec((tm, d), lambda i: (i, 0)),
            pl.BlockSpec((1, d), one),
            pl.BlockSpec((None, 1, d), mrow),
            pl.BlockSpec((None, 1, d), mrow),
            pl.BlockSpec((d, EVEN_IN), one),
            tab_spec, tab_spec, tab_spec,
            pl.BlockSpec((1, LANES), one), pl.BlockSpec((1, LANES), one),
            pl.BlockSpec((LANES, LANES), one),
        ],
        out_specs=[
            pl.BlockSpec((tm, qw), lambda i: (i, 0)),
            pl.BlockSpec((tm, qw), lambda i: (i, 0)),
            pl.BlockSpec((tm, vgw), lambda i: (i, 0)),
            pl.BlockSpec((ATT_HEADS, tm, ATT_HD), lambda i: (0, i, 0)),
            pl.BlockSpec((ATT_KV_HEADS, tm, ATT_HD), lambda i: (0, i, 0)),
            pl.BlockSpec((ATT_KV_HEADS, tm, ATT_HD), lambda i: (0, i, 0)),
        ],
        out_shape=[
            jax.ShapeDtypeStruct((ta, qw), BF16),
            jax.ShapeDtypeStruct((ta, qw), BF16),
            jax.ShapeDtypeStruct((ta, vgw), BF16),
            jax.ShapeDtypeStruct((ATT_HEADS, ta, ATT_HD), BF16),
            jax.ShapeDtypeStruct((ATT_KV_HEADS, ta, ATT_HD), BF16),
            jax.ShapeDtypeStruct((ATT_KV_HEADS, ta, ATT_HD), BF16),
        ],
        compiler_params=_cparams(("arbitrary",)),
        name="inproj_even",
    )(xa, gain.reshape(1, d), shift, scale, w, cos_t, s1_t, s2_t, qg, kg, bd)


def rope_tables(seq, tm):
    nf = ATT_HD // 4
    t = jnp.arange(seq)
    rows = (t // GRID_W).astype(F32)
    cols = (t % GRID_W).astype(F32)
    inv = ROPE_BASE ** (-jnp.arange(nf, dtype=F32) / nf)
    lane = np.arange(LANES)
    axis = (lane % ATT_HD) // (ATT_HD // 2)
    f = lane % nf
    upper = ((lane % (ATT_HD // 2)) >= nf)
    pos = jnp.where(jnp.asarray(axis)[None, :] == 0, rows[:, None], cols[:, None])
    ang = pos * inv[jnp.asarray(f)][None, :]
    cos = jnp.cos(ang)
    sin = jnp.sin(ang)
    s1 = jnp.where(jnp.asarray(upper)[None, :], 0.0, -sin)
    s2 = jnp.where(jnp.asarray(upper)[None, :], sin, 0.0)
    pad1 = jnp.ones((tm, LANES), F32)
    pad0 = jnp.zeros((tm, LANES), F32)
    return (jnp.concatenate([cos, pad1]), jnp.concatenate([s1, pad0]), jnp.concatenate([s2, pad0]))


def _retention_kernel(dec_ref, q_ref, k_ref, v_ref, g_ref, s0f_ref, s0b_ref,
                      o_ref, sff_ref, sfb_ref, st_ref, *, n_chunks, unroll):
    hp = pl.program_id(1)
    C = RET_CHUNK
    dk, dv = RET_DK, RET_DV
    pos = lax.broadcasted_iota(jnp.int32, (C, dk), 0).astype(F32)
    ii = lax.broadcasted_iota(jnp.int32, (C, C), 0)
    jj = lax.broadcasted_iota(jnp.int32, (C, C), 1)
    dpos = (ii - jj).astype(F32)
    heads = range(2)
    qs = [slice(hh * dk, (hh + 1) * dk) for hh in heads]
    vs = [slice(hh * dv, (hh + 1) * dv) for hh in heads]
    w_out, w_in, gcf, gcb, mask = [], [], [], [], []
    for hh in heads:
        h = 2 * hp + hh
        df = dec_ref[0, h]
        db = dec_ref[1, h]
        lf = -jnp.exp(jnp.full((C, C), df, F32))
        lb = -jnp.exp(jnp.full((C, C), db, F32))
        lfk = -jnp.exp(jnp.full((C, dk), df, F32))
        lbk = -jnp.exp(jnp.full((C, dk), db, F32))
        w_out.append(jnp.concatenate([jnp.exp(lfk * (C - 1.0 - pos)), jnp.exp(lbk * pos)], axis=1))
        w_in.append(jnp.concatenate([jnp.exp(lfk * (pos + 1.0)), jnp.exp(lbk * (C - pos))], axis=1))
        gcf.append(jnp.exp(-jnp.exp(jnp.full((dk, dv), df, F32)) * C))
        gcb.append(jnp.exp(-jnp.exp(jnp.full((dk, dv), db, F32)) * C))
        mask.append(jnp.where(dpos > 0, jnp.exp(lf * jnp.maximum(dpos, 0.0)),
                              jnp.where(dpos < 0, jnp.exp(lb * jnp.maximum(-dpos, 0.0)), 2.0)))

    def rows(n):
        return pl.ds(pl.multiple_of(n * C, C), C)

    items = [(u, hh) for u in range(unroll) for hh in heads]

    def sums_body(i, carry):
        kk = {}
        for u, hh in items:
            k = k_ref[rows(i * unroll + u), qs[hh]].astype(F32)
            kk[(u, hh)] = (jnp.concatenate([k, k], axis=1) * w_out[hh]).astype(BF16)
        kv = {(u, hh): lax.dot_general(kk[(u, hh)], v_ref[rows(i * unroll + u), vs[hh]], TN_DIMS,
                                       preferred_element_type=F32) for u, hh in items}
        for u, hh in items:
            st_ref[hh, i * unroll + u] = kv[(u, hh)]
        return carry

    lax.fori_loop(0, n_chunks // unroll, sums_body, 0)

    def scan_body(n, carry):
        n_rev = n_chunks - 1 - n
        out = []
        for hh in heads:
            sf, sb = carry[2 * hh], carry[2 * hh + 1]
            kvf = st_ref[hh, n, 0:dk, :]
            kvb = st_ref[hh, n_rev, dk:2 * dk, :]
            st_ref[hh, n, 0:dk, :] = sf
            st_ref[hh, n_rev, dk:2 * dk, :] = sb
            out += [gcf[hh] * sf + kvf, gcb[hh] * sb + kvb]
        return tuple(out)

    init = tuple(x for hh in heads for x in (s0f_ref[hh], s0b_ref[hh]))
    fin = lax.fori_loop(0, n_chunks, scan_body, init)
    for hh in heads:
        sff_ref[hh] = fin[2 * hh]
        sfb_ref[hh] = fin[2 * hh + 1]

    def out_body(i, carry):
        ns = [i * unroll + u for u in range(unroll)]
        qb = {(u, hh): q_ref[rows(ns[u]), qs[hh]] for u, hh in items}
        sc = {(u, hh): lax.dot_general(qb[(u, hh)], k_ref[rows(ns[u]), qs[hh]], NT_DIMS,
                                       preferred_element_type=F32) for u, hh in items}
        qw = {}
        for it in items:
            q = qb[it].astype(F32)
            qw[it] = (jnp.concatenate([q, q], axis=1) * w_in[it[1]]).astype(BF16)
        o1 = {(u, hh): _dot((sc[(u, hh)] * mask[hh]).astype(BF16), v_ref[rows(ns[u]), vs[hh]]) for u, hh in items}
        o2 = {(u, hh): _dot(qw[(u, hh)], st_ref[hh, ns[u]].astype(BF16)) for u, hh in items}
        for it in items:
            u, hh = it
            n = ns[u]
            o = o1[it] + o2[it]
            o = o * lax.rsqrt(jnp.mean(o * o, axis=-1, keepdims=True) + EPS)
            gate = g_ref[rows(n), vs[hh]].astype(F32)
            o_ref[rows(n), vs[hh]] = (_silu(gate) * o).astype(o_ref.dtype)
        return carry

    lax.fori_loop(0, n_chunks // unroll, out_body, 0)


def retention(dec, rq, rk, p, s0f, s0b, nb, seq, row_off_blocks):
    n_chunks = seq // RET_CHUNK
    hp_n = RET_HEADS // 2
    vcol = 0
    gcol = vcol + RET_HEADS * RET_DV // (2 * RET_DV)
    ta = rq.shape[0]
    st_spec = pl.BlockSpec((None, 2, RET_DK, RET_DV), lambda b, hp, *_: (b, hp, 0, 0))
    grid_spec = pltpu.PrefetchScalarGridSpec(
        num_scalar_prefetch=1,
        grid=(nb, hp_n),
        in_specs=[
            pl.BlockSpec((seq, 2 * RET_DK), lambda b, hp, *_: (row_off_blocks + b, hp)),
            pl.BlockSpec((seq, 2 * RET_DK), lambda b, hp, *_: (row_off_blocks + b, hp)),
            pl.BlockSpec((seq, 2 * RET_DV), lambda b, hp, *_: (row_off_blocks + b, vcol + hp)),
            pl.BlockSpec((seq, 2 * RET_DV), lambda b, hp, *_: (row_off_blocks + b, gcol + hp)),
            st_spec, st_spec,
        ],
        out_specs=[
            pl.BlockSpec((seq, 2 * RET_DV), lambda b, hp, *_: (b, hp)),
            st_spec, st_spec,
        ],
        scratch_shapes=[pltpu.VMEM((2, n_chunks, 2 * RET_DK, RET_DV), F32)],
    )
    st_shape = jax.ShapeDtypeStruct((nb, RET_HEADS, RET_DK, RET_DV), F32)
    return pl.pallas_call(
        functools.partial(_retention_kernel, n_chunks=n_chunks, unroll=math.gcd(n_chunks, 8)),
        grid_spec=grid_spec,
        out_shape=[jax.ShapeDtypeStruct((nb * seq, RET_HEADS * RET_DV), BF16), st_shape, st_shape],
        compiler_params=_cparams(("arbitrary", "arbitrary")),
        name="retention",
    )(dec, rq, rk, p, p, s0f, s0b)


ATT_VT_ROWS = ATT_HD + 16


ATT_PAIRS_PER_TRIP = 8


def _attn_kernel(q_ref, k_ref, vt_ref, o_ref, *s_refs, tk, c_start, c_end, rep):
    tq = q_ref.shape[2]
    sets = (s_refs[:rep], s_refs[rep:])
    last = c_end - 1

    def scores(bufs, j):
        j = jnp.minimum(j, last)
        c0 = pl.multiple_of(j * tk, tk)
        k = k_ref[pl.ds(c0, tk), :]
        mxs = []
        for r in range(rep):
            s = _dot(k, q_ref[r])
            bufs[r][...] = s
            mxs.append(jnp.max(s, axis=0, keepdims=True))
        return tuple(mxs)

    def softmax_pv(bufs, j, mxs, ms, accs):
        vt = vt_ref[j]
        new_m, new_acc = [], []
        for r in range(rep):
            m_new = jnp.maximum(ms[r], mxs[r])
            a = jnp.exp2(ms[r] - m_new)
            p = jnp.exp2(bufs[r][...] - m_new).astype(BF16)
            new_acc.append(a * accs[r] + _dot(vt, p))
            new_m.append(m_new)
        return tuple(new_m), tuple(new_acc)

    def pair(j, mx0, ms, accs):
        mx1 = scores(sets[1], j + 1)
        ms, accs = softmax_pv(sets[0], j, mx0, ms, accs)
        mx0 = scores(sets[0], j + 2)
        ms, accs = softmax_pv(sets[1], j + 1, mx1, ms, accs)
        return mx0, ms, accs

    def trip(t, carry):
        for u in range(ATT_PAIRS_PER_TRIP):
            carry = pair(c_start + 2 * (ATT_PAIRS_PER_TRIP * t + u), *carry)
        return carry

    n_pairs = (c_end - c_start) // 2
    n_trips = n_pairs // ATT_PAIRS_PER_TRIP
    ms = tuple(jnp.full((1, tq), -1e30, F32) for _ in range(rep))
    accs = tuple(jnp.zeros((ATT_VT_ROWS, tq), F32) for _ in range(rep))
    carry = (scores(sets[0], c_start), ms, accs)
    if n_trips:
        carry = lax.fori_loop(0, n_trips, trip, carry)
    for u in range(n_trips * ATT_PAIRS_PER_TRIP, n_pairs):
        carry = pair(c_start + 2 * u, *carry)
    mx0, ms, accs = carry
    if (c_end - c_start) % 2:
        ms, accs = softmax_pv(sets[0], last, mx0, ms, accs)
    outs = [(acc[:ATT_HD, :] / acc[ATT_HD:ATT_HD + 1, :]).T for acc in accs]
    o_ref[...] = jnp.concatenate(outs, axis=-1).astype(o_ref.dtype)


def attention(aq, kcat, vtcat, seq_q, q_off_blocks, c_start, tq, tk):
    rep = ATT_HEADS // ATT_KV_HEADS
    _, nb, lk, _ = kcat.shape
    nq = seq_q // tq
    n_chunks = lk // tk
    return pl.pallas_call(
        functools.partial(_attn_kernel, tk=tk, c_start=c_start, c_end=n_chunks, rep=rep),
        grid=(nb, ATT_KV_HEADS, nq),
        in_specs=[
            pl.BlockSpec((rep, ATT_HD, tq), lambda b, g, i: (g, 0, q_off_blocks + b * nq + i)),
            pl.BlockSpec((None, None, lk, ATT_HD), lambda b, g, i: (g, b, 0, 0)),
            pl.BlockSpec((None, None, n_chunks, ATT_VT_ROWS, tk), lambda b, g, i: (g, b, 0, 0, 0)),
        ],
        out_specs=pl.BlockSpec((tq, rep * ATT_HD), lambda b, g, i: (b * nq + i, g)),
        out_shape=jax.ShapeDtypeStruct((nb * seq_q, ATT_HEADS * ATT_HD), BF16),
        scratch_shapes=[pltpu.VMEM((tk, tq), F32) for _ in range(2 * rep)],
        compiler_params=_cparams(("arbitrary", "arbitrary", "arbitrary")),
        name="attention",
    )(aq, kcat, vtcat)


def _lat_ctx_specs(tm, width, n_lat_tiles):
    return [pl.BlockSpec((tm, width), lambda i: (jnp.minimum(i, n_lat_tiles - 1), 0)),
            pl.BlockSpec((tm, width), lambda i: (jnp.maximum(i - n_lat_tiles, 0), 0))]


def _outproj_even_kernel(r_lat, r_ctx, a_lat, a_ctx, w1_ref, w2_ref, res_ref, gate_ref, *rest, n_lat_tiles):
    route_in, (o_ref, *route_out) = rest[:6], rest[6:]
    is_lat = pl.program_id(0) < n_lat_tiles
    a1 = jnp.where(is_lat, r_lat[...], r_ctx[...])
    a2 = jnp.where(is_lat, a_lat[...], a_ctx[...])
    y = _dot(a1, w1_ref[...]) + _dot(a2, w2_ref[...])
    out = res_ref[...] + gate_ref[...] * y
    o_ref[...] = out
    _route_tile(out, *route_in, *route_out)


def outproj_even(ret_lat, ret_ctx, att_lat, att_ctx, w1, w2, xa, gate, route_params, tm, seq, nb):
    ta, d = xa.shape
    k1, k2 = w1.shape[0], w2.shape[0]
    n_lat_tiles = ret_lat.shape[0] // tm
    r_in, r_args, r_out, r_shape, r_scratch = _route_io(ta, d, tm, seq, nb, *route_params)
    return pl.pallas_call(
        functools.partial(_outproj_even_kernel, n_lat_tiles=n_lat_tiles),
        grid=(ta // tm,),
        in_specs=_lat_ctx_specs(tm, k1, n_lat_tiles) + _lat_ctx_specs(tm, k2, n_lat_tiles) + [
            pl.BlockSpec((k1, d), lambda i: (0, 0)),
            pl.BlockSpec((k2, d), lambda i: (0, 0)),
            pl.BlockSpec((tm, d), lambda i: (i, 0)),
            pl.BlockSpec((None, 1, d), _mod_row_map(tm, seq, nb)),
        ] + r_in,
        out_specs=[pl.BlockSpec((tm, d), lambda i: (i, 0))] + r_out,
        out_shape=[jax.ShapeDtypeStruct((ta, d), F32)] + r_shape,
        scratch_shapes=r_scratch,
        compiler_params=_cparams(("arbitrary",)),
        name="outproj_even",
    )(ret_lat, ret_ctx, att_lat, att_ctx, w1, w2, xa, gate, *r_args)


def _outproj_odd_kernel(f_lat, f_ctx, b_lat, b_ctx, z_ref, og_ref, w_ref, res_ref, gate_ref, *rest, n_lat_tiles):
    route_in, (o_ref, *route_out) = rest[:6], rest[6:]
    is_lat = pl.program_id(0) < n_lat_tiles
    og = og_ref[...]
    parts = []
    for h in range(DN_HEADS):
        cs = slice(h * DN_DV, (h + 1) * DN_DV)
        of = jnp.where(is_lat, f_lat[:, cs], f_ctx[:, cs]).astype(F32)
        ob = jnp.where(is_lat, b_lat[:, cs], b_ctx[:, cs]).astype(F32)
        o = of + ob
        o = o * lax.rsqrt(jnp.mean(o * o, axis=-1, keepdims=True) + EPS) * og
        parts.append((o * _silu(z_ref[:, cs].astype(F32))).astype(BF16))
    a = jnp.concatenate(parts, axis=-1)
    out = res_ref[...] + gate_ref[...] * _dot(a, w_ref[...])
    o_ref[...] = out
    _route_tile(out, *route_in, *route_out)


def outproj_odd(of_lat, of_ctx, ob_lat, ob_ctx, p, out_gain, w, xa, gate, route_params, tm, seq, nb):
    ta, d = xa.shape
    kdim = DN_HEADS * DN_DV
    n_lat_tiles = of_lat.shape[0] // tm
    r_in, r_args, r_out, r_shape, r_scratch = _route_io(ta, d, tm, seq, nb, *route_params)
    return pl.pallas_call(
        functools.partial(_outproj_odd_kernel, n_lat_tiles=n_lat_tiles),
        grid=(ta // tm,),
        in_specs=_lat_ctx_specs(tm, kdim, n_lat_tiles) + _lat_ctx_specs(tm, kdim, n_lat_tiles) + [
            pl.BlockSpec((tm, kdim), lambda i: (i, 0)),
            pl.BlockSpec((1, DN_DV), lambda i: (0, 0)),
            pl.BlockSpec((kdim, d), lambda i: (0, 0)),
            pl.BlockSpec((tm, d), lambda i: (i, 0)),
            pl.BlockSpec((None, 1, d), _mod_row_map(tm, seq, nb)),
        ] + r_in,
        out_specs=[pl.BlockSpec((tm, d), lambda i: (i, 0))] + r_out,
        out_shape=[jax.ShapeDtypeStruct((ta, d), F32)] + r_shape,
        scratch_shapes=r_scratch,
        compiler_params=_cparams(("arbitrary",)),
        name="outproj_odd",
    )(of_lat, of_ctx, ob_lat, ob_ctx, p, out_gain.reshape(1, DN_DV).astype(F32), w, xa, gate, *r_args)


def _inproj_odd_kernel(first_ref, last_ref, x_ref, xp_ref, xn_ref, g_ref, sh_ref, sc_ref, w_ref, cw_ref,
                       arow_ref, brow_ref, q_ref, k_ref, v_ref, z_ref, gb_ref, *, ctx_tile0, cseq):
    i = pl.program_id(0)
    tm = x_ref.shape[0]
    gain, shift, scale = g_ref[...], sh_ref[...], sc_ref[...]
    hrows = xp_ref.shape[0]
    hall = _norm_mod(jnp.concatenate([x_ref[...], xp_ref[...], xn_ref[...]], axis=0), gain, shift, scale).astype(BF16)
    hb = hall[:tm]
    keep_prev = 1.0 - first_ref[i].astype(F32)
    keep_next = 1.0 - last_ref[i].astype(F32)
    row = lax.broadcasted_iota(jnp.int32, (tm, LANES), 0)
    is_first = row == 0
    is_last = row == tm - 1
    inner = cseq < tm
    if inner:
        in_ctx = i >= ctx_tile0
        local = row & (cseq - 1)
        zero_dn = jnp.logical_and(in_ctx, local == 0)
        zero_up = jnp.logical_and(in_ctx, local == cseq - 1)
    n_qk = 2 * DN_HEADS * DN_DK // LANES
    n_q = DN_HEADS * DN_DK // LANES
    outs = (q_ref, k_ref, v_ref)
    wide = 2 * LANES
    for c0 in range(0, DN_QKV, wide):
        yall = _dot(hall, w_ref[:, c0:c0 + wide])
        y2 = yall[:tm]
        yh2 = yall[tm:]
        for u in range(2):
            j = c0 // LANES + u
            us = slice(u * LANES, (u + 1) * LANES)
            x = y2[:, us]
            xp = yh2[hrows - 1:hrows, us] * keep_prev
            xn = yh2[hrows:hrows + 1, us] * keep_next
            x_dn = jnp.where(is_first, xp, pltpu.roll(x, 1, 0))
            x_up = jnp.where(is_last, xn, pltpu.roll(x, tm - 1, 0))
            if inner:
                x_dn = jnp.where(zero_dn, 0.0, x_dn)
                x_up = jnp.where(zero_up, 0.0, x_up)
            w = cw_ref[:, j * LANES:(j + 1) * LANES]
            y = _silu(x_dn * w[0:1, :] + x * w[1:2, :] + x_up * w[2:3, :])
            if j < n_qk:
                y = y * lax.rsqrt(jnp.sum(y * y, axis=-1, keepdims=True) + EPS)
                if j < n_q:
                    y = y * DN_DK ** -0.5
            lj = j % n_q
            outs[j // n_q][:, lj * LANES:(lj + 1) * LANES] = y.astype(BF16)
    zw = DN_HEADS * DN_DV
    for c0 in range(0, zw, 512):
        z_ref[:, c0:c0 + 512] = _dot(hb, w_ref[:, DN_QKV + c0:DN_QKV + c0 + 512]).astype(BF16)

    a = _dot(hb, w_ref[:, DN_QKV + zw:DN_QKV + zw + LANES])
    lane = lax.broadcasted_iota(jnp.int32, (tm, LANES), 1)
    zz = a + brow_ref[...]
    softplus = jnp.maximum(zz, 0.0) + jnp.log(1.0 + jnp.exp(-jnp.abs(zz)))
    g = -jnp.exp(arow_ref[...]) * softplus
    beta = 1.0 / (1.0 + jnp.exp(-a))
    gb_ref[...] = jnp.where(lane < 2 * DN_HEADS, g, jnp.where(lane < 4 * DN_HEADS, beta, 0.0))


def inproj_odd(xa, gain, shift, scale, w, conv_w, arow, brow, first_flags, last_flags, tm, seq, nb, cseq):
    ta, d = xa.shape
    halo = 8
    hb = tm // halo
    n_h = ta // halo
    kdim = DN_HEADS * DN_DK
    assert cseq >= tm or (tm % cseq == 0 and cseq & (cseq - 1) == 0)
    mrow = lambda i, *_: _mod_row_map(tm, seq, nb)(i)
    one = lambda i, *_: (0, 0)
    row_blk = lambda i, *_: (i, 0)
    grid_spec = pltpu.PrefetchScalarGridSpec(
        num_scalar_prefetch=2,
        grid=(ta // tm,),
        in_specs=[
            pl.BlockSpec((tm, d), row_blk),
            pl.BlockSpec((halo, d), lambda i, *_: (jnp.maximum(i * hb - 1, 0), 0)),
            pl.BlockSpec((halo, d), lambda i, *_: (jnp.minimum((i + 1) * hb, n_h - 1), 0)),
            pl.BlockSpec((1, d), one),
            pl.BlockSpec((None, 1, d), mrow),
            pl.BlockSpec((None, 1, d), mrow),
            pl.BlockSpec((d, ODD_IN_PAD), one),
            pl.BlockSpec((DN_CONV, DN_QKV), one),
            pl.BlockSpec((1, LANES), one),
            pl.BlockSpec((1, LANES), one),
        ],
        out_specs=[pl.BlockSpec((tm, kdim), row_blk)] * 4 + [pl.BlockSpec((tm, LANES), row_blk)],
    )
    return pl.pallas_call(
        functools.partial(_inproj_odd_kernel, ctx_tile0=nb * seq // tm, cseq=cseq),
        grid_spec=grid_spec,
        out_shape=[jax.ShapeDtypeStruct((ta, kdim), BF16)] * 4 + [jax.ShapeDtypeStruct((ta, LANES), F32)],
        compiler_params=_cparams(("arbitrary",)),
        name="inproj_odd",
    )(first_flags, last_flags, xa, xa, xa, gain.reshape(1, d), shift, scale, w, conv_w, arow, brow)


def _deltanet_bidir_kernel(qf_ref, kf_ref, vf_ref, gbf_ref, gbtf_ref, qb_ref, kb_ref, vb_ref, gbb_ref, gbtb_ref,
                           s0f_ref, s0b_ref, of_ref, ob_ref, sff_ref, sfb_ref, sf_scr, sb_scr, *, n_chunks):
    t = pl.program_id(1)

    @pl.when(t == 0)
    def _():
        sf_scr[...] = s0f_ref[...]
        sb_scr[...] = s0b_ref[...]

    C = DN_CHUNK
    ii = lax.broadcasted_iota(jnp.int32, (C, C), 0)
    jj = lax.broadcasted_iota(jnp.int32, (C, C), 1)
    lower, upper = ii >= jj, ii <= jj
    eye = jnp.where(ii == jj, 1.0, 0.0).astype(F32)
    blk = ii ^ jj
    dirs = (
        dict(rev=False, incl=lower, strict=ii > jj, q=qf_ref, k=kf_ref, v=vf_ref, gb=gbf_ref, gbt=gbtf_ref,
             o=of_ref, scr=sf_scr, off=0, order=list(range(n_chunks))),
        dict(rev=True, incl=upper, strict=ii < jj, q=qb_ref, k=kb_ref, v=vb_ref, gb=gbb_ref, gbt=gbtb_ref,
             o=ob_ref, scr=sb_scr, off=DN_HEADS, order=list(range(n_chunks - 1, -1, -1))),
    )
    items = [(d, c, h) for d in range(2) for c in dirs[d]["order"] for h in range(DN_HEADS)]

    gcols, grows, gbs = {}, {}, {}
    for d, dr in enumerate(dirs):
        tri = jnp.where(dr["incl"], 1.0, 0.0).astype(F32)
        tri_t = jnp.where(upper if not dr["rev"] else lower, 1.0, 0.0).astype(F32)
        for c in dr["order"]:
            gb_c = dr["gb"][c * C:(c + 1) * C, :]
            gbs[(d, c)] = gb_c
            gcols[(d, c)] = jnp.dot(tri, gb_c, preferred_element_type=F32, precision=HIGHEST)
            grows[(d, c)] = jnp.dot(dr["gbt"][c], tri_t, preferred_element_type=F32, precision=HIGHEST)

    qb, kb16, decay, kbeta, egc, kd, gl, rhs = {}, {}, {}, {}, {}, {}, {}, {}
    for it in items:
        d, c, h = it
        dr = dirs[d]
        gi = dr["off"] + h
        bi = 2 * DN_HEADS + dr["off"] + h
        rows = slice(c * C, (c + 1) * C)
        cs = slice(h * DN_DK, (h + 1) * DN_DK)
        gc = gcols[(d, c)][:, gi:gi + 1]
        gr = grows[(d, c)][gi:gi + 1, :]
        beta = gbs[(d, c)][:, bi:bi + 1]
        qb[it] = dr["q"][rows, cs]
        kb16[it] = dr["k"][rows, cs]
        kf = kb16[it].astype(F32)
        decay[it] = jnp.where(dr["incl"], jnp.exp(jnp.where(dr["incl"], gc - gr, 0.0)), 0.0)
        kbeta[it] = kf * beta
        egc[it] = jnp.exp(gc)
        glast = gc[0:1, :] if dr["rev"] else gc[C - 1:C, :]
        kd[it] = (kf * jnp.exp(glast - gc)).astype(BF16)
        gl[it] = jnp.exp(glast)
        rhs[it] = jnp.concatenate([dr["v"][rows, cs].astype(F32) * beta, kbeta[it] * egc[it]], axis=1).astype(BF16)

    kk = {it: lax.dot_general(kbeta[it].astype(BF16), kb16[it], NT_DIMS, preferred_element_type=F32)
          for it in items}
    qk = {it: lax.dot_general(qb[it], kb16[it], NT_DIMS, preferred_element_type=F32) for it in items}
    lm = {it: jnp.where(dirs[it[0]]["strict"], kk[it] * decay[it], 0.0) for it in items}
    attn = {it: jnp.where(dirs[it[0]]["incl"], qk[it] * decay[it], 0.0).astype(BF16) for it in items}
    dinv = {it: eye - jnp.where(blk < 2, lm[it], 0.0) for it in items}
    s = 2
    while s < C:
        in_band = jnp.logical_and(blk >= s, blk < 2 * s)
        tmp = {it: _dot(dinv[it].astype(BF16), jnp.where(in_band, lm[it], 0.0).astype(BF16)) for it in items}
        dinv = {it: dinv[it] - _dot(tmp[it].astype(BF16), dinv[it].astype(BF16)) for it in items}
        s *= 2
    uw = {it: _dot(dinv[it].astype(BF16), rhs[it]) for it in items}
    wq = {it: jnp.concatenate([uw[it][:, DN_DV:], qb[it].astype(F32) * egc[it]], axis=0).astype(BF16)
          for it in items}

    states = {(d, h): dirs[d]["scr"][h] for d in range(2) for h in range(DN_HEADS)}
    for step in range(n_chunks):
        its = [(d, dirs[d]["order"][step], h) for d in range(2) for h in range(DN_HEADS)]
        r = {it: _dot(wq[it], states[(it[0], it[2])].astype(BF16)) for it in its}
        v_new = {it: (uw[it][:, :DN_DV] - r[it][:C]).astype(BF16) for it in its}
        o = {it: r[it][C:] + _dot(attn[it], v_new[it]) for it in its}
        for it in its:
            key = (it[0], it[2])
            states[key] = states[key] * gl[it] + lax.dot_general(kd[it], v_new[it], TN_DIMS,
                                                                 preferred_element_type=F32)
        for it in its:
            d, c, h = it
            dirs[d]["o"][c * C:(c + 1) * C, h * DN_DK:(h + 1) * DN_DK] = o[it].astype(of_ref.dtype)
    for (d, h), st in states.items():
        dirs[d]["scr"][h] = st

    @pl.when(t == pl.num_programs(1) - 1)
    def _():
        sff_ref[...] = sf_scr[...]
        sfb_ref[...] = sb_scr[...]


def deltanet_bidir(q, k, v, gb, gbt, s0f, s0b, nb, seq, row_off, tl):
    nblk = seq // tl
    n_chunks = tl // DN_CHUNK
    off_b = row_off // tl
    kdim = DN_HEADS * DN_DK

    def fwd_rb(b, t):
        return off_b + b * nblk + t

    def bwd_rb(b, t):
        return off_b + b * nblk + (nblk - 1 - t)

    def seq_specs(rb):
        spec = pl.BlockSpec((tl, kdim), lambda b, t: (rb(b, t), 0))
        return [spec, spec, spec,
                pl.BlockSpec((tl, LANES), lambda b, t: (rb(b, t), 0)),
                pl.BlockSpec((n_chunks, 4 * DN_HEADS, DN_CHUNK), lambda b, t: (rb(b, t), 0, 0))]

    st_spec = pl.BlockSpec((None, DN_HEADS, DN_DK, DN_DV), lambda b, t: (b, 0, 0, 0))
    st_shape = jax.ShapeDtypeStruct((nb, DN_HEADS, DN_DK, DN_DV), F32)
    o_shape = jax.ShapeDtypeStruct((nb * seq, kdim), BF16)
    return pl.pallas_call(
        functools.partial(_deltanet_bidir_kernel, n_chunks=n_chunks),
        grid=(nb, nblk),
        in_specs=seq_specs(fwd_rb) + seq_specs(bwd_rb) + [st_spec, st_spec],
        out_specs=[
            pl.BlockSpec((tl, kdim), lambda b, t: (b * nblk + t, 0)),
            pl.BlockSpec((tl, kdim), lambda b, t: (b * nblk + nblk - 1 - t, 0)),
            st_spec, st_spec,
        ],
        out_shape=[o_shape, o_shape, st_shape, st_shape],
        scratch_shapes=[pltpu.VMEM((DN_HEADS, DN_DK, DN_DV), F32), pltpu.VMEM((DN_HEADS, DN_DK, DN_DV), F32)],
        compiler_params=_cparams(("arbitrary", "arbitrary")),
        name="deltanet_bidir",
    )(q, k, v, gb, gbt, q, k, v, gb, gbt, s0f, s0b)


def _route_tile(x, g_ref, sh_ref, sc_ref, wr_ref, br_ref, ltri_ref, f_ref, r_ref, cnt_ref, base_ref):
    @pl.when(pl.program_id(0) == 0)
    def _():
        base_ref[...] = jnp.zeros_like(base_ref)

    h = _norm_mod(x, g_ref[...], sh_ref[...], sc_ref[...])
    f_ref[...] = _pack_bf16_pairs(h)
    logits = _dot(h.astype(BF16), wr_ref[...]) + br_ref[...]
    tm = logits.shape[0]
    lane = lax.broadcasted_iota(jnp.int32, (tm, LANES), 1)
    neg = -1e30
    big = 4 * LANES
    is_g = lane < N_GROUPS
    gl = jnp.where(is_g, logits, neg)
    gm = jnp.max(gl, axis=-1, keepdims=True)
    grp = jnp.min(jnp.where(gl == gm, lane, big), axis=-1, keepdims=True)
    psum = jnp.sum(jnp.where(is_g, jnp.exp(gl - gm), 0.0), axis=-1, keepdims=True)
    p_grp = 1.0 / psum
    e_lane = lane - N_GROUPS
    in_grp = jnp.logical_and(jnp.logical_and(e_lane >= 0, e_lane < N_EXPERTS),
                             (e_lane // EXPERTS_PER_GROUP) == grp)
    el = jnp.where(in_grp, logits, neg)
    m1 = jnp.max(el, axis=-1, keepdims=True)
    i1 = jnp.min(jnp.where(el == m1, lane, big), axis=-1, keepdims=True)
    el2 = jnp.where(lane == i1, neg, el)
    m2 = jnp.max(el2, axis=-1, keepdims=True)
    i2 = jnp.min(jnp.where(el2 == m2, lane, big), axis=-1, keepdims=True)
    e21 = jnp.exp(m2 - m1)
    w1 = p_grp / (1.0 + e21)
    w2 = p_grp * e21 / (1.0 + e21)
    e1 = (i1 - N_GROUPS).astype(F32)
    e2 = (i2 - N_GROUPS).astype(F32)
    oh1 = lane == i1
    oh2 = lane == i2
    oh1f = jnp.where(oh1, 1.0, 0.0)
    oh2f = jnp.where(oh2, 1.0, 0.0)
    ltri = ltri_ref[...]
    before1 = _dot(ltri, oh1f.astype(BF16))
    before2 = _dot(ltri, oh2f.astype(BF16))
    cnt1 = jnp.sum(oh1f, axis=0, keepdims=True)
    cnt2 = jnp.sum(oh2f, axis=0, keepdims=True)
    base = base_ref[0:1, :]
    rank1 = jnp.sum(jnp.where(oh1, base + before1, 0.0), axis=-1, keepdims=True)
    rank2 = jnp.sum(jnp.where(oh2, base + cnt1 + before2, 0.0), axis=-1, keepdims=True)
    total = base + cnt1 + cnt2
    base_ref[...] = jnp.broadcast_to(total, base_ref.shape)
    cnt_ref[...] = jnp.broadcast_to(total, cnt_ref.shape)
    vals = (e1, e2, w1, w2, rank1, rank2)
    out = jnp.zeros((tm, LANES), F32)
    for idx, val in enumerate(vals):
        out = jnp.where(lane == idx, val, out)
    r_ref[...] = out


def _route_io(ta, d, tm, seq, nb, gain, shift, scale, w_router, b_router):
    mrow = _mod_row_map(tm, seq, nb)
    ii = np.arange(tm)
    ltri = jnp.asarray((ii[:, None] > ii[None, :]).astype(np.float32)).astype(BF16)
    in_specs = [
        pl.BlockSpec((1, d), lambda i: (0, 0)),
        pl.BlockSpec((None, 1, d), mrow),
        pl.BlockSpec((None, 1, d), mrow),
        pl.BlockSpec((d, LANES), lambda i: (0, 0)),
        pl.BlockSpec((1, LANES), lambda i: (0, 0)),
        pl.BlockSpec((tm, tm), lambda i: (0, 0)),
    ]
    args = [gain.reshape(1, d), shift, scale, w_router, b_router, ltri]
    out_specs = [pl.BlockSpec((tm, d // 2), lambda i: (i, 0)), pl.BlockSpec((tm, LANES), lambda i: (i, 0)),
                 pl.BlockSpec((8, LANES), lambda i: (0, 0))]
    out_shape = [jax.ShapeDtypeStruct((ta, d // 2), U32), jax.ShapeDtypeStruct((ta, LANES), F32),
                 jax.ShapeDtypeStruct((8, LANES), F32)]
    return in_specs, args, out_specs, out_shape, [pltpu.VMEM((8, LANES), F32)]


ROW_DMA_UNROLL = 64


def _issue_row_copies(n_rows, make_copy):
    def trip(i, carry):
        for u in range(ROW_DMA_UNROLL):
            make_copy(i * ROW_DMA_UNROLL + u).start(priority=u % 2)
        return carry

    lax.fori_loop(0, n_rows // ROW_DMA_UNROLL, trip, 0)


def _moe_scatter_kernel(pos_ref, f_ref, xs_in, xs_out, sem):
    del xs_in
    tm = f_ref.shape[0]
    for k in range(TOP_K):
        _issue_row_copies(tm, lambda r, k=k: pltpu.make_async_copy(
            f_ref.at[pl.ds(r, 1)], xs_out.at[pl.ds(pos_ref[0, 0, k * tm + r], 1)], sem))
    for _ in range(2):
        pltpu.make_async_copy(f_ref, xs_out.at[pl.ds(0, tm)], sem).wait()


def moe_scatter(pos_tiles, f, xs_zero, tm):
    ta, d = f.shape
    return pl.pallas_call(
        _moe_scatter_kernel,
        grid=(ta // tm,),
        in_specs=[
            pl.BlockSpec((1, 1, 2 * tm), lambda i: (i, 0, 0), memory_space=pltpu.SMEM),
            pl.BlockSpec((tm, d), lambda i: (i, 0)),
            pl.BlockSpec(memory_space=pl.ANY),
        ],
        out_specs=pl.BlockSpec(memory_space=pl.ANY),
        out_shape=jax.ShapeDtypeStruct(xs_zero.shape, xs_zero.dtype),
        scratch_shapes=[pltpu.SemaphoreType.DMA(())],
        input_output_aliases={2: 0},
        compiler_params=_cparams(("arbitrary",)),
        name="moe_scatter",
    )(pos_tiles, f, xs_zero)


def _moe_ffn_kernel(te_ref, nu_ref, x_ref, wgu_ref, wd_ref, o_ref, wgu_bf, wd_bf):
    i = pl.program_id(0)
    fdim = wd_bf.shape[0]

    @pl.when(i < nu_ref[0])
    def _():
        prev = te_ref[jnp.maximum(i - 1, 0)]
        changed = jnp.logical_or(i == 0, te_ref[i] != prev)

        @pl.when(changed)
        def _():
            wgu_bf[...] = wgu_ref[...].astype(BF16)
            wd_bf[...] = wd_ref[...].astype(BF16)

        x_hi, x_lo = _unpack_bf16_pairs(x_ref[...])
        half = x_hi.shape[1]
        gu = _dot(x_hi.astype(BF16), wgu_bf[:half, :]) + _dot(x_lo.astype(BF16), wgu_bf[half:, :])
        hmid = _silu(gu[:, :fdim]) * gu[:, fdim:]
        o_ref[...] = _pack_bf16_pairs(_dot(hmid.astype(BF16), wd_bf[...]))

    @pl.when(i >= nu_ref[0])
    def _():
        o_ref[...] = jnp.zeros_like(o_ref)


def moe_ffn(tile_expert, n_used, xs, w_gate_up, w_down, layer, tm):
    n_pad, dh = xs.shape
    d = 2 * dh
    f2 = w_gate_up.shape[-1]
    fdim = w_down.shape[-2]
    grid_spec = pltpu.PrefetchScalarGridSpec(
        num_scalar_prefetch=2,
        grid=(n_pad // tm,),
        in_specs=[
            pl.BlockSpec((tm, dh), lambda i, te, nu: (i, 0)),
            pl.BlockSpec((None, None, d, f2), lambda i, te, nu: (layer, te[i], 0, 0)),
            pl.BlockSpec((None, None, fdim, d), lambda i, te, nu: (layer, te[i], 0, 0)),
        ],
        out_specs=pl.BlockSpec((tm, dh), lambda i, te, nu: (i, 0)),
        scratch_shapes=[pltpu.VMEM((d, f2), BF16), pltpu.VMEM((fdim, d), BF16)],
    )
    return pl.pallas_call(
        _moe_ffn_kernel,
        grid_spec=grid_spec,
        out_shape=jax.ShapeDtypeStruct((n_pad, dh), U32),
        compiler_params=_cparams(("arbitrary",)),
        name="moe_ffn",
    )(tile_expert, n_used, xs, w_gate_up, w_down)


def _moe_combine_kernel(pos_ref, posn_ref, x_ref, gate_ref, r_ref, fg_ref, y_hbm, o_ref, ybuf, sem, *, final):
    tm = x_ref.shape[0]
    i = pl.program_id(0)
    n = pl.num_programs(0)

    def gather(p_ref, slot):
        _issue_row_copies(2 * tm, lambda r: pltpu.make_async_copy(
            y_hbm.at[pl.ds(p_ref[0, 0, r], 1)], ybuf.at[slot, pl.ds(r, 1)], sem.at[slot]))

    @pl.when(i == 0)
    def _():
        gather(pos_ref, 0)

    @pl.when(i + 1 < n)
    def _():
        gather(posn_ref, (i + 1) % 2)

    slot = i % 2
    pltpu.make_async_copy(y_hbm.at[pl.ds(0, 2 * tm)], ybuf.at[slot], sem.at[slot]).wait()
    route = r_ref[...]
    w0, w1 = route[:, 2:3], route[:, 3:4]
    y0_hi, y0_lo = _unpack_bf16_pairs(ybuf[slot, 0:tm, :])
    y1_hi, y1_lo = _unpack_bf16_pairs(ybuf[slot, tm:2 * tm, :])
    y = jnp.concatenate([w0 * y0_hi + w1 * y1_hi, w0 * y0_lo + w1 * y1_lo], axis=-1)
    out = x_ref[...] + gate_ref[...] * y
    if final:
        out = out * lax.rsqrt(jnp.mean(out * out, axis=-1, keepdims=True) + EPS) * fg_ref[...]
    o_ref[...] = out


def moe_combine(pos_tiles, xa, gate, route, y_sorted, final_gain, n_rows, final, tm, seq, nb):
    d = xa.shape[1]
    n_tiles = n_rows // tm
    return pl.pallas_call(
        functools.partial(_moe_combine_kernel, final=final),
        grid=(n_tiles,),
        in_specs=[
            pl.BlockSpec((1, 1, 2 * tm), lambda i: (i, 0, 0), memory_space=pltpu.SMEM),
            pl.BlockSpec((1, 1, 2 * tm), lambda i: (jnp.minimum(i + 1, n_tiles - 1), 0, 0),
                         memory_space=pltpu.SMEM),
            pl.BlockSpec((tm, d), lambda i: (i, 0)),
            pl.BlockSpec((None, 1, d), _mod_row_map(tm, seq, nb)),
            pl.BlockSpec((tm, LANES), lambda i: (i, 0)),
            pl.BlockSpec((1, d), lambda i: (0, 0)),
            pl.BlockSpec(memory_space=pl.ANY),
        ],
        out_specs=pl.BlockSpec((tm, d), lambda i: (i, 0)),
        out_shape=jax.ShapeDtypeStruct((n_rows, d), F32),
        scratch_shapes=[pltpu.VMEM((2, 2 * tm, d // 2), U32), pltpu.SemaphoreType.DMA((2,))],
        compiler_params=_cparams(("arbitrary",)),
        name="moe_combine",
    )(pos_tiles, pos_tiles, xa, gate, route, final_gain.reshape(1, d).astype(F32), y_sorted)


def moe_slots(route, counts, tm_ffn, tm_tok):
    ta = route.shape[0]
    ids = route[:, 0:TOP_K].astype(jnp.int32)
    rank = route[:, 2 * TOP_K:3 * TOP_K].astype(jnp.int32)
    counts = counts[0, N_GROUPS:N_GROUPS + N_EXPERTS].astype(jnp.int32)
    padded = ((counts + tm_ffn - 1) // tm_ffn) * tm_ffn
    ends = jnp.cumsum(padded)
    starts = ends - padded
    experts = jnp.arange(N_EXPERTS, dtype=jnp.int32)
    pos = jnp.sum(jnp.where(ids[..., None] == experts, starts, 0), axis=-1) + rank
    n_tiles = (TOP_K * ta + N_EXPERTS * (tm_ffn - 1)) // tm_ffn
    tile_start = jnp.arange(n_tiles, dtype=jnp.int32) * tm_ffn
    tile_expert = jnp.sum((tile_start[:, None] >= ends[None, :]).astype(jnp.int32), axis=1)
    tile_expert = jnp.minimum(tile_expert, N_EXPERTS - 1)
    n_used = (ends[-1] // tm_ffn).astype(jnp.int32).reshape(1)
    pos_tiles = pos.reshape(ta // tm_tok, tm_tok, TOP_K).transpose(0, 2, 1).reshape(ta // tm_tok, 1, TOP_K * tm_tok)
    return tile_expert, n_used, n_tiles * tm_ffn, pos_tiles


def _seq_flags(t_lat, seq, tc, cseq, tm):
    starts = np.arange(0, t_lat + tc, tm)
    first = np.where(starts < t_lat, starts % seq == 0, (starts - t_lat) % cseq == 0)
    ends = starts + tm
    last = np.where(starts < t_lat, ends % seq == 0, (ends - t_lat) % cseq == 0)
    return jnp.asarray(first.astype(np.int32)), jnp.asarray(last.astype(np.int32))


def kernel(x, c, ctx, c_ctx, w_ada, b_ada, norm_mix, norm_ffn, ev_w_in, ev_q_gain, ev_k_gain, ev_decay_f,
           ev_decay_b, ev_w_out, od_w_in, od_conv, od_a_log_f, od_a_log_b, od_dt_bias_f, od_dt_bias_b,
           od_out_gain, od_w_out, moe_w_group, moe_b_group, moe_w_expert, moe_b_expert, moe_w_gate_up,
           moe_w_down, final_norm_gain):
    nb, seq, d = x.shape
    cseq = ctx.shape[1]
    depth = w_ada.shape[0]
    t_lat = nb * seq
    tc = nb * cseq
    assert nb + 1 <= 8 and seq % cseq == 0 and cseq % RET_CHUNK == 0 and seq % GRID_W == 0

    tm = 512 if tc % 512 == 0 else cseq
    tq = min(256, cseq)
    tk = min(256, cseq)
    tl = 2 * DN_CHUNK
    tm_ffn = 512
    tm_comb = tm

    xa = jnp.concatenate([x.reshape(t_lat, d), ctx.reshape(tc, d)], axis=0)
    c8 = jnp.zeros((8, d), F32).at[:nb].set(c).at[nb].set(c_ctx)
    mod = adaln(c8, w_ada, b_ada)

    tabs = rope_tables(seq, tm)
    first_flags, last_flags = _seq_flags(t_lat, seq, tc, cseq, tm)
    ret_zero = jnp.zeros((nb, RET_HEADS, RET_DK, RET_DV), F32)
    dn_zero = jnp.zeros((nb, DN_HEADS, DN_DK, DN_DV), F32)

    xs = None
    for layer in range(depth):
        m = mod[layer].reshape(8, 6, 1, d)
        sh1, sc1, g1, sh2, sc2, g2 = (m[:, j] for j in range(6))
        w_router = jnp.pad(jnp.concatenate([moe_w_group[layer], moe_w_expert[layer]], axis=1),
                           ((0, 0), (0, LANES - N_GROUPS - N_EXPERTS))).astype(BF16)
        b_router = jnp.pad(jnp.concatenate([moe_b_group[layer], moe_b_expert[layer]]),
                           (0, LANES - N_GROUPS - N_EXPERTS)).reshape(1, LANES).astype(F32)
        route_params = (norm_ffn[layer], sh2, sc2, w_router, b_router)
        i = layer // 2
        if layer % 2 == 0:
            w_in = ev_w_in[i].astype(BF16)
            rq, rk, p, aq, ak, av = inproj_even(xa, norm_mix[layer], sh1, sc1, w_in, tabs, ev_q_gain[i],
                                                ev_k_gain[i], tm, seq, nb)
            dec = jnp.stack([ev_decay_f[i], ev_decay_b[i]]).astype(F32)
            oc, scf, scb = retention(dec, rq, rk, p, ret_zero, ret_zero, nb, cseq, t_lat // cseq)
            ol, _, _ = retention(dec, rq, rk, p, scf, scb, nb, seq, 0)
            kcat = jnp.concatenate([ak[:, :t_lat].reshape(ATT_KV_HEADS, nb, seq, ATT_HD),
                                    ak[:, t_lat:].reshape(ATT_KV_HEADS, nb, cseq, ATT_HD)], axis=2)
            vcat = jnp.concatenate([av[:, :t_lat].reshape(ATT_KV_HEADS, nb, seq, ATT_HD),
                                    av[:, t_lat:].reshape(ATT_KV_HEADS, nb, cseq, ATT_HD)], axis=2)
            lk = seq + cseq
            vtcat = jnp.concatenate([vcat.transpose(0, 1, 3, 2),
                                     jnp.ones((ATT_KV_HEADS, nb, ATT_VT_ROWS - ATT_HD, lk), BF16)], axis=2)
            vtcat = vtcat.reshape(ATT_KV_HEADS, nb, ATT_VT_ROWS, lk // tk, tk).transpose(0, 1, 3, 2, 4)
            aqt = aq.transpose(0, 2, 1)
            att_l = attention(aqt, kcat, vtcat, seq, 0, 0, tq, tk)
            att_c = attention(aqt, kcat, vtcat, cseq, t_lat // tq, seq // tk, tq, tk)
            w_out = ev_w_out[i].astype(BF16)
            k1 = RET_HEADS * RET_DV
            xa, f, route, counts = outproj_even(ol, oc, att_l, att_c, w_out[:k1], w_out[k1:], xa, g1, route_params,
                                                tm, seq, nb)
        else:
            w_in = jnp.pad(od_w_in[i], ((0, 0), (0, ODD_IN_PAD - ODD_IN))).astype(BF16)
            zpad = jnp.zeros((LANES - 2 * DN_HEADS,), F32)
            arow = jnp.concatenate([od_a_log_f[i], od_a_log_b[i], zpad]).reshape(1, LANES).astype(F32)
            brow = jnp.concatenate([od_dt_bias_f[i], od_dt_bias_b[i], zpad]).reshape(1, LANES).astype(F32)
            q, k, v, p, gb = inproj_odd(xa, norm_mix[layer], sh1, sc1, w_in, od_conv[i].astype(F32), arow, brow,
                                        first_flags, last_flags, tm, seq, nb, cseq)
            ta = t_lat + tc
            gbt = gb.reshape(ta // DN_CHUNK, DN_CHUNK, LANES)[:, :, :4 * DN_HEADS].transpose(0, 2, 1)
            oc_f, oc_b, sc_f, sc_b = deltanet_bidir(q, k, v, gb, gbt, dn_zero, dn_zero, nb, cseq, t_lat, tl)
            ol_f, ol_b, _, _ = deltanet_bidir(q, k, v, gb, gbt, sc_f, sc_b, nb, seq, 0, tl)
            xa, f, route, counts = outproj_odd(ol_f, oc_f, ol_b, oc_b, p, od_out_gain[i], od_w_out[i].astype(BF16),
                                               xa, g1, route_params, tm, seq, nb)

        tile_expert, n_used, n_pad, pos_tiles = moe_slots(route, counts, tm_ffn, tm_comb)
        xs = moe_scatter(pos_tiles, f, jnp.zeros((n_pad, d // 2), U32) if xs is None else xs, tm_comb)
        y_sorted = moe_ffn(tile_expert, n_used, xs, moe_w_gate_up, moe_w_down, layer, tm_ffn)
        last = layer == depth - 1
        xa = moe_combine(pos_tiles, xa, g2, route, y_sorted, final_norm_gain, t_lat if last else t_lat + tc, last,
                         tm_comb, seq, nb)

    return xa.reshape(nb, seq, d)
```
